```python
import math
import jax
import jax.numpy as jnp
from jax import lax
import numpy as np

D_MODEL = 1024
BATCH = 16
SEQ = 256
DEPTH = 2
DEC_BATCH = 2
DEC_SEQ = 1024
PAST_LEN = 256

F32 = jnp.float32
GRID_W = 64
BLOCK = 128
ROPE_BASE = 10000.0
NEG = -1e30
LN_EPS = 1e-5
RMS_EPS = 1e-6

HY_W = 256
HY_BANDS = 16
HY_EMB = 1 + 2 * HY_BANDS
HY_FFN = 64
HY_SIN_FREQ = 1.0
HY_FAST_DECAY = 0.3
HY_SLOW_DECAY = 1.5
HY_TARGET = 1e-2

WIN_HEADS = 4
WIN_KV_HEADS = 2
WIN_HD = 64
WINDOW = 128

RET_HEADS = 4
RET_DK = 64
RET_DV = 128

MLA_HEADS = 4
MLA_Q_LORA = 256
MLA_KV_LORA = 128
MLA_NOPE = 64
MLA_ROPE = 32
MLA_V = 64

N_EXPERTS = 16
N_GROUPS = 4
TOP_K = 2
D_EXPERT = 256
ROUTE_SCALE = 2.5

N_BRANCH = 4
ALPHA = (2.0 * DEPTH) ** 0.25
BETA = (8.0 * DEPTH) ** -0.25

COL_SIZES = (3 * HY_W,
             WIN_HEADS * WIN_HD, WIN_KV_HEADS * WIN_HD, WIN_KV_HEADS * WIN_HD,
             RET_HEADS * RET_DK, RET_HEADS * RET_DK, RET_HEADS * RET_DV, RET_HEADS * RET_DV,
             MLA_Q_LORA, MLA_KV_LORA, MLA_ROPE,
             N_BRANCH * D_MODEL)
IN_COLS = 3 * HY_W + WIN_HEADS * WIN_HD + 2 * WIN_KV_HEADS * WIN_HD + 2 * RET_HEADS * RET_DK + 2 * RET_HEADS * RET_DV + MLA_Q_LORA + MLA_KV_LORA + MLA_ROPE + N_BRANCH * D_MODEL

kernel_name = 'hybrid_diffusion_prefix_trunk_step'


def layer_norm(x, g=None, b=None):
    xf = x.astype(F32)
    mu = xf.mean(-1, keepdims=True)
    var = jnp.square(xf - mu).mean(-1, keepdims=True)
    y = (xf - mu) * lax.rsqrt(var + LN_EPS)
    if g is not None:
        y = y * g.astype(F32) + b.astype(F32)
    return y.astype(x.dtype)


def rms_norm(x, g):
    xf = x.astype(F32)
    y = xf * lax.rsqrt(jnp.mean(xf * xf, -1, keepdims=True) + RMS_EPS)
    return y.astype(x.dtype) * g


def split_columns(z):
    offs = np.cumsum(COL_SIZES)[:-1].tolist()
    return jnp.split(z, offs, axis=-1)


def axial_rope(L, rot_dim):
    rows = L // GRID_W
    n_freq = rot_dim // 4
    inv = ROPE_BASE ** (-jnp.arange(n_freq, dtype=F32) / n_freq)
    row = jnp.broadcast_to(jnp.arange(rows, dtype=F32)[:, None], (rows, GRID_W)).reshape(L)
    col = jnp.broadcast_to(jnp.arange(GRID_W, dtype=F32)[None, :], (rows, GRID_W)).reshape(L)
    ang = jnp.concatenate([row[:, None] * inv, col[:, None] * inv], -1)
    return jnp.cos(ang), jnp.sin(ang)


def apply_rope(x, cos, sin):
    half = x.shape[-1] // 2
    x1, x2 = x[..., :half], x[..., half:]
    c = cos[None, :, None, :].astype(x.dtype)
    s = sin[None, :, None, :].astype(x.dtype)
    return jnp.concatenate([x1 * c - x2 * s, x1 * s + x2 * c], -1)


def short_conv3(u, w, b):
    L = u.shape[1]
    up = jnp.pad(u, ((0, 0), (1, 1), (0, 0)))
    return up[:, :L] * w[0] + up[:, 1:L + 1] * w[1] + up[:, 2:] * w[2] + b


def hyena_filters(L, w1, b1, w2, b2, w3):
    t01 = jnp.linspace(0.0, 1.0, L, dtype=F32)[:, None]
    bands = jnp.linspace(1e-4, HY_BANDS - 1, HY_BANDS, dtype=F32)
    ang = (2.0 * math.pi / L) * jnp.arange(L, dtype=F32)[:, None] * bands[None, :]
    z = jnp.concatenate([t01, jnp.cos(ang), -jnp.sin(ang)], -1)
    a = jnp.sin(HY_SIN_FREQ * (z @ w1.astype(F32) + b1.astype(F32)))
    a = jnp.sin(HY_SIN_FREQ * (a @ w2.astype(F32) + b2.astype(F32)))
    h = (a @ w3.astype(F32)).reshape(L, 2, 2, HY_W)
    deltas = jnp.abs(jnp.linspace(math.log(HY_TARGET) / HY_SLOW_DECAY, math.log(HY_TARGET) / HY_FAST_DECAY, HY_W, dtype=F32))
    h = h * jnp.exp(-t01 * deltas[None, :])[:, None, None, :]
    fwd, bwd = h[:, :, 0], h[:, :, 1]
    k = jnp.concatenate([fwd, jnp.zeros_like(fwd[:1]), bwd[:0:-1]], 0)
    return jnp.fft.rfft(k, axis=0)


def fft_long_conv(u, kf, bias):
    L = u.shape[1]
    uf = jnp.fft.rfft(u.astype(F32), n=2 * L, axis=1)
    y = jnp.fft.irfft(uf * kf[None], n=2 * L, axis=1)[:, :L]
    return (y + u.astype(F32) * bias.astype(F32)).astype(u.dtype)


def hyena_mixer(z, p):
    L = z.shape[1]
    z = short_conv3(z, p['hy_conv_w'], p['hy_conv_b'])
    v, x1, x2 = jnp.split(z, 3, axis=-1)
    kf = hyena_filters(L, p['hy_w1'], p['hy_b1'], p['hy_w2'], p['hy_b2'], p['hy_w3'])
    u = x1 * fft_long_conv(v, kf[:, 0], p['hy_bias'][0])
    return x2 * fft_long_conv(u, kf[:, 1], p['hy_bias'][1])


def dense_attention(q, k, v, sink=None):
    B, Lq, KVH, G, dq = q.shape
    nb = Lq // BLOCK
    scale = dq ** -0.5
    qb = jnp.moveaxis(q.reshape(B, nb, BLOCK, KVH, G, dq), 1, 0)

    def one_block(qi):
        s = jnp.einsum('bqkgd,bskd->bkgqs', qi, k, preferred_element_type=F32) * scale
        if sink is not None:
            sk = jnp.broadcast_to(sink.reshape(KVH, G)[None, :, :, None, None].astype(F32), s.shape[:-1] + (1,))
            pr = jax.nn.softmax(jnp.concatenate([s, sk], -1), -1)[..., :-1]
        else:
            pr = jax.nn.softmax(s, -1)
        return jnp.einsum('bkgqs,bskd->bqkgd', pr.astype(v.dtype), v)

    o = lax.map(one_block, qb)
    return jnp.moveaxis(o, 0, 1).reshape(B, Lq, KVH * G * v.shape[-1])


def window_attention(q, k, v, ck, cv, sink):
    B, L, H, d = q.shape
    KVH = k.shape[2]
    G = H // KVH
    nb = L // BLOCK
    P = ck.shape[1]
    W = 3 * BLOCK
    qb = q.reshape(B, nb, BLOCK, KVH, G, d)
    kpos = jnp.arange(nb)[:, None] * BLOCK + jnp.arange(-BLOCK, 2 * BLOCK)[None, :]
    pad = ((0, 0), (BLOCK, BLOCK), (0, 0), (0, 0))
    kb = jnp.pad(k, pad)[:, kpos + BLOCK]
    vb = jnp.pad(v, pad)[:, kpos + BLOCK]
    qpos = jnp.arange(L).reshape(nb, BLOCK)
    valid = (jnp.abs(qpos[:, :, None] - kpos[:, None, :]) <= WINDOW) & (kpos[:, None, :] >= 0) & (kpos[:, None, :] < L)
    scale = d ** -0.5
    s_loc = jnp.einsum('bnqkgd,bnskd->bnkgqs', qb, kb, preferred_element_type=F32) * scale
    s_loc = jnp.where(valid[None, :, None, None], s_loc, NEG)
    s_ctx = jnp.einsum('bnqkgd,bpkd->bnkgqp', qb, ck, preferred_element_type=F32) * scale
    sk = jnp.broadcast_to(sink.reshape(KVH, G)[None, None, :, :, None, None].astype(F32), s_loc.shape[:-1] + (1,))
    pr = jax.nn.softmax(jnp.concatenate([s_loc, s_ctx, sk], -1), -1).astype(v.dtype)
    o = (jnp.einsum('bnkgqs,bnskd->bnqkgd', pr[..., :W], vb)
         + jnp.einsum('bnkgqp,bpkd->bnqkgd', pr[..., W:W + P], cv))
    return o.reshape(B, L, H * d)


def retention_chunked(q, k, v, gamma, s0):
    B, L, H, _ = q.shape
    dv = v.shape[-1]
    nc = L // BLOCK
    log_g = jnp.log(gamma.astype(F32))
    pos = jnp.arange(BLOCK, dtype=F32)
    diff = pos[:, None] - pos[None, :]
    decay_in = jnp.where(diff >= 0, jnp.exp(jnp.maximum(diff, 0.0)[None] * log_g[:, None, None]), 0.0)
    decay_q = jnp.exp((pos + 1.0)[:, None] * log_g[None, :])
    decay_k = jnp.exp((BLOCK - 1.0 - pos)[:, None] * log_g[None, :])
    decay_c = jnp.exp(BLOCK * log_g)

    def chunks(a):
        return jnp.moveaxis(a.astype(F32).reshape(B, nc, BLOCK, H, a.shape[-1]), 1, 0)

    def step(S, inp):
        qi, ki, vi = inp
        att = jnp.einsum('bihd,bjhd->bhij', qi, ki) * decay_in[None]
        o = (jnp.einsum('bhij,bjhe->bihe', att, vi)
             + jnp.einsum('bihd,bhde->bihe', qi, S) * decay_q[None, :, :, None])
        S = S * decay_c[None, :, None, None] + jnp.einsum('bjhd,bjhe->bhde', ki * decay_k[None, :, :, None], vi)
        return S, o

    S, o = lax.scan(step, s0.astype(F32), (chunks(q), chunks(k), chunks(v)))
    return jnp.moveaxis(o, 0, 1).reshape(B, L, H, dv), S


def retention_mixer(q, k, v, g, dec_f, dec_b, s0f, s0b):
    B, L, _ = q.shape
    qh = q.reshape(B, L, RET_HEADS, RET_DK)
    kh = k.reshape(B, L, RET_HEADS, RET_DK) * (RET_DK ** -0.5)
    vh = v.reshape(B, L, RET_HEADS, RET_DV)
    of, sf = retention_chunked(qh, kh, vh, jax.nn.sigmoid(dec_f.astype(F32)), s0f)
    ob, sb = retention_chunked(qh[:, ::-1], kh[:, ::-1], vh[:, ::-1], jax.nn.sigmoid(dec_b.astype(F32)), s0b)
    o = layer_norm(of + ob[:, ::-1]).reshape(B, L, RET_HEADS * RET_DV)
    return jax.nn.silu(g) * o.astype(g.dtype), sf, sb


def mla_queries(cq, p):
    B, L, _ = cq.shape
    cq = rms_norm(cq, p['mla_q_norm'])
    return (cq @ p['mla_w_uq']).reshape(B, L, MLA_HEADS, MLA_NOPE + MLA_ROPE)


def mla_keys_values(ckv_n, kr, p):
    B, L, _ = ckv_n.shape
    kv = (ckv_n @ p['mla_w_ukv']).reshape(B, L, MLA_HEADS, MLA_NOPE + MLA_V)
    k = jnp.concatenate([kv[..., :MLA_NOPE], jnp.broadcast_to(kr, (B, L, MLA_HEADS, MLA_ROPE)).astype(kv.dtype)], -1)
    return k, kv[..., MLA_NOPE:]


def moe(h, p):
    B, L, D = h.shape
    t = h.reshape(B * L, D)
    epg = N_EXPERTS // N_GROUPS
    scores = jax.nn.sigmoid(jnp.dot(t, p['router_w'], preferred_element_type=F32))
    biased = (scores + p['router_b'].astype(F32)).reshape(-1, N_GROUPS, epg)
    grp = jnp.argmax(lax.top_k(biased, TOP_K)[0].sum(-1), -1)
    in_grp = jnp.take_along_axis(biased, grp[:, None, None], axis=1)[:, 0]
    _, local = lax.top_k(in_grp, TOP_K)
    expert = grp[:, None] * epg + local
    w = jnp.take_along_axis(scores, expert, -1)
    w = ROUTE_SCALE * w / w.sum(-1, keepdims=True)
    combine = (jax.nn.one_hot(expert, N_EXPERTS, dtype=F32) * w[..., None]).sum(1)
    hid = jax.nn.silu(jnp.einsum('td,edf->tef', t, p['moe_w_gate'])) * jnp.einsum('td,edf->tef', t, p['moe_w_up'])
    y = jnp.einsum('tef,efd->td', hid * combine[..., None].astype(hid.dtype), p['moe_w_down'])
    return y.reshape(B, L, D)


def modulated_input(x, mod):
    s1, sc1, g1, s2, sc2, g2 = jnp.split(mod, 6, axis=-1)
    h = layer_norm(x) * (1.0 + sc1) + s1
    return h, (g1, s2, sc2, g2)


def merge_and_channel_mix(x, y_a, y_b, y_c, y_d, gate_cols, mods, p):
    g1, s2, sc2, g2 = mods
    B, L, D = x.shape
    gates = jax.nn.sigmoid(gate_cols.reshape(B, L, N_BRANCH, D))
    merged = (gates[:, :, 0] * (y_a @ p['w_br_a']) + gates[:, :, 1] * (y_b @ p['w_br_b'])
              + gates[:, :, 2] * (y_c @ p['w_br_c']) + gates[:, :, 3] * (y_d @ p['w_br_d']))
    x = layer_norm(ALPHA * x + g1 * (merged @ p['w_out']), p['ln1_g'], p['ln1_b'])
    h2 = layer_norm(x) * (1.0 + sc2) + s2
    return layer_norm(ALPHA * x + g2 * moe(h2, p), p['ln2_g'], p['ln2_b'])


def context_layer(x, mod, p):
    B, P, _ = x.shape
    h, mods = modulated_input(x, mod)
    hy, wq, wk, wv, rq, rk, rv, rg, cq, ckv, krope, gate_cols = split_columns(h @ p['w_in'])
    y_a = hyena_mixer(hy, p)
    kB = wk.reshape(B, P, WIN_KV_HEADS, WIN_HD)
    vB = wv.reshape(B, P, WIN_KV_HEADS, WIN_HD)
    y_b = dense_attention(wq.reshape(B, P, WIN_KV_HEADS, WIN_HEADS // WIN_KV_HEADS, WIN_HD), kB, vB, p['win_sink'])
    zero = jnp.zeros((B, RET_HEADS, RET_DK, RET_DV), F32)
    y_c, sf, sb = retention_mixer(rq, rk, rv, rg, p['ret_decay_fwd'], p['ret_decay_bwd'], zero, zero)
    ckv_n = rms_norm(ckv, p['mla_kv_norm'])
    k_d, v_d = mla_keys_values(ckv_n, krope[:, :, None, :], p)
    y_d = dense_attention(mla_queries(cq, p)[:, :, :, None, :], k_d, v_d)
    x = merge_and_channel_mix(x, y_a, y_b, y_c, y_d, gate_cols, mods, p)
    return x, (kB, vB, ckv_n, krope, sf, sb)


def latent_layer(x, mod, p, ck, cv, cckv, ckrope, s0f, s0b):
    B, L, _ = x.shape
    h, mods = modulated_input(x, mod)
    hy, wq, wk, wv, rq, rk, rv, rg, cq, ckv, krope, gate_cols = split_columns(h @ p['w_in'])
    y_a = hyena_mixer(hy, p)
    cos_b, sin_b = axial_rope(L, WIN_HD)
    qB = apply_rope(wq.reshape(B, L, WIN_HEADS, WIN_HD), cos_b, sin_b)
    kB = apply_rope(wk.reshape(B, L, WIN_KV_HEADS, WIN_HD), cos_b, sin_b)
    vB = wv.reshape(B, L, WIN_KV_HEADS, WIN_HD)
    y_b = window_attention(qB, kB, vB, ck, cv, p['win_sink'])
    y_c, _, _ = retention_mixer(rq, rk, rv, rg, p['ret_decay_fwd'], p['ret_decay_bwd'], s0f, s0b)
    cos_d, sin_d = axial_rope(L, MLA_ROPE)
    q = mla_queries(cq, p)
    q = jnp.concatenate([q[..., :MLA_NOPE], apply_rope(q[..., MLA_NOPE:], cos_d, sin_d)], -1)
    k_lat, v_lat = mla_keys_values(rms_norm(ckv, p['mla_kv_norm']), apply_rope(krope[:, :, None, :], cos_d, sin_d), p)
    k_ctx, v_ctx = mla_keys_values(cckv, ckrope[:, :, None, :], p)
    k_all = jnp.concatenate([k_lat, k_ctx.astype(k_lat.dtype)], 1)
    v_all = jnp.concatenate([v_lat, v_ctx.astype(v_lat.dtype)], 1)
    y_d = dense_attention(q[:, :, :, None, :], k_all, v_all)
    return merge_and_channel_mix(x, y_a, y_b, y_c, y_d, gate_cols, mods, p)


def setup_inputs(seed: int = 0) -> dict:
    key = jax.random.key(seed)
    ks = iter(jax.random.split(key, 48))
    D = D_MODEL

    def nrm(shape, scale=1.0):
        return jax.random.normal(next(ks), shape, F32) * scale

    ret_base = jnp.log(2.0 ** (5.0 + jnp.arange(RET_HEADS, dtype=F32)) - 1.0)
    return {
        'x_prompt': nrm((BATCH, SEQ, D)),
        'x_sample': nrm((DEC_BATCH, DEC_SEQ, D)),
        'cache_win_k': nrm((DEC_BATCH, DEPTH, PAST_LEN, WIN_KV_HEADS, WIN_HD)),
        'cache_win_v': nrm((DEC_BATCH, DEPTH, PAST_LEN, WIN_KV_HEADS, WIN_HD)),
        'cache_mla_ckv': nrm((DEC_BATCH, DEPTH, PAST_LEN, MLA_KV_LORA)),
        'cache_mla_krope': nrm((DEC_BATCH, DEPTH, PAST_LEN, MLA_ROPE)),
        'state_ret_fwd': nrm((DEC_BATCH, DEPTH, RET_HEADS, RET_DK, RET_DV), 0.5),
        'state_ret_bwd': nrm((DEC_BATCH, DEPTH, RET_HEADS, RET_DK, RET_DV), 0.5),
        'c': nrm((DEC_BATCH, D)),
        'c_ctx': nrm((D,)),
        'w_ada': nrm((DEPTH, D, 6 * D), 0.5 * D ** -0.5),
        'b_ada': nrm((DEPTH, 6 * D), 0.02),
        'w_in': nrm((DEPTH, D, IN_COLS), D ** -0.5),
        'hy_conv_w': nrm((DEPTH, 3, 3 * HY_W), 3 ** -0.5),
        'hy_conv_b': nrm((DEPTH, 3 * HY_W), 0.02),
        'hy_w1': nrm((DEPTH, HY_EMB, HY_FFN), HY_EMB ** -0.5),
        'hy_b1': nrm((DEPTH, HY_FFN), 0.02),
        'hy_w2': nrm((DEPTH, HY_FFN, HY_FFN), HY_FFN ** -0.5),
        'hy_b2': nrm((DEPTH, HY_FFN), 0.02),
        'hy_w3': nrm((DEPTH, HY_FFN, 4 * HY_W), 0.05 * HY_FFN ** -0.5),
        'hy_bias': nrm((DEPTH, 2, HY_W), 0.1),
        'win_sink': nrm((DEPTH, WIN_HEADS), 0.5),
        'ret_decay_fwd': ret_base + nrm((DEPTH, RET_HEADS), 0.1),
        'ret_decay_bwd': ret_base + nrm((DEPTH, RET_HEADS), 0.1),
        'mla_q_norm': 1.0 + nrm((DEPTH, MLA_Q_LORA), 0.02),
        'mla_kv_norm': 1.0 + nrm((DEPTH, MLA_KV_LORA), 0.02),
        'mla_w_uq': nrm((DEPTH, MLA_Q_LORA, MLA_HEADS * (MLA_NOPE + MLA_ROPE)), MLA_Q_LORA ** -0.5),
        'mla_w_ukv': nrm((DEPTH, MLA_KV_LORA, MLA_HEADS * (MLA_NOPE + MLA_V)), MLA_KV_LORA ** -0.5),
        'w_br_a': nrm((DEPTH, HY_W, D), BETA * HY_W ** -0.5),
        'w_br_b': nrm((DEPTH, WIN_HEADS * WIN_HD, D), BETA * (WIN_HEADS * WIN_HD) ** -0.5),
        'w_br_c': nrm((DEPTH, RET_HEADS * RET_DV, D), BETA * (RET_HEADS * RET_DV) ** -0.5),
        'w_br_d': nrm((DEPTH, MLA_HEADS * MLA_V, D), BETA * (MLA_HEADS * MLA_V) ** -0.5),
        'w_out': nrm((DEPTH, D, D), BETA * D ** -0.5),
        'ln1_g': 1.0 + nrm((DEPTH, D), 0.02),
        'ln1_b': nrm((DEPTH, D), 0.02),
        'ln2_g': 1.0 + nrm((DEPTH, D), 0.02),
        'ln2_b': nrm((DEPTH, D), 0.02),
        'router_w': nrm((D, N_EXPERTS), D ** -0.5),
        'router_b': nrm((N_EXPERTS,), 0.01),
        'moe_w_gate': nrm((DEPTH, N_EXPERTS, D, D_EXPERT), D ** -0.5),
        'moe_w_up': nrm((DEPTH, N_EXPERTS, D, D_EXPERT), D ** -0.5),
        'moe_w_down': nrm((DEPTH, N_EXPERTS, D_EXPERT, D), BETA * D_EXPERT ** -0.5),
    }


def reference(x_prompt, x_sample, cache_win_k, cache_win_v, cache_mla_ckv, cache_mla_krope,
              state_ret_fwd, state_ret_bwd, c, c_ctx, w_ada, b_ada, w_in,
              hy_conv_w, hy_conv_b, hy_w1, hy_b1, hy_w2, hy_b2, hy_w3, hy_bias,
              win_sink, ret_decay_fwd, ret_decay_bwd, mla_q_norm, mla_kv_norm, mla_w_uq, mla_w_ukv,
              w_br_a, w_br_b, w_br_c, w_br_d, w_out, ln1_g, ln1_b, ln2_g, ln2_b,
              router_w, router_b, moe_w_gate, moe_w_up, moe_w_down):
    xp = x_prompt
    xs = x_sample
    ctx_lists = ([], [], [], [], [], [])
    for l in range(DEPTH):
        p = {
            'w_in': w_in[l], 'hy_conv_w': hy_conv_w[l], 'hy_conv_b': hy_conv_b[l],
            'hy_w1': hy_w1[l], 'hy_b1': hy_b1[l], 'hy_w2': hy_w2[l], 'hy_b2': hy_b2[l],
            'hy_w3': hy_w3[l], 'hy_bias': hy_bias[l], 'win_sink': win_sink[l],
            'ret_decay_fwd': ret_decay_fwd[l], 'ret_decay_bwd': ret_decay_bwd[l],
            'mla_q_norm': mla_q_norm[l], 'mla_kv_norm': mla_kv_norm[l],
            'mla_w_uq': mla_w_uq[l], 'mla_w_ukv': mla_w_ukv[l],
            'w_br_a': w_br_a[l], 'w_br_b': w_br_b[l], 'w_br_c': w_br_c[l], 'w_br_d': w_br_d[l],
            'w_out': w_out[l], 'ln1_g': ln1_g[l], 'ln1_b': ln1_b[l], 'ln2_g': ln2_g[l], 'ln2_b': ln2_b[l],
            'router_w': router_w, 'router_b': router_b,
            'moe_w_gate': moe_w_gate[l], 'moe_w_up': moe_w_up[l], 'moe_w_down': moe_w_down[l],
        }
        mod_ctx = (jax.nn.silu(c_ctx) @ w_ada[l] + b_ada[l])[None, None, :]
        mod_lat = (jax.nn.silu(c) @ w_ada[l] + b_ada[l])[:, None, :]
        xp, ctx_tensors = context_layer(xp, mod_ctx, p)
        for lst, t in zip(ctx_lists, ctx_tensors):
            lst.append(t)
        xs = latent_layer(xs, mod_lat, p, cache_win_k[:, l], cache_win_v[:, l], cache_mla_ckv[:, l],
                          cache_mla_krope[:, l], state_ret_fwd[:, l], state_ret_bwd[:, l])
    new_win_k = jnp.stack(ctx_lists[0], axis=1)
    new_win_v = jnp.stack(ctx_lists[1], axis=1)
    new_mla_ckv = jnp.stack(ctx_lists[2], axis=1)
    new_mla_krope = jnp.stack(ctx_lists[3], axis=1)
    new_ret_fwd = jnp.stack(ctx_lists[4], axis=1)
    new_ret_bwd = jnp.stack(ctx_lists[5], axis=1)
    return (xp, xs, new_win_k, new_win_v, new_mla_ckv, new_mla_krope, new_ret_fwd, new_ret_bwd)
```

```python
import functools
import math

import numpy as np
import jax
import jax.numpy as jnp
from jax import lax
from jax.experimental import pallas as pl
from jax.experimental.pallas import tpu as pltpu

F32 = jnp.float32
BF16 = jnp.bfloat16
HIGHEST = lax.Precision.HIGHEST

D_MODEL = 1024
BATCH = 16
SEQ = 256
DEPTH = 2
DEC_BATCH = 2
DEC_SEQ = 1024
PAST_LEN = 256
GRID_W = 64
CHUNK = 128
ROPE_BASE = 10000.0
NEG = -1e30
LN_EPS = 1e-5
RMS_EPS = 1e-6

HY_W = 256
HY_BANDS = 16
HY_EMB = 1 + 2 * HY_BANDS
HY_FFN = 64
HY_FAST_DECAY = 0.3
HY_SLOW_DECAY = 1.5
HY_TARGET = 1e-2

WIN_HEADS = 4
WIN_KV_HEADS = 2
WIN_HD = 64
WINDOW = 128

RET_HEADS = 4
RET_DK = 64
RET_DV = 128

MLA_HEADS = 4
MLA_Q_LORA = 256
MLA_KV_LORA = 128
MLA_NOPE = 64
MLA_ROPE = 32
MLA_V = 64

N_EXPERTS = 16
N_GROUPS = 4
EXPERTS_PER_GROUP = N_EXPERTS // N_GROUPS
D_EXPERT = 256
ROUTE_SCALE = 2.5

ALPHA = (2.0 * DEPTH) ** 0.25

T_CTX = BATCH * SEQ
T_LAT = DEC_BATCH * DEC_SEQ
T_ALL = T_CTX + T_LAT

COL_HY = 0
COL_WQ = 768
COL_WK = 1024
COL_WV = 1152
COL_RQ = 1280
COL_RK = 1536
COL_RV = 1792
COL_RG = 2304
COL_CQ = 2816
COL_CKV = 3072
COL_KROPE = 3200
COL_GATE = 3232
Z_MAIN = 3328

LANE = 128
VMEM_LIMIT = 56 * 1024 * 1024


def _params(*sem):
    return pltpu.CompilerParams(dimension_semantics=sem, vmem_limit_bytes=VMEM_LIMIT)


def _dot(a, b):
    return jnp.dot(a.astype(BF16), b.astype(BF16), preferred_element_type=F32)


def _dot_nt(a, b):
    return lax.dot_general(a.astype(BF16), b.astype(BF16), (((1,), (1,)), ((), ())),
                           preferred_element_type=F32)


def _dot_tn(a, b):
    return lax.dot_general(a.astype(BF16), b.astype(BF16), (((0,), (0,)), ((), ())),
                           preferred_element_type=F32)


def _layer_norm(x):
    mu = jnp.mean(x, -1, keepdims=True)
    xc = x - mu
    var = jnp.mean(xc * xc, -1, keepdims=True)
    return xc * lax.rsqrt(var + LN_EPS)


def _mod_row(tile_rows):
    def row(i):
        start = i * tile_rows
        return jnp.where(start < T_CTX, 0, 1 + (start - T_CTX) // DEC_SEQ)
    return row


@functools.lru_cache(maxsize=None)
def _dft_tables(L):
    f = np.arange(L, dtype=np.int64)[:, None]
    s = np.arange(L, dtype=np.int64)[None, :]
    ang = np.pi * ((f * s) % (2 * L)).astype(np.float64) / L
    cos = np.cos(ang)
    sin = np.sin(ang)
    alt = np.where(np.arange(L) % 2 == 0, 1.0, -1.0)
    fwd_im = -sin
    fwd_im[0, :] = alt
    fwd = np.concatenate([cos, fwd_im], 0)
    inv_re = cos.T / L
    inv_re[:, 0] = 1.0 / (2 * L)
    inv_im = -sin.T / L
    inv_im[:, 0] = alt / (2 * L)
    inv = np.concatenate([inv_re, inv_im], 1)
    return fwd.astype(np.float32), inv.astype(np.float32)


@functools.lru_cache(maxsize=None)
def _hyena_embedding(L):
    t01 = np.linspace(0.0, 1.0, L, dtype=np.float64)[:, None]
    bands = np.linspace(1e-4, HY_BANDS - 1, HY_BANDS, dtype=np.float64)
    ang = (2.0 * math.pi / L) * np.arange(L, dtype=np.float64)[:, None] * bands[None, :]
    z = np.concatenate([t01, np.cos(ang), -np.sin(ang)], -1)
    zp = np.zeros((L, LANE), np.float64)
    zp[:, :HY_EMB] = z
    deltas = np.abs(np.linspace(math.log(HY_TARGET) / HY_SLOW_DECAY,
                                math.log(HY_TARGET) / HY_FAST_DECAY, HY_W, dtype=np.float64))
    return zp.astype(np.float32), deltas[None, :].astype(np.float32)


@functools.lru_cache(maxsize=None)
def _rope_tables(L, rot_dim, width):
    rows = L // GRID_W
    n_freq = rot_dim // 4
    half = rot_dim // 2
    inv = ROPE_BASE ** (-np.arange(n_freq, dtype=np.float64) / n_freq)
    pos = np.arange(L)
    row = (pos // GRID_W).astype(np.float64)
    col = (pos % GRID_W).astype(np.float64)
    ang = np.concatenate([row[:, None] * inv, col[:, None] * inv], -1)
    cos, sin = np.cos(ang), np.sin(ang)
    zero = np.zeros_like(sin)
    c = np.tile(np.concatenate([cos, cos], -1), (1, width // rot_dim))
    s_up = np.tile(np.concatenate([-sin, zero], -1), (1, width // rot_dim))
    s_dn = np.tile(np.concatenate([zero, sin], -1), (1, width // rot_dim))
    return c.astype(np.float32), s_up.astype(np.float32), s_dn.astype(np.float32), half


def _rope128(x, c, s_up, s_dn, half):
    up = pltpu.roll(x, LANE - half, axis=1)
    dn = pltpu.roll(x, half, axis=1)
    return x * c + up * s_up + dn * s_dn


def _ada_kernel(c_ref, w_ref, b_ref, o_ref):
    cv = c_ref[...]
    s = cv * jax.nn.sigmoid(cv)
    o_ref[...] = jnp.dot(s, w_ref[...], precision=HIGHEST, preferred_element_type=F32) + b_ref[...]


def _ada_mods(cvec, w_ada, b_ada):
    tn = 1536
    n = 6 * D_MODEL
    return pl.pallas_call(
        _ada_kernel,
        out_shape=jax.ShapeDtypeStruct((DEPTH, 8, n), F32),
        grid=(DEPTH, n // tn),
        in_specs=[
            pl.BlockSpec((8, D_MODEL), lambda l, j: (0, 0)),
            pl.BlockSpec((None, D_MODEL, tn), lambda l, j: (l, 0, j)),
            pl.BlockSpec((None, 1, tn), lambda l, j: (l, 0, j)),
        ],
        out_specs=pl.BlockSpec((None, 8, tn), lambda l, j: (l, 0, j)),
        compiler_params=_params("parallel", "parallel"),
        name="ada_mods",
    )(cvec, w_ada, b_ada.reshape(DEPTH, 1, n))


def _lnmod_kernel(x_ref, m_ref, h_ref):
    y = _layer_norm(x_ref[...])
    s1 = m_ref[:, 0:D_MODEL]
    sc1 = m_ref[:, D_MODEL:2 * D_MODEL]
    h_ref[...] = (y * (1.0 + sc1) + s1).astype(h_ref.dtype)


def _ln_mod(x, mods_l):
    tm = 512
    row = _mod_row(tm)
    return pl.pallas_call(
        _lnmod_kernel,
        out_shape=jax.ShapeDtypeStruct((T_ALL, D_MODEL), BF16),
        grid=(T_ALL // tm,),
        in_specs=[
            pl.BlockSpec((tm, D_MODEL), lambda i: (i, 0)),
            pl.BlockSpec((None, 1, 6 * D_MODEL), lambda i: (row(i), 0, 0)),
        ],
        out_specs=pl.BlockSpec((tm, D_MODEL), lambda i: (i, 0)),
        compiler_params=_params("parallel"),
        name="ln_mod",
    )(x, mods_l)


def _proj_kernel(h_ref, w_ref, o_ref, wb_ref, *, gate):
    @pl.when(pl.program_id(1) == 0)
    def _():
        wb_ref[...] = w_ref[...].astype(BF16)

    r = jnp.dot(h_ref[...], wb_ref[...], preferred_element_type=F32)
    if gate:
        r = jax.nn.sigmoid(r)
    o_ref[...] = r.astype(o_ref.dtype)


def _in_proj(h, w, layer, n_cols, tn, out_dtype, gate):
    tm = 512
    return pl.pallas_call(
        functools.partial(_proj_kernel, gate=gate),
        out_shape=jax.ShapeDtypeStruct((T_ALL, n_cols), out_dtype),
        grid=(n_cols // tn, T_ALL // tm),
        in_specs=[
            pl.BlockSpec((tm, D_MODEL), lambda j, i: (i, 0)),
            pl.BlockSpec((None, D_MODEL, tn), lambda j, i: (layer, 0, j)),
        ],
        out_specs=pl.BlockSpec((tm, tn), lambda j, i: (i, j)),
        scratch_shapes=[pltpu.VMEM((D_MODEL, tn), BF16)],
        compiler_params=_params("parallel", "arbitrary"),
        name="gate_proj" if gate else "in_proj",
    )(h, w)


def _hy_filter_kernel(z_ref, dl_ref, w1_ref, b1_ref, w2_ref, b2_ref, w3_ref, fwd_ref,
                      kre_ref, kim_ref, *, L):
    z = z_ref[...]
    a = jnp.sin(jnp.dot(z, w1_ref[...], precision=HIGHEST, preferred_element_type=F32) + b1_ref[...])
    a = jnp.sin(jnp.dot(a, w2_ref[...], precision=HIGHEST, preferred_element_type=F32) + b2_ref[...])
    h = jnp.dot(a, w3_ref[...], precision=HIGHEST, preferred_element_type=F32)
    decay = jnp.exp(-z[:, 0:1] * dl_ref[...])
    not_first = lax.broadcasted_iota(jnp.int32, (L, HY_W), 0) > 0
    sums, diffs = [], []
    for o in range(2):
        fw = h[:, (2 * o) * HY_W:(2 * o + 1) * HY_W] * decay
        bw = jnp.where(not_first, h[:, (2 * o + 1) * HY_W:(2 * o + 2) * HY_W] * decay, 0.0)
        sums.append(fw + bw)
        diffs.append(fw - bw)
    p = _dot(fwd_ref[...], jnp.concatenate(sums, 1))
    q = _dot(fwd_ref[L:2 * L, :], jnp.concatenate(diffs, 1))
    kre_ref[...] = p[0:L]
    first = lax.broadcasted_iota(jnp.int32, (L, 2 * HY_W), 0) == 0
    kim_ref[...] = jnp.where(first, p[L:L + 1], q)


def _hy_filters(L, w1p, b1, w2, b2, w3, fwd):
    zemb, deltas = _hyena_embedding(L)
    out = jax.ShapeDtypeStruct((L, 2 * HY_W), F32)
    return pl.pallas_call(
        functools.partial(_hy_filter_kernel, L=L),
        out_shape=(out, out),
        compiler_params=pltpu.CompilerParams(vmem_limit_bytes=VMEM_LIMIT),
        name=f"hy_filters_{L}",
    )(jnp.asarray(zemb), jnp.asarray(deltas), w1p, b1, w2, b2, w3, fwd)


def _hyena_kernel(*refs, L, aliased):
    if aliased:
        refs = refs[1:]
    hy_ref, cw_ref, cb_ref, bias_ref, kre_ref, kim_ref, fwd_ref, inv_ref, o_ref = refs
    x = hy_ref[...]
    rows = lax.broadcasted_iota(jnp.int32, x.shape, 0)
    prev = jnp.where(rows == 0, 0.0, pltpu.roll(x, 1, axis=0))
    nxt = jnp.where(rows == L - 1, 0.0, pltpu.roll(x, L - 1, axis=0))
    z = prev * cw_ref[0:1, :] + x * cw_ref[1:2, :] + nxt * cw_ref[2:3, :] + cb_ref[...]
    v, x1, x2 = z[:, 0:HY_W], z[:, HY_W:2 * HY_W], z[:, 2 * HY_W:3 * HY_W]
    first = lax.broadcasted_iota(jnp.int32, (L, HY_W), 0) == 0

    def long_conv(u, o):
        uf = _dot(fwd_ref[...], u)
        ure, uim = uf[0:L], uf[L:2 * L]
        kre = kre_ref[:, o * HY_W:(o + 1) * HY_W]
        kim = kim_ref[:, o * HY_W:(o + 1) * HY_W]
        yre = jnp.where(first, ure * kre, ure * kre - uim * kim)
        yim = jnp.where(first, uim * kim, ure * kim + uim * kre)
        y = _dot(inv_ref[...], jnp.concatenate([yre, yim], 0))
        return y + u * bias_ref[o:o + 1, :]

    u = x1 * long_conv(v, 0)
    o_ref[...] = (x2 * long_conv(u, 1)).astype(o_ref.dtype)


def _hyena(z_main, prev_out, layer, L, n_seq, row_block0, conv_w, conv_b, bias, kre, kim, fwd, inv):
    aliased = prev_out is not None
    in_specs = [
        pl.BlockSpec((L, 3 * HY_W), lambda b: (row_block0 + b, 0)),
        pl.BlockSpec((None, 3, 3 * HY_W), lambda b: (layer, 0, 0)),
        pl.BlockSpec((None, 1, 3 * HY_W), lambda b: (layer, 0, 0)),
        pl.BlockSpec((None, 2, HY_W), lambda b: (layer, 0, 0)),
        pl.BlockSpec((L, 2 * HY_W), lambda b: (0, 0)),
        pl.BlockSpec((L, 2 * HY_W), lambda b: (0, 0)),
        pl.BlockSpec((2 * L, L), lambda b: (0, 0)),
        pl.BlockSpec((L, 2 * L), lambda b: (0, 0)),
    ]
    args = [z_main, conv_w, conv_b.reshape(DEPTH, 1, 3 * HY_W), bias, kre, kim, fwd, inv]
    if aliased:
        in_specs = [pl.BlockSpec(memory_space=pl.ANY)] + in_specs
        args = [prev_out] + args
    return pl.pallas_call(
        functools.partial(_hyena_kernel, L=L, aliased=aliased),
        out_shape=jax.ShapeDtypeStruct((T_ALL, HY_W), BF16),
        grid=(n_seq,),
        in_specs=in_specs,
        out_specs=pl.BlockSpec((L, HY_W), lambda b: (row_block0 + b, 0)),
        input_output_aliases={0: 0} if aliased else {},
        compiler_params=_params("parallel"),
        name=f"hyena_{L}",
    )(*args)


def _win_masks():
    lane = lax.broadcasted_iota(jnp.int32, (1, LANE), 1)
    return lane < WIN_HD, lane >= WIN_HD


def _win_head_operands(q, k, v, h):
    lo_mask, hi_mask = _win_masks()
    col = h // 2
    lo = h % 2 == 0
    q128 = jnp.where(lo_mask if lo else hi_mask, q[:, col * LANE:(col + 1) * LANE], 0.0)
    swap = h in (1, 2)
    if swap:
        k = pltpu.roll(k, WIN_HD, axis=1)
        v = pltpu.roll(v, WIN_HD, axis=1)
    return q128, k, v, lo


def _win_ctx_kernel(sink_ref, q_ref, kv_ref, o_ref, *, layer):
    q = q_ref[...]
    k = kv_ref[:, 0:LANE]
    v = kv_ref[:, LANE:2 * LANE]
    lo_mask, hi_mask = _win_masks()
    scale = WIN_HD ** -0.5
    cols = []
    for col in range(2):
        acc = None
        for h in (2 * col, 2 * col + 1):
            q128, kk, vv, lo = _win_head_operands(q, k, v, h)
            s = _dot_nt(q128, kk) * scale
            sink = sink_ref[layer, h]
            m = jnp.maximum(jnp.max(s, -1, keepdims=True), sink)
            p = jnp.exp(s - m)
            den = jnp.sum(p, -1, keepdims=True) + jnp.exp(sink - m)
            o = _dot(p, vv) / den
            o = jnp.where(lo_mask if lo else hi_mask, o, 0.0)
            acc = o if acc is None else acc + o
        cols.append(acc)
    o_ref[...] = jnp.concatenate(cols, 1).astype(o_ref.dtype)


def _win_ctx(z_main, sink, layer):
    return pl.pallas_call(
        functools.partial(_win_ctx_kernel, layer=layer),
        out_shape=jax.ShapeDtypeStruct((T_ALL, WIN_HEADS * WIN_HD), BF16),
        grid=(BATCH,),
        in_specs=[
            pl.BlockSpec(memory_space=pltpu.SMEM),
            pl.BlockSpec((SEQ, 256), lambda b: (b, COL_WQ // 256)),
            pl.BlockSpec((SEQ, 256), lambda b: (b, COL_WK // 256)),
        ],
        out_specs=pl.BlockSpec((SEQ, 256), lambda b: (b, 0)),
        compiler_params=_params("parallel"),
        name="win_ctx",
    )(sink, z_main, z_main)


def _win_lat_kernel(sink_ref, prev_ref, q_ref, kv_ref, ck_ref, cv_ref, c_ref, su_ref, sd_ref,
                    o_ref, *, layer):
    del prev_ref
    L = DEC_SEQ
    half = WIN_HD // 2
    c, su, sd = c_ref[...], su_ref[...], sd_ref[...]
    q = jnp.concatenate(
        [_rope128(q_ref[:, i * LANE:(i + 1) * LANE], c, su, sd, half) for i in range(2)], 1)
    k = _rope128(kv_ref[:, 0:LANE], c, su, sd, half)
    v = kv_ref[:, LANE:2 * LANE]
    ck = ck_ref[...]
    cv = cv_ref[...]
    lo_mask, hi_mask = _win_masks()
    scale = WIN_HD ** -0.5
    nb = L // CHUNK
    cols = []
    for col in range(2):
        acc_blocks = [None] * nb
        for h in (2 * col, 2 * col + 1):
            q128, kk, vv, lo = _win_head_operands(q, k, v, h)
            _, ckk, cvv, _ = _win_head_operands(q, ck, cv, h)
            sink = sink_ref[layer, h]
            for n in range(nb):
                k0 = max(0, (n - 1) * CHUNK)
                k1 = min(L, (n + 2) * CHUNK)
                qn = q128[n * CHUNK:(n + 1) * CHUNK]
                s_loc = _dot_nt(qn, kk[k0:k1]) * scale
                qi = n * CHUNK + lax.broadcasted_iota(jnp.int32, s_loc.shape, 0)
                kj = k0 + lax.broadcasted_iota(jnp.int32, s_loc.shape, 1)
                s_loc = jnp.where(jnp.abs(qi - kj) <= WINDOW, s_loc, NEG)
                s_ctx = _dot_nt(qn, ckk) * scale
                m = jnp.maximum(jnp.maximum(jnp.max(s_loc, -1, keepdims=True),
                                            jnp.max(s_ctx, -1, keepdims=True)), sink)
                p_loc = jnp.exp(s_loc - m)
                p_ctx = jnp.exp(s_ctx - m)
                den = (jnp.sum(p_loc, -1, keepdims=True) + jnp.sum(p_ctx, -1, keepdims=True)
                       + jnp.exp(sink - m))
                o = (_dot(p_loc, vv[k0:k1]) + _dot(p_ctx, cvv)) / den
                o = jnp.where(lo_mask if lo else hi_mask, o, 0.0)
                acc_blocks[n] = o if acc_blocks[n] is None else acc_blocks[n] + o
        cols.append(jnp.concatenate(acc_blocks, 0))
    o_ref[...] = jnp.concatenate(cols, 1).astype(o_ref.dtype)


def _win_lat(z_main, prev_out, sink, cache_k, cache_v, layer):
    c, su, sd, _ = _rope_tables(DEC_SEQ, WIN_HD, LANE)
    rb0 = T_CTX // DEC_SEQ
    tab = pl.BlockSpec((DEC_SEQ, LANE), lambda b: (0, 0))
    cache = pl.BlockSpec((None, None, PAST_LEN, LANE), lambda b: (b, layer, 0, 0))
    return pl.pallas_call(
        functools.partial(_win_lat_kernel, layer=layer),
        out_shape=jax.ShapeDtypeStruct((T_ALL, WIN_HEADS * WIN_HD), BF16),
        grid=(DEC_BATCH,),
        in_specs=[
            pl.BlockSpec(memory_space=pltpu.SMEM),
            pl.BlockSpec(memory_space=pl.ANY),
            pl.BlockSpec((DEC_SEQ, 256), lambda b: (rb0 + b, COL_WQ // 256)),
            pl.BlockSpec((DEC_SEQ, 256), lambda b: (rb0 + b, COL_WK // 256)),
            cache, cache, tab, tab, tab,
        ],
        out_specs=pl.BlockSpec((DEC_SEQ, 256), lambda b: (rb0 + b, 0)),
        input_output_aliases={1: 0},
        compiler_params=_params("parallel"),
        name="win_lat",
    )(sink, prev_out, z_main, z_main, cache_k, cache_v,
      jnp.asarray(c), jnp.asarray(su), jnp.asarray(sd))


def _ret_kernel(*refs, L, layer, ctx):
    if ctx:
        (df_ref, db_ref, q_ref, k_ref, v0_ref, v1_ref, g0_ref, g1_ref,
         o_ref, sf_out, sb_out, s_ref, cross_ref) = refs
    else:
        (df_ref, db_ref, prev_ref, q_ref, k_ref, v0_ref, v1_ref, g0_ref, g1_ref, s0f_ref, s0b_ref,
         o_ref, s_ref, cross_ref) = refs
    C = CHUNK
    nc = L // C
    H = RET_HEADS
    qw = H * RET_DK
    vw = H * RET_DV

    def lane_table(width, per_head, fn):
        pos = lax.broadcasted_iota(jnp.int32, (C, per_head), 0).astype(F32)
        return jnp.concatenate([fn(h, pos) for h in range(H)], 1)

    def log_gamma(ref, h):
        d = jnp.full((1, 1), ref[layer, h], F32)
        return jnp.log(jax.nn.sigmoid(d))

    lgf = [log_gamma(df_ref, h) for h in range(H)]
    lgb = [log_gamma(db_ref, h) for h in range(H)]
    q_all = q_ref[...]
    k_all = k_ref[...] * (RET_DK ** -0.5)
    v_all = jnp.concatenate([v0_ref[...], v1_ref[...]], 1)

    srow = lax.broadcasted_iota(jnp.int32, (qw, vw), 0) // RET_DK
    scol = lax.broadcasted_iota(jnp.int32, (qw, vw), 1) // RET_DV
    diag = srow == scol

    def scan(lg, reverse, s0_ref, s_out):
        if reverse:
            dq = lane_table(vw, RET_DV, lambda h, pos: jnp.exp((C - pos) * lg[h]))
            dk = lane_table(qw, RET_DK, lambda h, pos: jnp.exp(pos * lg[h]))
        else:
            dq = lane_table(vw, RET_DV, lambda h, pos: jnp.exp((pos + 1.0) * lg[h]))
            dk = lane_table(qw, RET_DK, lambda h, pos: jnp.exp((C - 1.0 - pos) * lg[h]))
        dc = jnp.concatenate([jnp.broadcast_to(jnp.exp(C * lg[h]), (1, RET_DV)) for h in range(H)], 1)
        if s0_ref is None:
            s_ref[...] = jnp.zeros((qw, vw), F32)
        else:
            s_ref[...] = jnp.zeros((qw, vw), F32)
            for h in range(H):
                s_ref[h * RET_DK:(h + 1) * RET_DK, h * RET_DV:(h + 1) * RET_DV] = s0_ref[h]
        order = range(nc - 1, -1, -1) if reverse else range(nc)
        for ci in order:
            sl = slice(ci * C, (ci + 1) * C)
            qc, kc, vc = q_all[sl], k_all[sl], v_all[sl]
            s = s_ref[...]
            cross = _dot(qc, s) * dq
            if reverse:
                cross_ref[sl, :] = cross_ref[sl, :] + cross
            else:
                cross_ref[sl, :] = cross
            upd = jnp.where(diag, _dot_tn(kc * dk, vc), 0.0)
            s_ref[...] = s * dc + upd
        if s_out is not None:
            for h in range(H):
                s_out[h] = s_ref[h * RET_DK:(h + 1) * RET_DK, h * RET_DV:(h + 1) * RET_DV]

    scan(lgf, False, None if ctx else s0f_ref, sf_out if ctx else None)
    scan(lgb, True, None if ctx else s0b_ref, sb_out if ctx else None)

    ii = lax.broadcasted_iota(jnp.int32, (C, C), 0)
    jj = lax.broadcasted_iota(jnp.int32, (C, C), 1)
    diff = (ii - jj).astype(F32)
    lane_q = lax.broadcasted_iota(jnp.int32, (1, qw), 1) // RET_DK
    g_all = jnp.concatenate([g0_ref[...], g1_ref[...]], 1)
    for h in range(H):
        dmat = (jnp.where(diff >= 0, jnp.exp(jnp.maximum(diff, 0.0) * lgf[h]), 0.0)
                + jnp.where(diff <= 0, jnp.exp(jnp.maximum(-diff, 0.0) * lgb[h]), 0.0))
        hv = slice(h * RET_DV, (h + 1) * RET_DV)
        for ci in range(nc):
            sl = slice(ci * C, (ci + 1) * C)
            qh = jnp.where(lane_q == h, q_all[sl], 0.0)
            att = _dot_nt(qh, k_all[sl]) * dmat
            o = _dot(att, v_all[sl, hv]) + cross_ref[sl, hv]
            g = g_all[sl, hv]
            o_ref[sl, hv] = ((g * jax.nn.sigmoid(g)) * _layer_norm(o)).astype(o_ref.dtype)


def _retention(z_main, prev_out, dec_f, dec_b, s0f, s0b, layer, ctx):
    L = SEQ if ctx else DEC_SEQ
    n_seq = BATCH if ctx else DEC_BATCH
    rb0 = 0 if ctx else T_CTX // DEC_SEQ

    def zcol(col):
        return pl.BlockSpec((L, 256), lambda b: (rb0 + b, col // 256))

    smem = pl.BlockSpec(memory_space=pltpu.SMEM)
    z_specs = [zcol(COL_RQ), zcol(COL_RK), zcol(COL_RV), zcol(COL_RV + 256),
               zcol(COL_RG), zcol(COL_RG + 256)]
    y_shape = jax.ShapeDtypeStruct((T_ALL, RET_HEADS * RET_DV), BF16)
    y_spec = pl.BlockSpec((L, RET_HEADS * RET_DV), lambda b: (rb0 + b, 0))
    scratch = [pltpu.VMEM((RET_HEADS * RET_DK, RET_HEADS * RET_DV), F32),
               pltpu.VMEM((L, RET_HEADS * RET_DV), F32)]
    kern = functools.partial(_ret_kernel, L=L, layer=layer, ctx=ctx)
    if ctx:
        st_shape = jax.ShapeDtypeStruct((BATCH, RET_HEADS, RET_DK, RET_DV), F32)
        st_spec = pl.BlockSpec((None, RET_HEADS, RET_DK, RET_DV), lambda b: (b, 0, 0, 0))
        return pl.pallas_call(
            kern,
            out_shape=(y_shape, st_shape, st_shape),
            grid=(n_seq,),
            in_specs=[smem, smem] + z_specs,
            out_specs=(y_spec, st_spec, st_spec),
            scratch_shapes=scratch,
            compiler_params=_params("parallel"),
            name="ret_ctx",
        )(dec_f, dec_b, *([z_main] * 6))
    s0_spec = pl.BlockSpec((None, None, RET_HEADS, RET_DK, RET_DV), lambda b: (b, layer, 0, 0, 0))
    return pl.pallas_call(
        kern,
        out_shape=y_shape,
        grid=(n_seq,),
        in_specs=[smem, smem, pl.BlockSpec(memory_space=pl.ANY)] + z_specs + [s0_spec, s0_spec],
        out_specs=y_spec,
        scratch_shapes=scratch,
        input_output_aliases={2: 0},
        compiler_params=_params("parallel"),
        name="ret_lat",
    )(dec_f, dec_b, prev_out, *([z_main] * 6), s0f, s0b)


def _rms_norm(x, g):
    return x * lax.rsqrt(jnp.mean(x * x, -1, keepdims=True) + RMS_EPS) * g


def _mla_attend(qn, qr, kn, kr, vv, o_ref, row0):
    scale = (MLA_NOPE + MLA_ROPE) ** -0.5
    lane_n = lax.broadcasted_iota(jnp.int32, (1, MLA_HEADS * MLA_NOPE), 1) // MLA_NOPE
    lane_r = lax.broadcasted_iota(jnp.int32, (1, LANE), 1)
    kr32 = jnp.where(lane_r < MLA_ROPE, kr, 0.0)
    acc = None
    for h in range(MLA_HEADS):
        qnh = jnp.where(lane_n == h, qn, 0.0)
        qrh = qr if h == 0 else pltpu.roll(qr, LANE - h * MLA_ROPE, axis=1)
        qrh = jnp.where(lane_r < MLA_ROPE, qrh, 0.0)
        s = (_dot_nt(qnh, kn) + _dot_nt(qrh, kr32)) * scale
        m = jnp.max(s, -1, keepdims=True)
        p = jnp.exp(s - m)
        den = jnp.sum(p, -1, keepdims=True)
        o = jnp.where(lane_n == h, _dot(p, vv) / den, 0.0)
        acc = o if acc is None else acc + o
    o_ref[row0:row0 + acc.shape[0], :] = acc.astype(o_ref.dtype)


def _mla_ctx_kernel(cq_ref, ckv_ref, kr_ref, qg_ref, kg_ref, wqn_ref, wqr_ref, wk_ref, wv_ref,
                    o_ref, ckvn_ref):
    cqn = _rms_norm(cq_ref[...], qg_ref[...])
    qn = _dot(cqn, wqn_ref[...])
    qr = _dot(cqn, wqr_ref[...])
    ckvn = _rms_norm(ckv_ref[...], kg_ref[...])
    ckvn_ref[...] = ckvn
    kn = _dot(ckvn, wk_ref[...])
    vv = _dot(ckvn, wv_ref[...])
    _mla_attend(qn, qr, kn, kr_ref[...], vv, o_ref, 0)


def _mla_weight_specs(layer):
    return [
        pl.BlockSpec((None, 1, MLA_Q_LORA), lambda b: (layer, 0, 0)),
        pl.BlockSpec((None, 1, MLA_KV_LORA), lambda b: (layer, 0, 0)),
        pl.BlockSpec((None, MLA_Q_LORA, MLA_HEADS * MLA_NOPE), lambda b: (layer, 0, 0)),
        pl.BlockSpec((None, MLA_Q_LORA, MLA_HEADS * MLA_ROPE), lambda b: (layer, 0, 0)),
        pl.BlockSpec((None, MLA_KV_LORA, MLA_HEADS * MLA_NOPE), lambda b: (layer, 0, 0)),
        pl.BlockSpec((None, MLA_KV_LORA, MLA_HEADS * MLA_V), lambda b: (layer, 0, 0)),
    ]


def _mla_ctx(z_main, weights, layer):
    return pl.pallas_call(
        _mla_ctx_kernel,
        out_shape=(jax.ShapeDtypeStruct((T_ALL, MLA_HEADS * MLA_V), BF16),
                   jax.ShapeDtypeStruct((T_CTX, MLA_KV_LORA), F32)),
        grid=(BATCH,),
        in_specs=[
            pl.BlockSpec((SEQ, 256), lambda b: (b, COL_CQ // 256)),
            pl.BlockSpec((SEQ, LANE), lambda b: (b, COL_CKV // LANE)),
            pl.BlockSpec((SEQ, LANE), lambda b: (b, COL_KROPE // LANE)),
        ] + _mla_weight_specs(layer),
        out_specs=(pl.BlockSpec((SEQ, 256), lambda b: (b, 0)),
                   pl.BlockSpec((SEQ, MLA_KV_LORA), lambda b: (b, 0))),
        compiler_params=_params("parallel"),
        name="mla_ctx",
    )(z_main, z_main, z_main, *weights)


def _mla_lat_kernel(prev_ref, cq_ref, ckv_ref, kr_ref, cckv_ref, ckr_ref, c_ref, su_ref, sd_ref,
                    qg_ref, kg_ref, wqn_ref, wqr_ref, wk_ref, wv_ref, o_ref):
    del prev_ref
    half = MLA_ROPE // 2
    c, su, sd = c_ref[...], su_ref[...], sd_ref[...]
    cqn = _rms_norm(cq_ref[...], qg_ref[...])
    qn = _dot(cqn, wqn_ref[...])
    qr = _rope128(_dot(cqn, wqr_ref[...]), c, su, sd, half)
    ckvn = _rms_norm(ckv_ref[...], kg_ref[...])
    ckv_all = jnp.concatenate([ckvn, cckv_ref[...]], 0)
    kn = _dot(ckv_all, wk_ref[...])
    vv = _dot(ckv_all, wv_ref[...])
    kr = jnp.concatenate([_rope128(kr_ref[...], c, su, sd, half), ckr_ref[...]], 0)
    for n in range(DEC_SEQ // 256):
        rows = slice(n * 256, (n + 1) * 256)
        _mla_attend(qn[rows], qr[rows], kn, kr, vv, o_ref, n * 256)


def _mla_lat(z_main, prev_out, cache_ckv, cache_kr_pad, weights, layer):
    c, su, sd, _ = _rope_tables(DEC_SEQ, MLA_ROPE, LANE)
    rb0 = T_CTX // DEC_SEQ
    tab = pl.BlockSpec((DEC_SEQ, LANE), lambda b: (0, 0))
    cache = pl.BlockSpec((None, None, PAST_LEN, LANE), lambda b: (b, layer, 0, 0))
    return pl.pallas_call(
        _mla_lat_kernel,
        out_shape=jax.ShapeDtypeStruct((T_ALL, MLA_HEADS * MLA_V), BF16),
        grid=(DEC_BATCH,),
        in_specs=[
            pl.BlockSpec(memory_space=pl.ANY),
            pl.BlockSpec((DEC_SEQ, 256), lambda b: (rb0 + b, COL_CQ // 256)),
            pl.BlockSpec((DEC_SEQ, LANE), lambda b: (rb0 + b, COL_CKV // LANE)),
            pl.BlockSpec((DEC_SEQ, LANE), lambda b: (rb0 + b, COL_KROPE // LANE)),
            cache, cache, tab, tab, tab,
        ] + _mla_weight_specs(layer),
        out_specs=pl.BlockSpec((DEC_SEQ, 256), lambda b: (rb0 + b, 0)),
        input_output_aliases={0: 0},
        compiler_params=_params("parallel"),
        name="mla_lat",
    )(prev_out, z_main, z_main, z_main, cache_ckv, cache_kr_pad,
      jnp.asarray(c), jnp.asarray(su), jnp.asarray(sd), *weights)


def _route(logits_t, rb):
    scores = jax.nn.sigmoid(logits_t)
    biased = scores + rb
    sc = [scores[e:e + 1, :] for e in range(N_EXPERTS)]
    bi = [biased[e:e + 1, :] for e in range(N_EXPERTS)]
    epg = EXPERTS_PER_GROUP
    gsum = []
    for g in range(N_GROUPS):
        v = bi[g * epg:(g + 1) * epg]
        best = None
        for a in range(epg):
            for b in range(a + 1, epg):
                pair = v[a] + v[b]
                best = pair if best is None else jnp.maximum(best, pair)
        gsum.append(best)
    combine = []
    sel = []
    for g in range(N_GROUPS):
        is_best = None
        for g2 in range(N_GROUPS):
            if g2 == g:
                continue
            c = gsum[g] > gsum[g2] if g2 < g else gsum[g] >= gsum[g2]
            is_best = c if is_best is None else jnp.logical_and(is_best, c)
        for a in range(epg):
            e = g * epg + a
            rank = jnp.zeros_like(bi[e])
            for b in range(epg):
                if b == a:
                    continue
                e2 = g * epg + b
                ahead = bi[e2] >= bi[e] if b < a else bi[e2] > bi[e]
                rank = rank + jnp.where(ahead, 1.0, 0.0)
            sel.append(jnp.logical_and(is_best, rank < 2.0))
    wsum = None
    for e in range(N_EXPERTS):
        w = jnp.where(sel[e], sc[e], 0.0)
        wsum = w if wsum is None else wsum + w
    for e in range(N_EXPERTS):
        combine.append(jnp.where(sel[e], ROUTE_SCALE * sc[e] / wsum, 0.0))
    return jnp.concatenate(combine, 0)


def _merge_kernel(ya_ref, yb_ref, yc_ref, yd_ref, gt_ref, x_ref, m_ref,
                  wa_ref, wb_ref, wc_ref, wd_ref, wo_ref, g_ref, b_ref, rw_ref, rb_ref,
                  x1_ref, h2_ref, cmb_ref, wab, wbb, wcb, wdb, wob):
    @pl.when(pl.program_id(0) == 0)
    def _():
        wab[...] = wa_ref[...].astype(BF16)
        wbb[...] = wb_ref[...].astype(BF16)
        wcb[...] = wc_ref[...].astype(BF16)
        wdb[...] = wd_ref[...].astype(BF16)
        wob[...] = wo_ref[...].astype(BF16)

    D = D_MODEL
    merged = None
    for i, (y_ref, w) in enumerate(((ya_ref, wab), (yb_ref, wbb), (yc_ref, wcb), (yd_ref, wdb))):
        t = gt_ref[:, i * D:(i + 1) * D].astype(F32) * jnp.dot(
            y_ref[...], w[...], preferred_element_type=F32)
        merged = t if merged is None else merged + t
    out1 = jnp.dot(merged.astype(BF16), wob[...], preferred_element_type=F32)
    g1 = m_ref[:, 2 * D:3 * D]
    s2 = m_ref[:, 3 * D:4 * D]
    sc2 = m_ref[:, 4 * D:5 * D]
    x1 = _layer_norm(ALPHA * x_ref[...] + g1 * out1) * g_ref[...] + b_ref[...]
    x1_ref[...] = x1
    h2 = _layer_norm(x1) * (1.0 + sc2) + s2
    h2_ref[...] = h2.astype(h2_ref.dtype)
    logits_t = lax.dot_general(rw_ref[...], h2, (((1,), (1,)), ((), ())),
                               precision=HIGHEST, preferred_element_type=F32)
    cmb_ref[...] = _route(logits_t, rb_ref[...])


def _merge(ya, yb, yc, yd, gates, x, mods_l, w_br_a, w_br_b, w_br_c, w_br_d, w_out,
           ln1_g, ln1_b, router_wt, router_b, layer):
    tm = 256
    row = _mod_row(tm)
    D = D_MODEL

    def tile(w):
        return pl.BlockSpec((tm, w), lambda i: (i, 0))

    def weight(k, n):
        return pl.BlockSpec((None, k, n), lambda i: (layer, 0, 0))

    return pl.pallas_call(
        _merge_kernel,
        out_shape=(jax.ShapeDtypeStruct((T_ALL, D), F32),
                   jax.ShapeDtypeStruct((T_ALL, D), BF16),
                   jax.ShapeDtypeStruct((N_EXPERTS, T_ALL), F32)),
        grid=(T_ALL // tm,),
        in_specs=[
            tile(256), tile(256), tile(512), tile(256), tile(4 * D), tile(D),
            pl.BlockSpec((None, 1, 6 * D), lambda i: (row(i), 0, 0)),
            weight(256, D), weight(256, D), weight(512, D), weight(256, D), weight(D, D),
            weight(1, D), weight(1, D),
            pl.BlockSpec((N_EXPERTS, D), lambda i: (0, 0)),
            pl.BlockSpec((N_EXPERTS, 1), lambda i: (0, 0)),
        ],
        out_specs=(tile(D), tile(D), pl.BlockSpec((N_EXPERTS, tm), lambda i: (0, i))),
        scratch_shapes=[pltpu.VMEM((256, D), BF16), pltpu.VMEM((256, D), BF16),
                        pltpu.VMEM((512, D), BF16), pltpu.VMEM((256, D), BF16),
                        pltpu.VMEM((D, D), BF16)],
        compiler_params=_params("arbitrary"),
        name="merge",
    )(ya, yb, yc, yd, gates, x, mods_l, w_br_a, w_br_b, w_br_c, w_br_d, w_out,
      ln1_g.reshape(DEPTH, 1, D), ln1_b.reshape(DEPTH, 1, D), router_wt, router_b.reshape(N_EXPERTS, 1))


def _moe_kernel(h_ref, c_ref, x1_ref, m_ref, wg_ref, wu_ref, wd_ref, g_ref, b_ref, o_ref, acc_ref):
    e = pl.program_id(1)

    @pl.when(e == 0)
    def _():
        acc_ref[...] = jnp.zeros_like(acc_ref)

    h = h_ref[...]
    gate = jnp.dot(h, wg_ref[...].astype(BF16), preferred_element_type=F32)
    up = jnp.dot(h, wu_ref[...].astype(BF16), preferred_element_type=F32)
    cmb = c_ref[...]
    lane = lax.broadcasted_iota(jnp.int32, cmb.shape, 1)
    ce = jnp.sum(jnp.where(lane == e, cmb, 0.0), -1, keepdims=True)
    hid = (gate * jax.nn.sigmoid(gate)) * up * ce
    acc_ref[...] += jnp.dot(hid.astype(BF16), wd_ref[...].astype(BF16), preferred_element_type=F32)

    @pl.when(e == N_EXPERTS - 1)
    def _():
        g2 = m_ref[:, 5 * D_MODEL:6 * D_MODEL]
        y = _layer_norm(ALPHA * x1_ref[...] + g2 * acc_ref[...])
        o_ref[...] = y * g_ref[...] + b_ref[...]


def _moe(h2, combine, x1, mods_l, w_gate, w_up, w_down, ln2_g, ln2_b, layer):
    tm = 1024
    row = _mod_row(tm)
    D = D_MODEL
    return pl.pallas_call(
        _moe_kernel,
        out_shape=jax.ShapeDtypeStruct((T_ALL, D), F32),
        grid=(T_ALL // tm, N_EXPERTS),
        in_specs=[
            pl.BlockSpec((tm, D), lambda i, e: (i, 0)),
            pl.BlockSpec((tm, N_EXPERTS), lambda i, e: (i, 0)),
            pl.BlockSpec((tm, D), lambda i, e: (i, 0)),
            pl.BlockSpec((None, 1, 6 * D), lambda i, e: (row(i), 0, 0)),
            pl.BlockSpec((None, None, D, D_EXPERT), lambda i, e: (layer, e, 0, 0)),
            pl.BlockSpec((None, None, D, D_EXPERT), lambda i, e: (layer, e, 0, 0)),
            pl.BlockSpec((None, None, D_EXPERT, D), lambda i, e: (layer, e, 0, 0)),
            pl.BlockSpec((None, 1, D), lambda i, e: (layer, 0, 0)),
            pl.BlockSpec((None, 1, D), lambda i, e: (layer, 0, 0)),
        ],
        out_specs=pl.BlockSpec((tm, D), lambda i, e: (i, 0)),
        scratch_shapes=[pltpu.VMEM((tm, D), F32)],
        compiler_params=_params("parallel", "arbitrary"),
        name="moe",
    )(h2, combine, x1, mods_l, w_gate, w_up, w_down,
      ln2_g.reshape(DEPTH, 1, D), ln2_b.reshape(DEPTH, 1, D))


def kernel(x_prompt, x_sample, cache_win_k, cache_win_v, cache_mla_ckv, cache_mla_krope,
           state_ret_fwd, state_ret_bwd, c, c_ctx, w_ada, b_ada, w_in,
           hy_conv_w, hy_conv_b, hy_w1, hy_b1, hy_w2, hy_b2, hy_w3, hy_bias,
           win_sink, ret_decay_fwd, ret_decay_bwd, mla_q_norm, mla_kv_norm, mla_w_uq, mla_w_ukv,
           w_br_a, w_br_b, w_br_c, w_br_d, w_out, ln1_g, ln1_b, ln2_g, ln2_b,
           router_w, router_b, moe_w_gate, moe_w_up, moe_w_down):
    D = D_MODEL
    x = jnp.concatenate([x_prompt.reshape(T_CTX, D), x_sample.reshape(T_LAT, D)], 0)

    cvec = jnp.zeros((8, D), F32).at[0].set(c_ctx).at[1:1 + DEC_BATCH].set(c)
    mods = _ada_mods(cvec, w_ada, b_ada)[:, :1 + DEC_BATCH].reshape(DEPTH, 1 + DEC_BATCH, 1, 6 * D)

    w_gate_cols = w_in[:, :, COL_GATE:]
    cache_k = cache_win_k.reshape(DEC_BATCH, DEPTH, PAST_LEN, WIN_KV_HEADS * WIN_HD)
    cache_v = cache_win_v.reshape(DEC_BATCH, DEPTH, PAST_LEN, WIN_KV_HEADS * WIN_HD)
    cache_kr = jnp.pad(cache_mla_krope, ((0, 0), (0, 0), (0, 0), (0, LANE - MLA_ROPE)))

    uq = mla_w_uq.reshape(DEPTH, MLA_Q_LORA, MLA_HEADS, MLA_NOPE + MLA_ROPE)
    ukv = mla_w_ukv.reshape(DEPTH, MLA_KV_LORA, MLA_HEADS, MLA_NOPE + MLA_V)
    mla_weights = (
        mla_q_norm.reshape(DEPTH, 1, MLA_Q_LORA),
        mla_kv_norm.reshape(DEPTH, 1, MLA_KV_LORA),
        uq[..., :MLA_NOPE].reshape(DEPTH, MLA_Q_LORA, MLA_HEADS * MLA_NOPE),
        uq[..., MLA_NOPE:].reshape(DEPTH, MLA_Q_LORA, MLA_HEADS * MLA_ROPE),
        ukv[..., :MLA_NOPE].reshape(DEPTH, MLA_KV_LORA, MLA_HEADS * MLA_NOPE),
        ukv[..., MLA_NOPE:].reshape(DEPTH, MLA_KV_LORA, MLA_HEADS * MLA_V),
    )

    hy_w1p = jnp.pad(hy_w1, ((0, 0), (0, LANE - HY_EMB), (0, 0)))
    dft = {}
    for L in (SEQ, DEC_SEQ):
        fwd, inv = _dft_tables(L)
        dft[L] = (jnp.asarray(fwd).astype(BF16), jnp.asarray(inv).astype(BF16))
    router_wt = router_w.T

    new_k, new_v, new_ckv, new_kr, new_sf, new_sb = [], [], [], [], [], []
    for l in range(DEPTH):
        mods_l = mods[l]
        h = _ln_mod(x, mods_l)
        z = _in_proj(h, w_in, l, Z_MAIN, Z_MAIN // 2, F32, gate=False)
        gates = _in_proj(h, w_gate_cols, l, 4 * D, D, BF16, gate=True)

        ya = None
        for L, n_seq, rb0 in ((SEQ, BATCH, 0), (DEC_SEQ, DEC_BATCH, T_CTX // DEC_SEQ)):
            fwd, inv = dft[L]
            kre, kim = _hy_filters(L, hy_w1p[l], hy_b1[l][None], hy_w2[l], hy_b2[l][None], hy_w3[l], fwd)
            ya = _hyena(z, ya, l, L, n_seq, rb0, hy_conv_w, hy_conv_b, hy_bias, kre, kim, fwd, inv)

        yb = _win_ctx(z, win_sink, l)
        yb = _win_lat(z, yb, win_sink, cache_k, cache_v, l)

        yc, sf, sb = _retention(z, None, ret_decay_fwd, ret_decay_bwd, None, None, l, ctx=True)
        yc = _retention(z, yc, ret_decay_fwd, ret_decay_bwd, state_ret_fwd, state_ret_bwd, l, ctx=False)

        yd, ckvn = _mla_ctx(z, mla_weights, l)
        yd = _mla_lat(z, yd, cache_mla_ckv, cache_kr, mla_weights, l)

        x1, h2, combine_t = _merge(ya, yb, yc, yd, gates, x, mods_l, w_br_a, w_br_b, w_br_c, w_br_d,
                                   w_out, ln1_g, ln1_b, router_wt, router_b, l)
        x = _moe(h2, combine_t.T, x1, mods_l, moe_w_gate, moe_w_up, moe_w_down, ln2_g, ln2_b, l)

        zc = z[:T_CTX]
        new_k.append(zc[:, COL_WK:COL_WK + 128].reshape(BATCH, SEQ, WIN_KV_HEADS, WIN_HD))
        new_v.append(zc[:, COL_WV:COL_WV + 128].reshape(BATCH, SEQ, WIN_KV_HEADS, WIN_HD))
        new_ckv.append(ckvn.reshape(BATCH, SEQ, MLA_KV_LORA))
        new_kr.append(zc[:, COL_KROPE:COL_KROPE + MLA_ROPE].reshape(BATCH, SEQ, MLA_ROPE))
        new_sf.append(sf)
        new_sb.append(sb)

    y_prompt = x[:T_CTX].reshape(BATCH, SEQ, D)
    y_sample = x[T_CTX:].reshape(DEC_BATCH, DEC_SEQ, D)
    return (y_prompt, y_sample, jnp.stack(new_k, 1), jnp.stack(new_v, 1), jnp.stack(new_ckv, 1),
            jnp.stack(new_kr, 1), jnp.stack(new_sf, 1), jnp.stack(new_sb, 1))
```

```python
import functools
import math

import numpy as np
import jax
import jax.numpy as jnp
from jax import lax
from jax.experimental import pallas as pl
from jax.experimental.pallas import tpu as pltpu

F32 = jnp.float32
BF16 = jnp.bfloat16
HIGHEST = lax.Precision.HIGHEST

D_MODEL = 1024
BATCH = 16
SEQ = 256
DEPTH = 2
DEC_BATCH = 2
DEC_SEQ = 1024
PAST_LEN = 256
GRID_W = 64
CHUNK = 128
ROPE_BASE = 10000.0
NEG = -1e30
LN_EPS = 1e-5
RMS_EPS = 1e-6

HY_W = 256
HY_BANDS = 16
HY_EMB = 1 + 2 * HY_BANDS
HY_FFN = 64
HY_FAST_DECAY = 0.3
HY_SLOW_DECAY = 1.5
HY_TARGET = 1e-2

WIN_HEADS = 4
WIN_KV_HEADS = 2
WIN_HD = 64
WINDOW = 128

RET_HEADS = 4
RET_DK = 64
RET_DV = 128

MLA_HEADS = 4
MLA_Q_LORA = 256
MLA_KV_LORA = 128
MLA_NOPE = 64
MLA_ROPE = 32
MLA_V = 64

N_EXPERTS = 16
N_GROUPS = 4
EXPERTS_PER_GROUP = N_EXPERTS // N_GROUPS
D_EXPERT = 256
ROUTE_SCALE = 2.5

ALPHA = (2.0 * DEPTH) ** 0.25

T_CTX = BATCH * SEQ
T_LAT = DEC_BATCH * DEC_SEQ
T_ALL = T_CTX + T_LAT

COL_HY = 0
COL_WQ = 768
COL_WK = 1024
COL_WV = 1152
COL_RQ = 1280
COL_RK = 1536
COL_RV = 1792
COL_RG = 2304
COL_CQ = 2816
COL_CKV = 3072
COL_KROPE = 3200
COL_GATE = 3232
IN_COLS = COL_GATE + 4 * D_MODEL
Z_MAIN = 3328

LANE = 128
VMEM_LIMIT = 56 * 1024 * 1024


def _params(*sem):
    return pltpu.CompilerParams(dimension_semantics=sem, vmem_limit_bytes=VMEM_LIMIT)


def _dot(a, b):
    return jnp.dot(a.astype(BF16), b.astype(BF16), preferred_element_type=F32)


def _dot_nt(a, b):
    return lax.dot_general(a.astype(BF16), b.astype(BF16), (((1,), (1,)), ((), ())),
                           preferred_element_type=F32)


def _dot_tn(a, b):
    return lax.dot_general(a.astype(BF16), b.astype(BF16), (((0,), (0,)), ((), ())),
                           preferred_element_type=F32)


def _layer_norm(x):
    mu = jnp.mean(x, -1, keepdims=True)
    xc = x - mu
    var = jnp.mean(xc * xc, -1, keepdims=True)
    return xc * lax.rsqrt(var + LN_EPS)


def _mod_row(tile_rows):
    def row(i):
        start = i * tile_rows
        return jnp.where(start < T_CTX, 0, 1 + (start - T_CTX) // DEC_SEQ)
    return row


@functools.lru_cache(maxsize=None)
def _dft_tables(L):
    f = np.arange(L, dtype=np.int64)[:, None]
    s = np.arange(L, dtype=np.int64)[None, :]
    ang = np.pi * ((f * s) % (2 * L)).astype(np.float64) / L
    cos = np.cos(ang)
    sin = np.sin(ang)
    alt = np.where(np.arange(L) % 2 == 0, 1.0, -1.0)
    fwd_im = -sin
    fwd_im[0, :] = alt
    fwd = np.concatenate([cos, fwd_im], 0)
    inv_re = cos.T / L
    inv_re[:, 0] = 1.0 / (2 * L)
    inv_im = -sin.T / L
    inv_im[:, 0] = alt / (2 * L)
    inv = np.concatenate([inv_re, inv_im], 1)
    return fwd.astype(np.float32), inv.astype(np.float32)


@functools.lru_cache(maxsize=None)
def _hyena_embedding(L):
    t01 = np.linspace(0.0, 1.0, L, dtype=np.float64)[:, None]
    bands = np.linspace(1e-4, HY_BANDS - 1, HY_BANDS, dtype=np.float64)
    ang = (2.0 * math.pi / L) * np.arange(L, dtype=np.float64)[:, None] * bands[None, :]
    z = np.concatenate([t01, np.cos(ang), -np.sin(ang)], -1)
    zp = np.zeros((L, LANE), np.float64)
    zp[:, :HY_EMB] = z
    deltas = np.abs(np.linspace(math.log(HY_TARGET) / HY_SLOW_DECAY,
                                math.log(HY_TARGET) / HY_FAST_DECAY, HY_W, dtype=np.float64))
    return zp.astype(np.float32), deltas[None, :].astype(np.float32)


@functools.lru_cache(maxsize=None)
def _rope_tables(L, rot_dim, width):
    rows = L // GRID_W
    n_freq = rot_dim // 4
    half = rot_dim // 2
    inv = ROPE_BASE ** (-np.arange(n_freq, dtype=np.float64) / n_freq)
    pos = np.arange(L)
    row = (pos // GRID_W).astype(np.float64)
    col = (pos % GRID_W).astype(np.float64)
    ang = np.concatenate([row[:, None] * inv, col[:, None] * inv], -1)
    cos, sin = np.cos(ang), np.sin(ang)
    zero = np.zeros_like(sin)
    c = np.tile(np.concatenate([cos, cos], -1), (1, width // rot_dim))
    s_up = np.tile(np.concatenate([-sin, zero], -1), (1, width // rot_dim))
    s_dn = np.tile(np.concatenate([zero, sin], -1), (1, width // rot_dim))
    return c.astype(np.float32), s_up.astype(np.float32), s_dn.astype(np.float32), half


def _rope128(x, c, s_up, s_dn, half):
    up = pltpu.roll(x, LANE - half, axis=1)
    dn = pltpu.roll(x, half, axis=1)
    return x * c + up * s_up + dn * s_dn


def _ada_kernel(c_ref, w_ref, b_ref, o_ref):
    cv = c_ref[...]
    s = cv * jax.nn.sigmoid(cv)
    o_ref[...] = jnp.dot(s, w_ref[...], precision=HIGHEST, preferred_element_type=F32) + b_ref[...]


def _ada_mods(cvec, w_ada, b_ada):
    tn = 1536
    n = 6 * D_MODEL
    return pl.pallas_call(
        _ada_kernel,
        out_shape=jax.ShapeDtypeStruct((DEPTH, 8, n), F32),
        grid=(DEPTH, n // tn),
        in_specs=[
            pl.BlockSpec((8, D_MODEL), lambda l, j: (0, 0)),
            pl.BlockSpec((None, D_MODEL, tn), lambda l, j: (l, 0, j)),
            pl.BlockSpec((None, 1, tn), lambda l, j: (l, 0, j)),
        ],
        out_specs=pl.BlockSpec((None, 8, tn), lambda l, j: (l, 0, j)),
        compiler_params=_params("parallel", "parallel"),
        name="ada_mods",
    )(cvec, w_ada, b_ada.reshape(DEPTH, 1, n))


def _lnmod_kernel(*refs):
    x_ref, m_ref, h_ref = refs[-3:]
    y = _layer_norm(x_ref[...])
    s1 = m_ref[:, 0:D_MODEL]
    sc1 = m_ref[:, D_MODEL:2 * D_MODEL]
    h_ref[...] = (y * (1.0 + sc1) + s1).astype(h_ref.dtype)


def _ln_mod(x_group, prev_out, mods_l, row0):
    tm = 512
    row = _mod_row(tm)
    tile0 = row0 // tm
    in_specs = [
        pl.BlockSpec((tm, D_MODEL), lambda i: (i, 0)),
        pl.BlockSpec((None, 1, 6 * D_MODEL), lambda i: (row(tile0 + i), 0, 0)),
    ]
    args = [x_group, mods_l]
    if prev_out is not None:
        in_specs = [pl.BlockSpec(memory_space=pl.ANY)] + in_specs
        args = [prev_out] + args
    return pl.pallas_call(
        _lnmod_kernel,
        out_shape=jax.ShapeDtypeStruct((T_ALL, D_MODEL), BF16),
        grid=(x_group.shape[0] // tm,),
        in_specs=in_specs,
        out_specs=pl.BlockSpec((tm, D_MODEL), lambda i: (tile0 + i, 0)),
        input_output_aliases={} if prev_out is None else {0: 0},
        compiler_params=_params("parallel"),
        name="ln_mod",
    )(*args)


def _proj_kernel(h_ref, w_ref, o_ref, wb_ref, *, gate):
    @pl.when(pl.program_id(1) == 0)
    def _():
        wb_ref[...] = w_ref[...].T.astype(BF16)

    r = jnp.dot(h_ref[...], wb_ref[...], preferred_element_type=F32)
    if gate:
        r = jax.nn.sigmoid(r)
    o_ref[...] = r.astype(o_ref.dtype)


def _in_proj(h, w_t, layer, col0, n_cols, tn, out_dtype, gate):
    tm = 512
    return pl.pallas_call(
        functools.partial(_proj_kernel, gate=gate),
        out_shape=jax.ShapeDtypeStruct((T_ALL, n_cols), out_dtype),
        grid=(n_cols // tn, T_ALL // tm),
        in_specs=[
            pl.BlockSpec((tm, D_MODEL), lambda j, i: (i, 0)),
            pl.BlockSpec((pl.Element(tn), pl.Element(D_MODEL)),
                         lambda j, i: (pl.multiple_of(layer * IN_COLS + col0 + j * tn, 8), 0)),
        ],
        out_specs=pl.BlockSpec((tm, tn), lambda j, i: (i, j)),
        scratch_shapes=[pltpu.VMEM((D_MODEL, tn), BF16)],
        compiler_params=_params("parallel", "arbitrary"),
        name="gate_proj" if gate else "in_proj",
    )(h, w_t)


def _hy_filter_kernel(z_ref, dl_ref, w1_ref, b1_ref, w2_ref, b2_ref, w3_ref, fwd_ref,
                      kre_ref, kim_ref, *, L):
    z = z_ref[...]
    a = jnp.sin(jnp.dot(z, w1_ref[...], precision=HIGHEST, preferred_element_type=F32) + b1_ref[...])
    a = jnp.sin(jnp.dot(a, w2_ref[...], precision=HIGHEST, preferred_element_type=F32) + b2_ref[...])
    h = jnp.dot(a, w3_ref[...], precision=HIGHEST, preferred_element_type=F32)
    decay = jnp.exp(-z[:, 0:1] * dl_ref[...])
    not_first = lax.broadcasted_iota(jnp.int32, (L, HY_W), 0) > 0
    sums, diffs = [], []
    for o in range(2):
        fw = h[:, (2 * o) * HY_W:(2 * o + 1) * HY_W] * decay
        bw = jnp.where(not_first, h[:, (2 * o + 1) * HY_W:(2 * o + 2) * HY_W] * decay, 0.0)
        sums.append(fw + bw)
        diffs.append(fw - bw)
    p = _dot(fwd_ref[...], jnp.concatenate(sums, 1))
    q = _dot(fwd_ref[L:2 * L, :], jnp.concatenate(diffs, 1))
    kre_ref[...] = p[0:L]
    first = lax.broadcasted_iota(jnp.int32, (L, 2 * HY_W), 0) == 0
    kim_ref[...] = jnp.where(first, p[L:L + 1], q)


def _hy_filters(L, w1p, b1, w2, b2, w3, fwd):
    zemb, deltas = _hyena_embedding(L)
    out = jax.ShapeDtypeStruct((L, 2 * HY_W), F32)
    return pl.pallas_call(
        functools.partial(_hy_filter_kernel, L=L),
        out_shape=(out, out),
        compiler_params=pltpu.CompilerParams(vmem_limit_bytes=VMEM_LIMIT),
        name=f"hy_filters_{L}",
    )(jnp.asarray(zemb), jnp.asarray(deltas), w1p, b1, w2, b2, w3, fwd)


def _hyena_kernel(*refs, L, aliased):
    if aliased:
        refs = refs[1:]
    hy_ref, cw_ref, cb_ref, bias_ref, kre_ref, kim_ref, fwd_ref, inv_ref, o_ref = refs
    x = hy_ref[...]
    rows = lax.broadcasted_iota(jnp.int32, x.shape, 0)
    prev = jnp.where(rows == 0, 0.0, pltpu.roll(x, 1, axis=0))
    nxt = jnp.where(rows == L - 1, 0.0, pltpu.roll(x, L - 1, axis=0))
    z = prev * cw_ref[0:1, :] + x * cw_ref[1:2, :] + nxt * cw_ref[2:3, :] + cb_ref[...]
    v, x1, x2 = z[:, 0:HY_W], z[:, HY_W:2 * HY_W], z[:, 2 * HY_W:3 * HY_W]
    first = lax.broadcasted_iota(jnp.int32, (L, HY_W), 0) == 0

    def long_conv(u, o):
        uf = _dot(fwd_ref[...], u)
        ure, uim = uf[0:L], uf[L:2 * L]
        kre = kre_ref[:, o * HY_W:(o + 1) * HY_W]
        kim = kim_ref[:, o * HY_W:(o + 1) * HY_W]
        yre = jnp.where(first, ure * kre, ure * kre - uim * kim)
        yim = jnp.where(first, uim * kim, ure * kim + uim * kre)
        y = _dot(inv_ref[...], jnp.concatenate([yre, yim], 0))
        return y + u * bias_ref[o:o + 1, :]

    u = x1 * long_conv(v, 0)
    o_ref[...] = (x2 * long_conv(u, 1)).astype(o_ref.dtype)


def _hyena(z_main, prev_out, layer, L, n_seq, row_block0, conv_w, conv_b, bias, kre, kim, fwd, inv):
    aliased = prev_out is not None
    in_specs = [
        pl.BlockSpec((L, 3 * HY_W), lambda b: (row_block0 + b, 0)),
        pl.BlockSpec((None, 3, 3 * HY_W), lambda b: (layer, 0, 0)),
        pl.BlockSpec((None, 1, 3 * HY_W), lambda b: (layer, 0, 0)),
        pl.BlockSpec((None, 2, HY_W), lambda b: (layer, 0, 0)),
        pl.BlockSpec((L, 2 * HY_W), lambda b: (0, 0)),
        pl.BlockSpec((L, 2 * HY_W), lambda b: (0, 0)),
        pl.BlockSpec((2 * L, L), lambda b: (0, 0)),
        pl.BlockSpec((L, 2 * L), lambda b: (0, 0)),
    ]
    args = [z_main, conv_w, conv_b.reshape(DEPTH, 1, 3 * HY_W), bias, kre, kim, fwd, inv]
    if aliased:
        in_specs = [pl.BlockSpec(memory_space=pl.ANY)] + in_specs
        args = [prev_out] + args
    return pl.pallas_call(
        functools.partial(_hyena_kernel, L=L, aliased=aliased),
        out_shape=jax.ShapeDtypeStruct((T_ALL, HY_W), BF16),
        grid=(n_seq,),
        in_specs=in_specs,
        out_specs=pl.BlockSpec((L, HY_W), lambda b: (row_block0 + b, 0)),
        input_output_aliases={0: 0} if aliased else {},
        compiler_params=_params("parallel"),
        name=f"hyena_{L}",
    )(*args)


def _win_masks():
    lane = lax.broadcasted_iota(jnp.int32, (1, LANE), 1)
    return lane < WIN_HD, lane >= WIN_HD


def _win_head_operands(q, k, v, h):
    lo_mask, hi_mask = _win_masks()
    col = h // 2
    lo = h % 2 == 0
    q128 = jnp.where(lo_mask if lo else hi_mask, q[:, col * LANE:(col + 1) * LANE], 0.0)
    swap = h in (1, 2)
    if swap:
        k = pltpu.roll(k, WIN_HD, axis=1)
        v = pltpu.roll(v, WIN_HD, axis=1)
    return q128, k, v, lo


def _win_ctx_kernel(sink_ref, q_ref, kv_ref, o_ref, *, layer):
    q = q_ref[...]
    k = kv_ref[:, 0:LANE]
    v = kv_ref[:, LANE:2 * LANE]
    lo_mask, hi_mask = _win_masks()
    scale = WIN_HD ** -0.5
    cols = []
    for col in range(2):
        acc = None
        for h in (2 * col, 2 * col + 1):
            q128, kk, vv, lo = _win_head_operands(q, k, v, h)
            s = _dot_nt(q128, kk) * scale
            sink = sink_ref[layer, h]
            m = jnp.maximum(jnp.max(s, -1, keepdims=True), sink)
            p = jnp.exp(s - m)
            den = jnp.sum(p, -1, keepdims=True) + jnp.exp(sink - m)
            o = _dot(p, vv) / den
            o = jnp.where(lo_mask if lo else hi_mask, o, 0.0)
            acc = o if acc is None else acc + o
        cols.append(acc)
    o_ref[...] = jnp.concatenate(cols, 1).astype(o_ref.dtype)


def _win_ctx(z_main, sink, layer):
    return pl.pallas_call(
        functools.partial(_win_ctx_kernel, layer=layer),
        out_shape=jax.ShapeDtypeStruct((T_ALL, WIN_HEADS * WIN_HD), BF16),
        grid=(BATCH,),
        in_specs=[
            pl.BlockSpec(memory_space=pltpu.SMEM),
            pl.BlockSpec((SEQ, 256), lambda b: (b, COL_WQ // 256)),
            pl.BlockSpec((SEQ, 256), lambda b: (b, COL_WK // 256)),
        ],
        out_specs=pl.BlockSpec((SEQ, 256), lambda b: (b, 0)),
        compiler_params=_params("parallel"),
        name="win_ctx",
    )(sink, z_main, z_main)


def _win_lat_kernel(sink_ref, prev_ref, q_ref, kv_ref, ck_ref, cv_ref, c_ref, su_ref, sd_ref,
                    o_ref, *, layer):
    del prev_ref
    L = DEC_SEQ
    half = WIN_HD // 2
    c, su, sd = c_ref[...], su_ref[...], sd_ref[...]
    q = jnp.concatenate(
        [_rope128(q_ref[:, i * LANE:(i + 1) * LANE], c, su, sd, half) for i in range(2)], 1)
    k = _rope128(kv_ref[:, 0:LANE], c, su, sd, half)
    v = kv_ref[:, LANE:2 * LANE]
    ck = ck_ref[...]
    cv = cv_ref[...]
    lo_mask, hi_mask = _win_masks()
    scale = WIN_HD ** -0.5
    nb = L // CHUNK
    cols = []
    for col in range(2):
        acc_blocks = [None] * nb
        for h in (2 * col, 2 * col + 1):
            q128, kk, vv, lo = _win_head_operands(q, k, v, h)
            _, ckk, cvv, _ = _win_head_operands(q, ck, cv, h)
            sink = sink_ref[layer, h]
            for n in range(nb):
                k0 = max(0, (n - 1) * CHUNK)
                k1 = min(L, (n + 2) * CHUNK)
                qn = q128[n * CHUNK:(n + 1) * CHUNK]
                s_loc = _dot_nt(qn, kk[k0:k1]) * scale
                qi = n * CHUNK + lax.broadcasted_iota(jnp.int32, s_loc.shape, 0)
                kj = k0 + lax.broadcasted_iota(jnp.int32, s_loc.shape, 1)
                s_loc = jnp.where(jnp.abs(qi - kj) <= WINDOW, s_loc, NEG)
                s_ctx = _dot_nt(qn, ckk) * scale
                m = jnp.maximum(jnp.maximum(jnp.max(s_loc, -1, keepdims=True),
                                            jnp.max(s_ctx, -1, keepdims=True)), sink)
                p_loc = jnp.exp(s_loc - m)
                p_ctx = jnp.exp(s_ctx - m)
                den = (jnp.sum(p_loc, -1, keepdims=True) + jnp.sum(p_ctx, -1, keepdims=True)
                       + jnp.exp(sink - m))
                o = (_dot(p_loc, vv[k0:k1]) + _dot(p_ctx, cvv)) / den
                o = jnp.where(lo_mask if lo else hi_mask, o, 0.0)
                acc_blocks[n] = o if acc_blocks[n] is None else acc_blocks[n] + o
        cols.append(jnp.concatenate(acc_blocks, 0))
    o_ref[...] = jnp.concatenate(cols, 1).astype(o_ref.dtype)


def _win_lat(z_main, prev_out, sink, cache_k, cache_v, layer):
    c, su, sd, _ = _rope_tables(DEC_SEQ, WIN_HD, LANE)
    rb0 = T_CTX // DEC_SEQ
    tab = pl.BlockSpec((DEC_SEQ, LANE), lambda b: (0, 0))
    cache = pl.BlockSpec((None, None, PAST_LEN, LANE), lambda b: (b, layer, 0, 0))
    return pl.pallas_call(
        functools.partial(_win_lat_kernel, layer=layer),
        out_shape=jax.ShapeDtypeStruct((T_ALL, WIN_HEADS * WIN_HD), BF16),
        grid=(DEC_BATCH,),
        in_specs=[
            pl.BlockSpec(memory_space=pltpu.SMEM),
            pl.BlockSpec(memory_space=pl.ANY),
            pl.BlockSpec((DEC_SEQ, 256), lambda b: (rb0 + b, COL_WQ // 256)),
            pl.BlockSpec((DEC_SEQ, 256), lambda b: (rb0 + b, COL_WK // 256)),
            cache, cache, tab, tab, tab,
        ],
        out_specs=pl.BlockSpec((DEC_SEQ, 256), lambda b: (rb0 + b, 0)),
        input_output_aliases={1: 0},
        compiler_params=_params("parallel"),
        name="win_lat",
    )(sink, prev_out, z_main, z_main, cache_k, cache_v,
      jnp.asarray(c), jnp.asarray(su), jnp.asarray(sd))


def _ret_kernel(*refs, L, layer, ctx):
    if ctx:
        (df_ref, db_ref, q_ref, k_ref, v0_ref, v1_ref, g0_ref, g1_ref,
         o_ref, sf_out, sb_out, s_ref, cross_ref) = refs
    else:
        (df_ref, db_ref, prev_ref, q_ref, k_ref, v0_ref, v1_ref, g0_ref, g1_ref, s0f_ref, s0b_ref,
         o_ref, s_ref, cross_ref) = refs
    C = CHUNK
    nc = L // C
    H = RET_HEADS
    qw = H * RET_DK
    vw = H * RET_DV

    def lane_table(width, per_head, fn):
        pos = lax.broadcasted_iota(jnp.int32, (C, per_head), 0).astype(F32)
        return jnp.concatenate([fn(h, pos) for h in range(H)], 1)

    def log_gamma(ref, h):
        d = jnp.full((1, 1), ref[layer, h], F32)
        return jnp.log(jax.nn.sigmoid(d))

    lgf = [log_gamma(df_ref, h) for h in range(H)]
    lgb = [log_gamma(db_ref, h) for h in range(H)]
    q_all = q_ref[...]
    k_all = k_ref[...] * (RET_DK ** -0.5)
    v_all = jnp.concatenate([v0_ref[...], v1_ref[...]], 1)

    srow = lax.broadcasted_iota(jnp.int32, (qw, vw), 0) // RET_DK
    scol = lax.broadcasted_iota(jnp.int32, (qw, vw), 1) // RET_DV
    diag = srow == scol

    def scan(lg, reverse, s0_ref, s_out):
        if reverse:
            dq = lane_table(vw, RET_DV, lambda h, pos: jnp.exp((C - pos) * lg[h]))
            dk = lane_table(qw, RET_DK, lambda h, pos: jnp.exp(pos * lg[h]))
        else:
            dq = lane_table(vw, RET_DV, lambda h, pos: jnp.exp((pos + 1.0) * lg[h]))
            dk = lane_table(qw, RET_DK, lambda h, pos: jnp.exp((C - 1.0 - pos) * lg[h]))
        dc = jnp.concatenate([jnp.broadcast_to(jnp.exp(C * lg[h]), (1, RET_DV)) for h in range(H)], 1)
        if s0_ref is None:
            s_ref[...] = jnp.zeros((qw, vw), F32)
        else:
            s_ref[...] = jnp.zeros((qw, vw), F32)
            for h in range(H):
                s_ref[h * RET_DK:(h + 1) * RET_DK, h * RET_DV:(h + 1) * RET_DV] = s0_ref[h]
        order = range(nc - 1, -1, -1) if reverse else range(nc)
        for ci in order:
            sl = slice(ci * C, (ci + 1) * C)
            qc, kc, vc = q_all[sl], k_all[sl], v_all[sl]
            s = s_ref[...]
            cross = _dot(qc, s) * dq
            if reverse:
                cross_ref[sl, :] = cross_ref[sl, :] + cross
            else:
                cross_ref[sl, :] = cross
            upd = jnp.where(diag, _dot_tn(kc * dk, vc), 0.0)
            s_ref[...] = s * dc + upd
        if s_out is not None:
            for h in range(H):
                s_out[h] = s_ref[h * RET_DK:(h + 1) * RET_DK, h * RET_DV:(h + 1) * RET_DV]

    scan(lgf, False, None if ctx else s0f_ref, sf_out if ctx else None)
    scan(lgb, True, None if ctx else s0b_ref, sb_out if ctx else None)

    ii = lax.broadcasted_iota(jnp.int32, (C, C), 0)
    jj = lax.broadcasted_iota(jnp.int32, (C, C), 1)
    diff = (ii - jj).astype(F32)
    lane_q = lax.broadcasted_iota(jnp.int32, (1, qw), 1) // RET_DK
    g_all = jnp.concatenate([g0_ref[...], g1_ref[...]], 1)
    for h in range(H):
        dmat = (jnp.where(diff >= 0, jnp.exp(jnp.maximum(diff, 0.0) * lgf[h]), 0.0)
                + jnp.where(diff <= 0, jnp.exp(jnp.maximum(-diff, 0.0) * lgb[h]), 0.0))
        hv = slice(h * RET_DV, (h + 1) * RET_DV)
        for ci in range(nc):
            sl = slice(ci * C, (ci + 1) * C)
            qh = jnp.where(lane_q == h, q_all[sl], 0.0)
            att = _dot_nt(qh, k_all[sl]) * dmat
            o = _dot(att, v_all[sl, hv]) + cross_ref[sl, hv]
            g = g_all[sl, hv]
            o_ref[sl, hv] = ((g * jax.nn.sigmoid(g)) * _layer_norm(o)).astype(o_ref.dtype)


def _retention(z_main, prev_out, dec_f, dec_b, s0f, s0b, layer, ctx):
    L = SEQ if ctx else DEC_SEQ
    n_seq = BATCH if ctx else DEC_BATCH
    rb0 = 0 if ctx else T_CTX // DEC_SEQ

    def zcol(col):
        return pl.BlockSpec((L, 256), lambda b: (rb0 + b, col // 256))

    smem = pl.BlockSpec(memory_space=pltpu.SMEM)
    z_specs = [zcol(COL_RQ), zcol(COL_RK), zcol(COL_RV), zcol(COL_RV + 256),
               zcol(COL_RG), zcol(COL_RG + 256)]
    y_shape = jax.ShapeDtypeStruct((T_ALL, RET_HEADS * RET_DV), BF16)
    y_spec = pl.BlockSpec((L, RET_HEADS * RET_DV), lambda b: (rb0 + b, 0))
    scratch = [pltpu.VMEM((RET_HEADS * RET_DK, RET_HEADS * RET_DV), F32),
               pltpu.VMEM((L, RET_HEADS * RET_DV), F32)]
    kern = functools.partial(_ret_kernel, L=L, layer=layer, ctx=ctx)
    if ctx:
        st_shape = jax.ShapeDtypeStruct((BATCH, RET_HEADS, RET_DK, RET_DV), F32)
        st_spec = pl.BlockSpec((None, RET_HEADS, RET_DK, RET_DV), lambda b: (b, 0, 0, 0))
        return pl.pallas_call(
            kern,
            out_shape=(y_shape, st_shape, st_shape),
            grid=(n_seq,),
            in_specs=[smem, smem] + z_specs,
            out_specs=(y_spec, st_spec, st_spec),
            scratch_shapes=scratch,
            compiler_params=_params("parallel"),
            name="ret_ctx",
        )(dec_f, dec_b, *([z_main] * 6))
    s0_spec = pl.BlockSpec((None, None, RET_HEADS, RET_DK, RET_DV), lambda b: (b, layer, 0, 0, 0))
    return pl.pallas_call(
        kern,
        out_shape=y_shape,
        grid=(n_seq,),
        in_specs=[smem, smem, pl.BlockSpec(memory_space=pl.ANY)] + z_specs + [s0_spec, s0_spec],
        out_specs=y_spec,
        scratch_shapes=scratch,
        input_output_aliases={2: 0},
        compiler_params=_params("parallel"),
        name="ret_lat",
    )(dec_f, dec_b, prev_out, *([z_main] * 6), s0f, s0b)


def _rms_norm(x, g):
    return x * lax.rsqrt(jnp.mean(x * x, -1, keepdims=True) + RMS_EPS) * g


def _mla_attend(qn, qr, kn, kr, vv, o_ref, row0):
    scale = (MLA_NOPE + MLA_ROPE) ** -0.5
    lane_n = lax.broadcasted_iota(jnp.int32, (1, MLA_HEADS * MLA_NOPE), 1) // MLA_NOPE
    lane_r = lax.broadcasted_iota(jnp.int32, (1, LANE), 1)
    kr32 = jnp.where(lane_r < MLA_ROPE, kr, 0.0)
    acc = None
    for h in range(MLA_HEADS):
        qnh = jnp.where(lane_n == h, qn, 0.0)
        qrh = qr if h == 0 else pltpu.roll(qr, LANE - h * MLA_ROPE, axis=1)
        qrh = jnp.where(lane_r < MLA_ROPE, qrh, 0.0)
        s = (_dot_nt(qnh, kn) + _dot_nt(qrh, kr32)) * scale
        m = jnp.max(s, -1, keepdims=True)
        p = jnp.exp(s - m)
        den = jnp.sum(p, -1, keepdims=True)
        o = jnp.where(lane_n == h, _dot(p, vv) / den, 0.0)
        acc = o if acc is None else acc + o
    o_ref[row0:row0 + acc.shape[0], :] = acc.astype(o_ref.dtype)


def _mla_ctx_kernel(cq_ref, ckv_ref, kr_ref, qg_ref, kg_ref, wqn_ref, wqr_ref, wk_ref, wv_ref,
                    o_ref, ckvn_ref):
    cqn = _rms_norm(cq_ref[...], qg_ref[...])
    qn = _dot(cqn, wqn_ref[...])
    qr = _dot(cqn, wqr_ref[...])
    ckvn = _rms_norm(ckv_ref[...], kg_ref[...])
    ckvn_ref[...] = ckvn
    kn = _dot(ckvn, wk_ref[...])
    vv = _dot(ckvn, wv_ref[...])
    _mla_attend(qn, qr, kn, kr_ref[...], vv, o_ref, 0)


def _mla_weight_specs(layer):
    return [
        pl.BlockSpec((None, 1, MLA_Q_LORA), lambda b: (layer, 0, 0)),
        pl.BlockSpec((None, 1, MLA_KV_LORA), lambda b: (layer, 0, 0)),
        pl.BlockSpec((None, MLA_Q_LORA, MLA_HEADS * MLA_NOPE), lambda b: (layer, 0, 0)),
        pl.BlockSpec((None, MLA_Q_LORA, MLA_HEADS * MLA_ROPE), lambda b: (layer, 0, 0)),
        pl.BlockSpec((None, MLA_KV_LORA, MLA_HEADS * MLA_NOPE), lambda b: (layer, 0, 0)),
        pl.BlockSpec((None, MLA_KV_LORA, MLA_HEADS * MLA_V), lambda b: (layer, 0, 0)),
    ]


def _mla_ctx(z_main, weights, layer):
    return pl.pallas_call(
        _mla_ctx_kernel,
        out_shape=(jax.ShapeDtypeStruct((T_ALL, MLA_HEADS * MLA_V), BF16),
                   jax.ShapeDtypeStruct((T_CTX, MLA_KV_LORA), F32)),
        grid=(BATCH,),
        in_specs=[
            pl.BlockSpec((SEQ, 256), lambda b: (b, COL_CQ // 256)),
            pl.BlockSpec((SEQ, LANE), lambda b: (b, COL_CKV // LANE)),
            pl.BlockSpec((SEQ, LANE), lambda b: (b, COL_KROPE // LANE)),
        ] + _mla_weight_specs(layer),
        out_specs=(pl.BlockSpec((SEQ, 256), lambda b: (b, 0)),
                   pl.BlockSpec((SEQ, MLA_KV_LORA), lambda b: (b, 0))),
        compiler_params=_params("parallel"),
        name="mla_ctx",
    )(z_main, z_main, z_main, *weights)


def _mla_lat_kernel(prev_ref, cq_ref, ckv_ref, kr_ref, cckv_ref, ckr_ref, c_ref, su_ref, sd_ref,
                    qg_ref, kg_ref, wqn_ref, wqr_ref, wk_ref, wv_ref, o_ref):
    del prev_ref
    half = MLA_ROPE // 2
    c, su, sd = c_ref[...], su_ref[...], sd_ref[...]
    cqn = _rms_norm(cq_ref[...], qg_ref[...])
    qn = _dot(cqn, wqn_ref[...])
    qr = _rope128(_dot(cqn, wqr_ref[...]), c, su, sd, half)
    ckvn = _rms_norm(ckv_ref[...], kg_ref[...])
    ckv_all = jnp.concatenate([ckvn, cckv_ref[...]], 0)
    kn = _dot(ckv_all, wk_ref[...])
    vv = _dot(ckv_all, wv_ref[...])
    kr = jnp.concatenate([_rope128(kr_ref[...], c, su, sd, half), ckr_ref[...]], 0)
    for n in range(DEC_SEQ // 256):
        rows = slice(n * 256, (n + 1) * 256)
        _mla_attend(qn[rows], qr[rows], kn, kr, vv, o_ref, n * 256)


def _mla_lat(z_main, prev_out, cache_ckv, cache_kr_pad, weights, layer):
    c, su, sd, _ = _rope_tables(DEC_SEQ, MLA_ROPE, LANE)
    rb0 = T_CTX // DEC_SEQ
    tab = pl.BlockSpec((DEC_SEQ, LANE), lambda b: (0, 0))
    cache = pl.BlockSpec((None, None, PAST_LEN, LANE), lambda b: (b, layer, 0, 0))
    return pl.pallas_call(
        _mla_lat_kernel,
        out_shape=jax.ShapeDtypeStruct((T_ALL, MLA_HEADS * MLA_V), BF16),
        grid=(DEC_BATCH,),
        in_specs=[
            pl.BlockSpec(memory_space=pl.ANY),
            pl.BlockSpec((DEC_SEQ, 256), lambda b: (rb0 + b, COL_CQ // 256)),
            pl.BlockSpec((DEC_SEQ, LANE), lambda b: (rb0 + b, COL_CKV // LANE)),
            pl.BlockSpec((DEC_SEQ, LANE), lambda b: (rb0 + b, COL_KROPE // LANE)),
            cache, cache, tab, tab, tab,
        ] + _mla_weight_specs(layer),
        out_specs=pl.BlockSpec((DEC_SEQ, 256), lambda b: (rb0 + b, 0)),
        input_output_aliases={0: 0},
        compiler_params=_params("parallel"),
        name="mla_lat",
    )(prev_out, z_main, z_main, z_main, cache_ckv, cache_kr_pad,
      jnp.asarray(c), jnp.asarray(su), jnp.asarray(sd), *weights)


def _route(logits_t, rb):
    scores = jax.nn.sigmoid(logits_t)
    biased = scores + rb
    sc = [scores[e:e + 1, :] for e in range(N_EXPERTS)]
    bi = [biased[e:e + 1, :] for e in range(N_EXPERTS)]
    epg = EXPERTS_PER_GROUP
    gsum = []
    for g in range(N_GROUPS):
        v = bi[g * epg:(g + 1) * epg]
        best = None
        for a in range(epg):
            for b in range(a + 1, epg):
                pair = v[a] + v[b]
                best = pair if best is None else jnp.maximum(best, pair)
        gsum.append(best)
    combine = []
    sel = []
    for g in range(N_GROUPS):
        is_best = None
        for g2 in range(N_GROUPS):
            if g2 == g:
                continue
            c = gsum[g] > gsum[g2] if g2 < g else gsum[g] >= gsum[g2]
            is_best = c if is_best is None else jnp.logical_and(is_best, c)
        for a in range(epg):
            e = g * epg + a
            rank = jnp.zeros_like(bi[e])
            for b in range(epg):
                if b == a:
                    continue
                e2 = g * epg + b
                ahead = bi[e2] >= bi[e] if b < a else bi[e2] > bi[e]
                rank = rank + jnp.where(ahead, 1.0, 0.0)
            sel.append(jnp.logical_and(is_best, rank < 2.0))
    wsum = None
    for e in range(N_EXPERTS):
        w = jnp.where(sel[e], sc[e], 0.0)
        wsum = w if wsum is None else wsum + w
    for e in range(N_EXPERTS):
        combine.append(jnp.where(sel[e], ROUTE_SCALE * sc[e] / wsum, 0.0))
    return jnp.concatenate(combine, 0)


def _merge_kernel(ya_ref, yb_ref, yc_ref, yd_ref, gt_ref, xc_ref, xl_ref, m_ref,
                  wa_ref, wb_ref, wc_ref, wd_ref, wo_ref, g_ref, b_ref, rw_ref, rb_ref,
                  x1_ref, h2_ref, cmb_ref, *, ctx_tiles):
    D = D_MODEL
    merged = None
    for i, (y_ref, w) in enumerate(((ya_ref, wa_ref), (yb_ref, wb_ref), (yc_ref, wc_ref), (yd_ref, wd_ref))):
        t = gt_ref[:, i * D:(i + 1) * D].astype(F32) * jnp.dot(
            y_ref[...], w[...], preferred_element_type=F32)
        merged = t if merged is None else merged + t
    out1 = jnp.dot(merged.astype(BF16), wo_ref[...], preferred_element_type=F32)
    g1 = m_ref[:, 2 * D:3 * D]
    s2 = m_ref[:, 3 * D:4 * D]
    sc2 = m_ref[:, 4 * D:5 * D]
    x = jnp.where(pl.program_id(0) < ctx_tiles, xc_ref[...], xl_ref[...])
    x1 = _layer_norm(ALPHA * x + g1 * out1) * g_ref[...] + b_ref[...]
    x1_ref[...] = x1
    h2 = _layer_norm(x1) * (1.0 + sc2) + s2
    h2_ref[...] = h2.astype(h2_ref.dtype)
    logits = jnp.dot(h2, rw_ref[...], precision=HIGHEST, preferred_element_type=F32)
    cmb_ref[...] = _route(logits.T[0:N_EXPERTS], rb_ref[...])


def _merge(ya, yb, yc, yd, gates, x_ctx, x_lat, mods_l, w_br, w_out_bf, ln1_g, ln1_b,
           router_w_pad, router_b, layer):
    tm = 512
    row = _mod_row(tm)
    D = D_MODEL
    ctx_tiles = T_CTX // tm

    def tile(w):
        return pl.BlockSpec((tm, w), lambda i: (i, 0))

    def weight(k, n):
        return pl.BlockSpec((None, k, n), lambda i: (layer, 0, 0))

    return pl.pallas_call(
        functools.partial(_merge_kernel, ctx_tiles=ctx_tiles),
        out_shape=(jax.ShapeDtypeStruct((T_ALL, D), F32),
                   jax.ShapeDtypeStruct((T_ALL, D), BF16),
                   jax.ShapeDtypeStruct((N_EXPERTS, T_ALL), F32)),
        grid=(T_ALL // tm,),
        in_specs=[
            tile(256), tile(256), tile(512), tile(256), tile(4 * D),
            pl.BlockSpec((tm, D), lambda i: (jnp.minimum(i, ctx_tiles - 1), 0)),
            pl.BlockSpec((tm, D), lambda i: (jnp.maximum(i - ctx_tiles, 0), 0)),
            pl.BlockSpec((None, 1, 6 * D), lambda i: (row(i), 0, 0)),
            weight(256, D), weight(256, D), weight(512, D), weight(256, D), weight(D, D),
            weight(1, D), weight(1, D),
            pl.BlockSpec((D, LANE), lambda i: (0, 0)),
            pl.BlockSpec((N_EXPERTS, 1), lambda i: (0, 0)),
        ],
        out_specs=(tile(D), tile(D), pl.BlockSpec((N_EXPERTS, tm), lambda i: (0, i))),
        compiler_params=_params("parallel"),
        name="merge",
    )(ya, yb, yc, yd, gates, x_ctx, x_lat, mods_l, *w_br, w_out_bf,
      ln1_g.reshape(DEPTH, 1, D), ln1_b.reshape(DEPTH, 1, D), router_w_pad,
      router_b.reshape(N_EXPERTS, 1))


def _moe_kernel(h_ref, c_ref, x1_ref, m_ref, wg_ref, wu_ref, wd_ref, g_ref, b_ref, o_ref, acc_ref):
    e = pl.program_id(1)

    @pl.when(e == 0)
    def _():
        acc_ref[...] = jnp.zeros_like(acc_ref)

    h = h_ref[...]
    gate = jnp.dot(h, wg_ref[...].astype(BF16), preferred_element_type=F32)
    up = jnp.dot(h, wu_ref[...].astype(BF16), preferred_element_type=F32)
    cmb = c_ref[...]
    lane = lax.broadcasted_iota(jnp.int32, cmb.shape, 1)
    ce = jnp.sum(jnp.where(lane == e, cmb, 0.0), -1, keepdims=True)
    hid = (gate * jax.nn.sigmoid(gate)) * up * ce
    acc_ref[...] += jnp.dot(hid.astype(BF16), wd_ref[...].astype(BF16), preferred_element_type=F32)

    @pl.when(e == N_EXPERTS - 1)
    def _():
        g2 = m_ref[:, 5 * D_MODEL:6 * D_MODEL]
        y = _layer_norm(ALPHA * x1_ref[...] + g2 * acc_ref[...])
        o_ref[...] = y * g_ref[...] + b_ref[...]


def _moe(h2, combine, x1, mods_l, w_gate, w_up, w_down, ln2_g, ln2_b, layer, row0, n_rows):
    tm = 1024
    row = _mod_row(tm)
    D = D_MODEL
    t0 = row0 // tm
    return pl.pallas_call(
        _moe_kernel,
        out_shape=jax.ShapeDtypeStruct((n_rows, D), F32),
        grid=(n_rows // tm, N_EXPERTS),
        in_specs=[
            pl.BlockSpec((tm, D), lambda i, e: (t0 + i, 0)),
            pl.BlockSpec((tm, N_EXPERTS), lambda i, e: (t0 + i, 0)),
            pl.BlockSpec((tm, D), lambda i, e: (t0 + i, 0)),
            pl.BlockSpec((None, 1, 6 * D), lambda i, e: (row(t0 + i), 0, 0)),
            pl.BlockSpec((None, None, D, D_EXPERT), lambda i, e: (layer, e, 0, 0)),
            pl.BlockSpec((None, None, D, D_EXPERT), lambda i, e: (layer, e, 0, 0)),
            pl.BlockSpec((None, None, D_EXPERT, D), lambda i, e: (layer, e, 0, 0)),
            pl.BlockSpec((None, 1, D), lambda i, e: (layer, 0, 0)),
            pl.BlockSpec((None, 1, D), lambda i, e: (layer, 0, 0)),
        ],
        out_specs=pl.BlockSpec((tm, D), lambda i, e: (i, 0)),
        scratch_shapes=[pltpu.VMEM((tm, D), F32)],
        compiler_params=_params("parallel", "arbitrary"),
        name="moe",
    )(h2, combine, x1, mods_l, w_gate, w_up, w_down,
      ln2_g.reshape(DEPTH, 1, D), ln2_b.reshape(DEPTH, 1, D))


def kernel(x_prompt, x_sample, cache_win_k, cache_win_v, cache_mla_ckv, cache_mla_krope,
           state_ret_fwd, state_ret_bwd, c, c_ctx, w_ada, b_ada, w_in,
           hy_conv_w, hy_conv_b, hy_w1, hy_b1, hy_w2, hy_b2, hy_w3, hy_bias,
           win_sink, ret_decay_fwd, ret_decay_bwd, mla_q_norm, mla_kv_norm, mla_w_uq, mla_w_ukv,
           w_br_a, w_br_b, w_br_c, w_br_d, w_out, ln1_g, ln1_b, ln2_g, ln2_b,
           router_w, router_b, moe_w_gate, moe_w_up, moe_w_down):
    D = D_MODEL
    x_ctx = x_prompt.reshape(T_CTX, D)
    x_lat = x_sample.reshape(T_LAT, D)

    cvec = jnp.zeros((8, D), F32).at[0].set(c_ctx).at[1:1 + DEC_BATCH].set(c)
    mods = _ada_mods(cvec, w_ada, b_ada)[:, :1 + DEC_BATCH].reshape(DEPTH, 1 + DEC_BATCH, 1, 6 * D)

    w_in_t = jnp.swapaxes(w_in, 1, 2).reshape(DEPTH * IN_COLS, D)
    cache_k = cache_win_k.reshape(DEC_BATCH, DEPTH, PAST_LEN, WIN_KV_HEADS * WIN_HD)
    cache_v = cache_win_v.reshape(DEC_BATCH, DEPTH, PAST_LEN, WIN_KV_HEADS * WIN_HD)
    cache_kr = jnp.pad(cache_mla_krope, ((0, 0), (0, 0), (0, 0), (0, LANE - MLA_ROPE)))

    uq = mla_w_uq.reshape(DEPTH, MLA_Q_LORA, MLA_HEADS, MLA_NOPE + MLA_ROPE)
    ukv = mla_w_ukv.reshape(DEPTH, MLA_KV_LORA, MLA_HEADS, MLA_NOPE + MLA_V)
    mla_weights = (
        mla_q_norm.reshape(DEPTH, 1, MLA_Q_LORA),
        mla_kv_norm.reshape(DEPTH, 1, MLA_KV_LORA),
        uq[..., :MLA_NOPE].reshape(DEPTH, MLA_Q_LORA, MLA_HEADS * MLA_NOPE),
        uq[..., MLA_NOPE:].reshape(DEPTH, MLA_Q_LORA, MLA_HEADS * MLA_ROPE),
        ukv[..., :MLA_NOPE].reshape(DEPTH, MLA_KV_LORA, MLA_HEADS * MLA_NOPE),
        ukv[..., MLA_NOPE:].reshape(DEPTH, MLA_KV_LORA, MLA_HEADS * MLA_V),
    )

    hy_w1p = jnp.pad(hy_w1, ((0, 0), (0, LANE - HY_EMB), (0, 0)))
    dft = {}
    for L in (SEQ, DEC_SEQ):
        fwd, inv = _dft_tables(L)
        dft[L] = (jnp.asarray(fwd).astype(BF16), jnp.asarray(inv).astype(BF16))
    router_w_pad = jnp.pad(router_w, ((0, 0), (0, LANE - N_EXPERTS)))
    w_br = tuple(w.astype(BF16) for w in (w_br_a, w_br_b, w_br_c, w_br_d))
    w_out_bf = w_out.astype(BF16)

    new_k, new_v, new_ckv, new_kr, new_sf, new_sb = [], [], [], [], [], []
    for l in range(DEPTH):
        mods_l = mods[l]
        h = _ln_mod(x_ctx, None, mods_l, 0)
        h = _ln_mod(x_lat, h, mods_l, T_CTX)
        z = _in_proj(h, w_in_t, l, 0, Z_MAIN, Z_MAIN // 2, F32, gate=False)
        gates = _in_proj(h, w_in_t, l, COL_GATE, 4 * D, D, BF16, gate=True)

        ya = None
        for L, n_seq, rb0 in ((SEQ, BATCH, 0), (DEC_SEQ, DEC_BATCH, T_CTX // DEC_SEQ)):
            fwd, inv = dft[L]
            kre, kim = _hy_filters(L, hy_w1p[l], hy_b1[l][None], hy_w2[l], hy_b2[l][None], hy_w3[l], fwd)
            ya = _hyena(z, ya, l, L, n_seq, rb0, hy_conv_w, hy_conv_b, hy_bias, kre, kim, fwd, inv)

        yb = _win_ctx(z, win_sink, l)
        yb = _win_lat(z, yb, win_sink, cache_k, cache_v, l)

        yc, sf, sb = _retention(z, None, ret_decay_fwd, ret_decay_bwd, None, None, l, ctx=True)
        yc = _retention(z, yc, ret_decay_fwd, ret_decay_bwd, state_ret_fwd, state_ret_bwd, l, ctx=False)

        yd, ckvn = _mla_ctx(z, mla_weights, l)
        yd = _mla_lat(z, yd, cache_mla_ckv, cache_kr, mla_weights, l)

        x1, h2, combine_t = _merge(ya, yb, yc, yd, gates, x_ctx, x_lat, mods_l, w_br, w_out_bf,
                                   ln1_g, ln1_b, router_w_pad, router_b, l)
        moe_args = (h2, combine_t.T, x1, mods_l, moe_w_gate, moe_w_up, moe_w_down, ln2_g, ln2_b, l)
        x_ctx = _moe(*moe_args, 0, T_CTX)
        x_lat = _moe(*moe_args, T_CTX, T_LAT)

        zc = z[:T_CTX]
        new_k.append(zc[:, COL_WK:COL_WK + 128].reshape(BATCH, SEQ, WIN_KV_HEADS, WIN_HD))
        new_v.append(zc[:, COL_WV:COL_WV + 128].reshape(BATCH, SEQ, WIN_KV_HEADS, WIN_HD))
        new_ckv.append(ckvn.reshape(BATCH, SEQ, MLA_KV_LORA))
        new_kr.append(zc[:, COL_KROPE:COL_KROPE + MLA_ROPE].reshape(BATCH, SEQ, MLA_ROPE))
        new_sf.append(sf)
        new_sb.append(sb)

    y_prompt = x_ctx.reshape(BATCH, SEQ, D)
    y_sample = x_lat.reshape(DEC_BATCH, DEC_SEQ, D)
    return (y_prompt, y_sample, jnp.stack(new_k, 1), jnp.stack(new_v, 1), jnp.stack(new_ckv, 1),
            jnp.stack(new_kr, 1), jnp.stack(new_sf, 1), jnp.stack(new_sb, 1))
```

```python
import functools
import math

import numpy as np
import jax
import jax.numpy as jnp
from jax import lax
from jax.experimental import pallas as pl
from jax.experimental.pallas import tpu as pltpu

F32 = jnp.float32
BF16 = jnp.bfloat16
HIGHEST = lax.Precision.HIGHEST

D_MODEL = 1024
BATCH = 16
SEQ = 256
DEPTH = 2
DEC_BATCH = 2
DEC_SEQ = 1024
PAST_LEN = 256
GRID_W = 64
CHUNK = 128
ROPE_BASE = 10000.0
NEG = -1e30
LN_EPS = 1e-5
RMS_EPS = 1e-6

HY_W = 256
HY_BANDS = 16
HY_EMB = 1 + 2 * HY_BANDS
HY_FFN = 64
HY_FAST_DECAY = 0.3
HY_SLOW_DECAY = 1.5
HY_TARGET = 1e-2

WIN_HEADS = 4
WIN_KV_HEADS = 2
WIN_HD = 64
WINDOW = 128

RET_HEADS = 4
RET_DK = 64
RET_DV = 128

MLA_HEADS = 4
MLA_Q_LORA = 256
MLA_KV_LORA = 128
MLA_NOPE = 64
MLA_ROPE = 32
MLA_V = 64

N_EXPERTS = 16
N_GROUPS = 4
EXPERTS_PER_GROUP = N_EXPERTS // N_GROUPS
D_EXPERT = 256
ROUTE_SCALE = 2.5

ALPHA = (2.0 * DEPTH) ** 0.25

T_CTX = BATCH * SEQ
T_LAT = DEC_BATCH * DEC_SEQ
T_ALL = T_CTX + T_LAT

COL_HY = 0
COL_WQ = 768
COL_WK = 1024
COL_WV = 1152
COL_RQ = 1280
COL_RK = 1536
COL_RV = 1792
COL_RG = 2304
COL_CQ = 2816
COL_CKV = 3072
COL_KROPE = 3200
COL_GATE = 3232
IN_COLS = COL_GATE + 4 * D_MODEL
Z_MAIN = 3328

LANE = 128
VMEM_LIMIT = 56 * 1024 * 1024


def _params(*sem):
    return pltpu.CompilerParams(dimension_semantics=sem, vmem_limit_bytes=VMEM_LIMIT)


def _dot(a, b):
    return jnp.dot(a.astype(BF16), b.astype(BF16), preferred_element_type=F32)


def _dot_nt(a, b):
    return lax.dot_general(a.astype(BF16), b.astype(BF16), (((1,), (1,)), ((), ())),
                           preferred_element_type=F32)


def _dot_tn(a, b):
    return lax.dot_general(a.astype(BF16), b.astype(BF16), (((0,), (0,)), ((), ())),
                           preferred_element_type=F32)


def _layer_norm(x):
    mu = jnp.mean(x, -1, keepdims=True)
    xc = x - mu
    var = jnp.mean(xc * xc, -1, keepdims=True)
    return xc * lax.rsqrt(var + LN_EPS)


def _mod_row(tile_rows):
    def row(i):
        start = i * tile_rows
        return jnp.where(start < T_CTX, 0, 1 + (start - T_CTX) // DEC_SEQ)
    return row


@functools.lru_cache(maxsize=None)
def _dft_tables(L):
    f = np.arange(L, dtype=np.int64)[:, None]
    s = np.arange(L, dtype=np.int64)[None, :]
    ang = np.pi * ((f * s) % (2 * L)).astype(np.float64) / L
    cos = np.cos(ang)
    sin = np.sin(ang)
    alt = np.where(np.arange(L) % 2 == 0, 1.0, -1.0)
    fwd_im = -sin
    fwd_im[0, :] = alt
    fwd = np.concatenate([cos, fwd_im], 0)
    inv_re = cos.T / L
    inv_re[:, 0] = 1.0 / (2 * L)
    inv_im = -sin.T / L
    inv_im[:, 0] = alt / (2 * L)
    inv = np.concatenate([inv_re, inv_im], 1)
    return fwd.astype(np.float32), inv.astype(np.float32)


@functools.lru_cache(maxsize=None)
def _hyena_embedding(L):
    t01 = np.linspace(0.0, 1.0, L, dtype=np.float64)[:, None]
    bands = np.linspace(1e-4, HY_BANDS - 1, HY_BANDS, dtype=np.float64)
    ang = (2.0 * math.pi / L) * np.arange(L, dtype=np.float64)[:, None] * bands[None, :]
    z = np.concatenate([t01, np.cos(ang), -np.sin(ang)], -1)
    zp = np.zeros((L, LANE), np.float64)
    zp[:, :HY_EMB] = z
    deltas = np.abs(np.linspace(math.log(HY_TARGET) / HY_SLOW_DECAY,
                                math.log(HY_TARGET) / HY_FAST_DECAY, HY_W, dtype=np.float64))
    return zp.astype(np.float32), deltas[None, :].astype(np.float32)


@functools.lru_cache(maxsize=None)
def _rope_tables(L, rot_dim, width):
    rows = L // GRID_W
    n_freq = rot_dim // 4
    half = rot_dim // 2
    inv = ROPE_BASE ** (-np.arange(n_freq, dtype=np.float64) / n_freq)
    pos = np.arange(L)
    row = (pos // GRID_W).astype(np.float64)
    col = (pos % GRID_W).astype(np.float64)
    ang = np.concatenate([row[:, None] * inv, col[:, None] * inv], -1)
    cos, sin = np.cos(ang), np.sin(ang)
    zero = np.zeros_like(sin)
    c = np.tile(np.concatenate([cos, cos], -1), (1, width // rot_dim))
    s_up = np.tile(np.concatenate([-sin, zero], -1), (1, width // rot_dim))
    s_dn = np.tile(np.concatenate([zero, sin], -1), (1, width // rot_dim))
    return c.astype(np.float32), s_up.astype(np.float32), s_dn.astype(np.float32), half


def _rope128(x, c, s_up, s_dn, half):
    up = pltpu.roll(x, LANE - half, axis=1)
    dn = pltpu.roll(x, half, axis=1)
    return x * c + up * s_up + dn * s_dn


def _ada_kernel(c_ref, w_ref, b_ref, o_ref):
    cv = c_ref[...]
    s = cv * jax.nn.sigmoid(cv)
    o_ref[...] = jnp.dot(s, w_ref[...], precision=HIGHEST, preferred_element_type=F32) + b_ref[...]


def _ada_mods(cvec, w_ada, b_ada):
    tn = 1536
    n = 6 * D_MODEL
    return pl.pallas_call(
        _ada_kernel,
        out_shape=jax.ShapeDtypeStruct((DEPTH, 8, n), F32),
        grid=(DEPTH, n // tn),
        in_specs=[
            pl.BlockSpec((8, D_MODEL), lambda l, j: (0, 0)),
            pl.BlockSpec((None, D_MODEL, tn), lambda l, j: (l, 0, j)),
            pl.BlockSpec((None, 1, tn), lambda l, j: (l, 0, j)),
        ],
        out_specs=pl.BlockSpec((None, 8, tn), lambda l, j: (l, 0, j)),
        compiler_params=_params("parallel", "parallel"),
        name="ada_mods",
    )(cvec, w_ada, b_ada.reshape(DEPTH, 1, n))


def _lnmod_kernel(*refs):
    x_ref, m_ref, h_ref = refs[-3:]
    y = _layer_norm(x_ref[...])
    s1 = m_ref[:, 0:D_MODEL]
    sc1 = m_ref[:, D_MODEL:2 * D_MODEL]
    h_ref[...] = (y * (1.0 + sc1) + s1).astype(h_ref.dtype)


def _ln_mod(x_group, prev_out, mods_l, row0):
    tm = 512
    row = _mod_row(tm)
    tile0 = row0 // tm
    in_specs = [
        pl.BlockSpec((tm, D_MODEL), lambda i: (i, 0)),
        pl.BlockSpec((None, 1, 6 * D_MODEL), lambda i: (row(tile0 + i), 0, 0)),
    ]
    args = [x_group, mods_l]
    if prev_out is not None:
        in_specs = [pl.BlockSpec(memory_space=pl.ANY)] + in_specs
        args = [prev_out] + args
    return pl.pallas_call(
        _lnmod_kernel,
        out_shape=jax.ShapeDtypeStruct((T_ALL, D_MODEL), BF16),
        grid=(x_group.shape[0] // tm,),
        in_specs=in_specs,
        out_specs=pl.BlockSpec((tm, D_MODEL), lambda i: (tile0 + i, 0)),
        input_output_aliases={} if prev_out is None else {0: 0},
        compiler_params=_params("parallel"),
        name="ln_mod",
    )(*args)


def _proj_kernel(h_ref, w_ref, o_ref, wb_ref, *, gate):
    @pl.when(pl.program_id(1) == 0)
    def _():
        wb_ref[...] = w_ref[...].T.astype(BF16)

    if not gate:
        o_ref[...] = jnp.dot(h_ref[...], wb_ref[...], preferred_element_type=F32).astype(o_ref.dtype)
        return
    sub = 2 * LANE
    for c0 in range(0, o_ref.shape[1], sub):
        r = jnp.dot(h_ref[...], wb_ref[:, c0:c0 + sub], preferred_element_type=F32)
        o_ref[:, c0:c0 + sub] = (0.5 * jnp.tanh(0.5 * r) + 0.5).astype(o_ref.dtype)


def _in_proj(h, w_t, layer, col0, n_cols, tn, out_dtype, gate):
    tm = 1024
    return pl.pallas_call(
        functools.partial(_proj_kernel, gate=gate),
        out_shape=jax.ShapeDtypeStruct((T_ALL, n_cols), out_dtype),
        grid=(n_cols // tn, T_ALL // tm),
        in_specs=[
            pl.BlockSpec((tm, D_MODEL), lambda j, i: (i, 0)),
            pl.BlockSpec((pl.Element(tn), pl.Element(D_MODEL)),
                         lambda j, i: (pl.multiple_of(layer * IN_COLS + col0 + j * tn, 8), 0)),
        ],
        out_specs=pl.BlockSpec((tm, tn), lambda j, i: (i, j)),
        scratch_shapes=[pltpu.VMEM((D_MODEL, tn), BF16)],
        compiler_params=_params("parallel", "arbitrary"),
        name="gate_proj" if gate else "in_proj",
    )(h, w_t)


def _hy_filter_kernel(z_ref, dl_ref, w1_ref, b1_ref, w2_ref, b2_ref, w3_ref, fwd_ref,
                      kre_ref, kim_ref, *, L):
    z = z_ref[...]
    a = jnp.sin(jnp.dot(z, w1_ref[...], precision=HIGHEST, preferred_element_type=F32) + b1_ref[...])
    a = jnp.sin(jnp.dot(a, w2_ref[...], precision=HIGHEST, preferred_element_type=F32) + b2_ref[...])
    h = jnp.dot(a, w3_ref[...], precision=HIGHEST, preferred_element_type=F32)
    decay = jnp.exp(-z[:, 0:1] * dl_ref[...])
    not_first = lax.broadcasted_iota(jnp.int32, (L, HY_W), 0) > 0
    sums, diffs = [], []
    for o in range(2):
        fw = h[:, (2 * o) * HY_W:(2 * o + 1) * HY_W] * decay
        bw = jnp.where(not_first, h[:, (2 * o + 1) * HY_W:(2 * o + 2) * HY_W] * decay, 0.0)
        sums.append(fw + bw)
        diffs.append(fw - bw)
    p = _dot(fwd_ref[...], jnp.concatenate(sums, 1))
    q = _dot(fwd_ref[L:2 * L, :], jnp.concatenate(diffs, 1))
    kre_ref[...] = p[0:L]
    first = lax.broadcasted_iota(jnp.int32, (L, 2 * HY_W), 0) == 0
    kim_ref[...] = jnp.where(first, p[L:L + 1], q)


def _hy_filters(L, w1p, b1, w2, b2, w3, fwd):
    zemb, deltas = _hyena_embedding(L)
    out = jax.ShapeDtypeStruct((L, 2 * HY_W), F32)
    return pl.pallas_call(
        functools.partial(_hy_filter_kernel, L=L),
        out_shape=(out, out),
        compiler_params=pltpu.CompilerParams(vmem_limit_bytes=VMEM_LIMIT),
        name=f"hy_filters_{L}",
    )(jnp.asarray(zemb), jnp.asarray(deltas), w1p, b1, w2, b2, w3, fwd)


def _hyena_kernel(*refs, L, aliased):
    if aliased:
        refs = refs[1:]
    hy_ref, cw_ref, cb_ref, bias_ref, kre_ref, kim_ref, fwd_ref, inv_ref, o_ref = refs
    x = hy_ref[...]
    rows = lax.broadcasted_iota(jnp.int32, x.shape, 0)
    prev = jnp.where(rows == 0, 0.0, pltpu.roll(x, 1, axis=0))
    nxt = jnp.where(rows == L - 1, 0.0, pltpu.roll(x, L - 1, axis=0))
    z = prev * cw_ref[0:1, :] + x * cw_ref[1:2, :] + nxt * cw_ref[2:3, :] + cb_ref[...]
    v, x1, x2 = z[:, 0:HY_W], z[:, HY_W:2 * HY_W], z[:, 2 * HY_W:3 * HY_W]
    first = lax.broadcasted_iota(jnp.int32, (L, HY_W), 0) == 0

    def long_conv(u, o):
        uf = _dot(fwd_ref[...], u)
        ure, uim = uf[0:L], uf[L:2 * L]
        kre = kre_ref[:, o * HY_W:(o + 1) * HY_W]
        kim = kim_ref[:, o * HY_W:(o + 1) * HY_W]
        yre = jnp.where(first, ure * kre, ure * kre - uim * kim)
        yim = jnp.where(first, uim * kim, ure * kim + uim * kre)
        y = _dot(inv_ref[...], jnp.concatenate([yre, yim], 0))
        return y + u * bias_ref[o:o + 1, :]

    u = x1 * long_conv(v, 0)
    o_ref[...] = (x2 * long_conv(u, 1)).astype(o_ref.dtype)


def _hyena(z_main, prev_out, layer, L, n_seq, row_block0, conv_w, conv_b, bias, kre, kim, fwd, inv):
    aliased = prev_out is not None
    in_specs = [
        pl.BlockSpec((L, 3 * HY_W), lambda b: (row_block0 + b, 0)),
        pl.BlockSpec((None, 3, 3 * HY_W), lambda b: (layer, 0, 0)),
        pl.BlockSpec((None, 1, 3 * HY_W), lambda b: (layer, 0, 0)),
        pl.BlockSpec((None, 2, HY_W), lambda b: (layer, 0, 0)),
        pl.BlockSpec((L, 2 * HY_W), lambda b: (0, 0)),
        pl.BlockSpec((L, 2 * HY_W), lambda b: (0, 0)),
        pl.BlockSpec((2 * L, L), lambda b: (0, 0)),
        pl.BlockSpec((L, 2 * L), lambda b: (0, 0)),
    ]
    args = [z_main, conv_w, conv_b.reshape(DEPTH, 1, 3 * HY_W), bias, kre, kim, fwd, inv]
    if aliased:
        in_specs = [pl.BlockSpec(memory_space=pl.ANY)] + in_specs
        args = [prev_out] + args
    return pl.pallas_call(
        functools.partial(_hyena_kernel, L=L, aliased=aliased),
        out_shape=jax.ShapeDtypeStruct((T_ALL, HY_W), BF16),
        grid=(n_seq,),
        in_specs=in_specs,
        out_specs=pl.BlockSpec((L, HY_W), lambda b: (row_block0 + b, 0)),
        input_output_aliases={0: 0} if aliased else {},
        compiler_params=_params("parallel"),
        name=f"hyena_{L}",
    )(*args)


def _win_masks():
    lane = lax.broadcasted_iota(jnp.int32, (1, LANE), 1)
    return lane < WIN_HD, lane >= WIN_HD


def _win_head_operands(q, k, v, h):
    lo_mask, hi_mask = _win_masks()
    col = h // 2
    lo = h % 2 == 0
    q128 = jnp.where(lo_mask if lo else hi_mask, q[:, col * LANE:(col + 1) * LANE], 0.0)
    swap = h in (1, 2)
    if swap:
        k = pltpu.roll(k, WIN_HD, axis=1)
        v = pltpu.roll(v, WIN_HD, axis=1)
    return q128, k, v, lo


def _win_ctx_kernel(sink_ref, q_ref, kv_ref, o_ref, *, layer):
    q = q_ref[...]
    k = kv_ref[:, 0:LANE]
    v = kv_ref[:, LANE:2 * LANE]
    lo_mask, hi_mask = _win_masks()
    scale = WIN_HD ** -0.5
    cols = []
    for col in range(2):
        acc = None
        for h in (2 * col, 2 * col + 1):
            q128, kk, vv, lo = _win_head_operands(q, k, v, h)
            s = _dot_nt(q128, kk) * scale
            sink = sink_ref[layer, h]
            m = jnp.maximum(jnp.max(s, -1, keepdims=True), sink)
            p = jnp.exp(s - m)
            den = jnp.sum(p, -1, keepdims=True) + jnp.exp(sink - m)
            o = _dot(p, vv) / den
            o = jnp.where(lo_mask if lo else hi_mask, o, 0.0)
            acc = o if acc is None else acc + o
        cols.append(acc)
    o_ref[...] = jnp.concatenate(cols, 1).astype(o_ref.dtype)


def _win_ctx(z_main, sink, layer):
    return pl.pallas_call(
        functools.partial(_win_ctx_kernel, layer=layer),
        out_shape=jax.ShapeDtypeStruct((T_ALL, WIN_HEADS * WIN_HD), BF16),
        grid=(BATCH,),
        in_specs=[
            pl.BlockSpec(memory_space=pltpu.SMEM),
            pl.BlockSpec((SEQ, 256), lambda b: (b, COL_WQ // 256)),
            pl.BlockSpec((SEQ, 256), lambda b: (b, COL_WK // 256)),
        ],
        out_specs=pl.BlockSpec((SEQ, 256), lambda b: (b, 0)),
        compiler_params=_params("parallel"),
        name="win_ctx",
    )(sink, z_main, z_main)


def _win_lat_kernel(sink_ref, prev_ref, q_ref, kv_ref, ck_ref, cv_ref, c_ref, su_ref, sd_ref,
                    o_ref, *, layer):
    del prev_ref
    L = DEC_SEQ
    half = WIN_HD // 2
    c, su, sd = c_ref[...], su_ref[...], sd_ref[...]
    q = jnp.concatenate(
        [_rope128(q_ref[:, i * LANE:(i + 1) * LANE], c, su, sd, half) for i in range(2)], 1)
    k = _rope128(kv_ref[:, 0:LANE], c, su, sd, half)
    v = kv_ref[:, LANE:2 * LANE]
    ck = ck_ref[...]
    cv = cv_ref[...]
    lo_mask, hi_mask = _win_masks()
    scale = WIN_HD ** -0.5
    nb = L // CHUNK
    cols = []
    for col in range(2):
        acc_blocks = [None] * nb
        for h in (2 * col, 2 * col + 1):
            q128, kk, vv, lo = _win_head_operands(q, k, v, h)
            _, ckk, cvv, _ = _win_head_operands(q, ck, cv, h)
            sink = sink_ref[layer, h]
            for n in range(nb):
                k0 = max(0, (n - 1) * CHUNK)
                k1 = min(L, (n + 2) * CHUNK)
                qn = q128[n * CHUNK:(n + 1) * CHUNK]
                s_loc = _dot_nt(qn, kk[k0:k1]) * scale
                qi = n * CHUNK + lax.broadcasted_iota(jnp.int32, s_loc.shape, 0)
                kj = k0 + lax.broadcasted_iota(jnp.int32, s_loc.shape, 1)
                s_loc = jnp.where(jnp.abs(qi - kj) <= WINDOW, s_loc, NEG)
                s_ctx = _dot_nt(qn, ckk) * scale
                m = jnp.maximum(jnp.maximum(jnp.max(s_loc, -1, keepdims=True),
                                            jnp.max(s_ctx, -1, keepdims=True)), sink)
                p_loc = jnp.exp(s_loc - m)
                p_ctx = jnp.exp(s_ctx - m)
                den = (jnp.sum(p_loc, -1, keepdims=True) + jnp.sum(p_ctx, -1, keepdims=True)
                       + jnp.exp(sink - m))
                o = (_dot(p_loc, vv[k0:k1]) + _dot(p_ctx, cvv)) / den
                o = jnp.where(lo_mask if lo else hi_mask, o, 0.0)
                acc_blocks[n] = o if acc_blocks[n] is None else acc_blocks[n] + o
        cols.append(jnp.concatenate(acc_blocks, 0))
    o_ref[...] = jnp.concatenate(cols, 1).astype(o_ref.dtype)


def _win_lat(z_main, prev_out, sink, cache_k, cache_v, layer):
    c, su, sd, _ = _rope_tables(DEC_SEQ, WIN_HD, LANE)
    rb0 = T_CTX // DEC_SEQ
    tab = pl.BlockSpec((DEC_SEQ, LANE), lambda b: (0, 0))
    cache = pl.BlockSpec((None, None, PAST_LEN, LANE), lambda b: (b, layer, 0, 0))
    return pl.pallas_call(
        functools.partial(_win_lat_kernel, layer=layer),
        out_shape=jax.ShapeDtypeStruct((T_ALL, WIN_HEADS * WIN_HD), BF16),
        grid=(DEC_BATCH,),
        in_specs=[
            pl.BlockSpec(memory_space=pltpu.SMEM),
            pl.BlockSpec(memory_space=pl.ANY),
            pl.BlockSpec((DEC_SEQ, 256), lambda b: (rb0 + b, COL_WQ // 256)),
            pl.BlockSpec((DEC_SEQ, 256), lambda b: (rb0 + b, COL_WK // 256)),
            cache, cache, tab, tab, tab,
        ],
        out_specs=pl.BlockSpec((DEC_SEQ, 256), lambda b: (rb0 + b, 0)),
        input_output_aliases={1: 0},
        compiler_params=_params("parallel"),
        name="win_lat",
    )(sink, prev_out, z_main, z_main, cache_k, cache_v,
      jnp.asarray(c), jnp.asarray(su), jnp.asarray(sd))


def _ret_kernel(*refs, L, layer, ctx):
    if ctx:
        (df_ref, db_ref, q_ref, k_ref, v0_ref, v1_ref, g0_ref, g1_ref,
         o_ref, sf_out, sb_out, s_ref, cross_ref) = refs
    else:
        (df_ref, db_ref, prev_ref, q_ref, k_ref, v0_ref, v1_ref, g0_ref, g1_ref, s0f_ref, s0b_ref,
         o_ref, s_ref, cross_ref) = refs
    C = CHUNK
    nc = L // C
    H = RET_HEADS
    qw = H * RET_DK
    vw = H * RET_DV

    def lane_table(width, per_head, fn):
        pos = lax.broadcasted_iota(jnp.int32, (C, per_head), 0).astype(F32)
        return jnp.concatenate([fn(h, pos) for h in range(H)], 1)

    def log_gamma(ref, h):
        d = jnp.full((1, 1), ref[layer, h], F32)
        return jnp.log(jax.nn.sigmoid(d))

    lgf = [log_gamma(df_ref, h) for h in range(H)]
    lgb = [log_gamma(db_ref, h) for h in range(H)]
    q_all = q_ref[...]
    k_all = k_ref[...] * (RET_DK ** -0.5)
    v_all = jnp.concatenate([v0_ref[...], v1_ref[...]], 1)

    srow = lax.broadcasted_iota(jnp.int32, (qw, vw), 0) // RET_DK
    scol = lax.broadcasted_iota(jnp.int32, (qw, vw), 1) // RET_DV
    diag = srow == scol

    def scan(lg, reverse, s0_ref, s_out):
        if reverse:
            dq = lane_table(vw, RET_DV, lambda h, pos: jnp.exp((C - pos) * lg[h]))
            dk = lane_table(qw, RET_DK, lambda h, pos: jnp.exp(pos * lg[h]))
        else:
            dq = lane_table(vw, RET_DV, lambda h, pos: jnp.exp((pos + 1.0) * lg[h]))
            dk = lane_table(qw, RET_DK, lambda h, pos: jnp.exp((C - 1.0 - pos) * lg[h]))
        dc = jnp.concatenate([jnp.broadcast_to(jnp.exp(C * lg[h]), (1, RET_DV)) for h in range(H)], 1)
        if s0_ref is None:
            s_ref[...] = jnp.zeros((qw, vw), F32)
        else:
            s_ref[...] = jnp.zeros((qw, vw), F32)
            for h in range(H):
                s_ref[h * RET_DK:(h + 1) * RET_DK, h * RET_DV:(h + 1) * RET_DV] = s0_ref[h]
        order = range(nc - 1, -1, -1) if reverse else range(nc)
        for ci in order:
            sl = slice(ci * C, (ci + 1) * C)
            qc, kc, vc = q_all[sl], k_all[sl], v_all[sl]
            s = s_ref[...]
            cross = _dot(qc, s) * dq
            if reverse:
                cross_ref[sl, :] = cross_ref[sl, :] + cross
            else:
                cross_ref[sl, :] = cross
            upd = jnp.where(diag, _dot_tn(kc * dk, vc), 0.0)
            s_ref[...] = s * dc + upd
        if s_out is not None:
            for h in range(H):
                s_out[h] = s_ref[h * RET_DK:(h + 1) * RET_DK, h * RET_DV:(h + 1) * RET_DV]

    scan(lgf, False, None if ctx else s0f_ref, sf_out if ctx else None)
    scan(lgb, True, None if ctx else s0b_ref, sb_out if ctx else None)

    ii = lax.broadcasted_iota(jnp.int32, (C, C), 0)
    jj = lax.broadcasted_iota(jnp.int32, (C, C), 1)
    diff = (ii - jj).astype(F32)
    lane_q = lax.broadcasted_iota(jnp.int32, (1, qw), 1) // RET_DK
    g_all = jnp.concatenate([g0_ref[...], g1_ref[...]], 1)
    for h in range(H):
        dmat = (jnp.where(diff >= 0, jnp.exp(jnp.maximum(diff, 0.0) * lgf[h]), 0.0)
                + jnp.where(diff <= 0, jnp.exp(jnp.maximum(-diff, 0.0) * lgb[h]), 0.0))
        hv = slice(h * RET_DV, (h + 1) * RET_DV)
        for ci in range(nc):
            sl = slice(ci * C, (ci + 1) * C)
            qh = jnp.where(lane_q == h, q_all[sl], 0.0)
            att = _dot_nt(qh, k_all[sl]) * dmat
            o = _dot(att, v_all[sl, hv]) + cross_ref[sl, hv]
            g = g_all[sl, hv]
            o_ref[sl, hv] = ((g * jax.nn.sigmoid(g)) * _layer_norm(o)).astype(o_ref.dtype)


def _retention(z_main, prev_out, dec_f, dec_b, s0f, s0b, layer, ctx):
    L = SEQ if ctx else DEC_SEQ
    n_seq = BATCH if ctx else DEC_BATCH
    rb0 = 0 if ctx else T_CTX // DEC_SEQ

    def zcol(col):
        return pl.BlockSpec((L, 256), lambda b: (rb0 + b, col // 256))

    smem = pl.BlockSpec(memory_space=pltpu.SMEM)
    z_specs = [zcol(COL_RQ), zcol(COL_RK), zcol(COL_RV), zcol(COL_RV + 256),
               zcol(COL_RG), zcol(COL_RG + 256)]
    y_shape = jax.ShapeDtypeStruct((T_ALL, RET_HEADS * RET_DV), BF16)
    y_spec = pl.BlockSpec((L, RET_HEADS * RET_DV), lambda b: (rb0 + b, 0))
    scratch = [pltpu.VMEM((RET_HEADS * RET_DK, RET_HEADS * RET_DV), F32),
               pltpu.VMEM((L, RET_HEADS * RET_DV), F32)]
    kern = functools.partial(_ret_kernel, L=L, layer=layer, ctx=ctx)
    if ctx:
        st_shape = jax.ShapeDtypeStruct((BATCH, RET_HEADS, RET_DK, RET_DV), F32)
        st_spec = pl.BlockSpec((None, RET_HEADS, RET_DK, RET_DV), lambda b: (b, 0, 0, 0))
        return pl.pallas_call(
            kern,
            out_shape=(y_shape, st_shape, st_shape),
            grid=(n_seq,),
            in_specs=[smem, smem] + z_specs,
            out_specs=(y_spec, st_spec, st_spec),
            scratch_shapes=scratch,
            compiler_params=_params("parallel"),
            name="ret_ctx",
        )(dec_f, dec_b, *([z_main] * 6))
    s0_spec = pl.BlockSpec((None, None, RET_HEADS, RET_DK, RET_DV), lambda b: (b, layer, 0, 0, 0))
    return pl.pallas_call(
        kern,
        out_shape=y_shape,
        grid=(n_seq,),
        in_specs=[smem, smem, pl.BlockSpec(memory_space=pl.ANY)] + z_specs + [s0_spec, s0_spec],
        out_specs=y_spec,
        scratch_shapes=scratch,
        input_output_aliases={2: 0},
        compiler_params=_params("parallel"),
        name="ret_lat",
    )(dec_f, dec_b, prev_out, *([z_main] * 6), s0f, s0b)


def _rms_norm(x, g):
    return x * lax.rsqrt(jnp.mean(x * x, -1, keepdims=True) + RMS_EPS) * g


def _mla_attend(qn, qr, kn, kr, vv, o_ref, row0):
    scale = (MLA_NOPE + MLA_ROPE) ** -0.5
    lane_n = lax.broadcasted_iota(jnp.int32, (1, MLA_HEADS * MLA_NOPE), 1) // MLA_NOPE
    lane_r = lax.broadcasted_iota(jnp.int32, (1, LANE), 1)
    kr32 = jnp.where(lane_r < MLA_ROPE, kr, 0.0)
    acc = None
    for h in range(MLA_HEADS):
        qnh = jnp.where(lane_n == h, qn, 0.0)
        qrh = qr if h == 0 else pltpu.roll(qr, LANE - h * MLA_ROPE, axis=1)
        qrh = jnp.where(lane_r < MLA_ROPE, qrh, 0.0)
        s = (_dot_nt(qnh, kn) + _dot_nt(qrh, kr32)) * scale
        m = jnp.max(s, -1, keepdims=True)
        p = jnp.exp(s - m)
        den = jnp.sum(p, -1, keepdims=True)
        o = jnp.where(lane_n == h, _dot(p, vv) / den, 0.0)
        acc = o if acc is None else acc + o
    o_ref[row0:row0 + acc.shape[0], :] = acc.astype(o_ref.dtype)


def _mla_ctx_kernel(cq_ref, ckv_ref, kr_ref, qg_ref, kg_ref, wqn_ref, wqr_ref, wk_ref, wv_ref,
                    o_ref, ckvn_ref):
    cqn = _rms_norm(cq_ref[...], qg_ref[...])
    qn = _dot(cqn, wqn_ref[...])
    qr = _dot(cqn, wqr_ref[...])
    ckvn = _rms_norm(ckv_ref[...], kg_ref[...])
    ckvn_ref[...] = ckvn
    kn = _dot(ckvn, wk_ref[...])
    vv = _dot(ckvn, wv_ref[...])
    _mla_attend(qn, qr, kn, kr_ref[...], vv, o_ref, 0)


def _mla_weight_specs(layer):
    return [
        pl.BlockSpec((None, 1, MLA_Q_LORA), lambda b: (layer, 0, 0)),
        pl.BlockSpec((None, 1, MLA_KV_LORA), lambda b: (layer, 0, 0)),
        pl.BlockSpec((None, MLA_Q_LORA, MLA_HEADS * MLA_NOPE), lambda b: (layer, 0, 0)),
        pl.BlockSpec((None, MLA_Q_LORA, MLA_HEADS * MLA_ROPE), lambda b: (layer, 0, 0)),
        pl.BlockSpec((None, MLA_KV_LORA, MLA_HEADS * MLA_NOPE), lambda b: (layer, 0, 0)),
        pl.BlockSpec((None, MLA_KV_LORA, MLA_HEADS * MLA_V), lambda b: (layer, 0, 0)),
    ]


def _mla_ctx(z_main, weights, layer):
    return pl.pallas_call(
        _mla_ctx_kernel,
        out_shape=(jax.ShapeDtypeStruct((T_ALL, MLA_HEADS * MLA_V), BF16),
                   jax.ShapeDtypeStruct((T_CTX, MLA_KV_LORA), F32)),
        grid=(BATCH,),
        in_specs=[
            pl.BlockSpec((SEQ, 256), lambda b: (b, COL_CQ // 256)),
            pl.BlockSpec((SEQ, LANE), lambda b: (b, COL_CKV // LANE)),
            pl.BlockSpec((SEQ, LANE), lambda b: (b, COL_KROPE // LANE)),
        ] + _mla_weight_specs(layer),
        out_specs=(pl.BlockSpec((SEQ, 256), lambda b: (b, 0)),
                   pl.BlockSpec((SEQ, MLA_KV_LORA), lambda b: (b, 0))),
        compiler_params=_params("parallel"),
        name="mla_ctx",
    )(z_main, z_main, z_main, *weights)


def _mla_lat_kernel(prev_ref, cq_ref, ckv_ref, kr_ref, cckv_ref, ckr_ref, c_ref, su_ref, sd_ref,
                    qg_ref, kg_ref, wqn_ref, wqr_ref, wk_ref, wv_ref, o_ref):
    del prev_ref
    half = MLA_ROPE // 2
    c, su, sd = c_ref[...], su_ref[...], sd_ref[...]
    cqn = _rms_norm(cq_ref[...], qg_ref[...])
    qn = _dot(cqn, wqn_ref[...])
    qr = _rope128(_dot(cqn, wqr_ref[...]), c, su, sd, half)
    ckvn = _rms_norm(ckv_ref[...], kg_ref[...])
    ckv_all = jnp.concatenate([ckvn, cckv_ref[...]], 0)
    kn = _dot(ckv_all, wk_ref[...])
    vv = _dot(ckv_all, wv_ref[...])
    kr = jnp.concatenate([_rope128(kr_ref[...], c, su, sd, half), ckr_ref[...]], 0)
    for n in range(DEC_SEQ // 256):
        rows = slice(n * 256, (n + 1) * 256)
        _mla_attend(qn[rows], qr[rows], kn, kr, vv, o_ref, n * 256)


def _mla_lat(z_main, prev_out, cache_ckv, cache_kr_pad, weights, layer):
    c, su, sd, _ = _rope_tables(DEC_SEQ, MLA_ROPE, LANE)
    rb0 = T_CTX // DEC_SEQ
    tab = pl.BlockSpec((DEC_SEQ, LANE), lambda b: (0, 0))
    cache = pl.BlockSpec((None, None, PAST_LEN, LANE), lambda b: (b, layer, 0, 0))
    return pl.pallas_call(
        _mla_lat_kernel,
        out_shape=jax.ShapeDtypeStruct((T_ALL, MLA_HEADS * MLA_V), BF16),
        grid=(DEC_BATCH,),
        in_specs=[
            pl.BlockSpec(memory_space=pl.ANY),
            pl.BlockSpec((DEC_SEQ, 256), lambda b: (rb0 + b, COL_CQ // 256)),
            pl.BlockSpec((DEC_SEQ, LANE), lambda b: (rb0 + b, COL_CKV // LANE)),
            pl.BlockSpec((DEC_SEQ, LANE), lambda b: (rb0 + b, COL_KROPE // LANE)),
            cache, cache, tab, tab, tab,
        ] + _mla_weight_specs(layer),
        out_specs=pl.BlockSpec((DEC_SEQ, 256), lambda b: (rb0 + b, 0)),
        input_output_aliases={0: 0},
        compiler_params=_params("parallel"),
        name="mla_lat",
    )(prev_out, z_main, z_main, z_main, cache_ckv, cache_kr_pad,
      jnp.asarray(c), jnp.asarray(su), jnp.asarray(sd), *weights)


def _route(logits_t, rb):
    scores = jax.nn.sigmoid(logits_t)
    biased = scores + rb
    sc = [scores[e:e + 1, :] for e in range(N_EXPERTS)]
    bi = [biased[e:e + 1, :] for e in range(N_EXPERTS)]
    epg = EXPERTS_PER_GROUP
    gsum = []
    for g in range(N_GROUPS):
        v = bi[g * epg:(g + 1) * epg]
        best = None
        for a in range(epg):
            for b in range(a + 1, epg):
                pair = v[a] + v[b]
                best = pair if best is None else jnp.maximum(best, pair)
        gsum.append(best)
    combine = []
    sel = []
    for g in range(N_GROUPS):
        is_best = None
        for g2 in range(N_GROUPS):
            if g2 == g:
                continue
            c = gsum[g] > gsum[g2] if g2 < g else gsum[g] >= gsum[g2]
            is_best = c if is_best is None else jnp.logical_and(is_best, c)
        for a in range(epg):
            e = g * epg + a
            rank = jnp.zeros_like(bi[e])
            for b in range(epg):
                if b == a:
                    continue
                e2 = g * epg + b
                ahead = bi[e2] >= bi[e] if b < a else bi[e2] > bi[e]
                rank = rank + jnp.where(ahead, 1.0, 0.0)
            sel.append(jnp.logical_and(is_best, rank < 2.0))
    wsum = None
    for e in range(N_EXPERTS):
        w = jnp.where(sel[e], sc[e], 0.0)
        wsum = w if wsum is None else wsum + w
    for e in range(N_EXPERTS):
        combine.append(jnp.where(sel[e], ROUTE_SCALE * sc[e] / wsum, 0.0))
    return jnp.concatenate(combine, 0)


def _merge_kernel(ya_ref, yb_ref, yc_ref, yd_ref, gt_ref, xc_ref, xl_ref, m_ref,
                  wa_ref, wb_ref, wc_ref, wd_ref, wo_ref, g_ref, b_ref, rwh_ref, rwl_ref, rb_ref,
                  x1_ref, h2_ref, cmb_ref, *, ctx_tiles):
    D = D_MODEL
    merged = None
    for i, (y_ref, w) in enumerate(((ya_ref, wa_ref), (yb_ref, wb_ref), (yc_ref, wc_ref), (yd_ref, wd_ref))):
        t = gt_ref[:, i * D:(i + 1) * D].astype(F32) * jnp.dot(
            y_ref[...], w[...], preferred_element_type=F32)
        merged = t if merged is None else merged + t
    out1 = jnp.dot(merged.astype(BF16), wo_ref[...], preferred_element_type=F32)
    g1 = m_ref[:, 2 * D:3 * D]
    s2 = m_ref[:, 3 * D:4 * D]
    sc2 = m_ref[:, 4 * D:5 * D]
    x = jnp.where(pl.program_id(0) < ctx_tiles, xc_ref[...], xl_ref[...])
    x1 = _layer_norm(ALPHA * x + g1 * out1) * g_ref[...] + b_ref[...]
    x1_ref[...] = x1
    h2 = _layer_norm(x1) * (1.0 + sc2) + s2
    h2_hi = h2.astype(BF16)
    h2_ref[...] = h2_hi
    h2_lo = (h2 - h2_hi.astype(F32)).astype(BF16)
    logits = (jnp.dot(h2_hi, rwh_ref[...], preferred_element_type=F32)
              + (jnp.dot(h2_lo, rwh_ref[...], preferred_element_type=F32)
                 + jnp.dot(h2_hi, rwl_ref[...], preferred_element_type=F32)))
    cmb_ref[...] = _route(logits.T[0:N_EXPERTS], rb_ref[...])


def _merge(ya, yb, yc, yd, gates, x_ctx, x_lat, mods_l, w_br, w_out_bf, ln1_g, ln1_b,
           router_w_parts, router_b, layer):
    tm = 512
    row = _mod_row(tm)
    D = D_MODEL
    ctx_tiles = T_CTX // tm

    def tile(w):
        return pl.BlockSpec((tm, w), lambda i: (i, 0))

    def weight(k, n):
        return pl.BlockSpec((None, k, n), lambda i: (layer, 0, 0))

    return pl.pallas_call(
        functools.partial(_merge_kernel, ctx_tiles=ctx_tiles),
        out_shape=(jax.ShapeDtypeStruct((T_ALL, D), F32),
                   jax.ShapeDtypeStruct((T_ALL, D), BF16),
                   jax.ShapeDtypeStruct((N_EXPERTS, T_ALL), F32)),
        grid=(T_ALL // tm,),
        in_specs=[
            tile(256), tile(256), tile(512), tile(256), tile(4 * D),
            pl.BlockSpec((tm, D), lambda i: (jnp.minimum(i, ctx_tiles - 1), 0)),
            pl.BlockSpec((tm, D), lambda i: (jnp.maximum(i - ctx_tiles, 0), 0)),
            pl.BlockSpec((None, 1, 6 * D), lambda i: (row(i), 0, 0)),
            weight(256, D), weight(256, D), weight(512, D), weight(256, D), weight(D, D),
            weight(1, D), weight(1, D),
            pl.BlockSpec((D, LANE), lambda i: (0, 0)),
            pl.BlockSpec((D, LANE), lambda i: (0, 0)),
            pl.BlockSpec((N_EXPERTS, 1), lambda i: (0, 0)),
        ],
        out_specs=(tile(D), tile(D), pl.BlockSpec((N_EXPERTS, tm), lambda i: (0, i))),
        compiler_params=_params("parallel"),
        name="merge",
    )(ya, yb, yc, yd, gates, x_ctx, x_lat, mods_l, *w_br, w_out_bf,
      ln1_g.reshape(DEPTH, 1, D), ln1_b.reshape(DEPTH, 1, D), *router_w_parts,
      router_b.reshape(N_EXPERTS, 1))


MOE_EXPERTS_PER_STEP = 2


def _moe_kernel(h_ref, c_ref, x1_ref, m_ref, wg_ref, wu_ref, wd_ref, g_ref, b_ref, o_ref, acc_ref):
    eg = pl.program_id(1)

    @pl.when(eg == 0)
    def _():
        acc_ref[...] = jnp.zeros_like(acc_ref)

    h = h_ref[...]
    cmb = c_ref[...]
    lane = lax.broadcasted_iota(jnp.int32, cmb.shape, 1)
    hid = []
    for k in range(MOE_EXPERTS_PER_STEP):
        gate = jnp.dot(h, wg_ref[k].astype(BF16), preferred_element_type=F32)
        up = jnp.dot(h, wu_ref[k].astype(BF16), preferred_element_type=F32)
        e = eg * MOE_EXPERTS_PER_STEP + k
        ce = jnp.sum(jnp.where(lane == e, cmb, 0.0), -1, keepdims=True)
        hid.append(((gate * jax.nn.sigmoid(gate)) * up * ce).astype(BF16))
    wd = wd_ref[...].reshape(MOE_EXPERTS_PER_STEP * D_EXPERT, D_MODEL).astype(BF16)
    acc_ref[...] += jnp.dot(jnp.concatenate(hid, 1), wd, preferred_element_type=F32)

    @pl.when(eg == N_EXPERTS // MOE_EXPERTS_PER_STEP - 1)
    def _():
        g2 = m_ref[:, 5 * D_MODEL:6 * D_MODEL]
        y = _layer_norm(ALPHA * x1_ref[...] + g2 * acc_ref[...])
        o_ref[...] = y * g_ref[...] + b_ref[...]


def _moe(h2, combine, x1, mods_l, w_gate, w_up, w_down, ln2_g, ln2_b, layer, row0, n_rows):
    tm = 1024
    row = _mod_row(tm)
    D = D_MODEL
    t0 = row0 // tm
    eps = MOE_EXPERTS_PER_STEP
    return pl.pallas_call(
        _moe_kernel,
        out_shape=jax.ShapeDtypeStruct((n_rows, D), F32),
        grid=(n_rows // tm, N_EXPERTS // eps),
        in_specs=[
            pl.BlockSpec((tm, D), lambda i, e: (t0 + i, 0)),
            pl.BlockSpec((tm, N_EXPERTS), lambda i, e: (t0 + i, 0)),
            pl.BlockSpec((tm, D), lambda i, e: (t0 + i, 0)),
            pl.BlockSpec((None, 1, 6 * D), lambda i, e: (row(t0 + i), 0, 0)),
            pl.BlockSpec((None, eps, D, D_EXPERT), lambda i, e: (layer, e, 0, 0)),
            pl.BlockSpec((None, eps, D, D_EXPERT), lambda i, e: (layer, e, 0, 0)),
            pl.BlockSpec((None, eps, D_EXPERT, D), lambda i, e: (layer, e, 0, 0)),
            pl.BlockSpec((None, 1, D), lambda i, e: (layer, 0, 0)),
            pl.BlockSpec((None, 1, D), lambda i, e: (layer, 0, 0)),
        ],
        out_specs=pl.BlockSpec((tm, D), lambda i, e: (i, 0)),
        scratch_shapes=[pltpu.VMEM((tm, D), F32)],
        compiler_params=_params("parallel", "arbitrary"),
        name="moe",
    )(h2, combine, x1, mods_l, w_gate, w_up, w_down,
      ln2_g.reshape(DEPTH, 1, D), ln2_b.reshape(DEPTH, 1, D))


def kernel(x_prompt, x_sample, cache_win_k, cache_win_v, cache_mla_ckv, cache_mla_krope,
           state_ret_fwd, state_ret_bwd, c, c_ctx, w_ada, b_ada, w_in,
           hy_conv_w, hy_conv_b, hy_w1, hy_b1, hy_w2, hy_b2, hy_w3, hy_bias,
           win_sink, ret_decay_fwd, ret_decay_bwd, mla_q_norm, mla_kv_norm, mla_w_uq, mla_w_ukv,
           w_br_a, w_br_b, w_br_c, w_br_d, w_out, ln1_g, ln1_b, ln2_g, ln2_b,
           router_w, router_b, moe_w_gate, moe_w_up, moe_w_down):
    D = D_MODEL
    x_ctx = x_prompt.reshape(T_CTX, D)
    x_lat = x_sample.reshape(T_LAT, D)

    cvec = jnp.zeros((8, D), F32).at[0].set(c_ctx).at[1:1 + DEC_BATCH].set(c)
    mods = _ada_mods(cvec, w_ada, b_ada)[:, :1 + DEC_BATCH].reshape(DEPTH, 1 + DEC_BATCH, 1, 6 * D)

    w_in_t = jnp.swapaxes(w_in, 1, 2).reshape(DEPTH * IN_COLS, D)
    cache_k = cache_win_k.reshape(DEC_BATCH, DEPTH, PAST_LEN, WIN_KV_HEADS * WIN_HD)
    cache_v = cache_win_v.reshape(DEC_BATCH, DEPTH, PAST_LEN, WIN_KV_HEADS * WIN_HD)
    cache_kr = jnp.pad(cache_mla_krope, ((0, 0), (0, 0), (0, 0), (0, LANE - MLA_ROPE)))

    uq = mla_w_uq.reshape(DEPTH, MLA_Q_LORA, MLA_HEADS, MLA_NOPE + MLA_ROPE)
    ukv = mla_w_ukv.reshape(DEPTH, MLA_KV_LORA, MLA_HEADS, MLA_NOPE + MLA_V)
    mla_weights = (
        mla_q_norm.reshape(DEPTH, 1, MLA_Q_LORA),
        mla_kv_norm.reshape(DEPTH, 1, MLA_KV_LORA),
        uq[..., :MLA_NOPE].reshape(DEPTH, MLA_Q_LORA, MLA_HEADS * MLA_NOPE),
        uq[..., MLA_NOPE:].reshape(DEPTH, MLA_Q_LORA, MLA_HEADS * MLA_ROPE),
        ukv[..., :MLA_NOPE].reshape(DEPTH, MLA_KV_LORA, MLA_HEADS * MLA_NOPE),
        ukv[..., MLA_NOPE:].reshape(DEPTH, MLA_KV_LORA, MLA_HEADS * MLA_V),
    )

    hy_w1p = jnp.pad(hy_w1, ((0, 0), (0, LANE - HY_EMB), (0, 0)))
    dft = {}
    for L in (SEQ, DEC_SEQ):
        fwd, inv = _dft_tables(L)
        dft[L] = (jnp.asarray(fwd).astype(BF16), jnp.asarray(inv).astype(BF16))
    router_w_pad = jnp.pad(router_w, ((0, 0), (0, LANE - N_EXPERTS)))
    router_w_hi = router_w_pad.astype(BF16)
    router_w_parts = (router_w_hi, (router_w_pad - router_w_hi.astype(F32)).astype(BF16))
    w_br = tuple(w.astype(BF16) for w in (w_br_a, w_br_b, w_br_c, w_br_d))
    w_out_bf = w_out.astype(BF16)

    new_k, new_v, new_ckv, new_kr, new_sf, new_sb = [], [], [], [], [], []
    for l in range(DEPTH):
        mods_l = mods[l]
        h = _ln_mod(x_ctx, None, mods_l, 0)
        h = _ln_mod(x_lat, h, mods_l, T_CTX)
        z = _in_proj(h, w_in_t, l, 0, Z_MAIN, Z_MAIN // 2, F32, gate=False)
        gates = _in_proj(h, w_in_t, l, COL_GATE, 4 * D, D, BF16, gate=True)

        ya = None
        for L, n_seq, rb0 in ((SEQ, BATCH, 0), (DEC_SEQ, DEC_BATCH, T_CTX // DEC_SEQ)):
            fwd, inv = dft[L]
            kre, kim = _hy_filters(L, hy_w1p[l], hy_b1[l][None], hy_w2[l], hy_b2[l][None], hy_w3[l], fwd)
            ya = _hyena(z, ya, l, L, n_seq, rb0, hy_conv_w, hy_conv_b, hy_bias, kre, kim, fwd, inv)

        yb = _win_ctx(z, win_sink, l)
        yb = _win_lat(z, yb, win_sink, cache_k, cache_v, l)

        yc, sf, sb = _retention(z, None, ret_decay_fwd, ret_decay_bwd, None, None, l, ctx=True)
        yc = _retention(z, yc, ret_decay_fwd, ret_decay_bwd, state_ret_fwd, state_ret_bwd, l, ctx=False)

        yd, ckvn = _mla_ctx(z, mla_weights, l)
        yd = _mla_lat(z, yd, cache_mla_ckv, cache_kr, mla_weights, l)

        x1, h2, combine_t = _merge(ya, yb, yc, yd, gates, x_ctx, x_lat, mods_l, w_br, w_out_bf,
                                   ln1_g, ln1_b, router_w_parts, router_b, l)
        moe_args = (h2, combine_t.T, x1, mods_l, moe_w_gate, moe_w_up, moe_w_down, ln2_g, ln2_b, l)
        x_ctx = _moe(*moe_args, 0, T_CTX)
        x_lat = _moe(*moe_args, T_CTX, T_LAT)

        zc = z[:T_CTX]
        new_k.append(zc[:, COL_WK:COL_WK + 128].reshape(BATCH, SEQ, WIN_KV_HEADS, WIN_HD))
        new_v.append(zc[:, COL_WV:COL_WV + 128].reshape(BATCH, SEQ, WIN_KV_HEADS, WIN_HD))
        new_ckv.append(ckvn.reshape(BATCH, SEQ, MLA_KV_LORA))
        new_kr.append(zc[:, COL_KROPE:COL_KROPE + MLA_ROPE].reshape(BATCH, SEQ, MLA_ROPE))
        new_sf.append(sf)
        new_sb.append(sb)

    y_prompt = x_ctx.reshape(BATCH, SEQ, D)
    y_sample = x_lat.reshape(DEC_BATCH, DEC_SEQ, D)
    return (y_prompt, y_sample, jnp.stack(new_k, 1), jnp.stack(new_v, 1), jnp.stack(new_ckv, 1),
            jnp.stack(new_kr, 1), jnp.stack(new_sf, 1), jnp.stack(new_sb, 1))
```

```python
import functools
import math

import numpy as np
import jax
import jax.numpy as jnp
from jax import lax
from jax.experimental import pallas as pl
from jax.experimental.pallas import tpu as pltpu

F32 = jnp.float32
BF16 = jnp.bfloat16
HIGHEST = lax.Precision.HIGHEST

D_MODEL = 1024
BATCH = 16
SEQ = 256
DEPTH = 2
DEC_BATCH = 2
DEC_SEQ = 1024
PAST_LEN = 256
GRID_W = 64
CHUNK = 128
ROPE_BASE = 10000.0
NEG = -1e30
LN_EPS = 1e-5
RMS_EPS = 1e-6

HY_W = 256
HY_BANDS = 16
HY_EMB = 1 + 2 * HY_BANDS
HY_FFN = 64
HY_FAST_DECAY = 0.3
HY_SLOW_DECAY = 1.5
HY_TARGET = 1e-2

WIN_HEADS = 4
WIN_KV_HEADS = 2
WIN_HD = 64
WINDOW = 128

RET_HEADS = 4
RET_DK = 64
RET_DV = 128

MLA_HEADS = 4
MLA_Q_LORA = 256
MLA_KV_LORA = 128
MLA_NOPE = 64
MLA_ROPE = 32
MLA_V = 64

N_EXPERTS = 16
N_GROUPS = 4
EXPERTS_PER_GROUP = N_EXPERTS // N_GROUPS
D_EXPERT = 256
ROUTE_SCALE = 2.5

ALPHA = (2.0 * DEPTH) ** 0.25

T_CTX = BATCH * SEQ
T_LAT = DEC_BATCH * DEC_SEQ
T_ALL = T_CTX + T_LAT

COL_HY = 0
COL_WQ = 768
COL_WK = 1024
COL_WV = 1152
COL_RQ = 1280
COL_RK = 1536
COL_RV = 1792
COL_RG = 2304
COL_CQ = 2816
COL_CKV = 3072
COL_KROPE = 3200
COL_GATE = 3232
IN_COLS = COL_GATE + 4 * D_MODEL
Z_MAIN = 3328

LANE = 128
CTX_SEQS_PER_STEP = 2
VMEM_LIMIT = 56 * 1024 * 1024


def _params(*sem):
    return pltpu.CompilerParams(dimension_semantics=sem, vmem_limit_bytes=VMEM_LIMIT)


def _dot(a, b):
    return jnp.dot(a.astype(BF16), b.astype(BF16), preferred_element_type=F32)


def _dot_nt(a, b):
    return lax.dot_general(a.astype(BF16), b.astype(BF16), (((1,), (1,)), ((), ())),
                           preferred_element_type=F32)


def _dot_tn(a, b):
    return lax.dot_general(a.astype(BF16), b.astype(BF16), (((0,), (0,)), ((), ())),
                           preferred_element_type=F32)


def _layer_norm(x):
    mu = jnp.mean(x, -1, keepdims=True)
    xc = x - mu
    var = jnp.mean(xc * xc, -1, keepdims=True)
    return xc * lax.rsqrt(var + LN_EPS)


def _mod_row(tile_rows):
    def row(i):
        start = i * tile_rows
        return jnp.where(start < T_CTX, 0, 1 + (start - T_CTX) // DEC_SEQ)
    return row


@functools.lru_cache(maxsize=None)
def _dft_tables(L):
    f = np.arange(L, dtype=np.int64)[:, None]
    s = np.arange(L, dtype=np.int64)[None, :]
    ang = np.pi * ((f * s) % (2 * L)).astype(np.float64) / L
    cos = np.cos(ang)
    sin = np.sin(ang)
    alt = np.where(np.arange(L) % 2 == 0, 1.0, -1.0)
    fwd_im = -sin
    fwd_im[0, :] = alt
    fwd = np.concatenate([cos, fwd_im], 0)
    inv_re = cos.T / L
    inv_re[:, 0] = 1.0 / (2 * L)
    inv_im = -sin.T / L
    inv_im[:, 0] = alt / (2 * L)
    inv = np.concatenate([inv_re, inv_im], 1)
    return fwd.astype(np.float32), inv.astype(np.float32)


@functools.lru_cache(maxsize=None)
def _hyena_embedding(L):
    t01 = np.linspace(0.0, 1.0, L, dtype=np.float64)[:, None]
    bands = np.linspace(1e-4, HY_BANDS - 1, HY_BANDS, dtype=np.float64)
    ang = (2.0 * math.pi / L) * np.arange(L, dtype=np.float64)[:, None] * bands[None, :]
    z = np.concatenate([t01, np.cos(ang), -np.sin(ang)], -1)
    zp = np.zeros((L, LANE), np.float64)
    zp[:, :HY_EMB] = z
    deltas = np.abs(np.linspace(math.log(HY_TARGET) / HY_SLOW_DECAY,
                                math.log(HY_TARGET) / HY_FAST_DECAY, HY_W, dtype=np.float64))
    return zp.astype(np.float32), deltas[None, :].astype(np.float32)


@functools.lru_cache(maxsize=None)
def _rope_tables(L, rot_dim, width):
    rows = L // GRID_W
    n_freq = rot_dim // 4
    half = rot_dim // 2
    inv = ROPE_BASE ** (-np.arange(n_freq, dtype=np.float64) / n_freq)
    pos = np.arange(L)
    row = (pos // GRID_W).astype(np.float64)
    col = (pos % GRID_W).astype(np.float64)
    ang = np.concatenate([row[:, None] * inv, col[:, None] * inv], -1)
    cos, sin = np.cos(ang), np.sin(ang)
    zero = np.zeros_like(sin)
    c = np.tile(np.concatenate([cos, cos], -1), (1, width // rot_dim))
    s_up = np.tile(np.concatenate([-sin, zero], -1), (1, width // rot_dim))
    s_dn = np.tile(np.concatenate([zero, sin], -1), (1, width // rot_dim))
    return c.astype(np.float32), s_up.astype(np.float32), s_dn.astype(np.float32), half


def _rope128(x, c, s_up, s_dn, half):
    up = pltpu.roll(x, LANE - half, axis=1)
    dn = pltpu.roll(x, half, axis=1)
    return x * c + up * s_up + dn * s_dn


def _ada_kernel(c_ref, w_ref, b_ref, o_ref):
    cv = c_ref[...]
    s = cv * jax.nn.sigmoid(cv)
    s_hi = s.astype(BF16)
    s_lo = (s - s_hi.astype(F32)).astype(BF16)
    w = w_ref[...]
    w_hi = w.astype(BF16)
    w_lo = (w - w_hi.astype(F32)).astype(BF16)
    rows = s.shape[0]
    both = jnp.dot(jnp.concatenate([s_hi, s_lo], 0), w_hi, preferred_element_type=F32)
    o_ref[...] = (both[0:rows] + (both[rows:2 * rows] + jnp.dot(s_hi, w_lo, preferred_element_type=F32))
                  + b_ref[...])


ADA_ROWS = 16


def _ada_mods(cvec, w_ada, b_ada):
    tn = 1536
    n = 6 * D_MODEL
    return pl.pallas_call(
        _ada_kernel,
        out_shape=jax.ShapeDtypeStruct((DEPTH, ADA_ROWS, n), F32),
        grid=(DEPTH, n // tn),
        in_specs=[
            pl.BlockSpec((ADA_ROWS, D_MODEL), lambda l, j: (0, 0)),
            pl.BlockSpec((None, D_MODEL, tn), lambda l, j: (l, 0, j)),
            pl.BlockSpec((None, 1, tn), lambda l, j: (l, 0, j)),
        ],
        out_specs=pl.BlockSpec((None, ADA_ROWS, tn), lambda l, j: (l, 0, j)),
        compiler_params=_params("parallel", "parallel"),
        name="ada_mods",
    )(cvec, w_ada, b_ada.reshape(DEPTH, 1, n))


def _lnmod_kernel(*refs):
    x_ref, m_ref, h_ref = refs[-3:]
    y = _layer_norm(x_ref[...])
    s1 = m_ref[:, 0:D_MODEL]
    sc1 = m_ref[:, D_MODEL:2 * D_MODEL]
    h_ref[...] = (y * (1.0 + sc1) + s1).astype(h_ref.dtype)


def _ln_mod(x_group, prev_out, mods_l, row0):
    tm = 512
    row = _mod_row(tm)
    tile0 = row0 // tm
    in_specs = [
        pl.BlockSpec((tm, D_MODEL), lambda i: (i, 0)),
        pl.BlockSpec((None, 1, 6 * D_MODEL), lambda i: (row(tile0 + i), 0, 0)),
    ]
    args = [x_group, mods_l]
    if prev_out is not None:
        in_specs = [pl.BlockSpec(memory_space=pl.ANY)] + in_specs
        args = [prev_out] + args
    return pl.pallas_call(
        _lnmod_kernel,
        out_shape=jax.ShapeDtypeStruct((T_ALL, D_MODEL), BF16),
        grid=(x_group.shape[0] // tm,),
        in_specs=in_specs,
        out_specs=pl.BlockSpec((tm, D_MODEL), lambda i: (tile0 + i, 0)),
        input_output_aliases={} if prev_out is None else {0: 0},
        compiler_params=_params("parallel"),
        name="ln_mod",
    )(*args)


def _proj_kernel(h_ref, w_ref, o_ref, wb_ref, *, gate):
    @pl.when(pl.program_id(1) == 0)
    def _():
        wb_ref[...] = w_ref[...].T.astype(BF16)

    if not gate:
        o_ref[...] = jnp.dot(h_ref[...], wb_ref[...], preferred_element_type=F32).astype(o_ref.dtype)
        return
    sub = 2 * LANE
    for c0 in range(0, o_ref.shape[1], sub):
        r = jnp.dot(h_ref[...], wb_ref[:, c0:c0 + sub], preferred_element_type=F32)
        o_ref[:, c0:c0 + sub] = (0.5 * jnp.tanh(0.5 * r) + 0.5).astype(o_ref.dtype)


def _in_proj(h, w_t, layer, col0, n_cols, tn, out_dtype, gate):
    tm = 1024
    return pl.pallas_call(
        functools.partial(_proj_kernel, gate=gate),
        out_shape=jax.ShapeDtypeStruct((T_ALL, n_cols), out_dtype),
        grid=(n_cols // tn, T_ALL // tm),
        in_specs=[
            pl.BlockSpec((tm, D_MODEL), lambda j, i: (i, 0)),
            pl.BlockSpec((pl.Element(tn), pl.Element(D_MODEL)),
                         lambda j, i: (pl.multiple_of(layer * IN_COLS + col0 + j * tn, 8), 0)),
        ],
        out_specs=pl.BlockSpec((tm, tn), lambda j, i: (i, j)),
        scratch_shapes=[pltpu.VMEM((D_MODEL, tn), BF16)],
        compiler_params=_params("parallel", "arbitrary"),
        name="gate_proj" if gate else "in_proj",
    )(h, w_t)


def _hy_filter_kernel(z_ref, dl_ref, w1_ref, b1_ref, w2_ref, b2_ref, w3_ref, fwd_ref,
                      kre_ref, kim_ref, *, L):
    z = z_ref[...]
    a = jnp.sin(jnp.dot(z, w1_ref[...], precision=HIGHEST, preferred_element_type=F32) + b1_ref[...])
    a = jnp.sin(jnp.dot(a, w2_ref[...], precision=HIGHEST, preferred_element_type=F32) + b2_ref[...])
    h = jnp.dot(a, w3_ref[...], precision=HIGHEST, preferred_element_type=F32)
    decay = jnp.exp(-z[:, 0:1] * dl_ref[...])
    not_first = lax.broadcasted_iota(jnp.int32, (L, HY_W), 0) > 0
    sums, diffs = [], []
    for o in range(2):
        fw = h[:, (2 * o) * HY_W:(2 * o + 1) * HY_W] * decay
        bw = jnp.where(not_first, h[:, (2 * o + 1) * HY_W:(2 * o + 2) * HY_W] * decay, 0.0)
        sums.append(fw + bw)
        diffs.append(fw - bw)
    p = _dot(fwd_ref[...], jnp.concatenate(sums, 1))
    q = _dot(fwd_ref[L:2 * L, :], jnp.concatenate(diffs, 1))
    kre_ref[...] = p[0:L]
    first = lax.broadcasted_iota(jnp.int32, (L, 2 * HY_W), 0) == 0
    kim_ref[...] = jnp.where(first, p[L:L + 1], q)


def _hy_filters(L, w1p, b1, w2, b2, w3, fwd):
    zemb, deltas = _hyena_embedding(L)
    out = jax.ShapeDtypeStruct((L, 2 * HY_W), F32)
    return pl.pallas_call(
        functools.partial(_hy_filter_kernel, L=L),
        out_shape=(out, out),
        compiler_params=pltpu.CompilerParams(vmem_limit_bytes=VMEM_LIMIT),
        name=f"hy_filters_{L}",
    )(jnp.asarray(zemb), jnp.asarray(deltas), w1p, b1, w2, b2, w3, fwd)


def _hyena_kernel(*refs, L, aliased, seqs):
    if aliased:
        refs = refs[1:]
    hy_ref, cw_ref, cb_ref, bias_ref, kre_ref, kim_ref, fwd_ref, inv_ref, o_ref = refs
    first = lax.broadcasted_iota(jnp.int32, (L, HY_W), 0) == 0

    def long_conv(u, o):
        uf = _dot(fwd_ref[...], u)
        ure, uim = uf[0:L], uf[L:2 * L]
        kre = kre_ref[:, o * HY_W:(o + 1) * HY_W]
        kim = kim_ref[:, o * HY_W:(o + 1) * HY_W]
        yre = jnp.where(first, ure * kre, ure * kre - uim * kim)
        yim = jnp.where(first, uim * kim, ure * kim + uim * kre)
        y = _dot(inv_ref[...], jnp.concatenate([yre, yim], 0))
        return y + u * bias_ref[o:o + 1, :]

    for g in range(seqs):
        sl = slice(g * L, (g + 1) * L)
        x = hy_ref[sl, :]
        rows = lax.broadcasted_iota(jnp.int32, x.shape, 0)
        prev = jnp.where(rows == 0, 0.0, pltpu.roll(x, 1, axis=0))
        nxt = jnp.where(rows == L - 1, 0.0, pltpu.roll(x, L - 1, axis=0))
        z = prev * cw_ref[0:1, :] + x * cw_ref[1:2, :] + nxt * cw_ref[2:3, :] + cb_ref[...]
        v, x1, x2 = z[:, 0:HY_W], z[:, HY_W:2 * HY_W], z[:, 2 * HY_W:3 * HY_W]
        u = x1 * long_conv(v, 0)
        o_ref[sl, :] = (x2 * long_conv(u, 1)).astype(o_ref.dtype)


def _hyena(z_main, prev_out, layer, L, n_seq, row_block0, conv_w, conv_b, bias, kre, kim, fwd, inv):
    aliased = prev_out is not None
    seqs = CTX_SEQS_PER_STEP if L == SEQ else 1
    row_block0 //= seqs
    in_specs = [
        pl.BlockSpec((seqs * L, 3 * HY_W), lambda b: (row_block0 + b, 0)),
        pl.BlockSpec((None, 3, 3 * HY_W), lambda b: (layer, 0, 0)),
        pl.BlockSpec((None, 1, 3 * HY_W), lambda b: (layer, 0, 0)),
        pl.BlockSpec((None, 2, HY_W), lambda b: (layer, 0, 0)),
        pl.BlockSpec((L, 2 * HY_W), lambda b: (0, 0)),
        pl.BlockSpec((L, 2 * HY_W), lambda b: (0, 0)),
        pl.BlockSpec((2 * L, L), lambda b: (0, 0)),
        pl.BlockSpec((L, 2 * L), lambda b: (0, 0)),
    ]
    args = [z_main, conv_w, conv_b.reshape(DEPTH, 1, 3 * HY_W), bias, kre, kim, fwd, inv]
    if aliased:
        in_specs = [pl.BlockSpec(memory_space=pl.ANY)] + in_specs
        args = [prev_out] + args
    return pl.pallas_call(
        functools.partial(_hyena_kernel, L=L, aliased=aliased, seqs=seqs),
        out_shape=jax.ShapeDtypeStruct((T_ALL, HY_W), BF16),
        grid=(n_seq // seqs,),
        in_specs=in_specs,
        out_specs=pl.BlockSpec((seqs * L, HY_W), lambda b: (row_block0 + b, 0)),
        input_output_aliases={0: 0} if aliased else {},
        compiler_params=_params("parallel"),
        name=f"hyena_{L}",
    )(*args)


def _win_masks():
    lane = lax.broadcasted_iota(jnp.int32, (1, LANE), 1)
    return lane < WIN_HD, lane >= WIN_HD


def _win_head_operands(q, k, v, h):
    lo_mask, hi_mask = _win_masks()
    col = h // 2
    lo = h % 2 == 0
    q128 = jnp.where(lo_mask if lo else hi_mask, q[:, col * LANE:(col + 1) * LANE], 0.0)
    swap = h in (1, 2)
    if swap:
        k = pltpu.roll(k, WIN_HD, axis=1)
        v = pltpu.roll(v, WIN_HD, axis=1)
    return q128, k, v, lo


def _win_ctx_kernel(sink_ref, q_ref, kv_ref, o_ref, *, layer):
    lo_mask, hi_mask = _win_masks()
    scale = WIN_HD ** -0.5
    for g in range(CTX_SEQS_PER_STEP):
        sl = slice(g * SEQ, (g + 1) * SEQ)
        q = q_ref[sl, :]
        k = kv_ref[sl, 0:LANE]
        v = kv_ref[sl, LANE:2 * LANE]
        cols = []
        for col in range(2):
            acc = None
            for h in (2 * col, 2 * col + 1):
                q128, kk, vv, lo = _win_head_operands(q, k, v, h)
                s = _dot_nt(q128, kk) * scale
                sink = sink_ref[layer, h]
                m = jnp.maximum(jnp.max(s, -1, keepdims=True), sink)
                p = jnp.exp(s - m)
                den = jnp.sum(p, -1, keepdims=True) + jnp.exp(sink - m)
                o = _dot(p, vv) / den
                o = jnp.where(lo_mask if lo else hi_mask, o, 0.0)
                acc = o if acc is None else acc + o
            cols.append(acc)
        o_ref[sl, :] = jnp.concatenate(cols, 1).astype(o_ref.dtype)


def _win_ctx(z_main, sink, layer):
    return pl.pallas_call(
        functools.partial(_win_ctx_kernel, layer=layer),
        out_shape=jax.ShapeDtypeStruct((T_ALL, WIN_HEADS * WIN_HD), BF16),
        grid=(BATCH // CTX_SEQS_PER_STEP,),
        in_specs=[
            pl.BlockSpec(memory_space=pltpu.SMEM),
            pl.BlockSpec((CTX_SEQS_PER_STEP * SEQ, 256), lambda b: (b, COL_WQ // 256)),
            pl.BlockSpec((CTX_SEQS_PER_STEP * SEQ, 256), lambda b: (b, COL_WK // 256)),
        ],
        out_specs=pl.BlockSpec((CTX_SEQS_PER_STEP * SEQ, 256), lambda b: (b, 0)),
        compiler_params=_params("parallel"),
        name="win_ctx",
    )(sink, z_main, z_main)


def _win_lat_kernel(sink_ref, prev_ref, q_ref, kv_ref, ck_ref, cv_ref, c_ref, su_ref, sd_ref,
                    o_ref, *, layer):
    del prev_ref
    L = DEC_SEQ
    half = WIN_HD // 2
    c, su, sd = c_ref[...], su_ref[...], sd_ref[...]
    q = jnp.concatenate(
        [_rope128(q_ref[:, i * LANE:(i + 1) * LANE], c, su, sd, half) for i in range(2)], 1)
    k = _rope128(kv_ref[:, 0:LANE], c, su, sd, half)
    v = kv_ref[:, LANE:2 * LANE]
    ck = ck_ref[...]
    cv = cv_ref[...]
    lo_mask, hi_mask = _win_masks()
    scale = WIN_HD ** -0.5
    nb = L // CHUNK
    cols = []
    for col in range(2):
        acc_blocks = [None] * nb
        for h in (2 * col, 2 * col + 1):
            q128, kk, vv, lo = _win_head_operands(q, k, v, h)
            _, ckk, cvv, _ = _win_head_operands(q, ck, cv, h)
            sink = sink_ref[layer, h]
            for n in range(nb):
                k0 = max(0, (n - 1) * CHUNK)
                k1 = min(L, (n + 2) * CHUNK)
                qn = q128[n * CHUNK:(n + 1) * CHUNK]
                s_loc = _dot_nt(qn, kk[k0:k1]) * scale
                qi = n * CHUNK + lax.broadcasted_iota(jnp.int32, s_loc.shape, 0)
                kj = k0 + lax.broadcasted_iota(jnp.int32, s_loc.shape, 1)
                s_loc = jnp.where(jnp.abs(qi - kj) <= WINDOW, s_loc, NEG)
                s_ctx = _dot_nt(qn, ckk) * scale
                m = jnp.maximum(jnp.maximum(jnp.max(s_loc, -1, keepdims=True),
                                            jnp.max(s_ctx, -1, keepdims=True)), sink)
                p_loc = jnp.exp(s_loc - m)
                p_ctx = jnp.exp(s_ctx - m)
                den = (jnp.sum(p_loc, -1, keepdims=True) + jnp.sum(p_ctx, -1, keepdims=True)
                       + jnp.exp(sink - m))
                o = (_dot(p_loc, vv[k0:k1]) + _dot(p_ctx, cvv)) / den
                o = jnp.where(lo_mask if lo else hi_mask, o, 0.0)
                acc_blocks[n] = o if acc_blocks[n] is None else acc_blocks[n] + o
        cols.append(jnp.concatenate(acc_blocks, 0))
    o_ref[...] = jnp.concatenate(cols, 1).astype(o_ref.dtype)


def _win_lat(z_main, prev_out, sink, cache_k, cache_v, layer):
    c, su, sd, _ = _rope_tables(DEC_SEQ, WIN_HD, LANE)
    rb0 = T_CTX // DEC_SEQ
    tab = pl.BlockSpec((DEC_SEQ, LANE), lambda b: (0, 0))
    cache = pl.BlockSpec((None, None, PAST_LEN, LANE), lambda b: (b, layer, 0, 0))
    return pl.pallas_call(
        functools.partial(_win_lat_kernel, layer=layer),
        out_shape=jax.ShapeDtypeStruct((T_ALL, WIN_HEADS * WIN_HD), BF16),
        grid=(DEC_BATCH,),
        in_specs=[
            pl.BlockSpec(memory_space=pltpu.SMEM),
            pl.BlockSpec(memory_space=pl.ANY),
            pl.BlockSpec((DEC_SEQ, 256), lambda b: (rb0 + b, COL_WQ // 256)),
            pl.BlockSpec((DEC_SEQ, 256), lambda b: (rb0 + b, COL_WK // 256)),
            cache, cache, tab, tab, tab,
        ],
        out_specs=pl.BlockSpec((DEC_SEQ, 256), lambda b: (rb0 + b, 0)),
        input_output_aliases={1: 0},
        compiler_params=_params("parallel"),
        name="win_lat",
    )(sink, prev_out, z_main, z_main, cache_k, cache_v,
      jnp.asarray(c), jnp.asarray(su), jnp.asarray(sd))


def _ret_kernel(*refs, L, layer, ctx):
    if ctx:
        (df_ref, db_ref, q_ref, k_ref, v0_ref, v1_ref, g0_ref, g1_ref,
         o_ref, sf_out, sb_out, s_ref, cross_ref) = refs
        seqs = CTX_SEQS_PER_STEP
    else:
        (df_ref, db_ref, prev_ref, q_ref, k_ref, v0_ref, v1_ref, g0_ref, g1_ref, s0f_ref, s0b_ref,
         o_ref, s_ref, cross_ref) = refs
        seqs = 1
    C = CHUNK
    nc = L // C
    H = RET_HEADS
    qw = H * RET_DK
    vw = H * RET_DV

    def lane_table(width, per_head, fn):
        pos = lax.broadcasted_iota(jnp.int32, (C, per_head), 0).astype(F32)
        return jnp.concatenate([fn(h, pos) for h in range(H)], 1)

    def log_gamma(ref, h):
        d = jnp.full((1, 1), ref[layer, h], F32)
        return jnp.log(jax.nn.sigmoid(d))

    lgf = [log_gamma(df_ref, h) for h in range(H)]
    lgb = [log_gamma(db_ref, h) for h in range(H)]

    def tables(lg, reverse):
        if reverse:
            dq = lane_table(vw, RET_DV, lambda h, pos: jnp.exp((C - pos) * lg[h]))
            dk = lane_table(qw, RET_DK, lambda h, pos: jnp.exp(pos * lg[h]))
        else:
            dq = lane_table(vw, RET_DV, lambda h, pos: jnp.exp((pos + 1.0) * lg[h]))
            dk = lane_table(qw, RET_DK, lambda h, pos: jnp.exp((C - 1.0 - pos) * lg[h]))
        dc = jnp.concatenate([jnp.broadcast_to(jnp.exp(C * lg[h]), (1, RET_DV)) for h in range(H)], 1)
        return dq, dk, dc

    tab_f = tables(lgf, False)
    tab_b = tables(lgb, True)
    ii = lax.broadcasted_iota(jnp.int32, (C, C), 0)
    jj = lax.broadcasted_iota(jnp.int32, (C, C), 1)
    diff = (ii - jj).astype(F32)
    dmats = [jnp.where(diff >= 0, jnp.exp(jnp.maximum(diff, 0.0) * lgf[h]), 0.0)
             + jnp.where(diff <= 0, jnp.exp(jnp.maximum(-diff, 0.0) * lgb[h]), 0.0) for h in range(H)]
    lane_q = lax.broadcasted_iota(jnp.int32, (1, qw), 1) // RET_DK

    srow = lax.broadcasted_iota(jnp.int32, (qw, vw), 0) // RET_DK
    scol = lax.broadcasted_iota(jnp.int32, (qw, vw), 1) // RET_DV
    diag = srow == scol

    for g in range(seqs):
        base = g * L
        rows_all = slice(base, base + L)
        q_all = q_ref[rows_all, :]
        k_all = k_ref[rows_all, :] * (RET_DK ** -0.5)
        v_all = jnp.concatenate([v0_ref[rows_all, :], v1_ref[rows_all, :]], 1)
        g_all = jnp.concatenate([g0_ref[rows_all, :], g1_ref[rows_all, :]], 1)

        def scan(tabs, reverse, s0_ref, s_out):
            dq, dk, dc = tabs
            s_ref[g] = jnp.zeros((qw, vw), F32)
            if s0_ref is not None:
                for h in range(H):
                    s_ref[g, h * RET_DK:(h + 1) * RET_DK, h * RET_DV:(h + 1) * RET_DV] = s0_ref[h]
            order = range(nc - 1, -1, -1) if reverse else range(nc)
            for ci in order:
                sl = slice(ci * C, (ci + 1) * C)
                qc, kc, vc = q_all[sl], k_all[sl], v_all[sl]
                st = s_ref[g]
                cross = _dot(qc, st) * dq
                if reverse:
                    cross_ref[g, sl, :] = cross_ref[g, sl, :] + cross
                else:
                    cross_ref[g, sl, :] = cross
                upd = jnp.where(diag, _dot_tn(kc * dk, vc), 0.0)
                s_ref[g] = st * dc + upd
            if s_out is not None:
                for h in range(H):
                    s_out[g, h] = s_ref[g, h * RET_DK:(h + 1) * RET_DK, h * RET_DV:(h + 1) * RET_DV]

        scan(tab_f, False, None if ctx else s0f_ref, sf_out if ctx else None)
        scan(tab_b, True, None if ctx else s0b_ref, sb_out if ctx else None)

        for h in range(H):
            hv = slice(h * RET_DV, (h + 1) * RET_DV)
            for ci in range(nc):
                sl = slice(ci * C, (ci + 1) * C)
                qh = jnp.where(lane_q == h, q_all[sl], 0.0)
                att = _dot_nt(qh, k_all[sl]) * dmats[h]
                o = _dot(att, v_all[sl, hv]) + cross_ref[g, sl, hv]
                gt = g_all[sl, hv]
                o_ref[base + ci * C:base + (ci + 1) * C, hv] = (
                    (gt * jax.nn.sigmoid(gt)) * _layer_norm(o)).astype(o_ref.dtype)


def _retention(z_main, prev_out, dec_f, dec_b, s0f, s0b, layer, ctx):
    L = SEQ if ctx else DEC_SEQ
    n_seq = BATCH if ctx else DEC_BATCH
    seqs = CTX_SEQS_PER_STEP if ctx else 1
    rb0 = 0 if ctx else T_CTX // DEC_SEQ

    def zcol(col):
        return pl.BlockSpec((seqs * L, 256), lambda b: (rb0 + b, col // 256))

    smem = pl.BlockSpec(memory_space=pltpu.SMEM)
    z_specs = [zcol(COL_RQ), zcol(COL_RK), zcol(COL_RV), zcol(COL_RV + 256),
               zcol(COL_RG), zcol(COL_RG + 256)]
    y_shape = jax.ShapeDtypeStruct((T_ALL, RET_HEADS * RET_DV), BF16)
    y_spec = pl.BlockSpec((seqs * L, RET_HEADS * RET_DV), lambda b: (rb0 + b, 0))
    scratch = [pltpu.VMEM((seqs, RET_HEADS * RET_DK, RET_HEADS * RET_DV), F32),
               pltpu.VMEM((seqs, L, RET_HEADS * RET_DV), F32)]
    kern = functools.partial(_ret_kernel, L=L, layer=layer, ctx=ctx)
    if ctx:
        st_shape = jax.ShapeDtypeStruct((BATCH, RET_HEADS, RET_DK, RET_DV), F32)
        st_spec = pl.BlockSpec((seqs, RET_HEADS, RET_DK, RET_DV), lambda b: (b, 0, 0, 0))
        return pl.pallas_call(
            kern,
            out_shape=(y_shape, st_shape, st_shape),
            grid=(n_seq // seqs,),
            in_specs=[smem, smem] + z_specs,
            out_specs=(y_spec, st_spec, st_spec),
            scratch_shapes=scratch,
            compiler_params=_params("parallel"),
            name="ret_ctx",
        )(dec_f, dec_b, *([z_main] * 6))
    s0_spec = pl.BlockSpec((None, None, RET_HEADS, RET_DK, RET_DV), lambda b: (b, layer, 0, 0, 0))
    return pl.pallas_call(
        kern,
        out_shape=y_shape,
        grid=(n_seq,),
        in_specs=[smem, smem, pl.BlockSpec(memory_space=pl.ANY)] + z_specs + [s0_spec, s0_spec],
        out_specs=y_spec,
        scratch_shapes=scratch,
        input_output_aliases={2: 0},
        compiler_params=_params("parallel"),
        name="ret_lat",
    )(dec_f, dec_b, prev_out, *([z_main] * 6), s0f, s0b)


def _rms_norm(x, g):
    return x * lax.rsqrt(jnp.mean(x * x, -1, keepdims=True) + RMS_EPS) * g


def _mla_attend(qn, qr, kn, kr, vv, o_ref, row0):
    scale = (MLA_NOPE + MLA_ROPE) ** -0.5
    lane_n = lax.broadcasted_iota(jnp.int32, (1, MLA_HEADS * MLA_NOPE), 1) // MLA_NOPE
    lane_r = lax.broadcasted_iota(jnp.int32, (1, LANE), 1)
    kr32 = jnp.where(lane_r < MLA_ROPE, kr, 0.0)
    acc = None
    for h in range(MLA_HEADS):
        qnh = jnp.where(lane_n == h, qn, 0.0)
        qrh = qr if h == 0 else pltpu.roll(qr, LANE - h * MLA_ROPE, axis=1)
        qrh = jnp.where(lane_r < MLA_ROPE, qrh, 0.0)
        s = (_dot_nt(qnh, kn) + _dot_nt(qrh, kr32)) * scale
        m = jnp.max(s, -1, keepdims=True)
        p = jnp.exp(s - m)
        den = jnp.sum(p, -1, keepdims=True)
        o = jnp.where(lane_n == h, _dot(p, vv) / den, 0.0)
        acc = o if acc is None else acc + o
    o_ref[row0:row0 + acc.shape[0], :] = acc.astype(o_ref.dtype)


def _mla_ctx_kernel(cq_ref, ckv_ref, kr_ref, qg_ref, kg_ref, wqn_ref, wqr_ref, wk_ref, wv_ref,
                    o_ref, ckvn_ref):
    cqn = _rms_norm(cq_ref[...], qg_ref[...])
    qn = _dot(cqn, wqn_ref[...])
    qr = _dot(cqn, wqr_ref[...])
    ckvn = _rms_norm(ckv_ref[...], kg_ref[...])
    ckvn_ref[...] = ckvn
    kn = _dot(ckvn, wk_ref[...])
    vv = _dot(ckvn, wv_ref[...])
    for g in range(CTX_SEQS_PER_STEP):
        sl = slice(g * SEQ, (g + 1) * SEQ)
        _mla_attend(qn[sl], qr[sl], kn[sl], kr_ref[sl, :], vv[sl], o_ref, g * SEQ)


def _mla_weight_specs(layer):
    return [
        pl.BlockSpec((None, 1, MLA_Q_LORA), lambda b: (layer, 0, 0)),
        pl.BlockSpec((None, 1, MLA_KV_LORA), lambda b: (layer, 0, 0)),
        pl.BlockSpec((None, MLA_Q_LORA, MLA_HEADS * MLA_NOPE), lambda b: (layer, 0, 0)),
        pl.BlockSpec((None, MLA_Q_LORA, MLA_HEADS * MLA_ROPE), lambda b: (layer, 0, 0)),
        pl.BlockSpec((None, MLA_KV_LORA, MLA_HEADS * MLA_NOPE), lambda b: (layer, 0, 0)),
        pl.BlockSpec((None, MLA_KV_LORA, MLA_HEADS * MLA_V), lambda b: (layer, 0, 0)),
    ]


def _mla_ctx(z_main, weights, layer):
    return pl.pallas_call(
        _mla_ctx_kernel,
        out_shape=(jax.ShapeDtypeStruct((T_ALL, MLA_HEADS * MLA_V), BF16),
                   jax.ShapeDtypeStruct((T_CTX, MLA_KV_LORA), F32)),
        grid=(BATCH // CTX_SEQS_PER_STEP,),
        in_specs=[
            pl.BlockSpec((CTX_SEQS_PER_STEP * SEQ, 256), lambda b: (b, COL_CQ // 256)),
            pl.BlockSpec((CTX_SEQS_PER_STEP * SEQ, LANE), lambda b: (b, COL_CKV // LANE)),
            pl.BlockSpec((CTX_SEQS_PER_STEP * SEQ, LANE), lambda b: (b, COL_KROPE // LANE)),
        ] + _mla_weight_specs(layer),
        out_specs=(pl.BlockSpec((CTX_SEQS_PER_STEP * SEQ, 256), lambda b: (b, 0)),
                   pl.BlockSpec((CTX_SEQS_PER_STEP * SEQ, MLA_KV_LORA), lambda b: (b, 0))),
        compiler_params=_params("parallel"),
        name="mla_ctx",
    )(z_main, z_main, z_main, *weights)


def _mla_lat_kernel(prev_ref, cq_ref, ckv_ref, kr_ref, cckv_ref, ckr_ref, c_ref, su_ref, sd_ref,
                    qg_ref, kg_ref, wqn_ref, wqr_ref, wk_ref, wv_ref, o_ref):
    del prev_ref
    half = MLA_ROPE // 2
    c, su, sd = c_ref[...], su_ref[...], sd_ref[...]
    cqn = _rms_norm(cq_ref[...], qg_ref[...])
    qn = _dot(cqn, wqn_ref[...])
    qr = _rope128(_dot(cqn, wqr_ref[...]), c, su, sd, half)
    ckvn = _rms_norm(ckv_ref[...], kg_ref[...])
    ckv_all = jnp.concatenate([ckvn, cckv_ref[...]], 0)
    kn = _dot(ckv_all, wk_ref[...])
    vv = _dot(ckv_all, wv_ref[...])
    kr = jnp.concatenate([_rope128(kr_ref[...], c, su, sd, half), ckr_ref[...]], 0)
    for n in range(DEC_SEQ // 256):
        rows = slice(n * 256, (n + 1) * 256)
        _mla_attend(qn[rows], qr[rows], kn, kr, vv, o_ref, n * 256)


def _mla_lat(z_main, prev_out, cache_ckv, cache_kr_pad, weights, layer):
    c, su, sd, _ = _rope_tables(DEC_SEQ, MLA_ROPE, LANE)
    rb0 = T_CTX // DEC_SEQ
    tab = pl.BlockSpec((DEC_SEQ, LANE), lambda b: (0, 0))
    cache = pl.BlockSpec((None, None, PAST_LEN, LANE), lambda b: (b, layer, 0, 0))
    return pl.pallas_call(
        _mla_lat_kernel,
        out_shape=jax.ShapeDtypeStruct((T_ALL, MLA_HEADS * MLA_V), BF16),
        grid=(DEC_BATCH,),
        in_specs=[
            pl.BlockSpec(memory_space=pl.ANY),
            pl.BlockSpec((DEC_SEQ, 256), lambda b: (rb0 + b, COL_CQ // 256)),
            pl.BlockSpec((DEC_SEQ, LANE), lambda b: (rb0 + b, COL_CKV // LANE)),
            pl.BlockSpec((DEC_SEQ, LANE), lambda b: (rb0 + b, COL_KROPE // LANE)),
            cache, cache, tab, tab, tab,
        ] + _mla_weight_specs(layer),
        out_specs=pl.BlockSpec((DEC_SEQ, 256), lambda b: (rb0 + b, 0)),
        input_output_aliases={0: 0},
        compiler_params=_params("parallel"),
        name="mla_lat",
    )(prev_out, z_main, z_main, z_main, cache_ckv, cache_kr_pad,
      jnp.asarray(c), jnp.asarray(su), jnp.asarray(sd), *weights)


def _route(logits_t, rb):
    scores = jax.nn.sigmoid(logits_t)
    biased = scores + rb
    sc = [scores[e:e + 1, :] for e in range(N_EXPERTS)]
    bi = [biased[e:e + 1, :] for e in range(N_EXPERTS)]
    epg = EXPERTS_PER_GROUP
    gsum = []
    for g in range(N_GROUPS):
        v = bi[g * epg:(g + 1) * epg]
        best = None
        for a in range(epg):
            for b in range(a + 1, epg):
                pair = v[a] + v[b]
                best = pair if best is None else jnp.maximum(best, pair)
        gsum.append(best)
    combine = []
    sel = []
    for g in range(N_GROUPS):
        is_best = None
        for g2 in range(N_GROUPS):
            if g2 == g:
                continue
            c = gsum[g] > gsum[g2] if g2 < g else gsum[g] >= gsum[g2]
            is_best = c if is_best is None else jnp.logical_and(is_best, c)
        for a in range(epg):
            e = g * epg + a
            rank = jnp.zeros_like(bi[e])
            for b in range(epg):
                if b == a:
                    continue
                e2 = g * epg + b
                ahead = bi[e2] >= bi[e] if b < a else bi[e2] > bi[e]
                rank = rank + jnp.where(ahead, 1.0, 0.0)
            sel.append(jnp.logical_and(is_best, rank < 2.0))
    wsum = None
    for e in range(N_EXPERTS):
        w = jnp.where(sel[e], sc[e], 0.0)
        wsum = w if wsum is None else wsum + w
    for e in range(N_EXPERTS):
        combine.append(jnp.where(sel[e], ROUTE_SCALE * sc[e] / wsum, 0.0))
    return jnp.concatenate(combine, 0)


def _merge_kernel(ya_ref, yb_ref, yc_ref, yd_ref, gt_ref, xc_ref, xl_ref, m_ref,
                  wa_ref, wb_ref, wc_ref, wd_ref, wo_ref, g_ref, b_ref, rwh_ref, rwl_ref, rb_ref,
                  x1_ref, h2_ref, cmb_ref, *, ctx_tiles, sub_rows):
    D = D_MODEL
    g1 = m_ref[:, 2 * D:3 * D]
    s2 = m_ref[:, 3 * D:4 * D]
    sc2 = m_ref[:, 4 * D:5 * D]
    is_ctx = pl.program_id(0) < ctx_tiles
    branches = ((ya_ref, wa_ref), (yb_ref, wb_ref), (yc_ref, wc_ref), (yd_ref, wd_ref))
    for r0 in range(0, x1_ref.shape[0], sub_rows):
        rows = slice(r0, r0 + sub_rows)
        merged = None
        for i, (y_ref, w) in enumerate(branches):
            t = gt_ref[rows, i * D:(i + 1) * D].astype(F32) * jnp.dot(
                y_ref[rows, :], w[...], preferred_element_type=F32)
            merged = t if merged is None else merged + t
        out1 = jnp.dot(merged.astype(BF16), wo_ref[...], preferred_element_type=F32)
        x = jnp.where(is_ctx, xc_ref[rows, :], xl_ref[rows, :])
        x1 = _layer_norm(ALPHA * x + g1 * out1) * g_ref[...] + b_ref[...]
        x1_ref[rows, :] = x1
        h2 = _layer_norm(x1) * (1.0 + sc2) + s2
        h2_hi = h2.astype(BF16)
        h2_ref[rows, :] = h2_hi
        h2_lo = (h2 - h2_hi.astype(F32)).astype(BF16)
        logits = (jnp.dot(h2_hi, rwh_ref[...], preferred_element_type=F32)
                  + (jnp.dot(h2_lo, rwh_ref[...], preferred_element_type=F32)
                     + jnp.dot(h2_hi, rwl_ref[...], preferred_element_type=F32)))
        cmb_ref[:, rows] = _route(logits.T[0:N_EXPERTS], rb_ref[...])


def _merge(ya, yb, yc, yd, gates, x_ctx, x_lat, mods_l, w_br, w_out_bf, ln1_g, ln1_b,
           router_w_parts, router_b, layer):
    tm = 512
    row = _mod_row(tm)
    D = D_MODEL
    ctx_tiles = T_CTX // tm

    def tile(w):
        return pl.BlockSpec((tm, w), lambda i: (i, 0))

    def weight(k, n):
        return pl.BlockSpec((None, k, n), lambda i: (layer, 0, 0))

    return pl.pallas_call(
        functools.partial(_merge_kernel, ctx_tiles=ctx_tiles, sub_rows=256),
        out_shape=(jax.ShapeDtypeStruct((T_ALL, D), F32),
                   jax.ShapeDtypeStruct((T_ALL, D), BF16),
                   jax.ShapeDtypeStruct((N_EXPERTS, T_ALL), F32)),
        grid=(T_ALL // tm,),
        in_specs=[
            tile(256), tile(256), tile(512), tile(256), tile(4 * D),
            pl.BlockSpec((tm, D), lambda i: (jnp.minimum(i, ctx_tiles - 1), 0)),
            pl.BlockSpec((tm, D), lambda i: (jnp.maximum(i - ctx_tiles, 0), 0)),
            pl.BlockSpec((None, 1, 6 * D), lambda i: (row(i), 0, 0)),
            weight(256, D), weight(256, D), weight(512, D), weight(256, D), weight(D, D),
            weight(1, D), weight(1, D),
            pl.BlockSpec((D, LANE), lambda i: (0, 0)),
            pl.BlockSpec((D, LANE), lambda i: (0, 0)),
            pl.BlockSpec((N_EXPERTS, 1), lambda i: (0, 0)),
        ],
        out_specs=(tile(D), tile(D), pl.BlockSpec((N_EXPERTS, tm), lambda i: (0, i))),
        compiler_params=_params("parallel"),
        name="merge",
    )(ya, yb, yc, yd, gates, x_ctx, x_lat, mods_l, *w_br, w_out_bf,
      ln1_g.reshape(DEPTH, 1, D), ln1_b.reshape(DEPTH, 1, D), *router_w_parts,
      router_b.reshape(N_EXPERTS, 1))


MOE_EXPERTS_PER_STEP = 2


def _moe_kernel(h_ref, c_ref, x1_ref, m_ref, wg_ref, wu_ref, wd_ref, g_ref, b_ref, o_ref, acc_ref):
    eg = pl.program_id(1)

    @pl.when(eg == 0)
    def _():
        acc_ref[...] = jnp.zeros_like(acc_ref)

    h = h_ref[...]
    cmb = c_ref[...]
    lane = lax.broadcasted_iota(jnp.int32, cmb.shape, 1)
    hid = []
    for k in range(MOE_EXPERTS_PER_STEP):
        gate = jnp.dot(h, wg_ref[k].astype(BF16), preferred_element_type=F32)
        up = jnp.dot(h, wu_ref[k].astype(BF16), preferred_element_type=F32)
        e = eg * MOE_EXPERTS_PER_STEP + k
        ce = jnp.sum(jnp.where(lane == e, cmb, 0.0), -1, keepdims=True)
        hid.append(((gate * jax.nn.sigmoid(gate)) * up * ce).astype(BF16))
    wd = wd_ref[...].reshape(MOE_EXPERTS_PER_STEP * D_EXPERT, D_MODEL).astype(BF16)
    acc_ref[...] += jnp.dot(jnp.concatenate(hid, 1), wd, preferred_element_type=F32)

    @pl.when(eg == N_EXPERTS // MOE_EXPERTS_PER_STEP - 1)
    def _():
        g2 = m_ref[:, 5 * D_MODEL:6 * D_MODEL]
        y = _layer_norm(ALPHA * x1_ref[...] + g2 * acc_ref[...])
        o_ref[...] = y * g_ref[...] + b_ref[...]


def _moe(h2, combine, x1, mods_l, w_gate, w_up, w_down, ln2_g, ln2_b, layer, row0, n_rows):
    tm = 1024
    row = _mod_row(tm)
    D = D_MODEL
    t0 = row0 // tm
    eps = MOE_EXPERTS_PER_STEP
    return pl.pallas_call(
        _moe_kernel,
        out_shape=jax.ShapeDtypeStruct((n_rows, D), F32),
        grid=(n_rows // tm, N_EXPERTS // eps),
        in_specs=[
            pl.BlockSpec((tm, D), lambda i, e: (t0 + i, 0)),
            pl.BlockSpec((tm, N_EXPERTS), lambda i, e: (t0 + i, 0)),
            pl.BlockSpec((tm, D), lambda i, e: (t0 + i, 0)),
            pl.BlockSpec((None, 1, 6 * D), lambda i, e: (row(t0 + i), 0, 0)),
            pl.BlockSpec((None, eps, D, D_EXPERT), lambda i, e: (layer, e, 0, 0)),
            pl.BlockSpec((None, eps, D, D_EXPERT), lambda i, e: (layer, e, 0, 0)),
            pl.BlockSpec((None, eps, D_EXPERT, D), lambda i, e: (layer, e, 0, 0)),
            pl.BlockSpec((None, 1, D), lambda i, e: (layer, 0, 0)),
            pl.BlockSpec((None, 1, D), lambda i, e: (layer, 0, 0)),
        ],
        out_specs=pl.BlockSpec((tm, D), lambda i, e: (i, 0)),
        scratch_shapes=[pltpu.VMEM((tm, D), F32)],
        compiler_params=_params("parallel", "arbitrary"),
        name="moe",
    )(h2, combine, x1, mods_l, w_gate, w_up, w_down,
      ln2_g.reshape(DEPTH, 1, D), ln2_b.reshape(DEPTH, 1, D))


def kernel(x_prompt, x_sample, cache_win_k, cache_win_v, cache_mla_ckv, cache_mla_krope,
           state_ret_fwd, state_ret_bwd, c, c_ctx, w_ada, b_ada, w_in,
           hy_conv_w, hy_conv_b, hy_w1, hy_b1, hy_w2, hy_b2, hy_w3, hy_bias,
           win_sink, ret_decay_fwd, ret_decay_bwd, mla_q_norm, mla_kv_norm, mla_w_uq, mla_w_ukv,
           w_br_a, w_br_b, w_br_c, w_br_d, w_out, ln1_g, ln1_b, ln2_g, ln2_b,
           router_w, router_b, moe_w_gate, moe_w_up, moe_w_down):
    D = D_MODEL
    x_ctx = x_prompt.reshape(T_CTX, D)
    x_lat = x_sample.reshape(T_LAT, D)

    cvec = jnp.zeros((ADA_ROWS, D), F32).at[0].set(c_ctx).at[1:1 + DEC_BATCH].set(c)
    mods = _ada_mods(cvec, w_ada, b_ada)[:, :1 + DEC_BATCH].reshape(DEPTH, 1 + DEC_BATCH, 1, 6 * D)

    w_in_t = jnp.swapaxes(w_in, 1, 2).reshape(DEPTH * IN_COLS, D)
    cache_k = cache_win_k.reshape(DEC_BATCH, DEPTH, PAST_LEN, WIN_KV_HEADS * WIN_HD)
    cache_v = cache_win_v.reshape(DEC_BATCH, DEPTH, PAST_LEN, WIN_KV_HEADS * WIN_HD)
    cache_kr = jnp.pad(cache_mla_krope, ((0, 0), (0, 0), (0, 0), (0, LANE - MLA_ROPE)))

    uq = mla_w_uq.reshape(DEPTH, MLA_Q_LORA, MLA_HEADS, MLA_NOPE + MLA_ROPE)
    ukv = mla_w_ukv.reshape(DEPTH, MLA_KV_LORA, MLA_HEADS, MLA_NOPE + MLA_V)
    mla_weights = (
        mla_q_norm.reshape(DEPTH, 1, MLA_Q_LORA),
        mla_kv_norm.reshape(DEPTH, 1, MLA_KV_LORA),
        uq[..., :MLA_NOPE].reshape(DEPTH, MLA_Q_LORA, MLA_HEADS * MLA_NOPE),
        uq[..., MLA_NOPE:].reshape(DEPTH, MLA_Q_LORA, MLA_HEADS * MLA_ROPE),
        ukv[..., :MLA_NOPE].reshape(DEPTH, MLA_KV_LORA, MLA_HEADS * MLA_NOPE),
        ukv[..., MLA_NOPE:].reshape(DEPTH, MLA_KV_LORA, MLA_HEADS * MLA_V),
    )

    hy_w1p = jnp.pad(hy_w1, ((0, 0), (0, LANE - HY_EMB), (0, 0)))
    dft = {}
    for L in (SEQ, DEC_SEQ):
        fwd, inv = _dft_tables(L)
        dft[L] = (jnp.asarray(fwd).astype(BF16), jnp.asarray(inv).astype(BF16))
    router_w_pad = jnp.pad(router_w, ((0, 0), (0, LANE - N_EXPERTS)))
    router_w_hi = router_w_pad.astype(BF16)
    router_w_parts = (router_w_hi, (router_w_pad - router_w_hi.astype(F32)).astype(BF16))
    w_br = tuple(w.astype(BF16) for w in (w_br_a, w_br_b, w_br_c, w_br_d))
    w_out_bf = w_out.astype(BF16)

    new_k, new_v, new_ckv, new_kr, new_sf, new_sb = [], [], [], [], [], []
    for l in range(DEPTH):
        mods_l = mods[l]
        h = _ln_mod(x_ctx, None, mods_l, 0)
        h = _ln_mod(x_lat, h, mods_l, T_CTX)
        z = _in_proj(h, w_in_t, l, 0, Z_MAIN, Z_MAIN // 2, F32, gate=False)
        gates = _in_proj(h, w_in_t, l, COL_GATE, 4 * D, D, BF16, gate=True)

        ya = None
        for L, n_seq, rb0 in ((SEQ, BATCH, 0), (DEC_SEQ, DEC_BATCH, T_CTX // DEC_SEQ)):
            fwd, inv = dft[L]
            kre, kim = _hy_filters(L, hy_w1p[l], hy_b1[l][None], hy_w2[l], hy_b2[l][None], hy_w3[l], fwd)
            ya = _hyena(z, ya, l, L, n_seq, rb0, hy_conv_w, hy_conv_b, hy_bias, kre, kim, fwd, inv)

        yb = _win_ctx(z, win_sink, l)
        yb = _win_lat(z, yb, win_sink, cache_k, cache_v, l)

        yc, sf, sb = _retention(z, None, ret_decay_fwd, ret_decay_bwd, None, None, l, ctx=True)
        yc = _retention(z, yc, ret_decay_fwd, ret_decay_bwd, state_ret_fwd, state_ret_bwd, l, ctx=False)

        yd, ckvn = _mla_ctx(z, mla_weights, l)
        yd = _mla_lat(z, yd, cache_mla_ckv, cache_kr, mla_weights, l)

        x1, h2, combine_t = _merge(ya, yb, yc, yd, gates, x_ctx, x_lat, mods_l, w_br, w_out_bf,
                                   ln1_g, ln1_b, router_w_parts, router_b, l)
        moe_args = (h2, combine_t.T, x1, mods_l, moe_w_gate, moe_w_up, moe_w_down, ln2_g, ln2_b, l)
        x_ctx = _moe(*moe_args, 0, T_CTX)
        x_lat = _moe(*moe_args, T_CTX, T_LAT)

        zc = z[:T_CTX]
        new_k.append(zc[:, COL_WK:COL_WK + 128].reshape(BATCH, SEQ, WIN_KV_HEADS, WIN_HD))
        new_v.append(zc[:, COL_WV:COL_WV + 128].reshape(BATCH, SEQ, WIN_KV_HEADS, WIN_HD))
        new_ckv.append(ckvn.reshape(BATCH, SEQ, MLA_KV_LORA))
        new_kr.append(zc[:, COL_KROPE:COL_KROPE + MLA_ROPE].reshape(BATCH, SEQ, MLA_ROPE))
        new_sf.append(sf)
        new_sb.append(sb)

    y_prompt = x_ctx.reshape(BATCH, SEQ, D)
    y_sample = x_lat.reshape(DEC_BATCH, DEC_SEQ, D)
    return (y_prompt, y_sample, jnp.stack(new_k, 1), jnp.stack(new_v, 1), jnp.stack(new_ckv, 1),
            jnp.stack(new_kr, 1), jnp.stack(new_sf, 1), jnp.stack(new_sb, 1))
```

```python
import functools
import math

import numpy as np
import jax
import jax.numpy as jnp
from jax import lax
from jax.experimental import pallas as pl
from jax.experimental.pallas import tpu as pltpu

F32 = jnp.float32
BF16 = jnp.bfloat16
HIGHEST = lax.Precision.HIGHEST

D_MODEL = 1024
BATCH = 16
SEQ = 256
DEPTH = 2
DEC_BATCH = 2
DEC_SEQ = 1024
PAST_LEN = 256
GRID_W = 64
CHUNK = 128
ROPE_BASE = 10000.0
NEG = -1e30
LN_EPS = 1e-5
RMS_EPS = 1e-6

HY_W = 256
HY_BANDS = 16
HY_EMB = 1 + 2 * HY_BANDS
HY_FFN = 64
HY_FAST_DECAY = 0.3
HY_SLOW_DECAY = 1.5
HY_TARGET = 1e-2

WIN_HEADS = 4
WIN_KV_HEADS = 2
WIN_HD = 64
WINDOW = 128

RET_HEADS = 4
RET_DK = 64
RET_DV = 128

MLA_HEADS = 4
MLA_Q_LORA = 256
MLA_KV_LORA = 128
MLA_NOPE = 64
MLA_ROPE = 32
MLA_V = 64

N_EXPERTS = 16
N_GROUPS = 4
EXPERTS_PER_GROUP = N_EXPERTS // N_GROUPS
D_EXPERT = 256
ROUTE_SCALE = 2.5

ALPHA = (2.0 * DEPTH) ** 0.25

T_CTX = BATCH * SEQ
T_LAT = DEC_BATCH * DEC_SEQ
T_ALL = T_CTX + T_LAT

COL_HY = 0
COL_WQ = 768
COL_WK = 1024
COL_WV = 1152
COL_RQ = 1280
COL_RK = 1536
COL_RV = 1792
COL_RG = 2304
COL_CQ = 2816
COL_CKV = 3072
COL_KROPE = 3200
COL_GATE = 3232
IN_COLS = COL_GATE + 4 * D_MODEL
Z_MAIN = 3328

LANE = 128
CTX_SEQS_PER_STEP = 4
VMEM_LIMIT = 56 * 1024 * 1024


def _params(*sem):
    return pltpu.CompilerParams(dimension_semantics=sem, vmem_limit_bytes=VMEM_LIMIT)


def _dot(a, b):
    return jnp.dot(a.astype(BF16), b.astype(BF16), preferred_element_type=F32)


def _dot_nt(a, b):
    return lax.dot_general(a.astype(BF16), b.astype(BF16), (((1,), (1,)), ((), ())),
                           preferred_element_type=F32)


def _dot_tn(a, b):
    return lax.dot_general(a.astype(BF16), b.astype(BF16), (((0,), (0,)), ((), ())),
                           preferred_element_type=F32)


def _layer_norm(x):
    mu = jnp.mean(x, -1, keepdims=True)
    xc = x - mu
    var = jnp.mean(xc * xc, -1, keepdims=True)
    return xc * lax.rsqrt(var + LN_EPS)


def _mod_row(tile_rows):
    def row(i):
        start = i * tile_rows
        return jnp.where(start < T_CTX, 0, 1 + (start - T_CTX) // DEC_SEQ)
    return row


@functools.lru_cache(maxsize=None)
def _dft_tables(L):
    f = np.arange(L, dtype=np.int64)[:, None]
    s = np.arange(L, dtype=np.int64)[None, :]
    ang = np.pi * ((f * s) % (2 * L)).astype(np.float64) / L
    cos = np.cos(ang)
    sin = np.sin(ang)
    alt = np.where(np.arange(L) % 2 == 0, 1.0, -1.0)
    fwd_im = -sin
    fwd_im[0, :] = alt
    fwd = np.concatenate([cos, fwd_im], 0)
    inv_re = cos.T / L
    inv_re[:, 0] = 1.0 / (2 * L)
    inv_im = -sin.T / L
    inv_im[:, 0] = alt / (2 * L)
    inv = np.concatenate([inv_re, inv_im], 1)
    return fwd.astype(np.float32), inv.astype(np.float32)


@functools.lru_cache(maxsize=None)
def _hyena_embedding(L):
    t01 = np.linspace(0.0, 1.0, L, dtype=np.float64)[:, None]
    bands = np.linspace(1e-4, HY_BANDS - 1, HY_BANDS, dtype=np.float64)
    ang = (2.0 * math.pi / L) * np.arange(L, dtype=np.float64)[:, None] * bands[None, :]
    z = np.concatenate([t01, np.cos(ang), -np.sin(ang)], -1)
    zp = np.zeros((L, LANE), np.float64)
    zp[:, :HY_EMB] = z
    deltas = np.abs(np.linspace(math.log(HY_TARGET) / HY_SLOW_DECAY,
                                math.log(HY_TARGET) / HY_FAST_DECAY, HY_W, dtype=np.float64))
    return zp.astype(np.float32), deltas[None, :].astype(np.float32)


@functools.lru_cache(maxsize=None)
def _rope_tables(L, rot_dim, width):
    rows = L // GRID_W
    n_freq = rot_dim // 4
    half = rot_dim // 2
    inv = ROPE_BASE ** (-np.arange(n_freq, dtype=np.float64) / n_freq)
    pos = np.arange(L)
    row = (pos // GRID_W).astype(np.float64)
    col = (pos % GRID_W).astype(np.float64)
    ang = np.concatenate([row[:, None] * inv, col[:, None] * inv], -1)
    cos, sin = np.cos(ang), np.sin(ang)
    zero = np.zeros_like(sin)
    c = np.tile(np.concatenate([cos, cos], -1), (1, width // rot_dim))
    s_up = np.tile(np.concatenate([-sin, zero], -1), (1, width // rot_dim))
    s_dn = np.tile(np.concatenate([zero, sin], -1), (1, width // rot_dim))
    return c.astype(np.float32), s_up.astype(np.float32), s_dn.astype(np.float32), half


def _rope128(x, c, s_up, s_dn, half):
    up = pltpu.roll(x, LANE - half, axis=1)
    dn = pltpu.roll(x, half, axis=1)
    return x * c + up * s_up + dn * s_dn


def _ada_kernel(c_ref, w_ref, b_ref, o_ref):
    cv = c_ref[...]
    s = cv * jax.nn.sigmoid(cv)
    s_hi = s.astype(BF16)
    s_lo = (s - s_hi.astype(F32)).astype(BF16)
    w = w_ref[...]
    w_hi = w.astype(BF16)
    w_lo = (w - w_hi.astype(F32)).astype(BF16)
    rows = s.shape[0]
    both = jnp.dot(jnp.concatenate([s_hi, s_lo], 0), w_hi, preferred_element_type=F32)
    o_ref[...] = (both[0:rows] + (both[rows:2 * rows] + jnp.dot(s_hi, w_lo, preferred_element_type=F32))
                  + b_ref[...])


ADA_ROWS = 16


def _ada_mods(cvec, w_ada, b_ada):
    tn = 1536
    n = 6 * D_MODEL
    return pl.pallas_call(
        _ada_kernel,
        out_shape=jax.ShapeDtypeStruct((DEPTH, ADA_ROWS, n), F32),
        grid=(DEPTH, n // tn),
        in_specs=[
            pl.BlockSpec((ADA_ROWS, D_MODEL), lambda l, j: (0, 0)),
            pl.BlockSpec((None, D_MODEL, tn), lambda l, j: (l, 0, j)),
            pl.BlockSpec((None, 1, tn), lambda l, j: (l, 0, j)),
        ],
        out_specs=pl.BlockSpec((None, ADA_ROWS, tn), lambda l, j: (l, 0, j)),
        compiler_params=_params("parallel", "parallel"),
        name="ada_mods",
    )(cvec, w_ada, b_ada.reshape(DEPTH, 1, n))


def _lnmod_kernel(*refs):
    x_ref, m_ref, h_ref = refs[-3:]
    y = _layer_norm(x_ref[...])
    s1 = m_ref[:, 0:D_MODEL]
    sc1 = m_ref[:, D_MODEL:2 * D_MODEL]
    h_ref[...] = (y * (1.0 + sc1) + s1).astype(h_ref.dtype)


def _ln_mod(x_group, prev_out, mods_l, row0):
    tm = 512
    row = _mod_row(tm)
    tile0 = row0 // tm
    in_specs = [
        pl.BlockSpec((tm, D_MODEL), lambda i: (i, 0)),
        pl.BlockSpec((None, 1, 6 * D_MODEL), lambda i: (row(tile0 + i), 0, 0)),
    ]
    args = [x_group, mods_l]
    if prev_out is not None:
        in_specs = [pl.BlockSpec(memory_space=pl.ANY)] + in_specs
        args = [prev_out] + args
    return pl.pallas_call(
        _lnmod_kernel,
        out_shape=jax.ShapeDtypeStruct((T_ALL, D_MODEL), BF16),
        grid=(x_group.shape[0] // tm,),
        in_specs=in_specs,
        out_specs=pl.BlockSpec((tm, D_MODEL), lambda i: (tile0 + i, 0)),
        input_output_aliases={} if prev_out is None else {0: 0},
        compiler_params=_params("parallel"),
        name="ln_mod",
    )(*args)


def _proj_kernel(h_ref, w_ref, o_ref, wb_ref, *, gate):
    @pl.when(pl.program_id(1) == 0)
    def _():
        wb_ref[...] = w_ref[...].T.astype(BF16)

    if not gate:
        o_ref[...] = jnp.dot(h_ref[...], wb_ref[...], preferred_element_type=F32).astype(o_ref.dtype)
        return
    sub = 2 * LANE
    for c0 in range(0, o_ref.shape[1], sub):
        r = jnp.dot(h_ref[...], wb_ref[:, c0:c0 + sub], preferred_element_type=F32)
        o_ref[:, c0:c0 + sub] = (0.5 * jnp.tanh(0.5 * r) + 0.5).astype(o_ref.dtype)


def _in_proj(h, w_t, layer, col0, n_cols, tn, out_dtype, gate):
    tm = 1024
    return pl.pallas_call(
        functools.partial(_proj_kernel, gate=gate),
        out_shape=jax.ShapeDtypeStruct((T_ALL, n_cols), out_dtype),
        grid=(n_cols // tn, T_ALL // tm),
        in_specs=[
            pl.BlockSpec((tm, D_MODEL), lambda j, i: (i, 0)),
            pl.BlockSpec((pl.Element(tn), pl.Element(D_MODEL)),
                         lambda j, i: (pl.multiple_of(layer * IN_COLS + col0 + j * tn, 8), 0)),
        ],
        out_specs=pl.BlockSpec((tm, tn), lambda j, i: (i, j)),
        scratch_shapes=[pltpu.VMEM((D_MODEL, tn), BF16)],
        compiler_params=_params("parallel", "arbitrary"),
        name="gate_proj" if gate else "in_proj",
    )(h, w_t)


def _hy_filter_kernel(z_ref, dl_ref, w1_ref, b1_ref, w2_ref, b2_ref, w3_ref, fwd_ref,
                      kre_ref, kim_ref, *, L):
    z = z_ref[...]
    a = jnp.sin(jnp.dot(z, w1_ref[...], precision=HIGHEST, preferred_element_type=F32) + b1_ref[...])
    a = jnp.sin(jnp.dot(a, w2_ref[...], precision=HIGHEST, preferred_element_type=F32) + b2_ref[...])
    h = jnp.dot(a, w3_ref[...], precision=HIGHEST, preferred_element_type=F32)
    decay = jnp.exp(-z[:, 0:1] * dl_ref[...])
    not_first = lax.broadcasted_iota(jnp.int32, (L, HY_W), 0) > 0
    sums, diffs = [], []
    for o in range(2):
        fw = h[:, (2 * o) * HY_W:(2 * o + 1) * HY_W] * decay
        bw = jnp.where(not_first, h[:, (2 * o + 1) * HY_W:(2 * o + 2) * HY_W] * decay, 0.0)
        sums.append(fw + bw)
        diffs.append(fw - bw)
    p = _dot(fwd_ref[...], jnp.concatenate(sums, 1))
    q = _dot(fwd_ref[L:2 * L, :], jnp.concatenate(diffs, 1))
    kre_ref[...] = p[0:L]
    first = lax.broadcasted_iota(jnp.int32, (L, 2 * HY_W), 0) == 0
    kim_ref[...] = jnp.where(first, p[L:L + 1], q)


def _hy_filters(L, w1p, b1, w2, b2, w3, fwd):
    zemb, deltas = _hyena_embedding(L)
    out = jax.ShapeDtypeStruct((L, 2 * HY_W), F32)
    return pl.pallas_call(
        functools.partial(_hy_filter_kernel, L=L),
        out_shape=(out, out),
        compiler_params=pltpu.CompilerParams(vmem_limit_bytes=VMEM_LIMIT),
        name=f"hy_filters_{L}",
    )(jnp.asarray(zemb), jnp.asarray(deltas), w1p, b1, w2, b2, w3, fwd)


def _hyena_kernel(*refs, L, aliased, seqs):
    if aliased:
        refs = refs[1:]
    hy_ref, cw_ref, cb_ref, bias_ref, kre_ref, kim_ref, fwd_ref, inv_ref, o_ref = refs
    first = lax.broadcasted_iota(jnp.int32, (L, HY_W), 0) == 0

    def long_conv(u, o):
        uf = _dot(fwd_ref[...], u)
        ure, uim = uf[0:L], uf[L:2 * L]
        kre = kre_ref[:, o * HY_W:(o + 1) * HY_W]
        kim = kim_ref[:, o * HY_W:(o + 1) * HY_W]
        yre = jnp.where(first, ure * kre, ure * kre - uim * kim)
        yim = jnp.where(first, uim * kim, ure * kim + uim * kre)
        y = _dot(inv_ref[...], jnp.concatenate([yre, yim], 0))
        return y + u * bias_ref[o:o + 1, :]

    for g in range(seqs):
        sl = slice(g * L, (g + 1) * L)
        x = hy_ref[sl, :].astype(F32)
        rows = lax.broadcasted_iota(jnp.int32, x.shape, 0)
        prev = jnp.where(rows == 0, 0.0, pltpu.roll(x, 1, axis=0))
        nxt = jnp.where(rows == L - 1, 0.0, pltpu.roll(x, L - 1, axis=0))
        z = prev * cw_ref[0:1, :] + x * cw_ref[1:2, :] + nxt * cw_ref[2:3, :] + cb_ref[...]
        v, x1, x2 = z[:, 0:HY_W], z[:, HY_W:2 * HY_W], z[:, 2 * HY_W:3 * HY_W]
        u = x1 * long_conv(v, 0)
        o_ref[sl, :] = (x2 * long_conv(u, 1)).astype(o_ref.dtype)


def _hyena(z_main, prev_out, layer, L, n_seq, row_block0, conv_w, conv_b, bias, kre, kim, fwd, inv):
    aliased = prev_out is not None
    seqs = CTX_SEQS_PER_STEP if L == SEQ else 1
    row_block0 //= seqs
    in_specs = [
        pl.BlockSpec((seqs * L, 3 * HY_W), lambda b: (row_block0 + b, 0)),
        pl.BlockSpec((None, 3, 3 * HY_W), lambda b: (layer, 0, 0)),
        pl.BlockSpec((None, 1, 3 * HY_W), lambda b: (layer, 0, 0)),
        pl.BlockSpec((None, 2, HY_W), lambda b: (layer, 0, 0)),
        pl.BlockSpec((L, 2 * HY_W), lambda b: (0, 0)),
        pl.BlockSpec((L, 2 * HY_W), lambda b: (0, 0)),
        pl.BlockSpec((2 * L, L), lambda b: (0, 0)),
        pl.BlockSpec((L, 2 * L), lambda b: (0, 0)),
    ]
    args = [z_main, conv_w, conv_b.reshape(DEPTH, 1, 3 * HY_W), bias, kre, kim, fwd, inv]
    if aliased:
        in_specs = [pl.BlockSpec(memory_space=pl.ANY)] + in_specs
        args = [prev_out] + args
    return pl.pallas_call(
        functools.partial(_hyena_kernel, L=L, aliased=aliased, seqs=seqs),
        out_shape=jax.ShapeDtypeStruct((T_ALL, HY_W), BF16),
        grid=(n_seq // seqs,),
        in_specs=in_specs,
        out_specs=pl.BlockSpec((seqs * L, HY_W), lambda b: (row_block0 + b, 0)),
        input_output_aliases={0: 0} if aliased else {},
        compiler_params=_params("parallel"),
        name=f"hyena_{L}",
    )(*args)


def _win_masks():
    lane = lax.broadcasted_iota(jnp.int32, (1, LANE), 1)
    return lane < WIN_HD, lane >= WIN_HD


def _win_head_operands(q, k, v, h):
    lo_mask, hi_mask = _win_masks()
    col = h // 2
    lo = h % 2 == 0
    q128 = jnp.where(lo_mask if lo else hi_mask, q[:, col * LANE:(col + 1) * LANE], 0.0)
    swap = h in (1, 2)
    if swap:
        k = pltpu.roll(k, WIN_HD, axis=1)
        v = pltpu.roll(v, WIN_HD, axis=1)
    return q128, k, v, lo


def _win_ctx_kernel(sink_ref, q_ref, kv_ref, o_ref, *, layer):
    lo_mask, hi_mask = _win_masks()
    scale = WIN_HD ** -0.5
    for g in range(CTX_SEQS_PER_STEP):
        sl = slice(g * SEQ, (g + 1) * SEQ)
        q = q_ref[sl, :].astype(F32)
        k = kv_ref[sl, 0:LANE].astype(F32)
        v = kv_ref[sl, LANE:2 * LANE].astype(F32)
        cols = []
        for col in range(2):
            acc = None
            for h in (2 * col, 2 * col + 1):
                q128, kk, vv, lo = _win_head_operands(q, k, v, h)
                s = _dot_nt(q128, kk) * scale
                sink = sink_ref[layer, h]
                m = jnp.maximum(jnp.max(s, -1, keepdims=True), sink)
                p = jnp.exp(s - m)
                den = jnp.sum(p, -1, keepdims=True) + jnp.exp(sink - m)
                o = _dot(p, vv) / den
                o = jnp.where(lo_mask if lo else hi_mask, o, 0.0)
                acc = o if acc is None else acc + o
            cols.append(acc)
        o_ref[sl, :] = jnp.concatenate(cols, 1).astype(o_ref.dtype)


def _win_ctx(z_main, sink, layer):
    return pl.pallas_call(
        functools.partial(_win_ctx_kernel, layer=layer),
        out_shape=jax.ShapeDtypeStruct((T_ALL, WIN_HEADS * WIN_HD), BF16),
        grid=(BATCH // CTX_SEQS_PER_STEP,),
        in_specs=[
            pl.BlockSpec(memory_space=pltpu.SMEM),
            pl.BlockSpec((CTX_SEQS_PER_STEP * SEQ, 256), lambda b: (b, COL_WQ // 256)),
            pl.BlockSpec((CTX_SEQS_PER_STEP * SEQ, 256), lambda b: (b, COL_WK // 256)),
        ],
        out_specs=pl.BlockSpec((CTX_SEQS_PER_STEP * SEQ, 256), lambda b: (b, 0)),
        compiler_params=_params("parallel"),
        name="win_ctx",
    )(sink, z_main, z_main)


def _win_lat_kernel(sink_ref, prev_ref, q_ref, kv_ref, ck_ref, cv_ref, c_ref, su_ref, sd_ref,
                    o_ref, *, layer):
    del prev_ref
    L = DEC_SEQ
    half = WIN_HD // 2
    c, su, sd = c_ref[...], su_ref[...], sd_ref[...]
    q = jnp.concatenate(
        [_rope128(q_ref[:, i * LANE:(i + 1) * LANE].astype(F32), c, su, sd, half) for i in range(2)], 1)
    k = _rope128(kv_ref[:, 0:LANE].astype(F32), c, su, sd, half)
    v = kv_ref[:, LANE:2 * LANE].astype(F32)
    ck = ck_ref[...]
    cv = cv_ref[...]
    lo_mask, hi_mask = _win_masks()
    scale = WIN_HD ** -0.5
    nb = L // CHUNK
    cols = []
    for col in range(2):
        acc_blocks = [None] * nb
        for h in (2 * col, 2 * col + 1):
            q128, kk, vv, lo = _win_head_operands(q, k, v, h)
            _, ckk, cvv, _ = _win_head_operands(q, ck, cv, h)
            sink = sink_ref[layer, h]
            for n in range(nb):
                k0 = max(0, (n - 1) * CHUNK)
                k1 = min(L, (n + 2) * CHUNK)
                qn = q128[n * CHUNK:(n + 1) * CHUNK]
                s_loc = _dot_nt(qn, kk[k0:k1]) * scale
                qi = n * CHUNK + lax.broadcasted_iota(jnp.int32, s_loc.shape, 0)
                kj = k0 + lax.broadcasted_iota(jnp.int32, s_loc.shape, 1)
                s_loc = jnp.where(jnp.abs(qi - kj) <= WINDOW, s_loc, NEG)
                s_ctx = _dot_nt(qn, ckk) * scale
                m = jnp.maximum(jnp.maximum(jnp.max(s_loc, -1, keepdims=True),
                                            jnp.max(s_ctx, -1, keepdims=True)), sink)
                p_loc = jnp.exp(s_loc - m)
                p_ctx = jnp.exp(s_ctx - m)
                den = (jnp.sum(p_loc, -1, keepdims=True) + jnp.sum(p_ctx, -1, keepdims=True)
                       + jnp.exp(sink - m))
                o = (_dot(p_loc, vv[k0:k1]) + _dot(p_ctx, cvv)) / den
                o = jnp.where(lo_mask if lo else hi_mask, o, 0.0)
                acc_blocks[n] = o if acc_blocks[n] is None else acc_blocks[n] + o
        cols.append(jnp.concatenate(acc_blocks, 0))
    o_ref[...] = jnp.concatenate(cols, 1).astype(o_ref.dtype)


def _win_lat(z_main, prev_out, sink, cache_k, cache_v, layer):
    c, su, sd, _ = _rope_tables(DEC_SEQ, WIN_HD, LANE)
    rb0 = T_CTX // DEC_SEQ
    tab = pl.BlockSpec((DEC_SEQ, LANE), lambda b: (0, 0))
    cache = pl.BlockSpec((None, None, PAST_LEN, LANE), lambda b: (b, layer, 0, 0))
    return pl.pallas_call(
        functools.partial(_win_lat_kernel, layer=layer),
        out_shape=jax.ShapeDtypeStruct((T_ALL, WIN_HEADS * WIN_HD), BF16),
        grid=(DEC_BATCH,),
        in_specs=[
            pl.BlockSpec(memory_space=pltpu.SMEM),
            pl.BlockSpec(memory_space=pl.ANY),
            pl.BlockSpec((DEC_SEQ, 256), lambda b: (rb0 + b, COL_WQ // 256)),
            pl.BlockSpec((DEC_SEQ, 256), lambda b: (rb0 + b, COL_WK // 256)),
            cache, cache, tab, tab, tab,
        ],
        out_specs=pl.BlockSpec((DEC_SEQ, 256), lambda b: (rb0 + b, 0)),
        input_output_aliases={1: 0},
        compiler_params=_params("parallel"),
        name="win_lat",
    )(sink, prev_out, z_main, z_main, cache_k, cache_v,
      jnp.asarray(c), jnp.asarray(su), jnp.asarray(sd))


def _ret_kernel(*refs, L, layer, ctx):
    if ctx:
        (df_ref, db_ref, q_ref, k_ref, v0_ref, v1_ref, g0_ref, g1_ref,
         o_ref, sf_out, sb_out, s_ref, cross_ref) = refs
        seqs = CTX_SEQS_PER_STEP
    else:
        (df_ref, db_ref, prev_ref, q_ref, k_ref, v0_ref, v1_ref, g0_ref, g1_ref, s0f_ref, s0b_ref,
         o_ref, s_ref, cross_ref) = refs
        seqs = 1
    C = CHUNK
    nc = L // C
    H = RET_HEADS
    qw = H * RET_DK
    vw = H * RET_DV

    def lane_table(width, per_head, fn):
        pos = lax.broadcasted_iota(jnp.int32, (C, per_head), 0).astype(F32)
        return jnp.concatenate([fn(h, pos) for h in range(H)], 1)

    def log_gamma(ref, h):
        d = jnp.full((1, 1), ref[layer, h], F32)
        return jnp.log(jax.nn.sigmoid(d))

    lgf = [log_gamma(df_ref, h) for h in range(H)]
    lgb = [log_gamma(db_ref, h) for h in range(H)]

    def tables(lg, reverse):
        if reverse:
            dq = lane_table(vw, RET_DV, lambda h, pos: jnp.exp((C - pos) * lg[h]))
            dk = lane_table(qw, RET_DK, lambda h, pos: jnp.exp(pos * lg[h]))
        else:
            dq = lane_table(vw, RET_DV, lambda h, pos: jnp.exp((pos + 1.0) * lg[h]))
            dk = lane_table(qw, RET_DK, lambda h, pos: jnp.exp((C - 1.0 - pos) * lg[h]))
        dc = jnp.concatenate([jnp.broadcast_to(jnp.exp(C * lg[h]), (1, RET_DV)) for h in range(H)], 1)
        return dq, dk, dc

    tab_f = tables(lgf, False)
    tab_b = tables(lgb, True)
    ii = lax.broadcasted_iota(jnp.int32, (C, C), 0)
    jj = lax.broadcasted_iota(jnp.int32, (C, C), 1)
    diff = (ii - jj).astype(F32)
    dmats = [jnp.where(diff >= 0, jnp.exp(jnp.maximum(diff, 0.0) * lgf[h]), 0.0)
             + jnp.where(diff <= 0, jnp.exp(jnp.maximum(-diff, 0.0) * lgb[h]), 0.0) for h in range(H)]
    lane_q = lax.broadcasted_iota(jnp.int32, (1, qw), 1) // RET_DK

    srow = lax.broadcasted_iota(jnp.int32, (qw, vw), 0) // RET_DK
    scol = lax.broadcasted_iota(jnp.int32, (qw, vw), 1) // RET_DV
    diag = srow == scol

    for g in range(seqs):
        base = g * L
        rows_all = slice(base, base + L)
        q_all = q_ref[rows_all, :].astype(F32)
        k_all = k_ref[rows_all, :].astype(F32) * (RET_DK ** -0.5)
        v_all = jnp.concatenate([v0_ref[rows_all, :], v1_ref[rows_all, :]], 1).astype(F32)
        g_all = jnp.concatenate([g0_ref[rows_all, :], g1_ref[rows_all, :]], 1).astype(F32)

        def scan(tabs, reverse, s0_ref, s_out):
            dq, dk, dc = tabs
            s_ref[g] = jnp.zeros((qw, vw), F32)
            if s0_ref is not None:
                for h in range(H):
                    s_ref[g, h * RET_DK:(h + 1) * RET_DK, h * RET_DV:(h + 1) * RET_DV] = s0_ref[h]
            order = range(nc - 1, -1, -1) if reverse else range(nc)
            for ci in order:
                sl = slice(ci * C, (ci + 1) * C)
                qc, kc, vc = q_all[sl], k_all[sl], v_all[sl]
                st = s_ref[g]
                cross = _dot(qc, st) * dq
                if reverse:
                    cross_ref[g, sl, :] = cross_ref[g, sl, :] + cross
                else:
                    cross_ref[g, sl, :] = cross
                upd = jnp.where(diag, _dot_tn(kc * dk, vc), 0.0)
                s_ref[g] = st * dc + upd
            if s_out is not None:
                for h in range(H):
                    s_out[g, h] = s_ref[g, h * RET_DK:(h + 1) * RET_DK, h * RET_DV:(h + 1) * RET_DV]

        scan(tab_f, False, None if ctx else s0f_ref, sf_out if ctx else None)
        scan(tab_b, True, None if ctx else s0b_ref, sb_out if ctx else None)

        for h in range(H):
            hv = slice(h * RET_DV, (h + 1) * RET_DV)
            for ci in range(nc):
                sl = slice(ci * C, (ci + 1) * C)
                qh = jnp.where(lane_q == h, q_all[sl], 0.0)
                att = _dot_nt(qh, k_all[sl]) * dmats[h]
                o = _dot(att, v_all[sl, hv]) + cross_ref[g, sl, hv]
                gt = g_all[sl, hv]
                o_ref[base + ci * C:base + (ci + 1) * C, hv] = (
                    (gt * jax.nn.sigmoid(gt)) * _layer_norm(o)).astype(o_ref.dtype)


def _retention(z_main, prev_out, dec_f, dec_b, s0f, s0b, layer, ctx):
    L = SEQ if ctx else DEC_SEQ
    n_seq = BATCH if ctx else DEC_BATCH
    seqs = CTX_SEQS_PER_STEP if ctx else 1
    rb0 = 0 if ctx else T_CTX // DEC_SEQ

    def zcol(col):
        return pl.BlockSpec((seqs * L, 256), lambda b: (rb0 + b, col // 256))

    smem = pl.BlockSpec(memory_space=pltpu.SMEM)
    z_specs = [zcol(COL_RQ), zcol(COL_RK), zcol(COL_RV), zcol(COL_RV + 256),
               zcol(COL_RG), zcol(COL_RG + 256)]
    y_shape = jax.ShapeDtypeStruct((T_ALL, RET_HEADS * RET_DV), BF16)
    y_spec = pl.BlockSpec((seqs * L, RET_HEADS * RET_DV), lambda b: (rb0 + b, 0))
    scratch = [pltpu.VMEM((seqs, RET_HEADS * RET_DK, RET_HEADS * RET_DV), F32),
               pltpu.VMEM((seqs, L, RET_HEADS * RET_DV), F32)]
    kern = functools.partial(_ret_kernel, L=L, layer=layer, ctx=ctx)
    if ctx:
        st_shape = jax.ShapeDtypeStruct((BATCH, RET_HEADS, RET_DK, RET_DV), F32)
        st_spec = pl.BlockSpec((seqs, RET_HEADS, RET_DK, RET_DV), lambda b: (b, 0, 0, 0))
        return pl.pallas_call(
            kern,
            out_shape=(y_shape, st_shape, st_shape),
            grid=(n_seq // seqs,),
            in_specs=[smem, smem] + z_specs,
            out_specs=(y_spec, st_spec, st_spec),
            scratch_shapes=scratch,
            compiler_params=_params("parallel"),
            name="ret_ctx",
        )(dec_f, dec_b, *([z_main] * 6))
    s0_spec = pl.BlockSpec((None, None, RET_HEADS, RET_DK, RET_DV), lambda b: (b, layer, 0, 0, 0))
    return pl.pallas_call(
        kern,
        out_shape=y_shape,
        grid=(n_seq,),
        in_specs=[smem, smem, pl.BlockSpec(memory_space=pl.ANY)] + z_specs + [s0_spec, s0_spec],
        out_specs=y_spec,
        scratch_shapes=scratch,
        input_output_aliases={2: 0},
        compiler_params=_params("parallel"),
        name="ret_lat",
    )(dec_f, dec_b, prev_out, *([z_main] * 6), s0f, s0b)


def _rms_norm(x, g):
    return x * lax.rsqrt(jnp.mean(x * x, -1, keepdims=True) + RMS_EPS) * g


def _mla_attend(qn, qr, kn, kr, vv, o_ref, row0):
    scale = (MLA_NOPE + MLA_ROPE) ** -0.5
    lane_n = lax.broadcasted_iota(jnp.int32, (1, MLA_HEADS * MLA_NOPE), 1) // MLA_NOPE
    lane_r = lax.broadcasted_iota(jnp.int32, (1, LANE), 1)
    kr32 = jnp.where(lane_r < MLA_ROPE, kr, 0.0)
    acc = None
    for h in range(MLA_HEADS):
        qnh = jnp.where(lane_n == h, qn, 0.0)
        qrh = qr if h == 0 else pltpu.roll(qr, LANE - h * MLA_ROPE, axis=1)
        qrh = jnp.where(lane_r < MLA_ROPE, qrh, 0.0)
        s = (_dot_nt(qnh, kn) + _dot_nt(qrh, kr32)) * scale
        m = jnp.max(s, -1, keepdims=True)
        p = jnp.exp(s - m)
        den = jnp.sum(p, -1, keepdims=True)
        o = jnp.where(lane_n == h, _dot(p, vv) / den, 0.0)
        acc = o if acc is None else acc + o
    o_ref[row0:row0 + acc.shape[0], :] = acc.astype(o_ref.dtype)


def _mla_ctx_kernel(cq_ref, ckv_ref, kr_ref, qg_ref, kg_ref, wqn_ref, wqr_ref, wk_ref, wv_ref,
                    o_ref, ckvn_ref):
    cqn = _rms_norm(cq_ref[...].astype(F32), qg_ref[...])
    qn = _dot(cqn, wqn_ref[...])
    qr = _dot(cqn, wqr_ref[...])
    ckvn = _rms_norm(ckv_ref[...].astype(F32), kg_ref[...])
    ckvn_ref[...] = ckvn
    kn = _dot(ckvn, wk_ref[...])
    vv = _dot(ckvn, wv_ref[...])
    for g in range(CTX_SEQS_PER_STEP):
        sl = slice(g * SEQ, (g + 1) * SEQ)
        _mla_attend(qn[sl], qr[sl], kn[sl], kr_ref[sl, :].astype(F32), vv[sl], o_ref, g * SEQ)


def _mla_weight_specs(layer):
    return [
        pl.BlockSpec((None, 1, MLA_Q_LORA), lambda b: (layer, 0, 0)),
        pl.BlockSpec((None, 1, MLA_KV_LORA), lambda b: (layer, 0, 0)),
        pl.BlockSpec((None, MLA_Q_LORA, MLA_HEADS * MLA_NOPE), lambda b: (layer, 0, 0)),
        pl.BlockSpec((None, MLA_Q_LORA, MLA_HEADS * MLA_ROPE), lambda b: (layer, 0, 0)),
        pl.BlockSpec((None, MLA_KV_LORA, MLA_HEADS * MLA_NOPE), lambda b: (layer, 0, 0)),
        pl.BlockSpec((None, MLA_KV_LORA, MLA_HEADS * MLA_V), lambda b: (layer, 0, 0)),
    ]


def _mla_ctx(z_main, weights, layer):
    return pl.pallas_call(
        _mla_ctx_kernel,
        out_shape=(jax.ShapeDtypeStruct((T_ALL, MLA_HEADS * MLA_V), BF16),
                   jax.ShapeDtypeStruct((T_CTX, MLA_KV_LORA), F32)),
        grid=(BATCH // CTX_SEQS_PER_STEP,),
        in_specs=[
            pl.BlockSpec((CTX_SEQS_PER_STEP * SEQ, 256), lambda b: (b, COL_CQ // 256)),
            pl.BlockSpec((CTX_SEQS_PER_STEP * SEQ, LANE), lambda b: (b, COL_CKV // LANE)),
            pl.BlockSpec((CTX_SEQS_PER_STEP * SEQ, LANE), lambda b: (b, COL_KROPE // LANE)),
        ] + _mla_weight_specs(layer),
        out_specs=(pl.BlockSpec((CTX_SEQS_PER_STEP * SEQ, 256), lambda b: (b, 0)),
                   pl.BlockSpec((CTX_SEQS_PER_STEP * SEQ, MLA_KV_LORA), lambda b: (b, 0))),
        compiler_params=_params("parallel"),
        name="mla_ctx",
    )(z_main, z_main, z_main, *weights)


def _mla_lat_kernel(prev_ref, cq_ref, ckv_ref, kr_ref, cckv_ref, ckr_ref, c_ref, su_ref, sd_ref,
                    qg_ref, kg_ref, wqn_ref, wqr_ref, wk_ref, wv_ref, o_ref):
    del prev_ref
    half = MLA_ROPE // 2
    c, su, sd = c_ref[...], su_ref[...], sd_ref[...]
    cqn = _rms_norm(cq_ref[...].astype(F32), qg_ref[...])
    qn = _dot(cqn, wqn_ref[...])
    qr = _rope128(_dot(cqn, wqr_ref[...]), c, su, sd, half)
    ckvn = _rms_norm(ckv_ref[...].astype(F32), kg_ref[...])
    ckv_all = jnp.concatenate([ckvn, cckv_ref[...]], 0)
    kn = _dot(ckv_all, wk_ref[...])
    vv = _dot(ckv_all, wv_ref[...])
    kr = jnp.concatenate([_rope128(kr_ref[...].astype(F32), c, su, sd, half), ckr_ref[...]], 0)
    for n in range(DEC_SEQ // 256):
        rows = slice(n * 256, (n + 1) * 256)
        _mla_attend(qn[rows], qr[rows], kn, kr, vv, o_ref, n * 256)


def _mla_lat(z_main, prev_out, cache_ckv, cache_kr_pad, weights, layer):
    c, su, sd, _ = _rope_tables(DEC_SEQ, MLA_ROPE, LANE)
    rb0 = T_CTX // DEC_SEQ
    tab = pl.BlockSpec((DEC_SEQ, LANE), lambda b: (0, 0))
    cache = pl.BlockSpec((None, None, PAST_LEN, LANE), lambda b: (b, layer, 0, 0))
    return pl.pallas_call(
        _mla_lat_kernel,
        out_shape=jax.ShapeDtypeStruct((T_ALL, MLA_HEADS * MLA_V), BF16),
        grid=(DEC_BATCH,),
        in_specs=[
            pl.BlockSpec(memory_space=pl.ANY),
            pl.BlockSpec((DEC_SEQ, 256), lambda b: (rb0 + b, COL_CQ // 256)),
            pl.BlockSpec((DEC_SEQ, LANE), lambda b: (rb0 + b, COL_CKV // LANE)),
            pl.BlockSpec((DEC_SEQ, LANE), lambda b: (rb0 + b, COL_KROPE // LANE)),
            cache, cache, tab, tab, tab,
        ] + _mla_weight_specs(layer),
        out_specs=pl.BlockSpec((DEC_SEQ, 256), lambda b: (rb0 + b, 0)),
        input_output_aliases={0: 0},
        compiler_params=_params("parallel"),
        name="mla_lat",
    )(prev_out, z_main, z_main, z_main, cache_ckv, cache_kr_pad,
      jnp.asarray(c), jnp.asarray(su), jnp.asarray(sd), *weights)


def _route(logits_t, rb):
    scores = jax.nn.sigmoid(logits_t)
    biased = scores + rb
    sc = [scores[e:e + 1, :] for e in range(N_EXPERTS)]
    bi = [biased[e:e + 1, :] for e in range(N_EXPERTS)]
    epg = EXPERTS_PER_GROUP
    gsum = []
    for g in range(N_GROUPS):
        v = bi[g * epg:(g + 1) * epg]
        best = None
        for a in range(epg):
            for b in range(a + 1, epg):
                pair = v[a] + v[b]
                best = pair if best is None else jnp.maximum(best, pair)
        gsum.append(best)
    combine = []
    sel = []
    for g in range(N_GROUPS):
        is_best = None
        for g2 in range(N_GROUPS):
            if g2 == g:
                continue
            c = gsum[g] > gsum[g2] if g2 < g else gsum[g] >= gsum[g2]
            is_best = c if is_best is None else jnp.logical_and(is_best, c)
        for a in range(epg):
            e = g * epg + a
            rank = jnp.zeros_like(bi[e])
            for b in range(epg):
                if b == a:
                    continue
                e2 = g * epg + b
                ahead = bi[e2] >= bi[e] if b < a else bi[e2] > bi[e]
                rank = rank + jnp.where(ahead, 1.0, 0.0)
            sel.append(jnp.logical_and(is_best, rank < 2.0))
    wsum = None
    for e in range(N_EXPERTS):
        w = jnp.where(sel[e], sc[e], 0.0)
        wsum = w if wsum is None else wsum + w
    for e in range(N_EXPERTS):
        combine.append(jnp.where(sel[e], ROUTE_SCALE * sc[e] / wsum, 0.0))
    return jnp.concatenate(combine, 0)


def _merge_kernel(ya_ref, yb_ref, yc_ref, yd_ref, gt_ref, xc_ref, xl_ref, m_ref,
                  wa_ref, wb_ref, wc_ref, wd_ref, wo_ref, g_ref, b_ref, rwh_ref, rwl_ref, rb_ref,
                  x1_ref, h2_ref, cmb_ref, *, ctx_tiles, sub_rows):
    D = D_MODEL
    g1 = m_ref[:, 2 * D:3 * D]
    s2 = m_ref[:, 3 * D:4 * D]
    sc2 = m_ref[:, 4 * D:5 * D]
    is_ctx = pl.program_id(0) < ctx_tiles
    branches = ((ya_ref, wa_ref), (yb_ref, wb_ref), (yc_ref, wc_ref), (yd_ref, wd_ref))
    for r0 in range(0, x1_ref.shape[0], sub_rows):
        rows = slice(r0, r0 + sub_rows)
        merged = None
        for i, (y_ref, w) in enumerate(branches):
            t = gt_ref[rows, i * D:(i + 1) * D].astype(F32) * jnp.dot(
                y_ref[rows, :], w[...], preferred_element_type=F32)
            merged = t if merged is None else merged + t
        out1 = jnp.dot(merged.astype(BF16), wo_ref[...], preferred_element_type=F32)
        x = jnp.where(is_ctx, xc_ref[rows, :], xl_ref[rows, :])
        x1 = _layer_norm(ALPHA * x + g1 * out1) * g_ref[...] + b_ref[...]
        x1_ref[rows, :] = x1
        h2 = _layer_norm(x1) * (1.0 + sc2) + s2
        h2_hi = h2.astype(BF16)
        h2_ref[rows, :] = h2_hi
        h2_lo = (h2 - h2_hi.astype(F32)).astype(BF16)
        logits = (jnp.dot(h2_hi, rwh_ref[...], preferred_element_type=F32)
                  + (jnp.dot(h2_lo, rwh_ref[...], preferred_element_type=F32)
                     + jnp.dot(h2_hi, rwl_ref[...], preferred_element_type=F32)))
        cmb_ref[:, rows] = _route(logits.T[0:N_EXPERTS], rb_ref[...])


def _merge(ya, yb, yc, yd, gates, x_ctx, x_lat, mods_l, w_br, w_out_bf, ln1_g, ln1_b,
           router_w_parts, router_b, layer):
    tm = 512
    row = _mod_row(tm)
    D = D_MODEL
    ctx_tiles = T_CTX // tm

    def tile(w):
        return pl.BlockSpec((tm, w), lambda i: (i, 0))

    def weight(k, n):
        return pl.BlockSpec((None, k, n), lambda i: (layer, 0, 0))

    return pl.pallas_call(
        functools.partial(_merge_kernel, ctx_tiles=ctx_tiles, sub_rows=256),
        out_shape=(jax.ShapeDtypeStruct((T_ALL, D), F32),
                   jax.ShapeDtypeStruct((T_ALL, D), BF16),
                   jax.ShapeDtypeStruct((N_EXPERTS, T_ALL), F32)),
        grid=(T_ALL // tm,),
        in_specs=[
            tile(256), tile(256), tile(512), tile(256), tile(4 * D),
            pl.BlockSpec((tm, D), lambda i: (jnp.minimum(i, ctx_tiles - 1), 0)),
            pl.BlockSpec((tm, D), lambda i: (jnp.maximum(i - ctx_tiles, 0), 0)),
            pl.BlockSpec((None, 1, 6 * D), lambda i: (row(i), 0, 0)),
            weight(256, D), weight(256, D), weight(512, D), weight(256, D), weight(D, D),
            weight(1, D), weight(1, D),
            pl.BlockSpec((D, LANE), lambda i: (0, 0)),
            pl.BlockSpec((D, LANE), lambda i: (0, 0)),
            pl.BlockSpec((N_EXPERTS, 1), lambda i: (0, 0)),
        ],
        out_specs=(tile(D), tile(D), pl.BlockSpec((N_EXPERTS, tm), lambda i: (0, i))),
        compiler_params=_params("parallel"),
        name="merge",
    )(ya, yb, yc, yd, gates, x_ctx, x_lat, mods_l, *w_br, w_out_bf,
      ln1_g.reshape(DEPTH, 1, D), ln1_b.reshape(DEPTH, 1, D), *router_w_parts,
      router_b.reshape(N_EXPERTS, 1))


MOE_EXPERTS_PER_STEP = 2


def _moe_kernel(h_ref, c_ref, x1_ref, m_ref, wg_ref, wu_ref, wd_ref, g_ref, b_ref, o_ref, acc_ref):
    eg = pl.program_id(1)

    @pl.when(eg == 0)
    def _():
        acc_ref[...] = jnp.zeros_like(acc_ref)

    h = h_ref[...]
    cmb = c_ref[...]
    lane = lax.broadcasted_iota(jnp.int32, cmb.shape, 1)
    hid = []
    for k in range(MOE_EXPERTS_PER_STEP):
        gate = jnp.dot(h, wg_ref[k].astype(BF16), preferred_element_type=F32)
        up = jnp.dot(h, wu_ref[k].astype(BF16), preferred_element_type=F32)
        e = eg * MOE_EXPERTS_PER_STEP + k
        ce = jnp.sum(jnp.where(lane == e, cmb, 0.0), -1, keepdims=True)
        hid.append(((gate * jax.nn.sigmoid(gate)) * up * ce).astype(BF16))
    wd = wd_ref[...].reshape(MOE_EXPERTS_PER_STEP * D_EXPERT, D_MODEL).astype(BF16)
    acc_ref[...] += jnp.dot(jnp.concatenate(hid, 1), wd, preferred_element_type=F32)

    @pl.when(eg == N_EXPERTS // MOE_EXPERTS_PER_STEP - 1)
    def _():
        g2 = m_ref[:, 5 * D_MODEL:6 * D_MODEL]
        y = _layer_norm(ALPHA * x1_ref[...] + g2 * acc_ref[...])
        o_ref[...] = y * g_ref[...] + b_ref[...]


def _moe(h2, combine, x1, mods_l, w_gate, w_up, w_down, ln2_g, ln2_b, layer, row0, n_rows):
    tm = 1024
    row = _mod_row(tm)
    D = D_MODEL
    t0 = row0 // tm
    eps = MOE_EXPERTS_PER_STEP
    return pl.pallas_call(
        _moe_kernel,
        out_shape=jax.ShapeDtypeStruct((n_rows, D), F32),
        grid=(n_rows // tm, N_EXPERTS // eps),
        in_specs=[
            pl.BlockSpec((tm, D), lambda i, e: (t0 + i, 0)),
            pl.BlockSpec((tm, N_EXPERTS), lambda i, e: (t0 + i, 0)),
            pl.BlockSpec((tm, D), lambda i, e: (t0 + i, 0)),
            pl.BlockSpec((None, 1, 6 * D), lambda i, e: (row(t0 + i), 0, 0)),
            pl.BlockSpec((None, eps, D, D_EXPERT), lambda i, e: (layer, e, 0, 0)),
            pl.BlockSpec((None, eps, D, D_EXPERT), lambda i, e: (layer, e, 0, 0)),
            pl.BlockSpec((None, eps, D_EXPERT, D), lambda i, e: (layer, e, 0, 0)),
            pl.BlockSpec((None, 1, D), lambda i, e: (layer, 0, 0)),
            pl.BlockSpec((None, 1, D), lambda i, e: (layer, 0, 0)),
        ],
        out_specs=pl.BlockSpec((tm, D), lambda i, e: (i, 0)),
        scratch_shapes=[pltpu.VMEM((tm, D), F32)],
        compiler_params=_params("parallel", "arbitrary"),
        name="moe",
    )(h2, combine, x1, mods_l, w_gate, w_up, w_down,
      ln2_g.reshape(DEPTH, 1, D), ln2_b.reshape(DEPTH, 1, D))


def kernel(x_prompt, x_sample, cache_win_k, cache_win_v, cache_mla_ckv, cache_mla_krope,
           state_ret_fwd, state_ret_bwd, c, c_ctx, w_ada, b_ada, w_in,
           hy_conv_w, hy_conv_b, hy_w1, hy_b1, hy_w2, hy_b2, hy_w3, hy_bias,
           win_sink, ret_decay_fwd, ret_decay_bwd, mla_q_norm, mla_kv_norm, mla_w_uq, mla_w_ukv,
           w_br_a, w_br_b, w_br_c, w_br_d, w_out, ln1_g, ln1_b, ln2_g, ln2_b,
           router_w, router_b, moe_w_gate, moe_w_up, moe_w_down):
    D = D_MODEL
    x_ctx = x_prompt.reshape(T_CTX, D)
    x_lat = x_sample.reshape(T_LAT, D)

    cvec = jnp.zeros((ADA_ROWS, D), F32).at[0].set(c_ctx).at[1:1 + DEC_BATCH].set(c)
    mods = _ada_mods(cvec, w_ada, b_ada)[:, :1 + DEC_BATCH].reshape(DEPTH, 1 + DEC_BATCH, 1, 6 * D)

    w_in_t = jnp.swapaxes(w_in, 1, 2).reshape(DEPTH * IN_COLS, D)
    cache_k = cache_win_k.reshape(DEC_BATCH, DEPTH, PAST_LEN, WIN_KV_HEADS * WIN_HD)
    cache_v = cache_win_v.reshape(DEC_BATCH, DEPTH, PAST_LEN, WIN_KV_HEADS * WIN_HD)
    cache_kr = jnp.pad(cache_mla_krope, ((0, 0), (0, 0), (0, 0), (0, LANE - MLA_ROPE)))

    uq = mla_w_uq.reshape(DEPTH, MLA_Q_LORA, MLA_HEADS, MLA_NOPE + MLA_ROPE)
    ukv = mla_w_ukv.reshape(DEPTH, MLA_KV_LORA, MLA_HEADS, MLA_NOPE + MLA_V)
    mla_weights = (
        mla_q_norm.reshape(DEPTH, 1, MLA_Q_LORA),
        mla_kv_norm.reshape(DEPTH, 1, MLA_KV_LORA),
        uq[..., :MLA_NOPE].reshape(DEPTH, MLA_Q_LORA, MLA_HEADS * MLA_NOPE),
        uq[..., MLA_NOPE:].reshape(DEPTH, MLA_Q_LORA, MLA_HEADS * MLA_ROPE),
        ukv[..., :MLA_NOPE].reshape(DEPTH, MLA_KV_LORA, MLA_HEADS * MLA_NOPE),
        ukv[..., MLA_NOPE:].reshape(DEPTH, MLA_KV_LORA, MLA_HEADS * MLA_V),
    )

    hy_w1p = jnp.pad(hy_w1, ((0, 0), (0, LANE - HY_EMB), (0, 0)))
    dft = {}
    for L in (SEQ, DEC_SEQ):
        fwd, inv = _dft_tables(L)
        dft[L] = (jnp.asarray(fwd).astype(BF16), jnp.asarray(inv).astype(BF16))
    router_w_pad = jnp.pad(router_w, ((0, 0), (0, LANE - N_EXPERTS)))
    router_w_hi = router_w_pad.astype(BF16)
    router_w_parts = (router_w_hi, (router_w_pad - router_w_hi.astype(F32)).astype(BF16))
    w_br = tuple(w.astype(BF16) for w in (w_br_a, w_br_b, w_br_c, w_br_d))
    w_out_bf = w_out.astype(BF16)

    new_k, new_v, new_ckv, new_kr, new_sf, new_sb = [], [], [], [], [], []
    for l in range(DEPTH):
        mods_l = mods[l]
        h = _ln_mod(x_ctx, None, mods_l, 0)
        h = _ln_mod(x_lat, h, mods_l, T_CTX)
        z = _in_proj(h, w_in_t, l, 0, Z_MAIN, Z_MAIN // 2, BF16, gate=False)
        gates = _in_proj(h, w_in_t, l, COL_GATE, 4 * D, D, BF16, gate=True)

        ya = None
        for L, n_seq, rb0 in ((SEQ, BATCH, 0), (DEC_SEQ, DEC_BATCH, T_CTX // DEC_SEQ)):
            fwd, inv = dft[L]
            kre, kim = _hy_filters(L, hy_w1p[l], hy_b1[l][None], hy_w2[l], hy_b2[l][None], hy_w3[l], fwd)
            ya = _hyena(z, ya, l, L, n_seq, rb0, hy_conv_w, hy_conv_b, hy_bias, kre, kim, fwd, inv)

        yb = _win_ctx(z, win_sink, l)
        yb = _win_lat(z, yb, win_sink, cache_k, cache_v, l)

        yc, sf, sb = _retention(z, None, ret_decay_fwd, ret_decay_bwd, None, None, l, ctx=True)
        yc = _retention(z, yc, ret_decay_fwd, ret_decay_bwd, state_ret_fwd, state_ret_bwd, l, ctx=False)

        yd, ckvn = _mla_ctx(z, mla_weights, l)
        yd = _mla_lat(z, yd, cache_mla_ckv, cache_kr, mla_weights, l)

        x1, h2, combine_t = _merge(ya, yb, yc, yd, gates, x_ctx, x_lat, mods_l, w_br, w_out_bf,
                                   ln1_g, ln1_b, router_w_parts, router_b, l)
        moe_args = (h2, combine_t.T, x1, mods_l, moe_w_gate, moe_w_up, moe_w_down, ln2_g, ln2_b, l)
        x_ctx = _moe(*moe_args, 0, T_CTX)
        x_lat = _moe(*moe_args, T_CTX, T_LAT)

        zc = z[:T_CTX].astype(F32)
        new_k.append(zc[:, COL_WK:COL_WK + 128].reshape(BATCH, SEQ, WIN_KV_HEADS, WIN_HD))
        new_v.append(zc[:, COL_WV:COL_WV + 128].reshape(BATCH, SEQ, WIN_KV_HEADS, WIN_HD))
        new_ckv.append(ckvn.reshape(BATCH, SEQ, MLA_KV_LORA))
        new_kr.append(zc[:, COL_KROPE:COL_KROPE + MLA_ROPE].reshape(BATCH, SEQ, MLA_ROPE))
        new_sf.append(sf)
        new_sb.append(sb)

    y_prompt = x_ctx.reshape(BATCH, SEQ, D)
    y_sample = x_lat.reshape(DEC_BATCH, DEC_SEQ, D)
    return (y_prompt, y_sample, jnp.stack(new_k, 1), jnp.stack(new_v, 1), jnp.stack(new_ckv, 1),
            jnp.stack(new_kr, 1), jnp.stack(new_sf, 1), jnp.stack(new_sb, 1))
```

```python
import functools
import math

import numpy as np
import jax
import jax.numpy as jnp
from jax import lax
from jax.experimental import pallas as pl
from jax.experimental.pallas import tpu as pltpu

F32 = jnp.float32
BF16 = jnp.bfloat16
HIGHEST = lax.Precision.HIGHEST

D_MODEL = 1024
BATCH = 16
SEQ = 256
DEPTH = 2
DEC_BATCH = 2
DEC_SEQ = 1024
PAST_LEN = 256
GRID_W = 64
CHUNK = 128
ROPE_BASE = 10000.0
NEG = -1e30
LN_EPS = 1e-5
RMS_EPS = 1e-6

HY_W = 256
HY_BANDS = 16
HY_EMB = 1 + 2 * HY_BANDS
HY_FFN = 64
HY_FAST_DECAY = 0.3
HY_SLOW_DECAY = 1.5
HY_TARGET = 1e-2

WIN_HEADS = 4
WIN_KV_HEADS = 2
WIN_HD = 64
WINDOW = 128

RET_HEADS = 4
RET_DK = 64
RET_DV = 128

MLA_HEADS = 4
MLA_Q_LORA = 256
MLA_KV_LORA = 128
MLA_NOPE = 64
MLA_ROPE = 32
MLA_V = 64

N_EXPERTS = 16
N_GROUPS = 4
EXPERTS_PER_GROUP = N_EXPERTS // N_GROUPS
D_EXPERT = 256
ROUTE_SCALE = 2.5

ALPHA = (2.0 * DEPTH) ** 0.25

T_CTX = BATCH * SEQ
T_LAT = DEC_BATCH * DEC_SEQ
T_ALL = T_CTX + T_LAT

COL_HY = 0
COL_WQ = 768
COL_WK = 1024
COL_WV = 1152
COL_RQ = 1280
COL_RK = 1536
COL_RV = 1792
COL_RG = 2304
COL_CQ = 2816
COL_CKV = 3072
COL_KROPE = 3200
COL_GATE = 3232
IN_COLS = COL_GATE + 4 * D_MODEL
Z_MAIN = 3328

LANE = 128
CTX_SEQS_PER_STEP = 4
VMEM_LIMIT = 56 * 1024 * 1024


def _params(*sem):
    return pltpu.CompilerParams(dimension_semantics=sem, vmem_limit_bytes=VMEM_LIMIT)


def _dot(a, b):
    return jnp.dot(a.astype(BF16), b.astype(BF16), preferred_element_type=F32)


def _dot_nt(a, b):
    return lax.dot_general(a.astype(BF16), b.astype(BF16), (((1,), (1,)), ((), ())),
                           preferred_element_type=F32)


def _dot_tn(a, b):
    return lax.dot_general(a.astype(BF16), b.astype(BF16), (((0,), (0,)), ((), ())),
                           preferred_element_type=F32)


def _layer_norm(x):
    mu = jnp.mean(x, -1, keepdims=True)
    xc = x - mu
    var = jnp.mean(xc * xc, -1, keepdims=True)
    return xc * lax.rsqrt(var + LN_EPS)


def _mod_row(tile_rows):
    def row(i):
        start = i * tile_rows
        return jnp.where(start < T_CTX, 0, 1 + (start - T_CTX) // DEC_SEQ)
    return row


@functools.lru_cache(maxsize=None)
def _dft_tables(L):
    f = np.arange(L, dtype=np.int64)[:, None]
    s = np.arange(L, dtype=np.int64)[None, :]
    ang = np.pi * ((f * s) % (2 * L)).astype(np.float64) / L
    cos = np.cos(ang)
    sin = np.sin(ang)
    alt = np.where(np.arange(L) % 2 == 0, 1.0, -1.0)
    fwd_im = -sin
    fwd_im[0, :] = alt
    fwd = np.concatenate([cos, fwd_im], 0)
    inv_re = cos.T / L
    inv_re[:, 0] = 1.0 / (2 * L)
    inv_im = -sin.T / L
    inv_im[:, 0] = alt / (2 * L)
    inv = np.concatenate([inv_re, inv_im], 1)
    return fwd.astype(np.float32), inv.astype(np.float32)


@functools.lru_cache(maxsize=None)
def _hyena_embedding(L):
    t01 = np.linspace(0.0, 1.0, L, dtype=np.float64)[:, None]
    bands = np.linspace(1e-4, HY_BANDS - 1, HY_BANDS, dtype=np.float64)
    ang = (2.0 * math.pi / L) * np.arange(L, dtype=np.float64)[:, None] * bands[None, :]
    z = np.concatenate([t01, np.cos(ang), -np.sin(ang)], -1)
    zp = np.zeros((L, LANE), np.float64)
    zp[:, :HY_EMB] = z
    deltas = np.abs(np.linspace(math.log(HY_TARGET) / HY_SLOW_DECAY,
                                math.log(HY_TARGET) / HY_FAST_DECAY, HY_W, dtype=np.float64))
    return zp.astype(np.float32), deltas[None, :].astype(np.float32)


@functools.lru_cache(maxsize=None)
def _rope_tables(L, rot_dim, width):
    rows = L // GRID_W
    n_freq = rot_dim // 4
    half = rot_dim // 2
    inv = ROPE_BASE ** (-np.arange(n_freq, dtype=np.float64) / n_freq)
    pos = np.arange(L)
    row = (pos // GRID_W).astype(np.float64)
    col = (pos % GRID_W).astype(np.float64)
    ang = np.concatenate([row[:, None] * inv, col[:, None] * inv], -1)
    cos, sin = np.cos(ang), np.sin(ang)
    zero = np.zeros_like(sin)
    c = np.tile(np.concatenate([cos, cos], -1), (1, width // rot_dim))
    s_up = np.tile(np.concatenate([-sin, zero], -1), (1, width // rot_dim))
    s_dn = np.tile(np.concatenate([zero, sin], -1), (1, width // rot_dim))
    return c.astype(np.float32), s_up.astype(np.float32), s_dn.astype(np.float32), half


def _rope128(x, c, s_up, s_dn, half):
    up = pltpu.roll(x, LANE - half, axis=1)
    dn = pltpu.roll(x, half, axis=1)
    return x * c + up * s_up + dn * s_dn


def _ada_kernel(c_ref, w_ref, b_ref, o_ref):
    cv = c_ref[...]
    s = cv * jax.nn.sigmoid(cv)
    s_hi = s.astype(BF16)
    s_lo = (s - s_hi.astype(F32)).astype(BF16)
    w = w_ref[...]
    w_hi = w.astype(BF16)
    w_lo = (w - w_hi.astype(F32)).astype(BF16)
    rows = s.shape[0]
    both = jnp.dot(jnp.concatenate([s_hi, s_lo], 0), w_hi, preferred_element_type=F32)
    o_ref[...] = (both[0:rows] + (both[rows:2 * rows] + jnp.dot(s_hi, w_lo, preferred_element_type=F32))
                  + b_ref[...])


ADA_ROWS = 16


def _ada_mods(cvec, w_ada, b_ada):
    tn = 1536
    n = 6 * D_MODEL
    return pl.pallas_call(
        _ada_kernel,
        out_shape=jax.ShapeDtypeStruct((DEPTH, ADA_ROWS, n), F32),
        grid=(DEPTH, n // tn),
        in_specs=[
            pl.BlockSpec((ADA_ROWS, D_MODEL), lambda l, j: (0, 0)),
            pl.BlockSpec((None, D_MODEL, tn), lambda l, j: (l, 0, j)),
            pl.BlockSpec((None, 1, tn), lambda l, j: (l, 0, j)),
        ],
        out_specs=pl.BlockSpec((None, ADA_ROWS, tn), lambda l, j: (l, 0, j)),
        compiler_params=_params("parallel", "parallel"),
        name="ada_mods",
    )(cvec, w_ada, b_ada.reshape(DEPTH, 1, n))


def _lnmod_kernel(*refs):
    x_ref, m_ref, h_ref = refs[-3:]
    y = _layer_norm(x_ref[...])
    s1 = m_ref[:, 0:D_MODEL]
    sc1 = m_ref[:, D_MODEL:2 * D_MODEL]
    h_ref[...] = (y * (1.0 + sc1) + s1).astype(h_ref.dtype)


def _ln_mod(x_group, prev_out, mods_l, row0):
    tm = 512
    row = _mod_row(tm)
    tile0 = row0 // tm
    in_specs = [
        pl.BlockSpec((tm, D_MODEL), lambda i: (i, 0)),
        pl.BlockSpec((None, 1, 6 * D_MODEL), lambda i: (row(tile0 + i), 0, 0)),
    ]
    args = [x_group, mods_l]
    if prev_out is not None:
        in_specs = [pl.BlockSpec(memory_space=pl.ANY)] + in_specs
        args = [prev_out] + args
    return pl.pallas_call(
        _lnmod_kernel,
        out_shape=jax.ShapeDtypeStruct((T_ALL, D_MODEL), BF16),
        grid=(x_group.shape[0] // tm,),
        in_specs=in_specs,
        out_specs=pl.BlockSpec((tm, D_MODEL), lambda i: (tile0 + i, 0)),
        input_output_aliases={} if prev_out is None else {0: 0},
        compiler_params=_params("parallel"),
        name="ln_mod",
    )(*args)


def _proj_kernel(h_ref, w_ref, o_ref, wb_ref, *, gate):
    @pl.when(pl.program_id(1) == 0)
    def _():
        wb_ref[...] = w_ref[...].T.astype(BF16)

    if not gate:
        o_ref[...] = jnp.dot(h_ref[...], wb_ref[...], preferred_element_type=F32).astype(o_ref.dtype)
        return
    sub = 2 * LANE
    for c0 in range(0, o_ref.shape[1], sub):
        r = jnp.dot(h_ref[...], wb_ref[:, c0:c0 + sub], preferred_element_type=F32)
        o_ref[:, c0:c0 + sub] = (0.5 * jnp.tanh(0.5 * r) + 0.5).astype(o_ref.dtype)


def _in_proj(h, w_t, layer, col0, n_cols, tn, out_dtype, gate):
    tm = 2048
    return pl.pallas_call(
        functools.partial(_proj_kernel, gate=gate),
        out_shape=jax.ShapeDtypeStruct((T_ALL, n_cols), out_dtype),
        grid=(n_cols // tn, T_ALL // tm),
        in_specs=[
            pl.BlockSpec((tm, D_MODEL), lambda j, i: (i, 0)),
            pl.BlockSpec((pl.Element(tn), pl.Element(D_MODEL)),
                         lambda j, i: (pl.multiple_of(layer * IN_COLS + col0 + j * tn, 8), 0)),
        ],
        out_specs=pl.BlockSpec((tm, tn), lambda j, i: (i, j)),
        scratch_shapes=[pltpu.VMEM((D_MODEL, tn), BF16)],
        compiler_params=_params("parallel", "arbitrary"),
        name="gate_proj" if gate else "in_proj",
    )(h, w_t)


def _hy_filter_kernel(z_ref, dl_ref, w1_ref, b1_ref, w2_ref, b2_ref, w3_ref, fwd_ref,
                      kre_ref, kim_ref, *, L):
    z = z_ref[...]
    a = jnp.sin(jnp.dot(z, w1_ref[...], precision=HIGHEST, preferred_element_type=F32) + b1_ref[...])
    a = jnp.sin(jnp.dot(a, w2_ref[...], precision=HIGHEST, preferred_element_type=F32) + b2_ref[...])
    h = jnp.dot(a, w3_ref[...], precision=HIGHEST, preferred_element_type=F32)
    decay = jnp.exp(-z[:, 0:1] * dl_ref[...])
    not_first = lax.broadcasted_iota(jnp.int32, (L, HY_W), 0) > 0
    sums, diffs = [], []
    for o in range(2):
        fw = h[:, (2 * o) * HY_W:(2 * o + 1) * HY_W] * decay
        bw = jnp.where(not_first, h[:, (2 * o + 1) * HY_W:(2 * o + 2) * HY_W] * decay, 0.0)
        sums.append(fw + bw)
        diffs.append(fw - bw)
    p = _dot(fwd_ref[...], jnp.concatenate(sums, 1))
    q = _dot(fwd_ref[L:2 * L, :], jnp.concatenate(diffs, 1))
    kre_ref[...] = p[0:L]
    first = lax.broadcasted_iota(jnp.int32, (L, 2 * HY_W), 0) == 0
    kim_ref[...] = jnp.where(first, p[L:L + 1], q)


def _hy_filters(L, w1p, b1, w2, b2, w3, fwd):
    zemb, deltas = _hyena_embedding(L)
    out = jax.ShapeDtypeStruct((L, 2 * HY_W), F32)
    return pl.pallas_call(
        functools.partial(_hy_filter_kernel, L=L),
        out_shape=(out, out),
        compiler_params=pltpu.CompilerParams(vmem_limit_bytes=VMEM_LIMIT),
        name=f"hy_filters_{L}",
    )(jnp.asarray(zemb), jnp.asarray(deltas), w1p, b1, w2, b2, w3, fwd)


def _hyena_kernel(*refs, L, aliased, seqs):
    if aliased:
        refs = refs[1:]
    hy_ref, cw_ref, cb_ref, bias_ref, kre_ref, kim_ref, fwd_ref, inv_ref, o_ref = refs
    first = lax.broadcasted_iota(jnp.int32, (L, HY_W), 0) == 0

    def long_conv(u, o):
        uf = _dot(fwd_ref[...], u)
        ure, uim = uf[0:L], uf[L:2 * L]
        kre = kre_ref[:, o * HY_W:(o + 1) * HY_W]
        kim = kim_ref[:, o * HY_W:(o + 1) * HY_W]
        yre = jnp.where(first, ure * kre, ure * kre - uim * kim)
        yim = jnp.where(first, uim * kim, ure * kim + uim * kre)
        y = _dot(inv_ref[...], jnp.concatenate([yre, yim], 0))
        return y + u * bias_ref[o:o + 1, :]

    for g in range(seqs):
        sl = slice(g * L, (g + 1) * L)
        x = hy_ref[sl, :].astype(F32)
        rows = lax.broadcasted_iota(jnp.int32, x.shape, 0)
        prev = jnp.where(rows == 0, 0.0, pltpu.roll(x, 1, axis=0))
        nxt = jnp.where(rows == L - 1, 0.0, pltpu.roll(x, L - 1, axis=0))
        z = prev * cw_ref[0:1, :] + x * cw_ref[1:2, :] + nxt * cw_ref[2:3, :] + cb_ref[...]
        v, x1, x2 = z[:, 0:HY_W], z[:, HY_W:2 * HY_W], z[:, 2 * HY_W:3 * HY_W]
        u = x1 * long_conv(v, 0)
        o_ref[sl, :] = (x2 * long_conv(u, 1)).astype(o_ref.dtype)


def _hyena(z_main, prev_out, layer, L, n_seq, row_block0, conv_w, conv_b, bias, kre, kim, fwd, inv):
    aliased = prev_out is not None
    seqs = CTX_SEQS_PER_STEP if L == SEQ else 1
    row_block0 //= seqs
    in_specs = [
        pl.BlockSpec((seqs * L, 3 * HY_W), lambda b: (row_block0 + b, 0)),
        pl.BlockSpec((None, 3, 3 * HY_W), lambda b: (layer, 0, 0)),
        pl.BlockSpec((None, 1, 3 * HY_W), lambda b: (layer, 0, 0)),
        pl.BlockSpec((None, 2, HY_W), lambda b: (layer, 0, 0)),
        pl.BlockSpec((L, 2 * HY_W), lambda b: (0, 0)),
        pl.BlockSpec((L, 2 * HY_W), lambda b: (0, 0)),
        pl.BlockSpec((2 * L, L), lambda b: (0, 0)),
        pl.BlockSpec((L, 2 * L), lambda b: (0, 0)),
    ]
    args = [z_main, conv_w, conv_b.reshape(DEPTH, 1, 3 * HY_W), bias, kre, kim, fwd, inv]
    if aliased:
        in_specs = [pl.BlockSpec(memory_space=pl.ANY)] + in_specs
        args = [prev_out] + args
    return pl.pallas_call(
        functools.partial(_hyena_kernel, L=L, aliased=aliased, seqs=seqs),
        out_shape=jax.ShapeDtypeStruct((T_ALL, HY_W), BF16),
        grid=(n_seq // seqs,),
        in_specs=in_specs,
        out_specs=pl.BlockSpec((seqs * L, HY_W), lambda b: (row_block0 + b, 0)),
        input_output_aliases={0: 0} if aliased else {},
        compiler_params=_params("parallel"),
        name=f"hyena_{L}",
    )(*args)


def _win_masks():
    lane = lax.broadcasted_iota(jnp.int32, (1, LANE), 1)
    return lane < WIN_HD, lane >= WIN_HD


def _win_head_operands(q, k, v, h):
    lo_mask, hi_mask = _win_masks()
    col = h // 2
    lo = h % 2 == 0
    q128 = jnp.where(lo_mask if lo else hi_mask, q[:, col * LANE:(col + 1) * LANE], 0.0)
    swap = h in (1, 2)
    if swap:
        k = pltpu.roll(k, WIN_HD, axis=1)
        v = pltpu.roll(v, WIN_HD, axis=1)
    return q128, k, v, lo


def _win_ctx_kernel(sink_ref, q_ref, kv_ref, o_ref, *, layer):
    lo_mask, hi_mask = _win_masks()
    scale = WIN_HD ** -0.5
    for g in range(CTX_SEQS_PER_STEP):
        sl = slice(g * SEQ, (g + 1) * SEQ)
        q = q_ref[sl, :].astype(F32)
        k = kv_ref[sl, 0:LANE].astype(F32)
        v = kv_ref[sl, LANE:2 * LANE].astype(F32)
        cols = []
        for col in range(2):
            acc = None
            for h in (2 * col, 2 * col + 1):
                q128, kk, vv, lo = _win_head_operands(q, k, v, h)
                s = _dot_nt(q128, kk) * scale
                sink = sink_ref[layer, h]
                m = jnp.maximum(jnp.max(s, -1, keepdims=True), sink)
                p = jnp.exp(s - m)
                den = jnp.sum(p, -1, keepdims=True) + jnp.exp(sink - m)
                o = _dot(p, vv) / den
                o = jnp.where(lo_mask if lo else hi_mask, o, 0.0)
                acc = o if acc is None else acc + o
            cols.append(acc)
        o_ref[sl, :] = jnp.concatenate(cols, 1).astype(o_ref.dtype)


def _win_ctx(z_main, sink, layer):
    return pl.pallas_call(
        functools.partial(_win_ctx_kernel, layer=layer),
        out_shape=jax.ShapeDtypeStruct((T_ALL, WIN_HEADS * WIN_HD), BF16),
        grid=(BATCH // CTX_SEQS_PER_STEP,),
        in_specs=[
            pl.BlockSpec(memory_space=pltpu.SMEM),
            pl.BlockSpec((CTX_SEQS_PER_STEP * SEQ, 256), lambda b: (b, COL_WQ // 256)),
            pl.BlockSpec((CTX_SEQS_PER_STEP * SEQ, 256), lambda b: (b, COL_WK // 256)),
        ],
        out_specs=pl.BlockSpec((CTX_SEQS_PER_STEP * SEQ, 256), lambda b: (b, 0)),
        compiler_params=_params("parallel"),
        name="win_ctx",
    )(sink, z_main, z_main)


def _win_lat_kernel(sink_ref, prev_ref, q_ref, kv_ref, ck_ref, cv_ref, c_ref, su_ref, sd_ref,
                    o_ref, *, layer):
    del prev_ref
    L = DEC_SEQ
    half = WIN_HD // 2
    c, su, sd = c_ref[...], su_ref[...], sd_ref[...]
    q = jnp.concatenate(
        [_rope128(q_ref[:, i * LANE:(i + 1) * LANE].astype(F32), c, su, sd, half) for i in range(2)], 1)
    k = _rope128(kv_ref[:, 0:LANE].astype(F32), c, su, sd, half)
    v = kv_ref[:, LANE:2 * LANE].astype(F32)
    ck = ck_ref[...]
    cv = cv_ref[...]
    lo_mask, hi_mask = _win_masks()
    scale = WIN_HD ** -0.5
    nb = L // CHUNK
    cols = []
    for col in range(2):
        acc_blocks = [None] * nb
        for h in (2 * col, 2 * col + 1):
            q128, kk, vv, lo = _win_head_operands(q, k, v, h)
            _, ckk, cvv, _ = _win_head_operands(q, ck, cv, h)
            sink = sink_ref[layer, h]
            for n in range(nb):
                k0 = max(0, (n - 1) * CHUNK)
                k1 = min(L, (n + 2) * CHUNK)
                qn = q128[n * CHUNK:(n + 1) * CHUNK]
                s_loc = _dot_nt(qn, kk[k0:k1]) * scale
                qi = n * CHUNK + lax.broadcasted_iota(jnp.int32, s_loc.shape, 0)
                kj = k0 + lax.broadcasted_iota(jnp.int32, s_loc.shape, 1)
                s_loc = jnp.where(jnp.abs(qi - kj) <= WINDOW, s_loc, NEG)
                s_ctx = _dot_nt(qn, ckk) * scale
                m = jnp.maximum(jnp.maximum(jnp.max(s_loc, -1, keepdims=True),
                                            jnp.max(s_ctx, -1, keepdims=True)), sink)
                p_loc = jnp.exp(s_loc - m)
                p_ctx = jnp.exp(s_ctx - m)
                den = (jnp.sum(p_loc, -1, keepdims=True) + jnp.sum(p_ctx, -1, keepdims=True)
                       + jnp.exp(sink - m))
                o = (_dot(p_loc, vv[k0:k1]) + _dot(p_ctx, cvv)) / den
                o = jnp.where(lo_mask if lo else hi_mask, o, 0.0)
                acc_blocks[n] = o if acc_blocks[n] is None else acc_blocks[n] + o
        cols.append(jnp.concatenate(acc_blocks, 0))
    o_ref[...] = jnp.concatenate(cols, 1).astype(o_ref.dtype)


def _win_lat(z_main, prev_out, sink, cache_k, cache_v, layer):
    c, su, sd, _ = _rope_tables(DEC_SEQ, WIN_HD, LANE)
    rb0 = T_CTX // DEC_SEQ
    tab = pl.BlockSpec((DEC_SEQ, LANE), lambda b: (0, 0))
    cache = pl.BlockSpec((None, None, PAST_LEN, LANE), lambda b: (b, layer, 0, 0))
    return pl.pallas_call(
        functools.partial(_win_lat_kernel, layer=layer),
        out_shape=jax.ShapeDtypeStruct((T_ALL, WIN_HEADS * WIN_HD), BF16),
        grid=(DEC_BATCH,),
        in_specs=[
            pl.BlockSpec(memory_space=pltpu.SMEM),
            pl.BlockSpec(memory_space=pl.ANY),
            pl.BlockSpec((DEC_SEQ, 256), lambda b: (rb0 + b, COL_WQ // 256)),
            pl.BlockSpec((DEC_SEQ, 256), lambda b: (rb0 + b, COL_WK // 256)),
            cache, cache, tab, tab, tab,
        ],
        out_specs=pl.BlockSpec((DEC_SEQ, 256), lambda b: (rb0 + b, 0)),
        input_output_aliases={1: 0},
        compiler_params=_params("parallel"),
        name="win_lat",
    )(sink, prev_out, z_main, z_main, cache_k, cache_v,
      jnp.asarray(c), jnp.asarray(su), jnp.asarray(sd))


def _ret_kernel(*refs, L, layer, ctx):
    if ctx:
        (df_ref, db_ref, q_ref, k_ref, v0_ref, v1_ref, g0_ref, g1_ref,
         o_ref, sf_out, sb_out, s_ref, cross_ref) = refs
        seqs = CTX_SEQS_PER_STEP
    else:
        (df_ref, db_ref, prev_ref, q_ref, k_ref, v0_ref, v1_ref, g0_ref, g1_ref, s0f_ref, s0b_ref,
         o_ref, s_ref, cross_ref) = refs
        seqs = 1
    C = CHUNK
    nc = L // C
    H = RET_HEADS
    qw = H * RET_DK
    vw = H * RET_DV

    def lane_table(width, per_head, fn):
        pos = lax.broadcasted_iota(jnp.int32, (C, per_head), 0).astype(F32)
        return jnp.concatenate([fn(h, pos) for h in range(H)], 1)

    def log_gamma(ref, h):
        d = jnp.full((1, 1), ref[layer, h], F32)
        return jnp.log(jax.nn.sigmoid(d))

    lgf = [log_gamma(df_ref, h) for h in range(H)]
    lgb = [log_gamma(db_ref, h) for h in range(H)]

    def tables(lg, reverse):
        if reverse:
            dq = lane_table(vw, RET_DV, lambda h, pos: jnp.exp((C - pos) * lg[h]))
            dk = lane_table(qw, RET_DK, lambda h, pos: jnp.exp(pos * lg[h]))
        else:
            dq = lane_table(vw, RET_DV, lambda h, pos: jnp.exp((pos + 1.0) * lg[h]))
            dk = lane_table(qw, RET_DK, lambda h, pos: jnp.exp((C - 1.0 - pos) * lg[h]))
        dc = jnp.concatenate([jnp.broadcast_to(jnp.exp(C * lg[h]), (1, RET_DV)) for h in range(H)], 1)
        return dq, dk, dc

    tab_f = tables(lgf, False)
    tab_b = tables(lgb, True)
    ii = lax.broadcasted_iota(jnp.int32, (C, C), 0)
    jj = lax.broadcasted_iota(jnp.int32, (C, C), 1)
    diff = (ii - jj).astype(F32)
    dmats = [jnp.where(diff >= 0, jnp.exp(jnp.maximum(diff, 0.0) * lgf[h]), 0.0)
             + jnp.where(diff <= 0, jnp.exp(jnp.maximum(-diff, 0.0) * lgb[h]), 0.0) for h in range(H)]
    lane_q = lax.broadcasted_iota(jnp.int32, (1, qw), 1) // RET_DK

    srow = lax.broadcasted_iota(jnp.int32, (qw, vw), 0) // RET_DK
    scol = lax.broadcasted_iota(jnp.int32, (qw, vw), 1) // RET_DV
    diag = srow == scol

    for g in range(seqs):
        base = g * L
        rows_all = slice(base, base + L)
        q_all = q_ref[rows_all, :].astype(F32)
        k_all = k_ref[rows_all, :].astype(F32) * (RET_DK ** -0.5)
        v_all = jnp.concatenate([v0_ref[rows_all, :], v1_ref[rows_all, :]], 1).astype(F32)
        g_all = jnp.concatenate([g0_ref[rows_all, :], g1_ref[rows_all, :]], 1).astype(F32)

        def scan(tabs, reverse, s0_ref, s_out):
            dq, dk, dc = tabs
            s_ref[g] = jnp.zeros((qw, vw), F32)
            if s0_ref is not None:
                for h in range(H):
                    s_ref[g, h * RET_DK:(h + 1) * RET_DK, h * RET_DV:(h + 1) * RET_DV] = s0_ref[h]
            order = range(nc - 1, -1, -1) if reverse else range(nc)
            for ci in order:
                sl = slice(ci * C, (ci + 1) * C)
                qc, kc, vc = q_all[sl], k_all[sl], v_all[sl]
                st = s_ref[g]
                cross = _dot(qc, st) * dq
                if reverse:
                    cross_ref[g, sl, :] = cross_ref[g, sl, :] + cross
                else:
                    cross_ref[g, sl, :] = cross
                upd = jnp.where(diag, _dot_tn(kc * dk, vc), 0.0)
                s_ref[g] = st * dc + upd
            if s_out is not None:
                for h in range(H):
                    s_out[g, h] = s_ref[g, h * RET_DK:(h + 1) * RET_DK, h * RET_DV:(h + 1) * RET_DV]

        scan(tab_f, False, None if ctx else s0f_ref, sf_out if ctx else None)
        scan(tab_b, True, None if ctx else s0b_ref, sb_out if ctx else None)

        for h in range(H):
            hv = slice(h * RET_DV, (h + 1) * RET_DV)
            for ci in range(nc):
                sl = slice(ci * C, (ci + 1) * C)
                qh = jnp.where(lane_q == h, q_all[sl], 0.0)
                att = _dot_nt(qh, k_all[sl]) * dmats[h]
                o = _dot(att, v_all[sl, hv]) + cross_ref[g, sl, hv]
                gt = g_all[sl, hv]
                o_ref[base + ci * C:base + (ci + 1) * C, hv] = (
                    (gt * jax.nn.sigmoid(gt)) * _layer_norm(o)).astype(o_ref.dtype)


def _retention(z_main, prev_out, dec_f, dec_b, s0f, s0b, layer, ctx):
    L = SEQ if ctx else DEC_SEQ
    n_seq = BATCH if ctx else DEC_BATCH
    seqs = CTX_SEQS_PER_STEP if ctx else 1
    rb0 = 0 if ctx else T_CTX // DEC_SEQ

    def zcol(col):
        return pl.BlockSpec((seqs * L, 256), lambda b: (rb0 + b, col // 256))

    smem = pl.BlockSpec(memory_space=pltpu.SMEM)
    z_specs = [zcol(COL_RQ), zcol(COL_RK), zcol(COL_RV), zcol(COL_RV + 256),
               zcol(COL_RG), zcol(COL_RG + 256)]
    y_shape = jax.ShapeDtypeStruct((T_ALL, RET_HEADS * RET_DV), BF16)
    y_spec = pl.BlockSpec((seqs * L, RET_HEADS * RET_DV), lambda b: (rb0 + b, 0))
    scratch = [pltpu.VMEM((seqs, RET_HEADS * RET_DK, RET_HEADS * RET_DV), F32),
               pltpu.VMEM((seqs, L, RET_HEADS * RET_DV), F32)]
    kern = functools.partial(_ret_kernel, L=L, layer=layer, ctx=ctx)
    if ctx:
        st_shape = jax.ShapeDtypeStruct((BATCH, RET_HEADS, RET_DK, RET_DV), F32)
        st_spec = pl.BlockSpec((seqs, RET_HEADS, RET_DK, RET_DV), lambda b: (b, 0, 0, 0))
        return pl.pallas_call(
            kern,
            out_shape=(y_shape, st_shape, st_shape),
            grid=(n_seq // seqs,),
            in_specs=[smem, smem] + z_specs,
            out_specs=(y_spec, st_spec, st_spec),
            scratch_shapes=scratch,
            compiler_params=_params("parallel"),
            name="ret_ctx",
        )(dec_f, dec_b, *([z_main] * 6))
    s0_spec = pl.BlockSpec((None, None, RET_HEADS, RET_DK, RET_DV), lambda b: (b, layer, 0, 0, 0))
    return pl.pallas_call(
        kern,
        out_shape=y_shape,
        grid=(n_seq,),
        in_specs=[smem, smem, pl.BlockSpec(memory_space=pl.ANY)] + z_specs + [s0_spec, s0_spec],
        out_specs=y_spec,
        scratch_shapes=scratch,
        input_output_aliases={2: 0},
        compiler_params=_params("parallel"),
        name="ret_lat",
    )(dec_f, dec_b, prev_out, *([z_main] * 6), s0f, s0b)


def _rms_norm(x, g):
    return x * lax.rsqrt(jnp.mean(x * x, -1, keepdims=True) + RMS_EPS) * g


def _mla_attend(qn, qr, kn, kr, vv, o_ref, row0):
    scale = (MLA_NOPE + MLA_ROPE) ** -0.5
    lane_n = lax.broadcasted_iota(jnp.int32, (1, MLA_HEADS * MLA_NOPE), 1) // MLA_NOPE
    lane_r = lax.broadcasted_iota(jnp.int32, (1, LANE), 1)
    kr32 = jnp.where(lane_r < MLA_ROPE, kr, 0.0)
    acc = None
    for h in range(MLA_HEADS):
        qnh = jnp.where(lane_n == h, qn, 0.0)
        qrh = qr if h == 0 else pltpu.roll(qr, LANE - h * MLA_ROPE, axis=1)
        qrh = jnp.where(lane_r < MLA_ROPE, qrh, 0.0)
        s = (_dot_nt(qnh, kn) + _dot_nt(qrh, kr32)) * scale
        m = jnp.max(s, -1, keepdims=True)
        p = jnp.exp(s - m)
        den = jnp.sum(p, -1, keepdims=True)
        o = jnp.where(lane_n == h, _dot(p, vv) / den, 0.0)
        acc = o if acc is None else acc + o
    o_ref[row0:row0 + acc.shape[0], :] = acc.astype(o_ref.dtype)


def _mla_ctx_kernel(cq_ref, ckv_ref, kr_ref, qg_ref, kg_ref, wqn_ref, wqr_ref, wk_ref, wv_ref,
                    o_ref, ckvn_ref):
    cqn = _rms_norm(cq_ref[...].astype(F32), qg_ref[...])
    qn = _dot(cqn, wqn_ref[...])
    qr = _dot(cqn, wqr_ref[...])
    ckvn = _rms_norm(ckv_ref[...].astype(F32), kg_ref[...])
    ckvn_ref[...] = ckvn
    kn = _dot(ckvn, wk_ref[...])
    vv = _dot(ckvn, wv_ref[...])
    for g in range(CTX_SEQS_PER_STEP):
        sl = slice(g * SEQ, (g + 1) * SEQ)
        _mla_attend(qn[sl], qr[sl], kn[sl], kr_ref[sl, :].astype(F32), vv[sl], o_ref, g * SEQ)


def _mla_weight_specs(layer):
    return [
        pl.BlockSpec((None, 1, MLA_Q_LORA), lambda b: (layer, 0, 0)),
        pl.BlockSpec((None, 1, MLA_KV_LORA), lambda b: (layer, 0, 0)),
        pl.BlockSpec((None, MLA_Q_LORA, MLA_HEADS * MLA_NOPE), lambda b: (layer, 0, 0)),
        pl.BlockSpec((None, MLA_Q_LORA, MLA_HEADS * MLA_ROPE), lambda b: (layer, 0, 0)),
        pl.BlockSpec((None, MLA_KV_LORA, MLA_HEADS * MLA_NOPE), lambda b: (layer, 0, 0)),
        pl.BlockSpec((None, MLA_KV_LORA, MLA_HEADS * MLA_V), lambda b: (layer, 0, 0)),
    ]


def _mla_ctx(z_main, weights, layer):
    return pl.pallas_call(
        _mla_ctx_kernel,
        out_shape=(jax.ShapeDtypeStruct((T_ALL, MLA_HEADS * MLA_V), BF16),
                   jax.ShapeDtypeStruct((T_CTX, MLA_KV_LORA), F32)),
        grid=(BATCH // CTX_SEQS_PER_STEP,),
        in_specs=[
            pl.BlockSpec((CTX_SEQS_PER_STEP * SEQ, 256), lambda b: (b, COL_CQ // 256)),
            pl.BlockSpec((CTX_SEQS_PER_STEP * SEQ, LANE), lambda b: (b, COL_CKV // LANE)),
            pl.BlockSpec((CTX_SEQS_PER_STEP * SEQ, LANE), lambda b: (b, COL_KROPE // LANE)),
        ] + _mla_weight_specs(layer),
        out_specs=(pl.BlockSpec((CTX_SEQS_PER_STEP * SEQ, 256), lambda b: (b, 0)),
                   pl.BlockSpec((CTX_SEQS_PER_STEP * SEQ, MLA_KV_LORA), lambda b: (b, 0))),
        compiler_params=_params("parallel"),
        name="mla_ctx",
    )(z_main, z_main, z_main, *weights)


def _mla_lat_kernel(prev_ref, cq_ref, ckv_ref, kr_ref, cckv_ref, ckr_ref, c_ref, su_ref, sd_ref,
                    qg_ref, kg_ref, wqn_ref, wqr_ref, wk_ref, wv_ref, o_ref):
    del prev_ref
    half = MLA_ROPE // 2
    c, su, sd = c_ref[...], su_ref[...], sd_ref[...]
    cqn = _rms_norm(cq_ref[...].astype(F32), qg_ref[...])
    qn = _dot(cqn, wqn_ref[...])
    qr = _rope128(_dot(cqn, wqr_ref[...]), c, su, sd, half)
    ckvn = _rms_norm(ckv_ref[...].astype(F32), kg_ref[...])
    ckv_all = jnp.concatenate([ckvn, cckv_ref[...]], 0)
    kn = _dot(ckv_all, wk_ref[...])
    vv = _dot(ckv_all, wv_ref[...])
    kr = jnp.concatenate([_rope128(kr_ref[...].astype(F32), c, su, sd, half), ckr_ref[...]], 0)
    for n in range(DEC_SEQ // 256):
        rows = slice(n * 256, (n + 1) * 256)
        _mla_attend(qn[rows], qr[rows], kn, kr, vv, o_ref, n * 256)


def _mla_lat(z_main, prev_out, cache_ckv, cache_kr_pad, weights, layer):
    c, su, sd, _ = _rope_tables(DEC_SEQ, MLA_ROPE, LANE)
    rb0 = T_CTX // DEC_SEQ
    tab = pl.BlockSpec((DEC_SEQ, LANE), lambda b: (0, 0))
    cache = pl.BlockSpec((None, None, PAST_LEN, LANE), lambda b: (b, layer, 0, 0))
    return pl.pallas_call(
        _mla_lat_kernel,
        out_shape=jax.ShapeDtypeStruct((T_ALL, MLA_HEADS * MLA_V), BF16),
        grid=(DEC_BATCH,),
        in_specs=[
            pl.BlockSpec(memory_space=pl.ANY),
            pl.BlockSpec((DEC_SEQ, 256), lambda b: (rb0 + b, COL_CQ // 256)),
            pl.BlockSpec((DEC_SEQ, LANE), lambda b: (rb0 + b, COL_CKV // LANE)),
            pl.BlockSpec((DEC_SEQ, LANE), lambda b: (rb0 + b, COL_KROPE // LANE)),
            cache, cache, tab, tab, tab,
        ] + _mla_weight_specs(layer),
        out_specs=pl.BlockSpec((DEC_SEQ, 256), lambda b: (rb0 + b, 0)),
        input_output_aliases={0: 0},
        compiler_params=_params("parallel"),
        name="mla_lat",
    )(prev_out, z_main, z_main, z_main, cache_ckv, cache_kr_pad,
      jnp.asarray(c), jnp.asarray(su), jnp.asarray(sd), *weights)


def _route(logits_t, rb):
    scores = jax.nn.sigmoid(logits_t)
    biased = scores + rb
    sc = [scores[e:e + 1, :] for e in range(N_EXPERTS)]
    bi = [biased[e:e + 1, :] for e in range(N_EXPERTS)]
    epg = EXPERTS_PER_GROUP
    gsum = []
    for g in range(N_GROUPS):
        v = bi[g * epg:(g + 1) * epg]
        best = None
        for a in range(epg):
            for b in range(a + 1, epg):
                pair = v[a] + v[b]
                best = pair if best is None else jnp.maximum(best, pair)
        gsum.append(best)
    combine = []
    sel = []
    for g in range(N_GROUPS):
        is_best = None
        for g2 in range(N_GROUPS):
            if g2 == g:
                continue
            c = gsum[g] > gsum[g2] if g2 < g else gsum[g] >= gsum[g2]
            is_best = c if is_best is None else jnp.logical_and(is_best, c)
        for a in range(epg):
            e = g * epg + a
            rank = jnp.zeros_like(bi[e])
            for b in range(epg):
                if b == a:
                    continue
                e2 = g * epg + b
                ahead = bi[e2] >= bi[e] if b < a else bi[e2] > bi[e]
                rank = rank + jnp.where(ahead, 1.0, 0.0)
            sel.append(jnp.logical_and(is_best, rank < 2.0))
    wsum = None
    for e in range(N_EXPERTS):
        w = jnp.where(sel[e], sc[e], 0.0)
        wsum = w if wsum is None else wsum + w
    for e in range(N_EXPERTS):
        combine.append(jnp.where(sel[e], ROUTE_SCALE * sc[e] / wsum, 0.0))
    return jnp.concatenate(combine, 0)


def _merge_kernel(ya_ref, yb_ref, yc_ref, yd_ref, gt_ref, xc_ref, xl_ref, m_ref,
                  wa_ref, wb_ref, wc_ref, wd_ref, wo_ref, g_ref, b_ref, rwh_ref, rwl_ref, rb_ref,
                  x1_ref, h2_ref, cmb_ref, *, ctx_tiles, sub_rows):
    D = D_MODEL
    g1 = m_ref[:, 2 * D:3 * D]
    s2 = m_ref[:, 3 * D:4 * D]
    sc2 = m_ref[:, 4 * D:5 * D]
    is_ctx = pl.program_id(0) < ctx_tiles
    branches = ((ya_ref, wa_ref), (yb_ref, wb_ref), (yc_ref, wc_ref), (yd_ref, wd_ref))
    for r0 in range(0, x1_ref.shape[0], sub_rows):
        rows = slice(r0, r0 + sub_rows)
        merged = None
        for i, (y_ref, w) in enumerate(branches):
            t = gt_ref[rows, i * D:(i + 1) * D].astype(F32) * jnp.dot(
                y_ref[rows, :], w[...], preferred_element_type=F32)
            merged = t if merged is None else merged + t
        out1 = jnp.dot(merged.astype(BF16), wo_ref[...], preferred_element_type=F32)
        x = jnp.where(is_ctx, xc_ref[rows, :], xl_ref[rows, :])
        x1 = _layer_norm(ALPHA * x + g1 * out1) * g_ref[...] + b_ref[...]
        x1_ref[rows, :] = x1
        h2 = _layer_norm(x1) * (1.0 + sc2) + s2
        h2_hi = h2.astype(BF16)
        h2_ref[rows, :] = h2_hi
        h2_lo = (h2 - h2_hi.astype(F32)).astype(BF16)
        logits = (jnp.dot(h2_hi, rwh_ref[...], preferred_element_type=F32)
                  + (jnp.dot(h2_lo, rwh_ref[...], preferred_element_type=F32)
                     + jnp.dot(h2_hi, rwl_ref[...], preferred_element_type=F32)))
        cmb_ref[:, rows] = _route(logits.T[0:N_EXPERTS], rb_ref[...])


def _merge(ya, yb, yc, yd, gates, x_ctx, x_lat, mods_l, w_br, w_out_bf, ln1_g, ln1_b,
           router_w_parts, router_b, layer):
    tm = 512
    row = _mod_row(tm)
    D = D_MODEL
    ctx_tiles = T_CTX // tm

    def tile(w):
        return pl.BlockSpec((tm, w), lambda i: (i, 0))

    def weight(k, n):
        return pl.BlockSpec((None, k, n), lambda i: (layer, 0, 0))

    return pl.pallas_call(
        functools.partial(_merge_kernel, ctx_tiles=ctx_tiles, sub_rows=256),
        out_shape=(jax.ShapeDtypeStruct((T_ALL, D), F32),
                   jax.ShapeDtypeStruct((T_ALL, D), BF16),
                   jax.ShapeDtypeStruct((N_EXPERTS, T_ALL), F32)),
        grid=(T_ALL // tm,),
        in_specs=[
            tile(256), tile(256), tile(512), tile(256), tile(4 * D),
            pl.BlockSpec((tm, D), lambda i: (jnp.minimum(i, ctx_tiles - 1), 0)),
            pl.BlockSpec((tm, D), lambda i: (jnp.maximum(i - ctx_tiles, 0), 0)),
            pl.BlockSpec((None, 1, 6 * D), lambda i: (row(i), 0, 0)),
            weight(256, D), weight(256, D), weight(512, D), weight(256, D), weight(D, D),
            weight(1, D), weight(1, D),
            pl.BlockSpec((D, LANE), lambda i: (0, 0)),
            pl.BlockSpec((D, LANE), lambda i: (0, 0)),
            pl.BlockSpec((N_EXPERTS, 1), lambda i: (0, 0)),
        ],
        out_specs=(tile(D), tile(D), pl.BlockSpec((N_EXPERTS, tm), lambda i: (0, i))),
        compiler_params=_params("parallel"),
        name="merge",
    )(ya, yb, yc, yd, gates, x_ctx, x_lat, mods_l, *w_br, w_out_bf,
      ln1_g.reshape(DEPTH, 1, D), ln1_b.reshape(DEPTH, 1, D), *router_w_parts,
      router_b.reshape(N_EXPERTS, 1))


MOE_EXPERTS_PER_STEP = 2


def _moe_kernel(h_ref, c_ref, x1_ref, m_ref, wg_ref, wu_ref, wd_ref, g_ref, b_ref, o_ref, acc_ref):
    eg = pl.program_id(1)

    @pl.when(eg == 0)
    def _():
        acc_ref[...] = jnp.zeros_like(acc_ref)

    h = h_ref[...]
    cmb = c_ref[...]
    lane = lax.broadcasted_iota(jnp.int32, cmb.shape, 1)
    hid = []
    for k in range(MOE_EXPERTS_PER_STEP):
        gate = jnp.dot(h, wg_ref[k].astype(BF16), preferred_element_type=F32)
        up = jnp.dot(h, wu_ref[k].astype(BF16), preferred_element_type=F32)
        e = eg * MOE_EXPERTS_PER_STEP + k
        ce = jnp.sum(jnp.where(lane == e, cmb, 0.0), -1, keepdims=True)
        hid.append(((gate * jax.nn.sigmoid(gate)) * up * ce).astype(BF16))
    wd = wd_ref[...].reshape(MOE_EXPERTS_PER_STEP * D_EXPERT, D_MODEL).astype(BF16)
    acc_ref[...] += jnp.dot(jnp.concatenate(hid, 1), wd, preferred_element_type=F32)

    @pl.when(eg == N_EXPERTS // MOE_EXPERTS_PER_STEP - 1)
    def _():
        g2 = m_ref[:, 5 * D_MODEL:6 * D_MODEL]
        y = _layer_norm(ALPHA * x1_ref[...] + g2 * acc_ref[...])
        o_ref[...] = y * g_ref[...] + b_ref[...]


def _moe(h2, combine, x1, mods_l, w_gate, w_up, w_down, ln2_g, ln2_b, layer, row0, n_rows):
    tm = 1024
    row = _mod_row(tm)
    D = D_MODEL
    t0 = row0 // tm
    eps = MOE_EXPERTS_PER_STEP
    return pl.pallas_call(
        _moe_kernel,
        out_shape=jax.ShapeDtypeStruct((n_rows, D), F32),
        grid=(n_rows // tm, N_EXPERTS // eps),
        in_specs=[
            pl.BlockSpec((tm, D), lambda i, e: (t0 + i, 0)),
            pl.BlockSpec((tm, N_EXPERTS), lambda i, e: (t0 + i, 0)),
            pl.BlockSpec((tm, D), lambda i, e: (t0 + i, 0)),
            pl.BlockSpec((None, 1, 6 * D), lambda i, e: (row(t0 + i), 0, 0)),
            pl.BlockSpec((None, eps, D, D_EXPERT), lambda i, e: (layer, e, 0, 0)),
            pl.BlockSpec((None, eps, D, D_EXPERT), lambda i, e: (layer, e, 0, 0)),
            pl.BlockSpec((None, eps, D_EXPERT, D), lambda i, e: (layer, e, 0, 0)),
            pl.BlockSpec((None, 1, D), lambda i, e: (layer, 0, 0)),
            pl.BlockSpec((None, 1, D), lambda i, e: (layer, 0, 0)),
        ],
        out_specs=pl.BlockSpec((tm, D), lambda i, e: (i, 0)),
        scratch_shapes=[pltpu.VMEM((tm, D), F32)],
        compiler_params=_params("parallel", "arbitrary"),
        name="moe",
    )(h2, combine, x1, mods_l, w_gate, w_up, w_down,
      ln2_g.reshape(DEPTH, 1, D), ln2_b.reshape(DEPTH, 1, D))


def kernel(x_prompt, x_sample, cache_win_k, cache_win_v, cache_mla_ckv, cache_mla_krope,
           state_ret_fwd, state_ret_bwd, c, c_ctx, w_ada, b_ada, w_in,
           hy_conv_w, hy_conv_b, hy_w1, hy_b1, hy_w2, hy_b2, hy_w3, hy_bias,
           win_sink, ret_decay_fwd, ret_decay_bwd, mla_q_norm, mla_kv_norm, mla_w_uq, mla_w_ukv,
           w_br_a, w_br_b, w_br_c, w_br_d, w_out, ln1_g, ln1_b, ln2_g, ln2_b,
           router_w, router_b, moe_w_gate, moe_w_up, moe_w_down):
    D = D_MODEL
    x_ctx = x_prompt.reshape(T_CTX, D)
    x_lat = x_sample.reshape(T_LAT, D)

    cvec = jnp.zeros((ADA_ROWS, D), F32).at[0].set(c_ctx).at[1:1 + DEC_BATCH].set(c)
    mods = _ada_mods(cvec, w_ada, b_ada)[:, :1 + DEC_BATCH].reshape(DEPTH, 1 + DEC_BATCH, 1, 6 * D)

    w_in_t = jnp.swapaxes(w_in, 1, 2).reshape(DEPTH * IN_COLS, D)
    cache_k = cache_win_k.reshape(DEC_BATCH, DEPTH, PAST_LEN, WIN_KV_HEADS * WIN_HD)
    cache_v = cache_win_v.reshape(DEC_BATCH, DEPTH, PAST_LEN, WIN_KV_HEADS * WIN_HD)
    cache_kr = jnp.pad(cache_mla_krope, ((0, 0), (0, 0), (0, 0), (0, LANE - MLA_ROPE)))

    uq = mla_w_uq.reshape(DEPTH, MLA_Q_LORA, MLA_HEADS, MLA_NOPE + MLA_ROPE)
    ukv = mla_w_ukv.reshape(DEPTH, MLA_KV_LORA, MLA_HEADS, MLA_NOPE + MLA_V)
    mla_weights = (
        mla_q_norm.reshape(DEPTH, 1, MLA_Q_LORA),
        mla_kv_norm.reshape(DEPTH, 1, MLA_KV_LORA),
        uq[..., :MLA_NOPE].reshape(DEPTH, MLA_Q_LORA, MLA_HEADS * MLA_NOPE),
        uq[..., MLA_NOPE:].reshape(DEPTH, MLA_Q_LORA, MLA_HEADS * MLA_ROPE),
        ukv[..., :MLA_NOPE].reshape(DEPTH, MLA_KV_LORA, MLA_HEADS * MLA_NOPE),
        ukv[..., MLA_NOPE:].reshape(DEPTH, MLA_KV_LORA, MLA_HEADS * MLA_V),
    )

    hy_w1p = jnp.pad(hy_w1, ((0, 0), (0, LANE - HY_EMB), (0, 0)))
    dft = {}
    for L in (SEQ, DEC_SEQ):
        fwd, inv = _dft_tables(L)
        dft[L] = (jnp.asarray(fwd).astype(BF16), jnp.asarray(inv).astype(BF16))
    router_w_pad = jnp.pad(router_w, ((0, 0), (0, LANE - N_EXPERTS)))
    router_w_hi = router_w_pad.astype(BF16)
    router_w_parts = (router_w_hi, (router_w_pad - router_w_hi.astype(F32)).astype(BF16))
    w_br = tuple(w.astype(BF16) for w in (w_br_a, w_br_b, w_br_c, w_br_d))
    w_out_bf = w_out.astype(BF16)

    new_k, new_v, new_ckv, new_kr, new_sf, new_sb = [], [], [], [], [], []
    for l in range(DEPTH):
        mods_l = mods[l]
        h = _ln_mod(x_ctx, None, mods_l, 0)
        h = _ln_mod(x_lat, h, mods_l, T_CTX)
        z = _in_proj(h, w_in_t, l, 0, Z_MAIN, Z_MAIN // 2, BF16, gate=False)
        gates = _in_proj(h, w_in_t, l, COL_GATE, 4 * D, D, BF16, gate=True)

        ya = None
        for L, n_seq, rb0 in ((SEQ, BATCH, 0), (DEC_SEQ, DEC_BATCH, T_CTX // DEC_SEQ)):
            fwd, inv = dft[L]
            kre, kim = _hy_filters(L, hy_w1p[l], hy_b1[l][None], hy_w2[l], hy_b2[l][None], hy_w3[l], fwd)
            ya = _hyena(z, ya, l, L, n_seq, rb0, hy_conv_w, hy_conv_b, hy_bias, kre, kim, fwd, inv)

        yb = _win_ctx(z, win_sink, l)
        yb = _win_lat(z, yb, win_sink, cache_k, cache_v, l)

        yc, sf, sb = _retention(z, None, ret_decay_fwd, ret_decay_bwd, None, None, l, ctx=True)
        yc = _retention(z, yc, ret_decay_fwd, ret_decay_bwd, state_ret_fwd, state_ret_bwd, l, ctx=False)

        yd, ckvn = _mla_ctx(z, mla_weights, l)
        yd = _mla_lat(z, yd, cache_mla_ckv, cache_kr, mla_weights, l)

        x1, h2, combine_t = _merge(ya, yb, yc, yd, gates, x_ctx, x_lat, mods_l, w_br, w_out_bf,
                                   ln1_g, ln1_b, router_w_parts, router_b, l)
        moe_args = (h2, combine_t.T, x1, mods_l, moe_w_gate, moe_w_up, moe_w_down, ln2_g, ln2_b, l)
        x_ctx = _moe(*moe_args, 0, T_CTX)
        x_lat = _moe(*moe_args, T_CTX, T_LAT)

        def ctx_cols(col, width):
            return z[:T_CTX, col:col + width].astype(F32)

        new_k.append(ctx_cols(COL_WK, 128).reshape(BATCH, SEQ, WIN_KV_HEADS, WIN_HD))
        new_v.append(ctx_cols(COL_WV, 128).reshape(BATCH, SEQ, WIN_KV_HEADS, WIN_HD))
        new_ckv.append(ckvn.reshape(BATCH, SEQ, MLA_KV_LORA))
        new_kr.append(ctx_cols(COL_KROPE, MLA_ROPE).reshape(BATCH, SEQ, MLA_ROPE))
        new_sf.append(sf)
        new_sb.append(sb)

    y_prompt = x_ctx.reshape(BATCH, SEQ, D)
    y_sample = x_lat.reshape(DEC_BATCH, DEC_SEQ, D)
    return (y_prompt, y_sample, jnp.stack(new_k, 1), jnp.stack(new_v, 1), jnp.stack(new_ckv, 1),
            jnp.stack(new_kr, 1), jnp.stack(new_sf, 1), jnp.stack(new_sb, 1))
```

```python
import functools
import math

import numpy as np
import jax
import jax.numpy as jnp
from jax import lax
from jax.experimental import pallas as pl
from jax.experimental.pallas import tpu as pltpu

F32 = jnp.float32
BF16 = jnp.bfloat16

D_MODEL = 1024
BATCH = 16
SEQ = 256
DEPTH = 2
DEC_BATCH = 2
DEC_SEQ = 1024
PAST_LEN = 256
GRID_W = 64
CHUNK = 128
ROPE_BASE = 10000.0
NEG = -1e30
LN_EPS = 1e-5
RMS_EPS = 1e-6

HY_W = 256
HY_BANDS = 16
HY_EMB = 1 + 2 * HY_BANDS
HY_FFN = 64
HY_FAST_DECAY = 0.3
HY_SLOW_DECAY = 1.5
HY_TARGET = 1e-2

WIN_HEADS = 4
WIN_KV_HEADS = 2
WIN_HD = 64
WINDOW = 128

RET_HEADS = 4
RET_DK = 64
RET_DV = 128

MLA_HEADS = 4
MLA_Q_LORA = 256
MLA_KV_LORA = 128
MLA_NOPE = 64
MLA_ROPE = 32
MLA_V = 64

N_EXPERTS = 16
N_GROUPS = 4
EXPERTS_PER_GROUP = N_EXPERTS // N_GROUPS
D_EXPERT = 256
ROUTE_SCALE = 2.5

ALPHA = (2.0 * DEPTH) ** 0.25

T_CTX = BATCH * SEQ
T_LAT = DEC_BATCH * DEC_SEQ
T_ALL = T_CTX + T_LAT

COL_HY = 0
COL_WQ = 768
COL_WK = 1024
COL_WV = 1152
COL_RQ = 1280
COL_RK = 1536
COL_RV = 1792
COL_RG = 2304
COL_CQ = 2816
COL_CKV = 3072
COL_KROPE = 3200
COL_GATE = 3232
IN_COLS = COL_GATE + 4 * D_MODEL
Z_MAIN = 3328

LANE = 128
CTX_SEQS_PER_STEP = 4
VMEM_LIMIT = 56 * 1024 * 1024


def _params(*sem):
    return pltpu.CompilerParams(dimension_semantics=sem, vmem_limit_bytes=VMEM_LIMIT)


def _dot(a, b):
    return jnp.dot(a.astype(BF16), b.astype(BF16), preferred_element_type=F32)


def _dot_split(a, b):
    a_hi = a.astype(BF16)
    a_lo = (a - a_hi.astype(F32)).astype(BF16)
    b_hi = b.astype(BF16)
    b_lo = (b - b_hi.astype(F32)).astype(BF16)

    def mm(x, y):
        return jnp.dot(x, y, preferred_element_type=F32)

    return mm(a_hi, b_hi) + (mm(a_lo, b_hi) + mm(a_hi, b_lo))


def _dot_nt(a, b):
    return lax.dot_general(a.astype(BF16), b.astype(BF16), (((1,), (1,)), ((), ())),
                           preferred_element_type=F32)


def _dot_tn(a, b):
    return lax.dot_general(a.astype(BF16), b.astype(BF16), (((0,), (0,)), ((), ())),
                           preferred_element_type=F32)


def _layer_norm(x):
    mu = jnp.mean(x, -1, keepdims=True)
    xc = x - mu
    var = jnp.mean(xc * xc, -1, keepdims=True)
    return xc * lax.rsqrt(var + LN_EPS)


def _mod_row(tile_rows):
    def row(i):
        start = i * tile_rows
        return jnp.where(start < T_CTX, 0, 1 + (start - T_CTX) // DEC_SEQ)
    return row


@functools.lru_cache(maxsize=None)
def _dft_tables(L):
    f = np.arange(L, dtype=np.int64)[:, None]
    s = np.arange(L, dtype=np.int64)[None, :]
    ang = np.pi * ((f * s) % (2 * L)).astype(np.float64) / L
    cos = np.cos(ang)
    sin = np.sin(ang)
    alt = np.where(np.arange(L) % 2 == 0, 1.0, -1.0)
    fwd_im = -sin
    fwd_im[0, :] = alt
    fwd = np.concatenate([cos, fwd_im], 0)
    inv_re = cos.T / L
    inv_re[:, 0] = 1.0 / (2 * L)
    inv_im = -sin.T / L
    inv_im[:, 0] = alt / (2 * L)
    inv = np.concatenate([inv_re, inv_im], 1)
    return fwd.astype(np.float32), inv.astype(np.float32)


@functools.lru_cache(maxsize=None)
def _hyena_embedding(L):
    t01 = np.linspace(0.0, 1.0, L, dtype=np.float64)[:, None]
    bands = np.linspace(1e-4, HY_BANDS - 1, HY_BANDS, dtype=np.float64)
    ang = (2.0 * math.pi / L) * np.arange(L, dtype=np.float64)[:, None] * bands[None, :]
    z = np.concatenate([t01, np.cos(ang), -np.sin(ang)], -1)
    zp = np.zeros((L, LANE), np.float64)
    zp[:, :HY_EMB] = z
    deltas = np.abs(np.linspace(math.log(HY_TARGET) / HY_SLOW_DECAY,
                                math.log(HY_TARGET) / HY_FAST_DECAY, HY_W, dtype=np.float64))
    return zp.astype(np.float32), deltas[None, :].astype(np.float32)


@functools.lru_cache(maxsize=None)
def _rope_tables(L, rot_dim, width):
    rows = L // GRID_W
    n_freq = rot_dim // 4
    half = rot_dim // 2
    inv = ROPE_BASE ** (-np.arange(n_freq, dtype=np.float64) / n_freq)
    pos = np.arange(L)
    row = (pos // GRID_W).astype(np.float64)
    col = (pos % GRID_W).astype(np.float64)
    ang = np.concatenate([row[:, None] * inv, col[:, None] * inv], -1)
    cos, sin = np.cos(ang), np.sin(ang)
    zero = np.zeros_like(sin)
    c = np.tile(np.concatenate([cos, cos], -1), (1, width // rot_dim))
    s_up = np.tile(np.concatenate([-sin, zero], -1), (1, width // rot_dim))
    s_dn = np.tile(np.concatenate([zero, sin], -1), (1, width // rot_dim))
    return c.astype(np.float32), s_up.astype(np.float32), s_dn.astype(np.float32), half


def _rope128(x, c, s_up, s_dn, half):
    up = pltpu.roll(x, LANE - half, axis=1)
    dn = pltpu.roll(x, half, axis=1)
    return x * c + up * s_up + dn * s_dn


def _ada_kernel(c_ref, w_ref, b_ref, o_ref):
    cv = c_ref[...]
    s = cv * jax.nn.sigmoid(cv)
    s_hi = s.astype(BF16)
    s_lo = (s - s_hi.astype(F32)).astype(BF16)
    w = w_ref[...]
    w_hi = w.astype(BF16)
    w_lo = (w - w_hi.astype(F32)).astype(BF16)
    rows = s.shape[0]
    both = jnp.dot(jnp.concatenate([s_hi, s_lo], 0), w_hi, preferred_element_type=F32)
    o_ref[...] = (both[0:rows] + (both[rows:2 * rows] + jnp.dot(s_hi, w_lo, preferred_element_type=F32))
                  + b_ref[...])


ADA_ROWS = 16


def _ada_mods(cvec, w_ada, b_ada):
    tn = 1536
    n = 6 * D_MODEL
    return pl.pallas_call(
        _ada_kernel,
        out_shape=jax.ShapeDtypeStruct((DEPTH, ADA_ROWS, n), F32),
        grid=(DEPTH, n // tn),
        in_specs=[
            pl.BlockSpec((ADA_ROWS, D_MODEL), lambda l, j: (0, 0)),
            pl.BlockSpec((None, D_MODEL, tn), lambda l, j: (l, 0, j)),
            pl.BlockSpec((None, 1, tn), lambda l, j: (l, 0, j)),
        ],
        out_specs=pl.BlockSpec((None, ADA_ROWS, tn), lambda l, j: (l, 0, j)),
        compiler_params=_params("parallel", "parallel"),
        name="ada_mods",
    )(cvec, w_ada, b_ada.reshape(DEPTH, 1, n))


def _lnmod_kernel(*refs):
    x_ref, m_ref, h_ref = refs[-3:]
    y = _layer_norm(x_ref[...])
    s1 = m_ref[:, 0:D_MODEL]
    sc1 = m_ref[:, D_MODEL:2 * D_MODEL]
    h_ref[...] = (y * (1.0 + sc1) + s1).astype(h_ref.dtype)


def _ln_mod(x_group, prev_out, mods_l, row0):
    tm = 512
    row = _mod_row(tm)
    tile0 = row0 // tm
    in_specs = [
        pl.BlockSpec((tm, D_MODEL), lambda i: (i, 0)),
        pl.BlockSpec((None, 1, 6 * D_MODEL), lambda i: (row(tile0 + i), 0, 0)),
    ]
    args = [x_group, mods_l]
    if prev_out is not None:
        in_specs = [pl.BlockSpec(memory_space=pl.ANY)] + in_specs
        args = [prev_out] + args
    return pl.pallas_call(
        _lnmod_kernel,
        out_shape=jax.ShapeDtypeStruct((T_ALL, D_MODEL), BF16),
        grid=(x_group.shape[0] // tm,),
        in_specs=in_specs,
        out_specs=pl.BlockSpec((tm, D_MODEL), lambda i: (tile0 + i, 0)),
        input_output_aliases={} if prev_out is None else {0: 0},
        compiler_params=_params("parallel"),
        name="ln_mod",
    )(*args)


def _proj_kernel(h_ref, w_ref, o_ref, wb_ref, *, gate):
    @pl.when(pl.program_id(1) == 0)
    def _():
        wb_ref[...] = w_ref[...].T.astype(BF16)

    if not gate:
        o_ref[...] = jnp.dot(h_ref[...], wb_ref[...], preferred_element_type=F32).astype(o_ref.dtype)
        return
    sub = 2 * LANE
    for c0 in range(0, o_ref.shape[1], sub):
        r = jnp.dot(h_ref[...], wb_ref[:, c0:c0 + sub], preferred_element_type=F32)
        o_ref[:, c0:c0 + sub] = (0.5 * jnp.tanh(0.5 * r) + 0.5).astype(o_ref.dtype)


def _in_proj(h, w_t, layer, col0, n_cols, tn, out_dtype, gate):
    tm = 2048
    return pl.pallas_call(
        functools.partial(_proj_kernel, gate=gate),
        out_shape=jax.ShapeDtypeStruct((T_ALL, n_cols), out_dtype),
        grid=(n_cols // tn, T_ALL // tm),
        in_specs=[
            pl.BlockSpec((tm, D_MODEL), lambda j, i: (i, 0)),
            pl.BlockSpec((pl.Element(tn), pl.Element(D_MODEL)),
                         lambda j, i: (pl.multiple_of(layer * IN_COLS + col0 + j * tn, 8), 0)),
        ],
        out_specs=pl.BlockSpec((tm, tn), lambda j, i: (i, j)),
        scratch_shapes=[pltpu.VMEM((D_MODEL, tn), BF16)],
        compiler_params=_params("parallel", "arbitrary"),
        name="gate_proj" if gate else "in_proj",
    )(h, w_t)


def _hy_filter_kernel(z_ref, dl_ref, w1_ref, b1_ref, w2_ref, b2_ref, w3_ref, fwd_ref,
                      kre_ref, kim_ref, *, L):
    z = z_ref[...]
    a = jnp.sin(_dot_split(z, w1_ref[...]) + b1_ref[...])
    a = jnp.sin(_dot_split(a, w2_ref[...]) + b2_ref[...])
    h = _dot_split(a, w3_ref[...])
    decay = jnp.exp(-z[:, 0:1] * dl_ref[...])
    not_first = lax.broadcasted_iota(jnp.int32, (L, HY_W), 0) > 0
    sums, diffs = [], []
    for o in range(2):
        fw = h[:, (2 * o) * HY_W:(2 * o + 1) * HY_W] * decay
        bw = jnp.where(not_first, h[:, (2 * o + 1) * HY_W:(2 * o + 2) * HY_W] * decay, 0.0)
        sums.append(fw + bw)
        diffs.append(fw - bw)
    p = _dot(fwd_ref[...], jnp.concatenate(sums, 1))
    q = _dot(fwd_ref[L:2 * L, :], jnp.concatenate(diffs, 1))
    kre_ref[...] = p[0:L]
    first = lax.broadcasted_iota(jnp.int32, (L, 2 * HY_W), 0) == 0
    kim_ref[...] = jnp.where(first, p[L:L + 1], q)


def _hy_filters(L, w1p, b1, w2, b2, w3, fwd):
    zemb, deltas = _hyena_embedding(L)
    out = jax.ShapeDtypeStruct((L, 2 * HY_W), F32)
    return pl.pallas_call(
        functools.partial(_hy_filter_kernel, L=L),
        out_shape=(out, out),
        compiler_params=pltpu.CompilerParams(vmem_limit_bytes=VMEM_LIMIT),
        name=f"hy_filters_{L}",
    )(jnp.asarray(zemb), jnp.asarray(deltas), w1p, b1, w2, b2, w3, fwd)


def _hyena_kernel(*refs, L, aliased, seqs):
    if aliased:
        refs = refs[1:]
    hy_ref, cw_ref, cb_ref, bias_ref, kre_ref, kim_ref, fwd_ref, inv_ref, o_ref = refs
    first = lax.broadcasted_iota(jnp.int32, (L, HY_W), 0) == 0

    def long_conv(u, o):
        uf = _dot(fwd_ref[...], u)
        ure, uim = uf[0:L], uf[L:2 * L]
        kre = kre_ref[:, o * HY_W:(o + 1) * HY_W]
        kim = kim_ref[:, o * HY_W:(o + 1) * HY_W]
        yre = jnp.where(first, ure * kre, ure * kre - uim * kim)
        yim = jnp.where(first, uim * kim, ure * kim + uim * kre)
        y = _dot(inv_ref[...], jnp.concatenate([yre, yim], 0))
        return y + u * bias_ref[o:o + 1, :]

    for g in range(seqs):
        sl = slice(g * L, (g + 1) * L)
        x = hy_ref[sl, :].astype(F32)
        rows = lax.broadcasted_iota(jnp.int32, x.shape, 0)
        prev = jnp.where(rows == 0, 0.0, pltpu.roll(x, 1, axis=0))
        nxt = jnp.where(rows == L - 1, 0.0, pltpu.roll(x, L - 1, axis=0))
        z = prev * cw_ref[0:1, :] + x * cw_ref[1:2, :] + nxt * cw_ref[2:3, :] + cb_ref[...]
        v, x1, x2 = z[:, 0:HY_W], z[:, HY_W:2 * HY_W], z[:, 2 * HY_W:3 * HY_W]
        u = x1 * long_conv(v, 0)
        o_ref[sl, :] = (x2 * long_conv(u, 1)).astype(o_ref.dtype)


def _hyena(z_main, prev_out, layer, L, n_seq, row_block0, conv_w, conv_b, bias, kre, kim, fwd, inv):
    aliased = prev_out is not None
    seqs = CTX_SEQS_PER_STEP if L == SEQ else 1
    row_block0 //= seqs
    in_specs = [
        pl.BlockSpec((seqs * L, 3 * HY_W), lambda b: (row_block0 + b, 0)),
        pl.BlockSpec((None, 3, 3 * HY_W), lambda b: (layer, 0, 0)),
        pl.BlockSpec((None, 1, 3 * HY_W), lambda b: (layer, 0, 0)),
        pl.BlockSpec((None, 2, HY_W), lambda b: (layer, 0, 0)),
        pl.BlockSpec((L, 2 * HY_W), lambda b: (0, 0)),
        pl.BlockSpec((L, 2 * HY_W), lambda b: (0, 0)),
        pl.BlockSpec((2 * L, L), lambda b: (0, 0)),
        pl.BlockSpec((L, 2 * L), lambda b: (0, 0)),
    ]
    args = [z_main, conv_w, conv_b.reshape(DEPTH, 1, 3 * HY_W), bias, kre, kim, fwd, inv]
    if aliased:
        in_specs = [pl.BlockSpec(memory_space=pl.ANY)] + in_specs
        args = [prev_out] + args
    return pl.pallas_call(
        functools.partial(_hyena_kernel, L=L, aliased=aliased, seqs=seqs),
        out_shape=jax.ShapeDtypeStruct((T_ALL, HY_W), BF16),
        grid=(n_seq // seqs,),
        in_specs=in_specs,
        out_specs=pl.BlockSpec((seqs * L, HY_W), lambda b: (row_block0 + b, 0)),
        input_output_aliases={0: 0} if aliased else {},
        compiler_params=_params("parallel"),
        name=f"hyena_{L}",
    )(*args)


def _win_masks():
    lane = lax.broadcasted_iota(jnp.int32, (1, LANE), 1)
    return lane < WIN_HD, lane >= WIN_HD


def _win_head_operands(q, k, v, h):
    lo_mask, hi_mask = _win_masks()
    col = h // 2
    lo = h % 2 == 0
    q128 = jnp.where(lo_mask if lo else hi_mask, q[:, col * LANE:(col + 1) * LANE], 0.0)
    swap = h in (1, 2)
    if swap:
        k = pltpu.roll(k, WIN_HD, axis=1)
        v = pltpu.roll(v, WIN_HD, axis=1)
    return q128, k, v, lo


def _win_ctx_kernel(sink_ref, q_ref, kv_ref, o_ref, *, layer):
    lo_mask, hi_mask = _win_masks()
    scale = WIN_HD ** -0.5
    for g in range(CTX_SEQS_PER_STEP):
        sl = slice(g * SEQ, (g + 1) * SEQ)
        q = q_ref[sl, :].astype(F32)
        k = kv_ref[sl, 0:LANE].astype(F32)
        v = kv_ref[sl, LANE:2 * LANE].astype(F32)
        cols = []
        for col in range(2):
            acc = None
            for h in (2 * col, 2 * col + 1):
                q128, kk, vv, lo = _win_head_operands(q, k, v, h)
                s = _dot_nt(q128, kk) * scale
                sink = sink_ref[layer, h]
                m = jnp.maximum(jnp.max(s, -1, keepdims=True), sink)
                p = jnp.exp(s - m)
                den = jnp.sum(p, -1, keepdims=True) + jnp.exp(sink - m)
                o = _dot(p, vv) / den
                o = jnp.where(lo_mask if lo else hi_mask, o, 0.0)
                acc = o if acc is None else acc + o
            cols.append(acc)
        o_ref[sl, :] = jnp.concatenate(cols, 1).astype(o_ref.dtype)


def _win_ctx(z_main, sink, layer):
    return pl.pallas_call(
        functools.partial(_win_ctx_kernel, layer=layer),
        out_shape=jax.ShapeDtypeStruct((T_ALL, WIN_HEADS * WIN_HD), BF16),
        grid=(BATCH // CTX_SEQS_PER_STEP,),
        in_specs=[
            pl.BlockSpec(memory_space=pltpu.SMEM),
            pl.BlockSpec((CTX_SEQS_PER_STEP * SEQ, 256), lambda b: (b, COL_WQ // 256)),
            pl.BlockSpec((CTX_SEQS_PER_STEP * SEQ, 256), lambda b: (b, COL_WK // 256)),
        ],
        out_specs=pl.BlockSpec((CTX_SEQS_PER_STEP * SEQ, 256), lambda b: (b, 0)),
        compiler_params=_params("parallel"),
        name="win_ctx",
    )(sink, z_main, z_main)


def _win_lat_kernel(sink_ref, prev_ref, q_ref, kv_ref, ck_ref, cv_ref, c_ref, su_ref, sd_ref,
                    o_ref, *, layer):
    del prev_ref
    L = DEC_SEQ
    half = WIN_HD // 2
    c, su, sd = c_ref[...], su_ref[...], sd_ref[...]
    q = jnp.concatenate(
        [_rope128(q_ref[:, i * LANE:(i + 1) * LANE].astype(F32), c, su, sd, half) for i in range(2)], 1)
    k = _rope128(kv_ref[:, 0:LANE].astype(F32), c, su, sd, half)
    v = kv_ref[:, LANE:2 * LANE].astype(F32)
    ck = ck_ref[...]
    cv = cv_ref[...]
    lo_mask, hi_mask = _win_masks()
    scale = WIN_HD ** -0.5
    nb = L // CHUNK
    cols = []
    for col in range(2):
        acc_blocks = [None] * nb
        for h in (2 * col, 2 * col + 1):
            q128, kk, vv, lo = _win_head_operands(q, k, v, h)
            _, ckk, cvv, _ = _win_head_operands(q, ck, cv, h)
            sink = sink_ref[layer, h]
            for n in range(nb):
                k0 = max(0, (n - 1) * CHUNK)
                k1 = min(L, (n + 2) * CHUNK)
                qn = q128[n * CHUNK:(n + 1) * CHUNK]
                s_loc = _dot_nt(qn, kk[k0:k1]) * scale
                qi = n * CHUNK + lax.broadcasted_iota(jnp.int32, s_loc.shape, 0)
                kj = k0 + lax.broadcasted_iota(jnp.int32, s_loc.shape, 1)
                s_loc = jnp.where(jnp.abs(qi - kj) <= WINDOW, s_loc, NEG)
                s_ctx = _dot_nt(qn, ckk) * scale
                m = jnp.maximum(jnp.maximum(jnp.max(s_loc, -1, keepdims=True),
                                            jnp.max(s_ctx, -1, keepdims=True)), sink)
                p_loc = jnp.exp(s_loc - m)
                p_ctx = jnp.exp(s_ctx - m)
                den = (jnp.sum(p_loc, -1, keepdims=True) + jnp.sum(p_ctx, -1, keepdims=True)
                       + jnp.exp(sink - m))
                o = (_dot(p_loc, vv[k0:k1]) + _dot(p_ctx, cvv)) / den
                o = jnp.where(lo_mask if lo else hi_mask, o, 0.0)
                acc_blocks[n] = o if acc_blocks[n] is None else acc_blocks[n] + o
        cols.append(jnp.concatenate(acc_blocks, 0))
    o_ref[...] = jnp.concatenate(cols, 1).astype(o_ref.dtype)


def _win_lat(z_main, prev_out, sink, cache_k, cache_v, layer):
    c, su, sd, _ = _rope_tables(DEC_SEQ, WIN_HD, LANE)
    rb0 = T_CTX // DEC_SEQ
    tab = pl.BlockSpec((DEC_SEQ, LANE), lambda b: (0, 0))
    cache = pl.BlockSpec((None, None, PAST_LEN, LANE), lambda b: (b, layer, 0, 0))
    return pl.pallas_call(
        functools.partial(_win_lat_kernel, layer=layer),
        out_shape=jax.ShapeDtypeStruct((T_ALL, WIN_HEADS * WIN_HD), BF16),
        grid=(DEC_BATCH,),
        in_specs=[
            pl.BlockSpec(memory_space=pltpu.SMEM),
            pl.BlockSpec(memory_space=pl.ANY),
            pl.BlockSpec((DEC_SEQ, 256), lambda b: (rb0 + b, COL_WQ // 256)),
            pl.BlockSpec((DEC_SEQ, 256), lambda b: (rb0 + b, COL_WK // 256)),
            cache, cache, tab, tab, tab,
        ],
        out_specs=pl.BlockSpec((DEC_SEQ, 256), lambda b: (rb0 + b, 0)),
        input_output_aliases={1: 0},
        compiler_params=_params("parallel"),
        name="win_lat",
    )(sink, prev_out, z_main, z_main, cache_k, cache_v,
      jnp.asarray(c), jnp.asarray(su), jnp.asarray(sd))


def _ret_kernel(*refs, L, layer, ctx):
    if ctx:
        (df_ref, db_ref, q_ref, k_ref, v0_ref, v1_ref, g0_ref, g1_ref,
         o_ref, sf_out, sb_out, s_ref, cross_ref) = refs
        seqs = CTX_SEQS_PER_STEP
    else:
        (df_ref, db_ref, prev_ref, q_ref, k_ref, v0_ref, v1_ref, g0_ref, g1_ref, s0f_ref, s0b_ref,
         o_ref, s_ref, cross_ref) = refs
        seqs = 1
    C = CHUNK
    nc = L // C
    H = RET_HEADS
    qw = H * RET_DK
    vw = H * RET_DV

    def lane_table(width, per_head, fn):
        pos = lax.broadcasted_iota(jnp.int32, (C, per_head), 0).astype(F32)
        return jnp.concatenate([fn(h, pos) for h in range(H)], 1)

    def log_gamma(ref, h):
        d = jnp.full((1, 1), ref[layer, h], F32)
        return jnp.log(jax.nn.sigmoid(d))

    lgf = [log_gamma(df_ref, h) for h in range(H)]
    lgb = [log_gamma(db_ref, h) for h in range(H)]

    def tables(lg, reverse):
        if reverse:
            dq = lane_table(vw, RET_DV, lambda h, pos: jnp.exp((C - pos) * lg[h]))
            dk = lane_table(qw, RET_DK, lambda h, pos: jnp.exp(pos * lg[h]))
        else:
            dq = lane_table(vw, RET_DV, lambda h, pos: jnp.exp((pos + 1.0) * lg[h]))
            dk = lane_table(qw, RET_DK, lambda h, pos: jnp.exp((C - 1.0 - pos) * lg[h]))
        dc = jnp.concatenate([jnp.broadcast_to(jnp.exp(C * lg[h]), (1, RET_DV)) for h in range(H)], 1)
        return dq, dk, dc

    tab_f = tables(lgf, False)
    tab_b = tables(lgb, True)
    ii = lax.broadcasted_iota(jnp.int32, (C, C), 0)
    jj = lax.broadcasted_iota(jnp.int32, (C, C), 1)
    diff = (ii - jj).astype(F32)
    dmats = [jnp.where(diff >= 0, jnp.exp(jnp.maximum(diff, 0.0) * lgf[h]), 0.0)
             + jnp.where(diff <= 0, jnp.exp(jnp.maximum(-diff, 0.0) * lgb[h]), 0.0) for h in range(H)]
    lane_q = lax.broadcasted_iota(jnp.int32, (1, qw), 1) // RET_DK

    srow = lax.broadcasted_iota(jnp.int32, (qw, vw), 0) // RET_DK
    scol = lax.broadcasted_iota(jnp.int32, (qw, vw), 1) // RET_DV
    diag = srow == scol

    for g in range(seqs):
        base = g * L
        rows_all = slice(base, base + L)
        q_all = q_ref[rows_all, :].astype(F32)
        k_all = k_ref[rows_all, :].astype(F32) * (RET_DK ** -0.5)
        v_all = jnp.concatenate([v0_ref[rows_all, :], v1_ref[rows_all, :]], 1).astype(F32)
        g_all = jnp.concatenate([g0_ref[rows_all, :], g1_ref[rows_all, :]], 1).astype(F32)

        def scan(tabs, reverse, s0_ref, s_out):
            dq, dk, dc = tabs
            s_ref[g] = jnp.zeros((qw, vw), F32)
            if s0_ref is not None:
                for h in range(H):
                    s_ref[g, h * RET_DK:(h + 1) * RET_DK, h * RET_DV:(h + 1) * RET_DV] = s0_ref[h]
            order = range(nc - 1, -1, -1) if reverse else range(nc)
            for ci in order:
                sl = slice(ci * C, (ci + 1) * C)
                qc, kc, vc = q_all[sl], k_all[sl], v_all[sl]
                st = s_ref[g]
                cross = _dot(qc, st) * dq
                if reverse:
                    cross_ref[g, sl, :] = cross_ref[g, sl, :] + cross
                else:
                    cross_ref[g, sl, :] = cross
                upd = jnp.where(diag, _dot_tn(kc * dk, vc), 0.0)
                s_ref[g] = st * dc + upd
            if s_out is not None:
                for h in range(H):
                    s_out[g, h] = s_ref[g, h * RET_DK:(h + 1) * RET_DK, h * RET_DV:(h + 1) * RET_DV]

        scan(tab_f, False, None if ctx else s0f_ref, sf_out if ctx else None)
        scan(tab_b, True, None if ctx else s0b_ref, sb_out if ctx else None)

        for h in range(H):
            hv = slice(h * RET_DV, (h + 1) * RET_DV)
            for ci in range(nc):
                sl = slice(ci * C, (ci + 1) * C)
                qh = jnp.where(lane_q == h, q_all[sl], 0.0)
                att = _dot_nt(qh, k_all[sl]) * dmats[h]
                o = _dot(att, v_all[sl, hv]) + cross_ref[g, sl, hv]
                gt = g_all[sl, hv]
                o_ref[base + ci * C:base + (ci + 1) * C, hv] = (
                    (gt * jax.nn.sigmoid(gt)) * _layer_norm(o)).astype(o_ref.dtype)


def _retention(z_main, prev_out, dec_f, dec_b, s0f, s0b, layer, ctx):
    L = SEQ if ctx else DEC_SEQ
    n_seq = BATCH if ctx else DEC_BATCH
    seqs = CTX_SEQS_PER_STEP if ctx else 1
    rb0 = 0 if ctx else T_CTX // DEC_SEQ

    def zcol(col):
        return pl.BlockSpec((seqs * L, 256), lambda b: (rb0 + b, col // 256))

    smem = pl.BlockSpec(memory_space=pltpu.SMEM)
    z_specs = [zcol(COL_RQ), zcol(COL_RK), zcol(COL_RV), zcol(COL_RV + 256),
               zcol(COL_RG), zcol(COL_RG + 256)]
    y_shape = jax.ShapeDtypeStruct((T_ALL, RET_HEADS * RET_DV), BF16)
    y_spec = pl.BlockSpec((seqs * L, RET_HEADS * RET_DV), lambda b: (rb0 + b, 0))
    scratch = [pltpu.VMEM((seqs, RET_HEADS * RET_DK, RET_HEADS * RET_DV), F32),
               pltpu.VMEM((seqs, L, RET_HEADS * RET_DV), F32)]
    kern = functools.partial(_ret_kernel, L=L, layer=layer, ctx=ctx)
    if ctx:
        st_shape = jax.ShapeDtypeStruct((BATCH, RET_HEADS, RET_DK, RET_DV), F32)
        st_spec = pl.BlockSpec((seqs, RET_HEADS, RET_DK, RET_DV), lambda b: (b, 0, 0, 0))
        return pl.pallas_call(
            kern,
            out_shape=(y_shape, st_shape, st_shape),
            grid=(n_seq // seqs,),
            in_specs=[smem, smem] + z_specs,
            out_specs=(y_spec, st_spec, st_spec),
            scratch_shapes=scratch,
            compiler_params=_params("parallel"),
            name="ret_ctx",
        )(dec_f, dec_b, *([z_main] * 6))
    s0_spec = pl.BlockSpec((None, None, RET_HEADS, RET_DK, RET_DV), lambda b: (b, layer, 0, 0, 0))
    return pl.pallas_call(
        kern,
        out_shape=y_shape,
        grid=(n_seq,),
        in_specs=[smem, smem, pl.BlockSpec(memory_space=pl.ANY)] + z_specs + [s0_spec, s0_spec],
        out_specs=y_spec,
        scratch_shapes=scratch,
        input_output_aliases={2: 0},
        compiler_params=_params("parallel"),
        name="ret_lat",
    )(dec_f, dec_b, prev_out, *([z_main] * 6), s0f, s0b)


def _rms_norm(x, g):
    return x * lax.rsqrt(jnp.mean(x * x, -1, keepdims=True) + RMS_EPS) * g


def _mla_attend(qn, qr, kn, kr, vv, o_ref, row0):
    scale = (MLA_NOPE + MLA_ROPE) ** -0.5
    lane_n = lax.broadcasted_iota(jnp.int32, (1, MLA_HEADS * MLA_NOPE), 1) // MLA_NOPE
    lane_r = lax.broadcasted_iota(jnp.int32, (1, LANE), 1)
    kr32 = jnp.where(lane_r < MLA_ROPE, kr, 0.0)
    acc = None
    for h in range(MLA_HEADS):
        qnh = jnp.where(lane_n == h, qn, 0.0)
        qrh = qr if h == 0 else pltpu.roll(qr, LANE - h * MLA_ROPE, axis=1)
        qrh = jnp.where(lane_r < MLA_ROPE, qrh, 0.0)
        s = (_dot_nt(qnh, kn) + _dot_nt(qrh, kr32)) * scale
        m = jnp.max(s, -1, keepdims=True)
        p = jnp.exp(s - m)
        den = jnp.sum(p, -1, keepdims=True)
        o = jnp.where(lane_n == h, _dot(p, vv) / den, 0.0)
        acc = o if acc is None else acc + o
    o_ref[row0:row0 + acc.shape[0], :] = acc.astype(o_ref.dtype)


def _mla_ctx_kernel(cq_ref, ckv_ref, kr_ref, qg_ref, kg_ref, wqn_ref, wqr_ref, wk_ref, wv_ref,
                    o_ref, ckvn_ref):
    cqn = _rms_norm(cq_ref[...].astype(F32), qg_ref[...])
    qn = _dot(cqn, wqn_ref[...])
    qr = _dot(cqn, wqr_ref[...])
    ckvn = _rms_norm(ckv_ref[...].astype(F32), kg_ref[...])
    ckvn_ref[...] = ckvn
    kn = _dot(ckvn, wk_ref[...])
    vv = _dot(ckvn, wv_ref[...])
    for g in range(CTX_SEQS_PER_STEP):
        sl = slice(g * SEQ, (g + 1) * SEQ)
        _mla_attend(qn[sl], qr[sl], kn[sl], kr_ref[sl, :].astype(F32), vv[sl], o_ref, g * SEQ)


def _mla_weight_specs(layer):
    return [
        pl.BlockSpec((None, 1, MLA_Q_LORA), lambda b: (layer, 0, 0)),
        pl.BlockSpec((None, 1, MLA_KV_LORA), lambda b: (layer, 0, 0)),
        pl.BlockSpec((None, MLA_Q_LORA, MLA_HEADS * MLA_NOPE), lambda b: (layer, 0, 0)),
        pl.BlockSpec((None, MLA_Q_LORA, MLA_HEADS * MLA_ROPE), lambda b: (layer, 0, 0)),
        pl.BlockSpec((None, MLA_KV_LORA, MLA_HEADS * MLA_NOPE), lambda b: (layer, 0, 0)),
        pl.BlockSpec((None, MLA_KV_LORA, MLA_HEADS * MLA_V), lambda b: (layer, 0, 0)),
    ]


def _mla_ctx(z_main, weights, layer):
    return pl.pallas_call(
        _mla_ctx_kernel,
        out_shape=(jax.ShapeDtypeStruct((T_ALL, MLA_HEADS * MLA_V), BF16),
                   jax.ShapeDtypeStruct((T_CTX, MLA_KV_LORA), F32)),
        grid=(BATCH // CTX_SEQS_PER_STEP,),
        in_specs=[
            pl.BlockSpec((CTX_SEQS_PER_STEP * SEQ, 256), lambda b: (b, COL_CQ // 256)),
            pl.BlockSpec((CTX_SEQS_PER_STEP * SEQ, LANE), lambda b: (b, COL_CKV // LANE)),
            pl.BlockSpec((CTX_SEQS_PER_STEP * SEQ, LANE), lambda b: (b, COL_KROPE // LANE)),
        ] + _mla_weight_specs(layer),
        out_specs=(pl.BlockSpec((CTX_SEQS_PER_STEP * SEQ, 256), lambda b: (b, 0)),
                   pl.BlockSpec((CTX_SEQS_PER_STEP * SEQ, MLA_KV_LORA), lambda b: (b, 0))),
        compiler_params=_params("parallel"),
        name="mla_ctx",
    )(z_main, z_main, z_main, *weights)


def _mla_lat_kernel(prev_ref, cq_ref, ckv_ref, kr_ref, cckv_ref, ckr_ref, c_ref, su_ref, sd_ref,
                    qg_ref, kg_ref, wqn_ref, wqr_ref, wk_ref, wv_ref, o_ref):
    del prev_ref
    half = MLA_ROPE // 2
    c, su, sd = c_ref[...], su_ref[...], sd_ref[...]
    cqn = _rms_norm(cq_ref[...].astype(F32), qg_ref[...])
    qn = _dot(cqn, wqn_ref[...])
    qr = _rope128(_dot(cqn, wqr_ref[...]), c, su, sd, half)
    ckvn = _rms_norm(ckv_ref[...].astype(F32), kg_ref[...])
    ckv_all = jnp.concatenate([ckvn, cckv_ref[...]], 0)
    kn = _dot(ckv_all, wk_ref[...])
    vv = _dot(ckv_all, wv_ref[...])
    kr = jnp.concatenate([_rope128(kr_ref[...].astype(F32), c, su, sd, half), ckr_ref[...]], 0)
    for n in range(DEC_SEQ // 256):
        rows = slice(n * 256, (n + 1) * 256)
        _mla_attend(qn[rows], qr[rows], kn, kr, vv, o_ref, n * 256)


def _mla_lat(z_main, prev_out, cache_ckv, cache_kr_pad, weights, layer):
    c, su, sd, _ = _rope_tables(DEC_SEQ, MLA_ROPE, LANE)
    rb0 = T_CTX // DEC_SEQ
    tab = pl.BlockSpec((DEC_SEQ, LANE), lambda b: (0, 0))
    cache = pl.BlockSpec((None, None, PAST_LEN, LANE), lambda b: (b, layer, 0, 0))
    return pl.pallas_call(
        _mla_lat_kernel,
        out_shape=jax.ShapeDtypeStruct((T_ALL, MLA_HEADS * MLA_V), BF16),
        grid=(DEC_BATCH,),
        in_specs=[
            pl.BlockSpec(memory_space=pl.ANY),
            pl.BlockSpec((DEC_SEQ, 256), lambda b: (rb0 + b, COL_CQ // 256)),
            pl.BlockSpec((DEC_SEQ, LANE), lambda b: (rb0 + b, COL_CKV // LANE)),
            pl.BlockSpec((DEC_SEQ, LANE), lambda b: (rb0 + b, COL_KROPE // LANE)),
            cache, cache, tab, tab, tab,
        ] + _mla_weight_specs(layer),
        out_specs=pl.BlockSpec((DEC_SEQ, 256), lambda b: (rb0 + b, 0)),
        input_output_aliases={0: 0},
        compiler_params=_params("parallel"),
        name="mla_lat",
    )(prev_out, z_main, z_main, z_main, cache_ckv, cache_kr_pad,
      jnp.asarray(c), jnp.asarray(su), jnp.asarray(sd), *weights)


def _route(logits_t, rb):
    scores = jax.nn.sigmoid(logits_t)
    biased = scores + rb
    sc = [scores[e:e + 1, :] for e in range(N_EXPERTS)]
    bi = [biased[e:e + 1, :] for e in range(N_EXPERTS)]
    epg = EXPERTS_PER_GROUP
    gsum = []
    for g in range(N_GROUPS):
        v = bi[g * epg:(g + 1) * epg]
        best = None
        for a in range(epg):
            for b in range(a + 1, epg):
                pair = v[a] + v[b]
                best = pair if best is None else jnp.maximum(best, pair)
        gsum.append(best)
    combine = []
    sel = []
    for g in range(N_GROUPS):
        is_best = None
        for g2 in range(N_GROUPS):
            if g2 == g:
                continue
            c = gsum[g] > gsum[g2] if g2 < g else gsum[g] >= gsum[g2]
            is_best = c if is_best is None else jnp.logical_and(is_best, c)
        for a in range(epg):
            e = g * epg + a
            rank = jnp.zeros_like(bi[e])
            for b in range(epg):
                if b == a:
                    continue
                e2 = g * epg + b
                ahead = bi[e2] >= bi[e] if b < a else bi[e2] > bi[e]
                rank = rank + jnp.where(ahead, 1.0, 0.0)
            sel.append(jnp.logical_and(is_best, rank < 2.0))
    wsum = None
    for e in range(N_EXPERTS):
        w = jnp.where(sel[e], sc[e], 0.0)
        wsum = w if wsum is None else wsum + w
    for e in range(N_EXPERTS):
        combine.append(jnp.where(sel[e], ROUTE_SCALE * sc[e] / wsum, 0.0))
    return jnp.concatenate(combine, 0)


def _merge_kernel(ya_ref, yb_ref, yc_ref, yd_ref, gt_ref, xc_ref, xl_ref, m_ref,
                  wa_ref, wb_ref, wc_ref, wd_ref, wo_ref, g_ref, b_ref, rwh_ref, rwl_ref, rb_ref,
                  x1_ref, h2_ref, cmb_ref, *, ctx_tiles, sub_rows):
    D = D_MODEL
    g1 = m_ref[:, 2 * D:3 * D]
    s2 = m_ref[:, 3 * D:4 * D]
    sc2 = m_ref[:, 4 * D:5 * D]
    is_ctx = pl.program_id(0) < ctx_tiles
    branches = ((ya_ref, wa_ref), (yb_ref, wb_ref), (yc_ref, wc_ref), (yd_ref, wd_ref))
    for r0 in range(0, x1_ref.shape[0], sub_rows):
        rows = slice(r0, r0 + sub_rows)
        merged = None
        for i, (y_ref, w) in enumerate(branches):
            t = gt_ref[rows, i * D:(i + 1) * D].astype(F32) * jnp.dot(
                y_ref[rows, :], w[...], preferred_element_type=F32)
            merged = t if merged is None else merged + t
        out1 = jnp.dot(merged.astype(BF16), wo_ref[...], preferred_element_type=F32)
        x = jnp.where(is_ctx, xc_ref[rows, :], xl_ref[rows, :])
        x1 = _layer_norm(ALPHA * x + g1 * out1) * g_ref[...] + b_ref[...]
        x1_ref[rows, :] = x1
        h2 = _layer_norm(x1) * (1.0 + sc2) + s2
        h2_hi = h2.astype(BF16)
        h2_ref[rows, :] = h2_hi
        h2_lo = (h2 - h2_hi.astype(F32)).astype(BF16)
        logits = (jnp.dot(h2_hi, rwh_ref[...], preferred_element_type=F32)
                  + (jnp.dot(h2_lo, rwh_ref[...], preferred_element_type=F32)
                     + jnp.dot(h2_hi, rwl_ref[...], preferred_element_type=F32)))
        cmb_ref[:, rows] = _route(logits.T[0:N_EXPERTS], rb_ref[...])


def _merge(ya, yb, yc, yd, gates, x_ctx, x_lat, mods_l, w_br, w_out_bf, ln1_g, ln1_b,
           router_w_parts, router_b, layer):
    tm = 512
    row = _mod_row(tm)
    D = D_MODEL
    ctx_tiles = T_CTX // tm

    def tile(w):
        return pl.BlockSpec((tm, w), lambda i: (i, 0))

    def weight(k, n):
        return pl.BlockSpec((None, k, n), lambda i: (layer, 0, 0))

    return pl.pallas_call(
        functools.partial(_merge_kernel, ctx_tiles=ctx_tiles, sub_rows=256),
        out_shape=(jax.ShapeDtypeStruct((T_ALL, D), F32),
                   jax.ShapeDtypeStruct((T_ALL, D), BF16),
                   jax.ShapeDtypeStruct((N_EXPERTS, T_ALL), F32)),
        grid=(T_ALL // tm,),
        in_specs=[
            tile(256), tile(256), tile(512), tile(256), tile(4 * D),
            pl.BlockSpec((tm, D), lambda i: (jnp.minimum(i, ctx_tiles - 1), 0)),
            pl.BlockSpec((tm, D), lambda i: (jnp.maximum(i - ctx_tiles, 0), 0)),
            pl.BlockSpec((None, 1, 6 * D), lambda i: (row(i), 0, 0)),
            weight(256, D), weight(256, D), weight(512, D), weight(256, D), weight(D, D),
            weight(1, D), weight(1, D),
            pl.BlockSpec((D, LANE), lambda i: (0, 0)),
            pl.BlockSpec((D, LANE), lambda i: (0, 0)),
            pl.BlockSpec((N_EXPERTS, 1), lambda i: (0, 0)),
        ],
        out_specs=(tile(D), tile(D), pl.BlockSpec((N_EXPERTS, tm), lambda i: (0, i))),
        compiler_params=_params("parallel"),
        name="merge",
    )(ya, yb, yc, yd, gates, x_ctx, x_lat, mods_l, *w_br, w_out_bf,
      ln1_g.reshape(DEPTH, 1, D), ln1_b.reshape(DEPTH, 1, D), *router_w_parts,
      router_b.reshape(N_EXPERTS, 1))


MOE_EXPERTS_PER_STEP = 4


def _moe_kernel(*refs, next_h):
    if next_h:
        (h_ref, c_ref, x1_ref, m_ref, wg_ref, wu_ref, wd_ref, g_ref, b_ref, mn_ref,
         o_ref, hn_ref) = refs[-12:]
    else:
        h_ref, c_ref, x1_ref, m_ref, wg_ref, wu_ref, wd_ref, g_ref, b_ref, o_ref = refs
    eg = pl.program_id(1)
    h = h_ref[...]
    cmb = c_ref[...]
    lane = lax.broadcasted_iota(jnp.int32, cmb.shape, 1)
    hid = []
    for k in range(MOE_EXPERTS_PER_STEP):
        gate = jnp.dot(h, wg_ref[k].astype(BF16), preferred_element_type=F32)
        up = jnp.dot(h, wu_ref[k].astype(BF16), preferred_element_type=F32)
        e = eg * MOE_EXPERTS_PER_STEP + k
        ce = jnp.sum(jnp.where(lane == e, cmb, 0.0), -1, keepdims=True)
        hid.append(((gate * jax.nn.sigmoid(gate)) * up * ce).astype(BF16))
    wd = wd_ref[...].reshape(MOE_EXPERTS_PER_STEP * D_EXPERT, D_MODEL).astype(BF16)
    part = jnp.dot(jnp.concatenate(hid, 1), wd, preferred_element_type=F32)

    @pl.when(eg == 0)
    def _():
        o_ref[...] = part

    @pl.when(eg > 0)
    def _():
        o_ref[...] += part

    @pl.when(eg == N_EXPERTS // MOE_EXPERTS_PER_STEP - 1)
    def _():
        g2 = m_ref[:, 5 * D_MODEL:6 * D_MODEL]
        y = _layer_norm(ALPHA * x1_ref[...] + g2 * o_ref[...])
        y = y * g_ref[...] + b_ref[...]
        o_ref[...] = y
        if next_h:
            s1 = mn_ref[:, 0:D_MODEL]
            sc1 = mn_ref[:, D_MODEL:2 * D_MODEL]
            hn_ref[...] = (_layer_norm(y) * (1.0 + sc1) + s1).astype(hn_ref.dtype)


def _moe(h2, combine, x1, mods_l, w_gate, w_up, w_down, ln2_g, ln2_b, layer, row0, n_rows,
         mods_next=None, prev_h=None):
    tm = 1024
    row = _mod_row(tm)
    D = D_MODEL
    t0 = row0 // tm
    eps = MOE_EXPERTS_PER_STEP
    next_h = mods_next is not None
    mod_spec = pl.BlockSpec((None, 1, 6 * D), lambda i, e: (row(t0 + i), 0, 0))
    in_specs = [
        pl.BlockSpec((tm, D), lambda i, e: (t0 + i, 0)),
        pl.BlockSpec((tm, N_EXPERTS), lambda i, e: (t0 + i, 0)),
        pl.BlockSpec((tm, D), lambda i, e: (t0 + i, 0), pipeline_mode=pl.Buffered(1)),
        mod_spec,
        pl.BlockSpec((None, eps, D, D_EXPERT), lambda i, e: (layer, e, 0, 0)),
        pl.BlockSpec((None, eps, D, D_EXPERT), lambda i, e: (layer, e, 0, 0)),
        pl.BlockSpec((None, eps, D_EXPERT, D), lambda i, e: (layer, e, 0, 0)),
        pl.BlockSpec((None, 1, D), lambda i, e: (layer, 0, 0)),
        pl.BlockSpec((None, 1, D), lambda i, e: (layer, 0, 0)),
    ]
    args = [h2, combine, x1, mods_l, w_gate, w_up, w_down,
            ln2_g.reshape(DEPTH, 1, D), ln2_b.reshape(DEPTH, 1, D)]
    out_shape = jax.ShapeDtypeStruct((n_rows, D), F32)
    out_specs = pl.BlockSpec((tm, D), lambda i, e: (i, 0))
    aliases = {}
    if next_h:
        in_specs.append(mod_spec)
        args.append(mods_next)
        out_shape = (out_shape, jax.ShapeDtypeStruct((T_ALL, D), BF16))
        out_specs = (out_specs, pl.BlockSpec((tm, D), lambda i, e: (t0 + i, 0)))
        if prev_h is not None:
            in_specs = [pl.BlockSpec(memory_space=pl.ANY)] + in_specs
            args = [prev_h] + args
            aliases = {0: 1}
    return pl.pallas_call(
        functools.partial(_moe_kernel, next_h=next_h),
        out_shape=out_shape,
        grid=(n_rows // tm, N_EXPERTS // eps),
        in_specs=in_specs,
        out_specs=out_specs,
        input_output_aliases=aliases,
        compiler_params=_params("parallel", "arbitrary"),
        name="moe",
    )(*args)


def kernel(x_prompt, x_sample, cache_win_k, cache_win_v, cache_mla_ckv, cache_mla_krope,
           state_ret_fwd, state_ret_bwd, c, c_ctx, w_ada, b_ada, w_in,
           hy_conv_w, hy_conv_b, hy_w1, hy_b1, hy_w2, hy_b2, hy_w3, hy_bias,
           win_sink, ret_decay_fwd, ret_decay_bwd, mla_q_norm, mla_kv_norm, mla_w_uq, mla_w_ukv,
           w_br_a, w_br_b, w_br_c, w_br_d, w_out, ln1_g, ln1_b, ln2_g, ln2_b,
           router_w, router_b, moe_w_gate, moe_w_up, moe_w_down):
    D = D_MODEL
    x_ctx = x_prompt.reshape(T_CTX, D)
    x_lat = x_sample.reshape(T_LAT, D)

    cvec = jnp.zeros((ADA_ROWS, D), F32).at[0].set(c_ctx).at[1:1 + DEC_BATCH].set(c)
    mods = _ada_mods(cvec, w_ada, b_ada)[:, :1 + DEC_BATCH].reshape(DEPTH, 1 + DEC_BATCH, 1, 6 * D)

    w_in_t = jnp.swapaxes(w_in, 1, 2).reshape(DEPTH * IN_COLS, D)
    cache_k = cache_win_k.reshape(DEC_BATCH, DEPTH, PAST_LEN, WIN_KV_HEADS * WIN_HD)
    cache_v = cache_win_v.reshape(DEC_BATCH, DEPTH, PAST_LEN, WIN_KV_HEADS * WIN_HD)
    cache_kr = jnp.pad(cache_mla_krope, ((0, 0), (0, 0), (0, 0), (0, LANE - MLA_ROPE)))

    uq = mla_w_uq.reshape(DEPTH, MLA_Q_LORA, MLA_HEADS, MLA_NOPE + MLA_ROPE)
    ukv = mla_w_ukv.reshape(DEPTH, MLA_KV_LORA, MLA_HEADS, MLA_NOPE + MLA_V)
    mla_weights = (
        mla_q_norm.reshape(DEPTH, 1, MLA_Q_LORA),
        mla_kv_norm.reshape(DEPTH, 1, MLA_KV_LORA),
        uq[..., :MLA_NOPE].reshape(DEPTH, MLA_Q_LORA, MLA_HEADS * MLA_NOPE),
        uq[..., MLA_NOPE:].reshape(DEPTH, MLA_Q_LORA, MLA_HEADS * MLA_ROPE),
        ukv[..., :MLA_NOPE].reshape(DEPTH, MLA_KV_LORA, MLA_HEADS * MLA_NOPE),
        ukv[..., MLA_NOPE:].reshape(DEPTH, MLA_KV_LORA, MLA_HEADS * MLA_V),
    )

    hy_w1p = jnp.pad(hy_w1, ((0, 0), (0, LANE - HY_EMB), (0, 0)))
    dft = {}
    for L in (SEQ, DEC_SEQ):
        fwd, inv = _dft_tables(L)
        dft[L] = (jnp.asarray(fwd).astype(BF16), jnp.asarray(inv).astype(BF16))
    router_w_pad = jnp.pad(router_w, ((0, 0), (0, LANE - N_EXPERTS)))
    router_w_hi = router_w_pad.astype(BF16)
    router_w_parts = (router_w_hi, (router_w_pad - router_w_hi.astype(F32)).astype(BF16))
    w_br = tuple(w.astype(BF16) for w in (w_br_a, w_br_b, w_br_c, w_br_d))
    w_out_bf = w_out.astype(BF16)

    new_k, new_v, new_ckv, new_kr, new_sf, new_sb = [], [], [], [], [], []
    for l in range(DEPTH):
        mods_l = mods[l]
        if l == 0:
            h = _ln_mod(x_ctx, None, mods_l, 0)
            h = _ln_mod(x_lat, h, mods_l, T_CTX)
        z = _in_proj(h, w_in_t, l, 0, Z_MAIN, Z_MAIN // 2, BF16, gate=False)
        gates = _in_proj(h, w_in_t, l, COL_GATE, 4 * D, D, BF16, gate=True)

        ya = None
        for L, n_seq, rb0 in ((SEQ, BATCH, 0), (DEC_SEQ, DEC_BATCH, T_CTX // DEC_SEQ)):
            fwd, inv = dft[L]
            kre, kim = _hy_filters(L, hy_w1p[l], hy_b1[l][None], hy_w2[l], hy_b2[l][None], hy_w3[l], fwd)
            ya = _hyena(z, ya, l, L, n_seq, rb0, hy_conv_w, hy_conv_b, hy_bias, kre, kim, fwd, inv)

        yb = _win_ctx(z, win_sink, l)
        yb = _win_lat(z, yb, win_sink, cache_k, cache_v, l)

        yc, sf, sb = _retention(z, None, ret_decay_fwd, ret_decay_bwd, None, None, l, ctx=True)
        yc = _retention(z, yc, ret_decay_fwd, ret_decay_bwd, state_ret_fwd, state_ret_bwd, l, ctx=False)

        yd, ckvn = _mla_ctx(z, mla_weights, l)
        yd = _mla_lat(z, yd, cache_mla_ckv, cache_kr, mla_weights, l)

        x1, h2, combine_t = _merge(ya, yb, yc, yd, gates, x_ctx, x_lat, mods_l, w_br, w_out_bf,
                                   ln1_g, ln1_b, router_w_parts, router_b, l)
        moe_args = (h2, combine_t.T, x1, mods_l, moe_w_gate, moe_w_up, moe_w_down, ln2_g, ln2_b, l)
        if l + 1 < DEPTH:
            x_ctx, h = _moe(*moe_args, 0, T_CTX, mods_next=mods[l + 1])
            x_lat, h = _moe(*moe_args, T_CTX, T_LAT, mods_next=mods[l + 1], prev_h=h)
        else:
            x_ctx = _moe(*moe_args, 0, T_CTX)
            x_lat = _moe(*moe_args, T_CTX, T_LAT)

        def ctx_cols(col, width):
            return z[:T_CTX, col:col + width].astype(F32)

        new_k.append(ctx_cols(COL_WK, 128).reshape(BATCH, SEQ, WIN_KV_HEADS, WIN_HD))
        new_v.append(ctx_cols(COL_WV, 128).reshape(BATCH, SEQ, WIN_KV_HEADS, WIN_HD))
        new_ckv.append(ckvn.reshape(BATCH, SEQ, MLA_KV_LORA))
        new_kr.append(ctx_cols(COL_KROPE, MLA_ROPE).reshape(BATCH, SEQ, MLA_ROPE))
        new_sf.append(sf)
        new_sb.append(sb)

    y_prompt = x_ctx.reshape(BATCH, SEQ, D)
    y_sample = x_lat.reshape(DEC_BATCH, DEC_SEQ, D)
    return (y_prompt, y_sample, jnp.stack(new_k, 1), jnp.stack(new_v, 1), jnp.stack(new_ckv, 1),
            jnp.stack(new_kr, 1), jnp.stack(new_sf, 1), jnp.stack(new_sb, 1))
```

```python
import functools
import math

import numpy as np
import jax
import jax.numpy as jnp
from jax import lax
from jax.experimental import pallas as pl
from jax.experimental.pallas import tpu as pltpu

F32 = jnp.float32
BF16 = jnp.bfloat16

D_MODEL = 1024
BATCH = 16
SEQ = 256
DEPTH = 2
DEC_BATCH = 2
DEC_SEQ = 1024
PAST_LEN = 256
GRID_W = 64
CHUNK = 128
ROPE_BASE = 10000.0
NEG = -1e30
LN_EPS = 1e-5
RMS_EPS = 1e-6

HY_W = 256
HY_BANDS = 16
HY_EMB = 1 + 2 * HY_BANDS
HY_FFN = 64
HY_FAST_DECAY = 0.3
HY_SLOW_DECAY = 1.5
HY_TARGET = 1e-2

WIN_HEADS = 4
WIN_KV_HEADS = 2
WIN_HD = 64
WINDOW = 128

RET_HEADS = 4
RET_DK = 64
RET_DV = 128

MLA_HEADS = 4
MLA_Q_LORA = 256
MLA_KV_LORA = 128
MLA_NOPE = 64
MLA_ROPE = 32
MLA_V = 64

N_EXPERTS = 16
N_GROUPS = 4
EXPERTS_PER_GROUP = N_EXPERTS // N_GROUPS
D_EXPERT = 256
ROUTE_SCALE = 2.5

ALPHA = (2.0 * DEPTH) ** 0.25

T_CTX = BATCH * SEQ
T_LAT = DEC_BATCH * DEC_SEQ
T_ALL = T_CTX + T_LAT

COL_HY = 0
COL_WQ = 768
COL_WK = 1024
COL_WV = 1152
COL_RQ = 1280
COL_RK = 1536
COL_RV = 1792
COL_RG = 2304
COL_CQ = 2816
COL_CKV = 3072
COL_KROPE = 3200
COL_GATE = 3232
IN_COLS = COL_GATE + 4 * D_MODEL
Z_MAIN = 3328

LANE = 128
CTX_SEQS_PER_STEP = 4
VMEM_LIMIT = 56 * 1024 * 1024


def _params(*sem):
    return pltpu.CompilerParams(dimension_semantics=sem, vmem_limit_bytes=VMEM_LIMIT)


def _dot(a, b):
    return jnp.dot(a.astype(BF16), b.astype(BF16), preferred_element_type=F32)


def _dot_split(a, b):
    a_hi = a.astype(BF16)
    a_lo = (a - a_hi.astype(F32)).astype(BF16)
    b_hi = b.astype(BF16)
    b_lo = (b - b_hi.astype(F32)).astype(BF16)

    def mm(x, y):
        return jnp.dot(x, y, preferred_element_type=F32)

    return mm(a_hi, b_hi) + (mm(a_lo, b_hi) + mm(a_hi, b_lo))


def _dot_nt(a, b):
    return lax.dot_general(a.astype(BF16), b.astype(BF16), (((1,), (1,)), ((), ())),
                           preferred_element_type=F32)


def _dot_tn(a, b):
    return lax.dot_general(a.astype(BF16), b.astype(BF16), (((0,), (0,)), ((), ())),
                           preferred_element_type=F32)


def _layer_norm(x):
    mu = jnp.mean(x, -1, keepdims=True)
    xc = x - mu
    var = jnp.mean(xc * xc, -1, keepdims=True)
    return xc * lax.rsqrt(var + LN_EPS)


def _mod_row(tile_rows):
    def row(i):
        start = i * tile_rows
        return jnp.where(start < T_CTX, 0, 1 + (start - T_CTX) // DEC_SEQ)
    return row


@functools.lru_cache(maxsize=None)
def _dft_tables(L):
    f = np.arange(L, dtype=np.int64)[:, None]
    s = np.arange(L, dtype=np.int64)[None, :]
    ang = np.pi * ((f * s) % (2 * L)).astype(np.float64) / L
    cos = np.cos(ang)
    sin = np.sin(ang)
    alt = np.where(np.arange(L) % 2 == 0, 1.0, -1.0)
    fwd_im = -sin
    fwd_im[0, :] = alt
    fwd = np.concatenate([cos, fwd_im], 0)
    inv_re = cos.T / L
    inv_re[:, 0] = 1.0 / (2 * L)
    inv_im = -sin.T / L
    inv_im[:, 0] = alt / (2 * L)
    inv = np.concatenate([inv_re, inv_im], 1)
    return fwd.astype(np.float32), inv.astype(np.float32)


@functools.lru_cache(maxsize=None)
def _hyena_embedding(L):
    t01 = np.linspace(0.0, 1.0, L, dtype=np.float64)[:, None]
    bands = np.linspace(1e-4, HY_BANDS - 1, HY_BANDS, dtype=np.float64)
    ang = (2.0 * math.pi / L) * np.arange(L, dtype=np.float64)[:, None] * bands[None, :]
    z = np.concatenate([t01, np.cos(ang), -np.sin(ang)], -1)
    zp = np.zeros((L, LANE), np.float64)
    zp[:, :HY_EMB] = z
    deltas = np.abs(np.linspace(math.log(HY_TARGET) / HY_SLOW_DECAY,
                                math.log(HY_TARGET) / HY_FAST_DECAY, HY_W, dtype=np.float64))
    return zp.astype(np.float32), deltas[None, :].astype(np.float32)


@functools.lru_cache(maxsize=None)
def _rope_tables(L, rot_dim, width):
    rows = L // GRID_W
    n_freq = rot_dim // 4
    half = rot_dim // 2
    inv = ROPE_BASE ** (-np.arange(n_freq, dtype=np.float64) / n_freq)
    pos = np.arange(L)
    row = (pos // GRID_W).astype(np.float64)
    col = (pos % GRID_W).astype(np.float64)
    ang = np.concatenate([row[:, None] * inv, col[:, None] * inv], -1)
    cos, sin = np.cos(ang), np.sin(ang)
    zero = np.zeros_like(sin)
    c = np.tile(np.concatenate([cos, cos], -1), (1, width // rot_dim))
    s_up = np.tile(np.concatenate([-sin, zero], -1), (1, width // rot_dim))
    s_dn = np.tile(np.concatenate([zero, sin], -1), (1, width // rot_dim))
    return c.astype(np.float32), s_up.astype(np.float32), s_dn.astype(np.float32), half


def _rope128(x, c, s_up, s_dn, half):
    up = pltpu.roll(x, LANE - half, axis=1)
    dn = pltpu.roll(x, half, axis=1)
    return x * c + up * s_up + dn * s_dn


def _ada_kernel(c_ref, w_ref, b_ref, o_ref):
    cv = c_ref[...]
    s = cv * jax.nn.sigmoid(cv)
    s_hi = s.astype(BF16)
    s_lo = (s - s_hi.astype(F32)).astype(BF16)
    w = w_ref[...]
    w_hi = w.astype(BF16)
    w_lo = (w - w_hi.astype(F32)).astype(BF16)
    rows = s.shape[0]
    both = jnp.dot(jnp.concatenate([s_hi, s_lo], 0), w_hi, preferred_element_type=F32)
    o_ref[...] = (both[0:rows] + (both[rows:2 * rows] + jnp.dot(s_hi, w_lo, preferred_element_type=F32))
                  + b_ref[...])


ADA_ROWS = 16


def _ada_mods(cvec, w_ada, b_ada):
    tn = 1536
    n = 6 * D_MODEL
    return pl.pallas_call(
        _ada_kernel,
        out_shape=jax.ShapeDtypeStruct((DEPTH, ADA_ROWS, n), F32),
        grid=(DEPTH, n // tn),
        in_specs=[
            pl.BlockSpec((ADA_ROWS, D_MODEL), lambda l, j: (0, 0)),
            pl.BlockSpec((None, D_MODEL, tn), lambda l, j: (l, 0, j)),
            pl.BlockSpec((None, 1, tn), lambda l, j: (l, 0, j)),
        ],
        out_specs=pl.BlockSpec((None, ADA_ROWS, tn), lambda l, j: (l, 0, j)),
        compiler_params=_params("parallel", "parallel"),
        name="ada_mods",
    )(cvec, w_ada, b_ada.reshape(DEPTH, 1, n))


def _lnmod_kernel(*refs):
    x_ref, m_ref, h_ref = refs[-3:]
    y = _layer_norm(x_ref[...])
    s1 = m_ref[:, 0:D_MODEL]
    sc1 = m_ref[:, D_MODEL:2 * D_MODEL]
    h_ref[...] = (y * (1.0 + sc1) + s1).astype(h_ref.dtype)


def _ln_mod(x_group, prev_out, mods_l, row0):
    tm = 512
    row = _mod_row(tm)
    tile0 = row0 // tm
    in_specs = [
        pl.BlockSpec((tm, D_MODEL), lambda i: (i, 0)),
        pl.BlockSpec((None, 1, 6 * D_MODEL), lambda i: (row(tile0 + i), 0, 0)),
    ]
    args = [x_group, mods_l]
    if prev_out is not None:
        in_specs = [pl.BlockSpec(memory_space=pl.ANY)] + in_specs
        args = [prev_out] + args
    return pl.pallas_call(
        _lnmod_kernel,
        out_shape=jax.ShapeDtypeStruct((T_ALL, D_MODEL), BF16),
        grid=(x_group.shape[0] // tm,),
        in_specs=in_specs,
        out_specs=pl.BlockSpec((tm, D_MODEL), lambda i: (tile0 + i, 0)),
        input_output_aliases={} if prev_out is None else {0: 0},
        compiler_params=_params("parallel"),
        name="ln_mod",
    )(*args)


def _proj_kernel(h_ref, w_ref, o_ref, wb_ref, *, gate):
    @pl.when(pl.program_id(1) == 0)
    def _():
        wb_ref[...] = w_ref[...].T.astype(BF16)

    if not gate:
        o_ref[...] = jnp.dot(h_ref[...], wb_ref[...], preferred_element_type=F32).astype(o_ref.dtype)
        return
    sub = 2 * LANE
    for c0 in range(0, o_ref.shape[1], sub):
        r = jnp.dot(h_ref[...], wb_ref[:, c0:c0 + sub], preferred_element_type=F32)
        o_ref[:, c0:c0 + sub] = (0.5 * jnp.tanh(0.5 * r) + 0.5).astype(o_ref.dtype)


def _in_proj(h, w_t, layer, col0, n_cols, tn, out_dtype, gate):
    tm = 2048
    return pl.pallas_call(
        functools.partial(_proj_kernel, gate=gate),
        out_shape=jax.ShapeDtypeStruct((T_ALL, n_cols), out_dtype),
        grid=(n_cols // tn, T_ALL // tm),
        in_specs=[
            pl.BlockSpec((tm, D_MODEL), lambda j, i: (i, 0)),
            pl.BlockSpec((pl.Element(tn), pl.Element(D_MODEL)),
                         lambda j, i: (pl.multiple_of(layer * IN_COLS + col0 + j * tn, 8), 0)),
        ],
        out_specs=pl.BlockSpec((tm, tn), lambda j, i: (i, j)),
        scratch_shapes=[pltpu.VMEM((D_MODEL, tn), BF16)],
        compiler_params=_params("parallel", "arbitrary"),
        name="gate_proj" if gate else "in_proj",
    )(h, w_t)


def _hy_filter_kernel(z_ref, dl_ref, w1_ref, b1_ref, w2_ref, b2_ref, w3_ref, fwd_ref,
                      kre_ref, kim_ref, *, L):
    z = z_ref[...]
    a = jnp.sin(_dot_split(z, w1_ref[...]) + b1_ref[...])
    a = jnp.sin(_dot_split(a, w2_ref[...]) + b2_ref[...])
    h = _dot_split(a, w3_ref[...])
    decay = jnp.exp(-z[:, 0:1] * dl_ref[...])
    not_first = lax.broadcasted_iota(jnp.int32, (L, HY_W), 0) > 0
    sums, diffs = [], []
    for o in range(2):
        fw = h[:, (2 * o) * HY_W:(2 * o + 1) * HY_W] * decay
        bw = jnp.where(not_first, h[:, (2 * o + 1) * HY_W:(2 * o + 2) * HY_W] * decay, 0.0)
        sums.append(fw + bw)
        diffs.append(fw - bw)
    p = _dot(fwd_ref[...], jnp.concatenate(sums, 1))
    q = _dot(fwd_ref[L:2 * L, :], jnp.concatenate(diffs, 1))
    kre_ref[...] = p[0:L]
    first = lax.broadcasted_iota(jnp.int32, (L, 2 * HY_W), 0) == 0
    kim_ref[...] = jnp.where(first, p[L:L + 1], q)


def _hy_filters(L, w1p, b1, w2, b2, w3, fwd):
    zemb, deltas = _hyena_embedding(L)
    out = jax.ShapeDtypeStruct((L, 2 * HY_W), F32)
    return pl.pallas_call(
        functools.partial(_hy_filter_kernel, L=L),
        out_shape=(out, out),
        compiler_params=pltpu.CompilerParams(vmem_limit_bytes=VMEM_LIMIT),
        name=f"hy_filters_{L}",
    )(jnp.asarray(zemb), jnp.asarray(deltas), w1p, b1, w2, b2, w3, fwd)


def _hyena_kernel(*refs, L, aliased, seqs):
    if aliased:
        refs = refs[1:]
    hy_ref, cw_ref, cb_ref, bias_ref, kre_ref, kim_ref, fwd_ref, inv_ref, o_ref = refs
    first = lax.broadcasted_iota(jnp.int32, (L, HY_W), 0) == 0

    def long_conv(u, o):
        uf = _dot(fwd_ref[...], u)
        ure, uim = uf[0:L], uf[L:2 * L]
        kre = kre_ref[:, o * HY_W:(o + 1) * HY_W]
        kim = kim_ref[:, o * HY_W:(o + 1) * HY_W]
        yre = jnp.where(first, ure * kre, ure * kre - uim * kim)
        yim = jnp.where(first, uim * kim, ure * kim + uim * kre)
        y = _dot(inv_ref[...], jnp.concatenate([yre, yim], 0))
        return y + u * bias_ref[o:o + 1, :]

    for g in range(seqs):
        sl = slice(g * L, (g + 1) * L)
        x = hy_ref[sl, :].astype(F32)
        rows = lax.broadcasted_iota(jnp.int32, x.shape, 0)
        prev = jnp.where(rows == 0, 0.0, pltpu.roll(x, 1, axis=0))
        nxt = jnp.where(rows == L - 1, 0.0, pltpu.roll(x, L - 1, axis=0))
        z = prev * cw_ref[0:1, :] + x * cw_ref[1:2, :] + nxt * cw_ref[2:3, :] + cb_ref[...]
        v, x1, x2 = z[:, 0:HY_W], z[:, HY_W:2 * HY_W], z[:, 2 * HY_W:3 * HY_W]
        u = x1 * long_conv(v, 0)
        o_ref[sl, :] = (x2 * long_conv(u, 1)).astype(o_ref.dtype)


def _hyena(z_main, prev_out, layer, L, n_seq, row_block0, conv_w, conv_b, bias, kre, kim, fwd, inv):
    aliased = prev_out is not None
    seqs = CTX_SEQS_PER_STEP if L == SEQ else 1
    row_block0 //= seqs
    in_specs = [
        pl.BlockSpec((seqs * L, 3 * HY_W), lambda b: (row_block0 + b, 0)),
        pl.BlockSpec((None, 3, 3 * HY_W), lambda b: (layer, 0, 0)),
        pl.BlockSpec((None, 1, 3 * HY_W), lambda b: (layer, 0, 0)),
        pl.BlockSpec((None, 2, HY_W), lambda b: (layer, 0, 0)),
        pl.BlockSpec((L, 2 * HY_W), lambda b: (0, 0)),
        pl.BlockSpec((L, 2 * HY_W), lambda b: (0, 0)),
        pl.BlockSpec((2 * L, L), lambda b: (0, 0)),
        pl.BlockSpec((L, 2 * L), lambda b: (0, 0)),
    ]
    args = [z_main, conv_w, conv_b.reshape(DEPTH, 1, 3 * HY_W), bias, kre, kim, fwd, inv]
    if aliased:
        in_specs = [pl.BlockSpec(memory_space=pl.ANY)] + in_specs
        args = [prev_out] + args
    return pl.pallas_call(
        functools.partial(_hyena_kernel, L=L, aliased=aliased, seqs=seqs),
        out_shape=jax.ShapeDtypeStruct((T_ALL, HY_W), BF16),
        grid=(n_seq // seqs,),
        in_specs=in_specs,
        out_specs=pl.BlockSpec((seqs * L, HY_W), lambda b: (row_block0 + b, 0)),
        input_output_aliases={0: 0} if aliased else {},
        compiler_params=_params("parallel"),
        name=f"hyena_{L}",
    )(*args)


def _win_masks():
    lane = lax.broadcasted_iota(jnp.int32, (1, LANE), 1)
    return lane < WIN_HD, lane >= WIN_HD


def _win_head_operands(q, k, v, h):
    lo_mask, hi_mask = _win_masks()
    col = h // 2
    lo = h % 2 == 0
    q128 = jnp.where(lo_mask if lo else hi_mask, q[:, col * LANE:(col + 1) * LANE], 0.0)
    swap = h in (1, 2)
    if swap:
        k = pltpu.roll(k, WIN_HD, axis=1)
        v = pltpu.roll(v, WIN_HD, axis=1)
    return q128, k, v, lo


def _win_ctx_kernel(sink_ref, q_ref, kv_ref, o_ref, *, layer):
    lo_mask, hi_mask = _win_masks()
    scale = WIN_HD ** -0.5
    for g in range(CTX_SEQS_PER_STEP):
        sl = slice(g * SEQ, (g + 1) * SEQ)
        q = q_ref[sl, :].astype(F32)
        k = kv_ref[sl, 0:LANE].astype(F32)
        v = kv_ref[sl, LANE:2 * LANE].astype(F32)
        cols = []
        for col in range(2):
            acc = None
            for h in (2 * col, 2 * col + 1):
                q128, kk, vv, lo = _win_head_operands(q, k, v, h)
                s = _dot_nt(q128, kk) * scale
                sink = sink_ref[layer, h]
                m = jnp.maximum(jnp.max(s, -1, keepdims=True), sink)
                p = jnp.exp(s - m)
                den = jnp.sum(p, -1, keepdims=True) + jnp.exp(sink - m)
                o = _dot(p, vv) / den
                o = jnp.where(lo_mask if lo else hi_mask, o, 0.0)
                acc = o if acc is None else acc + o
            cols.append(acc)
        o_ref[sl, :] = jnp.concatenate(cols, 1).astype(o_ref.dtype)


def _win_ctx(z_main, sink, layer):
    return pl.pallas_call(
        functools.partial(_win_ctx_kernel, layer=layer),
        out_shape=jax.ShapeDtypeStruct((T_ALL, WIN_HEADS * WIN_HD), BF16),
        grid=(BATCH // CTX_SEQS_PER_STEP,),
        in_specs=[
            pl.BlockSpec(memory_space=pltpu.SMEM),
            pl.BlockSpec((CTX_SEQS_PER_STEP * SEQ, 256), lambda b: (b, COL_WQ // 256)),
            pl.BlockSpec((CTX_SEQS_PER_STEP * SEQ, 256), lambda b: (b, COL_WK // 256)),
        ],
        out_specs=pl.BlockSpec((CTX_SEQS_PER_STEP * SEQ, 256), lambda b: (b, 0)),
        compiler_params=_params("parallel"),
        name="win_ctx",
    )(sink, z_main, z_main)


def _win_lat_kernel(sink_ref, prev_ref, q_ref, kv_ref, ck_ref, cv_ref, c_ref, su_ref, sd_ref,
                    o_ref, *, layer):
    del prev_ref
    L = DEC_SEQ
    half = WIN_HD // 2
    c, su, sd = c_ref[...], su_ref[...], sd_ref[...]
    q = jnp.concatenate(
        [_rope128(q_ref[:, i * LANE:(i + 1) * LANE].astype(F32), c, su, sd, half) for i in range(2)], 1)
    k = _rope128(kv_ref[:, 0:LANE].astype(F32), c, su, sd, half)
    v = kv_ref[:, LANE:2 * LANE].astype(F32)
    ck = ck_ref[...]
    cv = cv_ref[...]
    lo_mask, hi_mask = _win_masks()
    scale = WIN_HD ** -0.5
    nb = L // CHUNK
    cols = []
    for col in range(2):
        acc_blocks = [None] * nb
        for h in (2 * col, 2 * col + 1):
            q128, kk, vv, lo = _win_head_operands(q, k, v, h)
            _, ckk, cvv, _ = _win_head_operands(q, ck, cv, h)
            sink = sink_ref[layer, h]
            for n in range(nb):
                k0 = max(0, (n - 1) * CHUNK)
                k1 = min(L, (n + 2) * CHUNK)
                qn = q128[n * CHUNK:(n + 1) * CHUNK]
                s_loc = _dot_nt(qn, kk[k0:k1]) * scale
                qi = n * CHUNK + lax.broadcasted_iota(jnp.int32, s_loc.shape, 0)
                kj = k0 + lax.broadcasted_iota(jnp.int32, s_loc.shape, 1)
                s_loc = jnp.where(jnp.abs(qi - kj) <= WINDOW, s_loc, NEG)
                s_ctx = _dot_nt(qn, ckk) * scale
                m = jnp.maximum(jnp.maximum(jnp.max(s_loc, -1, keepdims=True),
                                            jnp.max(s_ctx, -1, keepdims=True)), sink)
                p_loc = jnp.exp(s_loc - m)
                p_ctx = jnp.exp(s_ctx - m)
                den = (jnp.sum(p_loc, -1, keepdims=True) + jnp.sum(p_ctx, -1, keepdims=True)
                       + jnp.exp(sink - m))
                o = (_dot(p_loc, vv[k0:k1]) + _dot(p_ctx, cvv)) / den
                o = jnp.where(lo_mask if lo else hi_mask, o, 0.0)
                acc_blocks[n] = o if acc_blocks[n] is None else acc_blocks[n] + o
        cols.append(jnp.concatenate(acc_blocks, 0))
    o_ref[...] = jnp.concatenate(cols, 1).astype(o_ref.dtype)


def _win_lat(z_main, prev_out, sink, cache_k, cache_v, layer):
    c, su, sd, _ = _rope_tables(DEC_SEQ, WIN_HD, LANE)
    rb0 = T_CTX // DEC_SEQ
    tab = pl.BlockSpec((DEC_SEQ, LANE), lambda b: (0, 0))
    cache = pl.BlockSpec((None, None, PAST_LEN, LANE), lambda b: (b, layer, 0, 0))
    return pl.pallas_call(
        functools.partial(_win_lat_kernel, layer=layer),
        out_shape=jax.ShapeDtypeStruct((T_ALL, WIN_HEADS * WIN_HD), BF16),
        grid=(DEC_BATCH,),
        in_specs=[
            pl.BlockSpec(memory_space=pltpu.SMEM),
            pl.BlockSpec(memory_space=pl.ANY),
            pl.BlockSpec((DEC_SEQ, 256), lambda b: (rb0 + b, COL_WQ // 256)),
            pl.BlockSpec((DEC_SEQ, 256), lambda b: (rb0 + b, COL_WK // 256)),
            cache, cache, tab, tab, tab,
        ],
        out_specs=pl.BlockSpec((DEC_SEQ, 256), lambda b: (rb0 + b, 0)),
        input_output_aliases={1: 0},
        compiler_params=_params("parallel"),
        name="win_lat",
    )(sink, prev_out, z_main, z_main, cache_k, cache_v,
      jnp.asarray(c), jnp.asarray(su), jnp.asarray(sd))


def _ret_kernel(*refs, L, layer, ctx):
    if ctx:
        (df_ref, db_ref, q_ref, k_ref, v0_ref, v1_ref, g0_ref, g1_ref,
         o_ref, sf_out, sb_out, s_ref, cross_ref) = refs
        seqs = CTX_SEQS_PER_STEP
    else:
        (df_ref, db_ref, prev_ref, q_ref, k_ref, v0_ref, v1_ref, g0_ref, g1_ref, s0f_ref, s0b_ref,
         o_ref, s_ref, cross_ref) = refs
        seqs = 1
    C = CHUNK
    nc = L // C
    H = RET_HEADS
    qw = H * RET_DK
    vw = H * RET_DV

    def lane_table(width, per_head, fn):
        pos = lax.broadcasted_iota(jnp.int32, (C, per_head), 0).astype(F32)
        return jnp.concatenate([fn(h, pos) for h in range(H)], 1)

    def log_gamma(ref, h):
        d = jnp.full((1, 1), ref[layer, h], F32)
        return jnp.log(jax.nn.sigmoid(d))

    lgf = [log_gamma(df_ref, h) for h in range(H)]
    lgb = [log_gamma(db_ref, h) for h in range(H)]

    def tables(lg, reverse):
        if reverse:
            dq = lane_table(vw, RET_DV, lambda h, pos: jnp.exp((C - pos) * lg[h]))
            dk = lane_table(qw, RET_DK, lambda h, pos: jnp.exp(pos * lg[h]))
        else:
            dq = lane_table(vw, RET_DV, lambda h, pos: jnp.exp((pos + 1.0) * lg[h]))
            dk = lane_table(qw, RET_DK, lambda h, pos: jnp.exp((C - 1.0 - pos) * lg[h]))
        dc = jnp.concatenate([jnp.broadcast_to(jnp.exp(C * lg[h]), (1, RET_DV)) for h in range(H)], 1)
        return dq, dk, dc

    tab_f = tables(lgf, False)
    tab_b = tables(lgb, True)
    ii = lax.broadcasted_iota(jnp.int32, (C, C), 0)
    jj = lax.broadcasted_iota(jnp.int32, (C, C), 1)
    diff = (ii - jj).astype(F32)
    dmats = [jnp.where(diff >= 0, jnp.exp(jnp.maximum(diff, 0.0) * lgf[h]), 0.0)
             + jnp.where(diff <= 0, jnp.exp(jnp.maximum(-diff, 0.0) * lgb[h]), 0.0) for h in range(H)]
    lane_q = lax.broadcasted_iota(jnp.int32, (1, qw), 1) // RET_DK

    srow = lax.broadcasted_iota(jnp.int32, (qw, vw), 0) // RET_DK
    scol = lax.broadcasted_iota(jnp.int32, (qw, vw), 1) // RET_DV
    diag = srow == scol

    for g in range(seqs):
        base = g * L
        rows_all = slice(base, base + L)
        q_all = q_ref[rows_all, :].astype(F32)
        k_all = k_ref[rows_all, :].astype(F32) * (RET_DK ** -0.5)
        v_all = jnp.concatenate([v0_ref[rows_all, :], v1_ref[rows_all, :]], 1).astype(F32)
        g_all = jnp.concatenate([g0_ref[rows_all, :], g1_ref[rows_all, :]], 1).astype(F32)

        def scan(tabs, reverse, s0_ref, s_out):
            dq, dk, dc = tabs
            s_ref[g] = jnp.zeros((qw, vw), F32)
            if s0_ref is not None:
                for h in range(H):
                    s_ref[g, h * RET_DK:(h + 1) * RET_DK, h * RET_DV:(h + 1) * RET_DV] = s0_ref[h]
            order = range(nc - 1, -1, -1) if reverse else range(nc)
            for ci in order:
                sl = slice(ci * C, (ci + 1) * C)
                qc, kc, vc = q_all[sl], k_all[sl], v_all[sl]
                st = s_ref[g]
                cross = _dot(qc, st) * dq
                if reverse:
                    cross_ref[g, sl, :] = cross_ref[g, sl, :] + cross
                else:
                    cross_ref[g, sl, :] = cross
                upd = jnp.where(diag, _dot_tn(kc * dk, vc), 0.0)
                s_ref[g] = st * dc + upd
            if s_out is not None:
                for h in range(H):
                    s_out[g, h] = s_ref[g, h * RET_DK:(h + 1) * RET_DK, h * RET_DV:(h + 1) * RET_DV]

        scan(tab_f, False, None if ctx else s0f_ref, sf_out if ctx else None)
        scan(tab_b, True, None if ctx else s0b_ref, sb_out if ctx else None)

        for h in range(H):
            hv = slice(h * RET_DV, (h + 1) * RET_DV)
            for ci in range(nc):
                sl = slice(ci * C, (ci + 1) * C)
                qh = jnp.where(lane_q == h, q_all[sl], 0.0)
                att = _dot_nt(qh, k_all[sl]) * dmats[h]
                o = _dot(att, v_all[sl, hv]) + cross_ref[g, sl, hv]
                gt = g_all[sl, hv]
                o_ref[base + ci * C:base + (ci + 1) * C, hv] = (
                    (gt * jax.nn.sigmoid(gt)) * _layer_norm(o)).astype(o_ref.dtype)


def _retention(z_main, prev_out, dec_f, dec_b, s0f, s0b, layer, ctx):
    L = SEQ if ctx else DEC_SEQ
    n_seq = BATCH if ctx else DEC_BATCH
    seqs = CTX_SEQS_PER_STEP if ctx else 1
    rb0 = 0 if ctx else T_CTX // DEC_SEQ

    def zcol(col):
        return pl.BlockSpec((seqs * L, 256), lambda b: (rb0 + b, col // 256))

    smem = pl.BlockSpec(memory_space=pltpu.SMEM)
    z_specs = [zcol(COL_RQ), zcol(COL_RK), zcol(COL_RV), zcol(COL_RV + 256),
               zcol(COL_RG), zcol(COL_RG + 256)]
    y_shape = jax.ShapeDtypeStruct((T_ALL, RET_HEADS * RET_DV), BF16)
    y_spec = pl.BlockSpec((seqs * L, RET_HEADS * RET_DV), lambda b: (rb0 + b, 0))
    scratch = [pltpu.VMEM((seqs, RET_HEADS * RET_DK, RET_HEADS * RET_DV), F32),
               pltpu.VMEM((seqs, L, RET_HEADS * RET_DV), F32)]
    kern = functools.partial(_ret_kernel, L=L, layer=layer, ctx=ctx)
    if ctx:
        st_shape = jax.ShapeDtypeStruct((BATCH, RET_HEADS, RET_DK, RET_DV), F32)
        st_spec = pl.BlockSpec((seqs, RET_HEADS, RET_DK, RET_DV), lambda b: (b, 0, 0, 0))
        return pl.pallas_call(
            kern,
            out_shape=(y_shape, st_shape, st_shape),
            grid=(n_seq // seqs,),
            in_specs=[smem, smem] + z_specs,
            out_specs=(y_spec, st_spec, st_spec),
            scratch_shapes=scratch,
            compiler_params=_params("parallel"),
            name="ret_ctx",
        )(dec_f, dec_b, *([z_main] * 6))
    s0_spec = pl.BlockSpec((None, None, RET_HEADS, RET_DK, RET_DV), lambda b: (b, layer, 0, 0, 0))
    return pl.pallas_call(
        kern,
        out_shape=y_shape,
        grid=(n_seq,),
        in_specs=[smem, smem, pl.BlockSpec(memory_space=pl.ANY)] + z_specs + [s0_spec, s0_spec],
        out_specs=y_spec,
        scratch_shapes=scratch,
        input_output_aliases={2: 0},
        compiler_params=_params("parallel"),
        name="ret_lat",
    )(dec_f, dec_b, prev_out, *([z_main] * 6), s0f, s0b)


def _rms_norm(x, g):
    return x * lax.rsqrt(jnp.mean(x * x, -1, keepdims=True) + RMS_EPS) * g


def _mla_attend(qn, qr, kn, kr, vv, o_ref, row0):
    scale = (MLA_NOPE + MLA_ROPE) ** -0.5
    lane_n = lax.broadcasted_iota(jnp.int32, (1, MLA_HEADS * MLA_NOPE), 1) // MLA_NOPE
    lane_r = lax.broadcasted_iota(jnp.int32, (1, LANE), 1)
    kr32 = jnp.where(lane_r < MLA_ROPE, kr, 0.0)
    acc = None
    for h in range(MLA_HEADS):
        qnh = jnp.where(lane_n == h, qn, 0.0)
        qrh = qr if h == 0 else pltpu.roll(qr, LANE - h * MLA_ROPE, axis=1)
        qrh = jnp.where(lane_r < MLA_ROPE, qrh, 0.0)
        s = (_dot_nt(qnh, kn) + _dot_nt(qrh, kr32)) * scale
        m = jnp.max(s, -1, keepdims=True)
        p = jnp.exp(s - m)
        den = jnp.sum(p, -1, keepdims=True)
        o = jnp.where(lane_n == h, _dot(p, vv) / den, 0.0)
        acc = o if acc is None else acc + o
    o_ref[row0:row0 + acc.shape[0], :] = acc.astype(o_ref.dtype)


def _mla_ctx_kernel(cq_ref, ckv_ref, kr_ref, qg_ref, kg_ref, wqn_ref, wqr_ref, wk_ref, wv_ref,
                    o_ref, ckvn_ref):
    cqn = _rms_norm(cq_ref[...].astype(F32), qg_ref[...])
    qn = _dot(cqn, wqn_ref[...])
    qr = _dot(cqn, wqr_ref[...])
    ckvn = _rms_norm(ckv_ref[...].astype(F32), kg_ref[...])
    ckvn_ref[...] = ckvn
    kn = _dot(ckvn, wk_ref[...])
    vv = _dot(ckvn, wv_ref[...])
    for g in range(CTX_SEQS_PER_STEP):
        sl = slice(g * SEQ, (g + 1) * SEQ)
        _mla_attend(qn[sl], qr[sl], kn[sl], kr_ref[sl, :].astype(F32), vv[sl], o_ref, g * SEQ)


def _mla_weight_specs(layer):
    return [
        pl.BlockSpec((None, 1, MLA_Q_LORA), lambda b: (layer, 0, 0)),
        pl.BlockSpec((None, 1, MLA_KV_LORA), lambda b: (layer, 0, 0)),
        pl.BlockSpec((None, MLA_Q_LORA, MLA_HEADS * MLA_NOPE), lambda b: (layer, 0, 0)),
        pl.BlockSpec((None, MLA_Q_LORA, MLA_HEADS * MLA_ROPE), lambda b: (layer, 0, 0)),
        pl.BlockSpec((None, MLA_KV_LORA, MLA_HEADS * MLA_NOPE), lambda b: (layer, 0, 0)),
        pl.BlockSpec((None, MLA_KV_LORA, MLA_HEADS * MLA_V), lambda b: (layer, 0, 0)),
    ]


def _mla_ctx(z_main, weights, layer):
    return pl.pallas_call(
        _mla_ctx_kernel,
        out_shape=(jax.ShapeDtypeStruct((T_ALL, MLA_HEADS * MLA_V), BF16),
                   jax.ShapeDtypeStruct((T_CTX, MLA_KV_LORA), F32)),
        grid=(BATCH // CTX_SEQS_PER_STEP,),
        in_specs=[
            pl.BlockSpec((CTX_SEQS_PER_STEP * SEQ, 256), lambda b: (b, COL_CQ // 256)),
            pl.BlockSpec((CTX_SEQS_PER_STEP * SEQ, LANE), lambda b: (b, COL_CKV // LANE)),
            pl.BlockSpec((CTX_SEQS_PER_STEP * SEQ, LANE), lambda b: (b, COL_KROPE // LANE)),
        ] + _mla_weight_specs(layer),
        out_specs=(pl.BlockSpec((CTX_SEQS_PER_STEP * SEQ, 256), lambda b: (b, 0)),
                   pl.BlockSpec((CTX_SEQS_PER_STEP * SEQ, MLA_KV_LORA), lambda b: (b, 0))),
        compiler_params=_params("parallel"),
        name="mla_ctx",
    )(z_main, z_main, z_main, *weights)


def _mla_lat_kernel(prev_ref, cq_ref, ckv_ref, kr_ref, cckv_ref, ckr_ref, c_ref, su_ref, sd_ref,
                    qg_ref, kg_ref, wqn_ref, wqr_ref, wk_ref, wv_ref, o_ref):
    del prev_ref
    half = MLA_ROPE // 2
    c, su, sd = c_ref[...], su_ref[...], sd_ref[...]
    cqn = _rms_norm(cq_ref[...].astype(F32), qg_ref[...])
    qn = _dot(cqn, wqn_ref[...])
    qr = _rope128(_dot(cqn, wqr_ref[...]), c, su, sd, half)
    ckvn = _rms_norm(ckv_ref[...].astype(F32), kg_ref[...])
    ckv_all = jnp.concatenate([ckvn, cckv_ref[...]], 0)
    kn = _dot(ckv_all, wk_ref[...])
    vv = _dot(ckv_all, wv_ref[...])
    kr = jnp.concatenate([_rope128(kr_ref[...].astype(F32), c, su, sd, half), ckr_ref[...]], 0)
    for n in range(DEC_SEQ // 256):
        rows = slice(n * 256, (n + 1) * 256)
        _mla_attend(qn[rows], qr[rows], kn, kr, vv, o_ref, n * 256)


def _mla_lat(z_main, prev_out, cache_ckv, cache_kr_pad, weights, layer):
    c, su, sd, _ = _rope_tables(DEC_SEQ, MLA_ROPE, LANE)
    rb0 = T_CTX // DEC_SEQ
    tab = pl.BlockSpec((DEC_SEQ, LANE), lambda b: (0, 0))
    cache = pl.BlockSpec((None, None, PAST_LEN, LANE), lambda b: (b, layer, 0, 0))
    return pl.pallas_call(
        _mla_lat_kernel,
        out_shape=jax.ShapeDtypeStruct((T_ALL, MLA_HEADS * MLA_V), BF16),
        grid=(DEC_BATCH,),
        in_specs=[
            pl.BlockSpec(memory_space=pl.ANY),
            pl.BlockSpec((DEC_SEQ, 256), lambda b: (rb0 + b, COL_CQ // 256)),
            pl.BlockSpec((DEC_SEQ, LANE), lambda b: (rb0 + b, COL_CKV // LANE)),
            pl.BlockSpec((DEC_SEQ, LANE), lambda b: (rb0 + b, COL_KROPE // LANE)),
            cache, cache, tab, tab, tab,
        ] + _mla_weight_specs(layer),
        out_specs=pl.BlockSpec((DEC_SEQ, 256), lambda b: (rb0 + b, 0)),
        input_output_aliases={0: 0},
        compiler_params=_params("parallel"),
        name="mla_lat",
    )(prev_out, z_main, z_main, z_main, cache_ckv, cache_kr_pad,
      jnp.asarray(c), jnp.asarray(su), jnp.asarray(sd), *weights)


def _route(logits_t, rb):
    scores = jax.nn.sigmoid(logits_t)
    biased = scores + rb
    sc = [scores[e:e + 1, :] for e in range(N_EXPERTS)]
    bi = [biased[e:e + 1, :] for e in range(N_EXPERTS)]
    epg = EXPERTS_PER_GROUP
    gsum = []
    for g in range(N_GROUPS):
        v = bi[g * epg:(g + 1) * epg]
        best = None
        for a in range(epg):
            for b in range(a + 1, epg):
                pair = v[a] + v[b]
                best = pair if best is None else jnp.maximum(best, pair)
        gsum.append(best)
    combine = []
    sel = []
    for g in range(N_GROUPS):
        is_best = None
        for g2 in range(N_GROUPS):
            if g2 == g:
                continue
            c = gsum[g] > gsum[g2] if g2 < g else gsum[g] >= gsum[g2]
            is_best = c if is_best is None else jnp.logical_and(is_best, c)
        for a in range(epg):
            e = g * epg + a
            rank = jnp.zeros_like(bi[e])
            for b in range(epg):
                if b == a:
                    continue
                e2 = g * epg + b
                ahead = bi[e2] >= bi[e] if b < a else bi[e2] > bi[e]
                rank = rank + jnp.where(ahead, 1.0, 0.0)
            sel.append(jnp.logical_and(is_best, rank < 2.0))
    wsum = None
    for e in range(N_EXPERTS):
        w = jnp.where(sel[e], sc[e], 0.0)
        wsum = w if wsum is None else wsum + w
    for e in range(N_EXPERTS):
        combine.append(jnp.where(sel[e], ROUTE_SCALE * sc[e] / wsum, 0.0))
    return jnp.concatenate(combine, 0)


def _merge_kernel(ya_ref, yb_ref, yc_ref, yd_ref, gt_ref, xc_ref, xl_ref, m_ref,
                  wa_ref, wb_ref, wc_ref, wd_ref, wo_ref, g_ref, b_ref, rwh_ref, rwl_ref, rb_ref,
                  x1_ref, h2_ref, cmb_ref, *, ctx_tiles, sub_rows):
    D = D_MODEL
    g1 = m_ref[:, 2 * D:3 * D]
    s2 = m_ref[:, 3 * D:4 * D]
    sc2 = m_ref[:, 4 * D:5 * D]
    is_ctx = pl.program_id(0) < ctx_tiles
    branches = ((ya_ref, wa_ref), (yb_ref, wb_ref), (yc_ref, wc_ref), (yd_ref, wd_ref))
    for r0 in range(0, x1_ref.shape[0], sub_rows):
        rows = slice(r0, r0 + sub_rows)
        merged = None
        for i, (y_ref, w) in enumerate(branches):
            t = gt_ref[rows, i * D:(i + 1) * D].astype(F32) * jnp.dot(
                y_ref[rows, :], w[...], preferred_element_type=F32)
            merged = t if merged is None else merged + t
        out1 = jnp.dot(merged.astype(BF16), wo_ref[...], preferred_element_type=F32)
        x = jnp.where(is_ctx, xc_ref[rows, :], xl_ref[rows, :])
        x1 = _layer_norm(ALPHA * x + g1 * out1) * g_ref[...] + b_ref[...]
        x1_ref[rows, :] = x1
        h2 = _layer_norm(x1) * (1.0 + sc2) + s2
        h2_hi = h2.astype(BF16)
        h2_ref[rows, :] = h2_hi
        h2_lo = (h2 - h2_hi.astype(F32)).astype(BF16)
        logits = (jnp.dot(h2_hi, rwh_ref[...], preferred_element_type=F32)
                  + (jnp.dot(h2_lo, rwh_ref[...], preferred_element_type=F32)
                     + jnp.dot(h2_hi, rwl_ref[...], preferred_element_type=F32)))
        cmb_ref[:, rows] = _route(logits.T[0:N_EXPERTS], rb_ref[...])


def _merge(ya, yb, yc, yd, gates, x_ctx, x_lat, mods_l, w_br, w_out_bf, ln1_g, ln1_b,
           router_w_parts, router_b, layer):
    tm = 512
    row = _mod_row(tm)
    D = D_MODEL
    ctx_tiles = T_CTX // tm

    def tile(w):
        return pl.BlockSpec((tm, w), lambda i: (i, 0))

    def weight(k, n):
        return pl.BlockSpec((None, k, n), lambda i: (layer, 0, 0))

    return pl.pallas_call(
        functools.partial(_merge_kernel, ctx_tiles=ctx_tiles, sub_rows=256),
        out_shape=(jax.ShapeDtypeStruct((T_ALL, D), F32),
                   jax.ShapeDtypeStruct((T_ALL, D), BF16),
                   jax.ShapeDtypeStruct((N_EXPERTS, T_ALL), F32)),
        grid=(T_ALL // tm,),
        in_specs=[
            tile(256), tile(256), tile(512), tile(256), tile(4 * D),
            pl.BlockSpec((tm, D), lambda i: (jnp.minimum(i, ctx_tiles - 1), 0)),
            pl.BlockSpec((tm, D), lambda i: (jnp.maximum(i - ctx_tiles, 0), 0)),
            pl.BlockSpec((None, 1, 6 * D), lambda i: (row(i), 0, 0)),
            weight(256, D), weight(256, D), weight(512, D), weight(256, D), weight(D, D),
            weight(1, D), weight(1, D),
            pl.BlockSpec((D, LANE), lambda i: (0, 0)),
            pl.BlockSpec((D, LANE), lambda i: (0, 0)),
            pl.BlockSpec((N_EXPERTS, 1), lambda i: (0, 0)),
        ],
        out_specs=(tile(D), tile(D), pl.BlockSpec((N_EXPERTS, tm), lambda i: (0, i))),
        compiler_params=_params("parallel"),
        name="merge",
    )(ya, yb, yc, yd, gates, x_ctx, x_lat, mods_l, *w_br, w_out_bf,
      ln1_g.reshape(DEPTH, 1, D), ln1_b.reshape(DEPTH, 1, D), *router_w_parts,
      router_b.reshape(N_EXPERTS, 1))


MOE_EXPERTS_PER_STEP = 2


def _moe_kernel(*refs, next_h):
    if next_h:
        (h_ref, c_ref, x1_ref, m_ref, wg_ref, wu_ref, wd_ref, g_ref, b_ref, mn_ref,
         o_ref, hn_ref) = refs[-12:]
    else:
        h_ref, c_ref, x1_ref, m_ref, wg_ref, wu_ref, wd_ref, g_ref, b_ref, o_ref = refs
    eg = pl.program_id(1)
    h = h_ref[...]
    cmb = c_ref[...]
    lane = lax.broadcasted_iota(jnp.int32, cmb.shape, 1)
    hid = []
    for k in range(MOE_EXPERTS_PER_STEP):
        gate = jnp.dot(h, wg_ref[k].astype(BF16), preferred_element_type=F32)
        up = jnp.dot(h, wu_ref[k].astype(BF16), preferred_element_type=F32)
        e = eg * MOE_EXPERTS_PER_STEP + k
        ce = jnp.sum(jnp.where(lane == e, cmb, 0.0), -1, keepdims=True)
        hid.append(((gate * jax.nn.sigmoid(gate)) * up * ce).astype(BF16))
    wd = wd_ref[...].reshape(MOE_EXPERTS_PER_STEP * D_EXPERT, D_MODEL).astype(BF16)
    part = jnp.dot(jnp.concatenate(hid, 1), wd, preferred_element_type=F32)

    @pl.when(eg == 0)
    def _():
        o_ref[...] = part

    @pl.when(eg > 0)
    def _():
        o_ref[...] += part

    @pl.when(eg == N_EXPERTS // MOE_EXPERTS_PER_STEP - 1)
    def _():
        g2 = m_ref[:, 5 * D_MODEL:6 * D_MODEL]
        y = _layer_norm(ALPHA * x1_ref[...] + g2 * o_ref[...])
        y = y * g_ref[...] + b_ref[...]
        o_ref[...] = y
        if next_h:
            s1 = mn_ref[:, 0:D_MODEL]
            sc1 = mn_ref[:, D_MODEL:2 * D_MODEL]
            hn_ref[...] = (_layer_norm(y) * (1.0 + sc1) + s1).astype(hn_ref.dtype)


def _moe(h2, combine, x1, mods_l, w_gate, w_up, w_down, ln2_g, ln2_b, layer, row0, n_rows,
         mods_next=None, prev_h=None):
    tm = 1024
    row = _mod_row(tm)
    D = D_MODEL
    t0 = row0 // tm
    eps = MOE_EXPERTS_PER_STEP
    next_h = mods_next is not None
    mod_spec = pl.BlockSpec((None, 1, 6 * D), lambda i, e: (row(t0 + i), 0, 0))
    in_specs = [
        pl.BlockSpec((tm, D), lambda i, e: (t0 + i, 0)),
        pl.BlockSpec((tm, N_EXPERTS), lambda i, e: (t0 + i, 0)),
        pl.BlockSpec((tm, D), lambda i, e: (t0 + i, 0)),
        mod_spec,
        pl.BlockSpec((None, eps, D, D_EXPERT), lambda i, e: (layer, e, 0, 0)),
        pl.BlockSpec((None, eps, D, D_EXPERT), lambda i, e: (layer, e, 0, 0)),
        pl.BlockSpec((None, eps, D_EXPERT, D), lambda i, e: (layer, e, 0, 0)),
        pl.BlockSpec((None, 1, D), lambda i, e: (layer, 0, 0)),
        pl.BlockSpec((None, 1, D), lambda i, e: (layer, 0, 0)),
    ]
    args = [h2, combine, x1, mods_l, w_gate, w_up, w_down,
            ln2_g.reshape(DEPTH, 1, D), ln2_b.reshape(DEPTH, 1, D)]
    out_shape = jax.ShapeDtypeStruct((n_rows, D), F32)
    out_specs = pl.BlockSpec((tm, D), lambda i, e: (i, 0))
    aliases = {}
    if next_h:
        in_specs.append(mod_spec)
        args.append(mods_next)
        out_shape = (out_shape, jax.ShapeDtypeStruct((T_ALL, D), BF16))
        out_specs = (out_specs, pl.BlockSpec((tm, D), lambda i, e: (t0 + i, 0)))
        if prev_h is not None:
            in_specs = [pl.BlockSpec(memory_space=pl.ANY)] + in_specs
            args = [prev_h] + args
            aliases = {0: 1}
    return pl.pallas_call(
        functools.partial(_moe_kernel, next_h=next_h),
        out_shape=out_shape,
        grid=(n_rows // tm, N_EXPERTS // eps),
        in_specs=in_specs,
        out_specs=out_specs,
        input_output_aliases=aliases,
        compiler_params=_params("parallel", "arbitrary"),
        name="moe",
    )(*args)


def kernel(x_prompt, x_sample, cache_win_k, cache_win_v, cache_mla_ckv, cache_mla_krope,
           state_ret_fwd, state_ret_bwd, c, c_ctx, w_ada, b_ada, w_in,
           hy_conv_w, hy_conv_b, hy_w1, hy_b1, hy_w2, hy_b2, hy_w3, hy_bias,
           win_sink, ret_decay_fwd, ret_decay_bwd, mla_q_norm, mla_kv_norm, mla_w_uq, mla_w_ukv,
           w_br_a, w_br_b, w_br_c, w_br_d, w_out, ln1_g, ln1_b, ln2_g, ln2_b,
           router_w, router_b, moe_w_gate, moe_w_up, moe_w_down):
    D = D_MODEL
    x_ctx = x_prompt.reshape(T_CTX, D)
    x_lat = x_sample.reshape(T_LAT, D)

    cvec = jnp.zeros((ADA_ROWS, D), F32).at[0].set(c_ctx).at[1:1 + DEC_BATCH].set(c)
    mods = _ada_mods(cvec, w_ada, b_ada)[:, :1 + DEC_BATCH].reshape(DEPTH, 1 + DEC_BATCH, 1, 6 * D)

    w_in_t = jnp.swapaxes(w_in, 1, 2).reshape(DEPTH * IN_COLS, D)
    cache_k = cache_win_k.reshape(DEC_BATCH, DEPTH, PAST_LEN, WIN_KV_HEADS * WIN_HD)
    cache_v = cache_win_v.reshape(DEC_BATCH, DEPTH, PAST_LEN, WIN_KV_HEADS * WIN_HD)
    cache_kr = jnp.pad(cache_mla_krope, ((0, 0), (0, 0), (0, 0), (0, LANE - MLA_ROPE)))

    uq = mla_w_uq.reshape(DEPTH, MLA_Q_LORA, MLA_HEADS, MLA_NOPE + MLA_ROPE)
    ukv = mla_w_ukv.reshape(DEPTH, MLA_KV_LORA, MLA_HEADS, MLA_NOPE + MLA_V)
    mla_weights = (
        mla_q_norm.reshape(DEPTH, 1, MLA_Q_LORA),
        mla_kv_norm.reshape(DEPTH, 1, MLA_KV_LORA),
        uq[..., :MLA_NOPE].reshape(DEPTH, MLA_Q_LORA, MLA_HEADS * MLA_NOPE),
        uq[..., MLA_NOPE:].reshape(DEPTH, MLA_Q_LORA, MLA_HEADS * MLA_ROPE),
        ukv[..., :MLA_NOPE].reshape(DEPTH, MLA_KV_LORA, MLA_HEADS * MLA_NOPE),
        ukv[..., MLA_NOPE:].reshape(DEPTH, MLA_KV_LORA, MLA_HEADS * MLA_V),
    )

    hy_w1p = jnp.pad(hy_w1, ((0, 0), (0, LANE - HY_EMB), (0, 0)))
    dft = {}
    for L in (SEQ, DEC_SEQ):
        fwd, inv = _dft_tables(L)
        dft[L] = (jnp.asarray(fwd).astype(BF16), jnp.asarray(inv).astype(BF16))
    router_w_pad = jnp.pad(router_w, ((0, 0), (0, LANE - N_EXPERTS)))
    router_w_hi = router_w_pad.astype(BF16)
    router_w_parts = (router_w_hi, (router_w_pad - router_w_hi.astype(F32)).astype(BF16))
    w_br = tuple(w.astype(BF16) for w in (w_br_a, w_br_b, w_br_c, w_br_d))
    w_out_bf = w_out.astype(BF16)

    new_k, new_v, new_ckv, new_kr, new_sf, new_sb = [], [], [], [], [], []
    for l in range(DEPTH):
        mods_l = mods[l]
        if l == 0:
            h = _ln_mod(x_ctx, None, mods_l, 0)
            h = _ln_mod(x_lat, h, mods_l, T_CTX)
        z = _in_proj(h, w_in_t, l, 0, Z_MAIN, Z_MAIN // 2, BF16, gate=False)
        gates = _in_proj(h, w_in_t, l, COL_GATE, 4 * D, D, BF16, gate=True)

        ya = None
        for L, n_seq, rb0 in ((SEQ, BATCH, 0), (DEC_SEQ, DEC_BATCH, T_CTX // DEC_SEQ)):
            fwd, inv = dft[L]
            kre, kim = _hy_filters(L, hy_w1p[l], hy_b1[l][None], hy_w2[l], hy_b2[l][None], hy_w3[l], fwd)
            ya = _hyena(z, ya, l, L, n_seq, rb0, hy_conv_w, hy_conv_b, hy_bias, kre, kim, fwd, inv)

        yb = _win_ctx(z, win_sink, l)
        yb = _win_lat(z, yb, win_sink, cache_k, cache_v, l)

        yc, sf, sb = _retention(z, None, ret_decay_fwd, ret_decay_bwd, None, None, l, ctx=True)
        yc = _retention(z, yc, ret_decay_fwd, ret_decay_bwd, state_ret_fwd, state_ret_bwd, l, ctx=False)

        yd, ckvn = _mla_ctx(z, mla_weights, l)
        yd = _mla_lat(z, yd, cache_mla_ckv, cache_kr, mla_weights, l)

        x1, h2, combine_t = _merge(ya, yb, yc, yd, gates, x_ctx, x_lat, mods_l, w_br, w_out_bf,
                                   ln1_g, ln1_b, router_w_parts, router_b, l)
        moe_args = (h2, combine_t.T, x1, mods_l, moe_w_gate, moe_w_up, moe_w_down, ln2_g, ln2_b, l)
        if l + 1 < DEPTH:
            x_ctx, h = _moe(*moe_args, 0, T_CTX, mods_next=mods[l + 1])
            x_lat, h = _moe(*moe_args, T_CTX, T_LAT, mods_next=mods[l + 1], prev_h=h)
        else:
            x_ctx = _moe(*moe_args, 0, T_CTX)
            x_lat = _moe(*moe_args, T_CTX, T_LAT)

        def ctx_cols(col, width):
            return z[:T_CTX, col:col + width].astype(F32)

        new_k.append(ctx_cols(COL_WK, 128).reshape(BATCH, SEQ, WIN_KV_HEADS, WIN_HD))
        new_v.append(ctx_cols(COL_WV, 128).reshape(BATCH, SEQ, WIN_KV_HEADS, WIN_HD))
        new_ckv.append(ckvn.reshape(BATCH, SEQ, MLA_KV_LORA))
        new_kr.append(ctx_cols(COL_KROPE, MLA_ROPE).reshape(BATCH, SEQ, MLA_ROPE))
        new_sf.append(sf)
        new_sb.append(sb)

    y_prompt = x_ctx.reshape(BATCH, SEQ, D)
    y_sample = x_lat.reshape(DEC_BATCH, DEC_SEQ, D)
    return (y_prompt, y_sample, jnp.stack(new_k, 1), jnp.stack(new_v, 1), jnp.stack(new_ckv, 1),
            jnp.stack(new_kr, 1), jnp.stack(new_sf, 1), jnp.stack(new_sb, 1))
```

```python
import functools
import math

import numpy as np
import jax
import jax.numpy as jnp
from jax import lax
from jax.experimental import pallas as pl
from jax.experimental.pallas import tpu as pltpu

F32 = jnp.float32
BF16 = jnp.bfloat16

D_MODEL = 1024
BATCH = 16
SEQ = 256
DEPTH = 2
DEC_BATCH = 2
DEC_SEQ = 1024
PAST_LEN = 256
GRID_W = 64
CHUNK = 128
ROPE_BASE = 10000.0
NEG = -1e30
LN_EPS = 1e-5
RMS_EPS = 1e-6

HY_W = 256
HY_BANDS = 16
HY_EMB = 1 + 2 * HY_BANDS
HY_FFN = 64
HY_FAST_DECAY = 0.3
HY_SLOW_DECAY = 1.5
HY_TARGET = 1e-2

WIN_HEADS = 4
WIN_KV_HEADS = 2
WIN_HD = 64
WINDOW = 128

RET_HEADS = 4
RET_DK = 64
RET_DV = 128

MLA_HEADS = 4
MLA_Q_LORA = 256
MLA_KV_LORA = 128
MLA_NOPE = 64
MLA_ROPE = 32
MLA_V = 64

N_EXPERTS = 16
N_GROUPS = 4
EXPERTS_PER_GROUP = N_EXPERTS // N_GROUPS
D_EXPERT = 256
ROUTE_SCALE = 2.5

ALPHA = (2.0 * DEPTH) ** 0.25

T_CTX = BATCH * SEQ
T_LAT = DEC_BATCH * DEC_SEQ
T_ALL = T_CTX + T_LAT

COL_HY = 0
COL_WQ = 768
COL_WK = 1024
COL_WV = 1152
COL_RQ = 1280
COL_RK = 1536
COL_RV = 1792
COL_RG = 2304
COL_CQ = 2816
COL_CKV = 3072
COL_KROPE = 3200
COL_GATE = 3232
IN_COLS = COL_GATE + 4 * D_MODEL
Z_MAIN = 3328

LANE = 128
STEP_ROWS = 1024
CTX_SEQS_PER_STEP = STEP_ROWS // SEQ
CTX_STEPS = T_CTX // STEP_ROWS
MIXER_STEPS = T_ALL // STEP_ROWS
VMEM_LIMIT = 56 * 1024 * 1024


def _params(*sem):
    return pltpu.CompilerParams(dimension_semantics=sem, vmem_limit_bytes=VMEM_LIMIT)


def _dot(a, b):
    return jnp.dot(a.astype(BF16), b.astype(BF16), preferred_element_type=F32)


def _dot_split(a, b):
    a_hi = a.astype(BF16)
    a_lo = (a - a_hi.astype(F32)).astype(BF16)
    b_hi = b.astype(BF16)
    b_lo = (b - b_hi.astype(F32)).astype(BF16)

    def mm(x, y):
        return jnp.dot(x, y, preferred_element_type=F32)

    return mm(a_hi, b_hi) + (mm(a_lo, b_hi) + mm(a_hi, b_lo))


def _dot_nt(a, b):
    return lax.dot_general(a.astype(BF16), b.astype(BF16), (((1,), (1,)), ((), ())),
                           preferred_element_type=F32)


def _dot_tn(a, b):
    return lax.dot_general(a.astype(BF16), b.astype(BF16), (((0,), (0,)), ((), ())),
                           preferred_element_type=F32)


def _layer_norm(x):
    mu = jnp.mean(x, -1, keepdims=True)
    xc = x - mu
    var = jnp.mean(xc * xc, -1, keepdims=True)
    return xc * lax.rsqrt(var + LN_EPS)


def _mod_row(tile_rows):
    def row(i):
        start = i * tile_rows
        return jnp.where(start < T_CTX, 0, 1 + (start - T_CTX) // DEC_SEQ)
    return row


@functools.lru_cache(maxsize=None)
def _dft_tables(L):
    f = np.arange(L, dtype=np.int64)[:, None]
    s = np.arange(L, dtype=np.int64)[None, :]
    ang = np.pi * ((f * s) % (2 * L)).astype(np.float64) / L
    cos = np.cos(ang)
    sin = np.sin(ang)
    alt = np.where(np.arange(L) % 2 == 0, 1.0, -1.0)
    fwd_im = -sin
    fwd_im[0, :] = alt
    fwd = np.concatenate([cos, fwd_im], 0)
    inv_re = cos.T / L
    inv_re[:, 0] = 1.0 / (2 * L)
    inv_im = -sin.T / L
    inv_im[:, 0] = alt / (2 * L)
    inv = np.concatenate([inv_re, inv_im], 1)
    return fwd.astype(np.float32), inv.astype(np.float32)


@functools.lru_cache(maxsize=None)
def _hyena_embedding(L):
    t01 = np.linspace(0.0, 1.0, L, dtype=np.float64)[:, None]
    bands = np.linspace(1e-4, HY_BANDS - 1, HY_BANDS, dtype=np.float64)
    ang = (2.0 * math.pi / L) * np.arange(L, dtype=np.float64)[:, None] * bands[None, :]
    z = np.concatenate([t01, np.cos(ang), -np.sin(ang)], -1)
    zp = np.zeros((L, LANE), np.float64)
    zp[:, :HY_EMB] = z
    deltas = np.abs(np.linspace(math.log(HY_TARGET) / HY_SLOW_DECAY,
                                math.log(HY_TARGET) / HY_FAST_DECAY, HY_W, dtype=np.float64))
    return zp.astype(np.float32), deltas[None, :].astype(np.float32)


@functools.lru_cache(maxsize=None)
def _rope_tables(L, rot_dim, width):
    rows = L // GRID_W
    n_freq = rot_dim // 4
    half = rot_dim // 2
    inv = ROPE_BASE ** (-np.arange(n_freq, dtype=np.float64) / n_freq)
    pos = np.arange(L)
    row = (pos // GRID_W).astype(np.float64)
    col = (pos % GRID_W).astype(np.float64)
    ang = np.concatenate([row[:, None] * inv, col[:, None] * inv], -1)
    cos, sin = np.cos(ang), np.sin(ang)
    zero = np.zeros_like(sin)
    c = np.tile(np.concatenate([cos, cos], -1), (1, width // rot_dim))
    s_up = np.tile(np.concatenate([-sin, zero], -1), (1, width // rot_dim))
    s_dn = np.tile(np.concatenate([zero, sin], -1), (1, width // rot_dim))
    return c.astype(np.float32), s_up.astype(np.float32), s_dn.astype(np.float32), half


def _rope128(x, c, s_up, s_dn, half):
    up = pltpu.roll(x, LANE - half, axis=1)
    dn = pltpu.roll(x, half, axis=1)
    return x * c + up * s_up + dn * s_dn


def _ada_kernel(c_ref, w_ref, b_ref, o_ref):
    cv = c_ref[...]
    s = cv * jax.nn.sigmoid(cv)
    s_hi = s.astype(BF16)
    s_lo = (s - s_hi.astype(F32)).astype(BF16)
    w = w_ref[...]
    w_hi = w.astype(BF16)
    w_lo = (w - w_hi.astype(F32)).astype(BF16)
    rows = s.shape[0]
    both = jnp.dot(jnp.concatenate([s_hi, s_lo], 0), w_hi, preferred_element_type=F32)
    o_ref[...] = (both[0:rows] + (both[rows:2 * rows] + jnp.dot(s_hi, w_lo, preferred_element_type=F32))
                  + b_ref[...])


ADA_ROWS = 16


def _ada_mods(cvec, w_ada, b_ada):
    tn = 1536
    n = 6 * D_MODEL
    return pl.pallas_call(
        _ada_kernel,
        out_shape=jax.ShapeDtypeStruct((DEPTH, ADA_ROWS, n), F32),
        grid=(DEPTH, n // tn),
        in_specs=[
            pl.BlockSpec((ADA_ROWS, D_MODEL), lambda l, j: (0, 0)),
            pl.BlockSpec((None, D_MODEL, tn), lambda l, j: (l, 0, j)),
            pl.BlockSpec((None, 1, tn), lambda l, j: (l, 0, j)),
        ],
        out_specs=pl.BlockSpec((None, ADA_ROWS, tn), lambda l, j: (l, 0, j)),
        compiler_params=_params("parallel", "parallel"),
        name="ada_mods",
    )(cvec, w_ada, b_ada.reshape(DEPTH, 1, n))


def _lnmod_kernel(*refs):
    x_ref, m_ref, h_ref = refs[-3:]
    y = _layer_norm(x_ref[...])
    s1 = m_ref[:, 0:D_MODEL]
    sc1 = m_ref[:, D_MODEL:2 * D_MODEL]
    h_ref[...] = (y * (1.0 + sc1) + s1).astype(h_ref.dtype)


def _ln_mod(x_group, prev_out, mods_l, row0):
    tm = 512
    row = _mod_row(tm)
    tile0 = row0 // tm
    in_specs = [
        pl.BlockSpec((tm, D_MODEL), lambda i: (i, 0)),
        pl.BlockSpec((None, 1, 6 * D_MODEL), lambda i: (row(tile0 + i), 0, 0)),
    ]
    args = [x_group, mods_l]
    if prev_out is not None:
        in_specs = [pl.BlockSpec(memory_space=pl.ANY)] + in_specs
        args = [prev_out] + args
    return pl.pallas_call(
        _lnmod_kernel,
        out_shape=jax.ShapeDtypeStruct((T_ALL, D_MODEL), BF16),
        grid=(x_group.shape[0] // tm,),
        in_specs=in_specs,
        out_specs=pl.BlockSpec((tm, D_MODEL), lambda i: (tile0 + i, 0)),
        input_output_aliases={} if prev_out is None else {0: 0},
        compiler_params=_params("parallel"),
        name="ln_mod",
    )(*args)


def _proj_kernel(h_ref, w_ref, o_ref, wb_ref, *, gate):
    @pl.when(pl.program_id(1) == 0)
    def _():
        wb_ref[...] = w_ref[...].T.astype(BF16)

    if not gate:
        o_ref[...] = jnp.dot(h_ref[...], wb_ref[...], preferred_element_type=F32).astype(o_ref.dtype)
        return
    sub = 2 * LANE
    for c0 in range(0, o_ref.shape[1], sub):
        r = jnp.dot(h_ref[...], wb_ref[:, c0:c0 + sub], preferred_element_type=F32)
        o_ref[:, c0:c0 + sub] = (0.5 * jnp.tanh(0.5 * r) + 0.5).astype(o_ref.dtype)


def _in_proj(h, w_t, layer, col0, n_cols, tn, out_dtype, gate):
    tm = 2048
    return pl.pallas_call(
        functools.partial(_proj_kernel, gate=gate),
        out_shape=jax.ShapeDtypeStruct((T_ALL, n_cols), out_dtype),
        grid=(n_cols // tn, T_ALL // tm),
        in_specs=[
            pl.BlockSpec((tm, D_MODEL), lambda j, i: (i, 0)),
            pl.BlockSpec((pl.Element(tn), pl.Element(D_MODEL)),
                         lambda j, i: (pl.multiple_of(layer * IN_COLS + col0 + j * tn, 8), 0)),
        ],
        out_specs=pl.BlockSpec((tm, tn), lambda j, i: (i, j)),
        scratch_shapes=[pltpu.VMEM((D_MODEL, tn), BF16)],
        compiler_params=_params("parallel", "arbitrary"),
        name="gate_proj" if gate else "in_proj",
    )(h, w_t)


def _hy_filter_kernel(z_ref, dl_ref, w1_ref, b1_ref, w2_ref, b2_ref, w3_ref, fwd_ref,
                      kre_ref, kim_ref, *, L):
    z = z_ref[...]
    a = jnp.sin(_dot_split(z, w1_ref[...]) + b1_ref[...])
    a = jnp.sin(_dot_split(a, w2_ref[...]) + b2_ref[...])
    h = _dot_split(a, w3_ref[...])
    decay = jnp.exp(-z[:, 0:1] * dl_ref[...])
    not_first = lax.broadcasted_iota(jnp.int32, (L, HY_W), 0) > 0
    sums, diffs = [], []
    for o in range(2):
        fw = h[:, (2 * o) * HY_W:(2 * o + 1) * HY_W] * decay
        bw = jnp.where(not_first, h[:, (2 * o + 1) * HY_W:(2 * o + 2) * HY_W] * decay, 0.0)
        sums.append(fw + bw)
        diffs.append(fw - bw)
    p = _dot(fwd_ref[...], jnp.concatenate(sums, 1))
    q = _dot(fwd_ref[L:2 * L, :], jnp.concatenate(diffs, 1))
    kre_ref[...] = p[0:L]
    first = lax.broadcasted_iota(jnp.int32, (L, 2 * HY_W), 0) == 0
    kim_ref[...] = jnp.where(first, p[L:L + 1], q)


def _hy_filters(L, w1p, b1, w2, b2, w3, fwd):
    zemb, deltas = _hyena_embedding(L)
    out = jax.ShapeDtypeStruct((L, 2 * HY_W), F32)
    return pl.pallas_call(
        functools.partial(_hy_filter_kernel, L=L),
        out_shape=(out, out),
        compiler_params=pltpu.CompilerParams(vmem_limit_bytes=VMEM_LIMIT),
        name=f"hy_filters_{L}",
    )(jnp.asarray(zemb), jnp.asarray(deltas), w1p, b1, w2, b2, w3, fwd)


def _group_step(ctx_body, lat_body):
    i = pl.program_id(0)
    pl.when(i < CTX_STEPS)(ctx_body)
    pl.when(i >= CTX_STEPS)(lat_body)


def _lat_index(i):
    return jnp.maximum(i - CTX_STEPS, 0)


def _hyena_kernel(hy_ref, cw_ref, cb_ref, bias_ref, kre_c, kim_c, fwd_c, inv_c,
                  kre_l, kim_l, fwd_l, inv_l, o_ref):
    _group_step(
        lambda: _hyena_body(hy_ref, cw_ref, cb_ref, bias_ref, kre_c, kim_c, fwd_c, inv_c, o_ref,
                            SEQ, CTX_SEQS_PER_STEP),
        lambda: _hyena_body(hy_ref, cw_ref, cb_ref, bias_ref, kre_l, kim_l, fwd_l, inv_l, o_ref,
                            DEC_SEQ, 1))


def _hyena_body(hy_ref, cw_ref, cb_ref, bias_ref, kre_ref, kim_ref, fwd_ref, inv_ref, o_ref, L, seqs):
    first = lax.broadcasted_iota(jnp.int32, (L, HY_W), 0) == 0

    def long_conv(u, o):
        uf = _dot(fwd_ref[...], u)
        ure, uim = uf[0:L], uf[L:2 * L]
        kre = kre_ref[:, o * HY_W:(o + 1) * HY_W]
        kim = kim_ref[:, o * HY_W:(o + 1) * HY_W]
        yre = jnp.where(first, ure * kre, ure * kre - uim * kim)
        yim = jnp.where(first, uim * kim, ure * kim + uim * kre)
        y = _dot(inv_ref[...], jnp.concatenate([yre, yim], 0))
        return y + u * bias_ref[o:o + 1, :]

    for g in range(seqs):
        sl = slice(g * L, (g + 1) * L)
        x = hy_ref[sl, :].astype(F32)
        rows = lax.broadcasted_iota(jnp.int32, x.shape, 0)
        prev = jnp.where(rows == 0, 0.0, pltpu.roll(x, 1, axis=0))
        nxt = jnp.where(rows == L - 1, 0.0, pltpu.roll(x, L - 1, axis=0))
        z = prev * cw_ref[0:1, :] + x * cw_ref[1:2, :] + nxt * cw_ref[2:3, :] + cb_ref[...]
        v, x1, x2 = z[:, 0:HY_W], z[:, HY_W:2 * HY_W], z[:, 2 * HY_W:3 * HY_W]
        u = x1 * long_conv(v, 0)
        o_ref[sl, :] = (x2 * long_conv(u, 1)).astype(o_ref.dtype)


def _const_spec(shape):
    return pl.BlockSpec(shape, lambda i: (0,) * len(shape))


def _hyena(z_main, layer, conv_w, conv_b, bias, filters, dft):
    tables, table_specs = [], []
    for L in (SEQ, DEC_SEQ):
        tables += [*filters[L], *dft[L]]
        table_specs += [_const_spec((L, 2 * HY_W)), _const_spec((L, 2 * HY_W)),
                        _const_spec((2 * L, L)), _const_spec((L, 2 * L))]
    return pl.pallas_call(
        _hyena_kernel,
        out_shape=jax.ShapeDtypeStruct((T_ALL, HY_W), BF16),
        grid=(MIXER_STEPS,),
        in_specs=[
            pl.BlockSpec((STEP_ROWS, 3 * HY_W), lambda i: (i, 0)),
            pl.BlockSpec((None, 3, 3 * HY_W), lambda i: (layer, 0, 0)),
            pl.BlockSpec((None, 1, 3 * HY_W), lambda i: (layer, 0, 0)),
            pl.BlockSpec((None, 2, HY_W), lambda i: (layer, 0, 0)),
        ] + table_specs,
        out_specs=pl.BlockSpec((STEP_ROWS, HY_W), lambda i: (i, 0)),
        compiler_params=_params("parallel"),
        name="hyena",
    )(z_main, conv_w, conv_b.reshape(DEPTH, 1, 3 * HY_W), bias, *tables)


def _win_masks():
    lane = lax.broadcasted_iota(jnp.int32, (1, LANE), 1)
    return lane < WIN_HD, lane >= WIN_HD


def _win_head_operands(q, k, v, h):
    lo_mask, hi_mask = _win_masks()
    col = h // 2
    lo = h % 2 == 0
    q128 = jnp.where(lo_mask if lo else hi_mask, q[:, col * LANE:(col + 1) * LANE], 0.0)
    swap = h in (1, 2)
    if swap:
        k = pltpu.roll(k, WIN_HD, axis=1)
        v = pltpu.roll(v, WIN_HD, axis=1)
    return q128, k, v, lo


def _win_kernel(sink_ref, q_ref, kv_ref, ck_ref, cv_ref, c_ref, su_ref, sd_ref, o_ref, *, layer):
    _group_step(
        lambda: _win_ctx_body(sink_ref, q_ref, kv_ref, o_ref, layer),
        lambda: _win_lat_body(sink_ref, q_ref, kv_ref, ck_ref, cv_ref, c_ref, su_ref, sd_ref, o_ref, layer))


def _win_ctx_body(sink_ref, q_ref, kv_ref, o_ref, layer):
    lo_mask, hi_mask = _win_masks()
    scale = WIN_HD ** -0.5
    for g in range(CTX_SEQS_PER_STEP):
        sl = slice(g * SEQ, (g + 1) * SEQ)
        q = q_ref[sl, :].astype(F32)
        k = kv_ref[sl, 0:LANE].astype(F32)
        v = kv_ref[sl, LANE:2 * LANE].astype(F32)
        cols = []
        for col in range(2):
            acc = None
            for h in (2 * col, 2 * col + 1):
                q128, kk, vv, lo = _win_head_operands(q, k, v, h)
                s = _dot_nt(q128, kk) * scale
                sink = sink_ref[layer, h]
                m = jnp.maximum(jnp.max(s, -1, keepdims=True), sink)
                p = jnp.exp(s - m)
                den = jnp.sum(p, -1, keepdims=True) + jnp.exp(sink - m)
                o = _dot(p, vv) / den
                o = jnp.where(lo_mask if lo else hi_mask, o, 0.0)
                acc = o if acc is None else acc + o
            cols.append(acc)
        o_ref[sl, :] = jnp.concatenate(cols, 1).astype(o_ref.dtype)


def _win_lat_body(sink_ref, q_ref, kv_ref, ck_ref, cv_ref, c_ref, su_ref, sd_ref, o_ref, layer):
    L = DEC_SEQ
    half = WIN_HD // 2
    c, su, sd = c_ref[...], su_ref[...], sd_ref[...]
    q = jnp.concatenate(
        [_rope128(q_ref[:, i * LANE:(i + 1) * LANE].astype(F32), c, su, sd, half) for i in range(2)], 1)
    k = _rope128(kv_ref[:, 0:LANE].astype(F32), c, su, sd, half)
    v = kv_ref[:, LANE:2 * LANE].astype(F32)
    ck = ck_ref[...]
    cv = cv_ref[...]
    lo_mask, hi_mask = _win_masks()
    scale = WIN_HD ** -0.5
    nb = L // CHUNK
    cols = []
    for col in range(2):
        acc_blocks = [None] * nb
        for h in (2 * col, 2 * col + 1):
            q128, kk, vv, lo = _win_head_operands(q, k, v, h)
            _, ckk, cvv, _ = _win_head_operands(q, ck, cv, h)
            sink = sink_ref[layer, h]
            for n in range(nb):
                k0 = max(0, (n - 1) * CHUNK)
                k1 = min(L, (n + 2) * CHUNK)
                qn = q128[n * CHUNK:(n + 1) * CHUNK]
                s_loc = _dot_nt(qn, kk[k0:k1]) * scale
                qi = n * CHUNK + lax.broadcasted_iota(jnp.int32, s_loc.shape, 0)
                kj = k0 + lax.broadcasted_iota(jnp.int32, s_loc.shape, 1)
                s_loc = jnp.where(jnp.abs(qi - kj) <= WINDOW, s_loc, NEG)
                s_ctx = _dot_nt(qn, ckk) * scale
                m = jnp.maximum(jnp.maximum(jnp.max(s_loc, -1, keepdims=True),
                                            jnp.max(s_ctx, -1, keepdims=True)), sink)
                p_loc = jnp.exp(s_loc - m)
                p_ctx = jnp.exp(s_ctx - m)
                den = (jnp.sum(p_loc, -1, keepdims=True) + jnp.sum(p_ctx, -1, keepdims=True)
                       + jnp.exp(sink - m))
                o = (_dot(p_loc, vv[k0:k1]) + _dot(p_ctx, cvv)) / den
                o = jnp.where(lo_mask if lo else hi_mask, o, 0.0)
                acc_blocks[n] = o if acc_blocks[n] is None else acc_blocks[n] + o
        cols.append(jnp.concatenate(acc_blocks, 0))
    o_ref[...] = jnp.concatenate(cols, 1).astype(o_ref.dtype)


def _win(z_main, sink, cache_k, cache_v, layer):
    c, su, sd, _ = _rope_tables(DEC_SEQ, WIN_HD, LANE)
    tab = _const_spec((DEC_SEQ, LANE))
    cache = pl.BlockSpec((None, None, PAST_LEN, LANE), lambda i: (_lat_index(i), layer, 0, 0))
    return pl.pallas_call(
        functools.partial(_win_kernel, layer=layer),
        out_shape=jax.ShapeDtypeStruct((T_ALL, WIN_HEADS * WIN_HD), BF16),
        grid=(MIXER_STEPS,),
        in_specs=[
            pl.BlockSpec(memory_space=pltpu.SMEM),
            pl.BlockSpec((STEP_ROWS, 256), lambda i: (i, COL_WQ // 256)),
            pl.BlockSpec((STEP_ROWS, 256), lambda i: (i, COL_WK // 256)),
            cache, cache, tab, tab, tab,
        ],
        out_specs=pl.BlockSpec((STEP_ROWS, 256), lambda i: (i, 0)),
        compiler_params=_params("parallel"),
        name="win",
    )(sink, z_main, z_main, cache_k, cache_v, jnp.asarray(c), jnp.asarray(su), jnp.asarray(sd))


def _ret_kernel(*refs, layer):
    _group_step(lambda: _ret_body(*refs, L=SEQ, layer=layer, ctx=True),
                lambda: _ret_body(*refs, L=DEC_SEQ, layer=layer, ctx=False))


def _ret_body(df_ref, db_ref, q_ref, k_ref, v0_ref, v1_ref, g0_ref, g1_ref, s0f_ref, s0b_ref,
              o_ref, sf_out, sb_out, s_ref, cross_ref, *, L, layer, ctx):
    seqs = STEP_ROWS // L
    if not ctx:
        sf_out[...] = jnp.zeros_like(sf_out)
        sb_out[...] = jnp.zeros_like(sb_out)
    C = CHUNK
    nc = L // C
    H = RET_HEADS
    qw = H * RET_DK
    vw = H * RET_DV

    def lane_table(width, per_head, fn):
        pos = lax.broadcasted_iota(jnp.int32, (C, per_head), 0).astype(F32)
        return jnp.concatenate([fn(h, pos) for h in range(H)], 1)

    def log_gamma(ref, h):
        d = jnp.full((1, 1), ref[layer, h], F32)
        return jnp.log(jax.nn.sigmoid(d))

    lgf = [log_gamma(df_ref, h) for h in range(H)]
    lgb = [log_gamma(db_ref, h) for h in range(H)]

    def tables(lg, reverse):
        if reverse:
            dq = lane_table(vw, RET_DV, lambda h, pos: jnp.exp((C - pos) * lg[h]))
            dk = lane_table(qw, RET_DK, lambda h, pos: jnp.exp(pos * lg[h]))
        else:
            dq = lane_table(vw, RET_DV, lambda h, pos: jnp.exp((pos + 1.0) * lg[h]))
            dk = lane_table(qw, RET_DK, lambda h, pos: jnp.exp((C - 1.0 - pos) * lg[h]))
        dc = jnp.concatenate([jnp.broadcast_to(jnp.exp(C * lg[h]), (1, RET_DV)) for h in range(H)], 1)
        return dq, dk, dc

    tab_f = tables(lgf, False)
    tab_b = tables(lgb, True)
    ii = lax.broadcasted_iota(jnp.int32, (C, C), 0)
    jj = lax.broadcasted_iota(jnp.int32, (C, C), 1)
    diff = (ii - jj).astype(F32)
    dmats = [jnp.where(diff >= 0, jnp.exp(jnp.maximum(diff, 0.0) * lgf[h]), 0.0)
             + jnp.where(diff <= 0, jnp.exp(jnp.maximum(-diff, 0.0) * lgb[h]), 0.0) for h in range(H)]
    lane_q = lax.broadcasted_iota(jnp.int32, (1, qw), 1) // RET_DK

    srow = lax.broadcasted_iota(jnp.int32, (qw, vw), 0) // RET_DK
    scol = lax.broadcasted_iota(jnp.int32, (qw, vw), 1) // RET_DV
    diag = srow == scol

    for g in range(seqs):
        base = g * L
        rows_all = slice(base, base + L)
        q_all = q_ref[rows_all, :].astype(F32)
        k_all = k_ref[rows_all, :].astype(F32) * (RET_DK ** -0.5)
        v_all = jnp.concatenate([v0_ref[rows_all, :], v1_ref[rows_all, :]], 1).astype(F32)
        g_all = jnp.concatenate([g0_ref[rows_all, :], g1_ref[rows_all, :]], 1).astype(F32)

        def scan(tabs, reverse, s0_ref, s_out):
            dq, dk, dc = tabs
            s_ref[g] = jnp.zeros((qw, vw), F32)
            if s0_ref is not None:
                for h in range(H):
                    s_ref[g, h * RET_DK:(h + 1) * RET_DK, h * RET_DV:(h + 1) * RET_DV] = s0_ref[h]
            order = range(nc - 1, -1, -1) if reverse else range(nc)
            for ci in order:
                sl = slice(ci * C, (ci + 1) * C)
                rs = slice(base + ci * C, base + (ci + 1) * C)
                qc, kc, vc = q_all[sl], k_all[sl], v_all[sl]
                st = s_ref[g]
                cross = _dot(qc, st) * dq
                if reverse:
                    cross_ref[rs, :] = cross_ref[rs, :] + cross
                else:
                    cross_ref[rs, :] = cross
                upd = jnp.where(diag, _dot_tn(kc * dk, vc), 0.0)
                s_ref[g] = st * dc + upd
            if s_out is not None:
                for h in range(H):
                    s_out[g, h] = s_ref[g, h * RET_DK:(h + 1) * RET_DK, h * RET_DV:(h + 1) * RET_DV]

        scan(tab_f, False, None if ctx else s0f_ref, sf_out if ctx else None)
        scan(tab_b, True, None if ctx else s0b_ref, sb_out if ctx else None)

        for h in range(H):
            hv = slice(h * RET_DV, (h + 1) * RET_DV)
            for ci in range(nc):
                sl = slice(ci * C, (ci + 1) * C)
                qh = jnp.where(lane_q == h, q_all[sl], 0.0)
                att = _dot_nt(qh, k_all[sl]) * dmats[h]
                o = _dot(att, v_all[sl, hv]) + cross_ref[base + ci * C:base + (ci + 1) * C, hv]
                gt = g_all[sl, hv]
                o_ref[base + ci * C:base + (ci + 1) * C, hv] = (
                    (gt * jax.nn.sigmoid(gt)) * _layer_norm(o)).astype(o_ref.dtype)


def _retention(z_main, dec_f, dec_b, s0f, s0b, layer):
    def zcol(col):
        return pl.BlockSpec((STEP_ROWS, 256), lambda i: (i, col // 256))

    smem = pl.BlockSpec(memory_space=pltpu.SMEM)
    z_specs = [zcol(COL_RQ), zcol(COL_RK), zcol(COL_RV), zcol(COL_RV + 256),
               zcol(COL_RG), zcol(COL_RG + 256)]
    s0_spec = pl.BlockSpec((None, None, RET_HEADS, RET_DK, RET_DV),
                           lambda i: (_lat_index(i), layer, 0, 0, 0))
    st_shape = jax.ShapeDtypeStruct((MIXER_STEPS * CTX_SEQS_PER_STEP, RET_HEADS, RET_DK, RET_DV), F32)
    st_spec = pl.BlockSpec((CTX_SEQS_PER_STEP, RET_HEADS, RET_DK, RET_DV), lambda i: (i, 0, 0, 0))
    return pl.pallas_call(
        functools.partial(_ret_kernel, layer=layer),
        out_shape=(jax.ShapeDtypeStruct((T_ALL, RET_HEADS * RET_DV), BF16), st_shape, st_shape),
        grid=(MIXER_STEPS,),
        in_specs=[smem, smem] + z_specs + [s0_spec, s0_spec],
        out_specs=(pl.BlockSpec((STEP_ROWS, RET_HEADS * RET_DV), lambda i: (i, 0)), st_spec, st_spec),
        scratch_shapes=[pltpu.VMEM((CTX_SEQS_PER_STEP, RET_HEADS * RET_DK, RET_HEADS * RET_DV), F32),
                        pltpu.VMEM((STEP_ROWS, RET_HEADS * RET_DV), F32)],
        compiler_params=_params("parallel"),
        name="retention",
    )(dec_f, dec_b, *([z_main] * 6), s0f, s0b)


def _rms_norm(x, g):
    return x * lax.rsqrt(jnp.mean(x * x, -1, keepdims=True) + RMS_EPS) * g


def _mla_attend(qn, qr, kn, kr, vv, o_ref, row0):
    scale = (MLA_NOPE + MLA_ROPE) ** -0.5
    lane_n = lax.broadcasted_iota(jnp.int32, (1, MLA_HEADS * MLA_NOPE), 1) // MLA_NOPE
    lane_r = lax.broadcasted_iota(jnp.int32, (1, LANE), 1)
    kr32 = jnp.where(lane_r < MLA_ROPE, kr, 0.0)
    acc = None
    for h in range(MLA_HEADS):
        qnh = jnp.where(lane_n == h, qn, 0.0)
        qrh = qr if h == 0 else pltpu.roll(qr, LANE - h * MLA_ROPE, axis=1)
        qrh = jnp.where(lane_r < MLA_ROPE, qrh, 0.0)
        s = (_dot_nt(qnh, kn) + _dot_nt(qrh, kr32)) * scale
        m = jnp.max(s, -1, keepdims=True)
        p = jnp.exp(s - m)
        den = jnp.sum(p, -1, keepdims=True)
        o = jnp.where(lane_n == h, _dot(p, vv) / den, 0.0)
        acc = o if acc is None else acc + o
    o_ref[row0:row0 + acc.shape[0], :] = acc.astype(o_ref.dtype)


def _mla_kernel(cq_ref, ckv_ref, kr_ref, cckv_ref, ckr_ref, c_ref, su_ref, sd_ref,
                qg_ref, kg_ref, wqn_ref, wqr_ref, wk_ref, wv_ref, o_ref, ckvn_ref):
    weights = (qg_ref, kg_ref, wqn_ref, wqr_ref, wk_ref, wv_ref)
    _group_step(
        lambda: _mla_ctx_body(cq_ref, ckv_ref, kr_ref, *weights, o_ref, ckvn_ref),
        lambda: _mla_lat_body(cq_ref, ckv_ref, kr_ref, cckv_ref, ckr_ref, c_ref, su_ref, sd_ref,
                              *weights, o_ref, ckvn_ref))


def _mla_ctx_body(cq_ref, ckv_ref, kr_ref, qg_ref, kg_ref, wqn_ref, wqr_ref, wk_ref, wv_ref,
                  o_ref, ckvn_ref):
    cqn = _rms_norm(cq_ref[...].astype(F32), qg_ref[...])
    qn = _dot(cqn, wqn_ref[...])
    qr = _dot(cqn, wqr_ref[...])
    ckvn = _rms_norm(ckv_ref[...].astype(F32), kg_ref[...])
    ckvn_ref[...] = ckvn
    kn = _dot(ckvn, wk_ref[...])
    vv = _dot(ckvn, wv_ref[...])
    for g in range(CTX_SEQS_PER_STEP):
        sl = slice(g * SEQ, (g + 1) * SEQ)
        _mla_attend(qn[sl], qr[sl], kn[sl], kr_ref[sl, :].astype(F32), vv[sl], o_ref, g * SEQ)


def _mla_lat_body(cq_ref, ckv_ref, kr_ref, cckv_ref, ckr_ref, c_ref, su_ref, sd_ref,
                  qg_ref, kg_ref, wqn_ref, wqr_ref, wk_ref, wv_ref, o_ref, ckvn_ref):
    half = MLA_ROPE // 2
    c, su, sd = c_ref[...], su_ref[...], sd_ref[...]
    cqn = _rms_norm(cq_ref[...].astype(F32), qg_ref[...])
    qn = _dot(cqn, wqn_ref[...])
    qr = _rope128(_dot(cqn, wqr_ref[...]), c, su, sd, half)
    ckvn = _rms_norm(ckv_ref[...].astype(F32), kg_ref[...])
    ckvn_ref[...] = ckvn
    ckv_all = jnp.concatenate([ckvn, cckv_ref[...]], 0)
    kn = _dot(ckv_all, wk_ref[...])
    vv = _dot(ckv_all, wv_ref[...])
    kr = jnp.concatenate([_rope128(kr_ref[...].astype(F32), c, su, sd, half), ckr_ref[...]], 0)
    for n in range(DEC_SEQ // 256):
        rows = slice(n * 256, (n + 1) * 256)
        _mla_attend(qn[rows], qr[rows], kn, kr, vv, o_ref, n * 256)


def _mla(z_main, cache_ckv, cache_kr_pad, weights, layer):
    c, su, sd, _ = _rope_tables(DEC_SEQ, MLA_ROPE, LANE)
    tab = _const_spec((DEC_SEQ, LANE))
    cache = pl.BlockSpec((None, None, PAST_LEN, LANE), lambda i: (_lat_index(i), layer, 0, 0))

    def weight(*shape):
        return pl.BlockSpec((None,) + shape, lambda i: (layer, 0, 0))

    return pl.pallas_call(
        _mla_kernel,
        out_shape=(jax.ShapeDtypeStruct((T_ALL, MLA_HEADS * MLA_V), BF16),
                   jax.ShapeDtypeStruct((T_ALL, MLA_KV_LORA), F32)),
        grid=(MIXER_STEPS,),
        in_specs=[
            pl.BlockSpec((STEP_ROWS, 256), lambda i: (i, COL_CQ // 256)),
            pl.BlockSpec((STEP_ROWS, LANE), lambda i: (i, COL_CKV // LANE)),
            pl.BlockSpec((STEP_ROWS, LANE), lambda i: (i, COL_KROPE // LANE)),
            cache, cache, tab, tab, tab,
            weight(1, MLA_Q_LORA), weight(1, MLA_KV_LORA),
            weight(MLA_Q_LORA, MLA_HEADS * MLA_NOPE), weight(MLA_Q_LORA, MLA_HEADS * MLA_ROPE),
            weight(MLA_KV_LORA, MLA_HEADS * MLA_NOPE), weight(MLA_KV_LORA, MLA_HEADS * MLA_V),
        ],
        out_specs=(pl.BlockSpec((STEP_ROWS, 256), lambda i: (i, 0)),
                   pl.BlockSpec((STEP_ROWS, MLA_KV_LORA), lambda i: (i, 0))),
        compiler_params=_params("parallel"),
        name="mla",
    )(z_main, z_main, z_main, cache_ckv, cache_kr_pad,
      jnp.asarray(c), jnp.asarray(su), jnp.asarray(sd), *weights)


def _route(logits_t, rb):
    scores = jax.nn.sigmoid(logits_t)
    biased = scores + rb
    sc = [scores[e:e + 1, :] for e in range(N_EXPERTS)]
    bi = [biased[e:e + 1, :] for e in range(N_EXPERTS)]
    epg = EXPERTS_PER_GROUP
    gsum = []
    for g in range(N_GROUPS):
        v = bi[g * epg:(g + 1) * epg]
        best = None
        for a in range(epg):
            for b in range(a + 1, epg):
                pair = v[a] + v[b]
                best = pair if best is None else jnp.maximum(best, pair)
        gsum.append(best)
    combine = []
    sel = []
    for g in range(N_GROUPS):
        is_best = None
        for g2 in range(N_GROUPS):
            if g2 == g:
                continue
            c = gsum[g] > gsum[g2] if g2 < g else gsum[g] >= gsum[g2]
            is_best = c if is_best is None else jnp.logical_and(is_best, c)
        for a in range(epg):
            e = g * epg + a
            rank = jnp.zeros_like(bi[e])
            for b in range(epg):
                if b == a:
                    continue
                e2 = g * epg + b
                ahead = bi[e2] >= bi[e] if b < a else bi[e2] > bi[e]
                rank = rank + jnp.where(ahead, 1.0, 0.0)
            sel.append(jnp.logical_and(is_best, rank < 2.0))
    wsum = None
    for e in range(N_EXPERTS):
        w = jnp.where(sel[e], sc[e], 0.0)
        wsum = w if wsum is None else wsum + w
    for e in range(N_EXPERTS):
        combine.append(jnp.where(sel[e], ROUTE_SCALE * sc[e] / wsum, 0.0))
    return jnp.concatenate(combine, 0)


def _merge_kernel(ya_ref, yb_ref, yc_ref, yd_ref, gt_ref, xc_ref, xl_ref, m_ref,
                  wa_ref, wb_ref, wc_ref, wd_ref, wo_ref, g_ref, b_ref, rwh_ref, rwl_ref, rb_ref,
                  x1_ref, h2_ref, cmb_ref, *, ctx_tiles, sub_rows):
    D = D_MODEL
    g1 = m_ref[:, 2 * D:3 * D]
    s2 = m_ref[:, 3 * D:4 * D]
    sc2 = m_ref[:, 4 * D:5 * D]
    is_ctx = pl.program_id(0) < ctx_tiles
    branches = ((ya_ref, wa_ref), (yb_ref, wb_ref), (yc_ref, wc_ref), (yd_ref, wd_ref))
    for r0 in range(0, x1_ref.shape[0], sub_rows):
        rows = slice(r0, r0 + sub_rows)
        merged = None
        for i, (y_ref, w) in enumerate(branches):
            t = gt_ref[rows, i * D:(i + 1) * D].astype(F32) * jnp.dot(
                y_ref[rows, :], w[...], preferred_element_type=F32)
            merged = t if merged is None else merged + t
        out1 = jnp.dot(merged.astype(BF16), wo_ref[...], preferred_element_type=F32)
        x = jnp.where(is_ctx, xc_ref[rows, :], xl_ref[rows, :])
        x1 = _layer_norm(ALPHA * x + g1 * out1) * g_ref[...] + b_ref[...]
        x1_ref[rows, :] = x1
        h2 = _layer_norm(x1) * (1.0 + sc2) + s2
        h2_hi = h2.astype(BF16)
        h2_ref[rows, :] = h2_hi
        h2_lo = (h2 - h2_hi.astype(F32)).astype(BF16)
        logits = (jnp.dot(h2_hi, rwh_ref[...], preferred_element_type=F32)
                  + (jnp.dot(h2_lo, rwh_ref[...], preferred_element_type=F32)
                     + jnp.dot(h2_hi, rwl_ref[...], preferred_element_type=F32)))
        cmb_ref[:, rows] = _route(logits.T[0:N_EXPERTS], rb_ref[...])


def _merge(ya, yb, yc, yd, gates, x_ctx, x_lat, mods_l, w_br, w_out_bf, ln1_g, ln1_b,
           router_w_parts, router_b, layer):
    tm = 512
    row = _mod_row(tm)
    D = D_MODEL
    ctx_tiles = T_CTX // tm

    def tile(w):
        return pl.BlockSpec((tm, w), lambda i: (i, 0))

    def weight(k, n):
        return pl.BlockSpec((None, k, n), lambda i: (layer, 0, 0))

    return pl.pallas_call(
        functools.partial(_merge_kernel, ctx_tiles=ctx_tiles, sub_rows=256),
        out_shape=(jax.ShapeDtypeStruct((T_ALL, D), F32),
                   jax.ShapeDtypeStruct((T_ALL, D), BF16),
                   jax.ShapeDtypeStruct((N_EXPERTS, T_ALL), F32)),
        grid=(T_ALL // tm,),
        in_specs=[
            tile(256), tile(256), tile(512), tile(256), tile(4 * D),
            pl.BlockSpec((tm, D), lambda i: (jnp.minimum(i, ctx_tiles - 1), 0)),
            pl.BlockSpec((tm, D), lambda i: (jnp.maximum(i - ctx_tiles, 0), 0)),
            pl.BlockSpec((None, 1, 6 * D), lambda i: (row(i), 0, 0)),
            weight(256, D), weight(256, D), weight(512, D), weight(256, D), weight(D, D),
            weight(1, D), weight(1, D),
            pl.BlockSpec((D, LANE), lambda i: (0, 0)),
            pl.BlockSpec((D, LANE), lambda i: (0, 0)),
            pl.BlockSpec((N_EXPERTS, 1), lambda i: (0, 0)),
        ],
        out_specs=(tile(D), tile(D), pl.BlockSpec((N_EXPERTS, tm), lambda i: (0, i))),
        compiler_params=_params("parallel"),
        name="merge",
    )(ya, yb, yc, yd, gates, x_ctx, x_lat, mods_l, *w_br, w_out_bf,
      ln1_g.reshape(DEPTH, 1, D), ln1_b.reshape(DEPTH, 1, D), *router_w_parts,
      router_b.reshape(N_EXPERTS, 1))


MOE_EXPERTS_PER_STEP = 2


def _moe_kernel(*refs, next_h):
    if next_h:
        (h_ref, c_ref, x1_ref, m_ref, wg_ref, wu_ref, wd_ref, g_ref, b_ref, mn_ref,
         o_ref, hn_ref, acc_ref) = refs[-13:]
    else:
        h_ref, c_ref, x1_ref, m_ref, wg_ref, wu_ref, wd_ref, g_ref, b_ref, o_ref, acc_ref = refs
    eg = pl.program_id(1)

    @pl.when(eg == 0)
    def _():
        acc_ref[...] = jnp.zeros_like(acc_ref)

    h = h_ref[...]
    cmb = c_ref[...]
    lane = lax.broadcasted_iota(jnp.int32, cmb.shape, 1)
    hid = []
    for k in range(MOE_EXPERTS_PER_STEP):
        gate = jnp.dot(h, wg_ref[k].astype(BF16), preferred_element_type=F32)
        up = jnp.dot(h, wu_ref[k].astype(BF16), preferred_element_type=F32)
        e = eg * MOE_EXPERTS_PER_STEP + k
        ce = jnp.sum(jnp.where(lane == e, cmb, 0.0), -1, keepdims=True)
        hid.append(((gate * jax.nn.sigmoid(gate)) * up * ce).astype(BF16))
    wd = wd_ref[...].reshape(MOE_EXPERTS_PER_STEP * D_EXPERT, D_MODEL).astype(BF16)
    acc_ref[...] += jnp.dot(jnp.concatenate(hid, 1), wd, preferred_element_type=F32)

    @pl.when(eg == N_EXPERTS // MOE_EXPERTS_PER_STEP - 1)
    def _():
        g2 = m_ref[:, 5 * D_MODEL:6 * D_MODEL]
        y = _layer_norm(ALPHA * x1_ref[...] + g2 * acc_ref[...])
        y = y * g_ref[...] + b_ref[...]
        o_ref[...] = y
        if next_h:
            s1 = mn_ref[:, 0:D_MODEL]
            sc1 = mn_ref[:, D_MODEL:2 * D_MODEL]
            hn_ref[...] = (_layer_norm(y) * (1.0 + sc1) + s1).astype(hn_ref.dtype)


def _moe(h2, combine, x1, mods_l, w_gate, w_up, w_down, ln2_g, ln2_b, layer, row0, n_rows,
         mods_next=None, prev_h=None):
    tm = 1024
    row = _mod_row(tm)
    D = D_MODEL
    t0 = row0 // tm
    eps = MOE_EXPERTS_PER_STEP
    next_h = mods_next is not None
    mod_spec = pl.BlockSpec((None, 1, 6 * D), lambda i, e: (row(t0 + i), 0, 0))
    in_specs = [
        pl.BlockSpec((tm, D), lambda i, e: (t0 + i, 0)),
        pl.BlockSpec((tm, N_EXPERTS), lambda i, e: (t0 + i, 0)),
        pl.BlockSpec((tm, D), lambda i, e: (t0 + i, 0)),
        mod_spec,
        pl.BlockSpec((None, eps, D, D_EXPERT), lambda i, e: (layer, e, 0, 0)),
        pl.BlockSpec((None, eps, D, D_EXPERT), lambda i, e: (layer, e, 0, 0)),
        pl.BlockSpec((None, eps, D_EXPERT, D), lambda i, e: (layer, e, 0, 0)),
        pl.BlockSpec((None, 1, D), lambda i, e: (layer, 0, 0)),
        pl.BlockSpec((None, 1, D), lambda i, e: (layer, 0, 0)),
    ]
    args = [h2, combine, x1, mods_l, w_gate, w_up, w_down,
            ln2_g.reshape(DEPTH, 1, D), ln2_b.reshape(DEPTH, 1, D)]
    out_shape = jax.ShapeDtypeStruct((n_rows, D), F32)
    out_specs = pl.BlockSpec((tm, D), lambda i, e: (i, 0))
    aliases = {}
    if next_h:
        in_specs.append(mod_spec)
        args.append(mods_next)
        out_shape = (out_shape, jax.ShapeDtypeStruct((T_ALL, D), BF16))
        out_specs = (out_specs, pl.BlockSpec((tm, D), lambda i, e: (t0 + i, 0)))
        if prev_h is not None:
            in_specs = [pl.BlockSpec(memory_space=pl.ANY)] + in_specs
            args = [prev_h] + args
            aliases = {0: 1}
    return pl.pallas_call(
        functools.partial(_moe_kernel, next_h=next_h),
        out_shape=out_shape,
        grid=(n_rows // tm, N_EXPERTS // eps),
        in_specs=in_specs,
        out_specs=out_specs,
        input_output_aliases=aliases,
        scratch_shapes=[pltpu.VMEM((tm, D), F32)],
        compiler_params=_params("parallel", "arbitrary"),
        name="moe",
    )(*args)


def kernel(x_prompt, x_sample, cache_win_k, cache_win_v, cache_mla_ckv, cache_mla_krope,
           state_ret_fwd, state_ret_bwd, c, c_ctx, w_ada, b_ada, w_in,
           hy_conv_w, hy_conv_b, hy_w1, hy_b1, hy_w2, hy_b2, hy_w3, hy_bias,
           win_sink, ret_decay_fwd, ret_decay_bwd, mla_q_norm, mla_kv_norm, mla_w_uq, mla_w_ukv,
           w_br_a, w_br_b, w_br_c, w_br_d, w_out, ln1_g, ln1_b, ln2_g, ln2_b,
           router_w, router_b, moe_w_gate, moe_w_up, moe_w_down):
    D = D_MODEL
    x_ctx = x_prompt.reshape(T_CTX, D)
    x_lat = x_sample.reshape(T_LAT, D)

    cvec = jnp.zeros((ADA_ROWS, D), F32).at[0].set(c_ctx).at[1:1 + DEC_BATCH].set(c)
    mods = _ada_mods(cvec, w_ada, b_ada)[:, :1 + DEC_BATCH].reshape(DEPTH, 1 + DEC_BATCH, 1, 6 * D)

    w_in_t = jnp.swapaxes(w_in, 1, 2).reshape(DEPTH * IN_COLS, D)
    cache_k = cache_win_k.reshape(DEC_BATCH, DEPTH, PAST_LEN, WIN_KV_HEADS * WIN_HD)
    cache_v = cache_win_v.reshape(DEC_BATCH, DEPTH, PAST_LEN, WIN_KV_HEADS * WIN_HD)
    cache_kr = jnp.pad(cache_mla_krope, ((0, 0), (0, 0), (0, 0), (0, LANE - MLA_ROPE)))

    uq = mla_w_uq.reshape(DEPTH, MLA_Q_LORA, MLA_HEADS, MLA_NOPE + MLA_ROPE)
    ukv = mla_w_ukv.reshape(DEPTH, MLA_KV_LORA, MLA_HEADS, MLA_NOPE + MLA_V)
    mla_weights = (
        mla_q_norm.reshape(DEPTH, 1, MLA_Q_LORA),
        mla_kv_norm.reshape(DEPTH, 1, MLA_KV_LORA),
        uq[..., :MLA_NOPE].reshape(DEPTH, MLA_Q_LORA, MLA_HEADS * MLA_NOPE),
        uq[..., MLA_NOPE:].reshape(DEPTH, MLA_Q_LORA, MLA_HEADS * MLA_ROPE),
        ukv[..., :MLA_NOPE].reshape(DEPTH, MLA_KV_LORA, MLA_HEADS * MLA_NOPE),
        ukv[..., MLA_NOPE:].reshape(DEPTH, MLA_KV_LORA, MLA_HEADS * MLA_V),
    )

    hy_w1p = jnp.pad(hy_w1, ((0, 0), (0, LANE - HY_EMB), (0, 0)))
    dft = {}
    for L in (SEQ, DEC_SEQ):
        fwd, inv = _dft_tables(L)
        dft[L] = (jnp.asarray(fwd).astype(BF16), jnp.asarray(inv).astype(BF16))
    router_w_pad = jnp.pad(router_w, ((0, 0), (0, LANE - N_EXPERTS)))
    router_w_hi = router_w_pad.astype(BF16)
    router_w_parts = (router_w_hi, (router_w_pad - router_w_hi.astype(F32)).astype(BF16))
    w_br = tuple(w.astype(BF16) for w in (w_br_a, w_br_b, w_br_c, w_br_d))
    w_out_bf = w_out.astype(BF16)

    new_k, new_v, new_ckv, new_kr, new_sf, new_sb = [], [], [], [], [], []
    for l in range(DEPTH):
        mods_l = mods[l]
        if l == 0:
            h = _ln_mod(x_ctx, None, mods_l, 0)
            h = _ln_mod(x_lat, h, mods_l, T_CTX)
        z = _in_proj(h, w_in_t, l, 0, Z_MAIN, Z_MAIN // 2, BF16, gate=False)
        gates = _in_proj(h, w_in_t, l, COL_GATE, 4 * D, D, BF16, gate=True)

        filters = {L: _hy_filters(L, hy_w1p[l], hy_b1[l][None], hy_w2[l], hy_b2[l][None], hy_w3[l],
                                  dft[L][0]) for L in (SEQ, DEC_SEQ)}
        ya = _hyena(z, l, hy_conv_w, hy_conv_b, hy_bias, filters, dft)
        yb = _win(z, win_sink, cache_k, cache_v, l)
        yc, sf, sb = _retention(z, ret_decay_fwd, ret_decay_bwd, state_ret_fwd, state_ret_bwd, l)
        yd, ckvn = _mla(z, cache_mla_ckv, cache_kr, mla_weights, l)

        x1, h2, combine_t = _merge(ya, yb, yc, yd, gates, x_ctx, x_lat, mods_l, w_br, w_out_bf,
                                   ln1_g, ln1_b, router_w_parts, router_b, l)
        moe_args = (h2, combine_t.T, x1, mods_l, moe_w_gate, moe_w_up, moe_w_down, ln2_g, ln2_b, l)
        if l + 1 < DEPTH:
            x_ctx, h = _moe(*moe_args, 0, T_CTX, mods_next=mods[l + 1])
            x_lat, h = _moe(*moe_args, T_CTX, T_LAT, mods_next=mods[l + 1], prev_h=h)
        else:
            x_ctx = _moe(*moe_args, 0, T_CTX)
            x_lat = _moe(*moe_args, T_CTX, T_LAT)

        def ctx_cols(col, width):
            return z[:T_CTX, col:col + width].astype(F32)

        new_k.append(ctx_cols(COL_WK, 128).reshape(BATCH, SEQ, WIN_KV_HEADS, WIN_HD))
        new_v.append(ctx_cols(COL_WV, 128).reshape(BATCH, SEQ, WIN_KV_HEADS, WIN_HD))
        new_ckv.append(ckvn[:T_CTX].reshape(BATCH, SEQ, MLA_KV_LORA))
        new_kr.append(ctx_cols(COL_KROPE, MLA_ROPE).reshape(BATCH, SEQ, MLA_ROPE))
        new_sf.append(sf[:BATCH])
        new_sb.append(sb[:BATCH])

    y_prompt = x_ctx.reshape(BATCH, SEQ, D)
    y_sample = x_lat.reshape(DEC_BATCH, DEC_SEQ, D)
    return (y_prompt, y_sample, jnp.stack(new_k, 1), jnp.stack(new_v, 1), jnp.stack(new_ckv, 1),
            jnp.stack(new_kr, 1), jnp.stack(new_sf, 1), jnp.stack(new_sb, 1))
```

```python
import functools
import math

import numpy as np
import jax
import jax.numpy as jnp
from jax import lax
from jax.experimental import pallas as pl
from jax.experimental.pallas import tpu as pltpu

F32 = jnp.float32
BF16 = jnp.bfloat16

D_MODEL = 1024
BATCH = 16
SEQ = 256
DEPTH = 2
DEC_BATCH = 2
DEC_SEQ = 1024
PAST_LEN = 256
GRID_W = 64
CHUNK = 128
ROPE_BASE = 10000.0
NEG = -1e30
LN_EPS = 1e-5
RMS_EPS = 1e-6
LOG2E = math.log2(math.e)

HY_W = 256
HY_BANDS = 16
HY_EMB = 1 + 2 * HY_BANDS
HY_FFN = 64
HY_FAST_DECAY = 0.3
HY_SLOW_DECAY = 1.5
HY_TARGET = 1e-2

WIN_HEADS = 4
WIN_KV_HEADS = 2
WIN_HD = 64
WINDOW = 128

RET_HEADS = 4
RET_DK = 64
RET_DV = 128

MLA_HEADS = 4
MLA_Q_LORA = 256
MLA_KV_LORA = 128
MLA_NOPE = 64
MLA_ROPE = 32
MLA_V = 64

N_EXPERTS = 16
N_GROUPS = 4
EXPERTS_PER_GROUP = N_EXPERTS // N_GROUPS
D_EXPERT = 256
ROUTE_SCALE = 2.5

ALPHA = (2.0 * DEPTH) ** 0.25

T_CTX = BATCH * SEQ
T_LAT = DEC_BATCH * DEC_SEQ
T_ALL = T_CTX + T_LAT

COL_HY = 0
COL_WQ = 768
COL_WK = 1024
COL_WV = 1152
COL_RQ = 1280
COL_RK = 1536
COL_RV = 1792
COL_RG = 2304
COL_CQ = 2816
COL_CKV = 3072
COL_KROPE = 3200
COL_GATE = 3232
IN_COLS = COL_GATE + 4 * D_MODEL
Z_MAIN = 3328

LANE = 128
STEP_ROWS = 1024
CTX_SEQS_PER_STEP = STEP_ROWS // SEQ
CTX_STEPS = T_CTX // STEP_ROWS
MIXER_STEPS = T_ALL // STEP_ROWS
VMEM_LIMIT = 56 * 1024 * 1024


def _params(*sem):
    return pltpu.CompilerParams(dimension_semantics=sem, vmem_limit_bytes=VMEM_LIMIT)


def _dot(a, b):
    return jnp.dot(a.astype(BF16), b.astype(BF16), preferred_element_type=F32)


def _dot_split(a, b):
    a_hi = a.astype(BF16)
    a_lo = (a - a_hi.astype(F32)).astype(BF16)
    b_hi = b.astype(BF16)
    b_lo = (b - b_hi.astype(F32)).astype(BF16)

    def mm(x, y):
        return jnp.dot(x, y, preferred_element_type=F32)

    return mm(a_hi, b_hi) + (mm(a_lo, b_hi) + mm(a_hi, b_lo))


def _dot_nt(a, b):
    return lax.dot_general(a.astype(BF16), b.astype(BF16), (((1,), (1,)), ((), ())),
                           preferred_element_type=F32)


def _dot_tn(a, b):
    return lax.dot_general(a.astype(BF16), b.astype(BF16), (((0,), (0,)), ((), ())),
                           preferred_element_type=F32)


def _layer_norm(x):
    mu = jnp.mean(x, -1, keepdims=True)
    xc = x - mu
    var = jnp.mean(xc * xc, -1, keepdims=True)
    return xc * lax.rsqrt(var + LN_EPS)


def _mod_row(tile_rows):
    def row(i):
        start = i * tile_rows
        return jnp.where(start < T_CTX, 0, 1 + (start - T_CTX) // DEC_SEQ)
    return row


@functools.lru_cache(maxsize=None)
def _dft_tables(L):
    f = np.arange(L, dtype=np.int64)[:, None]
    s = np.arange(L, dtype=np.int64)[None, :]
    ang = np.pi * ((f * s) % (2 * L)).astype(np.float64) / L
    cos = np.cos(ang)
    sin = np.sin(ang)
    alt = np.where(np.arange(L) % 2 == 0, 1.0, -1.0)
    fwd_im = -sin
    fwd_im[0, :] = alt
    fwd = np.concatenate([cos, fwd_im], 0)
    inv_re = cos.T / L
    inv_re[:, 0] = 1.0 / (2 * L)
    inv_im = -sin.T / L
    inv_im[:, 0] = alt / (2 * L)
    inv = np.concatenate([inv_re, inv_im], 1)
    return fwd.astype(np.float32), inv.astype(np.float32)


@functools.lru_cache(maxsize=None)
def _hyena_embedding(L):
    t01 = np.linspace(0.0, 1.0, L, dtype=np.float64)[:, None]
    bands = np.linspace(1e-4, HY_BANDS - 1, HY_BANDS, dtype=np.float64)
    ang = (2.0 * math.pi / L) * np.arange(L, dtype=np.float64)[:, None] * bands[None, :]
    z = np.concatenate([t01, np.cos(ang), -np.sin(ang)], -1)
    zp = np.zeros((L, LANE), np.float64)
    zp[:, :HY_EMB] = z
    deltas = np.abs(np.linspace(math.log(HY_TARGET) / HY_SLOW_DECAY,
                                math.log(HY_TARGET) / HY_FAST_DECAY, HY_W, dtype=np.float64))
    return zp.astype(np.float32), deltas[None, :].astype(np.float32)


@functools.lru_cache(maxsize=None)
def _rope_tables(L, rot_dim, width):
    rows = L // GRID_W
    n_freq = rot_dim // 4
    half = rot_dim // 2
    inv = ROPE_BASE ** (-np.arange(n_freq, dtype=np.float64) / n_freq)
    pos = np.arange(L)
    row = (pos // GRID_W).astype(np.float64)
    col = (pos % GRID_W).astype(np.float64)
    ang = np.concatenate([row[:, None] * inv, col[:, None] * inv], -1)
    cos, sin = np.cos(ang), np.sin(ang)
    zero = np.zeros_like(sin)
    c = np.tile(np.concatenate([cos, cos], -1), (1, width // rot_dim))
    s_up = np.tile(np.concatenate([-sin, zero], -1), (1, width // rot_dim))
    s_dn = np.tile(np.concatenate([zero, sin], -1), (1, width // rot_dim))
    return c.astype(np.float32), s_up.astype(np.float32), s_dn.astype(np.float32), half


def _rope128(x, c, s_up, s_dn, half):
    up = pltpu.roll(x, LANE - half, axis=1)
    dn = pltpu.roll(x, half, axis=1)
    return x * c + up * s_up + dn * s_dn


def _ada_kernel(c_ref, w_ref, b_ref, o_ref):
    cv = c_ref[...]
    s = cv * jax.nn.sigmoid(cv)
    s_hi = s.astype(BF16)
    s_lo = (s - s_hi.astype(F32)).astype(BF16)
    w = w_ref[...]
    w_hi = w.astype(BF16)
    w_lo = (w - w_hi.astype(F32)).astype(BF16)
    rows = s.shape[0]
    both = jnp.dot(jnp.concatenate([s_hi, s_lo], 0), w_hi, preferred_element_type=F32)
    o_ref[...] = (both[0:rows] + (both[rows:2 * rows] + jnp.dot(s_hi, w_lo, preferred_element_type=F32))
                  + b_ref[...])


ADA_ROWS = 16


def _ada_mods(cvec, w_ada, b_ada):
    tn = 1536
    n = 6 * D_MODEL
    return pl.pallas_call(
        _ada_kernel,
        out_shape=jax.ShapeDtypeStruct((DEPTH, ADA_ROWS, n), F32),
        grid=(DEPTH, n // tn),
        in_specs=[
            pl.BlockSpec((ADA_ROWS, D_MODEL), lambda l, j: (0, 0)),
            pl.BlockSpec((None, D_MODEL, tn), lambda l, j: (l, 0, j)),
            pl.BlockSpec((None, 1, tn), lambda l, j: (l, 0, j)),
        ],
        out_specs=pl.BlockSpec((None, ADA_ROWS, tn), lambda l, j: (l, 0, j)),
        compiler_params=_params("parallel", "parallel"),
        name="ada_mods",
    )(cvec, w_ada, b_ada.reshape(DEPTH, 1, n))


def _lnmod_kernel(*refs):
    x_ref, m_ref, h_ref = refs[-3:]
    y = _layer_norm(x_ref[...])
    s1 = m_ref[:, 0:D_MODEL]
    sc1 = m_ref[:, D_MODEL:2 * D_MODEL]
    h_ref[...] = (y * (1.0 + sc1) + s1).astype(h_ref.dtype)


def _ln_mod(x_group, prev_out, mods_l, row0):
    tm = 512
    row = _mod_row(tm)
    tile0 = row0 // tm
    in_specs = [
        pl.BlockSpec((tm, D_MODEL), lambda i: (i, 0)),
        pl.BlockSpec((None, 1, 6 * D_MODEL), lambda i: (row(tile0 + i), 0, 0)),
    ]
    args = [x_group, mods_l]
    if prev_out is not None:
        in_specs = [pl.BlockSpec(memory_space=pl.ANY)] + in_specs
        args = [prev_out] + args
    return pl.pallas_call(
        _lnmod_kernel,
        out_shape=jax.ShapeDtypeStruct((T_ALL, D_MODEL), BF16),
        grid=(x_group.shape[0] // tm,),
        in_specs=in_specs,
        out_specs=pl.BlockSpec((tm, D_MODEL), lambda i: (tile0 + i, 0)),
        input_output_aliases={} if prev_out is None else {0: 0},
        compiler_params=_params("parallel"),
        name="ln_mod",
    )(*args)


def _proj_kernel(h_ref, w_ref, o_ref, wb_ref, *, gate):
    @pl.when(pl.program_id(1) == 0)
    def _():
        wb_ref[...] = w_ref[...].T.astype(BF16)

    if not gate:
        o_ref[...] = jnp.dot(h_ref[...], wb_ref[...], preferred_element_type=F32).astype(o_ref.dtype)
        return
    sub = 2 * LANE
    for c0 in range(0, o_ref.shape[1], sub):
        r = jnp.dot(h_ref[...], wb_ref[:, c0:c0 + sub], preferred_element_type=F32)
        o_ref[:, c0:c0 + sub] = (0.5 * jnp.tanh(0.5 * r) + 0.5).astype(o_ref.dtype)


def _in_proj(h, w_t, layer, col0, n_cols, tn, out_dtype, gate):
    tm = 2048
    return pl.pallas_call(
        functools.partial(_proj_kernel, gate=gate),
        out_shape=jax.ShapeDtypeStruct((T_ALL, n_cols), out_dtype),
        grid=(n_cols // tn, T_ALL // tm),
        in_specs=[
            pl.BlockSpec((tm, D_MODEL), lambda j, i: (i, 0)),
            pl.BlockSpec((pl.Element(tn), pl.Element(D_MODEL)),
                         lambda j, i: (pl.multiple_of(layer * IN_COLS + col0 + j * tn, 8), 0)),
        ],
        out_specs=pl.BlockSpec((tm, tn), lambda j, i: (i, j)),
        scratch_shapes=[pltpu.VMEM((D_MODEL, tn), BF16)],
        compiler_params=_params("parallel", "arbitrary"),
        name="gate_proj" if gate else "in_proj",
    )(h, w_t)


def _hy_filter_kernel(z_ref, dl_ref, w1_ref, b1_ref, w2_ref, b2_ref, w3_ref, fwd_ref,
                      kre_ref, kim_ref, *, L):
    z = z_ref[...]
    a = jnp.sin(_dot_split(z, w1_ref[...]) + b1_ref[...])
    a = jnp.sin(_dot_split(a, w2_ref[...]) + b2_ref[...])
    h = _dot_split(a, w3_ref[...])
    decay = jnp.exp(-z[:, 0:1] * dl_ref[...])
    not_first = lax.broadcasted_iota(jnp.int32, (L, HY_W), 0) > 0
    sums, diffs = [], []
    for o in range(2):
        fw = h[:, (2 * o) * HY_W:(2 * o + 1) * HY_W] * decay
        bw = jnp.where(not_first, h[:, (2 * o + 1) * HY_W:(2 * o + 2) * HY_W] * decay, 0.0)
        sums.append(fw + bw)
        diffs.append(fw - bw)
    p = _dot(fwd_ref[...], jnp.concatenate(sums, 1))
    q = _dot(fwd_ref[L:2 * L, :], jnp.concatenate(diffs, 1))
    kre_ref[...] = p[0:L]
    first = lax.broadcasted_iota(jnp.int32, (L, 2 * HY_W), 0) == 0
    kim_ref[...] = jnp.where(first, p[L:L + 1], q)


def _hy_filters(L, w1p, b1, w2, b2, w3, fwd):
    zemb, deltas = _hyena_embedding(L)
    out = jax.ShapeDtypeStruct((L, 2 * HY_W), F32)
    return pl.pallas_call(
        functools.partial(_hy_filter_kernel, L=L),
        out_shape=(out, out),
        compiler_params=pltpu.CompilerParams(vmem_limit_bytes=VMEM_LIMIT),
        name=f"hy_filters_{L}",
    )(jnp.asarray(zemb), jnp.asarray(deltas), w1p, b1, w2, b2, w3, fwd)


def _group_step(ctx_body, lat_body):
    i = pl.program_id(0)
    pl.when(i < CTX_STEPS)(ctx_body)
    pl.when(i >= CTX_STEPS)(lat_body)


def _lat_index(i):
    return jnp.maximum(i - CTX_STEPS, 0)


def _hyena_kernel(hy_ref, cw_ref, cb_ref, bias_ref, kre_c, kim_c, fwd_c, inv_c,
                  kre_l, kim_l, fwd_l, inv_l, o_ref):
    _group_step(
        lambda: _hyena_body(hy_ref, cw_ref, cb_ref, bias_ref, kre_c, kim_c, fwd_c, inv_c, o_ref,
                            SEQ, CTX_SEQS_PER_STEP),
        lambda: _hyena_body(hy_ref, cw_ref, cb_ref, bias_ref, kre_l, kim_l, fwd_l, inv_l, o_ref,
                            DEC_SEQ, 1))


def _hyena_body(hy_ref, cw_ref, cb_ref, bias_ref, kre_ref, kim_ref, fwd_ref, inv_ref, o_ref, L, seqs):
    first = lax.broadcasted_iota(jnp.int32, (L, HY_W), 0) == 0

    def long_conv(u, o):
        uf = _dot(fwd_ref[...], u)
        ure, uim = uf[0:L], uf[L:2 * L]
        kre = kre_ref[:, o * HY_W:(o + 1) * HY_W]
        kim = kim_ref[:, o * HY_W:(o + 1) * HY_W]
        yre = jnp.where(first, ure * kre, ure * kre - uim * kim)
        yim = jnp.where(first, uim * kim, ure * kim + uim * kre)
        y = _dot(inv_ref[...], jnp.concatenate([yre, yim], 0))
        return y + u * bias_ref[o:o + 1, :]

    for g in range(seqs):
        sl = slice(g * L, (g + 1) * L)
        x = hy_ref[sl, :].astype(F32)
        rows = lax.broadcasted_iota(jnp.int32, x.shape, 0)
        prev = jnp.where(rows == 0, 0.0, pltpu.roll(x, 1, axis=0))
        nxt = jnp.where(rows == L - 1, 0.0, pltpu.roll(x, L - 1, axis=0))
        z = prev * cw_ref[0:1, :] + x * cw_ref[1:2, :] + nxt * cw_ref[2:3, :] + cb_ref[...]
        v, x1, x2 = z[:, 0:HY_W], z[:, HY_W:2 * HY_W], z[:, 2 * HY_W:3 * HY_W]
        u = x1 * long_conv(v, 0)
        o_ref[sl, :] = (x2 * long_conv(u, 1)).astype(o_ref.dtype)


def _const_spec(shape):
    return pl.BlockSpec(shape, lambda i: (0,) * len(shape))


def _hyena(z_main, layer, conv_w, conv_b, bias, filters, dft):
    tables, table_specs = [], []
    for L in (SEQ, DEC_SEQ):
        tables += [*filters[L], *dft[L]]
        table_specs += [_const_spec((L, 2 * HY_W)), _const_spec((L, 2 * HY_W)),
                        _const_spec((2 * L, L)), _const_spec((L, 2 * L))]
    return pl.pallas_call(
        _hyena_kernel,
        out_shape=jax.ShapeDtypeStruct((T_ALL, HY_W), BF16),
        grid=(MIXER_STEPS,),
        in_specs=[
            pl.BlockSpec((STEP_ROWS, 3 * HY_W), lambda i: (i, 0)),
            pl.BlockSpec((None, 3, 3 * HY_W), lambda i: (layer, 0, 0)),
            pl.BlockSpec((None, 1, 3 * HY_W), lambda i: (layer, 0, 0)),
            pl.BlockSpec((None, 2, HY_W), lambda i: (layer, 0, 0)),
        ] + table_specs,
        out_specs=pl.BlockSpec((STEP_ROWS, HY_W), lambda i: (i, 0)),
        compiler_params=_params("parallel"),
        name="hyena",
    )(z_main, conv_w, conv_b.reshape(DEPTH, 1, 3 * HY_W), bias, *tables)


def _win_masks():
    lane = lax.broadcasted_iota(jnp.int32, (1, LANE), 1)
    return lane < WIN_HD, lane >= WIN_HD


def _win_head_operands(q, k, v, h):
    lo_mask, hi_mask = _win_masks()
    col = h // 2
    lo = h % 2 == 0
    q128 = jnp.where(lo_mask if lo else hi_mask, q[:, col * LANE:(col + 1) * LANE], 0.0)
    swap = h in (1, 2)
    if swap:
        k = pltpu.roll(k, WIN_HD, axis=1)
        v = pltpu.roll(v, WIN_HD, axis=1)
    return q128, k, v, lo


def _win_kernel(sink_ref, q_ref, kv_ref, ck_ref, cv_ref, c_ref, su_ref, sd_ref, o_ref, *, layer):
    _group_step(
        lambda: _win_ctx_body(sink_ref, q_ref, kv_ref, o_ref, layer),
        lambda: _win_lat_body(sink_ref, q_ref, kv_ref, ck_ref, cv_ref, c_ref, su_ref, sd_ref, o_ref, layer))


def _win_ctx_body(sink_ref, q_ref, kv_ref, o_ref, layer):
    lo_mask, hi_mask = _win_masks()
    qscale = WIN_HD ** -0.5 * LOG2E
    for g in range(CTX_SEQS_PER_STEP):
        sl = slice(g * SEQ, (g + 1) * SEQ)
        q = q_ref[sl, :].astype(F32) * qscale
        k = kv_ref[sl, 0:LANE].astype(F32)
        v = kv_ref[sl, LANE:2 * LANE].astype(F32)
        cols = []
        for col in range(2):
            acc = None
            for h in (2 * col, 2 * col + 1):
                q128, kk, vv, lo = _win_head_operands(q, k, v, h)
                s = _dot_nt(q128, kk)
                sink = sink_ref[layer, h] * LOG2E
                m = jnp.maximum(jnp.max(s, -1, keepdims=True), sink)
                p = jnp.exp2(s - m)
                den = jnp.sum(p, -1, keepdims=True) + jnp.exp2(sink - m)
                o = _dot(p, vv) / den
                o = jnp.where(lo_mask if lo else hi_mask, o, 0.0)
                acc = o if acc is None else acc + o
            cols.append(acc)
        o_ref[sl, :] = jnp.concatenate(cols, 1).astype(o_ref.dtype)


def _win_lat_body(sink_ref, q_ref, kv_ref, ck_ref, cv_ref, c_ref, su_ref, sd_ref, o_ref, layer):
    L = DEC_SEQ
    half = WIN_HD // 2
    c, su, sd = c_ref[...], su_ref[...], sd_ref[...]
    qscale = WIN_HD ** -0.5 * LOG2E
    q = jnp.concatenate(
        [_rope128(q_ref[:, i * LANE:(i + 1) * LANE].astype(F32), c, su, sd, half) for i in range(2)],
        1) * qscale
    k = _rope128(kv_ref[:, 0:LANE].astype(F32), c, su, sd, half)
    v = kv_ref[:, LANE:2 * LANE].astype(F32)
    ck = ck_ref[...]
    cv = cv_ref[...]
    lo_mask, hi_mask = _win_masks()
    nb = L // CHUNK
    assert WINDOW == CHUNK
    rr = lax.broadcasted_iota(jnp.int32, (CHUNK, CHUNK), 0)
    cc = lax.broadcasted_iota(jnp.int32, (CHUNK, CHUNK), 1)
    band = {-1: jnp.where(cc >= rr, 0.0, NEG), 0: jnp.zeros((CHUNK, CHUNK), F32),
            1: jnp.where(cc <= rr, 0.0, NEG)}
    cols = []
    for col in range(2):
        acc_blocks = [None] * nb
        for h in (2 * col, 2 * col + 1):
            q128, kk, vv, lo = _win_head_operands(q, k, v, h)
            _, ckk, cvv, _ = _win_head_operands(q, ck, cv, h)
            sink = sink_ref[layer, h] * LOG2E
            for n in range(nb):
                blocks = [d for d in (-1, 0, 1) if 0 <= n + d < nb]
                k0 = (n + blocks[0]) * CHUNK
                k1 = (n + blocks[-1] + 1) * CHUNK
                qn = q128[n * CHUNK:(n + 1) * CHUNK]
                s_loc = _dot_nt(qn, kk[k0:k1]) + jnp.concatenate([band[d] for d in blocks], 1)
                s_ctx = _dot_nt(qn, ckk)
                m = jnp.maximum(jnp.maximum(jnp.max(s_loc, -1, keepdims=True),
                                            jnp.max(s_ctx, -1, keepdims=True)), sink)
                p_loc = jnp.exp2(s_loc - m)
                p_ctx = jnp.exp2(s_ctx - m)
                den = (jnp.sum(p_loc, -1, keepdims=True) + jnp.sum(p_ctx, -1, keepdims=True)
                       + jnp.exp2(sink - m))
                o = (_dot(p_loc, vv[k0:k1]) + _dot(p_ctx, cvv)) / den
                o = jnp.where(lo_mask if lo else hi_mask, o, 0.0)
                acc_blocks[n] = o if acc_blocks[n] is None else acc_blocks[n] + o
        cols.append(jnp.concatenate(acc_blocks, 0))
    o_ref[...] = jnp.concatenate(cols, 1).astype(o_ref.dtype)


def _win(z_main, sink, cache_k, cache_v, layer):
    c, su, sd, _ = _rope_tables(DEC_SEQ, WIN_HD, LANE)
    tab = _const_spec((DEC_SEQ, LANE))
    cache = pl.BlockSpec((None, None, PAST_LEN, LANE), lambda i: (_lat_index(i), layer, 0, 0))
    return pl.pallas_call(
        functools.partial(_win_kernel, layer=layer),
        out_shape=jax.ShapeDtypeStruct((T_ALL, WIN_HEADS * WIN_HD), BF16),
        grid=(MIXER_STEPS,),
        in_specs=[
            pl.BlockSpec(memory_space=pltpu.SMEM),
            pl.BlockSpec((STEP_ROWS, 256), lambda i: (i, COL_WQ // 256)),
            pl.BlockSpec((STEP_ROWS, 256), lambda i: (i, COL_WK // 256)),
            cache, cache, tab, tab, tab,
        ],
        out_specs=pl.BlockSpec((STEP_ROWS, 256), lambda i: (i, 0)),
        compiler_params=_params("parallel"),
        name="win",
    )(sink, z_main, z_main, cache_k, cache_v, jnp.asarray(c), jnp.asarray(su), jnp.asarray(sd))


def _ret_kernel(*refs, layer):
    _group_step(lambda: _ret_body(*refs, L=SEQ, layer=layer, ctx=True),
                lambda: _ret_body(*refs, L=DEC_SEQ, layer=layer, ctx=False))


def _ret_body(df_ref, db_ref, q_ref, k_ref, v0_ref, v1_ref, g0_ref, g1_ref, s0f_ref, s0b_ref,
              o_ref, sf_out, sb_out, s_ref, cross_ref, *, L, layer, ctx):
    seqs = STEP_ROWS // L
    if not ctx:
        sf_out[...] = jnp.zeros_like(sf_out)
        sb_out[...] = jnp.zeros_like(sb_out)
    C = CHUNK
    nc = L // C
    H = RET_HEADS
    qw = H * RET_DK
    vw = H * RET_DV

    def lane_table(width, per_head, fn):
        pos = lax.broadcasted_iota(jnp.int32, (C, per_head), 0).astype(F32)
        return jnp.concatenate([fn(h, pos) for h in range(H)], 1)

    def log_gamma(ref, h):
        d = jnp.full((1, 1), ref[layer, h], F32)
        return jnp.log(jax.nn.sigmoid(d))

    lgf = [log_gamma(df_ref, h) for h in range(H)]
    lgb = [log_gamma(db_ref, h) for h in range(H)]

    def tables(lg, reverse):
        if reverse:
            dq = lane_table(vw, RET_DV, lambda h, pos: jnp.exp((C - pos) * lg[h]))
            dk = lane_table(qw, RET_DK, lambda h, pos: jnp.exp(pos * lg[h]))
        else:
            dq = lane_table(vw, RET_DV, lambda h, pos: jnp.exp((pos + 1.0) * lg[h]))
            dk = lane_table(qw, RET_DK, lambda h, pos: jnp.exp((C - 1.0 - pos) * lg[h]))
        dc = jnp.concatenate([jnp.broadcast_to(jnp.exp(C * lg[h]), (1, RET_DV)) for h in range(H)], 1)
        return dq, dk, dc

    tab_f = tables(lgf, False)
    tab_b = tables(lgb, True)
    ii = lax.broadcasted_iota(jnp.int32, (C, C), 0)
    jj = lax.broadcasted_iota(jnp.int32, (C, C), 1)
    diff = (ii - jj).astype(F32)
    dmats = [jnp.where(diff >= 0, jnp.exp(jnp.maximum(diff, 0.0) * lgf[h]), 0.0)
             + jnp.where(diff <= 0, jnp.exp(jnp.maximum(-diff, 0.0) * lgb[h]), 0.0) for h in range(H)]
    lane_q = lax.broadcasted_iota(jnp.int32, (1, qw), 1) // RET_DK

    srow = lax.broadcasted_iota(jnp.int32, (qw, vw), 0) // RET_DK
    scol = lax.broadcasted_iota(jnp.int32, (qw, vw), 1) // RET_DV
    diag = srow == scol

    for g in range(seqs):
        base = g * L
        rows_all = slice(base, base + L)
        q_all = q_ref[rows_all, :].astype(F32)
        k_all = k_ref[rows_all, :].astype(F32) * (RET_DK ** -0.5)
        v_all = jnp.concatenate([v0_ref[rows_all, :], v1_ref[rows_all, :]], 1).astype(F32)
        g_all = jnp.concatenate([g0_ref[rows_all, :], g1_ref[rows_all, :]], 1).astype(F32)

        def scan(tabs, reverse, s0_ref, s_out):
            dq, dk, dc = tabs
            s_ref[g] = jnp.zeros((qw, vw), F32)
            if s0_ref is not None:
                for h in range(H):
                    s_ref[g, h * RET_DK:(h + 1) * RET_DK, h * RET_DV:(h + 1) * RET_DV] = s0_ref[h]
            order = range(nc - 1, -1, -1) if reverse else range(nc)
            for ci in order:
                sl = slice(ci * C, (ci + 1) * C)
                rs = slice(base + ci * C, base + (ci + 1) * C)
                qc, kc, vc = q_all[sl], k_all[sl], v_all[sl]
                st = s_ref[g]
                cross = _dot(qc, st) * dq
                if reverse:
                    cross_ref[rs, :] = cross_ref[rs, :] + cross
                else:
                    cross_ref[rs, :] = cross
                upd = jnp.where(diag, _dot_tn(kc * dk, vc), 0.0)
                s_ref[g] = st * dc + upd
            if s_out is not None:
                for h in range(H):
                    s_out[g, h] = s_ref[g, h * RET_DK:(h + 1) * RET_DK, h * RET_DV:(h + 1) * RET_DV]

        scan(tab_f, False, None if ctx else s0f_ref, sf_out if ctx else None)
        scan(tab_b, True, None if ctx else s0b_ref, sb_out if ctx else None)

        for h in range(H):
            hv = slice(h * RET_DV, (h + 1) * RET_DV)
            for ci in range(nc):
                sl = slice(ci * C, (ci + 1) * C)
                qh = jnp.where(lane_q == h, q_all[sl], 0.0)
                att = _dot_nt(qh, k_all[sl]) * dmats[h]
                o = _dot(att, v_all[sl, hv]) + cross_ref[base + ci * C:base + (ci + 1) * C, hv]
                gt = g_all[sl, hv]
                o_ref[base + ci * C:base + (ci + 1) * C, hv] = (
                    (gt * jax.nn.sigmoid(gt)) * _layer_norm(o)).astype(o_ref.dtype)


def _retention(z_main, dec_f, dec_b, s0f, s0b, layer):
    def zcol(col):
        return pl.BlockSpec((STEP_ROWS, 256), lambda i: (i, col // 256))

    smem = pl.BlockSpec(memory_space=pltpu.SMEM)
    z_specs = [zcol(COL_RQ), zcol(COL_RK), zcol(COL_RV), zcol(COL_RV + 256),
               zcol(COL_RG), zcol(COL_RG + 256)]
    s0_spec = pl.BlockSpec((None, None, RET_HEADS, RET_DK, RET_DV),
                           lambda i: (_lat_index(i), layer, 0, 0, 0))
    st_shape = jax.ShapeDtypeStruct((MIXER_STEPS * CTX_SEQS_PER_STEP, RET_HEADS, RET_DK, RET_DV), F32)
    st_spec = pl.BlockSpec((CTX_SEQS_PER_STEP, RET_HEADS, RET_DK, RET_DV), lambda i: (i, 0, 0, 0))
    return pl.pallas_call(
        functools.partial(_ret_kernel, layer=layer),
        out_shape=(jax.ShapeDtypeStruct((T_ALL, RET_HEADS * RET_DV), BF16), st_shape, st_shape),
        grid=(MIXER_STEPS,),
        in_specs=[smem, smem] + z_specs + [s0_spec, s0_spec],
        out_specs=(pl.BlockSpec((STEP_ROWS, RET_HEADS * RET_DV), lambda i: (i, 0)), st_spec, st_spec),
        scratch_shapes=[pltpu.VMEM((CTX_SEQS_PER_STEP, RET_HEADS * RET_DK, RET_HEADS * RET_DV), F32),
                        pltpu.VMEM((STEP_ROWS, RET_HEADS * RET_DV), F32)],
        compiler_params=_params("parallel"),
        name="retention",
    )(dec_f, dec_b, *([z_main] * 6), s0f, s0b)


def _rms_norm(x, g):
    return x * lax.rsqrt(jnp.mean(x * x, -1, keepdims=True) + RMS_EPS) * g


def _mla_keys(kn, kr):
    lane_r = lax.broadcasted_iota(jnp.int32, (1, LANE), 1)
    return jnp.concatenate([kn, jnp.where(lane_r < MLA_ROPE, kr, 0.0)], 1).astype(BF16)


def _mla_attend(qn, qr, k_cat, vv, o_ref, row0):
    qscale = (MLA_NOPE + MLA_ROPE) ** -0.5 * LOG2E
    lane_n = lax.broadcasted_iota(jnp.int32, (1, MLA_HEADS * MLA_NOPE), 1) // MLA_NOPE
    lane_r = lax.broadcasted_iota(jnp.int32, (1, LANE), 1)
    qn = qn * qscale
    qr = qr * qscale
    acc = None
    for h in range(MLA_HEADS):
        qnh = jnp.where(lane_n == h, qn, 0.0)
        qrh = qr if h == 0 else pltpu.roll(qr, LANE - h * MLA_ROPE, axis=1)
        qrh = jnp.where(lane_r < MLA_ROPE, qrh, 0.0)
        s = _dot_nt(jnp.concatenate([qnh, qrh], 1), k_cat)
        m = jnp.max(s, -1, keepdims=True)
        p = jnp.exp2(s - m)
        den = jnp.sum(p, -1, keepdims=True)
        o = jnp.where(lane_n == h, _dot(p, vv) / den, 0.0)
        acc = o if acc is None else acc + o
    o_ref[row0:row0 + acc.shape[0], :] = acc.astype(o_ref.dtype)


def _mla_kernel(cq_ref, ckv_ref, kr_ref, cckv_ref, ckr_ref, c_ref, su_ref, sd_ref,
                qg_ref, kg_ref, wqn_ref, wqr_ref, wk_ref, wv_ref, o_ref, ckvn_ref):
    weights = (qg_ref, kg_ref, wqn_ref, wqr_ref, wk_ref, wv_ref)
    _group_step(
        lambda: _mla_ctx_body(cq_ref, ckv_ref, kr_ref, *weights, o_ref, ckvn_ref),
        lambda: _mla_lat_body(cq_ref, ckv_ref, kr_ref, cckv_ref, ckr_ref, c_ref, su_ref, sd_ref,
                              *weights, o_ref, ckvn_ref))


def _mla_ctx_body(cq_ref, ckv_ref, kr_ref, qg_ref, kg_ref, wqn_ref, wqr_ref, wk_ref, wv_ref,
                  o_ref, ckvn_ref):
    cqn = _rms_norm(cq_ref[...].astype(F32), qg_ref[...])
    qn = _dot(cqn, wqn_ref[...])
    qr = _dot(cqn, wqr_ref[...])
    ckvn = _rms_norm(ckv_ref[...].astype(F32), kg_ref[...])
    ckvn_ref[...] = ckvn
    k_cat = _mla_keys(_dot(ckvn, wk_ref[...]), kr_ref[...].astype(F32))
    vv = _dot(ckvn, wv_ref[...]).astype(BF16)
    for g in range(CTX_SEQS_PER_STEP):
        sl = slice(g * SEQ, (g + 1) * SEQ)
        _mla_attend(qn[sl], qr[sl], k_cat[sl], vv[sl], o_ref, g * SEQ)


def _mla_lat_body(cq_ref, ckv_ref, kr_ref, cckv_ref, ckr_ref, c_ref, su_ref, sd_ref,
                  qg_ref, kg_ref, wqn_ref, wqr_ref, wk_ref, wv_ref, o_ref, ckvn_ref):
    half = MLA_ROPE // 2
    c, su, sd = c_ref[...], su_ref[...], sd_ref[...]
    cqn = _rms_norm(cq_ref[...].astype(F32), qg_ref[...])
    qn = _dot(cqn, wqn_ref[...])
    qr = _rope128(_dot(cqn, wqr_ref[...]), c, su, sd, half)
    ckvn = _rms_norm(ckv_ref[...].astype(F32), kg_ref[...])
    ckvn_ref[...] = ckvn
    ckv_all = jnp.concatenate([ckvn, cckv_ref[...]], 0)
    vv = _dot(ckv_all, wv_ref[...]).astype(BF16)
    kr = jnp.concatenate([_rope128(kr_ref[...].astype(F32), c, su, sd, half), ckr_ref[...]], 0)
    k_cat = _mla_keys(_dot(ckv_all, wk_ref[...]), kr)
    rows_per_call = 1024
    for n in range(DEC_SEQ // rows_per_call):
        rows = slice(n * rows_per_call, (n + 1) * rows_per_call)
        _mla_attend(qn[rows], qr[rows], k_cat, vv, o_ref, n * rows_per_call)


def _mla(z_main, cache_ckv, cache_kr_pad, weights, layer):
    c, su, sd, _ = _rope_tables(DEC_SEQ, MLA_ROPE, LANE)
    tab = _const_spec((DEC_SEQ, LANE))
    cache = pl.BlockSpec((None, None, PAST_LEN, LANE), lambda i: (_lat_index(i), layer, 0, 0))

    def weight(*shape):
        return pl.BlockSpec((None,) + shape, lambda i: (layer, 0, 0))

    return pl.pallas_call(
        _mla_kernel,
        out_shape=(jax.ShapeDtypeStruct((T_ALL, MLA_HEADS * MLA_V), BF16),
                   jax.ShapeDtypeStruct((T_ALL, MLA_KV_LORA), F32)),
        grid=(MIXER_STEPS,),
        in_specs=[
            pl.BlockSpec((STEP_ROWS, 256), lambda i: (i, COL_CQ // 256)),
            pl.BlockSpec((STEP_ROWS, LANE), lambda i: (i, COL_CKV // LANE)),
            pl.BlockSpec((STEP_ROWS, LANE), lambda i: (i, COL_KROPE // LANE)),
            cache, cache, tab, tab, tab,
            weight(1, MLA_Q_LORA), weight(1, MLA_KV_LORA),
            weight(MLA_Q_LORA, MLA_HEADS * MLA_NOPE), weight(MLA_Q_LORA, MLA_HEADS * MLA_ROPE),
            weight(MLA_KV_LORA, MLA_HEADS * MLA_NOPE), weight(MLA_KV_LORA, MLA_HEADS * MLA_V),
        ],
        out_specs=(pl.BlockSpec((STEP_ROWS, 256), lambda i: (i, 0)),
                   pl.BlockSpec((STEP_ROWS, MLA_KV_LORA), lambda i: (i, 0))),
        compiler_params=_params("parallel"),
        name="mla",
    )(z_main, z_main, z_main, cache_ckv, cache_kr_pad,
      jnp.asarray(c), jnp.asarray(su), jnp.asarray(sd), *weights)


def _route(logits_t, rb):
    scores = jax.nn.sigmoid(logits_t)
    biased = scores + rb
    sc = [scores[e:e + 1, :] for e in range(N_EXPERTS)]
    bi = [biased[e:e + 1, :] for e in range(N_EXPERTS)]
    epg = EXPERTS_PER_GROUP
    gsum = []
    for g in range(N_GROUPS):
        v = bi[g * epg:(g + 1) * epg]
        best = None
        for a in range(epg):
            for b in range(a + 1, epg):
                pair = v[a] + v[b]
                best = pair if best is None else jnp.maximum(best, pair)
        gsum.append(best)
    combine = []
    sel = []
    for g in range(N_GROUPS):
        is_best = None
        for g2 in range(N_GROUPS):
            if g2 == g:
                continue
            c = gsum[g] > gsum[g2] if g2 < g else gsum[g] >= gsum[g2]
            is_best = c if is_best is None else jnp.logical_and(is_best, c)
        for a in range(epg):
            e = g * epg + a
            rank = jnp.zeros_like(bi[e])
            for b in range(epg):
                if b == a:
                    continue
                e2 = g * epg + b
                ahead = bi[e2] >= bi[e] if b < a else bi[e2] > bi[e]
                rank = rank + jnp.where(ahead, 1.0, 0.0)
            sel.append(jnp.logical_and(is_best, rank < 2.0))
    wsum = None
    for e in range(N_EXPERTS):
        w = jnp.where(sel[e], sc[e], 0.0)
        wsum = w if wsum is None else wsum + w
    for e in range(N_EXPERTS):
        combine.append(jnp.where(sel[e], ROUTE_SCALE * sc[e] / wsum, 0.0))
    return jnp.concatenate(combine, 0)


def _merge_kernel(ya_ref, yb_ref, yc_ref, yd_ref, gt_ref, xc_ref, xl_ref, m_ref,
                  wa_ref, wb_ref, wc_ref, wd_ref, wo_ref, g_ref, b_ref, rwh_ref, rwl_ref, rb_ref,
                  x1_ref, h2_ref, cmb_ref, *, ctx_tiles, sub_rows):
    D = D_MODEL
    g1 = m_ref[:, 2 * D:3 * D]
    s2 = m_ref[:, 3 * D:4 * D]
    sc2 = m_ref[:, 4 * D:5 * D]
    is_ctx = pl.program_id(0) < ctx_tiles
    branches = ((ya_ref, wa_ref), (yb_ref, wb_ref), (yc_ref, wc_ref), (yd_ref, wd_ref))
    for r0 in range(0, x1_ref.shape[0], sub_rows):
        rows = slice(r0, r0 + sub_rows)
        merged = None
        for i, (y_ref, w) in enumerate(branches):
            t = gt_ref[rows, i * D:(i + 1) * D].astype(F32) * jnp.dot(
                y_ref[rows, :], w[...], preferred_element_type=F32)
            merged = t if merged is None else merged + t
        out1 = jnp.dot(merged.astype(BF16), wo_ref[...], preferred_element_type=F32)
        x = jnp.where(is_ctx, xc_ref[rows, :], xl_ref[rows, :])
        x1 = _layer_norm(ALPHA * x + g1 * out1) * g_ref[...] + b_ref[...]
        x1_ref[rows, :] = x1
        h2 = _layer_norm(x1) * (1.0 + sc2) + s2
        h2_hi = h2.astype(BF16)
        h2_ref[rows, :] = h2_hi
        h2_lo = (h2 - h2_hi.astype(F32)).astype(BF16)
        logits = (jnp.dot(h2_hi, rwh_ref[...], preferred_element_type=F32)
                  + (jnp.dot(h2_lo, rwh_ref[...], preferred_element_type=F32)
                     + jnp.dot(h2_hi, rwl_ref[...], preferred_element_type=F32)))
        cmb_ref[:, rows] = _route(logits.T[0:N_EXPERTS], rb_ref[...])


def _merge(ya, yb, yc, yd, gates, x_ctx, x_lat, mods_l, w_br, w_out_bf, ln1_g, ln1_b,
           router_w_parts, router_b, layer):
    tm = 512
    row = _mod_row(tm)
    D = D_MODEL
    ctx_tiles = T_CTX // tm

    def tile(w):
        return pl.BlockSpec((tm, w), lambda i: (i, 0))

    def weight(k, n):
        return pl.BlockSpec((None, k, n), lambda i: (layer, 0, 0))

    return pl.pallas_call(
        functools.partial(_merge_kernel, ctx_tiles=ctx_tiles, sub_rows=256),
        out_shape=(jax.ShapeDtypeStruct((T_ALL, D), F32),
                   jax.ShapeDtypeStruct((T_ALL, D), BF16),
                   jax.ShapeDtypeStruct((N_EXPERTS, T_ALL), F32)),
        grid=(T_ALL // tm,),
        in_specs=[
            tile(256), tile(256), tile(512), tile(256), tile(4 * D),
            pl.BlockSpec((tm, D), lambda i: (jnp.minimum(i, ctx_tiles - 1), 0)),
            pl.BlockSpec((tm, D), lambda i: (jnp.maximum(i - ctx_tiles, 0), 0)),
            pl.BlockSpec((None, 1, 6 * D), lambda i: (row(i), 0, 0)),
            weight(256, D), weight(256, D), weight(512, D), weight(256, D), weight(D, D),
            weight(1, D), weight(1, D),
            pl.BlockSpec((D, LANE), lambda i: (0, 0)),
            pl.BlockSpec((D, LANE), lambda i: (0, 0)),
            pl.BlockSpec((N_EXPERTS, 1), lambda i: (0, 0)),
        ],
        out_specs=(tile(D), tile(D), pl.BlockSpec((N_EXPERTS, tm), lambda i: (0, i))),
        compiler_params=_params("parallel"),
        name="merge",
    )(ya, yb, yc, yd, gates, x_ctx, x_lat, mods_l, *w_br, w_out_bf,
      ln1_g.reshape(DEPTH, 1, D), ln1_b.reshape(DEPTH, 1, D), *router_w_parts,
      router_b.reshape(N_EXPERTS, 1))


MOE_EXPERTS_PER_STEP = 2


def _moe_kernel(*refs, next_h):
    if next_h:
        (h_ref, c_ref, x1_ref, m_ref, wg_ref, wu_ref, wd_ref, g_ref, b_ref, mn_ref,
         o_ref, hn_ref, acc_ref) = refs[-13:]
    else:
        h_ref, c_ref, x1_ref, m_ref, wg_ref, wu_ref, wd_ref, g_ref, b_ref, o_ref, acc_ref = refs
    eg = pl.program_id(1)

    @pl.when(eg == 0)
    def _():
        acc_ref[...] = jnp.zeros_like(acc_ref)

    h = h_ref[...]
    cmb = c_ref[...]
    lane = lax.broadcasted_iota(jnp.int32, cmb.shape, 1)
    hid = []
    for k in range(MOE_EXPERTS_PER_STEP):
        gate = jnp.dot(h, wg_ref[k].astype(BF16), preferred_element_type=F32)
        up = jnp.dot(h, wu_ref[k].astype(BF16), preferred_element_type=F32)
        e = eg * MOE_EXPERTS_PER_STEP + k
        ce = jnp.sum(jnp.where(lane == e, cmb, 0.0), -1, keepdims=True)
        hid.append(((gate * jax.nn.sigmoid(gate)) * up * ce).astype(BF16))
    wd = wd_ref[...].reshape(MOE_EXPERTS_PER_STEP * D_EXPERT, D_MODEL).astype(BF16)
    acc_ref[...] += jnp.dot(jnp.concatenate(hid, 1), wd, preferred_element_type=F32)

    @pl.when(eg == N_EXPERTS // MOE_EXPERTS_PER_STEP - 1)
    def _():
        g2 = m_ref[:, 5 * D_MODEL:6 * D_MODEL]
        y = _layer_norm(ALPHA * x1_ref[...] + g2 * acc_ref[...])
        y = y * g_ref[...] + b_ref[...]
        o_ref[...] = y
        if next_h:
            s1 = mn_ref[:, 0:D_MODEL]
            sc1 = mn_ref[:, D_MODEL:2 * D_MODEL]
            hn_ref[...] = (_layer_norm(y) * (1.0 + sc1) + s1).astype(hn_ref.dtype)


def _moe(h2, combine, x1, mods_l, w_gate, w_up, w_down, ln2_g, ln2_b, layer, row0, n_rows,
         mods_next=None, prev_h=None):
    tm = 1024
    row = _mod_row(tm)
    D = D_MODEL
    t0 = row0 // tm
    eps = MOE_EXPERTS_PER_STEP
    next_h = mods_next is not None
    mod_spec = pl.BlockSpec((None, 1, 6 * D), lambda i, e: (row(t0 + i), 0, 0))
    in_specs = [
        pl.BlockSpec((tm, D), lambda i, e: (t0 + i, 0)),
        pl.BlockSpec((tm, N_EXPERTS), lambda i, e: (t0 + i, 0)),
        pl.BlockSpec((tm, D), lambda i, e: (t0 + i, 0)),
        mod_spec,
        pl.BlockSpec((None, eps, D, D_EXPERT), lambda i, e: (layer, e, 0, 0)),
        pl.BlockSpec((None, eps, D, D_EXPERT), lambda i, e: (layer, e, 0, 0)),
        pl.BlockSpec((None, eps, D_EXPERT, D), lambda i, e: (layer, e, 0, 0)),
        pl.BlockSpec((None, 1, D), lambda i, e: (layer, 0, 0)),
        pl.BlockSpec((None, 1, D), lambda i, e: (layer, 0, 0)),
    ]
    args = [h2, combine, x1, mods_l, w_gate, w_up, w_down,
            ln2_g.reshape(DEPTH, 1, D), ln2_b.reshape(DEPTH, 1, D)]
    out_shape = jax.ShapeDtypeStruct((n_rows, D), F32)
    out_specs = pl.BlockSpec((tm, D), lambda i, e: (i, 0))
    aliases = {}
    if next_h:
        in_specs.append(mod_spec)
        args.append(mods_next)
        out_shape = (out_shape, jax.ShapeDtypeStruct((T_ALL, D), BF16))
        out_specs = (out_specs, pl.BlockSpec((tm, D), lambda i, e: (t0 + i, 0)))
        if prev_h is not None:
            in_specs = [pl.BlockSpec(memory_space=pl.ANY)] + in_specs
            args = [prev_h] + args
            aliases = {0: 1}
    return pl.pallas_call(
        functools.partial(_moe_kernel, next_h=next_h),
        out_shape=out_shape,
        grid=(n_rows // tm, N_EXPERTS // eps),
        in_specs=in_specs,
        out_specs=out_specs,
        input_output_aliases=aliases,
        scratch_shapes=[pltpu.VMEM((tm, D), F32)],
        compiler_params=_params("parallel", "arbitrary"),
        name="moe",
    )(*args)


def kernel(x_prompt, x_sample, cache_win_k, cache_win_v, cache_mla_ckv, cache_mla_krope,
           state_ret_fwd, state_ret_bwd, c, c_ctx, w_ada, b_ada, w_in,
           hy_conv_w, hy_conv_b, hy_w1, hy_b1, hy_w2, hy_b2, hy_w3, hy_bias,
           win_sink, ret_decay_fwd, ret_decay_bwd, mla_q_norm, mla_kv_norm, mla_w_uq, mla_w_ukv,
           w_br_a, w_br_b, w_br_c, w_br_d, w_out, ln1_g, ln1_b, ln2_g, ln2_b,
           router_w, router_b, moe_w_gate, moe_w_up, moe_w_down):
    D = D_MODEL
    x_ctx = x_prompt.reshape(T_CTX, D)
    x_lat = x_sample.reshape(T_LAT, D)

    cvec = jnp.zeros((ADA_ROWS, D), F32).at[0].set(c_ctx).at[1:1 + DEC_BATCH].set(c)
    mods = _ada_mods(cvec, w_ada, b_ada)[:, :1 + DEC_BATCH].reshape(DEPTH, 1 + DEC_BATCH, 1, 6 * D)

    w_in_t = jnp.swapaxes(w_in, 1, 2).reshape(DEPTH * IN_COLS, D)
    cache_k = cache_win_k.reshape(DEC_BATCH, DEPTH, PAST_LEN, WIN_KV_HEADS * WIN_HD)
    cache_v = cache_win_v.reshape(DEC_BATCH, DEPTH, PAST_LEN, WIN_KV_HEADS * WIN_HD)
    cache_kr = jnp.pad(cache_mla_krope, ((0, 0), (0, 0), (0, 0), (0, LANE - MLA_ROPE)))

    uq = mla_w_uq.reshape(DEPTH, MLA_Q_LORA, MLA_HEADS, MLA_NOPE + MLA_ROPE)
    ukv = mla_w_ukv.reshape(DEPTH, MLA_KV_LORA, MLA_HEADS, MLA_NOPE + MLA_V)
    mla_weights = (
        mla_q_norm.reshape(DEPTH, 1, MLA_Q_LORA),
        mla_kv_norm.reshape(DEPTH, 1, MLA_KV_LORA),
        uq[..., :MLA_NOPE].reshape(DEPTH, MLA_Q_LORA, MLA_HEADS * MLA_NOPE),
        uq[..., MLA_NOPE:].reshape(DEPTH, MLA_Q_LORA, MLA_HEADS * MLA_ROPE),
        ukv[..., :MLA_NOPE].reshape(DEPTH, MLA_KV_LORA, MLA_HEADS * MLA_NOPE),
        ukv[..., MLA_NOPE:].reshape(DEPTH, MLA_KV_LORA, MLA_HEADS * MLA_V),
    )

    hy_w1p = jnp.pad(hy_w1, ((0, 0), (0, LANE - HY_EMB), (0, 0)))
    dft = {}
    for L in (SEQ, DEC_SEQ):
        fwd, inv = _dft_tables(L)
        dft[L] = (jnp.asarray(fwd).astype(BF16), jnp.asarray(inv).astype(BF16))
    router_w_pad = jnp.pad(router_w, ((0, 0), (0, LANE - N_EXPERTS)))
    router_w_hi = router_w_pad.astype(BF16)
    router_w_parts = (router_w_hi, (router_w_pad - router_w_hi.astype(F32)).astype(BF16))
    w_br = tuple(w.astype(BF16) for w in (w_br_a, w_br_b, w_br_c, w_br_d))
    w_out_bf = w_out.astype(BF16)

    new_k, new_v, new_ckv, new_kr, new_sf, new_sb = [], [], [], [], [], []
    for l in range(DEPTH):
        mods_l = mods[l]
        if l == 0:
            h = _ln_mod(x_ctx, None, mods_l, 0)
            h = _ln_mod(x_lat, h, mods_l, T_CTX)
        z = _in_proj(h, w_in_t, l, 0, Z_MAIN, Z_MAIN // 2, BF16, gate=False)
        gates = _in_proj(h, w_in_t, l, COL_GATE, 4 * D, D, BF16, gate=True)

        filters = {L: _hy_filters(L, hy_w1p[l], hy_b1[l][None], hy_w2[l], hy_b2[l][None], hy_w3[l],
                                  dft[L][0]) for L in (SEQ, DEC_SEQ)}
        ya = _hyena(z, l, hy_conv_w, hy_conv_b, hy_bias, filters, dft)
        yb = _win(z, win_sink, cache_k, cache_v, l)
        yc, sf, sb = _retention(z, ret_decay_fwd, ret_decay_bwd, state_ret_fwd, state_ret_bwd, l)
        yd, ckvn = _mla(z, cache_mla_ckv, cache_kr, mla_weights, l)

        x1, h2, combine_t = _merge(ya, yb, yc, yd, gates, x_ctx, x_lat, mods_l, w_br, w_out_bf,
                                   ln1_g, ln1_b, router_w_parts, router_b, l)
        moe_args = (h2, combine_t.T, x1, mods_l, moe_w_gate, moe_w_up, moe_w_down, ln2_g, ln2_b, l)
        if l + 1 < DEPTH:
            x_ctx, h = _moe(*moe_args, 0, T_CTX, mods_next=mods[l + 1])
            x_lat, h = _moe(*moe_args, T_CTX, T_LAT, mods_next=mods[l + 1], prev_h=h)
        else:
            x_ctx = _moe(*moe_args, 0, T_CTX)
            x_lat = _moe(*moe_args, T_CTX, T_LAT)

        def ctx_cols(col, width):
            return z[:T_CTX, col:col + width].astype(F32)

        new_k.append(ctx_cols(COL_WK, 128).reshape(BATCH, SEQ, WIN_KV_HEADS, WIN_HD))
        new_v.append(ctx_cols(COL_WV, 128).reshape(BATCH, SEQ, WIN_KV_HEADS, WIN_HD))
        new_ckv.append(ckvn[:T_CTX].reshape(BATCH, SEQ, MLA_KV_LORA))
        new_kr.append(ctx_cols(COL_KROPE, MLA_ROPE).reshape(BATCH, SEQ, MLA_ROPE))
        new_sf.append(sf[:BATCH])
        new_sb.append(sb[:BATCH])

    y_prompt = x_ctx.reshape(BATCH, SEQ, D)
    y_sample = x_lat.reshape(DEC_BATCH, DEC_SEQ, D)
    return (y_prompt, y_sample, jnp.stack(new_k, 1), jnp.stack(new_v, 1), jnp.stack(new_ckv, 1),
            jnp.stack(new_kr, 1), jnp.stack(new_sf, 1), jnp.stack(new_sb, 1))
```

```python
import functools
import math

import numpy as np
import jax
import jax.numpy as jnp
from jax import lax
from jax.experimental import pallas as pl
from jax.experimental.pallas import tpu as pltpu

F32 = jnp.float32
BF16 = jnp.bfloat16

D_MODEL = 1024
BATCH = 16
SEQ = 256
DEPTH = 2
DEC_BATCH = 2
DEC_SEQ = 1024
PAST_LEN = 256
GRID_W = 64
CHUNK = 128
ROPE_BASE = 10000.0
NEG = -1e30
LN_EPS = 1e-5
RMS_EPS = 1e-6
LOG2E = math.log2(math.e)

HY_W = 256
HY_BANDS = 16
HY_EMB = 1 + 2 * HY_BANDS
HY_FFN = 64
HY_FAST_DECAY = 0.3
HY_SLOW_DECAY = 1.5
HY_TARGET = 1e-2

WIN_HEADS = 4
WIN_KV_HEADS = 2
WIN_HD = 64
WINDOW = 128

RET_HEADS = 4
RET_DK = 64
RET_DV = 128

MLA_HEADS = 4
MLA_Q_LORA = 256
MLA_KV_LORA = 128
MLA_NOPE = 64
MLA_ROPE = 32
MLA_V = 64

N_EXPERTS = 16
N_GROUPS = 4
EXPERTS_PER_GROUP = N_EXPERTS // N_GROUPS
D_EXPERT = 256
ROUTE_SCALE = 2.5

ALPHA = (2.0 * DEPTH) ** 0.25

T_CTX = BATCH * SEQ
T_LAT = DEC_BATCH * DEC_SEQ
T_ALL = T_CTX + T_LAT

COL_HY = 0
COL_WQ = 768
COL_WK = 1024
COL_WV = 1152
COL_RQ = 1280
COL_RK = 1536
COL_RV = 1792
COL_RG = 2304
COL_CQ = 2816
COL_CKV = 3072
COL_KROPE = 3200
COL_GATE = 3232
IN_COLS = COL_GATE + 4 * D_MODEL
Z_MAIN = 3328

LANE = 128
STEP_ROWS = 1024
CTX_SEQS_PER_STEP = STEP_ROWS // SEQ
CTX_STEPS = T_CTX // STEP_ROWS
MIXER_STEPS = T_ALL // STEP_ROWS
VMEM_LIMIT = 56 * 1024 * 1024


def _params(*sem):
    return pltpu.CompilerParams(dimension_semantics=sem, vmem_limit_bytes=VMEM_LIMIT)


def _dot(a, b):
    return jnp.dot(a.astype(BF16), b.astype(BF16), preferred_element_type=F32)


def _dot_split(a, b):
    a_hi = a.astype(BF16)
    a_lo = (a - a_hi.astype(F32)).astype(BF16)
    b_hi = b.astype(BF16)
    b_lo = (b - b_hi.astype(F32)).astype(BF16)

    def mm(x, y):
        return jnp.dot(x, y, preferred_element_type=F32)

    return mm(a_hi, b_hi) + (mm(a_lo, b_hi) + mm(a_hi, b_lo))


def _dot_nt(a, b):
    return lax.dot_general(a.astype(BF16), b.astype(BF16), (((1,), (1,)), ((), ())),
                           preferred_element_type=F32)


def _dot_tn(a, b):
    return lax.dot_general(a.astype(BF16), b.astype(BF16), (((0,), (0,)), ((), ())),
                           preferred_element_type=F32)


def _layer_norm(x):
    mu = jnp.mean(x, -1, keepdims=True)
    xc = x - mu
    var = jnp.mean(xc * xc, -1, keepdims=True)
    return xc * lax.rsqrt(var + LN_EPS)


def _mod_row(tile_rows):
    def row(i):
        start = i * tile_rows
        return jnp.where(start < T_CTX, 0, 1 + (start - T_CTX) // DEC_SEQ)
    return row


@functools.lru_cache(maxsize=None)
def _dft_tables(L):
    f = np.arange(L, dtype=np.int64)[:, None]
    s = np.arange(L, dtype=np.int64)[None, :]
    ang = np.pi * ((f * s) % (2 * L)).astype(np.float64) / L
    cos = np.cos(ang)
    sin = np.sin(ang)
    alt = np.where(np.arange(L) % 2 == 0, 1.0, -1.0)
    fwd_im = -sin
    fwd_im[0, :] = alt
    fwd = np.concatenate([cos, fwd_im], 0)
    inv_re = cos.T / L
    inv_re[:, 0] = 1.0 / (2 * L)
    inv_im = -sin.T / L
    inv_im[:, 0] = alt / (2 * L)
    inv = np.concatenate([inv_re, inv_im], 1)
    return fwd.astype(np.float32), inv.astype(np.float32)


@functools.lru_cache(maxsize=None)
def _hyena_embedding(L):
    t01 = np.linspace(0.0, 1.0, L, dtype=np.float64)[:, None]
    bands = np.linspace(1e-4, HY_BANDS - 1, HY_BANDS, dtype=np.float64)
    ang = (2.0 * math.pi / L) * np.arange(L, dtype=np.float64)[:, None] * bands[None, :]
    z = np.concatenate([t01, np.cos(ang), -np.sin(ang)], -1)
    zp = np.zeros((L, LANE), np.float64)
    zp[:, :HY_EMB] = z
    deltas = np.abs(np.linspace(math.log(HY_TARGET) / HY_SLOW_DECAY,
                                math.log(HY_TARGET) / HY_FAST_DECAY, HY_W, dtype=np.float64))
    return zp.astype(np.float32), deltas[None, :].astype(np.float32)


@functools.lru_cache(maxsize=None)
def _rope_tables(L, rot_dim, width):
    rows = L // GRID_W
    n_freq = rot_dim // 4
    half = rot_dim // 2
    inv = ROPE_BASE ** (-np.arange(n_freq, dtype=np.float64) / n_freq)
    pos = np.arange(L)
    row = (pos // GRID_W).astype(np.float64)
    col = (pos % GRID_W).astype(np.float64)
    ang = np.concatenate([row[:, None] * inv, col[:, None] * inv], -1)
    cos, sin = np.cos(ang), np.sin(ang)
    zero = np.zeros_like(sin)
    c = np.tile(np.concatenate([cos, cos], -1), (1, width // rot_dim))
    s_up = np.tile(np.concatenate([-sin, zero], -1), (1, width // rot_dim))
    s_dn = np.tile(np.concatenate([zero, sin], -1), (1, width // rot_dim))
    return c.astype(np.float32), s_up.astype(np.float32), s_dn.astype(np.float32), half


def _rope128(x, c, s_up, s_dn, half):
    up = pltpu.roll(x, LANE - half, axis=1)
    dn = pltpu.roll(x, half, axis=1)
    return x * c + up * s_up + dn * s_dn


def _ada_kernel(cc_ref, cl_ref, w_ref, b_ref, o_ref):
    row = lax.broadcasted_iota(jnp.int32, (ADA_ROWS, 1), 0)
    cv = jnp.where(row == 0, cc_ref[...], 0.0)
    for b in range(DEC_BATCH):
        cv = jnp.where(row == 1 + b, cl_ref[b:b + 1, :], cv)
    s = cv * jax.nn.sigmoid(cv)
    s_hi = s.astype(BF16)
    s_lo = (s - s_hi.astype(F32)).astype(BF16)
    w = w_ref[...]
    w_hi = w.astype(BF16)
    w_lo = (w - w_hi.astype(F32)).astype(BF16)
    rows = s.shape[0]
    both = jnp.dot(jnp.concatenate([s_hi, s_lo], 0), w_hi, preferred_element_type=F32)
    mod = (both[0:rows] + (both[rows:2 * rows] + jnp.dot(s_hi, w_lo, preferred_element_type=F32))
           + b_ref[...])
    for r in range(rows):
        o_ref[r] = mod[r:r + 1]


ADA_ROWS = 16


def _ada_mods(c_ctx, c, w_ada, b_ada):
    tn = 1536
    n = 6 * D_MODEL
    return pl.pallas_call(
        _ada_kernel,
        out_shape=jax.ShapeDtypeStruct((DEPTH, ADA_ROWS, 1, n), F32),
        grid=(DEPTH, n // tn),
        in_specs=[
            pl.BlockSpec((1, D_MODEL), lambda l, j: (0, 0)),
            pl.BlockSpec((DEC_BATCH, D_MODEL), lambda l, j: (0, 0)),
            pl.BlockSpec((None, D_MODEL, tn), lambda l, j: (l, 0, j)),
            pl.BlockSpec((None, 1, tn), lambda l, j: (l, 0, j)),
        ],
        out_specs=pl.BlockSpec((None, ADA_ROWS, 1, tn), lambda l, j: (l, 0, 0, j)),
        compiler_params=_params("parallel", "parallel"),
        name="ada_mods",
    )(c_ctx.reshape(1, D_MODEL), c, w_ada, b_ada.reshape(DEPTH, 1, n))


def _mod_spec(layer, row_of_step):
    return pl.BlockSpec((None, None, 1, 6 * D_MODEL), lambda *g: (layer, row_of_step(*g), 0, 0))


def _lnmod_kernel(*refs):
    x_ref, m_ref, h_ref = refs[-3:]
    y = _layer_norm(x_ref[...])
    s1 = m_ref[:, 0:D_MODEL]
    sc1 = m_ref[:, D_MODEL:2 * D_MODEL]
    h_ref[...] = (y * (1.0 + sc1) + s1).astype(h_ref.dtype)


def _ln_mod(x_group, prev_out, mods, layer, row0):
    tm = 512
    row = _mod_row(tm)
    tile0 = row0 // tm
    in_specs = [
        pl.BlockSpec((tm, D_MODEL), lambda i: (i, 0)),
        _mod_spec(layer, lambda i: row(tile0 + i)),
    ]
    args = [x_group, mods]
    if prev_out is not None:
        in_specs = [pl.BlockSpec(memory_space=pl.ANY)] + in_specs
        args = [prev_out] + args
    return pl.pallas_call(
        _lnmod_kernel,
        out_shape=jax.ShapeDtypeStruct((T_ALL, D_MODEL), BF16),
        grid=(x_group.shape[0] // tm,),
        in_specs=in_specs,
        out_specs=pl.BlockSpec((tm, D_MODEL), lambda i: (tile0 + i, 0)),
        input_output_aliases={} if prev_out is None else {0: 0},
        compiler_params=_params("parallel"),
        name="ln_mod",
    )(*args)


def _proj_kernel(h_ref, w_ref, o_ref, wb_ref, *, gate):
    @pl.when(pl.program_id(1) == 0)
    def _():
        wb_ref[...] = w_ref[...].T.astype(BF16)

    if not gate:
        o_ref[...] = jnp.dot(h_ref[...], wb_ref[...], preferred_element_type=F32).astype(o_ref.dtype)
        return
    sub = 2 * LANE
    for c0 in range(0, o_ref.shape[1], sub):
        r = jnp.dot(h_ref[...], wb_ref[:, c0:c0 + sub], preferred_element_type=F32)
        o_ref[:, c0:c0 + sub] = (0.5 * jnp.tanh(0.5 * r) + 0.5).astype(o_ref.dtype)


def _in_proj(h, w_t, layer, col0, n_cols, tn, out_dtype, gate):
    tm = 2048
    return pl.pallas_call(
        functools.partial(_proj_kernel, gate=gate),
        out_shape=jax.ShapeDtypeStruct((T_ALL, n_cols), out_dtype),
        grid=(n_cols // tn, T_ALL // tm),
        in_specs=[
            pl.BlockSpec((tm, D_MODEL), lambda j, i: (i, 0)),
            pl.BlockSpec((pl.Element(tn), pl.Element(D_MODEL)),
                         lambda j, i: (pl.multiple_of(layer * IN_COLS + col0 + j * tn, 8), 0)),
        ],
        out_specs=pl.BlockSpec((tm, tn), lambda j, i: (i, j)),
        scratch_shapes=[pltpu.VMEM((D_MODEL, tn), BF16)],
        compiler_params=_params("parallel", "arbitrary"),
        name="gate_proj" if gate else "in_proj",
    )(h, w_t)


def _hy_filter_kernel(z_ref, dl_ref, w1_ref, b1_ref, w2_ref, b2_ref, w3_ref, fwd_ref,
                      kre_ref, kim_ref, *, L):
    z = z_ref[...]
    a = jnp.sin(_dot_split(z, w1_ref[...]) + b1_ref[...])
    a = jnp.sin(_dot_split(a, w2_ref[...]) + b2_ref[...])
    h = _dot_split(a, w3_ref[...])
    decay = jnp.exp(-z[:, 0:1] * dl_ref[...])
    not_first = lax.broadcasted_iota(jnp.int32, (L, HY_W), 0) > 0
    sums, diffs = [], []
    for o in range(2):
        fw = h[:, (2 * o) * HY_W:(2 * o + 1) * HY_W] * decay
        bw = jnp.where(not_first, h[:, (2 * o + 1) * HY_W:(2 * o + 2) * HY_W] * decay, 0.0)
        sums.append(fw + bw)
        diffs.append(fw - bw)
    p = _dot(fwd_ref[...], jnp.concatenate(sums, 1))
    q = _dot(fwd_ref[L:2 * L, :], jnp.concatenate(diffs, 1))
    kre_ref[...] = p[0:L]
    first = lax.broadcasted_iota(jnp.int32, (L, 2 * HY_W), 0) == 0
    kim_ref[...] = jnp.where(first, p[L:L + 1], q)


def _hy_filters(L, w1p, b1, w2, b2, w3, fwd):
    zemb, deltas = _hyena_embedding(L)
    out = jax.ShapeDtypeStruct((L, 2 * HY_W), F32)
    return pl.pallas_call(
        functools.partial(_hy_filter_kernel, L=L),
        out_shape=(out, out),
        compiler_params=pltpu.CompilerParams(vmem_limit_bytes=VMEM_LIMIT),
        name=f"hy_filters_{L}",
    )(jnp.asarray(zemb), jnp.asarray(deltas), w1p, b1, w2, b2, w3, fwd)


def _group_step(ctx_body, lat_body):
    i = pl.program_id(0)
    pl.when(i < CTX_STEPS)(ctx_body)
    pl.when(i >= CTX_STEPS)(lat_body)


def _lat_index(i):
    return jnp.maximum(i - CTX_STEPS, 0)


def _hyena_kernel(hy_ref, cw_ref, cb_ref, bias_ref, kre_c, kim_c, fwd_c, inv_c,
                  kre_l, kim_l, fwd_l, inv_l, o_ref):
    _group_step(
        lambda: _hyena_body(hy_ref, cw_ref, cb_ref, bias_ref, kre_c, kim_c, fwd_c, inv_c, o_ref,
                            SEQ, CTX_SEQS_PER_STEP),
        lambda: _hyena_body(hy_ref, cw_ref, cb_ref, bias_ref, kre_l, kim_l, fwd_l, inv_l, o_ref,
                            DEC_SEQ, 1))


def _hyena_body(hy_ref, cw_ref, cb_ref, bias_ref, kre_ref, kim_ref, fwd_ref, inv_ref, o_ref, L, seqs):
    first = lax.broadcasted_iota(jnp.int32, (L, HY_W), 0) == 0

    def long_conv(u, o):
        uf = _dot(fwd_ref[...], u)
        ure, uim = uf[0:L], uf[L:2 * L]
        kre = kre_ref[:, o * HY_W:(o + 1) * HY_W]
        kim = kim_ref[:, o * HY_W:(o + 1) * HY_W]
        yre = jnp.where(first, ure * kre, ure * kre - uim * kim)
        yim = jnp.where(first, uim * kim, ure * kim + uim * kre)
        y = _dot(inv_ref[...], jnp.concatenate([yre, yim], 0))
        return y + u * bias_ref[o:o + 1, :]

    for g in range(seqs):
        sl = slice(g * L, (g + 1) * L)
        x = hy_ref[sl, :].astype(F32)
        rows = lax.broadcasted_iota(jnp.int32, x.shape, 0)
        prev = jnp.where(rows == 0, 0.0, pltpu.roll(x, 1, axis=0))
        nxt = jnp.where(rows == L - 1, 0.0, pltpu.roll(x, L - 1, axis=0))
        z = prev * cw_ref[0:1, :] + x * cw_ref[1:2, :] + nxt * cw_ref[2:3, :] + cb_ref[...]
        v, x1, x2 = z[:, 0:HY_W], z[:, HY_W:2 * HY_W], z[:, 2 * HY_W:3 * HY_W]
        u = x1 * long_conv(v, 0)
        o_ref[sl, :] = (x2 * long_conv(u, 1)).astype(o_ref.dtype)


def _const_spec(shape):
    return pl.BlockSpec(shape, lambda i: (0,) * len(shape))


def _hyena(z_main, layer, conv_w, conv_b, bias, filters, dft):
    tables, table_specs = [], []
    for L in (SEQ, DEC_SEQ):
        tables += [*filters[L], *dft[L]]
        table_specs += [_const_spec((L, 2 * HY_W)), _const_spec((L, 2 * HY_W)),
                        _const_spec((2 * L, L)), _const_spec((L, 2 * L))]
    return pl.pallas_call(
        _hyena_kernel,
        out_shape=jax.ShapeDtypeStruct((T_ALL, HY_W), BF16),
        grid=(MIXER_STEPS,),
        in_specs=[
            pl.BlockSpec((STEP_ROWS, 3 * HY_W), lambda i: (i, 0)),
            pl.BlockSpec((None, 3, 3 * HY_W), lambda i: (layer, 0, 0)),
            pl.BlockSpec((None, 1, 3 * HY_W), lambda i: (layer, 0, 0)),
            pl.BlockSpec((None, 2, HY_W), lambda i: (layer, 0, 0)),
        ] + table_specs,
        out_specs=pl.BlockSpec((STEP_ROWS, HY_W), lambda i: (i, 0)),
        compiler_params=_params("parallel"),
        name="hyena",
    )(z_main, conv_w, conv_b.reshape(DEPTH, 1, 3 * HY_W), bias, *tables)


def _win_masks():
    lane = lax.broadcasted_iota(jnp.int32, (1, LANE), 1)
    return lane < WIN_HD, lane >= WIN_HD


def _win_head_operands(q, k, v, h):
    lo_mask, hi_mask = _win_masks()
    col = h // 2
    lo = h % 2 == 0
    q128 = jnp.where(lo_mask if lo else hi_mask, q[:, col * LANE:(col + 1) * LANE], 0.0)
    swap = h in (1, 2)
    if swap:
        k = pltpu.roll(k, WIN_HD, axis=1)
        v = pltpu.roll(v, WIN_HD, axis=1)
    return q128, k, v, lo


def _win_kernel(sink_ref, q_ref, kv_ref, ck_ref, cv_ref, c_ref, su_ref, sd_ref, o_ref, *, layer):
    _group_step(
        lambda: _win_ctx_body(sink_ref, q_ref, kv_ref, o_ref, layer),
        lambda: _win_lat_body(sink_ref, q_ref, kv_ref, ck_ref, cv_ref, c_ref, su_ref, sd_ref, o_ref, layer))


def _win_ctx_body(sink_ref, q_ref, kv_ref, o_ref, layer):
    lo_mask, hi_mask = _win_masks()
    qscale = WIN_HD ** -0.5 * LOG2E
    for g in range(CTX_SEQS_PER_STEP):
        sl = slice(g * SEQ, (g + 1) * SEQ)
        q = q_ref[sl, :].astype(F32) * qscale
        k = kv_ref[sl, 0:LANE].astype(F32)
        v = kv_ref[sl, LANE:2 * LANE].astype(F32)
        cols = []
        for col in range(2):
            acc = None
            for h in (2 * col, 2 * col + 1):
                q128, kk, vv, lo = _win_head_operands(q, k, v, h)
                s = _dot_nt(q128, kk)
                sink = sink_ref[layer, h] * LOG2E
                m = jnp.maximum(jnp.max(s, -1, keepdims=True), sink)
                p = jnp.exp2(s - m)
                den = jnp.sum(p, -1, keepdims=True) + jnp.exp2(sink - m)
                o = _dot(p, vv) / den
                o = jnp.where(lo_mask if lo else hi_mask, o, 0.0)
                acc = o if acc is None else acc + o
            cols.append(acc)
        o_ref[sl, :] = jnp.concatenate(cols, 1).astype(o_ref.dtype)


def _win_lat_body(sink_ref, q_ref, kv_ref, ck_ref, cv_ref, c_ref, su_ref, sd_ref, o_ref, layer):
    L = DEC_SEQ
    half = WIN_HD // 2
    c, su, sd = c_ref[...], su_ref[...], sd_ref[...]
    qscale = WIN_HD ** -0.5 * LOG2E
    q = jnp.concatenate(
        [_rope128(q_ref[:, i * LANE:(i + 1) * LANE].astype(F32), c, su, sd, half) for i in range(2)],
        1) * qscale
    k = _rope128(kv_ref[:, 0:LANE].astype(F32), c, su, sd, half)
    v = kv_ref[:, LANE:2 * LANE].astype(F32)
    ck = ck_ref[...]
    cv = cv_ref[...]
    lo_mask, hi_mask = _win_masks()
    nb = L // CHUNK
    assert WINDOW == CHUNK
    rr = lax.broadcasted_iota(jnp.int32, (CHUNK, CHUNK), 0)
    cc = lax.broadcasted_iota(jnp.int32, (CHUNK, CHUNK), 1)
    band = {-1: jnp.where(cc >= rr, 0.0, NEG), 0: jnp.zeros((CHUNK, CHUNK), F32),
            1: jnp.where(cc <= rr, 0.0, NEG)}
    cols = []
    for col in range(2):
        acc_blocks = [None] * nb
        for h in (2 * col, 2 * col + 1):
            q128, kk, vv, lo = _win_head_operands(q, k, v, h)
            _, ckk, cvv, _ = _win_head_operands(q, ck, cv, h)
            sink = sink_ref[layer, h] * LOG2E
            for n in range(nb):
                blocks = [d for d in (-1, 0, 1) if 0 <= n + d < nb]
                k0 = (n + blocks[0]) * CHUNK
                k1 = (n + blocks[-1] + 1) * CHUNK
                qn = q128[n * CHUNK:(n + 1) * CHUNK]
                s_loc = _dot_nt(qn, kk[k0:k1]) + jnp.concatenate([band[d] for d in blocks], 1)
                s_ctx = _dot_nt(qn, ckk)
                m = jnp.maximum(jnp.maximum(jnp.max(s_loc, -1, keepdims=True),
                                            jnp.max(s_ctx, -1, keepdims=True)), sink)
                p_loc = jnp.exp2(s_loc - m)
                p_ctx = jnp.exp2(s_ctx - m)
                den = (jnp.sum(p_loc, -1, keepdims=True) + jnp.sum(p_ctx, -1, keepdims=True)
                       + jnp.exp2(sink - m))
                o = (_dot(p_loc, vv[k0:k1]) + _dot(p_ctx, cvv)) / den
                o = jnp.where(lo_mask if lo else hi_mask, o, 0.0)
                acc_blocks[n] = o if acc_blocks[n] is None else acc_blocks[n] + o
        cols.append(jnp.concatenate(acc_blocks, 0))
    o_ref[...] = jnp.concatenate(cols, 1).astype(o_ref.dtype)


def _win(z_main, sink, cache_k, cache_v, layer):
    c, su, sd, _ = _rope_tables(DEC_SEQ, WIN_HD, LANE)
    tab = _const_spec((DEC_SEQ, LANE))
    cache = pl.BlockSpec((None, None, PAST_LEN, LANE), lambda i: (_lat_index(i), layer, 0, 0))
    return pl.pallas_call(
        functools.partial(_win_kernel, layer=layer),
        out_shape=jax.ShapeDtypeStruct((T_ALL, WIN_HEADS * WIN_HD), BF16),
        grid=(MIXER_STEPS,),
        in_specs=[
            pl.BlockSpec(memory_space=pltpu.SMEM),
            pl.BlockSpec((STEP_ROWS, 256), lambda i: (i, COL_WQ // 256)),
            pl.BlockSpec((STEP_ROWS, 256), lambda i: (i, COL_WK // 256)),
            cache, cache, tab, tab, tab,
        ],
        out_specs=pl.BlockSpec((STEP_ROWS, 256), lambda i: (i, 0)),
        compiler_params=_params("parallel"),
        name="win",
    )(sink, z_main, z_main, cache_k, cache_v, jnp.asarray(c), jnp.asarray(su), jnp.asarray(sd))


def _ret_kernel(*refs, layer):
    _group_step(lambda: _ret_body(*refs, L=SEQ, layer=layer, ctx=True),
                lambda: _ret_body(*refs, L=DEC_SEQ, layer=layer, ctx=False))


def _ret_body(df_ref, db_ref, q_ref, k_ref, v0_ref, v1_ref, g0_ref, g1_ref, s0f_ref, s0b_ref,
              o_ref, sf_out, sb_out, s_ref, cross_ref, *, L, layer, ctx):
    seqs = STEP_ROWS // L
    if not ctx:
        sf_out[...] = jnp.zeros_like(sf_out)
        sb_out[...] = jnp.zeros_like(sb_out)
    C = CHUNK
    nc = L // C
    H = RET_HEADS
    qw = H * RET_DK
    vw = H * RET_DV

    def lane_table(width, per_head, fn):
        pos = lax.broadcasted_iota(jnp.int32, (C, per_head), 0).astype(F32)
        return jnp.concatenate([fn(h, pos) for h in range(H)], 1)

    def log_gamma(ref, h):
        d = jnp.full((1, 1), ref[layer, h], F32)
        return jnp.log(jax.nn.sigmoid(d))

    lgf = [log_gamma(df_ref, h) for h in range(H)]
    lgb = [log_gamma(db_ref, h) for h in range(H)]

    def tables(lg, reverse):
        if reverse:
            dq = lane_table(vw, RET_DV, lambda h, pos: jnp.exp((C - pos) * lg[h]))
            dk = lane_table(qw, RET_DK, lambda h, pos: jnp.exp(pos * lg[h]))
        else:
            dq = lane_table(vw, RET_DV, lambda h, pos: jnp.exp((pos + 1.0) * lg[h]))
            dk = lane_table(qw, RET_DK, lambda h, pos: jnp.exp((C - 1.0 - pos) * lg[h]))
        dc = jnp.concatenate([jnp.broadcast_to(jnp.exp(C * lg[h]), (1, RET_DV)) for h in range(H)], 1)
        return dq, dk, dc

    tab_f = tables(lgf, False)
    tab_b = tables(lgb, True)
    ii = lax.broadcasted_iota(jnp.int32, (C, C), 0)
    jj = lax.broadcasted_iota(jnp.int32, (C, C), 1)
    diff = (ii - jj).astype(F32)
    dmats = [jnp.where(diff >= 0, jnp.exp(jnp.maximum(diff, 0.0) * lgf[h]), 0.0)
             + jnp.where(diff <= 0, jnp.exp(jnp.maximum(-diff, 0.0) * lgb[h]), 0.0) for h in range(H)]
    lane_q = lax.broadcasted_iota(jnp.int32, (1, qw), 1) // RET_DK

    srow = lax.broadcasted_iota(jnp.int32, (qw, vw), 0) // RET_DK
    scol = lax.broadcasted_iota(jnp.int32, (qw, vw), 1) // RET_DV
    diag = srow == scol

    for g in range(seqs):
        base = g * L
        rows_all = slice(base, base + L)
        q_all = q_ref[rows_all, :].astype(F32)
        k_all = k_ref[rows_all, :].astype(F32) * (RET_DK ** -0.5)
        v_all = jnp.concatenate([v0_ref[rows_all, :], v1_ref[rows_all, :]], 1).astype(F32)
        g_all = jnp.concatenate([g0_ref[rows_all, :], g1_ref[rows_all, :]], 1).astype(F32)

        def scan(tabs, reverse, s0_ref, s_out):
            dq, dk, dc = tabs
            s_ref[g] = jnp.zeros((qw, vw), F32)
            if s0_ref is not None:
                for h in range(H):
                    s_ref[g, h * RET_DK:(h + 1) * RET_DK, h * RET_DV:(h + 1) * RET_DV] = s0_ref[h]
            order = range(nc - 1, -1, -1) if reverse else range(nc)
            for ci in order:
                sl = slice(ci * C, (ci + 1) * C)
                rs = slice(base + ci * C, base + (ci + 1) * C)
                qc, kc, vc = q_all[sl], k_all[sl], v_all[sl]
                st = s_ref[g]
                cross = _dot(qc, st) * dq
                if reverse:
                    cross_ref[rs, :] = cross_ref[rs, :] + cross
                else:
                    cross_ref[rs, :] = cross
                upd = jnp.where(diag, _dot_tn(kc * dk, vc), 0.0)
                s_ref[g] = st * dc + upd
            if s_out is not None:
                for h in range(H):
                    s_out[g, h] = s_ref[g, h * RET_DK:(h + 1) * RET_DK, h * RET_DV:(h + 1) * RET_DV]

        scan(tab_f, False, None if ctx else s0f_ref, sf_out if ctx else None)
        scan(tab_b, True, None if ctx else s0b_ref, sb_out if ctx else None)

        for h in range(H):
            hv = slice(h * RET_DV, (h + 1) * RET_DV)
            for ci in range(nc):
                sl = slice(ci * C, (ci + 1) * C)
                qh = jnp.where(lane_q == h, q_all[sl], 0.0)
                att = _dot_nt(qh, k_all[sl]) * dmats[h]
                o = _dot(att, v_all[sl, hv]) + cross_ref[base + ci * C:base + (ci + 1) * C, hv]
                gt = g_all[sl, hv]
                o_ref[base + ci * C:base + (ci + 1) * C, hv] = (
                    (gt * jax.nn.sigmoid(gt)) * _layer_norm(o)).astype(o_ref.dtype)


def _retention(z_main, dec_f, dec_b, s0f, s0b, layer):
    def zcol(col):
        return pl.BlockSpec((STEP_ROWS, 256), lambda i: (i, col // 256))

    smem = pl.BlockSpec(memory_space=pltpu.SMEM)
    z_specs = [zcol(COL_RQ), zcol(COL_RK), zcol(COL_RV), zcol(COL_RV + 256),
               zcol(COL_RG), zcol(COL_RG + 256)]
    s0_spec = pl.BlockSpec((None, None, RET_HEADS, RET_DK, RET_DV),
                           lambda i: (_lat_index(i), layer, 0, 0, 0))
    st_shape = jax.ShapeDtypeStruct((MIXER_STEPS * CTX_SEQS_PER_STEP, RET_HEADS, RET_DK, RET_DV), F32)
    st_spec = pl.BlockSpec((CTX_SEQS_PER_STEP, RET_HEADS, RET_DK, RET_DV), lambda i: (i, 0, 0, 0))
    return pl.pallas_call(
        functools.partial(_ret_kernel, layer=layer),
        out_shape=(jax.ShapeDtypeStruct((T_ALL, RET_HEADS * RET_DV), BF16), st_shape, st_shape),
        grid=(MIXER_STEPS,),
        in_specs=[smem, smem] + z_specs + [s0_spec, s0_spec],
        out_specs=(pl.BlockSpec((STEP_ROWS, RET_HEADS * RET_DV), lambda i: (i, 0)), st_spec, st_spec),
        scratch_shapes=[pltpu.VMEM((CTX_SEQS_PER_STEP, RET_HEADS * RET_DK, RET_HEADS * RET_DV), F32),
                        pltpu.VMEM((STEP_ROWS, RET_HEADS * RET_DV), F32)],
        compiler_params=_params("parallel"),
        name="retention",
    )(dec_f, dec_b, *([z_main] * 6), s0f, s0b)


def _rms_norm(x, g):
    return x * lax.rsqrt(jnp.mean(x * x, -1, keepdims=True) + RMS_EPS) * g


def _mla_keys(kn, kr):
    lane_r = lax.broadcasted_iota(jnp.int32, (1, LANE), 1)
    return jnp.concatenate([kn, jnp.where(lane_r < MLA_ROPE, kr, 0.0)], 1).astype(BF16)


def _mla_attend(qn, qr, k_cat, vv, o_ref, row0):
    qscale = (MLA_NOPE + MLA_ROPE) ** -0.5 * LOG2E
    lane_n = lax.broadcasted_iota(jnp.int32, (1, MLA_HEADS * MLA_NOPE), 1) // MLA_NOPE
    lane_r = lax.broadcasted_iota(jnp.int32, (1, LANE), 1)
    qn = qn * qscale
    qr = qr * qscale
    acc = None
    for h in range(MLA_HEADS):
        qnh = jnp.where(lane_n == h, qn, 0.0)
        qrh = qr if h == 0 else pltpu.roll(qr, LANE - h * MLA_ROPE, axis=1)
        qrh = jnp.where(lane_r < MLA_ROPE, qrh, 0.0)
        s = _dot_nt(jnp.concatenate([qnh, qrh], 1), k_cat)
        m = jnp.max(s, -1, keepdims=True)
        p = jnp.exp2(s - m)
        den = jnp.sum(p, -1, keepdims=True)
        o = jnp.where(lane_n == h, _dot(p, vv) / den, 0.0)
        acc = o if acc is None else acc + o
    o_ref[row0:row0 + acc.shape[0], :] = acc.astype(o_ref.dtype)


def _mla_kernel(cq_ref, ckv_ref, kr_ref, cckv_ref, ckr_ref, c_ref, su_ref, sd_ref,
                qg_ref, kg_ref, wqn_ref, wqr_ref, wk_ref, wv_ref, o_ref, ckvn_ref):
    weights = (qg_ref, kg_ref, wqn_ref, wqr_ref, wk_ref, wv_ref)
    _group_step(
        lambda: _mla_ctx_body(cq_ref, ckv_ref, kr_ref, *weights, o_ref, ckvn_ref),
        lambda: _mla_lat_body(cq_ref, ckv_ref, kr_ref, cckv_ref, ckr_ref, c_ref, su_ref, sd_ref,
                              *weights, o_ref, ckvn_ref))


def _mla_ctx_body(cq_ref, ckv_ref, kr_ref, qg_ref, kg_ref, wqn_ref, wqr_ref, wk_ref, wv_ref,
                  o_ref, ckvn_ref):
    cqn = _rms_norm(cq_ref[...].astype(F32), qg_ref[...])
    qn = _dot(cqn, wqn_ref[...])
    qr = _dot(cqn, wqr_ref[...])
    ckvn = _rms_norm(ckv_ref[...].astype(F32), kg_ref[...])
    ckvn_ref[...] = ckvn
    k_cat = _mla_keys(_dot(ckvn, wk_ref[...]), kr_ref[...].astype(F32))
    vv = _dot(ckvn, wv_ref[...]).astype(BF16)
    for g in range(CTX_SEQS_PER_STEP):
        sl = slice(g * SEQ, (g + 1) * SEQ)
        _mla_attend(qn[sl], qr[sl], k_cat[sl], vv[sl], o_ref, g * SEQ)


def _mla_lat_body(cq_ref, ckv_ref, kr_ref, cckv_ref, ckr_ref, c_ref, su_ref, sd_ref,
                  qg_ref, kg_ref, wqn_ref, wqr_ref, wk_ref, wv_ref, o_ref, ckvn_ref):
    half = MLA_ROPE // 2
    c, su, sd = c_ref[...], su_ref[...], sd_ref[...]
    cqn = _rms_norm(cq_ref[...].astype(F32), qg_ref[...])
    qn = _dot(cqn, wqn_ref[...])
    qr = _rope128(_dot(cqn, wqr_ref[...]), c, su, sd, half)
    ckvn = _rms_norm(ckv_ref[...].astype(F32), kg_ref[...])
    ckvn_ref[...] = ckvn
    ckv_all = jnp.concatenate([ckvn, cckv_ref[...]], 0)
    vv = _dot(ckv_all, wv_ref[...]).astype(BF16)
    kr = jnp.concatenate([_rope128(kr_ref[...].astype(F32), c, su, sd, half), ckr_ref[...]], 0)
    k_cat = _mla_keys(_dot(ckv_all, wk_ref[...]), kr)
    rows_per_call = 1024
    for n in range(DEC_SEQ // rows_per_call):
        rows = slice(n * rows_per_call, (n + 1) * rows_per_call)
        _mla_attend(qn[rows], qr[rows], k_cat, vv, o_ref, n * rows_per_call)


def _mla(z_main, cache_ckv, cache_kr_pad, weights, layer):
    c, su, sd, _ = _rope_tables(DEC_SEQ, MLA_ROPE, LANE)
    tab = _const_spec((DEC_SEQ, LANE))
    cache = pl.BlockSpec((None, None, PAST_LEN, LANE), lambda i: (_lat_index(i), layer, 0, 0))

    def weight(*shape):
        return pl.BlockSpec((None,) + shape, lambda i: (layer, 0, 0))

    return pl.pallas_call(
        _mla_kernel,
        out_shape=(jax.ShapeDtypeStruct((T_ALL, MLA_HEADS * MLA_V), BF16),
                   jax.ShapeDtypeStruct((T_ALL, MLA_KV_LORA), F32)),
        grid=(MIXER_STEPS,),
        in_specs=[
            pl.BlockSpec((STEP_ROWS, 256), lambda i: (i, COL_CQ // 256)),
            pl.BlockSpec((STEP_ROWS, LANE), lambda i: (i, COL_CKV // LANE)),
            pl.BlockSpec((STEP_ROWS, LANE), lambda i: (i, COL_KROPE // LANE)),
            cache, cache, tab, tab, tab,
            weight(1, MLA_Q_LORA), weight(1, MLA_KV_LORA),
            weight(MLA_Q_LORA, MLA_HEADS * MLA_NOPE), weight(MLA_Q_LORA, MLA_HEADS * MLA_ROPE),
            weight(MLA_KV_LORA, MLA_HEADS * MLA_NOPE), weight(MLA_KV_LORA, MLA_HEADS * MLA_V),
        ],
        out_specs=(pl.BlockSpec((STEP_ROWS, 256), lambda i: (i, 0)),
                   pl.BlockSpec((STEP_ROWS, MLA_KV_LORA), lambda i: (i, 0))),
        compiler_params=_params("parallel"),
        name="mla",
    )(z_main, z_main, z_main, cache_ckv, cache_kr_pad,
      jnp.asarray(c), jnp.asarray(su), jnp.asarray(sd), *weights)


def _route(logits_t, rb):
    scores = jax.nn.sigmoid(logits_t)
    biased = scores + rb
    sc = [scores[e:e + 1, :] for e in range(N_EXPERTS)]
    bi = [biased[e:e + 1, :] for e in range(N_EXPERTS)]
    epg = EXPERTS_PER_GROUP
    gsum = []
    for g in range(N_GROUPS):
        v = bi[g * epg:(g + 1) * epg]
        best = None
        for a in range(epg):
            for b in range(a + 1, epg):
                pair = v[a] + v[b]
                best = pair if best is None else jnp.maximum(best, pair)
        gsum.append(best)
    combine = []
    sel = []
    for g in range(N_GROUPS):
        is_best = None
        for g2 in range(N_GROUPS):
            if g2 == g:
                continue
            c = gsum[g] > gsum[g2] if g2 < g else gsum[g] >= gsum[g2]
            is_best = c if is_best is None else jnp.logical_and(is_best, c)
        for a in range(epg):
            e = g * epg + a
            rank = jnp.zeros_like(bi[e])
            for b in range(epg):
                if b == a:
                    continue
                e2 = g * epg + b
                ahead = bi[e2] >= bi[e] if b < a else bi[e2] > bi[e]
                rank = rank + jnp.where(ahead, 1.0, 0.0)
            sel.append(jnp.logical_and(is_best, rank < 2.0))
    wsum = None
    for e in range(N_EXPERTS):
        w = jnp.where(sel[e], sc[e], 0.0)
        wsum = w if wsum is None else wsum + w
    for e in range(N_EXPERTS):
        combine.append(jnp.where(sel[e], ROUTE_SCALE * sc[e] / wsum, 0.0))
    return jnp.concatenate(combine, 0)


MERGE_BRANCH_ROWS = (HY_W, WIN_HEADS * WIN_HD, RET_HEADS * RET_DV, MLA_HEADS * MLA_V)
MERGE_ROWS = sum(MERGE_BRANCH_ROWS) + D_MODEL


def _merge_kernel(ya_ref, yb_ref, yc_ref, yd_ref, gt_ref, xc_ref, xl_ref, m_ref,
                  w_ref, g_ref, b_ref, rw_ref, rb_ref,
                  x1_ref, h2_ref, cmb_ref, *, ctx_tiles, sub_rows):
    D = D_MODEL
    rw = rw_ref[...]
    rw_hi = rw.astype(BF16)
    rw_lo = (rw - rw_hi.astype(F32)).astype(BF16)
    g1 = m_ref[:, 2 * D:3 * D]
    s2 = m_ref[:, 3 * D:4 * D]
    sc2 = m_ref[:, 4 * D:5 * D]
    is_ctx = pl.program_id(0) < ctx_tiles
    offs = np.cumsum((0,) + MERGE_BRANCH_ROWS)
    branches = tuple((y_ref, slice(int(offs[i]), int(offs[i + 1])))
                     for i, y_ref in enumerate((ya_ref, yb_ref, yc_ref, yd_ref)))
    w_out_rows = slice(int(offs[-1]), MERGE_ROWS)
    for r0 in range(0, x1_ref.shape[0], sub_rows):
        rows = slice(r0, r0 + sub_rows)
        merged = None
        for i, (y_ref, w_rows) in enumerate(branches):
            t = gt_ref[rows, i * D:(i + 1) * D].astype(F32) * jnp.dot(
                y_ref[rows, :], w_ref[w_rows, :], preferred_element_type=F32)
            merged = t if merged is None else merged + t
        out1 = jnp.dot(merged.astype(BF16), w_ref[w_out_rows, :], preferred_element_type=F32)
        x = jnp.where(is_ctx, xc_ref[rows, :], xl_ref[rows, :])
        x1 = _layer_norm(ALPHA * x + g1 * out1) * g_ref[...] + b_ref[...]
        x1_ref[rows, :] = x1
        h2 = _layer_norm(x1) * (1.0 + sc2) + s2
        h2_hi = h2.astype(BF16)
        h2_ref[rows, :] = h2_hi
        h2_lo = (h2 - h2_hi.astype(F32)).astype(BF16)
        logits = (jnp.dot(h2_hi, rw_hi, preferred_element_type=F32)
                  + (jnp.dot(h2_lo, rw_hi, preferred_element_type=F32)
                     + jnp.dot(h2_hi, rw_lo, preferred_element_type=F32)))
        cmb_ref[:, rows] = _route(logits.T[0:N_EXPERTS], rb_ref[...])


def _merge(ya, yb, yc, yd, gates, x_ctx, x_lat, mods, w_merge, ln1_g, ln1_b,
           router_w, router_b, layer):
    tm = 512
    row = _mod_row(tm)
    D = D_MODEL
    ctx_tiles = T_CTX // tm

    def tile(w):
        return pl.BlockSpec((tm, w), lambda i: (i, 0))

    def weight(k, n):
        return pl.BlockSpec((None, k, n), lambda i: (layer, 0, 0))

    return pl.pallas_call(
        functools.partial(_merge_kernel, ctx_tiles=ctx_tiles, sub_rows=256),
        out_shape=(jax.ShapeDtypeStruct((T_ALL, D), F32),
                   jax.ShapeDtypeStruct((T_ALL, D), BF16),
                   jax.ShapeDtypeStruct((N_EXPERTS, T_ALL), F32)),
        grid=(T_ALL // tm,),
        in_specs=[
            tile(256), tile(256), tile(512), tile(256), tile(4 * D),
            pl.BlockSpec((tm, D), lambda i: (jnp.minimum(i, ctx_tiles - 1), 0)),
            pl.BlockSpec((tm, D), lambda i: (jnp.maximum(i - ctx_tiles, 0), 0)),
            _mod_spec(layer, row),
            weight(MERGE_ROWS, D), weight(1, D), weight(1, D),
            pl.BlockSpec((D, LANE), lambda i: (0, 0)),
            pl.BlockSpec((N_EXPERTS, 1), lambda i: (0, 0)),
        ],
        out_specs=(tile(D), tile(D), pl.BlockSpec((N_EXPERTS, tm), lambda i: (0, i))),
        compiler_params=_params("parallel"),
        name="merge",
    )(ya, yb, yc, yd, gates, x_ctx, x_lat, mods, w_merge,
      ln1_g.reshape(DEPTH, 1, D), ln1_b.reshape(DEPTH, 1, D), router_w,
      router_b.reshape(N_EXPERTS, 1))


MOE_EXPERTS_PER_STEP = 2


def _moe_kernel(*refs, next_h):
    if next_h:
        (h_ref, c_ref, x1_ref, m_ref, wg_ref, wu_ref, wd_ref, g_ref, b_ref, mn_ref,
         o_ref, hn_ref, acc_ref) = refs[-13:]
    else:
        h_ref, c_ref, x1_ref, m_ref, wg_ref, wu_ref, wd_ref, g_ref, b_ref, o_ref, acc_ref = refs
    eg = pl.program_id(1)

    @pl.when(eg == 0)
    def _():
        acc_ref[...] = jnp.zeros_like(acc_ref)

    h = h_ref[...]
    cmb = c_ref[...]
    lane = lax.broadcasted_iota(jnp.int32, cmb.shape, 1)
    hid = []
    for k in range(MOE_EXPERTS_PER_STEP):
        gate = jnp.dot(h, wg_ref[k].astype(BF16), preferred_element_type=F32)
        up = jnp.dot(h, wu_ref[k].astype(BF16), preferred_element_type=F32)
        e = eg * MOE_EXPERTS_PER_STEP + k
        ce = jnp.sum(jnp.where(lane == e, cmb, 0.0), -1, keepdims=True)
        sig = 0.5 * jnp.tanh(0.5 * gate) + 0.5
        hid.append((gate * sig * (up * ce)).astype(BF16))
    wd = wd_ref[...].reshape(MOE_EXPERTS_PER_STEP * D_EXPERT, D_MODEL).astype(BF16)
    acc_ref[...] += jnp.dot(jnp.concatenate(hid, 1), wd, preferred_element_type=F32)

    @pl.when(eg == N_EXPERTS // MOE_EXPERTS_PER_STEP - 1)
    def _():
        g2 = m_ref[:, 5 * D_MODEL:6 * D_MODEL]
        y = _layer_norm(ALPHA * x1_ref[...] + g2 * acc_ref[...])
        y = y * g_ref[...] + b_ref[...]
        o_ref[...] = y
        if next_h:
            s1 = mn_ref[:, 0:D_MODEL]
            sc1 = mn_ref[:, D_MODEL:2 * D_MODEL]
            hn_ref[...] = (_layer_norm(y) * (1.0 + sc1) + s1).astype(hn_ref.dtype)


def _moe(h2, combine, x1, mods, w_gate, w_up, w_down, ln2_g, ln2_b, layer, row0, n_rows,
         next_h=False, prev_h=None):
    tm = 1024
    row = _mod_row(tm)
    D = D_MODEL
    t0 = row0 // tm
    eps = MOE_EXPERTS_PER_STEP
    mod_spec = _mod_spec(layer, lambda i, e: row(t0 + i))
    in_specs = [
        pl.BlockSpec((tm, D), lambda i, e: (t0 + i, 0)),
        pl.BlockSpec((tm, N_EXPERTS), lambda i, e: (t0 + i, 0)),
        pl.BlockSpec((tm, D), lambda i, e: (t0 + i, 0)),
        mod_spec,
        pl.BlockSpec((None, eps, D, D_EXPERT), lambda i, e: (layer, e, 0, 0)),
        pl.BlockSpec((None, eps, D, D_EXPERT), lambda i, e: (layer, e, 0, 0)),
        pl.BlockSpec((None, eps, D_EXPERT, D), lambda i, e: (layer, e, 0, 0)),
        pl.BlockSpec((None, 1, D), lambda i, e: (layer, 0, 0)),
        pl.BlockSpec((None, 1, D), lambda i, e: (layer, 0, 0)),
    ]
    args = [h2, combine, x1, mods, w_gate, w_up, w_down,
            ln2_g.reshape(DEPTH, 1, D), ln2_b.reshape(DEPTH, 1, D)]
    out_shape = jax.ShapeDtypeStruct((n_rows, D), F32)
    out_specs = pl.BlockSpec((tm, D), lambda i, e: (i, 0))
    aliases = {}
    if next_h:
        in_specs.append(_mod_spec(layer + 1, lambda i, e: row(t0 + i)))
        args.append(mods)
        out_shape = (out_shape, jax.ShapeDtypeStruct((T_ALL, D), BF16))
        out_specs = (out_specs, pl.BlockSpec((tm, D), lambda i, e: (t0 + i, 0)))
        if prev_h is not None:
            in_specs = [pl.BlockSpec(memory_space=pl.ANY)] + in_specs
            args = [prev_h] + args
            aliases = {0: 1}
    return pl.pallas_call(
        functools.partial(_moe_kernel, next_h=next_h),
        out_shape=out_shape,
        grid=(n_rows // tm, N_EXPERTS // eps),
        in_specs=in_specs,
        out_specs=out_specs,
        input_output_aliases=aliases,
        scratch_shapes=[pltpu.VMEM((tm, D), F32)],
        compiler_params=_params("parallel", "arbitrary"),
        name="moe",
    )(*args)


def kernel(x_prompt, x_sample, cache_win_k, cache_win_v, cache_mla_ckv, cache_mla_krope,
           state_ret_fwd, state_ret_bwd, c, c_ctx, w_ada, b_ada, w_in,
           hy_conv_w, hy_conv_b, hy_w1, hy_b1, hy_w2, hy_b2, hy_w3, hy_bias,
           win_sink, ret_decay_fwd, ret_decay_bwd, mla_q_norm, mla_kv_norm, mla_w_uq, mla_w_ukv,
           w_br_a, w_br_b, w_br_c, w_br_d, w_out, ln1_g, ln1_b, ln2_g, ln2_b,
           router_w, router_b, moe_w_gate, moe_w_up, moe_w_down):
    D = D_MODEL
    x_ctx = x_prompt.reshape(T_CTX, D)
    x_lat = x_sample.reshape(T_LAT, D)

    mods = _ada_mods(c_ctx, c, w_ada, b_ada)

    w_in_t = jnp.swapaxes(w_in, 1, 2).reshape(DEPTH * IN_COLS, D)
    cache_k = cache_win_k.reshape(DEC_BATCH, DEPTH, PAST_LEN, WIN_KV_HEADS * WIN_HD)
    cache_v = cache_win_v.reshape(DEC_BATCH, DEPTH, PAST_LEN, WIN_KV_HEADS * WIN_HD)
    cache_kr = jnp.pad(cache_mla_krope, ((0, 0), (0, 0), (0, 0), (0, LANE - MLA_ROPE)))

    uq = mla_w_uq.reshape(DEPTH, MLA_Q_LORA, MLA_HEADS, MLA_NOPE + MLA_ROPE)
    ukv = mla_w_ukv.reshape(DEPTH, MLA_KV_LORA, MLA_HEADS, MLA_NOPE + MLA_V)
    mla_weights = (
        mla_q_norm.reshape(DEPTH, 1, MLA_Q_LORA),
        mla_kv_norm.reshape(DEPTH, 1, MLA_KV_LORA),
        uq[..., :MLA_NOPE].reshape(DEPTH, MLA_Q_LORA, MLA_HEADS * MLA_NOPE),
        uq[..., MLA_NOPE:].reshape(DEPTH, MLA_Q_LORA, MLA_HEADS * MLA_ROPE),
        ukv[..., :MLA_NOPE].reshape(DEPTH, MLA_KV_LORA, MLA_HEADS * MLA_NOPE),
        ukv[..., MLA_NOPE:].reshape(DEPTH, MLA_KV_LORA, MLA_HEADS * MLA_V),
    )

    hy_w1p = jnp.pad(hy_w1, ((0, 0), (0, LANE - HY_EMB), (0, 0)))
    dft = {}
    for L in (SEQ, DEC_SEQ):
        fwd, inv = _dft_tables(L)
        dft[L] = (jnp.asarray(fwd).astype(BF16), jnp.asarray(inv).astype(BF16))
    router_w_pad = jnp.pad(router_w, ((0, 0), (0, LANE - N_EXPERTS)))
    w_merge = jnp.concatenate([w_br_a, w_br_b, w_br_c, w_br_d, w_out], 1).astype(BF16)

    new_k, new_v, new_ckv, new_kr, new_sf, new_sb = [], [], [], [], [], []
    for l in range(DEPTH):
        if l == 0:
            h = _ln_mod(x_ctx, None, mods, l, 0)
            h = _ln_mod(x_lat, h, mods, l, T_CTX)
        z = _in_proj(h, w_in_t, l, 0, Z_MAIN, Z_MAIN // 2, BF16, gate=False)
        gates = _in_proj(h, w_in_t, l, COL_GATE, 4 * D, D, BF16, gate=True)

        filters = {L: _hy_filters(L, hy_w1p[l], hy_b1[l][None], hy_w2[l], hy_b2[l][None], hy_w3[l],
                                  dft[L][0]) for L in (SEQ, DEC_SEQ)}
        ya = _hyena(z, l, hy_conv_w, hy_conv_b, hy_bias, filters, dft)
        yb = _win(z, win_sink, cache_k, cache_v, l)
        yc, sf, sb = _retention(z, ret_decay_fwd, ret_decay_bwd, state_ret_fwd, state_ret_bwd, l)
        yd, ckvn = _mla(z, cache_mla_ckv, cache_kr, mla_weights, l)

        x1, h2, combine_t = _merge(ya, yb, yc, yd, gates, x_ctx, x_lat, mods, w_merge,
                                   ln1_g, ln1_b, router_w_pad, router_b, l)
        moe_args = (h2, combine_t.T, x1, mods, moe_w_gate, moe_w_up, moe_w_down, ln2_g, ln2_b, l)
        if l + 1 < DEPTH:
            x_ctx, h = _moe(*moe_args, 0, T_CTX, next_h=True)
            x_lat, h = _moe(*moe_args, T_CTX, T_LAT, next_h=True, prev_h=h)
        else:
            x_ctx = _moe(*moe_args, 0, T_CTX)
            x_lat = _moe(*moe_args, T_CTX, T_LAT)

        def ctx_cols(col, width):
            return z[:T_CTX, col:col + width].astype(F32)

        new_k.append(ctx_cols(COL_WK, 128).reshape(BATCH, SEQ, WIN_KV_HEADS, WIN_HD))
        new_v.append(ctx_cols(COL_WV, 128).reshape(BATCH, SEQ, WIN_KV_HEADS, WIN_HD))
        new_ckv.append(ckvn[:T_CTX].reshape(BATCH, SEQ, MLA_KV_LORA))
        new_kr.append(ctx_cols(COL_KROPE, MLA_ROPE).reshape(BATCH, SEQ, MLA_ROPE))
        new_sf.append(sf[:BATCH])
        new_sb.append(sb[:BATCH])

    y_prompt = x_ctx.reshape(BATCH, SEQ, D)
    y_sample = x_lat.reshape(DEC_BATCH, DEC_SEQ, D)
    return (y_prompt, y_sample, jnp.stack(new_k, 1), jnp.stack(new_v, 1), jnp.stack(new_ckv, 1),
            jnp.stack(new_kr, 1), jnp.stack(new_sf, 1), jnp.stack(new_sb, 1))
```

```python
import functools
import math

import numpy as np
import jax
import jax.numpy as jnp
from jax import lax
from jax.experimental import pallas as pl
from jax.experimental.pallas import tpu as pltpu

F32 = jnp.float32
BF16 = jnp.bfloat16

D_MODEL = 1024
BATCH = 16
SEQ = 256
DEPTH = 2
DEC_BATCH = 2
DEC_SEQ = 1024
PAST_LEN = 256
GRID_W = 64
CHUNK = 128
ROPE_BASE = 10000.0
NEG = -1e30
LN_EPS = 1e-5
RMS_EPS = 1e-6
LOG2E = math.log2(math.e)

HY_W = 256
HY_BANDS = 16
HY_EMB = 1 + 2 * HY_BANDS
HY_FFN = 64
HY_FAST_DECAY = 0.3
HY_SLOW_DECAY = 1.5
HY_TARGET = 1e-2

WIN_HEADS = 4
WIN_KV_HEADS = 2
WIN_HD = 64
WINDOW = 128

RET_HEADS = 4
RET_DK = 64
RET_DV = 128

MLA_HEADS = 4
MLA_Q_LORA = 256
MLA_KV_LORA = 128
MLA_NOPE = 64
MLA_ROPE = 32
MLA_V = 64

N_EXPERTS = 16
N_GROUPS = 4
EXPERTS_PER_GROUP = N_EXPERTS // N_GROUPS
D_EXPERT = 256
ROUTE_SCALE = 2.5

ALPHA = (2.0 * DEPTH) ** 0.25

T_CTX = BATCH * SEQ
T_LAT = DEC_BATCH * DEC_SEQ
T_ALL = T_CTX + T_LAT

COL_HY = 0
COL_WQ = 768
COL_WK = 1024
COL_WV = 1152
COL_RQ = 1280
COL_RK = 1536
COL_RV = 1792
COL_RG = 2304
COL_CQ = 2816
COL_CKV = 3072
COL_KROPE = 3200
COL_GATE = 3232
IN_COLS = COL_GATE + 4 * D_MODEL
Z_MAIN = 3328

LANE = 128
STEP_ROWS = 1024
CTX_SEQS_PER_STEP = STEP_ROWS // SEQ
CTX_STEPS = T_CTX // STEP_ROWS
MIXER_STEPS = T_ALL // STEP_ROWS
VMEM_LIMIT = 56 * 1024 * 1024


def _params(*sem):
    return pltpu.CompilerParams(dimension_semantics=sem, vmem_limit_bytes=VMEM_LIMIT)


def _dot(a, b):
    return jnp.dot(a.astype(BF16), b.astype(BF16), preferred_element_type=F32)


def _dot_split(a, b):
    a_hi = a.astype(BF16)
    a_lo = (a - a_hi.astype(F32)).astype(BF16)
    b_hi = b.astype(BF16)
    b_lo = (b - b_hi.astype(F32)).astype(BF16)

    def mm(x, y):
        return jnp.dot(x, y, preferred_element_type=F32)

    return mm(a_hi, b_hi) + (mm(a_lo, b_hi) + mm(a_hi, b_lo))


def _dot_nt(a, b):
    return lax.dot_general(a.astype(BF16), b.astype(BF16), (((1,), (1,)), ((), ())),
                           preferred_element_type=F32)


def _dot_tn(a, b):
    return lax.dot_general(a.astype(BF16), b.astype(BF16), (((0,), (0,)), ((), ())),
                           preferred_element_type=F32)


def _layer_norm(x):
    mu = jnp.mean(x, -1, keepdims=True)
    xc = x - mu
    var = jnp.mean(xc * xc, -1, keepdims=True)
    return xc * lax.rsqrt(var + LN_EPS)


def _mod_row(tile_rows):
    def row(i):
        start = i * tile_rows
        return jnp.where(start < T_CTX, 0, 1 + (start - T_CTX) // DEC_SEQ)
    return row


@functools.lru_cache(maxsize=None)
def _dft_tables(L):
    f = np.arange(L, dtype=np.int64)[:, None]
    s = np.arange(L, dtype=np.int64)[None, :]
    ang = np.pi * ((f * s) % (2 * L)).astype(np.float64) / L
    cos = np.cos(ang)
    sin = np.sin(ang)
    alt = np.where(np.arange(L) % 2 == 0, 1.0, -1.0)
    fwd_im = -sin
    fwd_im[0, :] = alt
    fwd = np.concatenate([cos, fwd_im], 0)
    inv_re = cos.T / L
    inv_re[:, 0] = 1.0 / (2 * L)
    inv_im = -sin.T / L
    inv_im[:, 0] = alt / (2 * L)
    inv = np.concatenate([inv_re, inv_im], 1)
    return fwd.astype(np.float32), inv.astype(np.float32)


@functools.lru_cache(maxsize=None)
def _hyena_embedding(L):
    t01 = np.linspace(0.0, 1.0, L, dtype=np.float64)[:, None]
    bands = np.linspace(1e-4, HY_BANDS - 1, HY_BANDS, dtype=np.float64)
    ang = (2.0 * math.pi / L) * np.arange(L, dtype=np.float64)[:, None] * bands[None, :]
    z = np.concatenate([t01, np.cos(ang), -np.sin(ang)], -1)
    zp = np.zeros((L, LANE), np.float64)
    zp[:, :HY_EMB] = z
    deltas = np.abs(np.linspace(math.log(HY_TARGET) / HY_SLOW_DECAY,
                                math.log(HY_TARGET) / HY_FAST_DECAY, HY_W, dtype=np.float64))
    return zp.astype(np.float32), deltas[None, :].astype(np.float32)


@functools.lru_cache(maxsize=None)
def _rope_tables(L, rot_dim, width):
    rows = L // GRID_W
    n_freq = rot_dim // 4
    half = rot_dim // 2
    inv = ROPE_BASE ** (-np.arange(n_freq, dtype=np.float64) / n_freq)
    pos = np.arange(L)
    row = (pos // GRID_W).astype(np.float64)
    col = (pos % GRID_W).astype(np.float64)
    ang = np.concatenate([row[:, None] * inv, col[:, None] * inv], -1)
    cos, sin = np.cos(ang), np.sin(ang)
    zero = np.zeros_like(sin)
    c = np.tile(np.concatenate([cos, cos], -1), (1, width // rot_dim))
    s_up = np.tile(np.concatenate([-sin, zero], -1), (1, width // rot_dim))
    s_dn = np.tile(np.concatenate([zero, sin], -1), (1, width // rot_dim))
    return c.astype(np.float32), s_up.astype(np.float32), s_dn.astype(np.float32), half


def _rope128(x, c, s_up, s_dn, half):
    up = pltpu.roll(x, LANE - half, axis=1)
    dn = pltpu.roll(x, half, axis=1)
    return x * c + up * s_up + dn * s_dn


def _ada_kernel(cc_ref, cl_ref, w_ref, b_ref, o_ref):
    row = lax.broadcasted_iota(jnp.int32, (ADA_ROWS, 1), 0)
    cv = jnp.where(row == 0, cc_ref[...], 0.0)
    for b in range(DEC_BATCH):
        cv = jnp.where(row == 1 + b, cl_ref[b:b + 1, :], cv)
    s = cv * jax.nn.sigmoid(cv)
    s_hi = s.astype(BF16)
    s_lo = (s - s_hi.astype(F32)).astype(BF16)
    w = w_ref[...]
    w_hi = w.astype(BF16)
    w_lo = (w - w_hi.astype(F32)).astype(BF16)
    rows = s.shape[0]
    both = jnp.dot(jnp.concatenate([s_hi, s_lo], 0), w_hi, preferred_element_type=F32)
    mod = (both[0:rows] + (both[rows:2 * rows] + jnp.dot(s_hi, w_lo, preferred_element_type=F32))
           + b_ref[...])
    for r in range(rows):
        o_ref[r] = mod[r:r + 1]


ADA_ROWS = 16


def _ada_mods(c_ctx, c, w_ada, b_ada):
    tn = 1536
    n = 6 * D_MODEL
    return pl.pallas_call(
        _ada_kernel,
        out_shape=jax.ShapeDtypeStruct((DEPTH, ADA_ROWS, 1, n), F32),
        grid=(DEPTH, n // tn),
        in_specs=[
            pl.BlockSpec((1, D_MODEL), lambda l, j: (0, 0)),
            pl.BlockSpec((DEC_BATCH, D_MODEL), lambda l, j: (0, 0)),
            pl.BlockSpec((None, D_MODEL, tn), lambda l, j: (l, 0, j)),
            pl.BlockSpec((None, 1, tn), lambda l, j: (l, 0, j)),
        ],
        out_specs=pl.BlockSpec((None, ADA_ROWS, 1, tn), lambda l, j: (l, 0, 0, j)),
        compiler_params=_params("parallel", "parallel"),
        name="ada_mods",
    )(c_ctx.reshape(1, D_MODEL), c, w_ada, b_ada.reshape(DEPTH, 1, n))


def _mod_spec(layer, row_of_step):
    return pl.BlockSpec((None, None, 1, 6 * D_MODEL), lambda *g: (layer, row_of_step(*g), 0, 0))


def _lnmod_kernel(*refs):
    x_ref, m_ref, h_ref = refs[-3:]
    y = _layer_norm(x_ref[...])
    s1 = m_ref[:, 0:D_MODEL]
    sc1 = m_ref[:, D_MODEL:2 * D_MODEL]
    h_ref[...] = (y * (1.0 + sc1) + s1).astype(h_ref.dtype)


def _ln_mod(x_group, prev_out, mods, layer, row0):
    tm = 512
    row = _mod_row(tm)
    tile0 = row0 // tm
    in_specs = [
        pl.BlockSpec((tm, D_MODEL), lambda i: (i, 0)),
        _mod_spec(layer, lambda i: row(tile0 + i)),
    ]
    args = [x_group, mods]
    if prev_out is not None:
        in_specs = [pl.BlockSpec(memory_space=pl.ANY)] + in_specs
        args = [prev_out] + args
    return pl.pallas_call(
        _lnmod_kernel,
        out_shape=jax.ShapeDtypeStruct((T_ALL, D_MODEL), BF16),
        grid=(x_group.shape[0] // tm,),
        in_specs=in_specs,
        out_specs=pl.BlockSpec((tm, D_MODEL), lambda i: (tile0 + i, 0)),
        input_output_aliases={} if prev_out is None else {0: 0},
        compiler_params=_params("parallel"),
        name="ln_mod",
    )(*args)


def _proj_kernel(h_ref, w_ref, o_ref, wb_ref, *, gate):
    @pl.when(pl.program_id(1) == 0)
    def _():
        wb_ref[...] = w_ref[...].T.astype(BF16)

    if not gate:
        o_ref[...] = jnp.dot(h_ref[...], wb_ref[...], preferred_element_type=F32).astype(o_ref.dtype)
        return
    sub = 2 * LANE
    for c0 in range(0, o_ref.shape[1], sub):
        r = jnp.dot(h_ref[...], wb_ref[:, c0:c0 + sub], preferred_element_type=F32)
        rb = r.astype(o_ref.dtype)
        o_ref[:, c0:c0 + sub] = 0.5 * jnp.tanh(0.5 * rb) + 0.5


def _in_proj(h, w_t, layer, col0, n_cols, tn, out_dtype, gate):
    tm = 2048
    return pl.pallas_call(
        functools.partial(_proj_kernel, gate=gate),
        out_shape=jax.ShapeDtypeStruct((T_ALL, n_cols), out_dtype),
        grid=(n_cols // tn, T_ALL // tm),
        in_specs=[
            pl.BlockSpec((tm, D_MODEL), lambda j, i: (i, 0)),
            pl.BlockSpec((pl.Element(tn), pl.Element(D_MODEL)),
                         lambda j, i: (pl.multiple_of(layer * IN_COLS + col0 + j * tn, 8), 0)),
        ],
        out_specs=pl.BlockSpec((tm, tn), lambda j, i: (i, j)),
        scratch_shapes=[pltpu.VMEM((D_MODEL, tn), BF16)],
        compiler_params=_params("parallel", "arbitrary"),
        name="gate_proj" if gate else "in_proj",
    )(h, w_t)


def _hy_filter_kernel(z_ref, dl_ref, w1_ref, b1_ref, w2_ref, b2_ref, w3_ref, fwd_ref,
                      kre_ref, kim_ref, *, L):
    z = z_ref[...]
    a = jnp.sin(_dot_split(z, w1_ref[...]) + b1_ref[...])
    a = jnp.sin(_dot_split(a, w2_ref[...]) + b2_ref[...])
    h = _dot_split(a, w3_ref[...])
    decay = jnp.exp(-z[:, 0:1] * dl_ref[...])
    not_first = lax.broadcasted_iota(jnp.int32, (L, HY_W), 0) > 0
    sums, diffs = [], []
    for o in range(2):
        fw = h[:, (2 * o) * HY_W:(2 * o + 1) * HY_W] * decay
        bw = jnp.where(not_first, h[:, (2 * o + 1) * HY_W:(2 * o + 2) * HY_W] * decay, 0.0)
        sums.append(fw + bw)
        diffs.append(fw - bw)
    p = _dot(fwd_ref[...], jnp.concatenate(sums, 1))
    q = _dot(fwd_ref[L:2 * L, :], jnp.concatenate(diffs, 1))
    kre_ref[...] = p[0:L]
    first = lax.broadcasted_iota(jnp.int32, (L, 2 * HY_W), 0) == 0
    kim_ref[...] = jnp.where(first, p[L:L + 1], q)


def _hy_filters(L, w1p, b1, w2, b2, w3, fwd):
    zemb, deltas = _hyena_embedding(L)
    out = jax.ShapeDtypeStruct((L, 2 * HY_W), F32)
    return pl.pallas_call(
        functools.partial(_hy_filter_kernel, L=L),
        out_shape=(out, out),
        compiler_params=pltpu.CompilerParams(vmem_limit_bytes=VMEM_LIMIT),
        name=f"hy_filters_{L}",
    )(jnp.asarray(zemb), jnp.asarray(deltas), w1p, b1, w2, b2, w3, fwd)


def _group_step(ctx_body, lat_body):
    i = pl.program_id(0)
    pl.when(i < CTX_STEPS)(ctx_body)
    pl.when(i >= CTX_STEPS)(lat_body)


def _lat_index(i):
    return jnp.maximum(i - CTX_STEPS, 0)


def _hyena_kernel(hy_ref, cw_ref, cb_ref, bias_ref, kre_c, kim_c, fwd_c, inv_c,
                  kre_l, kim_l, fwd_l, inv_l, o_ref):
    _group_step(
        lambda: _hyena_body(hy_ref, cw_ref, cb_ref, bias_ref, kre_c, kim_c, fwd_c, inv_c, o_ref,
                            SEQ, CTX_SEQS_PER_STEP),
        lambda: _hyena_body(hy_ref, cw_ref, cb_ref, bias_ref, kre_l, kim_l, fwd_l, inv_l, o_ref,
                            DEC_SEQ, 1))


def _hyena_body(hy_ref, cw_ref, cb_ref, bias_ref, kre_ref, kim_ref, fwd_ref, inv_ref, o_ref, L, seqs):
    first = lax.broadcasted_iota(jnp.int32, (L, HY_W), 0) == 0

    def long_conv(u, o):
        uf = _dot(fwd_ref[...], u)
        ure, uim = uf[0:L], uf[L:2 * L]
        kre = kre_ref[:, o * HY_W:(o + 1) * HY_W]
        kim = kim_ref[:, o * HY_W:(o + 1) * HY_W]
        yre = jnp.where(first, ure * kre, ure * kre - uim * kim)
        yim = jnp.where(first, uim * kim, ure * kim + uim * kre)
        y = _dot(inv_ref[...], jnp.concatenate([yre, yim], 0))
        return y + u * bias_ref[o:o + 1, :]

    for g in range(seqs):
        sl = slice(g * L, (g + 1) * L)
        x = hy_ref[sl, :].astype(F32)
        rows = lax.broadcasted_iota(jnp.int32, x.shape, 0)
        prev = jnp.where(rows == 0, 0.0, pltpu.roll(x, 1, axis=0))
        nxt = jnp.where(rows == L - 1, 0.0, pltpu.roll(x, L - 1, axis=0))
        z = prev * cw_ref[0:1, :] + x * cw_ref[1:2, :] + nxt * cw_ref[2:3, :] + cb_ref[...]
        v, x1, x2 = z[:, 0:HY_W], z[:, HY_W:2 * HY_W], z[:, 2 * HY_W:3 * HY_W]
        u = x1 * long_conv(v, 0)
        o_ref[sl, :] = (x2 * long_conv(u, 1)).astype(o_ref.dtype)


def _const_spec(shape):
    return pl.BlockSpec(shape, lambda i: (0,) * len(shape))


def _hyena(z_main, layer, conv_w, conv_b, bias, filters, dft):
    tables, table_specs = [], []
    for L in (SEQ, DEC_SEQ):
        tables += [*filters[L], *dft[L]]
        table_specs += [_const_spec((L, 2 * HY_W)), _const_spec((L, 2 * HY_W)),
                        _const_spec((2 * L, L)), _const_spec((L, 2 * L))]
    return pl.pallas_call(
        _hyena_kernel,
        out_shape=jax.ShapeDtypeStruct((T_ALL, HY_W), BF16),
        grid=(MIXER_STEPS,),
        in_specs=[
            pl.BlockSpec((STEP_ROWS, 3 * HY_W), lambda i: (i, 0)),
            pl.BlockSpec((None, 3, 3 * HY_W), lambda i: (layer, 0, 0)),
            pl.BlockSpec((None, 1, 3 * HY_W), lambda i: (layer, 0, 0)),
            pl.BlockSpec((None, 2, HY_W), lambda i: (layer, 0, 0)),
        ] + table_specs,
        out_specs=pl.BlockSpec((STEP_ROWS, HY_W), lambda i: (i, 0)),
        compiler_params=_params("parallel"),
        name="hyena",
    )(z_main, conv_w, conv_b.reshape(DEPTH, 1, 3 * HY_W), bias, *tables)


def _win_masks():
    lane = lax.broadcasted_iota(jnp.int32, (1, LANE), 1)
    return lane < WIN_HD, lane >= WIN_HD


def _win_head_operands(q, k, v, h):
    lo_mask, hi_mask = _win_masks()
    col = h // 2
    lo = h % 2 == 0
    q128 = jnp.where(lo_mask if lo else hi_mask, q[:, col * LANE:(col + 1) * LANE], 0.0)
    swap = h in (1, 2)
    if swap:
        k = pltpu.roll(k, WIN_HD, axis=1)
        v = pltpu.roll(v, WIN_HD, axis=1)
    return q128, k, v, lo


def _win_kernel(sink_ref, q_ref, kv_ref, ck_ref, cv_ref, c_ref, su_ref, sd_ref, o_ref, *, layer):
    _group_step(
        lambda: _win_ctx_body(sink_ref, q_ref, kv_ref, o_ref, layer),
        lambda: _win_lat_body(sink_ref, q_ref, kv_ref, ck_ref, cv_ref, c_ref, su_ref, sd_ref, o_ref, layer))


def _win_ctx_body(sink_ref, q_ref, kv_ref, o_ref, layer):
    lo_mask, hi_mask = _win_masks()
    qscale = WIN_HD ** -0.5 * LOG2E
    for g in range(CTX_SEQS_PER_STEP):
        sl = slice(g * SEQ, (g + 1) * SEQ)
        q = q_ref[sl, :].astype(F32) * qscale
        k = kv_ref[sl, 0:LANE].astype(F32)
        v = kv_ref[sl, LANE:2 * LANE].astype(F32)
        cols = []
        for col in range(2):
            acc = None
            for h in (2 * col, 2 * col + 1):
                q128, kk, vv, lo = _win_head_operands(q, k, v, h)
                s = _dot_nt(q128, kk)
                sink = sink_ref[layer, h] * LOG2E
                m = jnp.maximum(jnp.max(s, -1, keepdims=True), sink)
                p = jnp.exp2(s - m)
                den = jnp.sum(p, -1, keepdims=True) + jnp.exp2(sink - m)
                o = _dot(p, vv) / den
                o = jnp.where(lo_mask if lo else hi_mask, o, 0.0)
                acc = o if acc is None else acc + o
            cols.append(acc)
        o_ref[sl, :] = jnp.concatenate(cols, 1).astype(o_ref.dtype)


def _win_lat_body(sink_ref, q_ref, kv_ref, ck_ref, cv_ref, c_ref, su_ref, sd_ref, o_ref, layer):
    L = DEC_SEQ
    half = WIN_HD // 2
    c, su, sd = c_ref[...], su_ref[...], sd_ref[...]
    qscale = WIN_HD ** -0.5 * LOG2E
    q = jnp.concatenate(
        [_rope128(q_ref[:, i * LANE:(i + 1) * LANE].astype(F32), c, su, sd, half) for i in range(2)],
        1) * qscale
    k = _rope128(kv_ref[:, 0:LANE].astype(F32), c, su, sd, half)
    v = kv_ref[:, LANE:2 * LANE].astype(F32)
    ck = ck_ref[...]
    cv = cv_ref[...]
    lo_mask, hi_mask = _win_masks()
    nb = L // CHUNK
    assert WINDOW == CHUNK
    rr = lax.broadcasted_iota(jnp.int32, (CHUNK, CHUNK), 0)
    cc = lax.broadcasted_iota(jnp.int32, (CHUNK, CHUNK), 1)
    band = {-1: jnp.where(cc >= rr, 0.0, NEG), 0: jnp.zeros((CHUNK, CHUNK), F32),
            1: jnp.where(cc <= rr, 0.0, NEG)}
    cols = []
    for col in range(2):
        acc_blocks = [None] * nb
        for h in (2 * col, 2 * col + 1):
            q128, kk, vv, lo = _win_head_operands(q, k, v, h)
            _, ckk, cvv, _ = _win_head_operands(q, ck, cv, h)
            sink = sink_ref[layer, h] * LOG2E
            for n in range(nb):
                blocks = [d for d in (-1, 0, 1) if 0 <= n + d < nb]
                k0 = (n + blocks[0]) * CHUNK
                k1 = (n + blocks[-1] + 1) * CHUNK
                qn = q128[n * CHUNK:(n + 1) * CHUNK]
                s_loc = _dot_nt(qn, kk[k0:k1]) + jnp.concatenate([band[d] for d in blocks], 1)
                s_ctx = _dot_nt(qn, ckk)
                m = jnp.maximum(jnp.maximum(jnp.max(s_loc, -1, keepdims=True),
                                            jnp.max(s_ctx, -1, keepdims=True)), sink)
                p_loc = jnp.exp2(s_loc - m)
                p_ctx = jnp.exp2(s_ctx - m)
                den = (jnp.sum(p_loc, -1, keepdims=True) + jnp.sum(p_ctx, -1, keepdims=True)
                       + jnp.exp2(sink - m))
                o = (_dot(p_loc, vv[k0:k1]) + _dot(p_ctx, cvv)) / den
                o = jnp.where(lo_mask if lo else hi_mask, o, 0.0)
                acc_blocks[n] = o if acc_blocks[n] is None else acc_blocks[n] + o
        cols.append(jnp.concatenate(acc_blocks, 0))
    o_ref[...] = jnp.concatenate(cols, 1).astype(o_ref.dtype)


def _win(z_main, sink, cache_k, cache_v, layer):
    c, su, sd, _ = _rope_tables(DEC_SEQ, WIN_HD, LANE)
    tab = _const_spec((DEC_SEQ, LANE))
    cache = pl.BlockSpec((None, None, PAST_LEN, LANE), lambda i: (_lat_index(i), layer, 0, 0))
    return pl.pallas_call(
        functools.partial(_win_kernel, layer=layer),
        out_shape=jax.ShapeDtypeStruct((T_ALL, WIN_HEADS * WIN_HD), BF16),
        grid=(MIXER_STEPS,),
        in_specs=[
            pl.BlockSpec(memory_space=pltpu.SMEM),
            pl.BlockSpec((STEP_ROWS, 256), lambda i: (i, COL_WQ // 256)),
            pl.BlockSpec((STEP_ROWS, 256), lambda i: (i, COL_WK // 256)),
            cache, cache, tab, tab, tab,
        ],
        out_specs=pl.BlockSpec((STEP_ROWS, 256), lambda i: (i, 0)),
        compiler_params=_params("parallel"),
        name="win",
    )(sink, z_main, z_main, cache_k, cache_v, jnp.asarray(c), jnp.asarray(su), jnp.asarray(sd))


def _ret_kernel(*refs, layer):
    _group_step(lambda: _ret_body(*refs, L=SEQ, layer=layer, ctx=True),
                lambda: _ret_body(*refs, L=DEC_SEQ, layer=layer, ctx=False))


def _ret_body(df_ref, db_ref, q_ref, k_ref, v0_ref, v1_ref, g0_ref, g1_ref, s0f_ref, s0b_ref,
              o_ref, sf_out, sb_out, s_ref, cross_ref, *, L, layer, ctx):
    seqs = STEP_ROWS // L
    if not ctx:
        sf_out[...] = jnp.zeros_like(sf_out)
        sb_out[...] = jnp.zeros_like(sb_out)
    C = CHUNK
    nc = L // C
    H = RET_HEADS
    qw = H * RET_DK
    vw = H * RET_DV

    def lane_table(width, per_head, fn):
        pos = lax.broadcasted_iota(jnp.int32, (C, per_head), 0).astype(F32)
        return jnp.concatenate([fn(h, pos) for h in range(H)], 1)

    def log_gamma(ref, h):
        d = jnp.full((1, 1), ref[layer, h], F32)
        return jnp.log(jax.nn.sigmoid(d))

    lgf = [log_gamma(df_ref, h) for h in range(H)]
    lgb = [log_gamma(db_ref, h) for h in range(H)]

    def tables(lg, reverse):
        if reverse:
            dq = lane_table(vw, RET_DV, lambda h, pos: jnp.exp((C - pos) * lg[h]))
            dk = lane_table(qw, RET_DK, lambda h, pos: jnp.exp(pos * lg[h]))
        else:
            dq = lane_table(vw, RET_DV, lambda h, pos: jnp.exp((pos + 1.0) * lg[h]))
            dk = lane_table(qw, RET_DK, lambda h, pos: jnp.exp((C - 1.0 - pos) * lg[h]))
        dc = jnp.concatenate([jnp.broadcast_to(jnp.exp(C * lg[h]), (1, RET_DV)) for h in range(H)], 1)
        return dq, dk, dc

    tab_f = tables(lgf, False)
    tab_b = tables(lgb, True)
    ii = lax.broadcasted_iota(jnp.int32, (C, C), 0)
    jj = lax.broadcasted_iota(jnp.int32, (C, C), 1)
    diff = (ii - jj).astype(F32)
    dmats = [jnp.where(diff >= 0, jnp.exp(jnp.maximum(diff, 0.0) * lgf[h]), 0.0)
             + jnp.where(diff <= 0, jnp.exp(jnp.maximum(-diff, 0.0) * lgb[h]), 0.0) for h in range(H)]
    lane_q = lax.broadcasted_iota(jnp.int32, (1, qw), 1) // RET_DK

    srow = lax.broadcasted_iota(jnp.int32, (qw, vw), 0) // RET_DK
    scol = lax.broadcasted_iota(jnp.int32, (qw, vw), 1) // RET_DV
    diag = srow == scol

    for g in range(seqs):
        base = g * L
        rows_all = slice(base, base + L)
        q_all = q_ref[rows_all, :].astype(F32)
        k_all = k_ref[rows_all, :].astype(F32) * (RET_DK ** -0.5)
        v_all = jnp.concatenate([v0_ref[rows_all, :], v1_ref[rows_all, :]], 1).astype(F32)
        g_all = jnp.concatenate([g0_ref[rows_all, :], g1_ref[rows_all, :]], 1).astype(F32)

        def scan(tabs, reverse, s0_ref, s_out):
            dq, dk, dc = tabs
            s_ref[g] = jnp.zeros((qw, vw), F32)
            if s0_ref is not None:
                for h in range(H):
                    s_ref[g, h * RET_DK:(h + 1) * RET_DK, h * RET_DV:(h + 1) * RET_DV] = s0_ref[h]
            order = range(nc - 1, -1, -1) if reverse else range(nc)
            for ci in order:
                sl = slice(ci * C, (ci + 1) * C)
                rs = slice(base + ci * C, base + (ci + 1) * C)
                qc, kc, vc = q_all[sl], k_all[sl], v_all[sl]
                st = s_ref[g]
                cross = _dot(qc, st) * dq
                if reverse:
                    cross_ref[rs, :] = cross_ref[rs, :] + cross
                else:
                    cross_ref[rs, :] = cross
                upd = jnp.where(diag, _dot_tn(kc * dk, vc), 0.0)
                s_ref[g] = st * dc + upd
            if s_out is not None:
                for h in range(H):
                    s_out[g, h] = s_ref[g, h * RET_DK:(h + 1) * RET_DK, h * RET_DV:(h + 1) * RET_DV]

        scan(tab_f, False, None if ctx else s0f_ref, sf_out if ctx else None)
        scan(tab_b, True, None if ctx else s0b_ref, sb_out if ctx else None)

        for h in range(H):
            hv = slice(h * RET_DV, (h + 1) * RET_DV)
            for ci in range(nc):
                sl = slice(ci * C, (ci + 1) * C)
                qh = jnp.where(lane_q == h, q_all[sl], 0.0)
                att = _dot_nt(qh, k_all[sl]) * dmats[h]
                o = _dot(att, v_all[sl, hv]) + cross_ref[base + ci * C:base + (ci + 1) * C, hv]
                gt = g_all[sl, hv]
                o_ref[base + ci * C:base + (ci + 1) * C, hv] = (
                    (gt * jax.nn.sigmoid(gt)) * _layer_norm(o)).astype(o_ref.dtype)


def _retention(z_main, dec_f, dec_b, s0f, s0b, layer):
    def zcol(col):
        return pl.BlockSpec((STEP_ROWS, 256), lambda i: (i, col // 256))

    smem = pl.BlockSpec(memory_space=pltpu.SMEM)
    z_specs = [zcol(COL_RQ), zcol(COL_RK), zcol(COL_RV), zcol(COL_RV + 256),
               zcol(COL_RG), zcol(COL_RG + 256)]
    s0_spec = pl.BlockSpec((None, None, RET_HEADS, RET_DK, RET_DV),
                           lambda i: (_lat_index(i), layer, 0, 0, 0))
    st_shape = jax.ShapeDtypeStruct((MIXER_STEPS * CTX_SEQS_PER_STEP, RET_HEADS, RET_DK, RET_DV), F32)
    st_spec = pl.BlockSpec((CTX_SEQS_PER_STEP, RET_HEADS, RET_DK, RET_DV), lambda i: (i, 0, 0, 0))
    return pl.pallas_call(
        functools.partial(_ret_kernel, layer=layer),
        out_shape=(jax.ShapeDtypeStruct((T_ALL, RET_HEADS * RET_DV), BF16), st_shape, st_shape),
        grid=(MIXER_STEPS,),
        in_specs=[smem, smem] + z_specs + [s0_spec, s0_spec],
        out_specs=(pl.BlockSpec((STEP_ROWS, RET_HEADS * RET_DV), lambda i: (i, 0)), st_spec, st_spec),
        scratch_shapes=[pltpu.VMEM((CTX_SEQS_PER_STEP, RET_HEADS * RET_DK, RET_HEADS * RET_DV), F32),
                        pltpu.VMEM((STEP_ROWS, RET_HEADS * RET_DV), F32)],
        compiler_params=_params("parallel"),
        name="retention",
    )(dec_f, dec_b, *([z_main] * 6), s0f, s0b)


def _rms_norm(x, g):
    return x * lax.rsqrt(jnp.mean(x * x, -1, keepdims=True) + RMS_EPS) * g


def _mla_keys(kn, kr):
    lane_r = lax.broadcasted_iota(jnp.int32, (1, LANE), 1)
    return jnp.concatenate([kn, jnp.where(lane_r < MLA_ROPE, kr, 0.0)], 1).astype(BF16)


def _mla_attend(qn, qr, k_cat, vv, o_ref, row0):
    qscale = (MLA_NOPE + MLA_ROPE) ** -0.5 * LOG2E
    lane_n = lax.broadcasted_iota(jnp.int32, (1, MLA_HEADS * MLA_NOPE), 1) // MLA_NOPE
    lane_r = lax.broadcasted_iota(jnp.int32, (1, LANE), 1)
    qn = qn * qscale
    qr = qr * qscale
    acc = None
    for h in range(MLA_HEADS):
        qnh = jnp.where(lane_n == h, qn, 0.0)
        qrh = qr if h == 0 else pltpu.roll(qr, LANE - h * MLA_ROPE, axis=1)
        qrh = jnp.where(lane_r < MLA_ROPE, qrh, 0.0)
        s = _dot_nt(jnp.concatenate([qnh, qrh], 1), k_cat)
        m = jnp.max(s, -1, keepdims=True)
        p = jnp.exp2(s - m)
        den = jnp.sum(p, -1, keepdims=True)
        o = jnp.where(lane_n == h, _dot(p, vv) / den, 0.0)
        acc = o if acc is None else acc + o
    o_ref[row0:row0 + acc.shape[0], :] = acc.astype(o_ref.dtype)


def _mla_kernel(cq_ref, ckv_ref, kr_ref, cckv_ref, ckr_ref, c_ref, su_ref, sd_ref,
                qg_ref, kg_ref, wqn_ref, wqr_ref, wk_ref, wv_ref, o_ref, ckvn_ref):
    weights = (qg_ref, kg_ref, wqn_ref, wqr_ref, wk_ref, wv_ref)
    _group_step(
        lambda: _mla_ctx_body(cq_ref, ckv_ref, kr_ref, *weights, o_ref, ckvn_ref),
        lambda: _mla_lat_body(cq_ref, ckv_ref, kr_ref, cckv_ref, ckr_ref, c_ref, su_ref, sd_ref,
                              *weights, o_ref, ckvn_ref))


def _mla_ctx_body(cq_ref, ckv_ref, kr_ref, qg_ref, kg_ref, wqn_ref, wqr_ref, wk_ref, wv_ref,
                  o_ref, ckvn_ref):
    cqn = _rms_norm(cq_ref[...].astype(F32), qg_ref[...])
    qn = _dot(cqn, wqn_ref[...])
    qr = _dot(cqn, wqr_ref[...])
    ckvn = _rms_norm(ckv_ref[...].astype(F32), kg_ref[...])
    ckvn_ref[...] = ckvn
    k_cat = _mla_keys(_dot(ckvn, wk_ref[...]), kr_ref[...].astype(F32))
    vv = _dot(ckvn, wv_ref[...]).astype(BF16)
    for g in range(CTX_SEQS_PER_STEP):
        sl = slice(g * SEQ, (g + 1) * SEQ)
        _mla_attend(qn[sl], qr[sl], k_cat[sl], vv[sl], o_ref, g * SEQ)


def _mla_lat_body(cq_ref, ckv_ref, kr_ref, cckv_ref, ckr_ref, c_ref, su_ref, sd_ref,
                  qg_ref, kg_ref, wqn_ref, wqr_ref, wk_ref, wv_ref, o_ref, ckvn_ref):
    half = MLA_ROPE // 2
    c, su, sd = c_ref[...], su_ref[...], sd_ref[...]
    cqn = _rms_norm(cq_ref[...].astype(F32), qg_ref[...])
    qn = _dot(cqn, wqn_ref[...])
    qr = _rope128(_dot(cqn, wqr_ref[...]), c, su, sd, half)
    ckvn = _rms_norm(ckv_ref[...].astype(F32), kg_ref[...])
    ckvn_ref[...] = ckvn
    ckv_all = jnp.concatenate([ckvn, cckv_ref[...]], 0)
    vv = _dot(ckv_all, wv_ref[...]).astype(BF16)
    kr = jnp.concatenate([_rope128(kr_ref[...].astype(F32), c, su, sd, half), ckr_ref[...]], 0)
    k_cat = _mla_keys(_dot(ckv_all, wk_ref[...]), kr)
    rows_per_call = 1024
    for n in range(DEC_SEQ // rows_per_call):
        rows = slice(n * rows_per_call, (n + 1) * rows_per_call)
        _mla_attend(qn[rows], qr[rows], k_cat, vv, o_ref, n * rows_per_call)


def _mla(z_main, cache_ckv, cache_kr_pad, weights, layer):
    c, su, sd, _ = _rope_tables(DEC_SEQ, MLA_ROPE, LANE)
    tab = _const_spec((DEC_SEQ, LANE))
    cache = pl.BlockSpec((None, None, PAST_LEN, LANE), lambda i: (_lat_index(i), layer, 0, 0))

    def weight(*shape):
        return pl.BlockSpec((None,) + shape, lambda i: (layer, 0, 0))

    return pl.pallas_call(
        _mla_kernel,
        out_shape=(jax.ShapeDtypeStruct((T_ALL, MLA_HEADS * MLA_V), BF16),
                   jax.ShapeDtypeStruct((T_ALL, MLA_KV_LORA), F32)),
        grid=(MIXER_STEPS,),
        in_specs=[
            pl.BlockSpec((STEP_ROWS, 256), lambda i: (i, COL_CQ // 256)),
            pl.BlockSpec((STEP_ROWS, LANE), lambda i: (i, COL_CKV // LANE)),
            pl.BlockSpec((STEP_ROWS, LANE), lambda i: (i, COL_KROPE // LANE)),
            cache, cache, tab, tab, tab,
            weight(1, MLA_Q_LORA), weight(1, MLA_KV_LORA),
            weight(MLA_Q_LORA, MLA_HEADS * MLA_NOPE), weight(MLA_Q_LORA, MLA_HEADS * MLA_ROPE),
            weight(MLA_KV_LORA, MLA_HEADS * MLA_NOPE), weight(MLA_KV_LORA, MLA_HEADS * MLA_V),
        ],
        out_specs=(pl.BlockSpec((STEP_ROWS, 256), lambda i: (i, 0)),
                   pl.BlockSpec((STEP_ROWS, MLA_KV_LORA), lambda i: (i, 0))),
        compiler_params=_params("parallel"),
        name="mla",
    )(z_main, z_main, z_main, cache_ckv, cache_kr_pad,
      jnp.asarray(c), jnp.asarray(su), jnp.asarray(sd), *weights)


def _route(logits_t, rb):
    scores = jax.nn.sigmoid(logits_t)
    biased = scores + rb
    sc = [scores[e:e + 1, :] for e in range(N_EXPERTS)]
    bi = [biased[e:e + 1, :] for e in range(N_EXPERTS)]
    epg = EXPERTS_PER_GROUP
    gsum = []
    for g in range(N_GROUPS):
        v = bi[g * epg:(g + 1) * epg]
        best = None
        for a in range(epg):
            for b in range(a + 1, epg):
                pair = v[a] + v[b]
                best = pair if best is None else jnp.maximum(best, pair)
        gsum.append(best)
    combine = []
    sel = []
    for g in range(N_GROUPS):
        is_best = None
        for g2 in range(N_GROUPS):
            if g2 == g:
                continue
            c = gsum[g] > gsum[g2] if g2 < g else gsum[g] >= gsum[g2]
            is_best = c if is_best is None else jnp.logical_and(is_best, c)
        for a in range(epg):
            e = g * epg + a
            rank = jnp.zeros_like(bi[e])
            for b in range(epg):
                if b == a:
                    continue
                e2 = g * epg + b
                ahead = bi[e2] >= bi[e] if b < a else bi[e2] > bi[e]
                rank = rank + jnp.where(ahead, 1.0, 0.0)
            sel.append(jnp.logical_and(is_best, rank < 2.0))
    wsum = None
    for e in range(N_EXPERTS):
        w = jnp.where(sel[e], sc[e], 0.0)
        wsum = w if wsum is None else wsum + w
    for e in range(N_EXPERTS):
        combine.append(jnp.where(sel[e], ROUTE_SCALE * sc[e] / wsum, 0.0))
    return jnp.concatenate(combine, 0)


MERGE_BRANCH_ROWS = (HY_W, WIN_HEADS * WIN_HD, RET_HEADS * RET_DV, MLA_HEADS * MLA_V)
MERGE_ROWS = sum(MERGE_BRANCH_ROWS) + D_MODEL


def _merge_kernel(ya_ref, yb_ref, yc_ref, yd_ref, gt_ref, xc_ref, xl_ref, m_ref,
                  w_ref, g_ref, b_ref, rw_ref, rb_ref,
                  x1_ref, h2_ref, cmb_ref, *, ctx_tiles, sub_rows):
    D = D_MODEL
    rw = rw_ref[...]
    rw_hi = rw.astype(BF16)
    rw_lo = (rw - rw_hi.astype(F32)).astype(BF16)
    g1 = m_ref[:, 2 * D:3 * D]
    s2 = m_ref[:, 3 * D:4 * D]
    sc2 = m_ref[:, 4 * D:5 * D]
    is_ctx = pl.program_id(0) < ctx_tiles
    offs = np.cumsum((0,) + MERGE_BRANCH_ROWS)
    branches = tuple((y_ref, slice(int(offs[i]), int(offs[i + 1])))
                     for i, y_ref in enumerate((ya_ref, yb_ref, yc_ref, yd_ref)))
    w_out_rows = slice(int(offs[-1]), MERGE_ROWS)
    for r0 in range(0, x1_ref.shape[0], sub_rows):
        rows = slice(r0, r0 + sub_rows)
        merged = None
        for i, (y_ref, w_rows) in enumerate(branches):
            t = gt_ref[rows, i * D:(i + 1) * D] * jnp.dot(
                y_ref[rows, :], w_ref[w_rows, :], preferred_element_type=F32).astype(BF16)
            merged = t if merged is None else merged + t
        out1 = jnp.dot(merged, w_ref[w_out_rows, :], preferred_element_type=F32)
        x = jnp.where(is_ctx, xc_ref[rows, :], xl_ref[rows, :])
        x1 = _layer_norm(ALPHA * x + g1 * out1) * g_ref[...] + b_ref[...]
        x1_ref[rows, :] = x1
        h2 = _layer_norm(x1) * (1.0 + sc2) + s2
        h2_hi = h2.astype(BF16)
        h2_ref[rows, :] = h2_hi
        h2_lo = (h2 - h2_hi.astype(F32)).astype(BF16)
        logits = (jnp.dot(h2_hi, rw_hi, preferred_element_type=F32)
                  + (jnp.dot(h2_lo, rw_hi, preferred_element_type=F32)
                     + jnp.dot(h2_hi, rw_lo, preferred_element_type=F32)))
        cmb_ref[:, rows] = _route(logits.T[0:N_EXPERTS], rb_ref[...])


def _merge(ya, yb, yc, yd, gates, x_ctx, x_lat, mods, w_merge, ln1_g, ln1_b,
           router_w, router_b, layer):
    tm = 512
    row = _mod_row(tm)
    D = D_MODEL
    ctx_tiles = T_CTX // tm

    def tile(w):
        return pl.BlockSpec((tm, w), lambda i: (i, 0))

    def weight(k, n):
        return pl.BlockSpec((None, k, n), lambda i: (layer, 0, 0))

    return pl.pallas_call(
        functools.partial(_merge_kernel, ctx_tiles=ctx_tiles, sub_rows=256),
        out_shape=(jax.ShapeDtypeStruct((T_ALL, D), F32),
                   jax.ShapeDtypeStruct((T_ALL, D), BF16),
                   jax.ShapeDtypeStruct((N_EXPERTS, T_ALL), F32)),
        grid=(T_ALL // tm,),
        in_specs=[
            tile(256), tile(256), tile(512), tile(256), tile(4 * D),
            pl.BlockSpec((tm, D), lambda i: (jnp.minimum(i, ctx_tiles - 1), 0)),
            pl.BlockSpec((tm, D), lambda i: (jnp.maximum(i - ctx_tiles, 0), 0)),
            _mod_spec(layer, row),
            weight(MERGE_ROWS, D), weight(1, D), weight(1, D),
            pl.BlockSpec((D, LANE), lambda i: (0, 0)),
            pl.BlockSpec((N_EXPERTS, 1), lambda i: (0, 0)),
        ],
        out_specs=(tile(D), tile(D), pl.BlockSpec((N_EXPERTS, tm), lambda i: (0, i))),
        compiler_params=_params("parallel"),
        name="merge",
    )(ya, yb, yc, yd, gates, x_ctx, x_lat, mods, w_merge,
      ln1_g.reshape(DEPTH, 1, D), ln1_b.reshape(DEPTH, 1, D), router_w,
      router_b.reshape(N_EXPERTS, 1))


MOE_EXPERTS_PER_STEP = 2


def _moe_kernel(*refs, next_h):
    if next_h:
        (h_ref, c_ref, x1_ref, m_ref, wg_ref, wu_ref, wd_ref, g_ref, b_ref, mn_ref,
         o_ref, hn_ref, acc_ref) = refs[-13:]
    else:
        h_ref, c_ref, x1_ref, m_ref, wg_ref, wu_ref, wd_ref, g_ref, b_ref, o_ref, acc_ref = refs
    eg = pl.program_id(1)

    @pl.when(eg == 0)
    def _():
        acc_ref[...] = jnp.zeros_like(acc_ref)

    h = h_ref[...]
    cmb = c_ref[...]
    lane = lax.broadcasted_iota(jnp.int32, cmb.shape, 1)
    hid = []
    for k in range(MOE_EXPERTS_PER_STEP):
        gate = jnp.dot(h, wg_ref[k].astype(BF16), preferred_element_type=F32)
        up = jnp.dot(h, wu_ref[k].astype(BF16), preferred_element_type=F32)
        e = eg * MOE_EXPERTS_PER_STEP + k
        ce = jnp.sum(jnp.where(lane == e, cmb, 0.0), -1, keepdims=True)
        sig = 0.5 * jnp.tanh(0.5 * gate) + 0.5
        hid.append((gate * sig * (up * ce)).astype(BF16))
    wd = wd_ref[...].reshape(MOE_EXPERTS_PER_STEP * D_EXPERT, D_MODEL).astype(BF16)
    acc_ref[...] += jnp.dot(jnp.concatenate(hid, 1), wd, preferred_element_type=F32)

    @pl.when(eg == N_EXPERTS // MOE_EXPERTS_PER_STEP - 1)
    def _():
        g2 = m_ref[:, 5 * D_MODEL:6 * D_MODEL]
        y = _layer_norm(ALPHA * x1_ref[...] + g2 * acc_ref[...])
        y = y * g_ref[...] + b_ref[...]
        o_ref[...] = y
        if next_h:
            s1 = mn_ref[:, 0:D_MODEL]
            sc1 = mn_ref[:, D_MODEL:2 * D_MODEL]
            hn_ref[...] = (_layer_norm(y) * (1.0 + sc1) + s1).astype(hn_ref.dtype)


def _moe(h2, combine, x1, mods, w_gate, w_up, w_down, ln2_g, ln2_b, layer, row0, n_rows,
         next_h=False, prev_h=None):
    tm = 1024
    row = _mod_row(tm)
    D = D_MODEL
    t0 = row0 // tm
    eps = MOE_EXPERTS_PER_STEP
    mod_spec = _mod_spec(layer, lambda i, e: row(t0 + i))
    in_specs = [
        pl.BlockSpec((tm, D), lambda i, e: (t0 + i, 0)),
        pl.BlockSpec((tm, N_EXPERTS), lambda i, e: (t0 + i, 0)),
        pl.BlockSpec((tm, D), lambda i, e: (t0 + i, 0)),
        mod_spec,
        pl.BlockSpec((None, eps, D, D_EXPERT), lambda i, e: (layer, e, 0, 0)),
        pl.BlockSpec((None, eps, D, D_EXPERT), lambda i, e: (layer, e, 0, 0)),
        pl.BlockSpec((None, eps, D_EXPERT, D), lambda i, e: (layer, e, 0, 0)),
        pl.BlockSpec((None, 1, D), lambda i, e: (layer, 0, 0)),
        pl.BlockSpec((None, 1, D), lambda i, e: (layer, 0, 0)),
    ]
    args = [h2, combine, x1, mods, w_gate, w_up, w_down,
            ln2_g.reshape(DEPTH, 1, D), ln2_b.reshape(DEPTH, 1, D)]
    out_shape = jax.ShapeDtypeStruct((n_rows, D), F32)
    out_specs = pl.BlockSpec((tm, D), lambda i, e: (i, 0))
    aliases = {}
    if next_h:
        in_specs.append(_mod_spec(layer + 1, lambda i, e: row(t0 + i)))
        args.append(mods)
        out_shape = (out_shape, jax.ShapeDtypeStruct((T_ALL, D), BF16))
        out_specs = (out_specs, pl.BlockSpec((tm, D), lambda i, e: (t0 + i, 0)))
        if prev_h is not None:
            in_specs = [pl.BlockSpec(memory_space=pl.ANY)] + in_specs
            args = [prev_h] + args
            aliases = {0: 1}
    return pl.pallas_call(
        functools.partial(_moe_kernel, next_h=next_h),
        out_shape=out_shape,
        grid=(n_rows // tm, N_EXPERTS // eps),
        in_specs=in_specs,
        out_specs=out_specs,
        input_output_aliases=aliases,
        scratch_shapes=[pltpu.VMEM((tm, D), F32)],
        compiler_params=_params("parallel", "arbitrary"),
        name="moe",
    )(*args)


def kernel(x_prompt, x_sample, cache_win_k, cache_win_v, cache_mla_ckv, cache_mla_krope,
           state_ret_fwd, state_ret_bwd, c, c_ctx, w_ada, b_ada, w_in,
           hy_conv_w, hy_conv_b, hy_w1, hy_b1, hy_w2, hy_b2, hy_w3, hy_bias,
           win_sink, ret_decay_fwd, ret_decay_bwd, mla_q_norm, mla_kv_norm, mla_w_uq, mla_w_ukv,
           w_br_a, w_br_b, w_br_c, w_br_d, w_out, ln1_g, ln1_b, ln2_g, ln2_b,
           router_w, router_b, moe_w_gate, moe_w_up, moe_w_down):
    D = D_MODEL
    x_ctx = x_prompt.reshape(T_CTX, D)
    x_lat = x_sample.reshape(T_LAT, D)

    mods = _ada_mods(c_ctx, c, w_ada, b_ada)

    w_in_t = jnp.swapaxes(w_in, 1, 2).reshape(DEPTH * IN_COLS, D)
    cache_k = cache_win_k.reshape(DEC_BATCH, DEPTH, PAST_LEN, WIN_KV_HEADS * WIN_HD)
    cache_v = cache_win_v.reshape(DEC_BATCH, DEPTH, PAST_LEN, WIN_KV_HEADS * WIN_HD)
    cache_kr = jnp.pad(cache_mla_krope, ((0, 0), (0, 0), (0, 0), (0, LANE - MLA_ROPE)))

    uq = mla_w_uq.reshape(DEPTH, MLA_Q_LORA, MLA_HEADS, MLA_NOPE + MLA_ROPE)
    ukv = mla_w_ukv.reshape(DEPTH, MLA_KV_LORA, MLA_HEADS, MLA_NOPE + MLA_V)
    mla_weights = (
        mla_q_norm.reshape(DEPTH, 1, MLA_Q_LORA),
        mla_kv_norm.reshape(DEPTH, 1, MLA_KV_LORA),
        uq[..., :MLA_NOPE].reshape(DEPTH, MLA_Q_LORA, MLA_HEADS * MLA_NOPE),
        uq[..., MLA_NOPE:].reshape(DEPTH, MLA_Q_LORA, MLA_HEADS * MLA_ROPE),
        ukv[..., :MLA_NOPE].reshape(DEPTH, MLA_KV_LORA, MLA_HEADS * MLA_NOPE),
        ukv[..., MLA_NOPE:].reshape(DEPTH, MLA_KV_LORA, MLA_HEADS * MLA_V),
    )

    hy_w1p = jnp.pad(hy_w1, ((0, 0), (0, LANE - HY_EMB), (0, 0)))
    dft = {}
    for L in (SEQ, DEC_SEQ):
        fwd, inv = _dft_tables(L)
        dft[L] = (jnp.asarray(fwd).astype(BF16), jnp.asarray(inv).astype(BF16))
    router_w_pad = jnp.pad(router_w, ((0, 0), (0, LANE - N_EXPERTS)))
    w_merge = jnp.concatenate([w_br_a, w_br_b, w_br_c, w_br_d, w_out], 1).astype(BF16)

    new_k, new_v, new_ckv, new_kr, new_sf, new_sb = [], [], [], [], [], []
    for l in range(DEPTH):
        if l == 0:
            h = _ln_mod(x_ctx, None, mods, l, 0)
            h = _ln_mod(x_lat, h, mods, l, T_CTX)
        z = _in_proj(h, w_in_t, l, 0, Z_MAIN, Z_MAIN // 2, BF16, gate=False)
        gates = _in_proj(h, w_in_t, l, COL_GATE, 4 * D, D, BF16, gate=True)

        filters = {L: _hy_filters(L, hy_w1p[l], hy_b1[l][None], hy_w2[l], hy_b2[l][None], hy_w3[l],
                                  dft[L][0]) for L in (SEQ, DEC_SEQ)}
        ya = _hyena(z, l, hy_conv_w, hy_conv_b, hy_bias, filters, dft)
        yb = _win(z, win_sink, cache_k, cache_v, l)
        yc, sf, sb = _retention(z, ret_decay_fwd, ret_decay_bwd, state_ret_fwd, state_ret_bwd, l)
        yd, ckvn = _mla(z, cache_mla_ckv, cache_kr, mla_weights, l)

        x1, h2, combine_t = _merge(ya, yb, yc, yd, gates, x_ctx, x_lat, mods, w_merge,
                                   ln1_g, ln1_b, router_w_pad, router_b, l)
        moe_args = (h2, combine_t.T, x1, mods, moe_w_gate, moe_w_up, moe_w_down, ln2_g, ln2_b, l)
        if l + 1 < DEPTH:
            x_ctx, h = _moe(*moe_args, 0, T_CTX, next_h=True)
            x_lat, h = _moe(*moe_args, T_CTX, T_LAT, next_h=True, prev_h=h)
        else:
            x_ctx = _moe(*moe_args, 0, T_CTX)
            x_lat = _moe(*moe_args, T_CTX, T_LAT)

        def ctx_cols(col, width):
            return z[:T_CTX, col:col + width].astype(F32)

        new_k.append(ctx_cols(COL_WK, 128).reshape(BATCH, SEQ, WIN_KV_HEADS, WIN_HD))
        new_v.append(ctx_cols(COL_WV, 128).reshape(BATCH, SEQ, WIN_KV_HEADS, WIN_HD))
        new_ckv.append(ckvn[:T_CTX].reshape(BATCH, SEQ, MLA_KV_LORA))
        new_kr.append(ctx_cols(COL_KROPE, MLA_ROPE).reshape(BATCH, SEQ, MLA_ROPE))
        new_sf.append(sf[:BATCH])
        new_sb.append(sb[:BATCH])

    y_prompt = x_ctx.reshape(BATCH, SEQ, D)
    y_sample = x_lat.reshape(DEC_BATCH, DEC_SEQ, D)
    return (y_prompt, y_sample, jnp.stack(new_k, 1), jnp.stack(new_v, 1), jnp.stack(new_ckv, 1),
            jnp.stack(new_kr, 1), jnp.stack(new_sf, 1), jnp.stack(new_sb, 1))
```

```python
import functools
import math

import numpy as np
import jax
import jax.numpy as jnp
from jax import lax
from jax.experimental import pallas as pl
from jax.experimental.pallas import tpu as pltpu

F32 = jnp.float32
BF16 = jnp.bfloat16

D_MODEL = 1024
BATCH = 16
SEQ = 256
DEPTH = 2
DEC_BATCH = 2
DEC_SEQ = 1024
PAST_LEN = 256
GRID_W = 64
CHUNK = 128
ROPE_BASE = 10000.0
NEG = -1e30
LN_EPS = 1e-5
RMS_EPS = 1e-6
LOG2E = math.log2(math.e)

HY_W = 256
HY_BANDS = 16
HY_EMB = 1 + 2 * HY_BANDS
HY_FFN = 64
HY_FAST_DECAY = 0.3
HY_SLOW_DECAY = 1.5
HY_TARGET = 1e-2

WIN_HEADS = 4
WIN_KV_HEADS = 2
WIN_HD = 64
WINDOW = 128

RET_HEADS = 4
RET_DK = 64
RET_DV = 128

MLA_HEADS = 4
MLA_Q_LORA = 256
MLA_KV_LORA = 128
MLA_NOPE = 64
MLA_ROPE = 32
MLA_V = 64

N_EXPERTS = 16
N_GROUPS = 4
EXPERTS_PER_GROUP = N_EXPERTS // N_GROUPS
D_EXPERT = 256
ROUTE_SCALE = 2.5

ALPHA = (2.0 * DEPTH) ** 0.25

T_CTX = BATCH * SEQ
T_LAT = DEC_BATCH * DEC_SEQ
T_ALL = T_CTX + T_LAT

COL_HY = 0
COL_WQ = 768
COL_WK = 1024
COL_WV = 1152
COL_RQ = 1280
COL_RK = 1536
COL_RV = 1792
COL_RG = 2304
COL_CQ = 2816
COL_CKV = 3072
COL_KROPE = 3200
COL_GATE = 3232
IN_COLS = COL_GATE + 4 * D_MODEL
Z_MAIN = 3328

LANE = 128
STEP_ROWS = 1024
CTX_SEQS_PER_STEP = STEP_ROWS // SEQ
CTX_STEPS = T_CTX // STEP_ROWS
MIXER_STEPS = T_ALL // STEP_ROWS
VMEM_LIMIT = 56 * 1024 * 1024


def _params(*sem):
    return pltpu.CompilerParams(dimension_semantics=sem, vmem_limit_bytes=VMEM_LIMIT)


def _dot(a, b):
    return jnp.dot(a.astype(BF16), b.astype(BF16), preferred_element_type=F32)


def _dot_split(a, b):
    a_hi = a.astype(BF16)
    a_lo = (a - a_hi.astype(F32)).astype(BF16)
    b_hi = b.astype(BF16)
    b_lo = (b - b_hi.astype(F32)).astype(BF16)

    def mm(x, y):
        return jnp.dot(x, y, preferred_element_type=F32)

    return mm(a_hi, b_hi) + (mm(a_lo, b_hi) + mm(a_hi, b_lo))


def _dot_nt(a, b):
    return lax.dot_general(a.astype(BF16), b.astype(BF16), (((1,), (1,)), ((), ())),
                           preferred_element_type=F32)


def _dot_tn(a, b):
    return lax.dot_general(a.astype(BF16), b.astype(BF16), (((0,), (0,)), ((), ())),
                           preferred_element_type=F32)


def _layer_norm(x):
    mu = jnp.mean(x, -1, keepdims=True)
    xc = x - mu
    var = jnp.mean(xc * xc, -1, keepdims=True)
    return xc * lax.rsqrt(var + LN_EPS)


def _mod_row(tile_rows):
    def row(i):
        start = i * tile_rows
        return jnp.where(start < T_CTX, 0, 1 + (start - T_CTX) // DEC_SEQ)
    return row


@functools.lru_cache(maxsize=None)
def _dft_tables(L):
    f = np.arange(L, dtype=np.int64)[:, None]
    s = np.arange(L, dtype=np.int64)[None, :]
    ang = np.pi * ((f * s) % (2 * L)).astype(np.float64) / L
    cos = np.cos(ang)
    sin = np.sin(ang)
    alt = np.where(np.arange(L) % 2 == 0, 1.0, -1.0)
    fwd_im = -sin
    fwd_im[0, :] = alt
    fwd = np.concatenate([cos, fwd_im], 0)
    inv_re = cos.T / L
    inv_re[:, 0] = 1.0 / (2 * L)
    inv_im = -sin.T / L
    inv_im[:, 0] = alt / (2 * L)
    inv = np.concatenate([inv_re, inv_im], 1)
    return fwd.astype(np.float32), inv.astype(np.float32)


@functools.lru_cache(maxsize=None)
def _hyena_embedding(L):
    t01 = np.linspace(0.0, 1.0, L, dtype=np.float64)[:, None]
    bands = np.linspace(1e-4, HY_BANDS - 1, HY_BANDS, dtype=np.float64)
    ang = (2.0 * math.pi / L) * np.arange(L, dtype=np.float64)[:, None] * bands[None, :]
    z = np.concatenate([t01, np.cos(ang), -np.sin(ang)], -1)
    zp = np.zeros((L, LANE), np.float64)
    zp[:, :HY_EMB] = z
    deltas = np.abs(np.linspace(math.log(HY_TARGET) / HY_SLOW_DECAY,
                                math.log(HY_TARGET) / HY_FAST_DECAY, HY_W, dtype=np.float64))
    return zp.astype(np.float32), deltas[None, :].astype(np.float32)


@functools.lru_cache(maxsize=None)
def _rope_tables(L, rot_dim, width):
    rows = L // GRID_W
    n_freq = rot_dim // 4
    half = rot_dim // 2
    inv = ROPE_BASE ** (-np.arange(n_freq, dtype=np.float64) / n_freq)
    pos = np.arange(L)
    row = (pos // GRID_W).astype(np.float64)
    col = (pos % GRID_W).astype(np.float64)
    ang = np.concatenate([row[:, None] * inv, col[:, None] * inv], -1)
    cos, sin = np.cos(ang), np.sin(ang)
    zero = np.zeros_like(sin)
    c = np.tile(np.concatenate([cos, cos], -1), (1, width // rot_dim))
    s_up = np.tile(np.concatenate([-sin, zero], -1), (1, width // rot_dim))
    s_dn = np.tile(np.concatenate([zero, sin], -1), (1, width // rot_dim))
    return c.astype(np.float32), s_up.astype(np.float32), s_dn.astype(np.float32), half


def _rope128(x, c, s_up, s_dn, half):
    up = pltpu.roll(x, LANE - half, axis=1)
    dn = pltpu.roll(x, half, axis=1)
    return x * c + up * s_up + dn * s_dn


def _ada_kernel(cc_ref, cl_ref, w_ref, b_ref, o_ref):
    row = lax.broadcasted_iota(jnp.int32, (ADA_ROWS, 1), 0)
    cv = jnp.where(row == 0, cc_ref[...], 0.0)
    for b in range(DEC_BATCH):
        cv = jnp.where(row == 1 + b, cl_ref[b:b + 1, :], cv)
    s = cv * jax.nn.sigmoid(cv)
    s_hi = s.astype(BF16)
    s_lo = (s - s_hi.astype(F32)).astype(BF16)
    w = w_ref[...]
    w_hi = w.astype(BF16)
    w_lo = (w - w_hi.astype(F32)).astype(BF16)
    rows = s.shape[0]
    both = jnp.dot(jnp.concatenate([s_hi, s_lo], 0), w_hi, preferred_element_type=F32)
    mod = (both[0:rows] + (both[rows:2 * rows] + jnp.dot(s_hi, w_lo, preferred_element_type=F32))
           + b_ref[...])
    for r in range(rows):
        o_ref[r] = mod[r:r + 1]


ADA_ROWS = 16


def _ada_mods(c_ctx, c, w_ada, b_ada):
    tn = 1536
    n = 6 * D_MODEL
    return pl.pallas_call(
        _ada_kernel,
        out_shape=jax.ShapeDtypeStruct((DEPTH, ADA_ROWS, 1, n), F32),
        grid=(DEPTH, n // tn),
        in_specs=[
            pl.BlockSpec((1, D_MODEL), lambda l, j: (0, 0)),
            pl.BlockSpec((DEC_BATCH, D_MODEL), lambda l, j: (0, 0)),
            pl.BlockSpec((None, D_MODEL, tn), lambda l, j: (l, 0, j)),
            pl.BlockSpec((None, 1, tn), lambda l, j: (l, 0, j)),
        ],
        out_specs=pl.BlockSpec((None, ADA_ROWS, 1, tn), lambda l, j: (l, 0, 0, j)),
        compiler_params=_params("parallel", "parallel"),
        name="ada_mods",
    )(c_ctx.reshape(1, D_MODEL), c, w_ada, b_ada.reshape(DEPTH, 1, n))


def _mod_spec(layer, row_of_step):
    return pl.BlockSpec((None, None, 1, 6 * D_MODEL), lambda *g: (layer, row_of_step(*g), 0, 0))


def _lnmod_kernel(*refs):
    x_ref, m_ref, h_ref = refs[-3:]
    y = _layer_norm(x_ref[...])
    s1 = m_ref[:, 0:D_MODEL]
    sc1 = m_ref[:, D_MODEL:2 * D_MODEL]
    h_ref[...] = (y * (1.0 + sc1) + s1).astype(h_ref.dtype)


def _ln_mod(x_group, prev_out, mods, layer, row0):
    tm = 512
    row = _mod_row(tm)
    tile0 = row0 // tm
    in_specs = [
        pl.BlockSpec((tm, D_MODEL), lambda i: (i, 0)),
        _mod_spec(layer, lambda i: row(tile0 + i)),
    ]
    args = [x_group, mods]
    if prev_out is not None:
        in_specs = [pl.BlockSpec(memory_space=pl.ANY)] + in_specs
        args = [prev_out] + args
    return pl.pallas_call(
        _lnmod_kernel,
        out_shape=jax.ShapeDtypeStruct((T_ALL, D_MODEL), BF16),
        grid=(x_group.shape[0] // tm,),
        in_specs=in_specs,
        out_specs=pl.BlockSpec((tm, D_MODEL), lambda i: (tile0 + i, 0)),
        input_output_aliases={} if prev_out is None else {0: 0},
        compiler_params=_params("parallel"),
        name="ln_mod",
    )(*args)


def _proj_kernel(h_ref, w_ref, o_ref, wb_ref, *, gate):
    @pl.when(pl.program_id(1) == 0)
    def _():
        wb_ref[...] = w_ref[...].T.astype(BF16)

    if not gate:
        o_ref[...] = jnp.dot(h_ref[...], wb_ref[...], preferred_element_type=F32).astype(o_ref.dtype)
        return
    sub = 2 * LANE
    for c0 in range(0, o_ref.shape[1], sub):
        r = jnp.dot(h_ref[...], wb_ref[:, c0:c0 + sub], preferred_element_type=F32)
        rb = r.astype(o_ref.dtype)
        o_ref[:, c0:c0 + sub] = 0.5 * jnp.tanh(0.5 * rb) + 0.5


def _in_proj(h, w_t, layer, col0, n_cols, tn, out_dtype, gate):
    tm = 2048
    return pl.pallas_call(
        functools.partial(_proj_kernel, gate=gate),
        out_shape=jax.ShapeDtypeStruct((T_ALL, n_cols), out_dtype),
        grid=(n_cols // tn, T_ALL // tm),
        in_specs=[
            pl.BlockSpec((tm, D_MODEL), lambda j, i: (i, 0)),
            pl.BlockSpec((pl.Element(tn), pl.Element(D_MODEL)),
                         lambda j, i: (pl.multiple_of(layer * IN_COLS + col0 + j * tn, 8), 0)),
        ],
        out_specs=pl.BlockSpec((tm, tn), lambda j, i: (i, j)),
        scratch_shapes=[pltpu.VMEM((D_MODEL, tn), BF16)],
        compiler_params=_params("parallel", "arbitrary"),
        name="gate_proj" if gate else "in_proj",
    )(h, w_t)


def _hy_filter_kernel(z_ref, dl_ref, w1_ref, b1_ref, w2_ref, b2_ref, w3_ref, fwd_ref,
                      kre_ref, kim_ref, *, L):
    z = z_ref[...]
    a = jnp.sin(_dot_split(z, w1_ref[...]) + b1_ref[...])
    a = jnp.sin(_dot_split(a, w2_ref[...]) + b2_ref[...])
    h = _dot_split(a, w3_ref[...])
    decay = jnp.exp(-z[:, 0:1] * dl_ref[...])
    not_first = lax.broadcasted_iota(jnp.int32, (L, HY_W), 0) > 0
    sums, diffs = [], []
    for o in range(2):
        fw = h[:, (2 * o) * HY_W:(2 * o + 1) * HY_W] * decay
        bw = jnp.where(not_first, h[:, (2 * o + 1) * HY_W:(2 * o + 2) * HY_W] * decay, 0.0)
        sums.append(fw + bw)
        diffs.append(fw - bw)
    p = _dot(fwd_ref[...], jnp.concatenate(sums, 1))
    q = _dot(fwd_ref[L:2 * L, :], jnp.concatenate(diffs, 1))
    kre_ref[...] = p[0:L]
    first = lax.broadcasted_iota(jnp.int32, (L, 2 * HY_W), 0) == 0
    kim_ref[...] = jnp.where(first, p[L:L + 1], q)


def _hy_filters(L, w1p, b1, w2, b2, w3, fwd):
    zemb, deltas = _hyena_embedding(L)
    out = jax.ShapeDtypeStruct((L, 2 * HY_W), F32)
    return pl.pallas_call(
        functools.partial(_hy_filter_kernel, L=L),
        out_shape=(out, out),
        compiler_params=pltpu.CompilerParams(vmem_limit_bytes=VMEM_LIMIT),
        name=f"hy_filters_{L}",
    )(jnp.asarray(zemb), jnp.asarray(deltas), w1p, b1, w2, b2, w3, fwd)


def _group_step(ctx_body, lat_body):
    i = pl.program_id(0)
    pl.when(i < CTX_STEPS)(ctx_body)
    pl.when(i >= CTX_STEPS)(lat_body)


def _lat_index(i):
    return jnp.maximum(i - CTX_STEPS, 0)


def _hyena_kernel(hy_ref, cw_ref, cb_ref, bias_ref, kre_c, kim_c, fwd_c, inv_c,
                  kre_l, kim_l, fwd_l, inv_l, o_ref):
    _group_step(
        lambda: _hyena_body(hy_ref, cw_ref, cb_ref, bias_ref, kre_c, kim_c, fwd_c, inv_c, o_ref,
                            SEQ, CTX_SEQS_PER_STEP),
        lambda: _hyena_body(hy_ref, cw_ref, cb_ref, bias_ref, kre_l, kim_l, fwd_l, inv_l, o_ref,
                            DEC_SEQ, 1))


def _hyena_body(hy_ref, cw_ref, cb_ref, bias_ref, kre_ref, kim_ref, fwd_ref, inv_ref, o_ref, L, seqs):
    first = lax.broadcasted_iota(jnp.int32, (L, HY_W), 0) == 0

    def long_conv(u, o):
        uf = _dot(fwd_ref[...], u)
        ure, uim = uf[0:L], uf[L:2 * L]
        kre = kre_ref[:, o * HY_W:(o + 1) * HY_W]
        kim = kim_ref[:, o * HY_W:(o + 1) * HY_W]
        yre = jnp.where(first, ure * kre, ure * kre - uim * kim)
        yim = jnp.where(first, uim * kim, ure * kim + uim * kre)
        y = _dot(inv_ref[...], jnp.concatenate([yre, yim], 0))
        return y + u * bias_ref[o:o + 1, :]

    for g in range(seqs):
        sl = slice(g * L, (g + 1) * L)
        x = hy_ref[sl, :].astype(F32)
        rows = lax.broadcasted_iota(jnp.int32, x.shape, 0)
        prev = jnp.where(rows == 0, 0.0, pltpu.roll(x, 1, axis=0))
        nxt = jnp.where(rows == L - 1, 0.0, pltpu.roll(x, L - 1, axis=0))
        z = prev * cw_ref[0:1, :] + x * cw_ref[1:2, :] + nxt * cw_ref[2:3, :] + cb_ref[...]
        v, x1, x2 = z[:, 0:HY_W], z[:, HY_W:2 * HY_W], z[:, 2 * HY_W:3 * HY_W]
        u = x1 * long_conv(v, 0)
        o_ref[sl, :] = (x2 * long_conv(u, 1)).astype(o_ref.dtype)


def _const_spec(shape):
    return pl.BlockSpec(shape, lambda i: (0,) * len(shape))


def _hyena(z_main, layer, conv_w, conv_b, bias, filters, dft):
    tables, table_specs = [], []
    for L in (SEQ, DEC_SEQ):
        tables += [*filters[L], *dft[L]]
        table_specs += [_const_spec((L, 2 * HY_W)), _const_spec((L, 2 * HY_W)),
                        _const_spec((2 * L, L)), _const_spec((L, 2 * L))]
    return pl.pallas_call(
        _hyena_kernel,
        out_shape=jax.ShapeDtypeStruct((T_ALL, HY_W), BF16),
        grid=(MIXER_STEPS,),
        in_specs=[
            pl.BlockSpec((STEP_ROWS, 3 * HY_W), lambda i: (i, 0)),
            pl.BlockSpec((None, 3, 3 * HY_W), lambda i: (layer, 0, 0)),
            pl.BlockSpec((None, 1, 3 * HY_W), lambda i: (layer, 0, 0)),
            pl.BlockSpec((None, 2, HY_W), lambda i: (layer, 0, 0)),
        ] + table_specs,
        out_specs=pl.BlockSpec((STEP_ROWS, HY_W), lambda i: (i, 0)),
        compiler_params=_params("parallel"),
        name="hyena",
    )(z_main, conv_w, conv_b.reshape(DEPTH, 1, 3 * HY_W), bias, *tables)


def _win_masks():
    lane = lax.broadcasted_iota(jnp.int32, (1, LANE), 1)
    return lane < WIN_HD, lane >= WIN_HD


def _win_head_operands(q, k, v, h):
    lo_mask, hi_mask = _win_masks()
    col = h // 2
    lo = h % 2 == 0
    q128 = jnp.where(lo_mask if lo else hi_mask, q[:, col * LANE:(col + 1) * LANE], 0.0)
    swap = h in (1, 2)
    if swap:
        k = pltpu.roll(k, WIN_HD, axis=1)
        v = pltpu.roll(v, WIN_HD, axis=1)
    return q128, k, v, lo


def _win_kernel(sink_ref, q_ref, kv_ref, ck_ref, cv_ref, c_ref, su_ref, sd_ref, o_ref, *, layer):
    _group_step(
        lambda: _win_ctx_body(sink_ref, q_ref, kv_ref, o_ref, layer),
        lambda: _win_lat_body(sink_ref, q_ref, kv_ref, ck_ref, cv_ref, c_ref, su_ref, sd_ref, o_ref, layer))


WIN_PAIRS = ((0, 3), (1, 2))


def _win_pair_operands(q, k, v, pair):
    qa, kk, vv, _ = _win_head_operands(q, k, v, pair[0])
    qb, _, _, _ = _win_head_operands(q, k, v, pair[1])
    return qa, qb, kk, vv


def _win_pair_sink(sink_ref, layer, pair, rows):
    return jnp.concatenate([jnp.full((rows, 1), sink_ref[layer, h] * LOG2E, F32) for h in pair], 0)


def _win_scatter_heads(o, pair, rows, col_acc):
    lo_mask, hi_mask = _win_masks()
    for i, h in enumerate(pair):
        oh = jnp.where(lo_mask if h % 2 == 0 else hi_mask, o[i * rows:(i + 1) * rows], 0.0)
        col_acc[h // 2] = oh if col_acc[h // 2] is None else col_acc[h // 2] + oh


def _win_ctx_body(sink_ref, q_ref, kv_ref, o_ref, layer):
    qscale = WIN_HD ** -0.5 * LOG2E
    for g in range(CTX_SEQS_PER_STEP):
        sl = slice(g * SEQ, (g + 1) * SEQ)
        q = q_ref[sl, :].astype(F32) * qscale
        k = kv_ref[sl, 0:LANE].astype(F32)
        v = kv_ref[sl, LANE:2 * LANE].astype(F32)
        col_acc = [None, None]
        for pair in WIN_PAIRS:
            qa, qb, kk, vv = _win_pair_operands(q, k, v, pair)
            s = _dot_nt(jnp.concatenate([qa, qb], 0), kk)
            sink = _win_pair_sink(sink_ref, layer, pair, SEQ)
            m = jnp.maximum(jnp.max(s, -1, keepdims=True), sink)
            p = jnp.exp2(s - m)
            den = jnp.sum(p, -1, keepdims=True) + jnp.exp2(sink - m)
            _win_scatter_heads(_dot(p, vv) / den, pair, SEQ, col_acc)
        o_ref[sl, :] = jnp.concatenate(col_acc, 1).astype(o_ref.dtype)


def _win_lat_body(sink_ref, q_ref, kv_ref, ck_ref, cv_ref, c_ref, su_ref, sd_ref, o_ref, layer):
    L = DEC_SEQ
    half = WIN_HD // 2
    c, su, sd = c_ref[...], su_ref[...], sd_ref[...]
    qscale = WIN_HD ** -0.5 * LOG2E
    q = jnp.concatenate(
        [_rope128(q_ref[:, i * LANE:(i + 1) * LANE].astype(F32), c, su, sd, half) for i in range(2)],
        1) * qscale
    k = _rope128(kv_ref[:, 0:LANE].astype(F32), c, su, sd, half)
    v = kv_ref[:, LANE:2 * LANE].astype(F32)
    ck = ck_ref[...]
    cv = cv_ref[...]
    nb = L // CHUNK
    assert WINDOW == CHUNK
    rr = lax.broadcasted_iota(jnp.int32, (CHUNK, CHUNK), 0)
    cc = lax.broadcasted_iota(jnp.int32, (CHUNK, CHUNK), 1)
    band = {-1: jnp.where(cc >= rr, 0.0, NEG), 0: jnp.zeros((CHUNK, CHUNK), F32),
            1: jnp.where(cc <= rr, 0.0, NEG)}
    col_acc = [[None, None] for _ in range(nb)]
    for pair in WIN_PAIRS:
        qa, qb, kk, vv = _win_pair_operands(q, k, v, pair)
        _, _, ckk, cvv = _win_pair_operands(q, ck, cv, pair)
        sink = _win_pair_sink(sink_ref, layer, pair, CHUNK)
        for n in range(nb):
            blocks = [d for d in (-1, 0, 1) if 0 <= n + d < nb]
            k0 = (n + blocks[0]) * CHUNK
            k1 = (n + blocks[-1] + 1) * CHUNK
            rows = slice(n * CHUNK, (n + 1) * CHUNK)
            qn = jnp.concatenate([qa[rows], qb[rows]], 0)
            mask = jnp.concatenate([band[d] for d in blocks], 1)
            s_loc = _dot_nt(qn, kk[k0:k1]) + jnp.concatenate([mask, mask], 0)
            s_ctx = _dot_nt(qn, ckk)
            m = jnp.maximum(jnp.maximum(jnp.max(s_loc, -1, keepdims=True),
                                        jnp.max(s_ctx, -1, keepdims=True)), sink)
            p_loc = jnp.exp2(s_loc - m)
            p_ctx = jnp.exp2(s_ctx - m)
            den = (jnp.sum(p_loc, -1, keepdims=True) + jnp.sum(p_ctx, -1, keepdims=True)
                   + jnp.exp2(sink - m))
            o = (_dot(p_loc, vv[k0:k1]) + _dot(p_ctx, cvv)) / den
            _win_scatter_heads(o, pair, CHUNK, col_acc[n])
    o_ref[...] = jnp.concatenate([jnp.concatenate(cols, 1) for cols in col_acc], 0).astype(o_ref.dtype)


def _win(z_main, sink, cache_k, cache_v, layer):
    c, su, sd, _ = _rope_tables(DEC_SEQ, WIN_HD, LANE)
    tab = _const_spec((DEC_SEQ, LANE))
    cache = pl.BlockSpec((None, None, PAST_LEN, LANE), lambda i: (_lat_index(i), layer, 0, 0))
    return pl.pallas_call(
        functools.partial(_win_kernel, layer=layer),
        out_shape=jax.ShapeDtypeStruct((T_ALL, WIN_HEADS * WIN_HD), BF16),
        grid=(MIXER_STEPS,),
        in_specs=[
            pl.BlockSpec(memory_space=pltpu.SMEM),
            pl.BlockSpec((STEP_ROWS, 256), lambda i: (i, COL_WQ // 256)),
            pl.BlockSpec((STEP_ROWS, 256), lambda i: (i, COL_WK // 256)),
            cache, cache, tab, tab, tab,
        ],
        out_specs=pl.BlockSpec((STEP_ROWS, 256), lambda i: (i, 0)),
        compiler_params=_params("parallel"),
        name="win",
    )(sink, z_main, z_main, cache_k, cache_v, jnp.asarray(c), jnp.asarray(su), jnp.asarray(sd))


def _ret_kernel(*refs, layer):
    _group_step(lambda: _ret_body(*refs, L=SEQ, layer=layer, ctx=True),
                lambda: _ret_body(*refs, L=DEC_SEQ, layer=layer, ctx=False))


def _ret_body(df_ref, db_ref, q_ref, k_ref, v0_ref, v1_ref, g0_ref, g1_ref, s0f_ref, s0b_ref,
              o_ref, sf_out, sb_out, s_ref, cross_ref, *, L, layer, ctx):
    seqs = STEP_ROWS // L
    if not ctx:
        sf_out[...] = jnp.zeros_like(sf_out)
        sb_out[...] = jnp.zeros_like(sb_out)
    C = CHUNK
    nc = L // C
    H = RET_HEADS
    qw = H * RET_DK
    vw = H * RET_DV

    def lane_table(width, per_head, fn):
        pos = lax.broadcasted_iota(jnp.int32, (C, per_head), 0).astype(F32)
        return jnp.concatenate([fn(h, pos) for h in range(H)], 1)

    def log_gamma(ref, h):
        d = jnp.full((1, 1), ref[layer, h], F32)
        return jnp.log(jax.nn.sigmoid(d))

    lgf = [log_gamma(df_ref, h) for h in range(H)]
    lgb = [log_gamma(db_ref, h) for h in range(H)]

    def tables(lg, reverse):
        if reverse:
            dq = lane_table(vw, RET_DV, lambda h, pos: jnp.exp((C - pos) * lg[h]))
            dk = lane_table(qw, RET_DK, lambda h, pos: jnp.exp(pos * lg[h]))
        else:
            dq = lane_table(vw, RET_DV, lambda h, pos: jnp.exp((pos + 1.0) * lg[h]))
            dk = lane_table(qw, RET_DK, lambda h, pos: jnp.exp((C - 1.0 - pos) * lg[h]))
        dc = jnp.concatenate([jnp.broadcast_to(jnp.exp(C * lg[h]), (1, RET_DV)) for h in range(H)], 1)
        return dq, dk, dc

    tab_f = tables(lgf, False)
    tab_b = tables(lgb, True)
    ii = lax.broadcasted_iota(jnp.int32, (C, C), 0)
    jj = lax.broadcasted_iota(jnp.int32, (C, C), 1)
    diff = (ii - jj).astype(F32)
    dmats = [jnp.where(diff >= 0, jnp.exp(jnp.maximum(diff, 0.0) * lgf[h]), 0.0)
             + jnp.where(diff <= 0, jnp.exp(jnp.maximum(-diff, 0.0) * lgb[h]), 0.0) for h in range(H)]
    lane_q = lax.broadcasted_iota(jnp.int32, (1, qw), 1) // RET_DK

    srow = lax.broadcasted_iota(jnp.int32, (qw, vw), 0) // RET_DK
    scol = lax.broadcasted_iota(jnp.int32, (qw, vw), 1) // RET_DV
    diag = srow == scol

    for g in range(seqs):
        base = g * L
        rows_all = slice(base, base + L)
        q_all = q_ref[rows_all, :].astype(F32)
        k_all = k_ref[rows_all, :].astype(F32) * (RET_DK ** -0.5)
        v_all = jnp.concatenate([v0_ref[rows_all, :], v1_ref[rows_all, :]], 1).astype(F32)
        g_all = jnp.concatenate([g0_ref[rows_all, :], g1_ref[rows_all, :]], 1).astype(F32)

        def scan(tabs, reverse, s0_ref, s_out):
            dq, dk, dc = tabs
            s_ref[g] = jnp.zeros((qw, vw), F32)
            if s0_ref is not None:
                for h in range(H):
                    s_ref[g, h * RET_DK:(h + 1) * RET_DK, h * RET_DV:(h + 1) * RET_DV] = s0_ref[h]
            order = range(nc - 1, -1, -1) if reverse else range(nc)
            for ci in order:
                sl = slice(ci * C, (ci + 1) * C)
                rs = slice(base + ci * C, base + (ci + 1) * C)
                qc, kc, vc = q_all[sl], k_all[sl], v_all[sl]
                st = s_ref[g]
                cross = _dot(qc, st) * dq
                if reverse:
                    cross_ref[rs, :] = cross_ref[rs, :] + cross
                else:
                    cross_ref[rs, :] = cross
                upd = jnp.where(diag, _dot_tn(kc * dk, vc), 0.0)
                s_ref[g] = st * dc + upd
            if s_out is not None:
                for h in range(H):
                    s_out[g, h] = s_ref[g, h * RET_DK:(h + 1) * RET_DK, h * RET_DV:(h + 1) * RET_DV]

        scan(tab_f, False, None if ctx else s0f_ref, sf_out if ctx else None)
        scan(tab_b, True, None if ctx else s0b_ref, sb_out if ctx else None)

        for h in range(H):
            hv = slice(h * RET_DV, (h + 1) * RET_DV)
            for ci in range(nc):
                sl = slice(ci * C, (ci + 1) * C)
                qh = jnp.where(lane_q == h, q_all[sl], 0.0)
                att = _dot_nt(qh, k_all[sl]) * dmats[h]
                o = _dot(att, v_all[sl, hv]) + cross_ref[base + ci * C:base + (ci + 1) * C, hv]
                gt = g_all[sl, hv]
                o_ref[base + ci * C:base + (ci + 1) * C, hv] = (
                    (gt * jax.nn.sigmoid(gt)) * _layer_norm(o)).astype(o_ref.dtype)


def _retention(z_main, dec_f, dec_b, s0f, s0b, layer):
    def zcol(col):
        return pl.BlockSpec((STEP_ROWS, 256), lambda i: (i, col // 256))

    smem = pl.BlockSpec(memory_space=pltpu.SMEM)
    z_specs = [zcol(COL_RQ), zcol(COL_RK), zcol(COL_RV), zcol(COL_RV + 256),
               zcol(COL_RG), zcol(COL_RG + 256)]
    s0_spec = pl.BlockSpec((None, None, RET_HEADS, RET_DK, RET_DV),
                           lambda i: (_lat_index(i), layer, 0, 0, 0))
    st_shape = jax.ShapeDtypeStruct((MIXER_STEPS * CTX_SEQS_PER_STEP, RET_HEADS, RET_DK, RET_DV), F32)
    st_spec = pl.BlockSpec((CTX_SEQS_PER_STEP, RET_HEADS, RET_DK, RET_DV), lambda i: (i, 0, 0, 0))
    return pl.pallas_call(
        functools.partial(_ret_kernel, layer=layer),
        out_shape=(jax.ShapeDtypeStruct((T_ALL, RET_HEADS * RET_DV), BF16), st_shape, st_shape),
        grid=(MIXER_STEPS,),
        in_specs=[smem, smem] + z_specs + [s0_spec, s0_spec],
        out_specs=(pl.BlockSpec((STEP_ROWS, RET_HEADS * RET_DV), lambda i: (i, 0)), st_spec, st_spec),
        scratch_shapes=[pltpu.VMEM((CTX_SEQS_PER_STEP, RET_HEADS * RET_DK, RET_HEADS * RET_DV), F32),
                        pltpu.VMEM((STEP_ROWS, RET_HEADS * RET_DV), F32)],
        compiler_params=_params("parallel"),
        name="retention",
    )(dec_f, dec_b, *([z_main] * 6), s0f, s0b)


def _rms_norm(x, g):
    return x * lax.rsqrt(jnp.mean(x * x, -1, keepdims=True) + RMS_EPS) * g


def _mla_keys(kn, kr):
    lane_r = lax.broadcasted_iota(jnp.int32, (1, LANE), 1)
    return jnp.concatenate([kn, jnp.where(lane_r < MLA_ROPE, kr, 0.0)], 1).astype(BF16)


def _mla_attend(qn, qr, k_cat, vv, o_ref, row0):
    qscale = (MLA_NOPE + MLA_ROPE) ** -0.5 * LOG2E
    lane_n = lax.broadcasted_iota(jnp.int32, (1, MLA_HEADS * MLA_NOPE), 1) // MLA_NOPE
    lane_r = lax.broadcasted_iota(jnp.int32, (1, LANE), 1)
    qn = qn * qscale
    qr = qr * qscale
    lq = qn.shape[0]
    heads = []
    for h in range(MLA_HEADS):
        qnh = jnp.where(lane_n == h, qn, 0.0)
        qrh = qr if h == 0 else pltpu.roll(qr, LANE - h * MLA_ROPE, axis=1)
        qrh = jnp.where(lane_r < MLA_ROPE, qrh, 0.0)
        heads.append(jnp.concatenate([qnh, qrh], 1).astype(BF16))
    s = _dot_nt(jnp.concatenate(heads, 0), k_cat)
    m = jnp.max(s, -1, keepdims=True)
    p = jnp.exp2(s - m)
    den = jnp.sum(p, -1, keepdims=True)
    o = _dot(p, vv) / den
    acc = None
    for h in range(MLA_HEADS):
        oh = jnp.where(lane_n == h, o[h * lq:(h + 1) * lq], 0.0)
        acc = oh if acc is None else acc + oh
    o_ref[row0:row0 + lq, :] = acc.astype(o_ref.dtype)


def _mla_kernel(cq_ref, ckv_ref, kr_ref, cckv_ref, ckr_ref, c_ref, su_ref, sd_ref,
                qg_ref, kg_ref, wqn_ref, wqr_ref, wk_ref, wv_ref, o_ref, ckvn_ref):
    weights = (qg_ref, kg_ref, wqn_ref, wqr_ref, wk_ref, wv_ref)
    _group_step(
        lambda: _mla_ctx_body(cq_ref, ckv_ref, kr_ref, *weights, o_ref, ckvn_ref),
        lambda: _mla_lat_body(cq_ref, ckv_ref, kr_ref, cckv_ref, ckr_ref, c_ref, su_ref, sd_ref,
                              *weights, o_ref, ckvn_ref))


def _mla_ctx_body(cq_ref, ckv_ref, kr_ref, qg_ref, kg_ref, wqn_ref, wqr_ref, wk_ref, wv_ref,
                  o_ref, ckvn_ref):
    cqn = _rms_norm(cq_ref[...].astype(F32), qg_ref[...])
    qn = _dot(cqn, wqn_ref[...])
    qr = _dot(cqn, wqr_ref[...])
    ckvn = _rms_norm(ckv_ref[...].astype(F32), kg_ref[...])
    ckvn_ref[...] = ckvn
    k_cat = _mla_keys(_dot(ckvn, wk_ref[...]), kr_ref[...].astype(F32))
    vv = _dot(ckvn, wv_ref[...]).astype(BF16)
    for g in range(CTX_SEQS_PER_STEP):
        sl = slice(g * SEQ, (g + 1) * SEQ)
        _mla_attend(qn[sl], qr[sl], k_cat[sl], vv[sl], o_ref, g * SEQ)


def _mla_lat_body(cq_ref, ckv_ref, kr_ref, cckv_ref, ckr_ref, c_ref, su_ref, sd_ref,
                  qg_ref, kg_ref, wqn_ref, wqr_ref, wk_ref, wv_ref, o_ref, ckvn_ref):
    half = MLA_ROPE // 2
    c, su, sd = c_ref[...], su_ref[...], sd_ref[...]
    cqn = _rms_norm(cq_ref[...].astype(F32), qg_ref[...])
    qn = _dot(cqn, wqn_ref[...])
    qr = _rope128(_dot(cqn, wqr_ref[...]), c, su, sd, half)
    ckvn = _rms_norm(ckv_ref[...].astype(F32), kg_ref[...])
    ckvn_ref[...] = ckvn
    ckv_all = jnp.concatenate([ckvn, cckv_ref[...]], 0)
    vv = _dot(ckv_all, wv_ref[...]).astype(BF16)
    kr = jnp.concatenate([_rope128(kr_ref[...].astype(F32), c, su, sd, half), ckr_ref[...]], 0)
    k_cat = _mla_keys(_dot(ckv_all, wk_ref[...]), kr)
    rows_per_call = 256
    for n in range(DEC_SEQ // rows_per_call):
        rows = slice(n * rows_per_call, (n + 1) * rows_per_call)
        _mla_attend(qn[rows], qr[rows], k_cat, vv, o_ref, n * rows_per_call)


def _mla(z_main, cache_ckv, cache_kr_pad, weights, layer):
    c, su, sd, _ = _rope_tables(DEC_SEQ, MLA_ROPE, LANE)
    tab = _const_spec((DEC_SEQ, LANE))
    cache = pl.BlockSpec((None, None, PAST_LEN, LANE), lambda i: (_lat_index(i), layer, 0, 0))

    def weight(*shape):
        return pl.BlockSpec((None,) + shape, lambda i: (layer, 0, 0))

    return pl.pallas_call(
        _mla_kernel,
        out_shape=(jax.ShapeDtypeStruct((T_ALL, MLA_HEADS * MLA_V), BF16),
                   jax.ShapeDtypeStruct((T_ALL, MLA_KV_LORA), F32)),
        grid=(MIXER_STEPS,),
        in_specs=[
            pl.BlockSpec((STEP_ROWS, 256), lambda i: (i, COL_CQ // 256)),
            pl.BlockSpec((STEP_ROWS, LANE), lambda i: (i, COL_CKV // LANE)),
            pl.BlockSpec((STEP_ROWS, LANE), lambda i: (i, COL_KROPE // LANE)),
            cache, cache, tab, tab, tab,
            weight(1, MLA_Q_LORA), weight(1, MLA_KV_LORA),
            weight(MLA_Q_LORA, MLA_HEADS * MLA_NOPE), weight(MLA_Q_LORA, MLA_HEADS * MLA_ROPE),
            weight(MLA_KV_LORA, MLA_HEADS * MLA_NOPE), weight(MLA_KV_LORA, MLA_HEADS * MLA_V),
        ],
        out_specs=(pl.BlockSpec((STEP_ROWS, 256), lambda i: (i, 0)),
                   pl.BlockSpec((STEP_ROWS, MLA_KV_LORA), lambda i: (i, 0))),
        compiler_params=_params("parallel"),
        name="mla",
    )(z_main, z_main, z_main, cache_ckv, cache_kr_pad,
      jnp.asarray(c), jnp.asarray(su), jnp.asarray(sd), *weights)


def _route(logits_t, rb):
    scores = jax.nn.sigmoid(logits_t)
    biased = scores + rb
    sc = [scores[e:e + 1, :] for e in range(N_EXPERTS)]
    bi = [biased[e:e + 1, :] for e in range(N_EXPERTS)]
    epg = EXPERTS_PER_GROUP
    gsum = []
    for g in range(N_GROUPS):
        v = bi[g * epg:(g + 1) * epg]
        best = None
        for a in range(epg):
            for b in range(a + 1, epg):
                pair = v[a] + v[b]
                best = pair if best is None else jnp.maximum(best, pair)
        gsum.append(best)
    combine = []
    sel = []
    for g in range(N_GROUPS):
        is_best = None
        for g2 in range(N_GROUPS):
            if g2 == g:
                continue
            c = gsum[g] > gsum[g2] if g2 < g else gsum[g] >= gsum[g2]
            is_best = c if is_best is None else jnp.logical_and(is_best, c)
        for a in range(epg):
            e = g * epg + a
            rank = jnp.zeros_like(bi[e])
            for b in range(epg):
                if b == a:
                    continue
                e2 = g * epg + b
                ahead = bi[e2] >= bi[e] if b < a else bi[e2] > bi[e]
                rank = rank + jnp.where(ahead, 1.0, 0.0)
            sel.append(jnp.logical_and(is_best, rank < 2.0))
    wsum = None
    for e in range(N_EXPERTS):
        w = jnp.where(sel[e], sc[e], 0.0)
        wsum = w if wsum is None else wsum + w
    for e in range(N_EXPERTS):
        combine.append(jnp.where(sel[e], ROUTE_SCALE * sc[e] / wsum, 0.0))
    return jnp.concatenate(combine, 0)


MERGE_BRANCH_ROWS = (HY_W, WIN_HEADS * WIN_HD, RET_HEADS * RET_DV, MLA_HEADS * MLA_V)
MERGE_ROWS = sum(MERGE_BRANCH_ROWS) + D_MODEL


def _merge_kernel(ya_ref, yb_ref, yc_ref, yd_ref, gt_ref, xc_ref, xl_ref, m_ref,
                  w_ref, g_ref, b_ref, rw_ref, rb_ref,
                  x1_ref, h2_ref, cmb_ref, *, ctx_tiles, sub_rows):
    D = D_MODEL
    rw = rw_ref[...]
    rw_hi = rw.astype(BF16)
    rw_lo = (rw - rw_hi.astype(F32)).astype(BF16)
    g1 = m_ref[:, 2 * D:3 * D]
    s2 = m_ref[:, 3 * D:4 * D]
    sc2 = m_ref[:, 4 * D:5 * D]
    is_ctx = pl.program_id(0) < ctx_tiles
    offs = np.cumsum((0,) + MERGE_BRANCH_ROWS)
    branches = tuple((y_ref, slice(int(offs[i]), int(offs[i + 1])))
                     for i, y_ref in enumerate((ya_ref, yb_ref, yc_ref, yd_ref)))
    w_out_rows = slice(int(offs[-1]), MERGE_ROWS)
    for r0 in range(0, x1_ref.shape[0], sub_rows):
        rows = slice(r0, r0 + sub_rows)
        merged = None
        for i, (y_ref, w_rows) in enumerate(branches):
            t = gt_ref[rows, i * D:(i + 1) * D] * jnp.dot(
                y_ref[rows, :], w_ref[w_rows, :], preferred_element_type=F32).astype(BF16)
            merged = t if merged is None else merged + t
        out1 = jnp.dot(merged, w_ref[w_out_rows, :], preferred_element_type=F32)
        x = jnp.where(is_ctx, xc_ref[rows, :], xl_ref[rows, :])
        x1 = _layer_norm(ALPHA * x + g1 * out1) * g_ref[...] + b_ref[...]
        x1_ref[rows, :] = x1
        h2 = _layer_norm(x1) * (1.0 + sc2) + s2
        h2_hi = h2.astype(BF16)
        h2_ref[rows, :] = h2_hi
        h2_lo = (h2 - h2_hi.astype(F32)).astype(BF16)
        logits = (jnp.dot(h2_hi, rw_hi, preferred_element_type=F32)
                  + (jnp.dot(h2_lo, rw_hi, preferred_element_type=F32)
                     + jnp.dot(h2_hi, rw_lo, preferred_element_type=F32)))
        cmb_ref[:, rows] = _route(logits.T[0:N_EXPERTS], rb_ref[...])


def _merge(ya, yb, yc, yd, gates, x_ctx, x_lat, mods, w_merge, ln1_g, ln1_b,
           router_w, router_b, layer):
    tm = 512
    row = _mod_row(tm)
    D = D_MODEL
    ctx_tiles = T_CTX // tm

    def tile(w):
        return pl.BlockSpec((tm, w), lambda i: (i, 0))

    def weight(k, n):
        return pl.BlockSpec((None, k, n), lambda i: (layer, 0, 0))

    return pl.pallas_call(
        functools.partial(_merge_kernel, ctx_tiles=ctx_tiles, sub_rows=256),
        out_shape=(jax.ShapeDtypeStruct((T_ALL, D), F32),
                   jax.ShapeDtypeStruct((T_ALL, D), BF16),
                   jax.ShapeDtypeStruct((N_EXPERTS, T_ALL), F32)),
        grid=(T_ALL // tm,),
        in_specs=[
            tile(256), tile(256), tile(512), tile(256), tile(4 * D),
            pl.BlockSpec((tm, D), lambda i: (jnp.minimum(i, ctx_tiles - 1), 0)),
            pl.BlockSpec((tm, D), lambda i: (jnp.maximum(i - ctx_tiles, 0), 0)),
            _mod_spec(layer, row),
            weight(MERGE_ROWS, D), weight(1, D), weight(1, D),
            pl.BlockSpec((D, LANE), lambda i: (0, 0)),
            pl.BlockSpec((N_EXPERTS, 1), lambda i: (0, 0)),
        ],
        out_specs=(tile(D), tile(D), pl.BlockSpec((N_EXPERTS, tm), lambda i: (0, i))),
        compiler_params=_params("parallel"),
        name="merge",
    )(ya, yb, yc, yd, gates, x_ctx, x_lat, mods, w_merge,
      ln1_g.reshape(DEPTH, 1, D), ln1_b.reshape(DEPTH, 1, D), router_w,
      router_b.reshape(N_EXPERTS, 1))


MOE_EXPERTS_PER_STEP = 2


def _moe_kernel(*refs, next_h):
    if next_h:
        (h_ref, c_ref, x1_ref, m_ref, wg_ref, wu_ref, wd_ref, g_ref, b_ref, mn_ref,
         o_ref, hn_ref, acc_ref) = refs[-13:]
    else:
        h_ref, c_ref, x1_ref, m_ref, wg_ref, wu_ref, wd_ref, g_ref, b_ref, o_ref, acc_ref = refs
    eg = pl.program_id(1)

    @pl.when(eg == 0)
    def _():
        acc_ref[...] = jnp.zeros_like(acc_ref)

    h = h_ref[...]
    cmb = c_ref[...]
    lane = lax.broadcasted_iota(jnp.int32, cmb.shape, 1)
    hid = []
    for k in range(MOE_EXPERTS_PER_STEP):
        gate = jnp.dot(h, wg_ref[k].astype(BF16), preferred_element_type=F32)
        up = jnp.dot(h, wu_ref[k].astype(BF16), preferred_element_type=F32)
        e = eg * MOE_EXPERTS_PER_STEP + k
        ce = jnp.sum(jnp.where(lane == e, cmb, 0.0), -1, keepdims=True)
        sig = 0.5 * jnp.tanh(0.5 * gate) + 0.5
        hid.append((gate * sig * (up * ce)).astype(BF16))
    wd = wd_ref[...].reshape(MOE_EXPERTS_PER_STEP * D_EXPERT, D_MODEL).astype(BF16)
    acc_ref[...] += jnp.dot(jnp.concatenate(hid, 1), wd, preferred_element_type=F32)

    @pl.when(eg == N_EXPERTS // MOE_EXPERTS_PER_STEP - 1)
    def _():
        g2 = m_ref[:, 5 * D_MODEL:6 * D_MODEL]
        y = _layer_norm(ALPHA * x1_ref[...] + g2 * acc_ref[...])
        y = y * g_ref[...] + b_ref[...]
        o_ref[...] = y
        if next_h:
            s1 = mn_ref[:, 0:D_MODEL]
            sc1 = mn_ref[:, D_MODEL:2 * D_MODEL]
            hn_ref[...] = (_layer_norm(y) * (1.0 + sc1) + s1).astype(hn_ref.dtype)


def _moe(h2, combine, x1, mods, w_gate, w_up, w_down, ln2_g, ln2_b, layer, row0, n_rows,
         next_h=False, prev_h=None):
    tm = 1024
    row = _mod_row(tm)
    D = D_MODEL
    t0 = row0 // tm
    eps = MOE_EXPERTS_PER_STEP
    mod_spec = _mod_spec(layer, lambda i, e: row(t0 + i))
    in_specs = [
        pl.BlockSpec((tm, D), lambda i, e: (t0 + i, 0)),
        pl.BlockSpec((tm, N_EXPERTS), lambda i, e: (t0 + i, 0)),
        pl.BlockSpec((tm, D), lambda i, e: (t0 + i, 0)),
        mod_spec,
        pl.BlockSpec((None, eps, D, D_EXPERT), lambda i, e: (layer, e, 0, 0)),
        pl.BlockSpec((None, eps, D, D_EXPERT), lambda i, e: (layer, e, 0, 0)),
        pl.BlockSpec((None, eps, D_EXPERT, D), lambda i, e: (layer, e, 0, 0)),
        pl.BlockSpec((None, 1, D), lambda i, e: (layer, 0, 0)),
        pl.BlockSpec((None, 1, D), lambda i, e: (layer, 0, 0)),
    ]
    args = [h2, combine, x1, mods, w_gate, w_up, w_down,
            ln2_g.reshape(DEPTH, 1, D), ln2_b.reshape(DEPTH, 1, D)]
    out_shape = jax.ShapeDtypeStruct((n_rows, D), F32)
    out_specs = pl.BlockSpec((tm, D), lambda i, e: (i, 0))
    aliases = {}
    if next_h:
        in_specs.append(_mod_spec(layer + 1, lambda i, e: row(t0 + i)))
        args.append(mods)
        out_shape = (out_shape, jax.ShapeDtypeStruct((T_ALL, D), BF16))
        out_specs = (out_specs, pl.BlockSpec((tm, D), lambda i, e: (t0 + i, 0)))
        if prev_h is not None:
            in_specs = [pl.BlockSpec(memory_space=pl.ANY)] + in_specs
            args = [prev_h] + args
            aliases = {0: 1}
    return pl.pallas_call(
        functools.partial(_moe_kernel, next_h=next_h),
        out_shape=out_shape,
        grid=(n_rows // tm, N_EXPERTS // eps),
        in_specs=in_specs,
        out_specs=out_specs,
        input_output_aliases=aliases,
        scratch_shapes=[pltpu.VMEM((tm, D), F32)],
        compiler_params=_params("parallel", "arbitrary"),
        name="moe",
    )(*args)


def kernel(x_prompt, x_sample, cache_win_k, cache_win_v, cache_mla_ckv, cache_mla_krope,
           state_ret_fwd, state_ret_bwd, c, c_ctx, w_ada, b_ada, w_in,
           hy_conv_w, hy_conv_b, hy_w1, hy_b1, hy_w2, hy_b2, hy_w3, hy_bias,
           win_sink, ret_decay_fwd, ret_decay_bwd, mla_q_norm, mla_kv_norm, mla_w_uq, mla_w_ukv,
           w_br_a, w_br_b, w_br_c, w_br_d, w_out, ln1_g, ln1_b, ln2_g, ln2_b,
           router_w, router_b, moe_w_gate, moe_w_up, moe_w_down):
    D = D_MODEL
    x_ctx = x_prompt.reshape(T_CTX, D)
    x_lat = x_sample.reshape(T_LAT, D)

    mods = _ada_mods(c_ctx, c, w_ada, b_ada)

    w_in_t = jnp.swapaxes(w_in, 1, 2).reshape(DEPTH * IN_COLS, D)
    cache_k = cache_win_k.reshape(DEC_BATCH, DEPTH, PAST_LEN, WIN_KV_HEADS * WIN_HD)
    cache_v = cache_win_v.reshape(DEC_BATCH, DEPTH, PAST_LEN, WIN_KV_HEADS * WIN_HD)
    cache_kr = jnp.pad(cache_mla_krope, ((0, 0), (0, 0), (0, 0), (0, LANE - MLA_ROPE)))

    uq = mla_w_uq.reshape(DEPTH, MLA_Q_LORA, MLA_HEADS, MLA_NOPE + MLA_ROPE)
    ukv = mla_w_ukv.reshape(DEPTH, MLA_KV_LORA, MLA_HEADS, MLA_NOPE + MLA_V)
    mla_weights = (
        mla_q_norm.reshape(DEPTH, 1, MLA_Q_LORA),
        mla_kv_norm.reshape(DEPTH, 1, MLA_KV_LORA),
        uq[..., :MLA_NOPE].reshape(DEPTH, MLA_Q_LORA, MLA_HEADS * MLA_NOPE),
        uq[..., MLA_NOPE:].reshape(DEPTH, MLA_Q_LORA, MLA_HEADS * MLA_ROPE),
        ukv[..., :MLA_NOPE].reshape(DEPTH, MLA_KV_LORA, MLA_HEADS * MLA_NOPE),
        ukv[..., MLA_NOPE:].reshape(DEPTH, MLA_KV_LORA, MLA_HEADS * MLA_V),
    )

    hy_w1p = jnp.pad(hy_w1, ((0, 0), (0, LANE - HY_EMB), (0, 0)))
    dft = {}
    for L in (SEQ, DEC_SEQ):
        fwd, inv = _dft_tables(L)
        dft[L] = (jnp.asarray(fwd).astype(BF16), jnp.asarray(inv).astype(BF16))
    router_w_pad = jnp.pad(router_w, ((0, 0), (0, LANE - N_EXPERTS)))
    w_merge = jnp.concatenate([w_br_a, w_br_b, w_br_c, w_br_d, w_out], 1).astype(BF16)

    new_k, new_v, new_ckv, new_kr, new_sf, new_sb = [], [], [], [], [], []
    for l in range(DEPTH):
        if l == 0:
            h = _ln_mod(x_ctx, None, mods, l, 0)
            h = _ln_mod(x_lat, h, mods, l, T_CTX)
        z = _in_proj(h, w_in_t, l, 0, Z_MAIN, Z_MAIN // 2, BF16, gate=False)
        gates = _in_proj(h, w_in_t, l, COL_GATE, 4 * D, D, BF16, gate=True)

        filters = {L: _hy_filters(L, hy_w1p[l], hy_b1[l][None], hy_w2[l], hy_b2[l][None], hy_w3[l],
                                  dft[L][0]) for L in (SEQ, DEC_SEQ)}
        ya = _hyena(z, l, hy_conv_w, hy_conv_b, hy_bias, filters, dft)
        yb = _win(z, win_sink, cache_k, cache_v, l)
        yc, sf, sb = _retention(z, ret_decay_fwd, ret_decay_bwd, state_ret_fwd, state_ret_bwd, l)
        yd, ckvn = _mla(z, cache_mla_ckv, cache_kr, mla_weights, l)

        x1, h2, combine_t = _merge(ya, yb, yc, yd, gates, x_ctx, x_lat, mods, w_merge,
                                   ln1_g, ln1_b, router_w_pad, router_b, l)
        moe_args = (h2, combine_t.T, x1, mods, moe_w_gate, moe_w_up, moe_w_down, ln2_g, ln2_b, l)
        if l + 1 < DEPTH:
            x_ctx, h = _moe(*moe_args, 0, T_CTX, next_h=True)
            x_lat, h = _moe(*moe_args, T_CTX, T_LAT, next_h=True, prev_h=h)
        else:
            x_ctx = _moe(*moe_args, 0, T_CTX)
            x_lat = _moe(*moe_args, T_CTX, T_LAT)

        def ctx_cols(col, width):
            return z[:T_CTX, col:col + width].astype(F32)

        new_k.append(ctx_cols(COL_WK, 128).reshape(BATCH, SEQ, WIN_KV_HEADS, WIN_HD))
        new_v.append(ctx_cols(COL_WV, 128).reshape(BATCH, SEQ, WIN_KV_HEADS, WIN_HD))
        new_ckv.append(ckvn[:T_CTX].reshape(BATCH, SEQ, MLA_KV_LORA))
        new_kr.append(ctx_cols(COL_KROPE, MLA_ROPE).reshape(BATCH, SEQ, MLA_ROPE))
        new_sf.append(sf[:BATCH])
        new_sb.append(sb[:BATCH])

    y_prompt = x_ctx.reshape(BATCH, SEQ, D)
    y_sample = x_lat.reshape(DEC_BATCH, DEC_SEQ, D)
    return (y_prompt, y_sample, jnp.stack(new_k, 1), jnp.stack(new_v, 1), jnp.stack(new_ckv, 1),
            jnp.stack(new_kr, 1), jnp.stack(new_sf, 1), jnp.stack(new_sb, 1))
```

```python
import functools
import math

import numpy as np
import jax
import jax.numpy as jnp
from jax import lax
from jax.experimental import pallas as pl
from jax.experimental.pallas import tpu as pltpu

F32 = jnp.float32
BF16 = jnp.bfloat16

D_MODEL = 1024
BATCH = 16
SEQ = 256
DEPTH = 2
DEC_BATCH = 2
DEC_SEQ = 1024
PAST_LEN = 256
GRID_W = 64
CHUNK = 128
ROPE_BASE = 10000.0
NEG = -1e30
LN_EPS = 1e-5
RMS_EPS = 1e-6
LOG2E = math.log2(math.e)

HY_W = 256
HY_BANDS = 16
HY_EMB = 1 + 2 * HY_BANDS
HY_FFN = 64
HY_FAST_DECAY = 0.3
HY_SLOW_DECAY = 1.5
HY_TARGET = 1e-2

WIN_HEADS = 4
WIN_KV_HEADS = 2
WIN_HD = 64
WINDOW = 128

RET_HEADS = 4
RET_DK = 64
RET_DV = 128

MLA_HEADS = 4
MLA_Q_LORA = 256
MLA_KV_LORA = 128
MLA_NOPE = 64
MLA_ROPE = 32
MLA_V = 64

N_EXPERTS = 16
N_GROUPS = 4
EXPERTS_PER_GROUP = N_EXPERTS // N_GROUPS
D_EXPERT = 256
ROUTE_SCALE = 2.5

ALPHA = (2.0 * DEPTH) ** 0.25

T_CTX = BATCH * SEQ
T_LAT = DEC_BATCH * DEC_SEQ
T_ALL = T_CTX + T_LAT

COL_HY = 0
COL_WQ = 768
COL_WK = 1024
COL_WV = 1152
COL_RQ = 1280
COL_RK = 1536
COL_RV = 1792
COL_RG = 2304
COL_CQ = 2816
COL_CKV = 3072
COL_KROPE = 3200
COL_GATE = 3232
IN_COLS = COL_GATE + 4 * D_MODEL
Z_MAIN = 3328

LANE = 128
STEP_ROWS = 1024
CTX_SEQS_PER_STEP = STEP_ROWS // SEQ
CTX_STEPS = T_CTX // STEP_ROWS
MIXER_STEPS = T_ALL // STEP_ROWS
VMEM_LIMIT = 56 * 1024 * 1024


def _params(*sem):
    return pltpu.CompilerParams(dimension_semantics=sem, vmem_limit_bytes=VMEM_LIMIT)


def _dot(a, b):
    return jnp.dot(a.astype(BF16), b.astype(BF16), preferred_element_type=F32)


def _dot_split(a, b):
    a_hi = a.astype(BF16)
    a_lo = (a - a_hi.astype(F32)).astype(BF16)
    b_hi = b.astype(BF16)
    b_lo = (b - b_hi.astype(F32)).astype(BF16)

    def mm(x, y):
        return jnp.dot(x, y, preferred_element_type=F32)

    return mm(a_hi, b_hi) + (mm(a_lo, b_hi) + mm(a_hi, b_lo))


def _dot_nt(a, b):
    return lax.dot_general(a.astype(BF16), b.astype(BF16), (((1,), (1,)), ((), ())),
                           preferred_element_type=F32)


def _dot_tn(a, b):
    return lax.dot_general(a.astype(BF16), b.astype(BF16), (((0,), (0,)), ((), ())),
                           preferred_element_type=F32)


def _layer_norm(x):
    mu = jnp.mean(x, -1, keepdims=True)
    xc = x - mu
    var = jnp.mean(xc * xc, -1, keepdims=True)
    return xc * lax.rsqrt(var + LN_EPS)


def _mod_row(tile_rows):
    def row(i):
        start = i * tile_rows
        return jnp.where(start < T_CTX, 0, 1 + (start - T_CTX) // DEC_SEQ)
    return row


@functools.lru_cache(maxsize=None)
def _dft_tables(L):
    f = np.arange(L, dtype=np.int64)[:, None]
    s = np.arange(L, dtype=np.int64)[None, :]
    ang = np.pi * ((f * s) % (2 * L)).astype(np.float64) / L
    cos = np.cos(ang)
    sin = np.sin(ang)
    alt = np.where(np.arange(L) % 2 == 0, 1.0, -1.0)
    fwd_im = -sin
    fwd_im[0, :] = alt
    fwd = np.concatenate([cos, fwd_im], 0)
    inv_re = cos.T / L
    inv_re[:, 0] = 1.0 / (2 * L)
    inv_im = -sin.T / L
    inv_im[:, 0] = alt / (2 * L)
    inv = np.concatenate([inv_re, inv_im], 1)
    return fwd.astype(np.float32), inv.astype(np.float32)


@functools.lru_cache(maxsize=None)
def _hyena_embedding(L):
    t01 = np.linspace(0.0, 1.0, L, dtype=np.float64)[:, None]
    bands = np.linspace(1e-4, HY_BANDS - 1, HY_BANDS, dtype=np.float64)
    ang = (2.0 * math.pi / L) * np.arange(L, dtype=np.float64)[:, None] * bands[None, :]
    z = np.concatenate([t01, np.cos(ang), -np.sin(ang)], -1)
    zp = np.zeros((L, LANE), np.float64)
    zp[:, :HY_EMB] = z
    deltas = np.abs(np.linspace(math.log(HY_TARGET) / HY_SLOW_DECAY,
                                math.log(HY_TARGET) / HY_FAST_DECAY, HY_W, dtype=np.float64))
    return zp.astype(np.float32), deltas[None, :].astype(np.float32)


@functools.lru_cache(maxsize=None)
def _rope_tables(L, rot_dim, width):
    rows = L // GRID_W
    n_freq = rot_dim // 4
    half = rot_dim // 2
    inv = ROPE_BASE ** (-np.arange(n_freq, dtype=np.float64) / n_freq)
    pos = np.arange(L)
    row = (pos // GRID_W).astype(np.float64)
    col = (pos % GRID_W).astype(np.float64)
    ang = np.concatenate([row[:, None] * inv, col[:, None] * inv], -1)
    cos, sin = np.cos(ang), np.sin(ang)
    zero = np.zeros_like(sin)
    c = np.tile(np.concatenate([cos, cos], -1), (1, width // rot_dim))
    s_up = np.tile(np.concatenate([-sin, zero], -1), (1, width // rot_dim))
    s_dn = np.tile(np.concatenate([zero, sin], -1), (1, width // rot_dim))
    return c.astype(np.float32), s_up.astype(np.float32), s_dn.astype(np.float32), half


def _rope128(x, c, s_up, s_dn, half):
    up = pltpu.roll(x, LANE - half, axis=1)
    dn = pltpu.roll(x, half, axis=1)
    return x * c + up * s_up + dn * s_dn


def _ada_kernel(cc_ref, cl_ref, w_ref, b_ref, o_ref):
    row = lax.broadcasted_iota(jnp.int32, (ADA_ROWS, 1), 0)
    cv = jnp.where(row == 0, cc_ref[...], 0.0)
    for b in range(DEC_BATCH):
        cv = jnp.where(row == 1 + b, cl_ref[b:b + 1, :], cv)
    s = cv * jax.nn.sigmoid(cv)
    s_hi = s.astype(BF16)
    s_lo = (s - s_hi.astype(F32)).astype(BF16)
    w = w_ref[...]
    w_hi = w.astype(BF16)
    w_lo = (w - w_hi.astype(F32)).astype(BF16)
    rows = s.shape[0]
    both = jnp.dot(jnp.concatenate([s_hi, s_lo], 0), w_hi, preferred_element_type=F32)
    mod = (both[0:rows] + (both[rows:2 * rows] + jnp.dot(s_hi, w_lo, preferred_element_type=F32))
           + b_ref[...])
    for r in range(rows):
        o_ref[r] = mod[r:r + 1]


ADA_ROWS = 16


def _ada_mods(c_ctx, c, w_ada, b_ada):
    tn = 1536
    n = 6 * D_MODEL
    return pl.pallas_call(
        _ada_kernel,
        out_shape=jax.ShapeDtypeStruct((DEPTH, ADA_ROWS, 1, n), F32),
        grid=(DEPTH, n // tn),
        in_specs=[
            pl.BlockSpec((1, D_MODEL), lambda l, j: (0, 0)),
            pl.BlockSpec((DEC_BATCH, D_MODEL), lambda l, j: (0, 0)),
            pl.BlockSpec((None, D_MODEL, tn), lambda l, j: (l, 0, j)),
            pl.BlockSpec((None, 1, tn), lambda l, j: (l, 0, j)),
        ],
        out_specs=pl.BlockSpec((None, ADA_ROWS, 1, tn), lambda l, j: (l, 0, 0, j)),
        compiler_params=_params("parallel", "parallel"),
        name="ada_mods",
    )(c_ctx.reshape(1, D_MODEL), c, w_ada, b_ada.reshape(DEPTH, 1, n))


def _mod_spec(layer, row_of_step):
    return pl.BlockSpec((None, None, 1, 6 * D_MODEL), lambda *g: (layer, row_of_step(*g), 0, 0))


def _lnmod_kernel(*refs):
    x_ref, m_ref, h_ref = refs[-3:]
    y = _layer_norm(x_ref[...])
    s1 = m_ref[:, 0:D_MODEL]
    sc1 = m_ref[:, D_MODEL:2 * D_MODEL]
    h_ref[...] = (y * (1.0 + sc1) + s1).astype(h_ref.dtype)


def _ln_mod(x_group, prev_out, mods, layer, row0):
    tm = 512
    row = _mod_row(tm)
    tile0 = row0 // tm
    in_specs = [
        pl.BlockSpec((tm, D_MODEL), lambda i: (i, 0)),
        _mod_spec(layer, lambda i: row(tile0 + i)),
    ]
    args = [x_group, mods]
    if prev_out is not None:
        in_specs = [pl.BlockSpec(memory_space=pl.ANY)] + in_specs
        args = [prev_out] + args
    return pl.pallas_call(
        _lnmod_kernel,
        out_shape=jax.ShapeDtypeStruct((T_ALL, D_MODEL), BF16),
        grid=(x_group.shape[0] // tm,),
        in_specs=in_specs,
        out_specs=pl.BlockSpec((tm, D_MODEL), lambda i: (tile0 + i, 0)),
        input_output_aliases={} if prev_out is None else {0: 0},
        compiler_params=_params("parallel"),
        name="ln_mod",
    )(*args)


def _proj_kernel(h_ref, w_ref, o_ref, wb_ref, *, gate):
    @pl.when(pl.program_id(1) == 0)
    def _():
        wb_ref[...] = w_ref[...].T.astype(BF16)

    if not gate:
        o_ref[...] = jnp.dot(h_ref[...], wb_ref[...], preferred_element_type=F32).astype(o_ref.dtype)
        return
    sub = 2 * LANE
    for c0 in range(0, o_ref.shape[1], sub):
        r = jnp.dot(h_ref[...], wb_ref[:, c0:c0 + sub], preferred_element_type=F32)
        rb = r.astype(o_ref.dtype)
        o_ref[:, c0:c0 + sub] = 0.5 * jnp.tanh(0.5 * rb) + 0.5


def _in_proj(h, w_t, layer, col0, n_cols, tn, out_dtype, gate):
    tm = 2048
    return pl.pallas_call(
        functools.partial(_proj_kernel, gate=gate),
        out_shape=jax.ShapeDtypeStruct((T_ALL, n_cols), out_dtype),
        grid=(n_cols // tn, T_ALL // tm),
        in_specs=[
            pl.BlockSpec((tm, D_MODEL), lambda j, i: (i, 0)),
            pl.BlockSpec((pl.Element(tn), pl.Element(D_MODEL)),
                         lambda j, i: (pl.multiple_of(layer * IN_COLS + col0 + j * tn, 8), 0)),
        ],
        out_specs=pl.BlockSpec((tm, tn), lambda j, i: (i, j)),
        scratch_shapes=[pltpu.VMEM((D_MODEL, tn), BF16)],
        compiler_params=_params("parallel", "arbitrary"),
        name="gate_proj" if gate else "in_proj",
    )(h, w_t)


def _hy_filter_kernel(z_ref, dl_ref, w1_ref, b1_ref, w2_ref, b2_ref, w3_ref, fwd_ref,
                      kre_ref, kim_ref, *, L):
    z = z_ref[...]
    a = jnp.sin(_dot_split(z, w1_ref[...]) + b1_ref[...])
    a = jnp.sin(_dot_split(a, w2_ref[...]) + b2_ref[...])
    h = _dot_split(a, w3_ref[...])
    decay = jnp.exp(-z[:, 0:1] * dl_ref[...])
    not_first = lax.broadcasted_iota(jnp.int32, (L, HY_W), 0) > 0
    sums, diffs = [], []
    for o in range(2):
        fw = h[:, (2 * o) * HY_W:(2 * o + 1) * HY_W] * decay
        bw = jnp.where(not_first, h[:, (2 * o + 1) * HY_W:(2 * o + 2) * HY_W] * decay, 0.0)
        sums.append(fw + bw)
        diffs.append(fw - bw)
    p = _dot(fwd_ref[...], jnp.concatenate(sums, 1))
    q = _dot(fwd_ref[L:2 * L, :], jnp.concatenate(diffs, 1))
    kre_ref[...] = p[0:L]
    first = lax.broadcasted_iota(jnp.int32, (L, 2 * HY_W), 0) == 0
    kim_ref[...] = jnp.where(first, p[L:L + 1], q)


def _hy_filters(L, w1p, b1, w2, b2, w3, fwd):
    zemb, deltas = _hyena_embedding(L)
    out = jax.ShapeDtypeStruct((L, 2 * HY_W), F32)
    return pl.pallas_call(
        functools.partial(_hy_filter_kernel, L=L),
        out_shape=(out, out),
        compiler_params=pltpu.CompilerParams(vmem_limit_bytes=VMEM_LIMIT),
        name=f"hy_filters_{L}",
    )(jnp.asarray(zemb), jnp.asarray(deltas), w1p, b1, w2, b2, w3, fwd)


def _group_step(ctx_body, lat_body):
    i = pl.program_id(0)
    pl.when(i < CTX_STEPS)(ctx_body)
    pl.when(i >= CTX_STEPS)(lat_body)


def _lat_index(i):
    return jnp.maximum(i - CTX_STEPS, 0)


def _hyena_kernel(hy_ref, cw_ref, cb_ref, bias_ref, kre_c, kim_c, fwd_c, inv_c,
                  kre_l, kim_l, fwd_l, inv_l, o_ref):
    _group_step(
        lambda: _hyena_body(hy_ref, cw_ref, cb_ref, bias_ref, kre_c, kim_c, fwd_c, inv_c, o_ref,
                            SEQ, CTX_SEQS_PER_STEP),
        lambda: _hyena_body(hy_ref, cw_ref, cb_ref, bias_ref, kre_l, kim_l, fwd_l, inv_l, o_ref,
                            DEC_SEQ, 1))


def _hyena_body(hy_ref, cw_ref, cb_ref, bias_ref, kre_ref, kim_ref, fwd_ref, inv_ref, o_ref, L, seqs):
    first = lax.broadcasted_iota(jnp.int32, (L, HY_W), 0) == 0

    def long_conv(u, o):
        uf = _dot(fwd_ref[...], u)
        ure, uim = uf[0:L], uf[L:2 * L]
        kre = kre_ref[:, o * HY_W:(o + 1) * HY_W]
        kim = kim_ref[:, o * HY_W:(o + 1) * HY_W]
        yre = jnp.where(first, ure * kre, ure * kre - uim * kim)
        yim = jnp.where(first, uim * kim, ure * kim + uim * kre)
        y = _dot(inv_ref[...], jnp.concatenate([yre, yim], 0))
        return y + u * bias_ref[o:o + 1, :]

    for g in range(seqs):
        sl = slice(g * L, (g + 1) * L)
        x = hy_ref[sl, :].astype(F32)
        rows = lax.broadcasted_iota(jnp.int32, x.shape, 0)
        prev = jnp.where(rows == 0, 0.0, pltpu.roll(x, 1, axis=0))
        nxt = jnp.where(rows == L - 1, 0.0, pltpu.roll(x, L - 1, axis=0))
        z = prev * cw_ref[0:1, :] + x * cw_ref[1:2, :] + nxt * cw_ref[2:3, :] + cb_ref[...]
        v, x1, x2 = z[:, 0:HY_W], z[:, HY_W:2 * HY_W], z[:, 2 * HY_W:3 * HY_W]
        u = x1 * long_conv(v, 0)
        o_ref[sl, :] = (x2 * long_conv(u, 1)).astype(o_ref.dtype)


def _const_spec(shape):
    return pl.BlockSpec(shape, lambda i: (0,) * len(shape))


def _hyena(z_main, layer, conv_w, conv_b, bias, filters, dft):
    tables, table_specs = [], []
    for L in (SEQ, DEC_SEQ):
        tables += [*filters[L], *dft[L]]
        table_specs += [_const_spec((L, 2 * HY_W)), _const_spec((L, 2 * HY_W)),
                        _const_spec((2 * L, L)), _const_spec((L, 2 * L))]
    return pl.pallas_call(
        _hyena_kernel,
        out_shape=jax.ShapeDtypeStruct((T_ALL, HY_W), BF16),
        grid=(MIXER_STEPS,),
        in_specs=[
            pl.BlockSpec((STEP_ROWS, 3 * HY_W), lambda i: (i, 0)),
            pl.BlockSpec((None, 3, 3 * HY_W), lambda i: (layer, 0, 0)),
            pl.BlockSpec((None, 1, 3 * HY_W), lambda i: (layer, 0, 0)),
            pl.BlockSpec((None, 2, HY_W), lambda i: (layer, 0, 0)),
        ] + table_specs,
        out_specs=pl.BlockSpec((STEP_ROWS, HY_W), lambda i: (i, 0)),
        compiler_params=_params("parallel"),
        name="hyena",
    )(z_main, conv_w, conv_b.reshape(DEPTH, 1, 3 * HY_W), bias, *tables)


def _win_masks():
    lane = lax.broadcasted_iota(jnp.int32, (1, LANE), 1)
    return lane < WIN_HD, lane >= WIN_HD


def _win_head_operands(q, k, v, h):
    lo_mask, hi_mask = _win_masks()
    col = h // 2
    lo = h % 2 == 0
    q128 = jnp.where(lo_mask if lo else hi_mask, q[:, col * LANE:(col + 1) * LANE], 0.0)
    swap = h in (1, 2)
    if swap:
        k = pltpu.roll(k, WIN_HD, axis=1)
        v = pltpu.roll(v, WIN_HD, axis=1)
    return q128, k, v, lo


def _win_kernel(sink_ref, q_ref, kv_ref, ck_ref, cv_ref, c_ref, su_ref, sd_ref, o_ref, *, layer):
    _group_step(
        lambda: _win_ctx_body(sink_ref, q_ref, kv_ref, o_ref, layer),
        lambda: _win_lat_body(sink_ref, q_ref, kv_ref, ck_ref, cv_ref, c_ref, su_ref, sd_ref, o_ref, layer))


def _win_ctx_body(sink_ref, q_ref, kv_ref, o_ref, layer):
    lo_mask, hi_mask = _win_masks()
    qscale = WIN_HD ** -0.5 * LOG2E
    for g in range(CTX_SEQS_PER_STEP):
        sl = slice(g * SEQ, (g + 1) * SEQ)
        q = q_ref[sl, :].astype(F32) * qscale
        k = kv_ref[sl, 0:LANE].astype(F32)
        v = kv_ref[sl, LANE:2 * LANE].astype(F32)
        cols = []
        for col in range(2):
            acc = None
            for h in (2 * col, 2 * col + 1):
                q128, kk, vv, lo = _win_head_operands(q, k, v, h)
                s = _dot_nt(q128, kk)
                sink = sink_ref[layer, h] * LOG2E
                m = jnp.maximum(jnp.max(s, -1, keepdims=True), sink)
                p = jnp.exp2(s - m)
                den = jnp.sum(p, -1, keepdims=True) + jnp.exp2(sink - m)
                o = _dot(p, vv) / den
                o = jnp.where(lo_mask if lo else hi_mask, o, 0.0)
                acc = o if acc is None else acc + o
            cols.append(acc)
        o_ref[sl, :] = jnp.concatenate(cols, 1).astype(o_ref.dtype)


def _win_lat_body(sink_ref, q_ref, kv_ref, ck_ref, cv_ref, c_ref, su_ref, sd_ref, o_ref, layer):
    L = DEC_SEQ
    half = WIN_HD // 2
    c, su, sd = c_ref[...], su_ref[...], sd_ref[...]
    qscale = WIN_HD ** -0.5 * LOG2E
    q = jnp.concatenate(
        [_rope128(q_ref[:, i * LANE:(i + 1) * LANE].astype(F32), c, su, sd, half) for i in range(2)],
        1) * qscale
    k = _rope128(kv_ref[:, 0:LANE].astype(F32), c, su, sd, half)
    v = kv_ref[:, LANE:2 * LANE].astype(F32)
    ck = ck_ref[...]
    cv = cv_ref[...]
    lo_mask, hi_mask = _win_masks()
    nb = L // CHUNK
    assert WINDOW == CHUNK
    rr = lax.broadcasted_iota(jnp.int32, (CHUNK, CHUNK), 0)
    cc = lax.broadcasted_iota(jnp.int32, (CHUNK, CHUNK), 1)
    band = {-1: jnp.where(cc >= rr, 0.0, NEG), 0: jnp.zeros((CHUNK, CHUNK), F32),
            1: jnp.where(cc <= rr, 0.0, NEG)}
    cols = []
    for col in range(2):
        acc_blocks = [None] * nb
        for h in (2 * col, 2 * col + 1):
            q128, kk, vv, lo = _win_head_operands(q, k, v, h)
            _, ckk, cvv, _ = _win_head_operands(q, ck, cv, h)
            sink = sink_ref[layer, h] * LOG2E
            for n in range(nb):
                blocks = [d for d in (-1, 0, 1) if 0 <= n + d < nb]
                k0 = (n + blocks[0]) * CHUNK
                k1 = (n + blocks[-1] + 1) * CHUNK
                qn = q128[n * CHUNK:(n + 1) * CHUNK]
                s_loc = _dot_nt(qn, kk[k0:k1]) + jnp.concatenate([band[d] for d in blocks], 1)
                s_ctx = _dot_nt(qn, ckk)
                m = jnp.maximum(jnp.maximum(jnp.max(s_loc, -1, keepdims=True),
                                            jnp.max(s_ctx, -1, keepdims=True)), sink)
                p_loc = jnp.exp2(s_loc - m)
                p_ctx = jnp.exp2(s_ctx - m)
                den = (jnp.sum(p_loc, -1, keepdims=True) + jnp.sum(p_ctx, -1, keepdims=True)
                       + jnp.exp2(sink - m))
                o = (_dot(p_loc, vv[k0:k1]) + _dot(p_ctx, cvv)) / den
                o = jnp.where(lo_mask if lo else hi_mask, o, 0.0)
                acc_blocks[n] = o if acc_blocks[n] is None else acc_blocks[n] + o
        cols.append(jnp.concatenate(acc_blocks, 0))
    o_ref[...] = jnp.concatenate(cols, 1).astype(o_ref.dtype)


def _win(z_main, sink, cache_k, cache_v, layer):
    c, su, sd, _ = _rope_tables(DEC_SEQ, WIN_HD, LANE)
    tab = _const_spec((DEC_SEQ, LANE))
    cache = pl.BlockSpec((None, None, PAST_LEN, LANE), lambda i: (_lat_index(i), layer, 0, 0))
    return pl.pallas_call(
        functools.partial(_win_kernel, layer=layer),
        out_shape=jax.ShapeDtypeStruct((T_ALL, WIN_HEADS * WIN_HD), BF16),
        grid=(MIXER_STEPS,),
        in_specs=[
            pl.BlockSpec(memory_space=pltpu.SMEM),
            pl.BlockSpec((STEP_ROWS, 256), lambda i: (i, COL_WQ // 256)),
            pl.BlockSpec((STEP_ROWS, 256), lambda i: (i, COL_WK // 256)),
            cache, cache, tab, tab, tab,
        ],
        out_specs=pl.BlockSpec((STEP_ROWS, 256), lambda i: (i, 0)),
        compiler_params=_params("parallel"),
        name="win",
    )(sink, z_main, z_main, cache_k, cache_v, jnp.asarray(c), jnp.asarray(su), jnp.asarray(sd))


def _ret_kernel(*refs, layer):
    _group_step(lambda: _ret_body(*refs, L=SEQ, layer=layer, ctx=True),
                lambda: _ret_body(*refs, L=DEC_SEQ, layer=layer, ctx=False))


def _ret_body(df_ref, db_ref, q_ref, k_ref, v0_ref, v1_ref, g0_ref, g1_ref, s0f_ref, s0b_ref,
              o_ref, sf_out, sb_out, s_ref, cross_ref, *, L, layer, ctx):
    seqs = STEP_ROWS // L
    if not ctx:
        sf_out[...] = jnp.zeros_like(sf_out)
        sb_out[...] = jnp.zeros_like(sb_out)
    C = CHUNK
    nc = L // C
    H = RET_HEADS
    qw = H * RET_DK
    vw = H * RET_DV

    def lane_table(width, per_head, fn):
        pos = lax.broadcasted_iota(jnp.int32, (C, per_head), 0).astype(F32)
        return jnp.concatenate([fn(h, pos) for h in range(H)], 1)

    def log_gamma(ref, h):
        d = jnp.full((1, 1), ref[layer, h], F32)
        return jnp.log(jax.nn.sigmoid(d))

    lgf = [log_gamma(df_ref, h) for h in range(H)]
    lgb = [log_gamma(db_ref, h) for h in range(H)]

    def tables(lg, reverse):
        if reverse:
            dq = lane_table(vw, RET_DV, lambda h, pos: jnp.exp((C - pos) * lg[h]))
            dk = lane_table(qw, RET_DK, lambda h, pos: jnp.exp(pos * lg[h]))
        else:
            dq = lane_table(vw, RET_DV, lambda h, pos: jnp.exp((pos + 1.0) * lg[h]))
            dk = lane_table(qw, RET_DK, lambda h, pos: jnp.exp((C - 1.0 - pos) * lg[h]))
        dc = jnp.concatenate([jnp.broadcast_to(jnp.exp(C * lg[h]), (1, RET_DV)) for h in range(H)], 1)
        return dq, dk, dc

    tab_f = tables(lgf, False)
    tab_b = tables(lgb, True)
    ii = lax.broadcasted_iota(jnp.int32, (C, C), 0)
    jj = lax.broadcasted_iota(jnp.int32, (C, C), 1)
    diff = (ii - jj).astype(F32)
    dmats = [jnp.where(diff >= 0, jnp.exp(jnp.maximum(diff, 0.0) * lgf[h]), 0.0)
             + jnp.where(diff <= 0, jnp.exp(jnp.maximum(-diff, 0.0) * lgb[h]), 0.0) for h in range(H)]
    dmat_stack = jnp.concatenate(dmats, 0)
    lane_q = lax.broadcasted_iota(jnp.int32, (1, qw), 1) // RET_DK

    srow = lax.broadcasted_iota(jnp.int32, (qw, vw), 0) // RET_DK
    scol = lax.broadcasted_iota(jnp.int32, (qw, vw), 1) // RET_DV
    diag = srow == scol

    for g in range(seqs):
        base = g * L
        rows_all = slice(base, base + L)
        q_all = q_ref[rows_all, :].astype(F32)
        k_all = k_ref[rows_all, :].astype(F32) * (RET_DK ** -0.5)
        v_all = jnp.concatenate([v0_ref[rows_all, :], v1_ref[rows_all, :]], 1).astype(F32)
        g_all = jnp.concatenate([g0_ref[rows_all, :], g1_ref[rows_all, :]], 1).astype(F32)

        def scan(tabs, reverse, s0_ref, s_out):
            dq, dk, dc = tabs
            s_ref[g] = jnp.zeros((qw, vw), F32)
            if s0_ref is not None:
                for h in range(H):
                    s_ref[g, h * RET_DK:(h + 1) * RET_DK, h * RET_DV:(h + 1) * RET_DV] = s0_ref[h]
            order = range(nc - 1, -1, -1) if reverse else range(nc)
            for ci in order:
                sl = slice(ci * C, (ci + 1) * C)
                rs = slice(base + ci * C, base + (ci + 1) * C)
                qc, kc, vc = q_all[sl], k_all[sl], v_all[sl]
                st = s_ref[g]
                cross = _dot(qc, st) * dq
                if reverse:
                    cross_ref[rs, :] = cross_ref[rs, :] + cross
                else:
                    cross_ref[rs, :] = cross
                upd = jnp.where(diag, _dot_tn(kc * dk, vc), 0.0)
                s_ref[g] = st * dc + upd
            if s_out is not None:
                for h in range(H):
                    s_out[g, h] = s_ref[g, h * RET_DK:(h + 1) * RET_DK, h * RET_DV:(h + 1) * RET_DV]

        scan(tab_f, False, None if ctx else s0f_ref, sf_out if ctx else None)
        scan(tab_b, True, None if ctx else s0b_ref, sb_out if ctx else None)

        for ci in range(nc):
            sl = slice(ci * C, (ci + 1) * C)
            rs = slice(base + ci * C, base + (ci + 1) * C)
            q_stack = jnp.concatenate([jnp.where(lane_q == h, q_all[sl], 0.0) for h in range(H)], 0)
            att = _dot_nt(q_stack, k_all[sl]) * dmat_stack
            ov = _dot(att, v_all[sl])
            for h in range(H):
                hv = slice(h * RET_DV, (h + 1) * RET_DV)
                o = ov[h * C:(h + 1) * C, hv] + cross_ref[rs, hv]
                gt = g_all[sl, hv]
                o_ref[rs, hv] = ((gt * jax.nn.sigmoid(gt)) * _layer_norm(o)).astype(o_ref.dtype)


def _retention(z_main, dec_f, dec_b, s0f, s0b, layer):
    def zcol(col):
        return pl.BlockSpec((STEP_ROWS, 256), lambda i: (i, col // 256))

    smem = pl.BlockSpec(memory_space=pltpu.SMEM)
    z_specs = [zcol(COL_RQ), zcol(COL_RK), zcol(COL_RV), zcol(COL_RV + 256),
               zcol(COL_RG), zcol(COL_RG + 256)]
    s0_spec = pl.BlockSpec((None, None, RET_HEADS, RET_DK, RET_DV),
                           lambda i: (_lat_index(i), layer, 0, 0, 0))
    st_shape = jax.ShapeDtypeStruct((MIXER_STEPS * CTX_SEQS_PER_STEP, RET_HEADS, RET_DK, RET_DV), F32)
    st_spec = pl.BlockSpec((CTX_SEQS_PER_STEP, RET_HEADS, RET_DK, RET_DV), lambda i: (i, 0, 0, 0))
    return pl.pallas_call(
        functools.partial(_ret_kernel, layer=layer),
        out_shape=(jax.ShapeDtypeStruct((T_ALL, RET_HEADS * RET_DV), BF16), st_shape, st_shape),
        grid=(MIXER_STEPS,),
        in_specs=[smem, smem] + z_specs + [s0_spec, s0_spec],
        out_specs=(pl.BlockSpec((STEP_ROWS, RET_HEADS * RET_DV), lambda i: (i, 0)), st_spec, st_spec),
        scratch_shapes=[pltpu.VMEM((CTX_SEQS_PER_STEP, RET_HEADS * RET_DK, RET_HEADS * RET_DV), F32),
                        pltpu.VMEM((STEP_ROWS, RET_HEADS * RET_DV), F32)],
        compiler_params=_params("parallel"),
        name="retention",
    )(dec_f, dec_b, *([z_main] * 6), s0f, s0b)


def _rms_norm(x, g):
    return x * lax.rsqrt(jnp.mean(x * x, -1, keepdims=True) + RMS_EPS) * g


def _mla_keys(kn, kr):
    lane_r = lax.broadcasted_iota(jnp.int32, (1, LANE), 1)
    return jnp.concatenate([kn, jnp.where(lane_r < MLA_ROPE, kr, 0.0)], 1).astype(BF16)


def _mla_attend(qn, qr, k_cat, vv, o_ref, row0):
    qscale = (MLA_NOPE + MLA_ROPE) ** -0.5 * LOG2E
    lane_n = lax.broadcasted_iota(jnp.int32, (1, MLA_HEADS * MLA_NOPE), 1) // MLA_NOPE
    lane_r = lax.broadcasted_iota(jnp.int32, (1, LANE), 1)
    qn = qn * qscale
    qr = qr * qscale
    lq = qn.shape[0]
    heads = []
    for h in range(MLA_HEADS):
        qnh = jnp.where(lane_n == h, qn, 0.0)
        qrh = qr if h == 0 else pltpu.roll(qr, LANE - h * MLA_ROPE, axis=1)
        qrh = jnp.where(lane_r < MLA_ROPE, qrh, 0.0)
        heads.append(jnp.concatenate([qnh, qrh], 1).astype(BF16))
    s = _dot_nt(jnp.concatenate(heads, 0), k_cat)
    m = jnp.max(s, -1, keepdims=True)
    p = jnp.exp2(s - m)
    den = jnp.sum(p, -1, keepdims=True)
    o = _dot(p, vv) / den
    acc = None
    for h in range(MLA_HEADS):
        oh = jnp.where(lane_n == h, o[h * lq:(h + 1) * lq], 0.0)
        acc = oh if acc is None else acc + oh
    o_ref[row0:row0 + lq, :] = acc.astype(o_ref.dtype)


def _mla_kernel(cq_ref, ckv_ref, kr_ref, cckv_ref, ckr_ref, c_ref, su_ref, sd_ref,
                qg_ref, kg_ref, wqn_ref, wqr_ref, wk_ref, wv_ref, o_ref, ckvn_ref):
    weights = (qg_ref, kg_ref, wqn_ref, wqr_ref, wk_ref, wv_ref)
    _group_step(
        lambda: _mla_ctx_body(cq_ref, ckv_ref, kr_ref, *weights, o_ref, ckvn_ref),
        lambda: _mla_lat_body(cq_ref, ckv_ref, kr_ref, cckv_ref, ckr_ref, c_ref, su_ref, sd_ref,
                              *weights, o_ref, ckvn_ref))


def _mla_ctx_body(cq_ref, ckv_ref, kr_ref, qg_ref, kg_ref, wqn_ref, wqr_ref, wk_ref, wv_ref,
                  o_ref, ckvn_ref):
    cqn = _rms_norm(cq_ref[...].astype(F32), qg_ref[...])
    qn = _dot(cqn, wqn_ref[...])
    qr = _dot(cqn, wqr_ref[...])
    ckvn = _rms_norm(ckv_ref[...].astype(F32), kg_ref[...])
    ckvn_ref[...] = ckvn
    k_cat = _mla_keys(_dot(ckvn, wk_ref[...]), kr_ref[...].astype(F32))
    vv = _dot(ckvn, wv_ref[...]).astype(BF16)
    for g in range(CTX_SEQS_PER_STEP):
        sl = slice(g * SEQ, (g + 1) * SEQ)
        _mla_attend(qn[sl], qr[sl], k_cat[sl], vv[sl], o_ref, g * SEQ)


def _mla_lat_body(cq_ref, ckv_ref, kr_ref, cckv_ref, ckr_ref, c_ref, su_ref, sd_ref,
                  qg_ref, kg_ref, wqn_ref, wqr_ref, wk_ref, wv_ref, o_ref, ckvn_ref):
    half = MLA_ROPE // 2
    c, su, sd = c_ref[...], su_ref[...], sd_ref[...]
    cqn = _rms_norm(cq_ref[...].astype(F32), qg_ref[...])
    qn = _dot(cqn, wqn_ref[...])
    qr = _rope128(_dot(cqn, wqr_ref[...]), c, su, sd, half)
    ckvn = _rms_norm(ckv_ref[...].astype(F32), kg_ref[...])
    ckvn_ref[...] = ckvn
    ckv_all = jnp.concatenate([ckvn, cckv_ref[...]], 0)
    vv = _dot(ckv_all, wv_ref[...]).astype(BF16)
    kr = jnp.concatenate([_rope128(kr_ref[...].astype(F32), c, su, sd, half), ckr_ref[...]], 0)
    k_cat = _mla_keys(_dot(ckv_all, wk_ref[...]), kr)
    rows_per_call = 256
    for n in range(DEC_SEQ // rows_per_call):
        rows = slice(n * rows_per_call, (n + 1) * rows_per_call)
        _mla_attend(qn[rows], qr[rows], k_cat, vv, o_ref, n * rows_per_call)


def _mla(z_main, cache_ckv, cache_kr_pad, weights, layer):
    c, su, sd, _ = _rope_tables(DEC_SEQ, MLA_ROPE, LANE)
    tab = _const_spec((DEC_SEQ, LANE))
    cache = pl.BlockSpec((None, None, PAST_LEN, LANE), lambda i: (_lat_index(i), layer, 0, 0))

    def weight(*shape):
        return pl.BlockSpec((None,) + shape, lambda i: (layer, 0, 0))

    return pl.pallas_call(
        _mla_kernel,
        out_shape=(jax.ShapeDtypeStruct((T_ALL, MLA_HEADS * MLA_V), BF16),
                   jax.ShapeDtypeStruct((T_ALL, MLA_KV_LORA), F32)),
        grid=(MIXER_STEPS,),
        in_specs=[
            pl.BlockSpec((STEP_ROWS, 256), lambda i: (i, COL_CQ // 256)),
            pl.BlockSpec((STEP_ROWS, LANE), lambda i: (i, COL_CKV // LANE)),
            pl.BlockSpec((STEP_ROWS, LANE), lambda i: (i, COL_KROPE // LANE)),
            cache, cache, tab, tab, tab,
            weight(1, MLA_Q_LORA), weight(1, MLA_KV_LORA),
            weight(MLA_Q_LORA, MLA_HEADS * MLA_NOPE), weight(MLA_Q_LORA, MLA_HEADS * MLA_ROPE),
            weight(MLA_KV_LORA, MLA_HEADS * MLA_NOPE), weight(MLA_KV_LORA, MLA_HEADS * MLA_V),
        ],
        out_specs=(pl.BlockSpec((STEP_ROWS, 256), lambda i: (i, 0)),
                   pl.BlockSpec((STEP_ROWS, MLA_KV_LORA), lambda i: (i, 0))),
        compiler_params=_params("parallel"),
        name="mla",
    )(z_main, z_main, z_main, cache_ckv, cache_kr_pad,
      jnp.asarray(c), jnp.asarray(su), jnp.asarray(sd), *weights)


def _route(logits_t, rb):
    scores = jax.nn.sigmoid(logits_t)
    biased = scores + rb
    sc = [scores[e:e + 1, :] for e in range(N_EXPERTS)]
    bi = [biased[e:e + 1, :] for e in range(N_EXPERTS)]
    epg = EXPERTS_PER_GROUP
    gsum = []
    for g in range(N_GROUPS):
        v = bi[g * epg:(g + 1) * epg]
        best = None
        for a in range(epg):
            for b in range(a + 1, epg):
                pair = v[a] + v[b]
                best = pair if best is None else jnp.maximum(best, pair)
        gsum.append(best)
    combine = []
    sel = []
    for g in range(N_GROUPS):
        is_best = None
        for g2 in range(N_GROUPS):
            if g2 == g:
                continue
            c = gsum[g] > gsum[g2] if g2 < g else gsum[g] >= gsum[g2]
            is_best = c if is_best is None else jnp.logical_and(is_best, c)
        for a in range(epg):
            e = g * epg + a
            rank = jnp.zeros_like(bi[e])
            for b in range(epg):
                if b == a:
                    continue
                e2 = g * epg + b
                ahead = bi[e2] >= bi[e] if b < a else bi[e2] > bi[e]
                rank = rank + jnp.where(ahead, 1.0, 0.0)
            sel.append(jnp.logical_and(is_best, rank < 2.0))
    wsum = None
    for e in range(N_EXPERTS):
        w = jnp.where(sel[e], sc[e], 0.0)
        wsum = w if wsum is None else wsum + w
    for e in range(N_EXPERTS):
        combine.append(jnp.where(sel[e], ROUTE_SCALE * sc[e] / wsum, 0.0))
    return jnp.concatenate(combine, 0)


MERGE_BRANCH_ROWS = (HY_W, WIN_HEADS * WIN_HD, RET_HEADS * RET_DV, MLA_HEADS * MLA_V)
MERGE_ROWS = sum(MERGE_BRANCH_ROWS) + D_MODEL


def _merge_kernel(ya_ref, yb_ref, yc_ref, yd_ref, gt_ref, xc_ref, xl_ref, m_ref,
                  w_ref, g_ref, b_ref, rw_ref, rb_ref,
                  x1_ref, h2_ref, cmb_ref, *, ctx_tiles, sub_rows):
    D = D_MODEL
    rw = rw_ref[...]
    rw_hi = rw.astype(BF16)
    rw_lo = (rw - rw_hi.astype(F32)).astype(BF16)
    g1 = m_ref[:, 2 * D:3 * D]
    s2 = m_ref[:, 3 * D:4 * D]
    sc2 = m_ref[:, 4 * D:5 * D]
    is_ctx = pl.program_id(0) < ctx_tiles
    offs = np.cumsum((0,) + MERGE_BRANCH_ROWS)
    branches = tuple((y_ref, slice(int(offs[i]), int(offs[i + 1])))
                     for i, y_ref in enumerate((ya_ref, yb_ref, yc_ref, yd_ref)))
    w_out_rows = slice(int(offs[-1]), MERGE_ROWS)
    for r0 in range(0, x1_ref.shape[0], sub_rows):
        rows = slice(r0, r0 + sub_rows)
        merged = None
        for i, (y_ref, w_rows) in enumerate(branches):
            t = gt_ref[rows, i * D:(i + 1) * D] * jnp.dot(
                y_ref[rows, :], w_ref[w_rows, :], preferred_element_type=F32).astype(BF16)
            merged = t if merged is None else merged + t
        out1 = jnp.dot(merged, w_ref[w_out_rows, :], preferred_element_type=F32)
        x = jnp.where(is_ctx, xc_ref[rows, :], xl_ref[rows, :])
        x1 = _layer_norm(ALPHA * x + g1 * out1) * g_ref[...] + b_ref[...]
        x1_ref[rows, :] = x1
        h2 = _layer_norm(x1) * (1.0 + sc2) + s2
        h2_hi = h2.astype(BF16)
        h2_ref[rows, :] = h2_hi
        h2_lo = (h2 - h2_hi.astype(F32)).astype(BF16)
        logits = (jnp.dot(h2_hi, rw_hi, preferred_element_type=F32)
                  + (jnp.dot(h2_lo, rw_hi, preferred_element_type=F32)
                     + jnp.dot(h2_hi, rw_lo, preferred_element_type=F32)))
        cmb_ref[:, rows] = _route(logits.T[0:N_EXPERTS], rb_ref[...])


def _merge(ya, yb, yc, yd, gates, x_ctx, x_lat, mods, w_merge, ln1_g, ln1_b,
           router_w, router_b, layer):
    tm = 512
    row = _mod_row(tm)
    D = D_MODEL
    ctx_tiles = T_CTX // tm

    def tile(w):
        return pl.BlockSpec((tm, w), lambda i: (i, 0))

    def weight(k, n):
        return pl.BlockSpec((None, k, n), lambda i: (layer, 0, 0))

    return pl.pallas_call(
        functools.partial(_merge_kernel, ctx_tiles=ctx_tiles, sub_rows=256),
        out_shape=(jax.ShapeDtypeStruct((T_ALL, D), F32),
                   jax.ShapeDtypeStruct((T_ALL, D), BF16),
                   jax.ShapeDtypeStruct((N_EXPERTS, T_ALL), F32)),
        grid=(T_ALL // tm,),
        in_specs=[
            tile(256), tile(256), tile(512), tile(256), tile(4 * D),
            pl.BlockSpec((tm, D), lambda i: (jnp.minimum(i, ctx_tiles - 1), 0)),
            pl.BlockSpec((tm, D), lambda i: (jnp.maximum(i - ctx_tiles, 0), 0)),
            _mod_spec(layer, row),
            weight(MERGE_ROWS, D), weight(1, D), weight(1, D),
            pl.BlockSpec((D, LANE), lambda i: (0, 0)),
            pl.BlockSpec((N_EXPERTS, 1), lambda i: (0, 0)),
        ],
        out_specs=(tile(D), tile(D), pl.BlockSpec((N_EXPERTS, tm), lambda i: (0, i))),
        compiler_params=_params("parallel"),
        name="merge",
    )(ya, yb, yc, yd, gates, x_ctx, x_lat, mods, w_merge,
      ln1_g.reshape(DEPTH, 1, D), ln1_b.reshape(DEPTH, 1, D), router_w,
      router_b.reshape(N_EXPERTS, 1))


MOE_EXPERTS_PER_STEP = 2


def _moe_kernel(*refs, next_h):
    if next_h:
        (h_ref, c_ref, x1_ref, m_ref, wg_ref, wu_ref, wd_ref, g_ref, b_ref, mn_ref,
         o_ref, hn_ref, acc_ref) = refs[-13:]
    else:
        h_ref, c_ref, x1_ref, m_ref, wg_ref, wu_ref, wd_ref, g_ref, b_ref, o_ref, acc_ref = refs
    eg = pl.program_id(1)

    @pl.when(eg == 0)
    def _():
        acc_ref[...] = jnp.zeros_like(acc_ref)

    h = h_ref[...]
    cmb = c_ref[...]
    lane = lax.broadcasted_iota(jnp.int32, cmb.shape, 1)
    hid = []
    for k in range(MOE_EXPERTS_PER_STEP):
        gate = jnp.dot(h, wg_ref[k].astype(BF16), preferred_element_type=F32)
        up = jnp.dot(h, wu_ref[k].astype(BF16), preferred_element_type=F32)
        e = eg * MOE_EXPERTS_PER_STEP + k
        ce = jnp.sum(jnp.where(lane == e, cmb, 0.0), -1, keepdims=True)
        sig = 0.5 * jnp.tanh(0.5 * gate) + 0.5
        hid.append((gate * sig * (up * ce)).astype(BF16))
    wd = wd_ref[...].reshape(MOE_EXPERTS_PER_STEP * D_EXPERT, D_MODEL).astype(BF16)
    acc_ref[...] += jnp.dot(jnp.concatenate(hid, 1), wd, preferred_element_type=F32)

    @pl.when(eg == N_EXPERTS // MOE_EXPERTS_PER_STEP - 1)
    def _():
        g2 = m_ref[:, 5 * D_MODEL:6 * D_MODEL]
        y = _layer_norm(ALPHA * x1_ref[...] + g2 * acc_ref[...])
        y = y * g_ref[...] + b_ref[...]
        o_ref[...] = y
        if next_h:
            s1 = mn_ref[:, 0:D_MODEL]
            sc1 = mn_ref[:, D_MODEL:2 * D_MODEL]
            hn_ref[...] = (_layer_norm(y) * (1.0 + sc1) + s1).astype(hn_ref.dtype)


def _moe(h2, combine, x1, mods, w_gate, w_up, w_down, ln2_g, ln2_b, layer, row0, n_rows,
         next_h=False, prev_h=None):
    tm = 1024
    row = _mod_row(tm)
    D = D_MODEL
    t0 = row0 // tm
    eps = MOE_EXPERTS_PER_STEP
    mod_spec = _mod_spec(layer, lambda i, e: row(t0 + i))
    in_specs = [
        pl.BlockSpec((tm, D), lambda i, e: (t0 + i, 0)),
        pl.BlockSpec((tm, N_EXPERTS), lambda i, e: (t0 + i, 0)),
        pl.BlockSpec((tm, D), lambda i, e: (t0 + i, 0)),
        mod_spec,
        pl.BlockSpec((None, eps, D, D_EXPERT), lambda i, e: (layer, e, 0, 0)),
        pl.BlockSpec((None, eps, D, D_EXPERT), lambda i, e: (layer, e, 0, 0)),
        pl.BlockSpec((None, eps, D_EXPERT, D), lambda i, e: (layer, e, 0, 0)),
        pl.BlockSpec((None, 1, D), lambda i, e: (layer, 0, 0)),
        pl.BlockSpec((None, 1, D), lambda i, e: (layer, 0, 0)),
    ]
    args = [h2, combine, x1, mods, w_gate, w_up, w_down,
            ln2_g.reshape(DEPTH, 1, D), ln2_b.reshape(DEPTH, 1, D)]
    out_shape = jax.ShapeDtypeStruct((n_rows, D), F32)
    out_specs = pl.BlockSpec((tm, D), lambda i, e: (i, 0))
    aliases = {}
    if next_h:
        in_specs.append(_mod_spec(layer + 1, lambda i, e: row(t0 + i)))
        args.append(mods)
        out_shape = (out_shape, jax.ShapeDtypeStruct((T_ALL, D), BF16))
        out_specs = (out_specs, pl.BlockSpec((tm, D), lambda i, e: (t0 + i, 0)))
        if prev_h is not None:
            in_specs = [pl.BlockSpec(memory_space=pl.ANY)] + in_specs
            args = [prev_h] + args
            aliases = {0: 1}
    return pl.pallas_call(
        functools.partial(_moe_kernel, next_h=next_h),
        out_shape=out_shape,
        grid=(n_rows // tm, N_EXPERTS // eps),
        in_specs=in_specs,
        out_specs=out_specs,
        input_output_aliases=aliases,
        scratch_shapes=[pltpu.VMEM((tm, D), F32)],
        compiler_params=_params("parallel", "arbitrary"),
        name="moe",
    )(*args)


def kernel(x_prompt, x_sample, cache_win_k, cache_win_v, cache_mla_ckv, cache_mla_krope,
           state_ret_fwd, state_ret_bwd, c, c_ctx, w_ada, b_ada, w_in,
           hy_conv_w, hy_conv_b, hy_w1, hy_b1, hy_w2, hy_b2, hy_w3, hy_bias,
           win_sink, ret_decay_fwd, ret_decay_bwd, mla_q_norm, mla_kv_norm, mla_w_uq, mla_w_ukv,
           w_br_a, w_br_b, w_br_c, w_br_d, w_out, ln1_g, ln1_b, ln2_g, ln2_b,
           router_w, router_b, moe_w_gate, moe_w_up, moe_w_down):
    D = D_MODEL
    x_ctx = x_prompt.reshape(T_CTX, D)
    x_lat = x_sample.reshape(T_LAT, D)

    mods = _ada_mods(c_ctx, c, w_ada, b_ada)

    w_in_t = jnp.swapaxes(w_in, 1, 2).reshape(DEPTH * IN_COLS, D)
    cache_k = cache_win_k.reshape(DEC_BATCH, DEPTH, PAST_LEN, WIN_KV_HEADS * WIN_HD)
    cache_v = cache_win_v.reshape(DEC_BATCH, DEPTH, PAST_LEN, WIN_KV_HEADS * WIN_HD)
    cache_kr = jnp.pad(cache_mla_krope, ((0, 0), (0, 0), (0, 0), (0, LANE - MLA_ROPE)))

    uq = mla_w_uq.reshape(DEPTH, MLA_Q_LORA, MLA_HEADS, MLA_NOPE + MLA_ROPE)
    ukv = mla_w_ukv.reshape(DEPTH, MLA_KV_LORA, MLA_HEADS, MLA_NOPE + MLA_V)
    mla_weights = (
        mla_q_norm.reshape(DEPTH, 1, MLA_Q_LORA),
        mla_kv_norm.reshape(DEPTH, 1, MLA_KV_LORA),
        uq[..., :MLA_NOPE].reshape(DEPTH, MLA_Q_LORA, MLA_HEADS * MLA_NOPE),
        uq[..., MLA_NOPE:].reshape(DEPTH, MLA_Q_LORA, MLA_HEADS * MLA_ROPE),
        ukv[..., :MLA_NOPE].reshape(DEPTH, MLA_KV_LORA, MLA_HEADS * MLA_NOPE),
        ukv[..., MLA_NOPE:].reshape(DEPTH, MLA_KV_LORA, MLA_HEADS * MLA_V),
    )

    hy_w1p = jnp.pad(hy_w1, ((0, 0), (0, LANE - HY_EMB), (0, 0)))
    dft = {}
    for L in (SEQ, DEC_SEQ):
        fwd, inv = _dft_tables(L)
        dft[L] = (jnp.asarray(fwd).astype(BF16), jnp.asarray(inv).astype(BF16))
    router_w_pad = jnp.pad(router_w, ((0, 0), (0, LANE - N_EXPERTS)))
    w_merge = jnp.concatenate([w_br_a, w_br_b, w_br_c, w_br_d, w_out], 1).astype(BF16)

    new_k, new_v, new_ckv, new_kr, new_sf, new_sb = [], [], [], [], [], []
    for l in range(DEPTH):
        if l == 0:
            h = _ln_mod(x_ctx, None, mods, l, 0)
            h = _ln_mod(x_lat, h, mods, l, T_CTX)
        z = _in_proj(h, w_in_t, l, 0, Z_MAIN, Z_MAIN // 2, BF16, gate=False)
        gates = _in_proj(h, w_in_t, l, COL_GATE, 4 * D, D, BF16, gate=True)

        filters = {L: _hy_filters(L, hy_w1p[l], hy_b1[l][None], hy_w2[l], hy_b2[l][None], hy_w3[l],
                                  dft[L][0]) for L in (SEQ, DEC_SEQ)}
        ya = _hyena(z, l, hy_conv_w, hy_conv_b, hy_bias, filters, dft)
        yb = _win(z, win_sink, cache_k, cache_v, l)
        yc, sf, sb = _retention(z, ret_decay_fwd, ret_decay_bwd, state_ret_fwd, state_ret_bwd, l)
        yd, ckvn = _mla(z, cache_mla_ckv, cache_kr, mla_weights, l)

        x1, h2, combine_t = _merge(ya, yb, yc, yd, gates, x_ctx, x_lat, mods, w_merge,
                                   ln1_g, ln1_b, router_w_pad, router_b, l)
        moe_args = (h2, combine_t.T, x1, mods, moe_w_gate, moe_w_up, moe_w_down, ln2_g, ln2_b, l)
        if l + 1 < DEPTH:
            x_ctx, h = _moe(*moe_args, 0, T_CTX, next_h=True)
            x_lat, h = _moe(*moe_args, T_CTX, T_LAT, next_h=True, prev_h=h)
        else:
            x_ctx = _moe(*moe_args, 0, T_CTX)
            x_lat = _moe(*moe_args, T_CTX, T_LAT)

        def ctx_cols(col, width):
            return z[:T_CTX, col:col + width].astype(F32)

        new_k.append(ctx_cols(COL_WK, 128).reshape(BATCH, SEQ, WIN_KV_HEADS, WIN_HD))
        new_v.append(ctx_cols(COL_WV, 128).reshape(BATCH, SEQ, WIN_KV_HEADS, WIN_HD))
        new_ckv.append(ckvn[:T_CTX].reshape(BATCH, SEQ, MLA_KV_LORA))
        new_kr.append(ctx_cols(COL_KROPE, MLA_ROPE).reshape(BATCH, SEQ, MLA_ROPE))
        new_sf.append(sf[:BATCH])
        new_sb.append(sb[:BATCH])

    y_prompt = x_ctx.reshape(BATCH, SEQ, D)
    y_sample = x_lat.reshape(DEC_BATCH, DEC_SEQ, D)
    return (y_prompt, y_sample, jnp.stack(new_k, 1), jnp.stack(new_v, 1), jnp.stack(new_ckv, 1),
            jnp.stack(new_kr, 1), jnp.stack(new_sf, 1), jnp.stack(new_sb, 1))
```

```python
import functools
import math

import numpy as np
import jax
import jax.numpy as jnp
from jax import lax
from jax.experimental import pallas as pl
from jax.experimental.pallas import tpu as pltpu

F32 = jnp.float32
BF16 = jnp.bfloat16

D_MODEL = 1024
BATCH = 16
SEQ = 256
DEPTH = 2
DEC_BATCH = 2
DEC_SEQ = 1024
PAST_LEN = 256
GRID_W = 64
CHUNK = 128
ROPE_BASE = 10000.0
NEG = -1e30
LN_EPS = 1e-5
RMS_EPS = 1e-6
LOG2E = math.log2(math.e)

HY_W = 256
HY_BANDS = 16
HY_EMB = 1 + 2 * HY_BANDS
HY_FFN = 64
HY_FAST_DECAY = 0.3
HY_SLOW_DECAY = 1.5
HY_TARGET = 1e-2

WIN_HEADS = 4
WIN_KV_HEADS = 2
WIN_HD = 64
WINDOW = 128

RET_HEADS = 4
RET_DK = 64
RET_DV = 128

MLA_HEADS = 4
MLA_Q_LORA = 256
MLA_KV_LORA = 128
MLA_NOPE = 64
MLA_ROPE = 32
MLA_V = 64

N_EXPERTS = 16
N_GROUPS = 4
EXPERTS_PER_GROUP = N_EXPERTS // N_GROUPS
D_EXPERT = 256
ROUTE_SCALE = 2.5

ALPHA = (2.0 * DEPTH) ** 0.25

T_CTX = BATCH * SEQ
T_LAT = DEC_BATCH * DEC_SEQ
T_ALL = T_CTX + T_LAT

COL_HY = 0
COL_WQ = 768
COL_WK = 1024
COL_WV = 1152
COL_RQ = 1280
COL_RK = 1536
COL_RV = 1792
COL_RG = 2304
COL_CQ = 2816
COL_CKV = 3072
COL_KROPE = 3200
COL_GATE = 3232
IN_COLS = COL_GATE + 4 * D_MODEL
Z_MAIN = 3328

LANE = 128
STEP_ROWS = 1024
CTX_SEQS_PER_STEP = STEP_ROWS // SEQ
CTX_STEPS = T_CTX // STEP_ROWS
MIXER_STEPS = T_ALL // STEP_ROWS
VMEM_LIMIT = 56 * 1024 * 1024


def _params(*sem):
    return pltpu.CompilerParams(dimension_semantics=sem, vmem_limit_bytes=VMEM_LIMIT)


def _dot(a, b):
    return jnp.dot(a.astype(BF16), b.astype(BF16), preferred_element_type=F32)


def _dot_split(a, b):
    a_hi = a.astype(BF16)
    a_lo = (a - a_hi.astype(F32)).astype(BF16)
    b_hi = b.astype(BF16)
    b_lo = (b - b_hi.astype(F32)).astype(BF16)

    def mm(x, y):
        return jnp.dot(x, y, preferred_element_type=F32)

    return mm(a_hi, b_hi) + (mm(a_lo, b_hi) + mm(a_hi, b_lo))


def _dot_nt(a, b):
    return lax.dot_general(a.astype(BF16), b.astype(BF16), (((1,), (1,)), ((), ())),
                           preferred_element_type=F32)


def _dot_tn(a, b):
    return lax.dot_general(a.astype(BF16), b.astype(BF16), (((0,), (0,)), ((), ())),
                           preferred_element_type=F32)


def _layer_norm(x):
    mu = jnp.mean(x, -1, keepdims=True)
    xc = x - mu
    var = jnp.mean(xc * xc, -1, keepdims=True)
    return xc * lax.rsqrt(var + LN_EPS)


def _mod_row(tile_rows):
    def row(i):
        start = i * tile_rows
        return jnp.where(start < T_CTX, 0, 1 + (start - T_CTX) // DEC_SEQ)
    return row


@functools.lru_cache(maxsize=None)
def _dft_tables(L):
    f = np.arange(L, dtype=np.int64)[:, None]
    s = np.arange(L, dtype=np.int64)[None, :]
    ang = np.pi * ((f * s) % (2 * L)).astype(np.float64) / L
    cos = np.cos(ang)
    sin = np.sin(ang)
    alt = np.where(np.arange(L) % 2 == 0, 1.0, -1.0)
    fwd_im = -sin
    fwd_im[0, :] = alt
    fwd = np.concatenate([cos, fwd_im], 0)
    inv_re = cos.T / L
    inv_re[:, 0] = 1.0 / (2 * L)
    inv_im = -sin.T / L
    inv_im[:, 0] = alt / (2 * L)
    inv = np.concatenate([inv_re, inv_im], 1)
    return fwd.astype(np.float32), inv.astype(np.float32)


@functools.lru_cache(maxsize=None)
def _hyena_embedding(L):
    t01 = np.linspace(0.0, 1.0, L, dtype=np.float64)[:, None]
    bands = np.linspace(1e-4, HY_BANDS - 1, HY_BANDS, dtype=np.float64)
    ang = (2.0 * math.pi / L) * np.arange(L, dtype=np.float64)[:, None] * bands[None, :]
    z = np.concatenate([t01, np.cos(ang), -np.sin(ang)], -1)
    zp = np.zeros((L, LANE), np.float64)
    zp[:, :HY_EMB] = z
    deltas = np.abs(np.linspace(math.log(HY_TARGET) / HY_SLOW_DECAY,
                                math.log(HY_TARGET) / HY_FAST_DECAY, HY_W, dtype=np.float64))
    return zp.astype(np.float32), deltas[None, :].astype(np.float32)


@functools.lru_cache(maxsize=None)
def _rope_tables(L, rot_dim, width):
    rows = L // GRID_W
    n_freq = rot_dim // 4
    half = rot_dim // 2
    inv = ROPE_BASE ** (-np.arange(n_freq, dtype=np.float64) / n_freq)
    pos = np.arange(L)
    row = (pos // GRID_W).astype(np.float64)
    col = (pos % GRID_W).astype(np.float64)
    ang = np.concatenate([row[:, None] * inv, col[:, None] * inv], -1)
    cos, sin = np.cos(ang), np.sin(ang)
    zero = np.zeros_like(sin)
    c = np.tile(np.concatenate([cos, cos], -1), (1, width // rot_dim))
    s_up = np.tile(np.concatenate([-sin, zero], -1), (1, width // rot_dim))
    s_dn = np.tile(np.concatenate([zero, sin], -1), (1, width // rot_dim))
    return c.astype(np.float32), s_up.astype(np.float32), s_dn.astype(np.float32), half


def _rope128(x, c, s_up, s_dn, half):
    up = pltpu.roll(x, LANE - half, axis=1)
    dn = pltpu.roll(x, half, axis=1)
    return x * c + up * s_up + dn * s_dn


def _ada_kernel(cc_ref, cl_ref, w_ref, b_ref, o_ref):
    row = lax.broadcasted_iota(jnp.int32, (ADA_ROWS, 1), 0)
    cv = jnp.where(row == 0, cc_ref[...], 0.0)
    for b in range(DEC_BATCH):
        cv = jnp.where(row == 1 + b, cl_ref[b:b + 1, :], cv)
    s = cv * jax.nn.sigmoid(cv)
    s_hi = s.astype(BF16)
    s_lo = (s - s_hi.astype(F32)).astype(BF16)
    w = w_ref[...]
    w_hi = w.astype(BF16)
    w_lo = (w - w_hi.astype(F32)).astype(BF16)
    rows = s.shape[0]
    both = jnp.dot(jnp.concatenate([s_hi, s_lo], 0), w_hi, preferred_element_type=F32)
    mod = (both[0:rows] + (both[rows:2 * rows] + jnp.dot(s_hi, w_lo, preferred_element_type=F32))
           + b_ref[...])
    for r in range(rows):
        o_ref[r] = mod[r:r + 1]


ADA_ROWS = 16


def _ada_mods(c_ctx, c, w_ada, b_ada):
    tn = 1536
    n = 6 * D_MODEL
    return pl.pallas_call(
        _ada_kernel,
        out_shape=jax.ShapeDtypeStruct((DEPTH, ADA_ROWS, 1, n), F32),
        grid=(DEPTH, n // tn),
        in_specs=[
            pl.BlockSpec((1, D_MODEL), lambda l, j: (0, 0)),
            pl.BlockSpec((DEC_BATCH, D_MODEL), lambda l, j: (0, 0)),
            pl.BlockSpec((None, D_MODEL, tn), lambda l, j: (l, 0, j)),
            pl.BlockSpec((None, 1, tn), lambda l, j: (l, 0, j)),
        ],
        out_specs=pl.BlockSpec((None, ADA_ROWS, 1, tn), lambda l, j: (l, 0, 0, j)),
        compiler_params=_params("parallel", "parallel"),
        name="ada_mods",
    )(c_ctx.reshape(1, D_MODEL), c, w_ada, b_ada.reshape(DEPTH, 1, n))


def _mod_spec(layer, row_of_step):
    return pl.BlockSpec((None, None, 1, 6 * D_MODEL), lambda *g: (layer, row_of_step(*g), 0, 0))


def _lnmod_kernel(*refs):
    x_ref, m_ref, h_ref = refs[-3:]
    y = _layer_norm(x_ref[...])
    s1 = m_ref[:, 0:D_MODEL]
    sc1 = m_ref[:, D_MODEL:2 * D_MODEL]
    h_ref[...] = (y * (1.0 + sc1) + s1).astype(h_ref.dtype)


def _ln_mod(x_group, prev_out, mods, layer, row0):
    tm = 512
    row = _mod_row(tm)
    tile0 = row0 // tm
    in_specs = [
        pl.BlockSpec((tm, D_MODEL), lambda i: (i, 0)),
        _mod_spec(layer, lambda i: row(tile0 + i)),
    ]
    args = [x_group, mods]
    if prev_out is not None:
        in_specs = [pl.BlockSpec(memory_space=pl.ANY)] + in_specs
        args = [prev_out] + args
    return pl.pallas_call(
        _lnmod_kernel,
        out_shape=jax.ShapeDtypeStruct((T_ALL, D_MODEL), BF16),
        grid=(x_group.shape[0] // tm,),
        in_specs=in_specs,
        out_specs=pl.BlockSpec((tm, D_MODEL), lambda i: (tile0 + i, 0)),
        input_output_aliases={} if prev_out is None else {0: 0},
        compiler_params=_params("parallel"),
        name="ln_mod",
    )(*args)


def _proj_kernel(h_ref, w_ref, o_ref, wb_ref, *, gate):
    @pl.when(pl.program_id(1) == 0)
    def _():
        wb_ref[...] = w_ref[...].T.astype(BF16)

    sub = 2 * LANE
    n_cols = o_ref.shape[1]
    if not gate:
        for c0 in range(0, n_cols, sub):
            c1 = min(c0 + sub, n_cols)
            o_ref[:, c0:c1] = jnp.dot(h_ref[...], wb_ref[:, c0:c1],
                                      preferred_element_type=F32).astype(o_ref.dtype)
        return
    for c0 in range(0, n_cols, sub):
        r = jnp.dot(h_ref[...], wb_ref[:, c0:c0 + sub], preferred_element_type=F32)
        rb = r.astype(o_ref.dtype)
        o_ref[:, c0:c0 + sub] = 0.5 * jnp.tanh(0.5 * rb) + 0.5


def _in_proj(h, w_t, layer, col0, n_cols, tn, out_dtype, gate):
    tm = 2048
    return pl.pallas_call(
        functools.partial(_proj_kernel, gate=gate),
        out_shape=jax.ShapeDtypeStruct((T_ALL, n_cols), out_dtype),
        grid=(n_cols // tn, T_ALL // tm),
        in_specs=[
            pl.BlockSpec((tm, D_MODEL), lambda j, i: (i, 0)),
            pl.BlockSpec((pl.Element(tn), pl.Element(D_MODEL)),
                         lambda j, i: (pl.multiple_of(layer * IN_COLS + col0 + j * tn, 8), 0)),
        ],
        out_specs=pl.BlockSpec((tm, tn), lambda j, i: (i, j)),
        scratch_shapes=[pltpu.VMEM((D_MODEL, tn), BF16)],
        compiler_params=_params("parallel", "arbitrary"),
        name="gate_proj" if gate else "in_proj",
    )(h, w_t)


def _hy_filter_kernel(z_ref, dl_ref, w1_ref, b1_ref, w2_ref, b2_ref, w3_ref, fwd_ref,
                      kre_ref, kim_ref, *, L):
    z = z_ref[...]
    a = jnp.sin(_dot_split(z, w1_ref[...]) + b1_ref[...])
    a = jnp.sin(_dot_split(a, w2_ref[...]) + b2_ref[...])
    h = _dot_split(a, w3_ref[...])
    decay = jnp.exp(-z[:, 0:1] * dl_ref[...])
    not_first = lax.broadcasted_iota(jnp.int32, (L, HY_W), 0) > 0
    sums, diffs = [], []
    for o in range(2):
        fw = h[:, (2 * o) * HY_W:(2 * o + 1) * HY_W] * decay
        bw = jnp.where(not_first, h[:, (2 * o + 1) * HY_W:(2 * o + 2) * HY_W] * decay, 0.0)
        sums.append(fw + bw)
        diffs.append(fw - bw)
    p = _dot(fwd_ref[...], jnp.concatenate(sums, 1))
    q = _dot(fwd_ref[L:2 * L, :], jnp.concatenate(diffs, 1))
    kre_ref[...] = p[0:L]
    first = lax.broadcasted_iota(jnp.int32, (L, 2 * HY_W), 0) == 0
    kim_ref[...] = jnp.where(first, p[L:L + 1], q)


def _hy_filters(L, w1p, b1, w2, b2, w3, fwd):
    zemb, deltas = _hyena_embedding(L)
    out = jax.ShapeDtypeStruct((L, 2 * HY_W), F32)
    return pl.pallas_call(
        functools.partial(_hy_filter_kernel, L=L),
        out_shape=(out, out),
        compiler_params=pltpu.CompilerParams(vmem_limit_bytes=VMEM_LIMIT),
        name=f"hy_filters_{L}",
    )(jnp.asarray(zemb), jnp.asarray(deltas), w1p, b1, w2, b2, w3, fwd)


def _group_step(ctx_body, lat_body):
    i = pl.program_id(0)
    pl.when(i < CTX_STEPS)(ctx_body)
    pl.when(i >= CTX_STEPS)(lat_body)


def _lat_index(i):
    return jnp.maximum(i - CTX_STEPS, 0)


def _hyena_kernel(hy_ref, cw_ref, cb_ref, bias_ref, kre_c, kim_c, fwd_c, inv_c,
                  kre_l, kim_l, fwd_l, inv_l, o_ref):
    _group_step(
        lambda: _hyena_body(hy_ref, cw_ref, cb_ref, bias_ref, kre_c, kim_c, fwd_c, inv_c, o_ref,
                            SEQ, CTX_SEQS_PER_STEP),
        lambda: _hyena_body(hy_ref, cw_ref, cb_ref, bias_ref, kre_l, kim_l, fwd_l, inv_l, o_ref,
                            DEC_SEQ, 1))


def _hyena_body(hy_ref, cw_ref, cb_ref, bias_ref, kre_ref, kim_ref, fwd_ref, inv_ref, o_ref, L, seqs):
    first = lax.broadcasted_iota(jnp.int32, (L, HY_W), 0) == 0

    def long_conv(u, o):
        uf = _dot(fwd_ref[...], u)
        ure, uim = uf[0:L], uf[L:2 * L]
        kre = kre_ref[:, o * HY_W:(o + 1) * HY_W]
        kim = kim_ref[:, o * HY_W:(o + 1) * HY_W]
        yre = jnp.where(first, ure * kre, ure * kre - uim * kim)
        yim = jnp.where(first, uim * kim, ure * kim + uim * kre)
        y = _dot(inv_ref[...], jnp.concatenate([yre, yim], 0))
        return y + u * bias_ref[o:o + 1, :]

    for g in range(seqs):
        sl = slice(g * L, (g + 1) * L)
        x = hy_ref[sl, :].astype(F32)
        rows = lax.broadcasted_iota(jnp.int32, x.shape, 0)
        prev = jnp.where(rows == 0, 0.0, pltpu.roll(x, 1, axis=0))
        nxt = jnp.where(rows == L - 1, 0.0, pltpu.roll(x, L - 1, axis=0))
        z = prev * cw_ref[0:1, :] + x * cw_ref[1:2, :] + nxt * cw_ref[2:3, :] + cb_ref[...]
        v, x1, x2 = z[:, 0:HY_W], z[:, HY_W:2 * HY_W], z[:, 2 * HY_W:3 * HY_W]
        u = x1 * long_conv(v, 0)
        o_ref[sl, :] = (x2 * long_conv(u, 1)).astype(o_ref.dtype)


def _const_spec(shape):
    return pl.BlockSpec(shape, lambda i: (0,) * len(shape))


def _hyena(z_main, layer, conv_w, conv_b, bias, filters, dft):
    tables, table_specs = [], []
    for L in (SEQ, DEC_SEQ):
        tables += [*filters[L], *dft[L]]
        table_specs += [_const_spec((L, 2 * HY_W)), _const_spec((L, 2 * HY_W)),
                        _const_spec((2 * L, L)), _const_spec((L, 2 * L))]
    return pl.pallas_call(
        _hyena_kernel,
        out_shape=jax.ShapeDtypeStruct((T_ALL, HY_W), BF16),
        grid=(MIXER_STEPS,),
        in_specs=[
            pl.BlockSpec((STEP_ROWS, 3 * HY_W), lambda i: (i, 0)),
            pl.BlockSpec((None, 3, 3 * HY_W), lambda i: (layer, 0, 0)),
            pl.BlockSpec((None, 1, 3 * HY_W), lambda i: (layer, 0, 0)),
            pl.BlockSpec((None, 2, HY_W), lambda i: (layer, 0, 0)),
        ] + table_specs,
        out_specs=pl.BlockSpec((STEP_ROWS, HY_W), lambda i: (i, 0)),
        compiler_params=_params("parallel"),
        name="hyena",
    )(z_main, conv_w, conv_b.reshape(DEPTH, 1, 3 * HY_W), bias, *tables)


def _win_masks():
    lane = lax.broadcasted_iota(jnp.int32, (1, LANE), 1)
    return lane < WIN_HD, lane >= WIN_HD


def _win_head_operands(q, k, v, h):
    lo_mask, hi_mask = _win_masks()
    col = h // 2
    lo = h % 2 == 0
    q128 = jnp.where(lo_mask if lo else hi_mask, q[:, col * LANE:(col + 1) * LANE], 0.0)
    swap = h in (1, 2)
    if swap:
        k = pltpu.roll(k, WIN_HD, axis=1)
        v = pltpu.roll(v, WIN_HD, axis=1)
    return q128, k, v, lo


def _win_kernel(sink_ref, q_ref, kv_ref, ck_ref, cv_ref, c_ref, su_ref, sd_ref, o_ref, *, layer):
    _group_step(
        lambda: _win_ctx_body(sink_ref, q_ref, kv_ref, o_ref, layer),
        lambda: _win_lat_body(sink_ref, q_ref, kv_ref, ck_ref, cv_ref, c_ref, su_ref, sd_ref, o_ref, layer))


def _win_ctx_body(sink_ref, q_ref, kv_ref, o_ref, layer):
    lo_mask, hi_mask = _win_masks()
    qscale = WIN_HD ** -0.5 * LOG2E
    for g in range(CTX_SEQS_PER_STEP):
        sl = slice(g * SEQ, (g + 1) * SEQ)
        q = q_ref[sl, :].astype(F32) * qscale
        k = kv_ref[sl, 0:LANE].astype(F32)
        v = kv_ref[sl, LANE:2 * LANE].astype(F32)
        cols = []
        for col in range(2):
            acc = None
            for h in (2 * col, 2 * col + 1):
                q128, kk, vv, lo = _win_head_operands(q, k, v, h)
                s = _dot_nt(q128, kk)
                sink = sink_ref[layer, h] * LOG2E
                m = jnp.maximum(jnp.max(s, -1, keepdims=True), sink)
                p = jnp.exp2(s - m)
                den = jnp.sum(p, -1, keepdims=True) + jnp.exp2(sink - m)
                o = _dot(p, vv) / den
                o = jnp.where(lo_mask if lo else hi_mask, o, 0.0)
                acc = o if acc is None else acc + o
            cols.append(acc)
        o_ref[sl, :] = jnp.concatenate(cols, 1).astype(o_ref.dtype)


def _win_lat_body(sink_ref, q_ref, kv_ref, ck_ref, cv_ref, c_ref, su_ref, sd_ref, o_ref, layer):
    L = DEC_SEQ
    half = WIN_HD // 2
    c, su, sd = c_ref[...], su_ref[...], sd_ref[...]
    qscale = WIN_HD ** -0.5 * LOG2E
    q = jnp.concatenate(
        [_rope128(q_ref[:, i * LANE:(i + 1) * LANE].astype(F32), c, su, sd, half) for i in range(2)],
        1) * qscale
    k = _rope128(kv_ref[:, 0:LANE].astype(F32), c, su, sd, half)
    v = kv_ref[:, LANE:2 * LANE].astype(F32)
    ck = ck_ref[...]
    cv = cv_ref[...]
    lo_mask, hi_mask = _win_masks()
    nb = L // CHUNK
    assert WINDOW == CHUNK
    rr = lax.broadcasted_iota(jnp.int32, (CHUNK, CHUNK), 0)
    cc = lax.broadcasted_iota(jnp.int32, (CHUNK, CHUNK), 1)
    band = {-1: jnp.where(cc >= rr, 0.0, NEG), 0: jnp.zeros((CHUNK, CHUNK), F32),
            1: jnp.where(cc <= rr, 0.0, NEG)}
    cols = []
    for col in range(2):
        acc_blocks = [None] * nb
        for h in (2 * col, 2 * col + 1):
            q128, kk, vv, lo = _win_head_operands(q, k, v, h)
            _, ckk, cvv, _ = _win_head_operands(q, ck, cv, h)
            sink = sink_ref[layer, h] * LOG2E
            for n in range(nb):
                blocks = [d for d in (-1, 0, 1) if 0 <= n + d < nb]
                k0 = (n + blocks[0]) * CHUNK
                k1 = (n + blocks[-1] + 1) * CHUNK
                qn = q128[n * CHUNK:(n + 1) * CHUNK]
                s_loc = _dot_nt(qn, kk[k0:k1]) + jnp.concatenate([band[d] for d in blocks], 1)
                s_ctx = _dot_nt(qn, ckk)
                m = jnp.maximum(jnp.maximum(jnp.max(s_loc, -1, keepdims=True),
                                            jnp.max(s_ctx, -1, keepdims=True)), sink)
                p_loc = jnp.exp2(s_loc - m)
                p_ctx = jnp.exp2(s_ctx - m)
                den = (jnp.sum(p_loc, -1, keepdims=True) + jnp.sum(p_ctx, -1, keepdims=True)
                       + jnp.exp2(sink - m))
                o = (_dot(p_loc, vv[k0:k1]) + _dot(p_ctx, cvv)) / den
                o = jnp.where(lo_mask if lo else hi_mask, o, 0.0)
                acc_blocks[n] = o if acc_blocks[n] is None else acc_blocks[n] + o
        cols.append(jnp.concatenate(acc_blocks, 0))
    o_ref[...] = jnp.concatenate(cols, 1).astype(o_ref.dtype)


def _win(z_main, sink, cache_k, cache_v, layer):
    c, su, sd, _ = _rope_tables(DEC_SEQ, WIN_HD, LANE)
    tab = _const_spec((DEC_SEQ, LANE))
    cache = pl.BlockSpec((None, None, PAST_LEN, LANE), lambda i: (_lat_index(i), layer, 0, 0))
    return pl.pallas_call(
        functools.partial(_win_kernel, layer=layer),
        out_shape=jax.ShapeDtypeStruct((T_ALL, WIN_HEADS * WIN_HD), BF16),
        grid=(MIXER_STEPS,),
        in_specs=[
            pl.BlockSpec(memory_space=pltpu.SMEM),
            pl.BlockSpec((STEP_ROWS, 256), lambda i: (i, COL_WQ // 256)),
            pl.BlockSpec((STEP_ROWS, 256), lambda i: (i, COL_WK // 256)),
            cache, cache, tab, tab, tab,
        ],
        out_specs=pl.BlockSpec((STEP_ROWS, 256), lambda i: (i, 0)),
        compiler_params=_params("parallel"),
        name="win",
    )(sink, z_main, z_main, cache_k, cache_v, jnp.asarray(c), jnp.asarray(su), jnp.asarray(sd))


def _ret_kernel(*refs, layer):
    _group_step(lambda: _ret_body(*refs, L=SEQ, layer=layer, ctx=True),
                lambda: _ret_body(*refs, L=DEC_SEQ, layer=layer, ctx=False))


def _ret_body(df_ref, db_ref, q_ref, k_ref, v0_ref, v1_ref, g0_ref, g1_ref, s0f_ref, s0b_ref,
              o_ref, sf_out, sb_out, s_ref, cross_ref, *, L, layer, ctx):
    seqs = STEP_ROWS // L
    if not ctx:
        sf_out[...] = jnp.zeros_like(sf_out)
        sb_out[...] = jnp.zeros_like(sb_out)
    C = CHUNK
    nc = L // C
    H = RET_HEADS
    qw = H * RET_DK
    vw = H * RET_DV

    def lane_table(width, per_head, fn):
        pos = lax.broadcasted_iota(jnp.int32, (C, per_head), 0).astype(F32)
        return jnp.concatenate([fn(h, pos) for h in range(H)], 1)

    def log_gamma(ref, h):
        d = jnp.full((1, 1), ref[layer, h], F32)
        return jnp.log(jax.nn.sigmoid(d))

    lgf = [log_gamma(df_ref, h) for h in range(H)]
    lgb = [log_gamma(db_ref, h) for h in range(H)]

    def tables(lg, reverse):
        if reverse:
            dq = lane_table(vw, RET_DV, lambda h, pos: jnp.exp((C - pos) * lg[h]))
            dk = lane_table(qw, RET_DK, lambda h, pos: jnp.exp(pos * lg[h]))
        else:
            dq = lane_table(vw, RET_DV, lambda h, pos: jnp.exp((pos + 1.0) * lg[h]))
            dk = lane_table(qw, RET_DK, lambda h, pos: jnp.exp((C - 1.0 - pos) * lg[h]))
        dc = jnp.concatenate([jnp.broadcast_to(jnp.exp(C * lg[h]), (1, RET_DV)) for h in range(H)], 1)
        return dq, dk, dc

    tab_f = tables(lgf, False)
    tab_b = tables(lgb, True)
    ii = lax.broadcasted_iota(jnp.int32, (C, C), 0)
    jj = lax.broadcasted_iota(jnp.int32, (C, C), 1)
    diff = (ii - jj).astype(F32)
    dmats = [jnp.where(diff >= 0, jnp.exp(jnp.maximum(diff, 0.0) * lgf[h]), 0.0)
             + jnp.where(diff <= 0, jnp.exp(jnp.maximum(-diff, 0.0) * lgb[h]), 0.0) for h in range(H)]
    dmat_stack = jnp.concatenate(dmats, 0)
    lane_q = lax.broadcasted_iota(jnp.int32, (1, qw), 1) // RET_DK

    srow = lax.broadcasted_iota(jnp.int32, (qw, vw), 0) // RET_DK
    scol = lax.broadcasted_iota(jnp.int32, (qw, vw), 1) // RET_DV
    diag = srow == scol

    for g in range(seqs):
        base = g * L
        rows_all = slice(base, base + L)
        q_all = q_ref[rows_all, :].astype(F32)
        k_all = k_ref[rows_all, :].astype(F32) * (RET_DK ** -0.5)
        v_all = jnp.concatenate([v0_ref[rows_all, :], v1_ref[rows_all, :]], 1).astype(F32)
        g_all = jnp.concatenate([g0_ref[rows_all, :], g1_ref[rows_all, :]], 1).astype(F32)

        def scan(tabs, reverse, s0_ref, s_out):
            dq, dk, dc = tabs
            s_ref[g] = jnp.zeros((qw, vw), F32)
            if s0_ref is not None:
                for h in range(H):
                    s_ref[g, h * RET_DK:(h + 1) * RET_DK, h * RET_DV:(h + 1) * RET_DV] = s0_ref[h]
            order = range(nc - 1, -1, -1) if reverse else range(nc)
            for ci in order:
                sl = slice(ci * C, (ci + 1) * C)
                rs = slice(base + ci * C, base + (ci + 1) * C)
                qc, kc, vc = q_all[sl], k_all[sl], v_all[sl]
                st = s_ref[g]
                cross = _dot(qc, st) * dq
                if reverse:
                    cross_ref[rs, :] = cross_ref[rs, :] + cross
                else:
                    cross_ref[rs, :] = cross
                upd = jnp.where(diag, _dot_tn(kc * dk, vc), 0.0)
                s_ref[g] = st * dc + upd
            if s_out is not None:
                for h in range(H):
                    s_out[g, h] = s_ref[g, h * RET_DK:(h + 1) * RET_DK, h * RET_DV:(h + 1) * RET_DV]

        scan(tab_f, False, None if ctx else s0f_ref, sf_out if ctx else None)
        scan(tab_b, True, None if ctx else s0b_ref, sb_out if ctx else None)

        for ci in range(nc):
            sl = slice(ci * C, (ci + 1) * C)
            rs = slice(base + ci * C, base + (ci + 1) * C)
            q_stack = jnp.concatenate([jnp.where(lane_q == h, q_all[sl], 0.0) for h in range(H)], 0)
            att = _dot_nt(q_stack, k_all[sl]) * dmat_stack
            ov = _dot(att, v_all[sl])
            for h in range(H):
                hv = slice(h * RET_DV, (h + 1) * RET_DV)
                o = ov[h * C:(h + 1) * C, hv] + cross_ref[rs, hv]
                gt = g_all[sl, hv]
                o_ref[rs, hv] = ((gt * jax.nn.sigmoid(gt)) * _layer_norm(o)).astype(o_ref.dtype)


def _retention(z_main, dec_f, dec_b, s0f, s0b, layer):
    def zcol(col):
        return pl.BlockSpec((STEP_ROWS, 256), lambda i: (i, col // 256))

    smem = pl.BlockSpec(memory_space=pltpu.SMEM)
    z_specs = [zcol(COL_RQ), zcol(COL_RK), zcol(COL_RV), zcol(COL_RV + 256),
               zcol(COL_RG), zcol(COL_RG + 256)]
    s0_spec = pl.BlockSpec((None, None, RET_HEADS, RET_DK, RET_DV),
                           lambda i: (_lat_index(i), layer, 0, 0, 0))
    st_shape = jax.ShapeDtypeStruct((MIXER_STEPS * CTX_SEQS_PER_STEP, RET_HEADS, RET_DK, RET_DV), F32)
    st_spec = pl.BlockSpec((CTX_SEQS_PER_STEP, RET_HEADS, RET_DK, RET_DV), lambda i: (i, 0, 0, 0))
    return pl.pallas_call(
        functools.partial(_ret_kernel, layer=layer),
        out_shape=(jax.ShapeDtypeStruct((T_ALL, RET_HEADS * RET_DV), BF16), st_shape, st_shape),
        grid=(MIXER_STEPS,),
        in_specs=[smem, smem] + z_specs + [s0_spec, s0_spec],
        out_specs=(pl.BlockSpec((STEP_ROWS, RET_HEADS * RET_DV), lambda i: (i, 0)), st_spec, st_spec),
        scratch_shapes=[pltpu.VMEM((CTX_SEQS_PER_STEP, RET_HEADS * RET_DK, RET_HEADS * RET_DV), F32),
                        pltpu.VMEM((STEP_ROWS, RET_HEADS * RET_DV), F32)],
        compiler_params=_params("parallel"),
        name="retention",
    )(dec_f, dec_b, *([z_main] * 6), s0f, s0b)


def _rms_norm(x, g):
    return x * lax.rsqrt(jnp.mean(x * x, -1, keepdims=True) + RMS_EPS) * g


def _mla_keys(kn, kr):
    lane_r = lax.broadcasted_iota(jnp.int32, (1, LANE), 1)
    return jnp.concatenate([kn, jnp.where(lane_r < MLA_ROPE, kr, 0.0)], 1).astype(BF16)


def _mla_attend(qn, qr, k_cat, vv, o_ref, row0):
    qscale = (MLA_NOPE + MLA_ROPE) ** -0.5 * LOG2E
    lane_n = lax.broadcasted_iota(jnp.int32, (1, MLA_HEADS * MLA_NOPE), 1) // MLA_NOPE
    lane_r = lax.broadcasted_iota(jnp.int32, (1, LANE), 1)
    qn = qn * qscale
    qr = qr * qscale
    lq = qn.shape[0]
    heads = []
    for h in range(MLA_HEADS):
        qnh = jnp.where(lane_n == h, qn, 0.0)
        qrh = qr if h == 0 else pltpu.roll(qr, LANE - h * MLA_ROPE, axis=1)
        qrh = jnp.where(lane_r < MLA_ROPE, qrh, 0.0)
        heads.append(jnp.concatenate([qnh, qrh], 1).astype(BF16))
    s = _dot_nt(jnp.concatenate(heads, 0), k_cat)
    m = jnp.max(s, -1, keepdims=True)
    p = jnp.exp2(s - m)
    den = jnp.sum(p, -1, keepdims=True)
    o = _dot(p, vv) / den
    acc = None
    for h in range(MLA_HEADS):
        oh = jnp.where(lane_n == h, o[h * lq:(h + 1) * lq], 0.0)
        acc = oh if acc is None else acc + oh
    o_ref[row0:row0 + lq, :] = acc.astype(o_ref.dtype)


def _mla_kernel(cq_ref, ckv_ref, kr_ref, cckv_ref, ckr_ref, c_ref, su_ref, sd_ref,
                qg_ref, kg_ref, wqn_ref, wqr_ref, wk_ref, wv_ref, o_ref, ckvn_ref):
    weights = (qg_ref, kg_ref, wqn_ref, wqr_ref, wk_ref, wv_ref)
    _group_step(
        lambda: _mla_ctx_body(cq_ref, ckv_ref, kr_ref, *weights, o_ref, ckvn_ref),
        lambda: _mla_lat_body(cq_ref, ckv_ref, kr_ref, cckv_ref, ckr_ref, c_ref, su_ref, sd_ref,
                              *weights, o_ref, ckvn_ref))


def _mla_ctx_body(cq_ref, ckv_ref, kr_ref, qg_ref, kg_ref, wqn_ref, wqr_ref, wk_ref, wv_ref,
                  o_ref, ckvn_ref):
    cqn = _rms_norm(cq_ref[...].astype(F32), qg_ref[...])
    qn = _dot(cqn, wqn_ref[...])
    qr = _dot(cqn, wqr_ref[...])
    ckvn = _rms_norm(ckv_ref[...].astype(F32), kg_ref[...])
    ckvn_ref[...] = ckvn
    k_cat = _mla_keys(_dot(ckvn, wk_ref[...]), kr_ref[...].astype(F32))
    vv = _dot(ckvn, wv_ref[...]).astype(BF16)
    for g in range(CTX_SEQS_PER_STEP):
        sl = slice(g * SEQ, (g + 1) * SEQ)
        _mla_attend(qn[sl], qr[sl], k_cat[sl], vv[sl], o_ref, g * SEQ)


def _mla_lat_body(cq_ref, ckv_ref, kr_ref, cckv_ref, ckr_ref, c_ref, su_ref, sd_ref,
                  qg_ref, kg_ref, wqn_ref, wqr_ref, wk_ref, wv_ref, o_ref, ckvn_ref):
    half = MLA_ROPE // 2
    c, su, sd = c_ref[...], su_ref[...], sd_ref[...]
    cqn = _rms_norm(cq_ref[...].astype(F32), qg_ref[...])
    qn = _dot(cqn, wqn_ref[...])
    qr = _rope128(_dot(cqn, wqr_ref[...]), c, su, sd, half)
    ckvn = _rms_norm(ckv_ref[...].astype(F32), kg_ref[...])
    ckvn_ref[...] = ckvn
    ckv_all = jnp.concatenate([ckvn, cckv_ref[...]], 0)
    vv = _dot(ckv_all, wv_ref[...]).astype(BF16)
    kr = jnp.concatenate([_rope128(kr_ref[...].astype(F32), c, su, sd, half), ckr_ref[...]], 0)
    k_cat = _mla_keys(_dot(ckv_all, wk_ref[...]), kr)
    rows_per_call = 256
    for n in range(DEC_SEQ // rows_per_call):
        rows = slice(n * rows_per_call, (n + 1) * rows_per_call)
        _mla_attend(qn[rows], qr[rows], k_cat, vv, o_ref, n * rows_per_call)


def _mla(z_main, cache_ckv, cache_kr_pad, weights, layer):
    c, su, sd, _ = _rope_tables(DEC_SEQ, MLA_ROPE, LANE)
    tab = _const_spec((DEC_SEQ, LANE))
    cache = pl.BlockSpec((None, None, PAST_LEN, LANE), lambda i: (_lat_index(i), layer, 0, 0))

    def weight(*shape):
        return pl.BlockSpec((None,) + shape, lambda i: (layer, 0, 0))

    return pl.pallas_call(
        _mla_kernel,
        out_shape=(jax.ShapeDtypeStruct((T_ALL, MLA_HEADS * MLA_V), BF16),
                   jax.ShapeDtypeStruct((T_ALL, MLA_KV_LORA), F32)),
        grid=(MIXER_STEPS,),
        in_specs=[
            pl.BlockSpec((STEP_ROWS, 256), lambda i: (i, COL_CQ // 256)),
            pl.BlockSpec((STEP_ROWS, LANE), lambda i: (i, COL_CKV // LANE)),
            pl.BlockSpec((STEP_ROWS, LANE), lambda i: (i, COL_KROPE // LANE)),
            cache, cache, tab, tab, tab,
            weight(1, MLA_Q_LORA), weight(1, MLA_KV_LORA),
            weight(MLA_Q_LORA, MLA_HEADS * MLA_NOPE), weight(MLA_Q_LORA, MLA_HEADS * MLA_ROPE),
            weight(MLA_KV_LORA, MLA_HEADS * MLA_NOPE), weight(MLA_KV_LORA, MLA_HEADS * MLA_V),
        ],
        out_specs=(pl.BlockSpec((STEP_ROWS, 256), lambda i: (i, 0)),
                   pl.BlockSpec((STEP_ROWS, MLA_KV_LORA), lambda i: (i, 0))),
        compiler_params=_params("parallel"),
        name="mla",
    )(z_main, z_main, z_main, cache_ckv, cache_kr_pad,
      jnp.asarray(c), jnp.asarray(su), jnp.asarray(sd), *weights)


def _route(logits_t, rb):
    scores = jax.nn.sigmoid(logits_t)
    biased = scores + rb
    sc = [scores[e:e + 1, :] for e in range(N_EXPERTS)]
    bi = [biased[e:e + 1, :] for e in range(N_EXPERTS)]
    epg = EXPERTS_PER_GROUP
    gsum = []
    for g in range(N_GROUPS):
        v = bi[g * epg:(g + 1) * epg]
        best = None
        for a in range(epg):
            for b in range(a + 1, epg):
                pair = v[a] + v[b]
                best = pair if best is None else jnp.maximum(best, pair)
        gsum.append(best)
    combine = []
    sel = []
    for g in range(N_GROUPS):
        is_best = None
        for g2 in range(N_GROUPS):
            if g2 == g:
                continue
            c = gsum[g] > gsum[g2] if g2 < g else gsum[g] >= gsum[g2]
            is_best = c if is_best is None else jnp.logical_and(is_best, c)
        for a in range(epg):
            e = g * epg + a
            rank = jnp.zeros_like(bi[e])
            for b in range(epg):
                if b == a:
                    continue
                e2 = g * epg + b
                ahead = bi[e2] >= bi[e] if b < a else bi[e2] > bi[e]
                rank = rank + jnp.where(ahead, 1.0, 0.0)
            sel.append(jnp.logical_and(is_best, rank < 2.0))
    wsum = None
    for e in range(N_EXPERTS):
        w = jnp.where(sel[e], sc[e], 0.0)
        wsum = w if wsum is None else wsum + w
    for e in range(N_EXPERTS):
        combine.append(jnp.where(sel[e], ROUTE_SCALE * sc[e] / wsum, 0.0))
    return jnp.concatenate(combine, 0)


MERGE_BRANCH_ROWS = (HY_W, WIN_HEADS * WIN_HD, RET_HEADS * RET_DV, MLA_HEADS * MLA_V)
MERGE_ROWS = sum(MERGE_BRANCH_ROWS) + D_MODEL


def _merge_kernel(ya_ref, yb_ref, yc_ref, yd_ref, gt_ref, xc_ref, xl_ref, m_ref,
                  w_ref, g_ref, b_ref, rw_ref, rb_ref,
                  x1_ref, h2_ref, cmb_ref, *, ctx_tiles, sub_rows):
    D = D_MODEL
    rw = rw_ref[...]
    rw_hi = rw.astype(BF16)
    rw_lo = (rw - rw_hi.astype(F32)).astype(BF16)
    g1 = m_ref[:, 2 * D:3 * D]
    s2 = m_ref[:, 3 * D:4 * D]
    sc2 = m_ref[:, 4 * D:5 * D]
    is_ctx = pl.program_id(0) < ctx_tiles
    offs = np.cumsum((0,) + MERGE_BRANCH_ROWS)
    branches = tuple((y_ref, slice(int(offs[i]), int(offs[i + 1])))
                     for i, y_ref in enumerate((ya_ref, yb_ref, yc_ref, yd_ref)))
    w_out_rows = slice(int(offs[-1]), MERGE_ROWS)
    for r0 in range(0, x1_ref.shape[0], sub_rows):
        rows = slice(r0, r0 + sub_rows)
        merged = None
        for i, (y_ref, w_rows) in enumerate(branches):
            t = gt_ref[rows, i * D:(i + 1) * D] * jnp.dot(
                y_ref[rows, :], w_ref[w_rows, :], preferred_element_type=F32).astype(BF16)
            merged = t if merged is None else merged + t
        out1 = jnp.dot(merged, w_ref[w_out_rows, :], preferred_element_type=F32)
        x = jnp.where(is_ctx, xc_ref[rows, :], xl_ref[rows, :])
        x1 = _layer_norm(ALPHA * x + g1 * out1) * g_ref[...] + b_ref[...]
        x1_ref[rows, :] = x1
        h2 = _layer_norm(x1) * (1.0 + sc2) + s2
        h2_hi = h2.astype(BF16)
        h2_ref[rows, :] = h2_hi
        h2_lo = (h2 - h2_hi.astype(F32)).astype(BF16)
        logits = (jnp.dot(h2_hi, rw_hi, preferred_element_type=F32)
                  + (jnp.dot(h2_lo, rw_hi, preferred_element_type=F32)
                     + jnp.dot(h2_hi, rw_lo, preferred_element_type=F32)))
        cmb_ref[:, rows] = _route(logits.T[0:N_EXPERTS], rb_ref[...])


def _merge(ya, yb, yc, yd, gates, x_ctx, x_lat, mods, w_merge, ln1_g, ln1_b,
           router_w, router_b, layer):
    tm = 512
    row = _mod_row(tm)
    D = D_MODEL
    ctx_tiles = T_CTX // tm

    def tile(w):
        return pl.BlockSpec((tm, w), lambda i: (i, 0))

    def weight(k, n):
        return pl.BlockSpec((None, k, n), lambda i: (layer, 0, 0))

    return pl.pallas_call(
        functools.partial(_merge_kernel, ctx_tiles=ctx_tiles, sub_rows=256),
        out_shape=(jax.ShapeDtypeStruct((T_ALL, D), F32),
                   jax.ShapeDtypeStruct((T_ALL, D), BF16),
                   jax.ShapeDtypeStruct((N_EXPERTS, T_ALL), F32)),
        grid=(T_ALL // tm,),
        in_specs=[
            tile(256), tile(256), tile(512), tile(256), tile(4 * D),
            pl.BlockSpec((tm, D), lambda i: (jnp.minimum(i, ctx_tiles - 1), 0)),
            pl.BlockSpec((tm, D), lambda i: (jnp.maximum(i - ctx_tiles, 0), 0)),
            _mod_spec(layer, row),
            weight(MERGE_ROWS, D), weight(1, D), weight(1, D),
            pl.BlockSpec((D, LANE), lambda i: (0, 0)),
            pl.BlockSpec((N_EXPERTS, 1), lambda i: (0, 0)),
        ],
        out_specs=(tile(D), tile(D), pl.BlockSpec((N_EXPERTS, tm), lambda i: (0, i))),
        compiler_params=_params("parallel"),
        name="merge",
    )(ya, yb, yc, yd, gates, x_ctx, x_lat, mods, w_merge,
      ln1_g.reshape(DEPTH, 1, D), ln1_b.reshape(DEPTH, 1, D), router_w,
      router_b.reshape(N_EXPERTS, 1))


MOE_EXPERTS_PER_STEP = 2


def _moe_kernel(*refs, next_h):
    if next_h:
        (h_ref, c_ref, x1_ref, m_ref, wg_ref, wu_ref, wd_ref, g_ref, b_ref, mn_ref,
         o_ref, hn_ref, acc_ref) = refs[-13:]
    else:
        h_ref, c_ref, x1_ref, m_ref, wg_ref, wu_ref, wd_ref, g_ref, b_ref, o_ref, acc_ref = refs
    eg = pl.program_id(1)

    @pl.when(eg == 0)
    def _():
        acc_ref[...] = jnp.zeros_like(acc_ref)

    h = h_ref[...]
    cmb = c_ref[...]
    lane = lax.broadcasted_iota(jnp.int32, cmb.shape, 1)
    hid = []
    for k in range(MOE_EXPERTS_PER_STEP):
        gate = jnp.dot(h, wg_ref[k].astype(BF16), preferred_element_type=F32)
        up = jnp.dot(h, wu_ref[k].astype(BF16), preferred_element_type=F32)
        e = eg * MOE_EXPERTS_PER_STEP + k
        ce = jnp.sum(jnp.where(lane == e, cmb, 0.0), -1, keepdims=True)
        sig = 0.5 * jnp.tanh(0.5 * gate) + 0.5
        hid.append((gate * sig * (up * ce)).astype(BF16))
    wd = wd_ref[...].reshape(MOE_EXPERTS_PER_STEP * D_EXPERT, D_MODEL).astype(BF16)
    acc_ref[...] += jnp.dot(jnp.concatenate(hid, 1), wd, preferred_element_type=F32)

    @pl.when(eg == N_EXPERTS // MOE_EXPERTS_PER_STEP - 1)
    def _():
        g2 = m_ref[:, 5 * D_MODEL:6 * D_MODEL]
        y = _layer_norm(ALPHA * x1_ref[...] + g2 * acc_ref[...])
        y = y * g_ref[...] + b_ref[...]
        o_ref[...] = y
        if next_h:
            s1 = mn_ref[:, 0:D_MODEL]
            sc1 = mn_ref[:, D_MODEL:2 * D_MODEL]
            hn_ref[...] = (_layer_norm(y) * (1.0 + sc1) + s1).astype(hn_ref.dtype)


def _moe(h2, combine, x1, mods, w_gate, w_up, w_down, ln2_g, ln2_b, layer, row0, n_rows,
         next_h=False, prev_h=None):
    tm = 1024
    row = _mod_row(tm)
    D = D_MODEL
    t0 = row0 // tm
    eps = MOE_EXPERTS_PER_STEP
    mod_spec = _mod_spec(layer, lambda i, e: row(t0 + i))
    in_specs = [
        pl.BlockSpec((tm, D), lambda i, e: (t0 + i, 0)),
        pl.BlockSpec((tm, N_EXPERTS), lambda i, e: (t0 + i, 0)),
        pl.BlockSpec((tm, D), lambda i, e: (t0 + i, 0)),
        mod_spec,
        pl.BlockSpec((None, eps, D, D_EXPERT), lambda i, e: (layer, e, 0, 0)),
        pl.BlockSpec((None, eps, D, D_EXPERT), lambda i, e: (layer, e, 0, 0)),
        pl.BlockSpec((None, eps, D_EXPERT, D), lambda i, e: (layer, e, 0, 0)),
        pl.BlockSpec((None, 1, D), lambda i, e: (layer, 0, 0)),
        pl.BlockSpec((None, 1, D), lambda i, e: (layer, 0, 0)),
    ]
    args = [h2, combine, x1, mods, w_gate, w_up, w_down,
            ln2_g.reshape(DEPTH, 1, D), ln2_b.reshape(DEPTH, 1, D)]
    out_shape = jax.ShapeDtypeStruct((n_rows, D), F32)
    out_specs = pl.BlockSpec((tm, D), lambda i, e: (i, 0))
    aliases = {}
    if next_h:
        in_specs.append(_mod_spec(layer + 1, lambda i, e: row(t0 + i)))
        args.append(mods)
        out_shape = (out_shape, jax.ShapeDtypeStruct((T_ALL, D), BF16))
        out_specs = (out_specs, pl.BlockSpec((tm, D), lambda i, e: (t0 + i, 0)))
        if prev_h is not None:
            in_specs = [pl.BlockSpec(memory_space=pl.ANY)] + in_specs
            args = [prev_h] + args
            aliases = {0: 1}
    return pl.pallas_call(
        functools.partial(_moe_kernel, next_h=next_h),
        out_shape=out_shape,
        grid=(n_rows // tm, N_EXPERTS // eps),
        in_specs=in_specs,
        out_specs=out_specs,
        input_output_aliases=aliases,
        scratch_shapes=[pltpu.VMEM((tm, D), F32)],
        compiler_params=_params("parallel", "arbitrary"),
        name="moe",
    )(*args)


def kernel(x_prompt, x_sample, cache_win_k, cache_win_v, cache_mla_ckv, cache_mla_krope,
           state_ret_fwd, state_ret_bwd, c, c_ctx, w_ada, b_ada, w_in,
           hy_conv_w, hy_conv_b, hy_w1, hy_b1, hy_w2, hy_b2, hy_w3, hy_bias,
           win_sink, ret_decay_fwd, ret_decay_bwd, mla_q_norm, mla_kv_norm, mla_w_uq, mla_w_ukv,
           w_br_a, w_br_b, w_br_c, w_br_d, w_out, ln1_g, ln1_b, ln2_g, ln2_b,
           router_w, router_b, moe_w_gate, moe_w_up, moe_w_down):
    D = D_MODEL
    x_ctx = x_prompt.reshape(T_CTX, D)
    x_lat = x_sample.reshape(T_LAT, D)

    mods = _ada_mods(c_ctx, c, w_ada, b_ada)

    w_in_t = jnp.swapaxes(w_in, 1, 2).reshape(DEPTH * IN_COLS, D)
    cache_k = cache_win_k.reshape(DEC_BATCH, DEPTH, PAST_LEN, WIN_KV_HEADS * WIN_HD)
    cache_v = cache_win_v.reshape(DEC_BATCH, DEPTH, PAST_LEN, WIN_KV_HEADS * WIN_HD)
    cache_kr = jnp.pad(cache_mla_krope, ((0, 0), (0, 0), (0, 0), (0, LANE - MLA_ROPE)))

    uq = mla_w_uq.reshape(DEPTH, MLA_Q_LORA, MLA_HEADS, MLA_NOPE + MLA_ROPE)
    ukv = mla_w_ukv.reshape(DEPTH, MLA_KV_LORA, MLA_HEADS, MLA_NOPE + MLA_V)
    mla_weights = (
        mla_q_norm.reshape(DEPTH, 1, MLA_Q_LORA),
        mla_kv_norm.reshape(DEPTH, 1, MLA_KV_LORA),
        uq[..., :MLA_NOPE].reshape(DEPTH, MLA_Q_LORA, MLA_HEADS * MLA_NOPE),
        uq[..., MLA_NOPE:].reshape(DEPTH, MLA_Q_LORA, MLA_HEADS * MLA_ROPE),
        ukv[..., :MLA_NOPE].reshape(DEPTH, MLA_KV_LORA, MLA_HEADS * MLA_NOPE),
        ukv[..., MLA_NOPE:].reshape(DEPTH, MLA_KV_LORA, MLA_HEADS * MLA_V),
    )

    hy_w1p = jnp.pad(hy_w1, ((0, 0), (0, LANE - HY_EMB), (0, 0)))
    dft = {}
    for L in (SEQ, DEC_SEQ):
        fwd, inv = _dft_tables(L)
        dft[L] = (jnp.asarray(fwd).astype(BF16), jnp.asarray(inv).astype(BF16))
    router_w_pad = jnp.pad(router_w, ((0, 0), (0, LANE - N_EXPERTS)))
    w_merge = jnp.concatenate([w_br_a, w_br_b, w_br_c, w_br_d, w_out], 1).astype(BF16)

    new_k, new_v, new_ckv, new_kr, new_sf, new_sb = [], [], [], [], [], []
    for l in range(DEPTH):
        if l == 0:
            h = _ln_mod(x_ctx, None, mods, l, 0)
            h = _ln_mod(x_lat, h, mods, l, T_CTX)
        z = _in_proj(h, w_in_t, l, 0, Z_MAIN, Z_MAIN // 2, BF16, gate=False)
        gates = _in_proj(h, w_in_t, l, COL_GATE, 4 * D, D, BF16, gate=True)

        filters = {L: _hy_filters(L, hy_w1p[l], hy_b1[l][None], hy_w2[l], hy_b2[l][None], hy_w3[l],
                                  dft[L][0]) for L in (SEQ, DEC_SEQ)}
        ya = _hyena(z, l, hy_conv_w, hy_conv_b, hy_bias, filters, dft)
        yb = _win(z, win_sink, cache_k, cache_v, l)
        yc, sf, sb = _retention(z, ret_decay_fwd, ret_decay_bwd, state_ret_fwd, state_ret_bwd, l)
        yd, ckvn = _mla(z, cache_mla_ckv, cache_kr, mla_weights, l)

        x1, h2, combine_t = _merge(ya, yb, yc, yd, gates, x_ctx, x_lat, mods, w_merge,
                                   ln1_g, ln1_b, router_w_pad, router_b, l)
        moe_args = (h2, combine_t.T, x1, mods, moe_w_gate, moe_w_up, moe_w_down, ln2_g, ln2_b, l)
        if l + 1 < DEPTH:
            x_ctx, h = _moe(*moe_args, 0, T_CTX, next_h=True)
            x_lat, h = _moe(*moe_args, T_CTX, T_LAT, next_h=True, prev_h=h)
        else:
            x_ctx = _moe(*moe_args, 0, T_CTX)
            x_lat = _moe(*moe_args, T_CTX, T_LAT)

        def ctx_cols(col, width):
            return z[:T_CTX, col:col + width].astype(F32)

        new_k.append(ctx_cols(COL_WK, 128).reshape(BATCH, SEQ, WIN_KV_HEADS, WIN_HD))
        new_v.append(ctx_cols(COL_WV, 128).reshape(BATCH, SEQ, WIN_KV_HEADS, WIN_HD))
        new_ckv.append(ckvn[:T_CTX].reshape(BATCH, SEQ, MLA_KV_LORA))
        new_kr.append(ctx_cols(COL_KROPE, MLA_ROPE).reshape(BATCH, SEQ, MLA_ROPE))
        new_sf.append(sf[:BATCH])
        new_sb.append(sb[:BATCH])

    y_prompt = x_ctx.reshape(BATCH, SEQ, D)
    y_sample = x_lat.reshape(DEC_BATCH, DEC_SEQ, D)
    return (y_prompt, y_sample, jnp.stack(new_k, 1), jnp.stack(new_v, 1), jnp.stack(new_ckv, 1),
            jnp.stack(new_kr, 1), jnp.stack(new_sf, 1), jnp.stack(new_sb, 1))
```

```python
import functools
import math

import numpy as np
import jax
import jax.numpy as jnp
from jax import lax
from jax.experimental import pallas as pl
from jax.experimental.pallas import tpu as pltpu

F32 = jnp.float32
BF16 = jnp.bfloat16

D_MODEL = 1024
BATCH = 16
SEQ = 256
DEPTH = 2
DEC_BATCH = 2
DEC_SEQ = 1024
PAST_LEN = 256
GRID_W = 64
CHUNK = 128
ROPE_BASE = 10000.0
NEG = -1e30
LN_EPS = 1e-5
RMS_EPS = 1e-6
LOG2E = math.log2(math.e)

HY_W = 256
HY_BANDS = 16
HY_EMB = 1 + 2 * HY_BANDS
HY_FFN = 64
HY_FAST_DECAY = 0.3
HY_SLOW_DECAY = 1.5
HY_TARGET = 1e-2

WIN_HEADS = 4
WIN_KV_HEADS = 2
WIN_HD = 64
WINDOW = 128

RET_HEADS = 4
RET_DK = 64
RET_DV = 128

MLA_HEADS = 4
MLA_Q_LORA = 256
MLA_KV_LORA = 128
MLA_NOPE = 64
MLA_ROPE = 32
MLA_V = 64

N_EXPERTS = 16
N_GROUPS = 4
EXPERTS_PER_GROUP = N_EXPERTS // N_GROUPS
D_EXPERT = 256
ROUTE_SCALE = 2.5

ALPHA = (2.0 * DEPTH) ** 0.25

T_CTX = BATCH * SEQ
T_LAT = DEC_BATCH * DEC_SEQ
T_ALL = T_CTX + T_LAT

COL_HY = 0
COL_WQ = 768
COL_WK = 1024
COL_WV = 1152
COL_RQ = 1280
COL_RK = 1536
COL_RV = 1792
COL_RG = 2304
COL_CQ = 2816
COL_CKV = 3072
COL_KROPE = 3200
COL_GATE = 3232
IN_COLS = COL_GATE + 4 * D_MODEL
Z_MAIN = 3328

LANE = 128
STEP_ROWS = 1024
CTX_SEQS_PER_STEP = STEP_ROWS // SEQ
CTX_STEPS = T_CTX // STEP_ROWS
MIXER_STEPS = T_ALL // STEP_ROWS
VMEM_LIMIT = 56 * 1024 * 1024


def _params(*sem):
    return pltpu.CompilerParams(dimension_semantics=sem, vmem_limit_bytes=VMEM_LIMIT)


def _dot(a, b):
    return jnp.dot(a.astype(BF16), b.astype(BF16), preferred_element_type=F32)


def _dot_split(a, b):
    a_hi = a.astype(BF16)
    a_lo = (a - a_hi.astype(F32)).astype(BF16)
    b_hi = b.astype(BF16)
    b_lo = (b - b_hi.astype(F32)).astype(BF16)

    def mm(x, y):
        return jnp.dot(x, y, preferred_element_type=F32)

    return mm(a_hi, b_hi) + (mm(a_lo, b_hi) + mm(a_hi, b_lo))


def _dot_nt(a, b):
    return lax.dot_general(a.astype(BF16), b.astype(BF16), (((1,), (1,)), ((), ())),
                           preferred_element_type=F32)


def _dot_tn(a, b):
    return lax.dot_general(a.astype(BF16), b.astype(BF16), (((0,), (0,)), ((), ())),
                           preferred_element_type=F32)


def _layer_norm(x):
    mu = jnp.mean(x, -1, keepdims=True)
    xc = x - mu
    var = jnp.mean(xc * xc, -1, keepdims=True)
    return xc * lax.rsqrt(var + LN_EPS)


def _mod_row(tile_rows):
    def row(i):
        start = i * tile_rows
        return jnp.where(start < T_CTX, 0, 1 + (start - T_CTX) // DEC_SEQ)
    return row


@functools.lru_cache(maxsize=None)
def _dft_tables(L):
    f = np.arange(L, dtype=np.int64)[:, None]
    s = np.arange(L, dtype=np.int64)[None, :]
    ang = np.pi * ((f * s) % (2 * L)).astype(np.float64) / L
    cos = np.cos(ang)
    sin = np.sin(ang)
    alt = np.where(np.arange(L) % 2 == 0, 1.0, -1.0)
    fwd_im = -sin
    fwd_im[0, :] = alt
    fwd = np.concatenate([cos, fwd_im], 0)
    inv_re = cos.T / L
    inv_re[:, 0] = 1.0 / (2 * L)
    inv_im = -sin.T / L
    inv_im[:, 0] = alt / (2 * L)
    inv = np.concatenate([inv_re, inv_im], 1)
    return fwd.astype(np.float32), inv.astype(np.float32)


@functools.lru_cache(maxsize=None)
def _hyena_embedding(L):
    t01 = np.linspace(0.0, 1.0, L, dtype=np.float64)[:, None]
    bands = np.linspace(1e-4, HY_BANDS - 1, HY_BANDS, dtype=np.float64)
    ang = (2.0 * math.pi / L) * np.arange(L, dtype=np.float64)[:, None] * bands[None, :]
    z = np.concatenate([t01, np.cos(ang), -np.sin(ang)], -1)
    zp = np.zeros((L, LANE), np.float64)
    zp[:, :HY_EMB] = z
    deltas = np.abs(np.linspace(math.log(HY_TARGET) / HY_SLOW_DECAY,
                                math.log(HY_TARGET) / HY_FAST_DECAY, HY_W, dtype=np.float64))
    return zp.astype(np.float32), deltas[None, :].astype(np.float32)


@functools.lru_cache(maxsize=None)
def _rope_tables(L, rot_dim, width):
    rows = L // GRID_W
    n_freq = rot_dim // 4
    half = rot_dim // 2
    inv = ROPE_BASE ** (-np.arange(n_freq, dtype=np.float64) / n_freq)
    pos = np.arange(L)
    row = (pos // GRID_W).astype(np.float64)
    col = (pos % GRID_W).astype(np.float64)
    ang = np.concatenate([row[:, None] * inv, col[:, None] * inv], -1)
    cos, sin = np.cos(ang), np.sin(ang)
    zero = np.zeros_like(sin)
    c = np.tile(np.concatenate([cos, cos], -1), (1, width // rot_dim))
    s_up = np.tile(np.concatenate([-sin, zero], -1), (1, width // rot_dim))
    s_dn = np.tile(np.concatenate([zero, sin], -1), (1, width // rot_dim))
    return c.astype(np.float32), s_up.astype(np.float32), s_dn.astype(np.float32), half


def _rope128(x, c, s_up, s_dn, half):
    up = pltpu.roll(x, LANE - half, axis=1)
    dn = pltpu.roll(x, half, axis=1)
    return x * c + up * s_up + dn * s_dn


MOD_ROWS = 1 + DEC_BATCH


def _ada_kernel(ct_ref, w_ref, b_ref, o_ref):
    ct = ct_ref[...]
    s = ct * jax.nn.sigmoid(ct)
    w = w_ref[...]
    for r in range(MOD_ROWS):
        o_ref[r] = jnp.sum(w * s[:, r:r + 1], axis=0, keepdims=True) + b_ref[...]


def _ada_mods(c_ctx, c, w_ada, b_ada):
    tn = 1536
    n = 6 * D_MODEL
    c_cols = jnp.concatenate([c_ctx[:, None], c.T], 1)
    return pl.pallas_call(
        _ada_kernel,
        out_shape=jax.ShapeDtypeStruct((DEPTH, MOD_ROWS, 1, n), F32),
        grid=(DEPTH, n // tn),
        in_specs=[
            pl.BlockSpec((D_MODEL, MOD_ROWS), lambda l, j: (0, 0)),
            pl.BlockSpec((None, D_MODEL, tn), lambda l, j: (l, 0, j)),
            pl.BlockSpec((None, 1, tn), lambda l, j: (l, 0, j)),
        ],
        out_specs=pl.BlockSpec((None, MOD_ROWS, 1, tn), lambda l, j: (l, 0, 0, j)),
        compiler_params=_params("parallel", "parallel"),
        name="ada_mods",
    )(c_cols, w_ada, b_ada.reshape(DEPTH, 1, n))


def _mod_spec(layer, row_of_step):
    return pl.BlockSpec((None, None, 1, 6 * D_MODEL), lambda *g: (layer, row_of_step(*g), 0, 0))


def _lnmod_kernel(*refs):
    x_ref, m_ref, h_ref = refs[-3:]
    y = _layer_norm(x_ref[...])
    s1 = m_ref[:, 0:D_MODEL]
    sc1 = m_ref[:, D_MODEL:2 * D_MODEL]
    h_ref[...] = (y * (1.0 + sc1) + s1).astype(h_ref.dtype)


def _ln_mod(x_group, prev_out, mods, layer, row0):
    tm = 512
    row = _mod_row(tm)
    tile0 = row0 // tm
    in_specs = [
        pl.BlockSpec((tm, D_MODEL), lambda i: (i, 0)),
        _mod_spec(layer, lambda i: row(tile0 + i)),
    ]
    args = [x_group, mods]
    if prev_out is not None:
        in_specs = [pl.BlockSpec(memory_space=pl.ANY)] + in_specs
        args = [prev_out] + args
    return pl.pallas_call(
        _lnmod_kernel,
        out_shape=jax.ShapeDtypeStruct((T_ALL, D_MODEL), BF16),
        grid=(x_group.shape[0] // tm,),
        in_specs=in_specs,
        out_specs=pl.BlockSpec((tm, D_MODEL), lambda i: (tile0 + i, 0)),
        input_output_aliases={} if prev_out is None else {0: 0},
        compiler_params=_params("parallel"),
        name="ln_mod",
    )(*args)


def _proj_kernel(h_ref, w_ref, o_ref, wb_ref, *, gate):
    @pl.when(pl.program_id(1) == 0)
    def _():
        wb_ref[...] = w_ref[...].T.astype(BF16)

    if not gate:
        o_ref[...] = jnp.dot(h_ref[...], wb_ref[...], preferred_element_type=F32).astype(o_ref.dtype)
        return
    sub = 2 * LANE
    for c0 in range(0, o_ref.shape[1], sub):
        r = jnp.dot(h_ref[...], wb_ref[:, c0:c0 + sub], preferred_element_type=F32)
        rb = r.astype(o_ref.dtype)
        o_ref[:, c0:c0 + sub] = 0.5 * jnp.tanh(0.5 * rb) + 0.5


def _in_proj(h, w_t, layer, col0, n_cols, tn, out_dtype, gate):
    tm = 2048
    return pl.pallas_call(
        functools.partial(_proj_kernel, gate=gate),
        out_shape=jax.ShapeDtypeStruct((T_ALL, n_cols), out_dtype),
        grid=(n_cols // tn, T_ALL // tm),
        in_specs=[
            pl.BlockSpec((tm, D_MODEL), lambda j, i: (i, 0)),
            pl.BlockSpec((pl.Element(tn), pl.Element(D_MODEL)),
                         lambda j, i: (pl.multiple_of(layer * IN_COLS + col0 + j * tn, 8), 0)),
        ],
        out_specs=pl.BlockSpec((tm, tn), lambda j, i: (i, j)),
        scratch_shapes=[pltpu.VMEM((D_MODEL, tn), BF16)],
        compiler_params=_params("parallel", "arbitrary"),
        name="gate_proj" if gate else "in_proj",
    )(h, w_t)


def _hy_filter_kernel(z_ref, dl_ref, w1_ref, b1_ref, w2_ref, b2_ref, w3_ref, fwd_ref,
                      kre_ref, kim_ref, *, L):
    z = z_ref[...]
    a = jnp.sin(_dot_split(z, w1_ref[...]) + b1_ref[...])
    a = jnp.sin(_dot_split(a, w2_ref[...]) + b2_ref[...])
    h = _dot_split(a, w3_ref[...])
    decay = jnp.exp(-z[:, 0:1] * dl_ref[...])
    not_first = lax.broadcasted_iota(jnp.int32, (L, HY_W), 0) > 0
    sums, diffs = [], []
    for o in range(2):
        fw = h[:, (2 * o) * HY_W:(2 * o + 1) * HY_W] * decay
        bw = jnp.where(not_first, h[:, (2 * o + 1) * HY_W:(2 * o + 2) * HY_W] * decay, 0.0)
        sums.append(fw + bw)
        diffs.append(fw - bw)
    p = _dot(fwd_ref[...], jnp.concatenate(sums, 1))
    q = _dot(fwd_ref[L:2 * L, :], jnp.concatenate(diffs, 1))
    kre_ref[...] = p[0:L]
    first = lax.broadcasted_iota(jnp.int32, (L, 2 * HY_W), 0) == 0
    kim_ref[...] = jnp.where(first, p[L:L + 1], q)


def _hy_filters(L, w1p, b1, w2, b2, w3, fwd):
    zemb, deltas = _hyena_embedding(L)
    out = jax.ShapeDtypeStruct((L, 2 * HY_W), F32)
    return pl.pallas_call(
        functools.partial(_hy_filter_kernel, L=L),
        out_shape=(out, out),
        compiler_params=pltpu.CompilerParams(vmem_limit_bytes=VMEM_LIMIT),
        name=f"hy_filters_{L}",
    )(jnp.asarray(zemb), jnp.asarray(deltas), w1p, b1, w2, b2, w3, fwd)


def _group_step(ctx_body, lat_body):
    i = pl.program_id(0)
    pl.when(i < CTX_STEPS)(ctx_body)
    pl.when(i >= CTX_STEPS)(lat_body)


def _lat_index(i):
    return jnp.maximum(i - CTX_STEPS, 0)


def _hyena_kernel(hy_ref, cw_ref, cb_ref, bias_ref, kre_c, kim_c, fwd_c, inv_c,
                  kre_l, kim_l, fwd_l, inv_l, o_ref):
    _group_step(
        lambda: _hyena_body(hy_ref, cw_ref, cb_ref, bias_ref, kre_c, kim_c, fwd_c, inv_c, o_ref,
                            SEQ, CTX_SEQS_PER_STEP),
        lambda: _hyena_body(hy_ref, cw_ref, cb_ref, bias_ref, kre_l, kim_l, fwd_l, inv_l, o_ref,
                            DEC_SEQ, 1))


def _hyena_body(hy_ref, cw_ref, cb_ref, bias_ref, kre_ref, kim_ref, fwd_ref, inv_ref, o_ref, L, seqs):
    first = lax.broadcasted_iota(jnp.int32, (L, HY_W), 0) == 0

    def long_conv(u, o):
        uf = _dot(fwd_ref[...], u)
        ure, uim = uf[0:L], uf[L:2 * L]
        kre = kre_ref[:, o * HY_W:(o + 1) * HY_W]
        kim = kim_ref[:, o * HY_W:(o + 1) * HY_W]
        yre = jnp.where(first, ure * kre, ure * kre - uim * kim)
        yim = jnp.where(first, uim * kim, ure * kim + uim * kre)
        y = _dot(inv_ref[...], jnp.concatenate([yre, yim], 0))
        return y + u * bias_ref[o:o + 1, :]

    for g in range(seqs):
        sl = slice(g * L, (g + 1) * L)
        x = hy_ref[sl, :].astype(F32)
        rows = lax.broadcasted_iota(jnp.int32, x.shape, 0)
        prev = jnp.where(rows == 0, 0.0, pltpu.roll(x, 1, axis=0))
        nxt = jnp.where(rows == L - 1, 0.0, pltpu.roll(x, L - 1, axis=0))
        z = prev * cw_ref[0:1, :] + x * cw_ref[1:2, :] + nxt * cw_ref[2:3, :] + cb_ref[...]
        v, x1, x2 = z[:, 0:HY_W], z[:, HY_W:2 * HY_W], z[:, 2 * HY_W:3 * HY_W]
        u = x1 * long_conv(v, 0)
        o_ref[sl, :] = (x2 * long_conv(u, 1)).astype(o_ref.dtype)


def _const_spec(shape):
    return pl.BlockSpec(shape, lambda i: (0,) * len(shape))


def _hyena(z_main, layer, conv_w, conv_b, bias, filters, dft):
    tables, table_specs = [], []
    for L in (SEQ, DEC_SEQ):
        tables += [*filters[L], *dft[L]]
        table_specs += [_const_spec((L, 2 * HY_W)), _const_spec((L, 2 * HY_W)),
                        _const_spec((2 * L, L)), _const_spec((L, 2 * L))]
    return pl.pallas_call(
        _hyena_kernel,
        out_shape=jax.ShapeDtypeStruct((T_ALL, HY_W), BF16),
        grid=(MIXER_STEPS,),
        in_specs=[
            pl.BlockSpec((STEP_ROWS, 3 * HY_W), lambda i: (i, 0)),
            pl.BlockSpec((None, 3, 3 * HY_W), lambda i: (layer, 0, 0)),
            pl.BlockSpec((None, 1, 3 * HY_W), lambda i: (layer, 0, 0)),
            pl.BlockSpec((None, 2, HY_W), lambda i: (layer, 0, 0)),
        ] + table_specs,
        out_specs=pl.BlockSpec((STEP_ROWS, HY_W), lambda i: (i, 0)),
        compiler_params=_params("parallel"),
        name="hyena",
    )(z_main, conv_w, conv_b.reshape(DEPTH, 1, 3 * HY_W), bias, *tables)


def _win_masks():
    lane = lax.broadcasted_iota(jnp.int32, (1, LANE), 1)
    return lane < WIN_HD, lane >= WIN_HD


def _win_head_operands(q, k, v, h):
    lo_mask, hi_mask = _win_masks()
    col = h // 2
    lo = h % 2 == 0
    q128 = jnp.where(lo_mask if lo else hi_mask, q[:, col * LANE:(col + 1) * LANE], 0.0)
    swap = h in (1, 2)
    if swap:
        k = pltpu.roll(k, WIN_HD, axis=1)
        v = pltpu.roll(v, WIN_HD, axis=1)
    return q128, k, v, lo


def _win_kernel(sink_ref, q_ref, kv_ref, ck_ref, cv_ref, c_ref, su_ref, sd_ref, o_ref, *, layer):
    _group_step(
        lambda: _win_ctx_body(sink_ref, q_ref, kv_ref, o_ref, layer),
        lambda: _win_lat_body(sink_ref, q_ref, kv_ref, ck_ref, cv_ref, c_ref, su_ref, sd_ref, o_ref, layer))


def _win_ctx_body(sink_ref, q_ref, kv_ref, o_ref, layer):
    lo_mask, hi_mask = _win_masks()
    qscale = WIN_HD ** -0.5 * LOG2E
    for g in range(CTX_SEQS_PER_STEP):
        sl = slice(g * SEQ, (g + 1) * SEQ)
        q = q_ref[sl, :].astype(F32) * qscale
        k = kv_ref[sl, 0:LANE].astype(F32)
        v = kv_ref[sl, LANE:2 * LANE].astype(F32)
        cols = []
        for col in range(2):
            acc = None
            for h in (2 * col, 2 * col + 1):
                q128, kk, vv, lo = _win_head_operands(q, k, v, h)
                s = _dot_nt(q128, kk)
                sink = sink_ref[layer, h] * LOG2E
                m = jnp.maximum(jnp.max(s, -1, keepdims=True), sink)
                p = jnp.exp2(s - m)
                den = jnp.sum(p, -1, keepdims=True) + jnp.exp2(sink - m)
                o = _dot(p, vv) / den
                o = jnp.where(lo_mask if lo else hi_mask, o, 0.0)
                acc = o if acc is None else acc + o
            cols.append(acc)
        o_ref[sl, :] = jnp.concatenate(cols, 1).astype(o_ref.dtype)


def _win_lat_body(sink_ref, q_ref, kv_ref, ck_ref, cv_ref, c_ref, su_ref, sd_ref, o_ref, layer):
    L = DEC_SEQ
    half = WIN_HD // 2
    c, su, sd = c_ref[...], su_ref[...], sd_ref[...]
    qscale = WIN_HD ** -0.5 * LOG2E
    q = jnp.concatenate(
        [_rope128(q_ref[:, i * LANE:(i + 1) * LANE].astype(F32), c, su, sd, half) for i in range(2)],
        1) * qscale
    k = _rope128(kv_ref[:, 0:LANE].astype(F32), c, su, sd, half)
    v = kv_ref[:, LANE:2 * LANE].astype(F32)
    ck = ck_ref[...]
    cv = cv_ref[...]
    lo_mask, hi_mask = _win_masks()
    nb = L // CHUNK
    assert WINDOW == CHUNK
    rr = lax.broadcasted_iota(jnp.int32, (CHUNK, CHUNK), 0)
    cc = lax.broadcasted_iota(jnp.int32, (CHUNK, CHUNK), 1)
    band = {-1: jnp.where(cc >= rr, 0.0, NEG), 0: jnp.zeros((CHUNK, CHUNK), F32),
            1: jnp.where(cc <= rr, 0.0, NEG)}
    cols = []
    for col in range(2):
        acc_blocks = [None] * nb
        for h in (2 * col, 2 * col + 1):
            q128, kk, vv, lo = _win_head_operands(q, k, v, h)
            _, ckk, cvv, _ = _win_head_operands(q, ck, cv, h)
            sink = sink_ref[layer, h] * LOG2E
            for n in range(nb):
                blocks = [d for d in (-1, 0, 1) if 0 <= n + d < nb]
                k0 = (n + blocks[0]) * CHUNK
                k1 = (n + blocks[-1] + 1) * CHUNK
                qn = q128[n * CHUNK:(n + 1) * CHUNK]
                s_loc = _dot_nt(qn, kk[k0:k1]) + jnp.concatenate([band[d] for d in blocks], 1)
                s_ctx = _dot_nt(qn, ckk)
                m = jnp.maximum(jnp.maximum(jnp.max(s_loc, -1, keepdims=True),
                                            jnp.max(s_ctx, -1, keepdims=True)), sink)
                p_loc = jnp.exp2(s_loc - m)
                p_ctx = jnp.exp2(s_ctx - m)
                den = (jnp.sum(p_loc, -1, keepdims=True) + jnp.sum(p_ctx, -1, keepdims=True)
                       + jnp.exp2(sink - m))
                o = (_dot(p_loc, vv[k0:k1]) + _dot(p_ctx, cvv)) / den
                o = jnp.where(lo_mask if lo else hi_mask, o, 0.0)
                acc_blocks[n] = o if acc_blocks[n] is None else acc_blocks[n] + o
        cols.append(jnp.concatenate(acc_blocks, 0))
    o_ref[...] = jnp.concatenate(cols, 1).astype(o_ref.dtype)


def _win(z_main, sink, cache_k, cache_v, layer):
    c, su, sd, _ = _rope_tables(DEC_SEQ, WIN_HD, LANE)
    tab = _const_spec((DEC_SEQ, LANE))
    cache = pl.BlockSpec((None, None, PAST_LEN, LANE), lambda i: (_lat_index(i), layer, 0, 0))
    return pl.pallas_call(
        functools.partial(_win_kernel, layer=layer),
        out_shape=jax.ShapeDtypeStruct((T_ALL, WIN_HEADS * WIN_HD), BF16),
        grid=(MIXER_STEPS,),
        in_specs=[
            pl.BlockSpec(memory_space=pltpu.SMEM),
            pl.BlockSpec((STEP_ROWS, 256), lambda i: (i, COL_WQ // 256)),
            pl.BlockSpec((STEP_ROWS, 256), lambda i: (i, COL_WK // 256)),
            cache, cache, tab, tab, tab,
        ],
        out_specs=pl.BlockSpec((STEP_ROWS, 256), lambda i: (i, 0)),
        compiler_params=_params("parallel"),
        name="win",
    )(sink, z_main, z_main, cache_k, cache_v, jnp.asarray(c), jnp.asarray(su), jnp.asarray(sd))


def _ret_kernel(*refs, layer):
    _group_step(lambda: _ret_body(*refs, L=SEQ, layer=layer, ctx=True),
                lambda: _ret_body(*refs, L=DEC_SEQ, layer=layer, ctx=False))


def _ret_body(df_ref, db_ref, q_ref, k_ref, v0_ref, v1_ref, g0_ref, g1_ref, s0f_ref, s0b_ref,
              o_ref, sf_out, sb_out, s_ref, cross_ref, *, L, layer, ctx):
    seqs = STEP_ROWS // L
    if not ctx:
        sf_out[...] = jnp.zeros_like(sf_out)
        sb_out[...] = jnp.zeros_like(sb_out)
    C = CHUNK
    nc = L // C
    H = RET_HEADS
    qw = H * RET_DK
    vw = H * RET_DV

    def lane_table(width, per_head, fn):
        pos = lax.broadcasted_iota(jnp.int32, (C, per_head), 0).astype(F32)
        return jnp.concatenate([fn(h, pos) for h in range(H)], 1)

    def log_gamma(ref, h):
        d = jnp.full((1, 1), ref[layer, h], F32)
        return jnp.log(jax.nn.sigmoid(d))

    lgf = [log_gamma(df_ref, h) for h in range(H)]
    lgb = [log_gamma(db_ref, h) for h in range(H)]

    def tables(lg, reverse):
        if reverse:
            dq = lane_table(vw, RET_DV, lambda h, pos: jnp.exp((C - pos) * lg[h]))
            dk = lane_table(qw, RET_DK, lambda h, pos: jnp.exp(pos * lg[h]))
        else:
            dq = lane_table(vw, RET_DV, lambda h, pos: jnp.exp((pos + 1.0) * lg[h]))
            dk = lane_table(qw, RET_DK, lambda h, pos: jnp.exp((C - 1.0 - pos) * lg[h]))
        dc = jnp.concatenate([jnp.broadcast_to(jnp.exp(C * lg[h]), (1, RET_DV)) for h in range(H)], 1)
        return dq, dk, dc

    tab_f = tables(lgf, False)
    tab_b = tables(lgb, True)
    ii = lax.broadcasted_iota(jnp.int32, (C, C), 0)
    jj = lax.broadcasted_iota(jnp.int32, (C, C), 1)
    diff = (ii - jj).astype(F32)
    dmats = [jnp.where(diff >= 0, jnp.exp(jnp.maximum(diff, 0.0) * lgf[h]), 0.0)
             + jnp.where(diff <= 0, jnp.exp(jnp.maximum(-diff, 0.0) * lgb[h]), 0.0) for h in range(H)]
    dmat_stack = jnp.concatenate(dmats, 0)
    lane_q = lax.broadcasted_iota(jnp.int32, (1, qw), 1) // RET_DK

    srow = lax.broadcasted_iota(jnp.int32, (qw, vw), 0) // RET_DK
    scol = lax.broadcasted_iota(jnp.int32, (qw, vw), 1) // RET_DV
    diag = srow == scol

    for g in range(seqs):
        base = g * L
        rows_all = slice(base, base + L)
        q_all = q_ref[rows_all, :].astype(F32)
        k_all = k_ref[rows_all, :].astype(F32) * (RET_DK ** -0.5)
        v_all = jnp.concatenate([v0_ref[rows_all, :], v1_ref[rows_all, :]], 1).astype(F32)
        g_all = jnp.concatenate([g0_ref[rows_all, :], g1_ref[rows_all, :]], 1).astype(F32)

        def scan(tabs, reverse, s0_ref, s_out):
            dq, dk, dc = tabs
            if s0_ref is not None:
                s_ref[g] = jnp.zeros((qw, vw), F32)
                for h in range(H):
                    s_ref[g, h * RET_DK:(h + 1) * RET_DK, h * RET_DV:(h + 1) * RET_DV] = s0_ref[h]
            order = range(nc - 1, -1, -1) if reverse else range(nc)
            for step, ci in enumerate(order):
                sl = slice(ci * C, (ci + 1) * C)
                rs = slice(base + ci * C, base + (ci + 1) * C)
                qc, kc, vc = q_all[sl], k_all[sl], v_all[sl]
                upd = jnp.where(diag, _dot_tn(kc * dk, vc), 0.0)
                if s0_ref is None and step == 0:
                    if not reverse:
                        cross_ref[rs, :] = jnp.zeros((C, vw), F32)
                    s_ref[g] = upd
                    continue
                st = s_ref[g]
                cross = _dot(qc, st) * dq
                if reverse:
                    cross_ref[rs, :] = cross_ref[rs, :] + cross
                else:
                    cross_ref[rs, :] = cross
                s_ref[g] = st * dc + upd
            if s_out is not None:
                for h in range(H):
                    s_out[g, h] = s_ref[g, h * RET_DK:(h + 1) * RET_DK, h * RET_DV:(h + 1) * RET_DV]

        scan(tab_f, False, None if ctx else s0f_ref, sf_out if ctx else None)
        scan(tab_b, True, None if ctx else s0b_ref, sb_out if ctx else None)

        for ci in range(nc):
            sl = slice(ci * C, (ci + 1) * C)
            rs = slice(base + ci * C, base + (ci + 1) * C)
            q_stack = jnp.concatenate([jnp.where(lane_q == h, q_all[sl], 0.0) for h in range(H)], 0)
            att = _dot_nt(q_stack, k_all[sl]) * dmat_stack
            ov = _dot(att, v_all[sl])
            for h in range(H):
                hv = slice(h * RET_DV, (h + 1) * RET_DV)
                o = ov[h * C:(h + 1) * C, hv] + cross_ref[rs, hv]
                gt = g_all[sl, hv]
                o_ref[rs, hv] = ((gt * jax.nn.sigmoid(gt)) * _layer_norm(o)).astype(o_ref.dtype)


def _retention(z_main, dec_f, dec_b, s0f, s0b, layer):
    def zcol(col):
        return pl.BlockSpec((STEP_ROWS, 256), lambda i: (i, col // 256))

    smem = pl.BlockSpec(memory_space=pltpu.SMEM)
    z_specs = [zcol(COL_RQ), zcol(COL_RK), zcol(COL_RV), zcol(COL_RV + 256),
               zcol(COL_RG), zcol(COL_RG + 256)]
    s0_spec = pl.BlockSpec((None, None, RET_HEADS, RET_DK, RET_DV),
                           lambda i: (_lat_index(i), layer, 0, 0, 0))
    st_shape = jax.ShapeDtypeStruct((MIXER_STEPS * CTX_SEQS_PER_STEP, RET_HEADS, RET_DK, RET_DV), F32)
    st_spec = pl.BlockSpec((CTX_SEQS_PER_STEP, RET_HEADS, RET_DK, RET_DV), lambda i: (i, 0, 0, 0))
    return pl.pallas_call(
        functools.partial(_ret_kernel, layer=layer),
        out_shape=(jax.ShapeDtypeStruct((T_ALL, RET_HEADS * RET_DV), BF16), st_shape, st_shape),
        grid=(MIXER_STEPS,),
        in_specs=[smem, smem] + z_specs + [s0_spec, s0_spec],
        out_specs=(pl.BlockSpec((STEP_ROWS, RET_HEADS * RET_DV), lambda i: (i, 0)), st_spec, st_spec),
        scratch_shapes=[pltpu.VMEM((CTX_SEQS_PER_STEP, RET_HEADS * RET_DK, RET_HEADS * RET_DV), F32),
                        pltpu.VMEM((STEP_ROWS, RET_HEADS * RET_DV), F32)],
        compiler_params=_params("parallel"),
        name="retention",
    )(dec_f, dec_b, *([z_main] * 6), s0f, s0b)


def _rms_norm(x, g):
    return x * lax.rsqrt(jnp.mean(x * x, -1, keepdims=True) + RMS_EPS) * g


def _mla_keys(kn, kr):
    lane_r = lax.broadcasted_iota(jnp.int32, (1, LANE), 1)
    return jnp.concatenate([kn, jnp.where(lane_r < MLA_ROPE, kr, 0.0)], 1).astype(BF16)


def _mla_attend(qn, qr, k_cat, vv, o_ref, row0):
    qscale = (MLA_NOPE + MLA_ROPE) ** -0.5 * LOG2E
    lane_n = lax.broadcasted_iota(jnp.int32, (1, MLA_HEADS * MLA_NOPE), 1) // MLA_NOPE
    lane_r = lax.broadcasted_iota(jnp.int32, (1, LANE), 1)
    qn = qn * qscale
    qr = qr * qscale
    lq = qn.shape[0]
    heads = []
    for h in range(MLA_HEADS):
        qnh = jnp.where(lane_n == h, qn, 0.0)
        qrh = qr if h == 0 else pltpu.roll(qr, LANE - h * MLA_ROPE, axis=1)
        qrh = jnp.where(lane_r < MLA_ROPE, qrh, 0.0)
        heads.append(jnp.concatenate([qnh, qrh], 1).astype(BF16))
    s = _dot_nt(jnp.concatenate(heads, 0), k_cat)
    m = jnp.max(s, -1, keepdims=True)
    p = jnp.exp2(s - m)
    den = jnp.sum(p, -1, keepdims=True)
    o = _dot(p, vv) / den
    acc = None
    for h in range(MLA_HEADS):
        oh = jnp.where(lane_n == h, o[h * lq:(h + 1) * lq], 0.0)
        acc = oh if acc is None else acc + oh
    o_ref[row0:row0 + lq, :] = acc.astype(o_ref.dtype)


def _mla_kernel(cq_ref, ckv_ref, kr_ref, cckv_ref, ckr_ref, c_ref, su_ref, sd_ref,
                qg_ref, kg_ref, wqn_ref, wqr_ref, wk_ref, wv_ref, o_ref, ckvn_ref):
    weights = (qg_ref, kg_ref, wqn_ref, wqr_ref, wk_ref, wv_ref)
    _group_step(
        lambda: _mla_ctx_body(cq_ref, ckv_ref, kr_ref, *weights, o_ref, ckvn_ref),
        lambda: _mla_lat_body(cq_ref, ckv_ref, kr_ref, cckv_ref, ckr_ref, c_ref, su_ref, sd_ref,
                              *weights, o_ref, ckvn_ref))


def _mla_ctx_body(cq_ref, ckv_ref, kr_ref, qg_ref, kg_ref, wqn_ref, wqr_ref, wk_ref, wv_ref,
                  o_ref, ckvn_ref):
    cqn = _rms_norm(cq_ref[...].astype(F32), qg_ref[...])
    qn = _dot(cqn, wqn_ref[...])
    qr = _dot(cqn, wqr_ref[...])
    ckvn = _rms_norm(ckv_ref[...].astype(F32), kg_ref[...])
    ckvn_ref[...] = ckvn
    k_cat = _mla_keys(_dot(ckvn, wk_ref[...]), kr_ref[...].astype(F32))
    vv = _dot(ckvn, wv_ref[...]).astype(BF16)
    for g in range(CTX_SEQS_PER_STEP):
        sl = slice(g * SEQ, (g + 1) * SEQ)
        _mla_attend(qn[sl], qr[sl], k_cat[sl], vv[sl], o_ref, g * SEQ)


def _mla_lat_body(cq_ref, ckv_ref, kr_ref, cckv_ref, ckr_ref, c_ref, su_ref, sd_ref,
                  qg_ref, kg_ref, wqn_ref, wqr_ref, wk_ref, wv_ref, o_ref, ckvn_ref):
    half = MLA_ROPE // 2
    c, su, sd = c_ref[...], su_ref[...], sd_ref[...]
    cqn = _rms_norm(cq_ref[...].astype(F32), qg_ref[...])
    qn = _dot(cqn, wqn_ref[...])
    qr = _rope128(_dot(cqn, wqr_ref[...]), c, su, sd, half)
    ckvn = _rms_norm(ckv_ref[...].astype(F32), kg_ref[...])
    ckvn_ref[...] = ckvn
    ckv_all = jnp.concatenate([ckvn, cckv_ref[...]], 0)
    vv = _dot(ckv_all, wv_ref[...]).astype(BF16)
    kr = jnp.concatenate([_rope128(kr_ref[...].astype(F32), c, su, sd, half), ckr_ref[...]], 0)
    k_cat = _mla_keys(_dot(ckv_all, wk_ref[...]), kr)
    rows_per_call = 256
    for n in range(DEC_SEQ // rows_per_call):
        rows = slice(n * rows_per_call, (n + 1) * rows_per_call)
        _mla_attend(qn[rows], qr[rows], k_cat, vv, o_ref, n * rows_per_call)


def _mla(z_main, cache_ckv, cache_kr_pad, weights, layer):
    c, su, sd, _ = _rope_tables(DEC_SEQ, MLA_ROPE, LANE)
    tab = _const_spec((DEC_SEQ, LANE))
    cache = pl.BlockSpec((None, None, PAST_LEN, LANE), lambda i: (_lat_index(i), layer, 0, 0))

    def weight(*shape):
        return pl.BlockSpec((None,) + shape, lambda i: (layer, 0, 0))

    return pl.pallas_call(
        _mla_kernel,
        out_shape=(jax.ShapeDtypeStruct((T_ALL, MLA_HEADS * MLA_V), BF16),
                   jax.ShapeDtypeStruct((T_ALL, MLA_KV_LORA), F32)),
        grid=(MIXER_STEPS,),
        in_specs=[
            pl.BlockSpec((STEP_ROWS, 256), lambda i: (i, COL_CQ // 256)),
            pl.BlockSpec((STEP_ROWS, LANE), lambda i: (i, COL_CKV // LANE)),
            pl.BlockSpec((STEP_ROWS, LANE), lambda i: (i, COL_KROPE // LANE)),
            cache, cache, tab, tab, tab,
            weight(1, MLA_Q_LORA), weight(1, MLA_KV_LORA),
            weight(MLA_Q_LORA, MLA_HEADS * MLA_NOPE), weight(MLA_Q_LORA, MLA_HEADS * MLA_ROPE),
            weight(MLA_KV_LORA, MLA_HEADS * MLA_NOPE), weight(MLA_KV_LORA, MLA_HEADS * MLA_V),
        ],
        out_specs=(pl.BlockSpec((STEP_ROWS, 256), lambda i: (i, 0)),
                   pl.BlockSpec((STEP_ROWS, MLA_KV_LORA), lambda i: (i, 0))),
        compiler_params=_params("parallel"),
        name="mla",
    )(z_main, z_main, z_main, cache_ckv, cache_kr_pad,
      jnp.asarray(c), jnp.asarray(su), jnp.asarray(sd), *weights)


def _route(logits_t, rb):
    scores = jax.nn.sigmoid(logits_t)
    biased = scores + rb
    sc = [scores[e:e + 1, :] for e in range(N_EXPERTS)]
    bi = [biased[e:e + 1, :] for e in range(N_EXPERTS)]
    epg = EXPERTS_PER_GROUP
    gsum = []
    for g in range(N_GROUPS):
        v = bi[g * epg:(g + 1) * epg]
        best = None
        for a in range(epg):
            for b in range(a + 1, epg):
                pair = v[a] + v[b]
                best = pair if best is None else jnp.maximum(best, pair)
        gsum.append(best)
    combine = []
    sel = []
    for g in range(N_GROUPS):
        is_best = None
        for g2 in range(N_GROUPS):
            if g2 == g:
                continue
            c = gsum[g] > gsum[g2] if g2 < g else gsum[g] >= gsum[g2]
            is_best = c if is_best is None else jnp.logical_and(is_best, c)
        for a in range(epg):
            e = g * epg + a
            rank = jnp.zeros_like(bi[e])
            for b in range(epg):
                if b == a:
                    continue
                e2 = g * epg + b
                ahead = bi[e2] >= bi[e] if b < a else bi[e2] > bi[e]
                rank = rank + jnp.where(ahead, 1.0, 0.0)
            sel.append(jnp.logical_and(is_best, rank < 2.0))
    wsum = None
    for e in range(N_EXPERTS):
        w = jnp.where(sel[e], sc[e], 0.0)
        wsum = w if wsum is None else wsum + w
    for e in range(N_EXPERTS):
        combine.append(jnp.where(sel[e], ROUTE_SCALE * sc[e] / wsum, 0.0))
    return jnp.concatenate(combine, 0)


MERGE_BRANCH_ROWS = (HY_W, WIN_HEADS * WIN_HD, RET_HEADS * RET_DV, MLA_HEADS * MLA_V)
MERGE_ROWS = sum(MERGE_BRANCH_ROWS) + D_MODEL


def _merge_kernel(ya_ref, yb_ref, yc_ref, yd_ref, gt_ref, xc_ref, xl_ref, m_ref,
                  w_ref, g_ref, b_ref, rw_ref, rb_ref,
                  x1_ref, h2_ref, cmb_ref, *, ctx_tiles, sub_rows):
    D = D_MODEL
    rw = rw_ref[...]
    rw_hi = rw.astype(BF16)
    rw_lo = (rw - rw_hi.astype(F32)).astype(BF16)
    g1 = m_ref[:, 2 * D:3 * D]
    s2 = m_ref[:, 3 * D:4 * D]
    sc2 = m_ref[:, 4 * D:5 * D]
    is_ctx = pl.program_id(0) < ctx_tiles
    offs = np.cumsum((0,) + MERGE_BRANCH_ROWS)
    branches = tuple((y_ref, slice(int(offs[i]), int(offs[i + 1])))
                     for i, y_ref in enumerate((ya_ref, yb_ref, yc_ref, yd_ref)))
    w_out_rows = slice(int(offs[-1]), MERGE_ROWS)
    for r0 in range(0, x1_ref.shape[0], sub_rows):
        rows = slice(r0, r0 + sub_rows)
        merged = None
        for i, (y_ref, w_rows) in enumerate(branches):
            t = gt_ref[rows, i * D:(i + 1) * D] * jnp.dot(
                y_ref[rows, :], w_ref[w_rows, :], preferred_element_type=F32).astype(BF16)
            merged = t if merged is None else merged + t
        out1 = jnp.dot(merged, w_ref[w_out_rows, :], preferred_element_type=F32)
        x = jnp.where(is_ctx, xc_ref[rows, :], xl_ref[rows, :])
        x1 = _layer_norm(ALPHA * x + g1 * out1) * g_ref[...] + b_ref[...]
        x1_ref[rows, :] = x1
        h2 = _layer_norm(x1) * (1.0 + sc2) + s2
        h2_hi = h2.astype(BF16)
        h2_ref[rows, :] = h2_hi
        h2_lo = (h2 - h2_hi.astype(F32)).astype(BF16)
        logits = (jnp.dot(h2_hi, rw_hi, preferred_element_type=F32)
                  + (jnp.dot(h2_lo, rw_hi, preferred_element_type=F32)
                     + jnp.dot(h2_hi, rw_lo, preferred_element_type=F32)))
        cmb_ref[:, rows] = _route(logits.T[0:N_EXPERTS], rb_ref[...])


def _merge(ya, yb, yc, yd, gates, x_ctx, x_lat, mods, w_merge, ln1_g, ln1_b,
           router_w, router_b, layer):
    tm = 512
    row = _mod_row(tm)
    D = D_MODEL
    ctx_tiles = T_CTX // tm

    def tile(w):
        return pl.BlockSpec((tm, w), lambda i: (i, 0))

    def weight(k, n):
        return pl.BlockSpec((None, k, n), lambda i: (layer, 0, 0))

    return pl.pallas_call(
        functools.partial(_merge_kernel, ctx_tiles=ctx_tiles, sub_rows=256),
        out_shape=(jax.ShapeDtypeStruct((T_ALL, D), F32),
                   jax.ShapeDtypeStruct((T_ALL, D), BF16),
                   jax.ShapeDtypeStruct((N_EXPERTS, T_ALL), F32)),
        grid=(T_ALL // tm,),
        in_specs=[
            tile(256), tile(256), tile(512), tile(256), tile(4 * D),
            pl.BlockSpec((tm, D), lambda i: (jnp.minimum(i, ctx_tiles - 1), 0)),
            pl.BlockSpec((tm, D), lambda i: (jnp.maximum(i - ctx_tiles, 0), 0)),
            _mod_spec(layer, row),
            weight(MERGE_ROWS, D), weight(1, D), weight(1, D),
            pl.BlockSpec((D, LANE), lambda i: (0, 0)),
            pl.BlockSpec((N_EXPERTS, 1), lambda i: (0, 0)),
        ],
        out_specs=(tile(D), tile(D), pl.BlockSpec((N_EXPERTS, tm), lambda i: (0, i))),
        compiler_params=_params("parallel"),
        name="merge",
    )(ya, yb, yc, yd, gates, x_ctx, x_lat, mods, w_merge,
      ln1_g.reshape(DEPTH, 1, D), ln1_b.reshape(DEPTH, 1, D), router_w,
      router_b.reshape(N_EXPERTS, 1))


MOE_EXPERTS_PER_STEP = 2


def _moe_kernel(*refs, next_h):
    if next_h:
        (h_ref, c_ref, x1_ref, m_ref, wg_ref, wu_ref, wd_ref, g_ref, b_ref, mn_ref,
         o_ref, hn_ref, acc_ref) = refs[-13:]
    else:
        h_ref, c_ref, x1_ref, m_ref, wg_ref, wu_ref, wd_ref, g_ref, b_ref, o_ref, acc_ref = refs
    eg = pl.program_id(1)

    @pl.when(eg == 0)
    def _():
        acc_ref[...] = jnp.zeros_like(acc_ref)

    h = h_ref[...]
    cmb = c_ref[...]
    lane = lax.broadcasted_iota(jnp.int32, cmb.shape, 1)
    hid = []
    for k in range(MOE_EXPERTS_PER_STEP):
        gate = jnp.dot(h, wg_ref[k].astype(BF16), preferred_element_type=F32)
        up = jnp.dot(h, wu_ref[k].astype(BF16), preferred_element_type=F32)
        e = eg * MOE_EXPERTS_PER_STEP + k
        ce = jnp.sum(jnp.where(lane == e, cmb, 0.0), -1, keepdims=True)
        sig = 0.5 * jnp.tanh(0.5 * gate) + 0.5
        hid.append((gate * sig * (up * ce)).astype(BF16))
    wd = wd_ref[...].reshape(MOE_EXPERTS_PER_STEP * D_EXPERT, D_MODEL).astype(BF16)
    acc_ref[...] += jnp.dot(jnp.concatenate(hid, 1), wd, preferred_element_type=F32)

    @pl.when(eg == N_EXPERTS // MOE_EXPERTS_PER_STEP - 1)
    def _():
        g2 = m_ref[:, 5 * D_MODEL:6 * D_MODEL]
        y = _layer_norm(ALPHA * x1_ref[...] + g2 * acc_ref[...])
        y = y * g_ref[...] + b_ref[...]
        o_ref[...] = y
        if next_h:
            s1 = mn_ref[:, 0:D_MODEL]
            sc1 = mn_ref[:, D_MODEL:2 * D_MODEL]
            hn_ref[...] = (_layer_norm(y) * (1.0 + sc1) + s1).astype(hn_ref.dtype)


def _moe(h2, combine, x1, mods, w_gate, w_up, w_down, ln2_g, ln2_b, layer, row0, n_rows,
         next_h=False, prev_h=None):
    tm = 1024
    row = _mod_row(tm)
    D = D_MODEL
    t0 = row0 // tm
    eps = MOE_EXPERTS_PER_STEP
    mod_spec = _mod_spec(layer, lambda i, e: row(t0 + i))
    in_specs = [
        pl.BlockSpec((tm, D), lambda i, e: (t0 + i, 0)),
        pl.BlockSpec((tm, N_EXPERTS), lambda i, e: (t0 + i, 0)),
        pl.BlockSpec((tm, D), lambda i, e: (t0 + i, 0)),
        mod_spec,
        pl.BlockSpec((None, eps, D, D_EXPERT), lambda i, e: (layer, e, 0, 0)),
        pl.BlockSpec((None, eps, D, D_EXPERT), lambda i, e: (layer, e, 0, 0)),
        pl.BlockSpec((None, eps, D_EXPERT, D), lambda i, e: (layer, e, 0, 0)),
        pl.BlockSpec((None, 1, D), lambda i, e: (layer, 0, 0)),
        pl.BlockSpec((None, 1, D), lambda i, e: (layer, 0, 0)),
    ]
    args = [h2, combine, x1, mods, w_gate, w_up, w_down,
            ln2_g.reshape(DEPTH, 1, D), ln2_b.reshape(DEPTH, 1, D)]
    out_shape = jax.ShapeDtypeStruct((n_rows, D), F32)
    out_specs = pl.BlockSpec((tm, D), lambda i, e: (i, 0))
    aliases = {}
    if next_h:
        in_specs.append(_mod_spec(layer + 1, lambda i, e: row(t0 + i)))
        args.append(mods)
        out_shape = (out_shape, jax.ShapeDtypeStruct((T_ALL, D), BF16))
        out_specs = (out_specs, pl.BlockSpec((tm, D), lambda i, e: (t0 + i, 0)))
        if prev_h is not None:
            in_specs = [pl.BlockSpec(memory_space=pl.ANY)] + in_specs
            args = [prev_h] + args
            aliases = {0: 1}
    return pl.pallas_call(
        functools.partial(_moe_kernel, next_h=next_h),
        out_shape=out_shape,
        grid=(n_rows // tm, N_EXPERTS // eps),
        in_specs=in_specs,
        out_specs=out_specs,
        input_output_aliases=aliases,
        scratch_shapes=[pltpu.VMEM((tm, D), F32)],
        compiler_params=_params("parallel", "arbitrary"),
        name="moe",
    )(*args)


def kernel(x_prompt, x_sample, cache_win_k, cache_win_v, cache_mla_ckv, cache_mla_krope,
           state_ret_fwd, state_ret_bwd, c, c_ctx, w_ada, b_ada, w_in,
           hy_conv_w, hy_conv_b, hy_w1, hy_b1, hy_w2, hy_b2, hy_w3, hy_bias,
           win_sink, ret_decay_fwd, ret_decay_bwd, mla_q_norm, mla_kv_norm, mla_w_uq, mla_w_ukv,
           w_br_a, w_br_b, w_br_c, w_br_d, w_out, ln1_g, ln1_b, ln2_g, ln2_b,
           router_w, router_b, moe_w_gate, moe_w_up, moe_w_down):
    D = D_MODEL
    x_ctx = x_prompt.reshape(T_CTX, D)
    x_lat = x_sample.reshape(T_LAT, D)

    mods = _ada_mods(c_ctx, c, w_ada, b_ada)

    w_in_t = jnp.swapaxes(w_in, 1, 2).reshape(DEPTH * IN_COLS, D)
    cache_k = cache_win_k.reshape(DEC_BATCH, DEPTH, PAST_LEN, WIN_KV_HEADS * WIN_HD)
    cache_v = cache_win_v.reshape(DEC_BATCH, DEPTH, PAST_LEN, WIN_KV_HEADS * WIN_HD)
    cache_kr = jnp.pad(cache_mla_krope, ((0, 0), (0, 0), (0, 0), (0, LANE - MLA_ROPE)))

    uq = mla_w_uq.reshape(DEPTH, MLA_Q_LORA, MLA_HEADS, MLA_NOPE + MLA_ROPE)
    ukv = mla_w_ukv.reshape(DEPTH, MLA_KV_LORA, MLA_HEADS, MLA_NOPE + MLA_V)
    mla_weights = (
        mla_q_norm.reshape(DEPTH, 1, MLA_Q_LORA),
        mla_kv_norm.reshape(DEPTH, 1, MLA_KV_LORA),
        uq[..., :MLA_NOPE].reshape(DEPTH, MLA_Q_LORA, MLA_HEADS * MLA_NOPE),
        uq[..., MLA_NOPE:].reshape(DEPTH, MLA_Q_LORA, MLA_HEADS * MLA_ROPE),
        ukv[..., :MLA_NOPE].reshape(DEPTH, MLA_KV_LORA, MLA_HEADS * MLA_NOPE),
        ukv[..., MLA_NOPE:].reshape(DEPTH, MLA_KV_LORA, MLA_HEADS * MLA_V),
    )

    hy_w1p = jnp.pad(hy_w1, ((0, 0), (0, LANE - HY_EMB), (0, 0)))
    dft = {}
    for L in (SEQ, DEC_SEQ):
        fwd, inv = _dft_tables(L)
        dft[L] = (jnp.asarray(fwd).astype(BF16), jnp.asarray(inv).astype(BF16))
    router_w_pad = jnp.pad(router_w, ((0, 0), (0, LANE - N_EXPERTS)))
    w_merge = jnp.concatenate([w_br_a, w_br_b, w_br_c, w_br_d, w_out], 1).astype(BF16)

    new_k, new_v, new_ckv, new_kr, new_sf, new_sb = [], [], [], [], [], []
    for l in range(DEPTH):
        if l == 0:
            h = _ln_mod(x_ctx, None, mods, l, 0)
            h = _ln_mod(x_lat, h, mods, l, T_CTX)
        z = _in_proj(h, w_in_t, l, 0, Z_MAIN, Z_MAIN // 2, BF16, gate=False)
        gates = _in_proj(h, w_in_t, l, COL_GATE, 4 * D, D, BF16, gate=True)

        filters = {L: _hy_filters(L, hy_w1p[l], hy_b1[l][None], hy_w2[l], hy_b2[l][None], hy_w3[l],
                                  dft[L][0]) for L in (SEQ, DEC_SEQ)}
        ya = _hyena(z, l, hy_conv_w, hy_conv_b, hy_bias, filters, dft)
        yb = _win(z, win_sink, cache_k, cache_v, l)
        yc, sf, sb = _retention(z, ret_decay_fwd, ret_decay_bwd, state_ret_fwd, state_ret_bwd, l)
        yd, ckvn = _mla(z, cache_mla_ckv, cache_kr, mla_weights, l)

        x1, h2, combine_t = _merge(ya, yb, yc, yd, gates, x_ctx, x_lat, mods, w_merge,
                                   ln1_g, ln1_b, router_w_pad, router_b, l)
        moe_args = (h2, combine_t.T, x1, mods, moe_w_gate, moe_w_up, moe_w_down, ln2_g, ln2_b, l)
        if l + 1 < DEPTH:
            x_ctx, h = _moe(*moe_args, 0, T_CTX, next_h=True)
            x_lat, h = _moe(*moe_args, T_CTX, T_LAT, next_h=True, prev_h=h)
        else:
            x_ctx = _moe(*moe_args, 0, T_CTX)
            x_lat = _moe(*moe_args, T_CTX, T_LAT)

        def ctx_cols(col, width):
            return z[:T_CTX, col:col + width].astype(F32)

        new_k.append(ctx_cols(COL_WK, 128).reshape(BATCH, SEQ, WIN_KV_HEADS, WIN_HD))
        new_v.append(ctx_cols(COL_WV, 128).reshape(BATCH, SEQ, WIN_KV_HEADS, WIN_HD))
        new_ckv.append(ckvn[:T_CTX].reshape(BATCH, SEQ, MLA_KV_LORA))
        new_kr.append(ctx_cols(COL_KROPE, MLA_ROPE).reshape(BATCH, SEQ, MLA_ROPE))
        new_sf.append(sf[:BATCH])
        new_sb.append(sb[:BATCH])

    y_prompt = x_ctx.reshape(BATCH, SEQ, D)
    y_sample = x_lat.reshape(DEC_BATCH, DEC_SEQ, D)
    return (y_prompt, y_sample, jnp.stack(new_k, 1), jnp.stack(new_v, 1), jnp.stack(new_ckv, 1),
            jnp.stack(new_kr, 1), jnp.stack(new_sf, 1), jnp.stack(new_sb, 1))
```

```python
import functools
import math

import numpy as np
import jax
import jax.numpy as jnp
from jax import lax
from jax.experimental import pallas as pl
from jax.experimental.pallas import tpu as pltpu

F32 = jnp.float32
BF16 = jnp.bfloat16

D_MODEL = 1024
BATCH = 16
SEQ = 256
DEPTH = 2
DEC_BATCH = 2
DEC_SEQ = 1024
PAST_LEN = 256
GRID_W = 64
CHUNK = 128
ROPE_BASE = 10000.0
NEG = -1e30
LN_EPS = 1e-5
RMS_EPS = 1e-6
LOG2E = math.log2(math.e)

HY_W = 256
HY_BANDS = 16
HY_EMB = 1 + 2 * HY_BANDS
HY_FFN = 64
HY_FAST_DECAY = 0.3
HY_SLOW_DECAY = 1.5
HY_TARGET = 1e-2

WIN_HEADS = 4
WIN_KV_HEADS = 2
WIN_HD = 64
WINDOW = 128

RET_HEADS = 4
RET_DK = 64
RET_DV = 128

MLA_HEADS = 4
MLA_Q_LORA = 256
MLA_KV_LORA = 128
MLA_NOPE = 64
MLA_ROPE = 32
MLA_V = 64

N_EXPERTS = 16
N_GROUPS = 4
EXPERTS_PER_GROUP = N_EXPERTS // N_GROUPS
D_EXPERT = 256
ROUTE_SCALE = 2.5

ALPHA = (2.0 * DEPTH) ** 0.25

T_CTX = BATCH * SEQ
T_LAT = DEC_BATCH * DEC_SEQ
T_ALL = T_CTX + T_LAT

COL_HY = 0
COL_WQ = 768
COL_WK = 1024
COL_WV = 1152
COL_RQ = 1280
COL_RK = 1536
COL_RV = 1792
COL_RG = 2304
COL_CQ = 2816
COL_CKV = 3072
COL_KROPE = 3200
COL_GATE = 3232
IN_COLS = COL_GATE + 4 * D_MODEL
Z_MAIN = 3328

LANE = 128
STEP_ROWS = 1024
CTX_SEQS_PER_STEP = STEP_ROWS // SEQ
CTX_STEPS = T_CTX // STEP_ROWS
MIXER_STEPS = T_ALL // STEP_ROWS
VMEM_LIMIT = 56 * 1024 * 1024


def _params(*sem):
    return pltpu.CompilerParams(dimension_semantics=sem, vmem_limit_bytes=VMEM_LIMIT)


def _dot(a, b):
    return jnp.dot(a.astype(BF16), b.astype(BF16), preferred_element_type=F32)


def _dot_split(a, b):
    a_hi = a.astype(BF16)
    a_lo = (a - a_hi.astype(F32)).astype(BF16)
    b_hi = b.astype(BF16)
    b_lo = (b - b_hi.astype(F32)).astype(BF16)

    def mm(x, y):
        return jnp.dot(x, y, preferred_element_type=F32)

    return mm(a_hi, b_hi) + (mm(a_lo, b_hi) + mm(a_hi, b_lo))


def _dot_nt(a, b):
    return lax.dot_general(a.astype(BF16), b.astype(BF16), (((1,), (1,)), ((), ())),
                           preferred_element_type=F32)


def _dot_tn(a, b):
    return lax.dot_general(a.astype(BF16), b.astype(BF16), (((0,), (0,)), ((), ())),
                           preferred_element_type=F32)


def _layer_norm(x):
    mu = jnp.mean(x, -1, keepdims=True)
    xc = x - mu
    var = jnp.mean(xc * xc, -1, keepdims=True)
    return xc * lax.rsqrt(var + LN_EPS)


def _mod_row(tile_rows):
    def row(i):
        start = i * tile_rows
        return jnp.where(start < T_CTX, 0, 1 + (start - T_CTX) // DEC_SEQ)
    return row


@functools.lru_cache(maxsize=None)
def _dft_tables(L):
    f = np.arange(L, dtype=np.int64)[:, None]
    s = np.arange(L, dtype=np.int64)[None, :]
    ang = np.pi * ((f * s) % (2 * L)).astype(np.float64) / L
    cos = np.cos(ang)
    sin = np.sin(ang)
    alt = np.where(np.arange(L) % 2 == 0, 1.0, -1.0)
    fwd_im = -sin
    fwd_im[0, :] = alt
    fwd = np.concatenate([cos, fwd_im], 0)
    inv_re = cos.T / L
    inv_re[:, 0] = 1.0 / (2 * L)
    inv_im = -sin.T / L
    inv_im[:, 0] = alt / (2 * L)
    inv = np.concatenate([inv_re, inv_im], 1)
    return fwd.astype(np.float32), inv.astype(np.float32)


@functools.lru_cache(maxsize=None)
def _hyena_embedding(L):
    t01 = np.linspace(0.0, 1.0, L, dtype=np.float64)[:, None]
    bands = np.linspace(1e-4, HY_BANDS - 1, HY_BANDS, dtype=np.float64)
    ang = (2.0 * math.pi / L) * np.arange(L, dtype=np.float64)[:, None] * bands[None, :]
    z = np.concatenate([t01, np.cos(ang), -np.sin(ang)], -1)
    zp = np.zeros((L, LANE), np.float64)
    zp[:, :HY_EMB] = z
    deltas = np.abs(np.linspace(math.log(HY_TARGET) / HY_SLOW_DECAY,
                                math.log(HY_TARGET) / HY_FAST_DECAY, HY_W, dtype=np.float64))
    return zp.astype(np.float32), deltas[None, :].astype(np.float32)


@functools.lru_cache(maxsize=None)
def _rope_tables(L, rot_dim, width):
    rows = L // GRID_W
    n_freq = rot_dim // 4
    half = rot_dim // 2
    inv = ROPE_BASE ** (-np.arange(n_freq, dtype=np.float64) / n_freq)
    pos = np.arange(L)
    row = (pos // GRID_W).astype(np.float64)
    col = (pos % GRID_W).astype(np.float64)
    ang = np.concatenate([row[:, None] * inv, col[:, None] * inv], -1)
    cos, sin = np.cos(ang), np.sin(ang)
    zero = np.zeros_like(sin)
    c = np.tile(np.concatenate([cos, cos], -1), (1, width // rot_dim))
    s_up = np.tile(np.concatenate([-sin, zero], -1), (1, width // rot_dim))
    s_dn = np.tile(np.concatenate([zero, sin], -1), (1, width // rot_dim))
    return c.astype(np.float32), s_up.astype(np.float32), s_dn.astype(np.float32), half


def _rope128(x, c, s_up, s_dn, half):
    up = pltpu.roll(x, LANE - half, axis=1)
    dn = pltpu.roll(x, half, axis=1)
    return x * c + up * s_up + dn * s_dn


MOD_ROWS = 1 + DEC_BATCH


def _ada_kernel(ct_ref, w_ref, b_ref, o_ref):
    @pl.when(pl.program_id(1) == 0)
    def _():
        for r in range(MOD_ROWS):
            o_ref[r] = b_ref[...]

    ct = ct_ref[...]
    s = ct * jax.nn.sigmoid(ct)
    w = w_ref[...]
    for r in range(MOD_ROWS):
        o_ref[r] += jnp.sum(w * s[:, r:r + 1], axis=0, keepdims=True)


def _ada_mods(c_ctx, c, w_ada, b_ada):
    tk = 256
    n = 6 * D_MODEL
    c_cols = jnp.concatenate([c_ctx[:, None], c.T], 1)
    return pl.pallas_call(
        _ada_kernel,
        out_shape=jax.ShapeDtypeStruct((DEPTH, MOD_ROWS, 1, n), F32),
        grid=(DEPTH, D_MODEL // tk),
        in_specs=[
            pl.BlockSpec((tk, MOD_ROWS), lambda l, k: (k, 0)),
            pl.BlockSpec((None, tk, n), lambda l, k: (l, k, 0)),
            pl.BlockSpec((None, 1, n), lambda l, k: (l, 0, 0)),
        ],
        out_specs=pl.BlockSpec((None, MOD_ROWS, 1, n), lambda l, k: (l, 0, 0, 0)),
        compiler_params=_params("parallel", "arbitrary"),
        name="ada_mods",
    )(c_cols, w_ada, b_ada.reshape(DEPTH, 1, n))


def _mod_spec(layer, row_of_step):
    return pl.BlockSpec((None, None, 1, 6 * D_MODEL), lambda *g: (layer, row_of_step(*g), 0, 0))


def _lnmod_kernel(*refs):
    x_ref, m_ref, h_ref = refs[-3:]
    y = _layer_norm(x_ref[...])
    s1 = m_ref[:, 0:D_MODEL]
    sc1 = m_ref[:, D_MODEL:2 * D_MODEL]
    h_ref[...] = (y * (1.0 + sc1) + s1).astype(h_ref.dtype)


def _ln_mod(x_group, prev_out, mods, layer, row0):
    tm = 512
    row = _mod_row(tm)
    tile0 = row0 // tm
    in_specs = [
        pl.BlockSpec((tm, D_MODEL), lambda i: (i, 0)),
        _mod_spec(layer, lambda i: row(tile0 + i)),
    ]
    args = [x_group, mods]
    if prev_out is not None:
        in_specs = [pl.BlockSpec(memory_space=pl.ANY)] + in_specs
        args = [prev_out] + args
    return pl.pallas_call(
        _lnmod_kernel,
        out_shape=jax.ShapeDtypeStruct((T_ALL, D_MODEL), BF16),
        grid=(x_group.shape[0] // tm,),
        in_specs=in_specs,
        out_specs=pl.BlockSpec((tm, D_MODEL), lambda i: (tile0 + i, 0)),
        input_output_aliases={} if prev_out is None else {0: 0},
        compiler_params=_params("parallel"),
        name="ln_mod",
    )(*args)


def _proj_kernel(h_ref, w_ref, o_ref, wb_ref, *, gate):
    @pl.when(pl.program_id(1) == 0)
    def _():
        wb_ref[...] = w_ref[...].T.astype(BF16)

    if not gate:
        o_ref[...] = jnp.dot(h_ref[...], wb_ref[...], preferred_element_type=F32).astype(o_ref.dtype)
        return
    sub = 2 * LANE
    for c0 in range(0, o_ref.shape[1], sub):
        r = jnp.dot(h_ref[...], wb_ref[:, c0:c0 + sub], preferred_element_type=F32)
        rb = r.astype(o_ref.dtype)
        o_ref[:, c0:c0 + sub] = 0.5 * jnp.tanh(0.5 * rb) + 0.5


def _in_proj(h, w_t, layer, col0, n_cols, tn, out_dtype, gate):
    tm = 2048
    return pl.pallas_call(
        functools.partial(_proj_kernel, gate=gate),
        out_shape=jax.ShapeDtypeStruct((T_ALL, n_cols), out_dtype),
        grid=(n_cols // tn, T_ALL // tm),
        in_specs=[
            pl.BlockSpec((tm, D_MODEL), lambda j, i: (i, 0)),
            pl.BlockSpec((pl.Element(tn), pl.Element(D_MODEL)),
                         lambda j, i: (pl.multiple_of(layer * IN_COLS + col0 + j * tn, 8), 0)),
        ],
        out_specs=pl.BlockSpec((tm, tn), lambda j, i: (i, j)),
        scratch_shapes=[pltpu.VMEM((D_MODEL, tn), BF16)],
        compiler_params=_params("parallel", "arbitrary"),
        name="gate_proj" if gate else "in_proj",
    )(h, w_t)


def _hy_filter_kernel(z_ref, dl_ref, w1_ref, b1_ref, w2_ref, b2_ref, w3_ref, fwd_ref,
                      kre_ref, kim_ref, *, L):
    z = z_ref[...]
    a = jnp.sin(_dot_split(z, w1_ref[...]) + b1_ref[...])
    a = jnp.sin(_dot_split(a, w2_ref[...]) + b2_ref[...])
    h = _dot_split(a, w3_ref[...])
    decay = jnp.exp(-z[:, 0:1] * dl_ref[...])
    not_first = lax.broadcasted_iota(jnp.int32, (L, HY_W), 0) > 0
    sums, diffs = [], []
    for o in range(2):
        fw = h[:, (2 * o) * HY_W:(2 * o + 1) * HY_W] * decay
        bw = jnp.where(not_first, h[:, (2 * o + 1) * HY_W:(2 * o + 2) * HY_W] * decay, 0.0)
        sums.append(fw + bw)
        diffs.append(fw - bw)
    p = _dot(fwd_ref[...], jnp.concatenate(sums, 1))
    q = _dot(fwd_ref[L:2 * L, :], jnp.concatenate(diffs, 1))
    kre_ref[...] = p[0:L]
    first = lax.broadcasted_iota(jnp.int32, (L, 2 * HY_W), 0) == 0
    kim_ref[...] = jnp.where(first, p[L:L + 1], q)


def _hy_filters(L, w1p, b1, w2, b2, w3, fwd):
    zemb, deltas = _hyena_embedding(L)
    out = jax.ShapeDtypeStruct((L, 2 * HY_W), F32)
    return pl.pallas_call(
        functools.partial(_hy_filter_kernel, L=L),
        out_shape=(out, out),
        compiler_params=pltpu.CompilerParams(vmem_limit_bytes=VMEM_LIMIT),
        name=f"hy_filters_{L}",
    )(jnp.asarray(zemb), jnp.asarray(deltas), w1p, b1, w2, b2, w3, fwd)


def _group_step(ctx_body, lat_body):
    i = pl.program_id(0)
    pl.when(i < CTX_STEPS)(ctx_body)
    pl.when(i >= CTX_STEPS)(lat_body)


def _lat_index(i):
    return jnp.maximum(i - CTX_STEPS, 0)


def _hyena_kernel(hy_ref, cw_ref, cb_ref, bias_ref, kre_c, kim_c, fwd_c, inv_c,
                  kre_l, kim_l, fwd_l, inv_l, o_ref):
    _group_step(
        lambda: _hyena_body(hy_ref, cw_ref, cb_ref, bias_ref, kre_c, kim_c, fwd_c, inv_c, o_ref,
                            SEQ, CTX_SEQS_PER_STEP),
        lambda: _hyena_body(hy_ref, cw_ref, cb_ref, bias_ref, kre_l, kim_l, fwd_l, inv_l, o_ref,
                            DEC_SEQ, 1))


def _hyena_body(hy_ref, cw_ref, cb_ref, bias_ref, kre_ref, kim_ref, fwd_ref, inv_ref, o_ref, L, seqs):
    first = lax.broadcasted_iota(jnp.int32, (L, HY_W), 0) == 0

    def long_conv(u, o):
        uf = _dot(fwd_ref[...], u)
        ure, uim = uf[0:L], uf[L:2 * L]
        kre = kre_ref[:, o * HY_W:(o + 1) * HY_W]
        kim = kim_ref[:, o * HY_W:(o + 1) * HY_W]
        yre = jnp.where(first, ure * kre, ure * kre - uim * kim)
        yim = jnp.where(first, uim * kim, ure * kim + uim * kre)
        y = _dot(inv_ref[...], jnp.concatenate([yre, yim], 0))
        return y + u * bias_ref[o:o + 1, :]

    for g in range(seqs):
        sl = slice(g * L, (g + 1) * L)
        x = hy_ref[sl, :].astype(F32)
        rows = lax.broadcasted_iota(jnp.int32, x.shape, 0)
        prev = jnp.where(rows == 0, 0.0, pltpu.roll(x, 1, axis=0))
        nxt = jnp.where(rows == L - 1, 0.0, pltpu.roll(x, L - 1, axis=0))
        z = prev * cw_ref[0:1, :] + x * cw_ref[1:2, :] + nxt * cw_ref[2:3, :] + cb_ref[...]
        v, x1, x2 = z[:, 0:HY_W], z[:, HY_W:2 * HY_W], z[:, 2 * HY_W:3 * HY_W]
        u = x1 * long_conv(v, 0)
        o_ref[sl, :] = (x2 * long_conv(u, 1)).astype(o_ref.dtype)


def _const_spec(shape):
    return pl.BlockSpec(shape, lambda i: (0,) * len(shape))


def _hyena(z_main, layer, conv_w, conv_b, bias, filters, dft):
    tables, table_specs = [], []
    for L in (SEQ, DEC_SEQ):
        tables += [*filters[L], *dft[L]]
        table_specs += [_const_spec((L, 2 * HY_W)), _const_spec((L, 2 * HY_W)),
                        _const_spec((2 * L, L)), _const_spec((L, 2 * L))]
    return pl.pallas_call(
        _hyena_kernel,
        out_shape=jax.ShapeDtypeStruct((T_ALL, HY_W), BF16),
        grid=(MIXER_STEPS,),
        in_specs=[
            pl.BlockSpec((STEP_ROWS, 3 * HY_W), lambda i: (i, 0)),
            pl.BlockSpec((None, 3, 3 * HY_W), lambda i: (layer, 0, 0)),
            pl.BlockSpec((None, 1, 3 * HY_W), lambda i: (layer, 0, 0)),
            pl.BlockSpec((None, 2, HY_W), lambda i: (layer, 0, 0)),
        ] + table_specs,
        out_specs=pl.BlockSpec((STEP_ROWS, HY_W), lambda i: (i, 0)),
        compiler_params=_params("parallel"),
        name="hyena",
    )(z_main, conv_w, conv_b.reshape(DEPTH, 1, 3 * HY_W), bias, *tables)


def _win_masks():
    lane = lax.broadcasted_iota(jnp.int32, (1, LANE), 1)
    return lane < WIN_HD, lane >= WIN_HD


def _win_head_operands(q, k, v, h):
    lo_mask, hi_mask = _win_masks()
    col = h // 2
    lo = h % 2 == 0
    q128 = jnp.where(lo_mask if lo else hi_mask, q[:, col * LANE:(col + 1) * LANE], 0.0)
    swap = h in (1, 2)
    if swap:
        k = pltpu.roll(k, WIN_HD, axis=1)
        v = pltpu.roll(v, WIN_HD, axis=1)
    return q128, k, v, lo


def _win_kernel(sink_ref, q_ref, kv_ref, ck_ref, cv_ref, c_ref, su_ref, sd_ref, o_ref, *, layer):
    _group_step(
        lambda: _win_ctx_body(sink_ref, q_ref, kv_ref, o_ref, layer),
        lambda: _win_lat_body(sink_ref, q_ref, kv_ref, ck_ref, cv_ref, c_ref, su_ref, sd_ref, o_ref, layer))


def _win_ctx_body(sink_ref, q_ref, kv_ref, o_ref, layer):
    lo_mask, hi_mask = _win_masks()
    qscale = WIN_HD ** -0.5 * LOG2E
    for g in range(CTX_SEQS_PER_STEP):
        sl = slice(g * SEQ, (g + 1) * SEQ)
        q = q_ref[sl, :].astype(F32) * qscale
        k = kv_ref[sl, 0:LANE].astype(F32)
        v = kv_ref[sl, LANE:2 * LANE].astype(F32)
        cols = []
        for col in range(2):
            acc = None
            for h in (2 * col, 2 * col + 1):
                q128, kk, vv, lo = _win_head_operands(q, k, v, h)
                s = _dot_nt(q128, kk)
                sink = sink_ref[layer, h] * LOG2E
                m = jnp.maximum(jnp.max(s, -1, keepdims=True), sink)
                p = jnp.exp2(s - m)
                den = jnp.sum(p, -1, keepdims=True) + jnp.exp2(sink - m)
                o = _dot(p, vv) / den
                o = jnp.where(lo_mask if lo else hi_mask, o, 0.0)
                acc = o if acc is None else acc + o
            cols.append(acc)
        o_ref[sl, :] = jnp.concatenate(cols, 1).astype(o_ref.dtype)


def _win_lat_body(sink_ref, q_ref, kv_ref, ck_ref, cv_ref, c_ref, su_ref, sd_ref, o_ref, layer):
    L = DEC_SEQ
    half = WIN_HD // 2
    c, su, sd = c_ref[...], su_ref[...], sd_ref[...]
    qscale = WIN_HD ** -0.5 * LOG2E
    q = jnp.concatenate(
        [_rope128(q_ref[:, i * LANE:(i + 1) * LANE].astype(F32), c, su, sd, half) for i in range(2)],
        1) * qscale
    k = _rope128(kv_ref[:, 0:LANE].astype(F32), c, su, sd, half)
    v = kv_ref[:, LANE:2 * LANE].astype(F32)
    ck = ck_ref[...]
    cv = cv_ref[...]
    lo_mask, hi_mask = _win_masks()
    nb = L // CHUNK
    assert WINDOW == CHUNK
    rr = lax.broadcasted_iota(jnp.int32, (CHUNK, CHUNK), 0)
    cc = lax.broadcasted_iota(jnp.int32, (CHUNK, CHUNK), 1)
    band = {-1: jnp.where(cc >= rr, 0.0, NEG), 0: jnp.zeros((CHUNK, CHUNK), F32),
            1: jnp.where(cc <= rr, 0.0, NEG)}
    cols = []
    for col in range(2):
        acc_blocks = [None] * nb
        for h in (2 * col, 2 * col + 1):
            q128, kk, vv, lo = _win_head_operands(q, k, v, h)
            _, ckk, cvv, _ = _win_head_operands(q, ck, cv, h)
            sink = sink_ref[layer, h] * LOG2E
            for n in range(nb):
                blocks = [d for d in (-1, 0, 1) if 0 <= n + d < nb]
                k0 = (n + blocks[0]) * CHUNK
                k1 = (n + blocks[-1] + 1) * CHUNK
                qn = q128[n * CHUNK:(n + 1) * CHUNK]
                s_loc = _dot_nt(qn, kk[k0:k1]) + jnp.concatenate([band[d] for d in blocks], 1)
                s_ctx = _dot_nt(qn, ckk)
                m = jnp.maximum(jnp.maximum(jnp.max(s_loc, -1, keepdims=True),
                                            jnp.max(s_ctx, -1, keepdims=True)), sink)
                p_loc = jnp.exp2(s_loc - m)
                p_ctx = jnp.exp2(s_ctx - m)
                den = (jnp.sum(p_loc, -1, keepdims=True) + jnp.sum(p_ctx, -1, keepdims=True)
                       + jnp.exp2(sink - m))
                o = (_dot(p_loc, vv[k0:k1]) + _dot(p_ctx, cvv)) / den
                o = jnp.where(lo_mask if lo else hi_mask, o, 0.0)
                acc_blocks[n] = o if acc_blocks[n] is None else acc_blocks[n] + o
        cols.append(jnp.concatenate(acc_blocks, 0))
    o_ref[...] = jnp.concatenate(cols, 1).astype(o_ref.dtype)


def _win(z_main, sink, cache_k, cache_v, layer):
    c, su, sd, _ = _rope_tables(DEC_SEQ, WIN_HD, LANE)
    tab = _const_spec((DEC_SEQ, LANE))
    cache = pl.BlockSpec((None, None, PAST_LEN, LANE), lambda i: (_lat_index(i), layer, 0, 0))
    return pl.pallas_call(
        functools.partial(_win_kernel, layer=layer),
        out_shape=jax.ShapeDtypeStruct((T_ALL, WIN_HEADS * WIN_HD), BF16),
        grid=(MIXER_STEPS,),
        in_specs=[
            pl.BlockSpec(memory_space=pltpu.SMEM),
            pl.BlockSpec((STEP_ROWS, 256), lambda i: (i, COL_WQ // 256)),
            pl.BlockSpec((STEP_ROWS, 256), lambda i: (i, COL_WK // 256)),
            cache, cache, tab, tab, tab,
        ],
        out_specs=pl.BlockSpec((STEP_ROWS, 256), lambda i: (i, 0)),
        compiler_params=_params("parallel"),
        name="win",
    )(sink, z_main, z_main, cache_k, cache_v, jnp.asarray(c), jnp.asarray(su), jnp.asarray(sd))


def _ret_kernel(*refs, layer):
    _group_step(lambda: _ret_body(*refs, L=SEQ, layer=layer, ctx=True),
                lambda: _ret_body(*refs, L=DEC_SEQ, layer=layer, ctx=False))


def _ret_body(df_ref, db_ref, q_ref, k_ref, v0_ref, v1_ref, g0_ref, g1_ref, s0f_ref, s0b_ref,
              o_ref, sf_out, sb_out, s_ref, cross_ref, *, L, layer, ctx):
    seqs = STEP_ROWS // L
    if not ctx:
        sf_out[...] = jnp.zeros_like(sf_out)
        sb_out[...] = jnp.zeros_like(sb_out)
    C = CHUNK
    nc = L // C
    H = RET_HEADS
    qw = H * RET_DK
    vw = H * RET_DV

    def lane_table(width, per_head, fn):
        pos = lax.broadcasted_iota(jnp.int32, (C, per_head), 0).astype(F32)
        return jnp.concatenate([fn(h, pos) for h in range(H)], 1)

    def log_gamma(ref, h):
        d = jnp.full((1, 1), ref[layer, h], F32)
        return jnp.log(jax.nn.sigmoid(d))

    lgf = [log_gamma(df_ref, h) for h in range(H)]
    lgb = [log_gamma(db_ref, h) for h in range(H)]

    def tables(lg, reverse):
        if reverse:
            dq = lane_table(vw, RET_DV, lambda h, pos: jnp.exp((C - pos) * lg[h]))
            dk = lane_table(qw, RET_DK, lambda h, pos: jnp.exp(pos * lg[h]))
        else:
            dq = lane_table(vw, RET_DV, lambda h, pos: jnp.exp((pos + 1.0) * lg[h]))
            dk = lane_table(qw, RET_DK, lambda h, pos: jnp.exp((C - 1.0 - pos) * lg[h]))
        dc = jnp.concatenate([jnp.broadcast_to(jnp.exp(C * lg[h]), (1, RET_DV)) for h in range(H)], 1)
        return dq, dk, dc

    tab_f = tables(lgf, False)
    tab_b = tables(lgb, True)
    ii = lax.broadcasted_iota(jnp.int32, (C, C), 0)
    jj = lax.broadcasted_iota(jnp.int32, (C, C), 1)
    diff = (ii - jj).astype(F32)
    dmats = [jnp.where(diff >= 0, jnp.exp(jnp.maximum(diff, 0.0) * lgf[h]), 0.0)
             + jnp.where(diff <= 0, jnp.exp(jnp.maximum(-diff, 0.0) * lgb[h]), 0.0) for h in range(H)]
    dmat_stack = jnp.concatenate(dmats, 0)
    lane_q = lax.broadcasted_iota(jnp.int32, (1, qw), 1) // RET_DK

    srow = lax.broadcasted_iota(jnp.int32, (qw, vw), 0) // RET_DK
    scol = lax.broadcasted_iota(jnp.int32, (qw, vw), 1) // RET_DV
    diag = srow == scol

    for g in range(seqs):
        base = g * L
        rows_all = slice(base, base + L)
        q_all = q_ref[rows_all, :].astype(F32)
        k_all = k_ref[rows_all, :].astype(F32) * (RET_DK ** -0.5)
        v_all = jnp.concatenate([v0_ref[rows_all, :], v1_ref[rows_all, :]], 1).astype(F32)
        g_all = jnp.concatenate([g0_ref[rows_all, :], g1_ref[rows_all, :]], 1).astype(F32)

        def scan(tabs, reverse, s0_ref, s_out):
            dq, dk, dc = tabs
            if s0_ref is not None:
                s_ref[g] = jnp.zeros((qw, vw), F32)
                for h in range(H):
                    s_ref[g, h * RET_DK:(h + 1) * RET_DK, h * RET_DV:(h + 1) * RET_DV] = s0_ref[h]
            order = range(nc - 1, -1, -1) if reverse else range(nc)
            for step, ci in enumerate(order):
                sl = slice(ci * C, (ci + 1) * C)
                rs = slice(base + ci * C, base + (ci + 1) * C)
                qc, kc, vc = q_all[sl], k_all[sl], v_all[sl]
                upd = jnp.where(diag, _dot_tn(kc * dk, vc), 0.0)
                if s0_ref is None and step == 0:
                    if not reverse:
                        cross_ref[rs, :] = jnp.zeros((C, vw), F32)
                    s_ref[g] = upd
                    continue
                st = s_ref[g]
                cross = _dot(qc, st) * dq
                if reverse:
                    cross_ref[rs, :] = cross_ref[rs, :] + cross
                else:
                    cross_ref[rs, :] = cross
                s_ref[g] = st * dc + upd
            if s_out is not None:
                for h in range(H):
                    s_out[g, h] = s_ref[g, h * RET_DK:(h + 1) * RET_DK, h * RET_DV:(h + 1) * RET_DV]

        scan(tab_f, False, None if ctx else s0f_ref, sf_out if ctx else None)
        scan(tab_b, True, None if ctx else s0b_ref, sb_out if ctx else None)

        for ci in range(nc):
            sl = slice(ci * C, (ci + 1) * C)
            rs = slice(base + ci * C, base + (ci + 1) * C)
            q_stack = jnp.concatenate([jnp.where(lane_q == h, q_all[sl], 0.0) for h in range(H)], 0)
            att = _dot_nt(q_stack, k_all[sl]) * dmat_stack
            ov = _dot(att, v_all[sl])
            for h in range(H):
                hv = slice(h * RET_DV, (h + 1) * RET_DV)
                o = ov[h * C:(h + 1) * C, hv] + cross_ref[rs, hv]
                gt = g_all[sl, hv]
                o_ref[rs, hv] = ((gt * jax.nn.sigmoid(gt)) * _layer_norm(o)).astype(o_ref.dtype)


def _retention(z_main, dec_f, dec_b, s0f, s0b, layer):
    def zcol(col):
        return pl.BlockSpec((STEP_ROWS, 256), lambda i: (i, col // 256))

    smem = pl.BlockSpec(memory_space=pltpu.SMEM)
    z_specs = [zcol(COL_RQ), zcol(COL_RK), zcol(COL_RV), zcol(COL_RV + 256),
               zcol(COL_RG), zcol(COL_RG + 256)]
    s0_spec = pl.BlockSpec((None, None, RET_HEADS, RET_DK, RET_DV),
                           lambda i: (_lat_index(i), layer, 0, 0, 0))
    st_shape = jax.ShapeDtypeStruct((MIXER_STEPS * CTX_SEQS_PER_STEP, RET_HEADS, RET_DK, RET_DV), F32)
    st_spec = pl.BlockSpec((CTX_SEQS_PER_STEP, RET_HEADS, RET_DK, RET_DV), lambda i: (i, 0, 0, 0))
    return pl.pallas_call(
        functools.partial(_ret_kernel, layer=layer),
        out_shape=(jax.ShapeDtypeStruct((T_ALL, RET_HEADS * RET_DV), BF16), st_shape, st_shape),
        grid=(MIXER_STEPS,),
        in_specs=[smem, smem] + z_specs + [s0_spec, s0_spec],
        out_specs=(pl.BlockSpec((STEP_ROWS, RET_HEADS * RET_DV), lambda i: (i, 0)), st_spec, st_spec),
        scratch_shapes=[pltpu.VMEM((CTX_SEQS_PER_STEP, RET_HEADS * RET_DK, RET_HEADS * RET_DV), F32),
                        pltpu.VMEM((STEP_ROWS, RET_HEADS * RET_DV), F32)],
        compiler_params=_params("parallel"),
        name="retention",
    )(dec_f, dec_b, *([z_main] * 6), s0f, s0b)


def _rms_norm(x, g):
    return x * lax.rsqrt(jnp.mean(x * x, -1, keepdims=True) + RMS_EPS) * g


def _mla_keys(kn, kr):
    lane_r = lax.broadcasted_iota(jnp.int32, (1, LANE), 1)
    return jnp.concatenate([kn, jnp.where(lane_r < MLA_ROPE, kr, 0.0)], 1).astype(BF16)


def _mla_attend(qn, qr, k_cat, vv, o_ref, row0):
    qscale = (MLA_NOPE + MLA_ROPE) ** -0.5 * LOG2E
    lane_n = lax.broadcasted_iota(jnp.int32, (1, MLA_HEADS * MLA_NOPE), 1) // MLA_NOPE
    lane_r = lax.broadcasted_iota(jnp.int32, (1, LANE), 1)
    qn = qn * qscale
    qr = qr * qscale
    lq = qn.shape[0]
    heads = []
    for h in range(MLA_HEADS):
        qnh = jnp.where(lane_n == h, qn, 0.0)
        qrh = qr if h == 0 else pltpu.roll(qr, LANE - h * MLA_ROPE, axis=1)
        qrh = jnp.where(lane_r < MLA_ROPE, qrh, 0.0)
        heads.append(jnp.concatenate([qnh, qrh], 1).astype(BF16))
    s = _dot_nt(jnp.concatenate(heads, 0), k_cat)
    m = jnp.max(s, -1, keepdims=True)
    p = jnp.exp2(s - m)
    den = jnp.sum(p, -1, keepdims=True)
    o = _dot(p, vv) / den
    acc = None
    for h in range(MLA_HEADS):
        oh = jnp.where(lane_n == h, o[h * lq:(h + 1) * lq], 0.0)
        acc = oh if acc is None else acc + oh
    o_ref[row0:row0 + lq, :] = acc.astype(o_ref.dtype)


MLA_QN = MLA_HEADS * MLA_NOPE


def _mla_kernel(cq_ref, ckv_ref, kr_ref, cckv_ref, ckr_ref, c_ref, su_ref, sd_ref,
                qg_ref, kg_ref, wq_ref, wkv_ref, o_ref, ckvn_ref):
    weights = (qg_ref, kg_ref, wq_ref, wkv_ref)
    _group_step(
        lambda: _mla_ctx_body(cq_ref, ckv_ref, kr_ref, *weights, o_ref, ckvn_ref),
        lambda: _mla_lat_body(cq_ref, ckv_ref, kr_ref, cckv_ref, ckr_ref, c_ref, su_ref, sd_ref,
                              *weights, o_ref, ckvn_ref))


def _mla_ctx_body(cq_ref, ckv_ref, kr_ref, qg_ref, kg_ref, wq_ref, wkv_ref, o_ref, ckvn_ref):
    q = _dot(_rms_norm(cq_ref[...].astype(F32), qg_ref[...]), wq_ref[...])
    qn, qr = q[:, 0:MLA_QN], q[:, MLA_QN:]
    ckvn = _rms_norm(ckv_ref[...].astype(F32), kg_ref[...])
    ckvn_ref[...] = ckvn
    kv = _dot(ckvn, wkv_ref[...])
    k_cat = _mla_keys(kv[:, 0:MLA_QN], kr_ref[...].astype(F32))
    vv = kv[:, MLA_QN:].astype(BF16)
    for g in range(CTX_SEQS_PER_STEP):
        sl = slice(g * SEQ, (g + 1) * SEQ)
        _mla_attend(qn[sl], qr[sl], k_cat[sl], vv[sl], o_ref, g * SEQ)


def _mla_lat_body(cq_ref, ckv_ref, kr_ref, cckv_ref, ckr_ref, c_ref, su_ref, sd_ref,
                  qg_ref, kg_ref, wq_ref, wkv_ref, o_ref, ckvn_ref):
    half = MLA_ROPE // 2
    c, su, sd = c_ref[...], su_ref[...], sd_ref[...]
    q = _dot(_rms_norm(cq_ref[...].astype(F32), qg_ref[...]), wq_ref[...])
    qn = q[:, 0:MLA_QN]
    qr = _rope128(q[:, MLA_QN:], c, su, sd, half)
    ckvn = _rms_norm(ckv_ref[...].astype(F32), kg_ref[...])
    ckvn_ref[...] = ckvn
    ckv_all = jnp.concatenate([ckvn, cckv_ref[...]], 0)
    kv = _dot(ckv_all, wkv_ref[...])
    vv = kv[:, MLA_QN:].astype(BF16)
    kr = jnp.concatenate([_rope128(kr_ref[...].astype(F32), c, su, sd, half), ckr_ref[...]], 0)
    k_cat = _mla_keys(kv[:, 0:MLA_QN], kr)
    rows_per_call = 256
    for n in range(DEC_SEQ // rows_per_call):
        rows = slice(n * rows_per_call, (n + 1) * rows_per_call)
        _mla_attend(qn[rows], qr[rows], k_cat, vv, o_ref, n * rows_per_call)


def _mla(z_main, cache_ckv, cache_kr_pad, weights, layer):
    c, su, sd, _ = _rope_tables(DEC_SEQ, MLA_ROPE, LANE)
    tab = _const_spec((DEC_SEQ, LANE))
    cache = pl.BlockSpec((None, None, PAST_LEN, LANE), lambda i: (_lat_index(i), layer, 0, 0))

    def weight(*shape):
        return pl.BlockSpec((None,) + shape, lambda i: (layer, 0, 0))

    return pl.pallas_call(
        _mla_kernel,
        out_shape=(jax.ShapeDtypeStruct((T_ALL, MLA_HEADS * MLA_V), BF16),
                   jax.ShapeDtypeStruct((T_ALL, MLA_KV_LORA), F32)),
        grid=(MIXER_STEPS,),
        in_specs=[
            pl.BlockSpec((STEP_ROWS, 256), lambda i: (i, COL_CQ // 256)),
            pl.BlockSpec((STEP_ROWS, LANE), lambda i: (i, COL_CKV // LANE)),
            pl.BlockSpec((STEP_ROWS, LANE), lambda i: (i, COL_KROPE // LANE)),
            cache, cache, tab, tab, tab,
            weight(1, MLA_Q_LORA), weight(1, MLA_KV_LORA),
            weight(MLA_Q_LORA, MLA_QN + MLA_HEADS * MLA_ROPE),
            weight(MLA_KV_LORA, MLA_QN + MLA_HEADS * MLA_V),
        ],
        out_specs=(pl.BlockSpec((STEP_ROWS, 256), lambda i: (i, 0)),
                   pl.BlockSpec((STEP_ROWS, MLA_KV_LORA), lambda i: (i, 0))),
        compiler_params=_params("parallel"),
        name="mla",
    )(z_main, z_main, z_main, cache_ckv, cache_kr_pad,
      jnp.asarray(c), jnp.asarray(su), jnp.asarray(sd), *weights)


def _route(logits_t, rb):
    scores = jax.nn.sigmoid(logits_t)
    biased = scores + rb
    sc = [scores[e:e + 1, :] for e in range(N_EXPERTS)]
    bi = [biased[e:e + 1, :] for e in range(N_EXPERTS)]
    epg = EXPERTS_PER_GROUP
    gsum = []
    for g in range(N_GROUPS):
        v = bi[g * epg:(g + 1) * epg]
        best = None
        for a in range(epg):
            for b in range(a + 1, epg):
                pair = v[a] + v[b]
                best = pair if best is None else jnp.maximum(best, pair)
        gsum.append(best)
    combine = []
    sel = []
    for g in range(N_GROUPS):
        is_best = None
        for g2 in range(N_GROUPS):
            if g2 == g:
                continue
            c = gsum[g] > gsum[g2] if g2 < g else gsum[g] >= gsum[g2]
            is_best = c if is_best is None else jnp.logical_and(is_best, c)
        for a in range(epg):
            e = g * epg + a
            rank = jnp.zeros_like(bi[e])
            for b in range(epg):
                if b == a:
                    continue
                e2 = g * epg + b
                ahead = bi[e2] >= bi[e] if b < a else bi[e2] > bi[e]
                rank = rank + jnp.where(ahead, 1.0, 0.0)
            sel.append(jnp.logical_and(is_best, rank < 2.0))
    wsum = None
    for e in range(N_EXPERTS):
        w = jnp.where(sel[e], sc[e], 0.0)
        wsum = w if wsum is None else wsum + w
    for e in range(N_EXPERTS):
        combine.append(jnp.where(sel[e], ROUTE_SCALE * sc[e] / wsum, 0.0))
    return jnp.concatenate(combine, 0)


MERGE_BRANCH_ROWS = (HY_W, WIN_HEADS * WIN_HD, RET_HEADS * RET_DV, MLA_HEADS * MLA_V)
MERGE_ROWS = sum(MERGE_BRANCH_ROWS) + D_MODEL


def _merge_kernel(ya_ref, yb_ref, yc_ref, yd_ref, gt_ref, xc_ref, xl_ref, m_ref,
                  w_ref, g_ref, b_ref, rw_ref, rb_ref,
                  x1_ref, h2_ref, cmb_ref, *, ctx_tiles, sub_rows):
    D = D_MODEL
    rw = rw_ref[...]
    rw_hi = rw.astype(BF16)
    rw_lo = (rw - rw_hi.astype(F32)).astype(BF16)
    g1 = m_ref[:, 2 * D:3 * D]
    s2 = m_ref[:, 3 * D:4 * D]
    sc2 = m_ref[:, 4 * D:5 * D]
    is_ctx = pl.program_id(0) < ctx_tiles
    offs = np.cumsum((0,) + MERGE_BRANCH_ROWS)
    branches = tuple((y_ref, slice(int(offs[i]), int(offs[i + 1])))
                     for i, y_ref in enumerate((ya_ref, yb_ref, yc_ref, yd_ref)))
    w_out_rows = slice(int(offs[-1]), MERGE_ROWS)
    for r0 in range(0, x1_ref.shape[0], sub_rows):
        rows = slice(r0, r0 + sub_rows)
        merged = None
        for i, (y_ref, w_rows) in enumerate(branches):
            t = gt_ref[rows, i * D:(i + 1) * D] * jnp.dot(
                y_ref[rows, :], w_ref[w_rows, :], preferred_element_type=F32).astype(BF16)
            merged = t if merged is None else merged + t
        out1 = jnp.dot(merged, w_ref[w_out_rows, :], preferred_element_type=F32)
        x = jnp.where(is_ctx, xc_ref[rows, :], xl_ref[rows, :])
        x1 = _layer_norm(ALPHA * x + g1 * out1) * g_ref[...] + b_ref[...]
        x1_ref[rows, :] = x1
        h2 = _layer_norm(x1) * (1.0 + sc2) + s2
        h2_hi = h2.astype(BF16)
        h2_ref[rows, :] = h2_hi
        h2_lo = (h2 - h2_hi.astype(F32)).astype(BF16)
        logits = (jnp.dot(h2_hi, rw_hi, preferred_element_type=F32)
                  + (jnp.dot(h2_lo, rw_hi, preferred_element_type=F32)
                     + jnp.dot(h2_hi, rw_lo, preferred_element_type=F32)))
        cmb_ref[:, rows] = _route(logits.T[0:N_EXPERTS], rb_ref[...])


def _merge(ya, yb, yc, yd, gates, x_ctx, x_lat, mods, w_merge, ln1_g, ln1_b,
           router_w, router_b, layer):
    tm = 512
    row = _mod_row(tm)
    D = D_MODEL
    ctx_tiles = T_CTX // tm

    def tile(w):
        return pl.BlockSpec((tm, w), lambda i: (i, 0))

    def weight(k, n):
        return pl.BlockSpec((None, k, n), lambda i: (layer, 0, 0))

    return pl.pallas_call(
        functools.partial(_merge_kernel, ctx_tiles=ctx_tiles, sub_rows=256),
        out_shape=(jax.ShapeDtypeStruct((T_ALL, D), F32),
                   jax.ShapeDtypeStruct((T_ALL, D), BF16),
                   jax.ShapeDtypeStruct((N_EXPERTS, T_ALL), F32)),
        grid=(T_ALL // tm,),
        in_specs=[
            tile(256), tile(256), tile(512), tile(256), tile(4 * D),
            pl.BlockSpec((tm, D), lambda i: (jnp.minimum(i, ctx_tiles - 1), 0)),
            pl.BlockSpec((tm, D), lambda i: (jnp.maximum(i - ctx_tiles, 0), 0)),
            _mod_spec(layer, row),
            weight(MERGE_ROWS, D), weight(1, D), weight(1, D),
            pl.BlockSpec((D, LANE), lambda i: (0, 0)),
            pl.BlockSpec((N_EXPERTS, 1), lambda i: (0, 0)),
        ],
        out_specs=(tile(D), tile(D), pl.BlockSpec((N_EXPERTS, tm), lambda i: (0, i))),
        compiler_params=_params("parallel"),
        name="merge",
    )(ya, yb, yc, yd, gates, x_ctx, x_lat, mods, w_merge,
      ln1_g.reshape(DEPTH, 1, D), ln1_b.reshape(DEPTH, 1, D), router_w,
      router_b.reshape(N_EXPERTS, 1))


MOE_EXPERTS_PER_STEP = 2


def _moe_kernel(*refs, next_h):
    if next_h:
        (h_ref, c_ref, x1_ref, m_ref, wg_ref, wu_ref, wd_ref, g_ref, b_ref, mn_ref,
         o_ref, hn_ref, acc_ref) = refs[-13:]
    else:
        h_ref, c_ref, x1_ref, m_ref, wg_ref, wu_ref, wd_ref, g_ref, b_ref, o_ref, acc_ref = refs
    eg = pl.program_id(1)

    @pl.when(eg == 0)
    def _():
        acc_ref[...] = jnp.zeros_like(acc_ref)

    h = h_ref[...]
    cmb = c_ref[...]
    lane = lax.broadcasted_iota(jnp.int32, cmb.shape, 1)
    hid = []
    for k in range(MOE_EXPERTS_PER_STEP):
        gate = jnp.dot(h, wg_ref[k].astype(BF16), preferred_element_type=F32)
        up = jnp.dot(h, wu_ref[k].astype(BF16), preferred_element_type=F32)
        e = eg * MOE_EXPERTS_PER_STEP + k
        ce = jnp.sum(jnp.where(lane == e, cmb, 0.0), -1, keepdims=True)
        sig = 0.5 * jnp.tanh(0.5 * gate) + 0.5
        hid.append((gate * sig * (up * ce)).astype(BF16))
    wd = wd_ref[...].reshape(MOE_EXPERTS_PER_STEP * D_EXPERT, D_MODEL).astype(BF16)
    acc_ref[...] += jnp.dot(jnp.concatenate(hid, 1), wd, preferred_element_type=F32)

    @pl.when(eg == N_EXPERTS // MOE_EXPERTS_PER_STEP - 1)
    def _():
        g2 = m_ref[:, 5 * D_MODEL:6 * D_MODEL]
        y = _layer_norm(ALPHA * x1_ref[...] + g2 * acc_ref[...])
        y = y * g_ref[...] + b_ref[...]
        o_ref[...] = y
        if next_h:
            s1 = mn_ref[:, 0:D_MODEL]
            sc1 = mn_ref[:, D_MODEL:2 * D_MODEL]
            hn_ref[...] = (_layer_norm(y) * (1.0 + sc1) + s1).astype(hn_ref.dtype)


def _moe(h2, combine, x1, mods, w_gate, w_up, w_down, ln2_g, ln2_b, layer, row0, n_rows,
         next_h=False, prev_h=None):
    tm = 1024
    row = _mod_row(tm)
    D = D_MODEL
    t0 = row0 // tm
    eps = MOE_EXPERTS_PER_STEP
    mod_spec = _mod_spec(layer, lambda i, e: row(t0 + i))
    in_specs = [
        pl.BlockSpec((tm, D), lambda i, e: (t0 + i, 0)),
        pl.BlockSpec((tm, N_EXPERTS), lambda i, e: (t0 + i, 0)),
        pl.BlockSpec((tm, D), lambda i, e: (t0 + i, 0)),
        mod_spec,
        pl.BlockSpec((None, eps, D, D_EXPERT), lambda i, e: (layer, e, 0, 0)),
        pl.BlockSpec((None, eps, D, D_EXPERT), lambda i, e: (layer, e, 0, 0)),
        pl.BlockSpec((None, eps, D_EXPERT, D), lambda i, e: (layer, e, 0, 0)),
        pl.BlockSpec((None, 1, D), lambda i, e: (layer, 0, 0)),
        pl.BlockSpec((None, 1, D), lambda i, e: (layer, 0, 0)),
    ]
    args = [h2, combine, x1, mods, w_gate, w_up, w_down,
            ln2_g.reshape(DEPTH, 1, D), ln2_b.reshape(DEPTH, 1, D)]
    out_shape = jax.ShapeDtypeStruct((n_rows, D), F32)
    out_specs = pl.BlockSpec((tm, D), lambda i, e: (i, 0))
    aliases = {}
    if next_h:
        in_specs.append(_mod_spec(layer + 1, lambda i, e: row(t0 + i)))
        args.append(mods)
        out_shape = (out_shape, jax.ShapeDtypeStruct((T_ALL, D), BF16))
        out_specs = (out_specs, pl.BlockSpec((tm, D), lambda i, e: (t0 + i, 0)))
        if prev_h is not None:
            in_specs = [pl.BlockSpec(memory_space=pl.ANY)] + in_specs
            args = [prev_h] + args
            aliases = {0: 1}
    return pl.pallas_call(
        functools.partial(_moe_kernel, next_h=next_h),
        out_shape=out_shape,
        grid=(n_rows // tm, N_EXPERTS // eps),
        in_specs=in_specs,
        out_specs=out_specs,
        input_output_aliases=aliases,
        scratch_shapes=[pltpu.VMEM((tm, D), F32)],
        compiler_params=_params("parallel", "arbitrary"),
        name="moe",
    )(*args)


def kernel(x_prompt, x_sample, cache_win_k, cache_win_v, cache_mla_ckv, cache_mla_krope,
           state_ret_fwd, state_ret_bwd, c, c_ctx, w_ada, b_ada, w_in,
           hy_conv_w, hy_conv_b, hy_w1, hy_b1, hy_w2, hy_b2, hy_w3, hy_bias,
           win_sink, ret_decay_fwd, ret_decay_bwd, mla_q_norm, mla_kv_norm, mla_w_uq, mla_w_ukv,
           w_br_a, w_br_b, w_br_c, w_br_d, w_out, ln1_g, ln1_b, ln2_g, ln2_b,
           router_w, router_b, moe_w_gate, moe_w_up, moe_w_down):
    D = D_MODEL
    x_ctx = x_prompt.reshape(T_CTX, D)
    x_lat = x_sample.reshape(T_LAT, D)

    mods = _ada_mods(c_ctx, c, w_ada, b_ada)

    w_in_t = jnp.swapaxes(w_in, 1, 2).reshape(DEPTH * IN_COLS, D)
    cache_k = cache_win_k.reshape(DEC_BATCH, DEPTH, PAST_LEN, WIN_KV_HEADS * WIN_HD)
    cache_v = cache_win_v.reshape(DEC_BATCH, DEPTH, PAST_LEN, WIN_KV_HEADS * WIN_HD)
    cache_kr = jnp.pad(cache_mla_krope, ((0, 0), (0, 0), (0, 0), (0, LANE - MLA_ROPE)))

    uq = mla_w_uq.reshape(DEPTH, MLA_Q_LORA, MLA_HEADS, MLA_NOPE + MLA_ROPE)
    ukv = mla_w_ukv.reshape(DEPTH, MLA_KV_LORA, MLA_HEADS, MLA_NOPE + MLA_V)
    mla_weights = (
        mla_q_norm.reshape(DEPTH, 1, MLA_Q_LORA),
        mla_kv_norm.reshape(DEPTH, 1, MLA_KV_LORA),
        jnp.concatenate([uq[..., :MLA_NOPE].reshape(DEPTH, MLA_Q_LORA, MLA_HEADS * MLA_NOPE),
                         uq[..., MLA_NOPE:].reshape(DEPTH, MLA_Q_LORA, MLA_HEADS * MLA_ROPE)], -1),
        jnp.concatenate([ukv[..., :MLA_NOPE].reshape(DEPTH, MLA_KV_LORA, MLA_HEADS * MLA_NOPE),
                         ukv[..., MLA_NOPE:].reshape(DEPTH, MLA_KV_LORA, MLA_HEADS * MLA_V)], -1),
    )

    hy_w1p = jnp.pad(hy_w1, ((0, 0), (0, LANE - HY_EMB), (0, 0)))
    dft = {}
    for L in (SEQ, DEC_SEQ):
        fwd, inv = _dft_tables(L)
        dft[L] = (jnp.asarray(fwd).astype(BF16), jnp.asarray(inv).astype(BF16))
    router_w_pad = jnp.pad(router_w, ((0, 0), (0, LANE - N_EXPERTS)))
    w_merge = jnp.concatenate([w_br_a, w_br_b, w_br_c, w_br_d, w_out], 1).astype(BF16)

    new_k, new_v, new_ckv, new_kr, new_sf, new_sb = [], [], [], [], [], []
    for l in range(DEPTH):
        if l == 0:
            h = _ln_mod(x_ctx, None, mods, l, 0)
            h = _ln_mod(x_lat, h, mods, l, T_CTX)
        z = _in_proj(h, w_in_t, l, 0, Z_MAIN, Z_MAIN // 2, BF16, gate=False)
        gates = _in_proj(h, w_in_t, l, COL_GATE, 4 * D, D, BF16, gate=True)

        filters = {L: _hy_filters(L, hy_w1p[l], hy_b1[l][None], hy_w2[l], hy_b2[l][None], hy_w3[l],
                                  dft[L][0]) for L in (SEQ, DEC_SEQ)}
        ya = _hyena(z, l, hy_conv_w, hy_conv_b, hy_bias, filters, dft)
        yb = _win(z, win_sink, cache_k, cache_v, l)
        yc, sf, sb = _retention(z, ret_decay_fwd, ret_decay_bwd, state_ret_fwd, state_ret_bwd, l)
        yd, ckvn = _mla(z, cache_mla_ckv, cache_kr, mla_weights, l)

        x1, h2, combine_t = _merge(ya, yb, yc, yd, gates, x_ctx, x_lat, mods, w_merge,
                                   ln1_g, ln1_b, router_w_pad, router_b, l)
        moe_args = (h2, combine_t.T, x1, mods, moe_w_gate, moe_w_up, moe_w_down, ln2_g, ln2_b, l)
        if l + 1 < DEPTH:
            x_ctx, h = _moe(*moe_args, 0, T_CTX, next_h=True)
            x_lat, h = _moe(*moe_args, T_CTX, T_LAT, next_h=True, prev_h=h)
        else:
            x_ctx = _moe(*moe_args, 0, T_CTX)
            x_lat = _moe(*moe_args, T_CTX, T_LAT)

        def ctx_cols(col, width):
            return z[:T_CTX, col:col + width].astype(F32)

        new_k.append(ctx_cols(COL_WK, 128).reshape(BATCH, SEQ, WIN_KV_HEADS, WIN_HD))
        new_v.append(ctx_cols(COL_WV, 128).reshape(BATCH, SEQ, WIN_KV_HEADS, WIN_HD))
        new_ckv.append(ckvn[:T_CTX].reshape(BATCH, SEQ, MLA_KV_LORA))
        new_kr.append(ctx_cols(COL_KROPE, MLA_ROPE).reshape(BATCH, SEQ, MLA_ROPE))
        new_sf.append(sf[:BATCH])
        new_sb.append(sb[:BATCH])

    y_prompt = x_ctx.reshape(BATCH, SEQ, D)
    y_sample = x_lat.reshape(DEC_BATCH, DEC_SEQ, D)
    return (y_prompt, y_sample, jnp.stack(new_k, 1), jnp.stack(new_v, 1), jnp.stack(new_ckv, 1),
            jnp.stack(new_kr, 1), jnp.stack(new_sf, 1), jnp.stack(new_sb, 1))
```

```python
import functools
import math

import numpy as np
import jax
import jax.numpy as jnp
from jax import lax
from jax.experimental import pallas as pl
from jax.experimental.pallas import tpu as pltpu

F32 = jnp.float32
BF16 = jnp.bfloat16

D_MODEL = 1024
BATCH = 16
SEQ = 256
DEPTH = 2
DEC_BATCH = 2
DEC_SEQ = 1024
PAST_LEN = 256
GRID_W = 64
CHUNK = 128
ROPE_BASE = 10000.0
NEG = -1e30
LN_EPS = 1e-5
RMS_EPS = 1e-6
LOG2E = math.log2(math.e)

HY_W = 256
HY_BANDS = 16
HY_EMB = 1 + 2 * HY_BANDS
HY_FFN = 64
HY_FAST_DECAY = 0.3
HY_SLOW_DECAY = 1.5
HY_TARGET = 1e-2

WIN_HEADS = 4
WIN_KV_HEADS = 2
WIN_HD = 64
WINDOW = 128

RET_HEADS = 4
RET_DK = 64
RET_DV = 128

MLA_HEADS = 4
MLA_Q_LORA = 256
MLA_KV_LORA = 128
MLA_NOPE = 64
MLA_ROPE = 32
MLA_V = 64

N_EXPERTS = 16
N_GROUPS = 4
EXPERTS_PER_GROUP = N_EXPERTS // N_GROUPS
D_EXPERT = 256
ROUTE_SCALE = 2.5

ALPHA = (2.0 * DEPTH) ** 0.25

T_CTX = BATCH * SEQ
T_LAT = DEC_BATCH * DEC_SEQ
T_ALL = T_CTX + T_LAT

COL_HY = 0
COL_WQ = 768
COL_WK = 1024
COL_WV = 1152
COL_RQ = 1280
COL_RK = 1536
COL_RV = 1792
COL_RG = 2304
COL_CQ = 2816
COL_CKV = 3072
COL_KROPE = 3200
COL_GATE = 3232
IN_COLS = COL_GATE + 4 * D_MODEL
Z_MAIN = 3328

LANE = 128
STEP_ROWS = 1024
CTX_SEQS_PER_STEP = STEP_ROWS // SEQ
CTX_STEPS = T_CTX // STEP_ROWS
MIXER_STEPS = T_ALL // STEP_ROWS
VMEM_LIMIT = 56 * 1024 * 1024


def _params(*sem):
    return pltpu.CompilerParams(dimension_semantics=sem, vmem_limit_bytes=VMEM_LIMIT)


def _dot(a, b):
    return jnp.dot(a.astype(BF16), b.astype(BF16), preferred_element_type=F32)


def _dot_split(a, b):
    a_hi = a.astype(BF16)
    a_lo = (a - a_hi.astype(F32)).astype(BF16)
    b_hi = b.astype(BF16)
    b_lo = (b - b_hi.astype(F32)).astype(BF16)

    def mm(x, y):
        return jnp.dot(x, y, preferred_element_type=F32)

    return mm(a_hi, b_hi) + (mm(a_lo, b_hi) + mm(a_hi, b_lo))


def _dot_nt(a, b):
    return lax.dot_general(a.astype(BF16), b.astype(BF16), (((1,), (1,)), ((), ())),
                           preferred_element_type=F32)


def _dot_tn(a, b):
    return lax.dot_general(a.astype(BF16), b.astype(BF16), (((0,), (0,)), ((), ())),
                           preferred_element_type=F32)


def _layer_norm(x):
    mu = jnp.mean(x, -1, keepdims=True)
    xc = x - mu
    var = jnp.mean(xc * xc, -1, keepdims=True)
    return xc * lax.rsqrt(var + LN_EPS)


def _mod_row(tile_rows):
    def row(i):
        start = i * tile_rows
        return jnp.where(start < T_CTX, 0, 1 + (start - T_CTX) // DEC_SEQ)
    return row


@functools.lru_cache(maxsize=None)
def _dft_tables(L):
    f = np.arange(L, dtype=np.int64)[:, None]
    s = np.arange(L, dtype=np.int64)[None, :]
    ang = np.pi * ((f * s) % (2 * L)).astype(np.float64) / L
    cos = np.cos(ang)
    sin = np.sin(ang)
    alt = np.where(np.arange(L) % 2 == 0, 1.0, -1.0)
    fwd_im = -sin
    fwd_im[0, :] = alt
    fwd = np.concatenate([cos, fwd_im], 0)
    inv_re = cos.T / L
    inv_re[:, 0] = 1.0 / (2 * L)
    inv_im = -sin.T / L
    inv_im[:, 0] = alt / (2 * L)
    inv = np.concatenate([inv_re, inv_im], 1)
    return fwd.astype(np.float32), inv.astype(np.float32)


@functools.lru_cache(maxsize=None)
def _hyena_embedding(L):
    t01 = np.linspace(0.0, 1.0, L, dtype=np.float64)[:, None]
    bands = np.linspace(1e-4, HY_BANDS - 1, HY_BANDS, dtype=np.float64)
    ang = (2.0 * math.pi / L) * np.arange(L, dtype=np.float64)[:, None] * bands[None, :]
    z = np.concatenate([t01, np.cos(ang), -np.sin(ang)], -1)
    zp = np.zeros((L, LANE), np.float64)
    zp[:, :HY_EMB] = z
    deltas = np.abs(np.linspace(math.log(HY_TARGET) / HY_SLOW_DECAY,
                                math.log(HY_TARGET) / HY_FAST_DECAY, HY_W, dtype=np.float64))
    return zp.astype(np.float32), deltas[None, :].astype(np.float32)


@functools.lru_cache(maxsize=None)
def _rope_tables(L, rot_dim, width):
    rows = L // GRID_W
    n_freq = rot_dim // 4
    half = rot_dim // 2
    inv = ROPE_BASE ** (-np.arange(n_freq, dtype=np.float64) / n_freq)
    pos = np.arange(L)
    row = (pos // GRID_W).astype(np.float64)
    col = (pos % GRID_W).astype(np.float64)
    ang = np.concatenate([row[:, None] * inv, col[:, None] * inv], -1)
    cos, sin = np.cos(ang), np.sin(ang)
    zero = np.zeros_like(sin)
    c = np.tile(np.concatenate([cos, cos], -1), (1, width // rot_dim))
    s_up = np.tile(np.concatenate([-sin, zero], -1), (1, width // rot_dim))
    s_dn = np.tile(np.concatenate([zero, sin], -1), (1, width // rot_dim))
    return c.astype(np.float32), s_up.astype(np.float32), s_dn.astype(np.float32), half


def _rope128(x, c, s_up, s_dn, half):
    up = pltpu.roll(x, LANE - half, axis=1)
    dn = pltpu.roll(x, half, axis=1)
    return x * c + up * s_up + dn * s_dn


MOD_ROWS = 1 + DEC_BATCH


def _ada_kernel(ct_ref, w_ref, b_ref, o_ref):
    @pl.when(pl.program_id(1) == 0)
    def _():
        for r in range(MOD_ROWS):
            o_ref[r] = b_ref[...]

    ct = ct_ref[...]
    s = ct * jax.nn.sigmoid(ct)
    tk, n = w_ref.shape
    rows, cols = 64, 4 * LANE
    s_cols = [[s[k0:k0 + rows, r:r + 1] for k0 in range(0, tk, rows)] for r in range(MOD_ROWS)]
    for c0 in range(0, n, cols):
        acc = [None] * MOD_ROWS
        for ki, k0 in enumerate(range(0, tk, rows)):
            wc = w_ref[k0:k0 + rows, c0:c0 + cols]
            for r in range(MOD_ROWS):
                part = jnp.sum((wc * s_cols[r][ki]).reshape(rows // 8, 8, cols), axis=0)
                acc[r] = part if acc[r] is None else acc[r] + part
        for r in range(MOD_ROWS):
            o_ref[r, :, c0:c0 + cols] += jnp.sum(acc[r], axis=0, keepdims=True)


def _ada_mods(c_ctx, c, w_ada, b_ada):
    tk = 256
    n = 6 * D_MODEL
    c_cols = jnp.concatenate([c_ctx[:, None], c.T], 1)
    return pl.pallas_call(
        _ada_kernel,
        out_shape=jax.ShapeDtypeStruct((DEPTH, MOD_ROWS, 1, n), F32),
        grid=(DEPTH, D_MODEL // tk),
        in_specs=[
            pl.BlockSpec((tk, MOD_ROWS), lambda l, k: (k, 0)),
            pl.BlockSpec((None, tk, n), lambda l, k: (l, k, 0)),
            pl.BlockSpec((None, 1, n), lambda l, k: (l, 0, 0)),
        ],
        out_specs=pl.BlockSpec((None, MOD_ROWS, 1, n), lambda l, k: (l, 0, 0, 0)),
        compiler_params=_params("parallel", "arbitrary"),
        name="ada_mods",
    )(c_cols, w_ada, b_ada.reshape(DEPTH, 1, n))


def _mod_spec(layer, row_of_step):
    return pl.BlockSpec((None, None, 1, 6 * D_MODEL), lambda *g: (layer, row_of_step(*g), 0, 0))


def _lnmod_kernel(*refs):
    x_ref, m_ref, h_ref = refs[-3:]
    y = _layer_norm(x_ref[...])
    s1 = m_ref[:, 0:D_MODEL]
    sc1 = m_ref[:, D_MODEL:2 * D_MODEL]
    h_ref[...] = (y * (1.0 + sc1) + s1).astype(h_ref.dtype)


def _ln_mod(x_group, prev_out, mods, layer, row0):
    tm = 512
    row = _mod_row(tm)
    tile0 = row0 // tm
    in_specs = [
        pl.BlockSpec((tm, D_MODEL), lambda i: (i, 0)),
        _mod_spec(layer, lambda i: row(tile0 + i)),
    ]
    args = [x_group, mods]
    if prev_out is not None:
        in_specs = [pl.BlockSpec(memory_space=pl.ANY)] + in_specs
        args = [prev_out] + args
    return pl.pallas_call(
        _lnmod_kernel,
        out_shape=jax.ShapeDtypeStruct((T_ALL, D_MODEL), BF16),
        grid=(x_group.shape[0] // tm,),
        in_specs=in_specs,
        out_specs=pl.BlockSpec((tm, D_MODEL), lambda i: (tile0 + i, 0)),
        input_output_aliases={} if prev_out is None else {0: 0},
        compiler_params=_params("parallel"),
        name="ln_mod",
    )(*args)


def _proj_kernel(h_ref, w_ref, o_ref, wb_ref, *, gate):
    @pl.when(pl.program_id(1) == 0)
    def _():
        wb_ref[...] = w_ref[...].T.astype(BF16)

    if not gate:
        o_ref[...] = jnp.dot(h_ref[...], wb_ref[...], preferred_element_type=F32).astype(o_ref.dtype)
        return
    sub = 2 * LANE
    for c0 in range(0, o_ref.shape[1], sub):
        r = jnp.dot(h_ref[...], wb_ref[:, c0:c0 + sub], preferred_element_type=F32)
        rb = r.astype(o_ref.dtype)
        o_ref[:, c0:c0 + sub] = 0.5 * jnp.tanh(0.5 * rb) + 0.5


def _in_proj(h, w_t, layer, col0, n_cols, tn, out_dtype, gate):
    tm = 2048
    return pl.pallas_call(
        functools.partial(_proj_kernel, gate=gate),
        out_shape=jax.ShapeDtypeStruct((T_ALL, n_cols), out_dtype),
        grid=(n_cols // tn, T_ALL // tm),
        in_specs=[
            pl.BlockSpec((tm, D_MODEL), lambda j, i: (i, 0)),
            pl.BlockSpec((pl.Element(tn), pl.Element(D_MODEL)),
                         lambda j, i: (pl.multiple_of(layer * IN_COLS + col0 + j * tn, 8), 0)),
        ],
        out_specs=pl.BlockSpec((tm, tn), lambda j, i: (i, j)),
        scratch_shapes=[pltpu.VMEM((D_MODEL, tn), BF16)],
        compiler_params=_params("parallel", "arbitrary"),
        name="gate_proj" if gate else "in_proj",
    )(h, w_t)


def _hy_filter_kernel(z_ref, dl_ref, w1_ref, b1_ref, w2_ref, b2_ref, w3_ref, fwd_ref,
                      kre_ref, kim_ref, *, L):
    z = z_ref[...]
    a = jnp.sin(_dot_split(z, w1_ref[...]) + b1_ref[...])
    a = jnp.sin(_dot_split(a, w2_ref[...]) + b2_ref[...])
    h = _dot_split(a, w3_ref[...])
    decay = jnp.exp(-z[:, 0:1] * dl_ref[...])
    not_first = lax.broadcasted_iota(jnp.int32, (L, HY_W), 0) > 0
    sums, diffs = [], []
    for o in range(2):
        fw = h[:, (2 * o) * HY_W:(2 * o + 1) * HY_W] * decay
        bw = jnp.where(not_first, h[:, (2 * o + 1) * HY_W:(2 * o + 2) * HY_W] * decay, 0.0)
        sums.append(fw + bw)
        diffs.append(fw - bw)
    p = _dot(fwd_ref[...], jnp.concatenate(sums, 1))
    q = _dot(fwd_ref[L:2 * L, :], jnp.concatenate(diffs, 1))
    kre_ref[...] = p[0:L]
    first = lax.broadcasted_iota(jnp.int32, (L, 2 * HY_W), 0) == 0
    kim_ref[...] = jnp.where(first, p[L:L + 1], q)


def _hy_filters(L, w1p, b1, w2, b2, w3, fwd):
    zemb, deltas = _hyena_embedding(L)
    out = jax.ShapeDtypeStruct((L, 2 * HY_W), F32)
    return pl.pallas_call(
        functools.partial(_hy_filter_kernel, L=L),
        out_shape=(out, out),
        compiler_params=pltpu.CompilerParams(vmem_limit_bytes=VMEM_LIMIT),
        name=f"hy_filters_{L}",
    )(jnp.asarray(zemb), jnp.asarray(deltas), w1p, b1, w2, b2, w3, fwd)


def _group_step(ctx_body, lat_body):
    i = pl.program_id(0)
    pl.when(i < CTX_STEPS)(ctx_body)
    pl.when(i >= CTX_STEPS)(lat_body)


def _lat_index(i):
    return jnp.maximum(i - CTX_STEPS, 0)


def _hyena_kernel(hy_ref, cw_ref, cb_ref, bias_ref, kre_c, kim_c, fwd_c, inv_c,
                  kre_l, kim_l, fwd_l, inv_l, o_ref):
    _group_step(
        lambda: _hyena_body(hy_ref, cw_ref, cb_ref, bias_ref, kre_c, kim_c, fwd_c, inv_c, o_ref,
                            SEQ, CTX_SEQS_PER_STEP),
        lambda: _hyena_body(hy_ref, cw_ref, cb_ref, bias_ref, kre_l, kim_l, fwd_l, inv_l, o_ref,
                            DEC_SEQ, 1))


def _hyena_body(hy_ref, cw_ref, cb_ref, bias_ref, kre_ref, kim_ref, fwd_ref, inv_ref, o_ref, L, seqs):
    first = lax.broadcasted_iota(jnp.int32, (L, HY_W), 0) == 0

    def long_conv(u, o):
        uf = _dot(fwd_ref[...], u)
        ure, uim = uf[0:L], uf[L:2 * L]
        kre = kre_ref[:, o * HY_W:(o + 1) * HY_W]
        kim = kim_ref[:, o * HY_W:(o + 1) * HY_W]
        yre = jnp.where(first, ure * kre, ure * kre - uim * kim)
        yim = jnp.where(first, uim * kim, ure * kim + uim * kre)
        y = _dot(inv_ref[...], jnp.concatenate([yre, yim], 0))
        return y + u * bias_ref[o:o + 1, :]

    for g in range(seqs):
        sl = slice(g * L, (g + 1) * L)
        x = hy_ref[sl, :].astype(F32)
        rows = lax.broadcasted_iota(jnp.int32, x.shape, 0)
        prev = jnp.where(rows == 0, 0.0, pltpu.roll(x, 1, axis=0))
        nxt = jnp.where(rows == L - 1, 0.0, pltpu.roll(x, L - 1, axis=0))
        z = prev * cw_ref[0:1, :] + x * cw_ref[1:2, :] + nxt * cw_ref[2:3, :] + cb_ref[...]
        v, x1, x2 = z[:, 0:HY_W], z[:, HY_W:2 * HY_W], z[:, 2 * HY_W:3 * HY_W]
        u = x1 * long_conv(v, 0)
        o_ref[sl, :] = (x2 * long_conv(u, 1)).astype(o_ref.dtype)


def _const_spec(shape):
    return pl.BlockSpec(shape, lambda i: (0,) * len(shape))


def _hyena(z_main, layer, conv_w, conv_b, bias, filters, dft):
    tables, table_specs = [], []
    for L in (SEQ, DEC_SEQ):
        tables += [*filters[L], *dft[L]]
        table_specs += [_const_spec((L, 2 * HY_W)), _const_spec((L, 2 * HY_W)),
                        _const_spec((2 * L, L)), _const_spec((L, 2 * L))]
    return pl.pallas_call(
        _hyena_kernel,
        out_shape=jax.ShapeDtypeStruct((T_ALL, HY_W), BF16),
        grid=(MIXER_STEPS,),
        in_specs=[
            pl.BlockSpec((STEP_ROWS, 3 * HY_W), lambda i: (i, 0)),
            pl.BlockSpec((None, 3, 3 * HY_W), lambda i: (layer, 0, 0)),
            pl.BlockSpec((None, 1, 3 * HY_W), lambda i: (layer, 0, 0)),
            pl.BlockSpec((None, 2, HY_W), lambda i: (layer, 0, 0)),
        ] + table_specs,
        out_specs=pl.BlockSpec((STEP_ROWS, HY_W), lambda i: (i, 0)),
        compiler_params=_params("parallel"),
        name="hyena",
    )(z_main, conv_w, conv_b.reshape(DEPTH, 1, 3 * HY_W), bias, *tables)


def _win_masks():
    lane = lax.broadcasted_iota(jnp.int32, (1, LANE), 1)
    return lane < WIN_HD, lane >= WIN_HD


def _win_head_operands(q, k, v, h):
    lo_mask, hi_mask = _win_masks()
    col = h // 2
    lo = h % 2 == 0
    q128 = jnp.where(lo_mask if lo else hi_mask, q[:, col * LANE:(col + 1) * LANE], 0.0)
    swap = h in (1, 2)
    if swap:
        k = pltpu.roll(k, WIN_HD, axis=1)
        v = pltpu.roll(v, WIN_HD, axis=1)
    return q128, k, v, lo


def _win_kernel(sink_ref, q_ref, kv_ref, ck_ref, cv_ref, c_ref, su_ref, sd_ref, o_ref, *, layer):
    _group_step(
        lambda: _win_ctx_body(sink_ref, q_ref, kv_ref, o_ref, layer),
        lambda: _win_lat_body(sink_ref, q_ref, kv_ref, ck_ref, cv_ref, c_ref, su_ref, sd_ref, o_ref, layer))


def _win_ctx_body(sink_ref, q_ref, kv_ref, o_ref, layer):
    lo_mask, hi_mask = _win_masks()
    qscale = WIN_HD ** -0.5 * LOG2E
    for g in range(CTX_SEQS_PER_STEP):
        sl = slice(g * SEQ, (g + 1) * SEQ)
        q = q_ref[sl, :].astype(F32) * qscale
        k = kv_ref[sl, 0:LANE].astype(F32)
        v = kv_ref[sl, LANE:2 * LANE].astype(F32)
        cols = []
        for col in range(2):
            acc = None
            for h in (2 * col, 2 * col + 1):
                q128, kk, vv, lo = _win_head_operands(q, k, v, h)
                s = _dot_nt(q128, kk)
                sink = sink_ref[layer, h] * LOG2E
                m = jnp.maximum(jnp.max(s, -1, keepdims=True), sink)
                p = jnp.exp2(s - m)
                den = jnp.sum(p, -1, keepdims=True) + jnp.exp2(sink - m)
                o = _dot(p, vv) / den
                o = jnp.where(lo_mask if lo else hi_mask, o, 0.0)
                acc = o if acc is None else acc + o
            cols.append(acc)
        o_ref[sl, :] = jnp.concatenate(cols, 1).astype(o_ref.dtype)


def _win_lat_body(sink_ref, q_ref, kv_ref, ck_ref, cv_ref, c_ref, su_ref, sd_ref, o_ref, layer):
    L = DEC_SEQ
    half = WIN_HD // 2
    c, su, sd = c_ref[...], su_ref[...], sd_ref[...]
    qscale = WIN_HD ** -0.5 * LOG2E
    q = jnp.concatenate(
        [_rope128(q_ref[:, i * LANE:(i + 1) * LANE].astype(F32), c, su, sd, half) for i in range(2)],
        1) * qscale
    k = _rope128(kv_ref[:, 0:LANE].astype(F32), c, su, sd, half)
    v = kv_ref[:, LANE:2 * LANE].astype(F32)
    ck = ck_ref[...]
    cv = cv_ref[...]
    lo_mask, hi_mask = _win_masks()
    nb = L // CHUNK
    assert WINDOW == CHUNK
    rr = lax.broadcasted_iota(jnp.int32, (CHUNK, CHUNK), 0)
    cc = lax.broadcasted_iota(jnp.int32, (CHUNK, CHUNK), 1)
    band = {-1: jnp.where(cc >= rr, 0.0, NEG), 0: jnp.zeros((CHUNK, CHUNK), F32),
            1: jnp.where(cc <= rr, 0.0, NEG)}
    cols = []
    for col in range(2):
        acc_blocks = [None] * nb
        for h in (2 * col, 2 * col + 1):
            q128, kk, vv, lo = _win_head_operands(q, k, v, h)
            _, ckk, cvv, _ = _win_head_operands(q, ck, cv, h)
            sink = sink_ref[layer, h] * LOG2E
            for n in range(nb):
                blocks = [d for d in (-1, 0, 1) if 0 <= n + d < nb]
                k0 = (n + blocks[0]) * CHUNK
                k1 = (n + blocks[-1] + 1) * CHUNK
                qn = q128[n * CHUNK:(n + 1) * CHUNK]
                s_loc = _dot_nt(qn, kk[k0:k1]) + jnp.concatenate([band[d] for d in blocks], 1)
                s_ctx = _dot_nt(qn, ckk)
                m = jnp.maximum(jnp.maximum(jnp.max(s_loc, -1, keepdims=True),
                                            jnp.max(s_ctx, -1, keepdims=True)), sink)
                p_loc = jnp.exp2(s_loc - m)
                p_ctx = jnp.exp2(s_ctx - m)
                den = (jnp.sum(p_loc, -1, keepdims=True) + jnp.sum(p_ctx, -1, keepdims=True)
                       + jnp.exp2(sink - m))
                o = (_dot(p_loc, vv[k0:k1]) + _dot(p_ctx, cvv)) / den
                o = jnp.where(lo_mask if lo else hi_mask, o, 0.0)
                acc_blocks[n] = o if acc_blocks[n] is None else acc_blocks[n] + o
        cols.append(jnp.concatenate(acc_blocks, 0))
    o_ref[...] = jnp.concatenate(cols, 1).astype(o_ref.dtype)


def _win(z_main, sink, cache_k, cache_v, layer):
    c, su, sd, _ = _rope_tables(DEC_SEQ, WIN_HD, LANE)
    tab = _const_spec((DEC_SEQ, LANE))
    cache = pl.BlockSpec((None, None, PAST_LEN, LANE), lambda i: (_lat_index(i), layer, 0, 0))
    return pl.pallas_call(
        functools.partial(_win_kernel, layer=layer),
        out_shape=jax.ShapeDtypeStruct((T_ALL, WIN_HEADS * WIN_HD), BF16),
        grid=(MIXER_STEPS,),
        in_specs=[
            pl.BlockSpec(memory_space=pltpu.SMEM),
            pl.BlockSpec((STEP_ROWS, 256), lambda i: (i, COL_WQ // 256)),
            pl.BlockSpec((STEP_ROWS, 256), lambda i: (i, COL_WK // 256)),
            cache, cache, tab, tab, tab,
        ],
        out_specs=pl.BlockSpec((STEP_ROWS, 256), lambda i: (i, 0)),
        compiler_params=_params("parallel"),
        name="win",
    )(sink, z_main, z_main, cache_k, cache_v, jnp.asarray(c), jnp.asarray(su), jnp.asarray(sd))


def _ret_kernel(*refs, layer):
    _group_step(lambda: _ret_body(*refs, L=SEQ, layer=layer, ctx=True),
                lambda: _ret_body(*refs, L=DEC_SEQ, layer=layer, ctx=False))


def _ret_body(df_ref, db_ref, q_ref, k_ref, v0_ref, v1_ref, g0_ref, g1_ref, s0f_ref, s0b_ref,
              o_ref, sf_out, sb_out, s_ref, cross_ref, *, L, layer, ctx):
    seqs = STEP_ROWS // L
    if not ctx:
        sf_out[...] = jnp.zeros_like(sf_out)
        sb_out[...] = jnp.zeros_like(sb_out)
    C = CHUNK
    nc = L // C
    H = RET_HEADS
    qw = H * RET_DK
    vw = H * RET_DV

    def lane_table(width, per_head, fn):
        pos = lax.broadcasted_iota(jnp.int32, (C, per_head), 0).astype(F32)
        return jnp.concatenate([fn(h, pos) for h in range(H)], 1)

    def log_gamma(ref, h):
        d = jnp.full((1, 1), ref[layer, h], F32)
        return jnp.log(jax.nn.sigmoid(d))

    lgf = [log_gamma(df_ref, h) for h in range(H)]
    lgb = [log_gamma(db_ref, h) for h in range(H)]

    def tables(lg, reverse):
        if reverse:
            dq = lane_table(vw, RET_DV, lambda h, pos: jnp.exp((C - pos) * lg[h]))
            dk = lane_table(qw, RET_DK, lambda h, pos: jnp.exp(pos * lg[h]))
        else:
            dq = lane_table(vw, RET_DV, lambda h, pos: jnp.exp((pos + 1.0) * lg[h]))
            dk = lane_table(qw, RET_DK, lambda h, pos: jnp.exp((C - 1.0 - pos) * lg[h]))
        dc = jnp.concatenate([jnp.broadcast_to(jnp.exp(C * lg[h]), (1, RET_DV)) for h in range(H)], 1)
        return dq, dk, dc

    tab_f = tables(lgf, False)
    tab_b = tables(lgb, True)
    ii = lax.broadcasted_iota(jnp.int32, (C, C), 0)
    jj = lax.broadcasted_iota(jnp.int32, (C, C), 1)
    diff = (ii - jj).astype(F32)
    dmats = [jnp.where(diff >= 0, jnp.exp(jnp.maximum(diff, 0.0) * lgf[h]), 0.0)
             + jnp.where(diff <= 0, jnp.exp(jnp.maximum(-diff, 0.0) * lgb[h]), 0.0) for h in range(H)]
    dmat_stack = jnp.concatenate(dmats, 0)
    lane_q = lax.broadcasted_iota(jnp.int32, (1, qw), 1) // RET_DK

    srow = lax.broadcasted_iota(jnp.int32, (qw, vw), 0) // RET_DK
    scol = lax.broadcasted_iota(jnp.int32, (qw, vw), 1) // RET_DV
    diag = srow == scol

    for g in range(seqs):
        base = g * L
        rows_all = slice(base, base + L)
        q_all = q_ref[rows_all, :].astype(F32)
        k_all = k_ref[rows_all, :].astype(F32) * (RET_DK ** -0.5)
        v_all = jnp.concatenate([v0_ref[rows_all, :], v1_ref[rows_all, :]], 1).astype(F32)
        g_all = jnp.concatenate([g0_ref[rows_all, :], g1_ref[rows_all, :]], 1).astype(F32)

        def scan(tabs, reverse, s0_ref, s_out):
            dq, dk, dc = tabs
            if s0_ref is not None:
                s_ref[g] = jnp.zeros((qw, vw), F32)
                for h in range(H):
                    s_ref[g, h * RET_DK:(h + 1) * RET_DK, h * RET_DV:(h + 1) * RET_DV] = s0_ref[h]
            order = range(nc - 1, -1, -1) if reverse else range(nc)
            for step, ci in enumerate(order):
                sl = slice(ci * C, (ci + 1) * C)
                rs = slice(base + ci * C, base + (ci + 1) * C)
                qc, kc, vc = q_all[sl], k_all[sl], v_all[sl]
                upd = jnp.where(diag, _dot_tn(kc * dk, vc), 0.0)
                if s0_ref is None and step == 0:
                    if not reverse:
                        cross_ref[rs, :] = jnp.zeros((C, vw), F32)
                    s_ref[g] = upd
                    continue
                st = s_ref[g]
                cross = _dot(qc, st) * dq
                if reverse:
                    cross_ref[rs, :] = cross_ref[rs, :] + cross
                else:
                    cross_ref[rs, :] = cross
                s_ref[g] = st * dc + upd
            if s_out is not None:
                for h in range(H):
                    s_out[g, h] = s_ref[g, h * RET_DK:(h + 1) * RET_DK, h * RET_DV:(h + 1) * RET_DV]

        scan(tab_f, False, None if ctx else s0f_ref, sf_out if ctx else None)
        scan(tab_b, True, None if ctx else s0b_ref, sb_out if ctx else None)

        for ci in range(nc):
            sl = slice(ci * C, (ci + 1) * C)
            rs = slice(base + ci * C, base + (ci + 1) * C)
            q_stack = jnp.concatenate([jnp.where(lane_q == h, q_all[sl], 0.0) for h in range(H)], 0)
            att = _dot_nt(q_stack, k_all[sl]) * dmat_stack
            ov = _dot(att, v_all[sl])
            for h in range(H):
                hv = slice(h * RET_DV, (h + 1) * RET_DV)
                o = ov[h * C:(h + 1) * C, hv] + cross_ref[rs, hv]
                gt = g_all[sl, hv]
                o_ref[rs, hv] = ((gt * jax.nn.sigmoid(gt)) * _layer_norm(o)).astype(o_ref.dtype)


def _retention(z_main, dec_f, dec_b, s0f, s0b, layer):
    def zcol(col):
        return pl.BlockSpec((STEP_ROWS, 256), lambda i: (i, col // 256))

    smem = pl.BlockSpec(memory_space=pltpu.SMEM)
    z_specs = [zcol(COL_RQ), zcol(COL_RK), zcol(COL_RV), zcol(COL_RV + 256),
               zcol(COL_RG), zcol(COL_RG + 256)]
    s0_spec = pl.BlockSpec((None, None, RET_HEADS, RET_DK, RET_DV),
                           lambda i: (_lat_index(i), layer, 0, 0, 0))
    st_shape = jax.ShapeDtypeStruct((MIXER_STEPS * CTX_SEQS_PER_STEP, RET_HEADS, RET_DK, RET_DV), F32)
    st_spec = pl.BlockSpec((CTX_SEQS_PER_STEP, RET_HEADS, RET_DK, RET_DV), lambda i: (i, 0, 0, 0))
    return pl.pallas_call(
        functools.partial(_ret_kernel, layer=layer),
        out_shape=(jax.ShapeDtypeStruct((T_ALL, RET_HEADS * RET_DV), BF16), st_shape, st_shape),
        grid=(MIXER_STEPS,),
        in_specs=[smem, smem] + z_specs + [s0_spec, s0_spec],
        out_specs=(pl.BlockSpec((STEP_ROWS, RET_HEADS * RET_DV), lambda i: (i, 0)), st_spec, st_spec),
        scratch_shapes=[pltpu.VMEM((CTX_SEQS_PER_STEP, RET_HEADS * RET_DK, RET_HEADS * RET_DV), F32),
                        pltpu.VMEM((STEP_ROWS, RET_HEADS * RET_DV), F32)],
        compiler_params=_params("parallel"),
        name="retention",
    )(dec_f, dec_b, *([z_main] * 6), s0f, s0b)


def _rms_norm(x, g):
    return x * lax.rsqrt(jnp.mean(x * x, -1, keepdims=True) + RMS_EPS) * g


def _mla_keys(kn, kr):
    lane_r = lax.broadcasted_iota(jnp.int32, (1, LANE), 1)
    return jnp.concatenate([kn, jnp.where(lane_r < MLA_ROPE, kr, 0.0)], 1).astype(BF16)


def _mla_attend(qn, qr, k_cat, vv, o_ref, row0):
    qscale = (MLA_NOPE + MLA_ROPE) ** -0.5 * LOG2E
    lane_n = lax.broadcasted_iota(jnp.int32, (1, MLA_HEADS * MLA_NOPE), 1) // MLA_NOPE
    lane_r = lax.broadcasted_iota(jnp.int32, (1, LANE), 1)
    qn = qn * qscale
    qr = qr * qscale
    lq = qn.shape[0]
    heads = []
    for h in range(MLA_HEADS):
        qnh = jnp.where(lane_n == h, qn, 0.0)
        qrh = qr if h == 0 else pltpu.roll(qr, LANE - h * MLA_ROPE, axis=1)
        qrh = jnp.where(lane_r < MLA_ROPE, qrh, 0.0)
        heads.append(jnp.concatenate([qnh, qrh], 1).astype(BF16))
    s = _dot_nt(jnp.concatenate(heads, 0), k_cat)
    m = jnp.max(s, -1, keepdims=True)
    p = jnp.exp2(s - m)
    den = jnp.sum(p, -1, keepdims=True)
    o = _dot(p, vv) / den
    acc = None
    for h in range(MLA_HEADS):
        oh = jnp.where(lane_n == h, o[h * lq:(h + 1) * lq], 0.0)
        acc = oh if acc is None else acc + oh
    o_ref[row0:row0 + lq, :] = acc.astype(o_ref.dtype)


MLA_QN = MLA_HEADS * MLA_NOPE


def _mla_kernel(cq_ref, ckv_ref, kr_ref, cckv_ref, ckr_ref, c_ref, su_ref, sd_ref,
                qg_ref, kg_ref, wq_ref, wkv_ref, o_ref, ckvn_ref):
    weights = (qg_ref, kg_ref, wq_ref, wkv_ref)
    _group_step(
        lambda: _mla_ctx_body(cq_ref, ckv_ref, kr_ref, *weights, o_ref, ckvn_ref),
        lambda: _mla_lat_body(cq_ref, ckv_ref, kr_ref, cckv_ref, ckr_ref, c_ref, su_ref, sd_ref,
                              *weights, o_ref, ckvn_ref))


def _mla_ctx_body(cq_ref, ckv_ref, kr_ref, qg_ref, kg_ref, wq_ref, wkv_ref, o_ref, ckvn_ref):
    q = _dot(_rms_norm(cq_ref[...].astype(F32), qg_ref[...]), wq_ref[...])
    qn, qr = q[:, 0:MLA_QN], q[:, MLA_QN:]
    ckvn = _rms_norm(ckv_ref[...].astype(F32), kg_ref[...])
    ckvn_ref[...] = ckvn
    kv = _dot(ckvn, wkv_ref[...])
    k_cat = _mla_keys(kv[:, 0:MLA_QN], kr_ref[...].astype(F32))
    vv = kv[:, MLA_QN:].astype(BF16)
    for g in range(CTX_SEQS_PER_STEP):
        sl = slice(g * SEQ, (g + 1) * SEQ)
        _mla_attend(qn[sl], qr[sl], k_cat[sl], vv[sl], o_ref, g * SEQ)


def _mla_lat_body(cq_ref, ckv_ref, kr_ref, cckv_ref, ckr_ref, c_ref, su_ref, sd_ref,
                  qg_ref, kg_ref, wq_ref, wkv_ref, o_ref, ckvn_ref):
    half = MLA_ROPE // 2
    c, su, sd = c_ref[...], su_ref[...], sd_ref[...]
    q = _dot(_rms_norm(cq_ref[...].astype(F32), qg_ref[...]), wq_ref[...])
    qn = q[:, 0:MLA_QN]
    qr = _rope128(q[:, MLA_QN:], c, su, sd, half)
    ckvn = _rms_norm(ckv_ref[...].astype(F32), kg_ref[...])
    ckvn_ref[...] = ckvn
    ckv_all = jnp.concatenate([ckvn, cckv_ref[...]], 0)
    kv = _dot(ckv_all, wkv_ref[...])
    vv = kv[:, MLA_QN:].astype(BF16)
    kr = jnp.concatenate([_rope128(kr_ref[...].astype(F32), c, su, sd, half), ckr_ref[...]], 0)
    k_cat = _mla_keys(kv[:, 0:MLA_QN], kr)
    rows_per_call = 256
    for n in range(DEC_SEQ // rows_per_call):
        rows = slice(n * rows_per_call, (n + 1) * rows_per_call)
        _mla_attend(qn[rows], qr[rows], k_cat, vv, o_ref, n * rows_per_call)


def _mla(z_main, cache_ckv, cache_kr_pad, weights, layer):
    c, su, sd, _ = _rope_tables(DEC_SEQ, MLA_ROPE, LANE)
    tab = _const_spec((DEC_SEQ, LANE))
    cache = pl.BlockSpec((None, None, PAST_LEN, LANE), lambda i: (_lat_index(i), layer, 0, 0))

    def weight(*shape):
        return pl.BlockSpec((None,) + shape, lambda i: (layer, 0, 0))

    return pl.pallas_call(
        _mla_kernel,
        out_shape=(jax.ShapeDtypeStruct((T_ALL, MLA_HEADS * MLA_V), BF16),
                   jax.ShapeDtypeStruct((T_ALL, MLA_KV_LORA), F32)),
        grid=(MIXER_STEPS,),
        in_specs=[
            pl.BlockSpec((STEP_ROWS, 256), lambda i: (i, COL_CQ // 256)),
            pl.BlockSpec((STEP_ROWS, LANE), lambda i: (i, COL_CKV // LANE)),
            pl.BlockSpec((STEP_ROWS, LANE), lambda i: (i, COL_KROPE // LANE)),
            cache, cache, tab, tab, tab,
            weight(1, MLA_Q_LORA), weight(1, MLA_KV_LORA),
            weight(MLA_Q_LORA, MLA_QN + MLA_HEADS * MLA_ROPE),
            weight(MLA_KV_LORA, MLA_QN + MLA_HEADS * MLA_V),
        ],
        out_specs=(pl.BlockSpec((STEP_ROWS, 256), lambda i: (i, 0)),
                   pl.BlockSpec((STEP_ROWS, MLA_KV_LORA), lambda i: (i, 0))),
        compiler_params=_params("parallel"),
        name="mla",
    )(z_main, z_main, z_main, cache_ckv, cache_kr_pad,
      jnp.asarray(c), jnp.asarray(su), jnp.asarray(sd), *weights)


def _route(logits_t, rb):
    scores = jax.nn.sigmoid(logits_t)
    biased = scores + rb
    sc = [scores[e:e + 1, :] for e in range(N_EXPERTS)]
    bi = [biased[e:e + 1, :] for e in range(N_EXPERTS)]
    epg = EXPERTS_PER_GROUP
    gsum = []
    for g in range(N_GROUPS):
        v = bi[g * epg:(g + 1) * epg]
        best = None
        for a in range(epg):
            for b in range(a + 1, epg):
                pair = v[a] + v[b]
                best = pair if best is None else jnp.maximum(best, pair)
        gsum.append(best)
    combine = []
    sel = []
    for g in range(N_GROUPS):
        is_best = None
        for g2 in range(N_GROUPS):
            if g2 == g:
                continue
            c = gsum[g] > gsum[g2] if g2 < g else gsum[g] >= gsum[g2]
            is_best = c if is_best is None else jnp.logical_and(is_best, c)
        for a in range(epg):
            e = g * epg + a
            rank = jnp.zeros_like(bi[e])
            for b in range(epg):
                if b == a:
                    continue
                e2 = g * epg + b
                ahead = bi[e2] >= bi[e] if b < a else bi[e2] > bi[e]
                rank = rank + jnp.where(ahead, 1.0, 0.0)
            sel.append(jnp.logical_and(is_best, rank < 2.0))
    wsum = None
    for e in range(N_EXPERTS):
        w = jnp.where(sel[e], sc[e], 0.0)
        wsum = w if wsum is None else wsum + w
    for e in range(N_EXPERTS):
        combine.append(jnp.where(sel[e], ROUTE_SCALE * sc[e] / wsum, 0.0))
    return jnp.concatenate(combine, 0)


MERGE_BRANCH_ROWS = (HY_W, WIN_HEADS * WIN_HD, RET_HEADS * RET_DV, MLA_HEADS * MLA_V)
MERGE_ROWS = sum(MERGE_BRANCH_ROWS) + D_MODEL


def _merge_kernel(ya_ref, yb_ref, yc_ref, yd_ref, gt_ref, xc_ref, xl_ref, m_ref,
                  w_ref, g_ref, b_ref, rw_ref, rb_ref,
                  x1_ref, h2_ref, cmb_ref, *, ctx_tiles, sub_rows):
    D = D_MODEL
    rw = rw_ref[...]
    rw_hi = rw.astype(BF16)
    rw_lo = (rw - rw_hi.astype(F32)).astype(BF16)
    g1 = m_ref[:, 2 * D:3 * D]
    s2 = m_ref[:, 3 * D:4 * D]
    sc2 = m_ref[:, 4 * D:5 * D]
    is_ctx = pl.program_id(0) < ctx_tiles
    offs = np.cumsum((0,) + MERGE_BRANCH_ROWS)
    branches = tuple((y_ref, slice(int(offs[i]), int(offs[i + 1])))
                     for i, y_ref in enumerate((ya_ref, yb_ref, yc_ref, yd_ref)))
    w_out_rows = slice(int(offs[-1]), MERGE_ROWS)
    for r0 in range(0, x1_ref.shape[0], sub_rows):
        rows = slice(r0, r0 + sub_rows)
        merged = None
        for i, (y_ref, w_rows) in enumerate(branches):
            t = gt_ref[rows, i * D:(i + 1) * D] * jnp.dot(
                y_ref[rows, :], w_ref[w_rows, :], preferred_element_type=F32).astype(BF16)
            merged = t if merged is None else merged + t
        out1 = jnp.dot(merged, w_ref[w_out_rows, :], preferred_element_type=F32)
        x = jnp.where(is_ctx, xc_ref[rows, :], xl_ref[rows, :])
        x1 = _layer_norm(ALPHA * x + g1 * out1) * g_ref[...] + b_ref[...]
        x1_ref[rows, :] = x1
        h2 = _layer_norm(x1) * (1.0 + sc2) + s2
        h2_hi = h2.astype(BF16)
        h2_ref[rows, :] = h2_hi
        h2_lo = (h2 - h2_hi.astype(F32)).astype(BF16)
        logits = (jnp.dot(h2_hi, rw_hi, preferred_element_type=F32)
                  + (jnp.dot(h2_lo, rw_hi, preferred_element_type=F32)
                     + jnp.dot(h2_hi, rw_lo, preferred_element_type=F32)))
        cmb_ref[:, rows] = _route(logits.T[0:N_EXPERTS], rb_ref[...])


def _merge(ya, yb, yc, yd, gates, x_ctx, x_lat, mods, w_merge, ln1_g, ln1_b,
           router_w, router_b, layer):
    tm = 512
    row = _mod_row(tm)
    D = D_MODEL
    ctx_tiles = T_CTX // tm

    def tile(w):
        return pl.BlockSpec((tm, w), lambda i: (i, 0))

    def weight(k, n):
        return pl.BlockSpec((None, k, n), lambda i: (layer, 0, 0))

    return pl.pallas_call(
        functools.partial(_merge_kernel, ctx_tiles=ctx_tiles, sub_rows=256),
        out_shape=(jax.ShapeDtypeStruct((T_ALL, D), F32),
                   jax.ShapeDtypeStruct((T_ALL, D), BF16),
                   jax.ShapeDtypeStruct((N_EXPERTS, T_ALL), F32)),
        grid=(T_ALL // tm,),
        in_specs=[
            tile(256), tile(256), tile(512), tile(256), tile(4 * D),
            pl.BlockSpec((tm, D), lambda i: (jnp.minimum(i, ctx_tiles - 1), 0)),
            pl.BlockSpec((tm, D), lambda i: (jnp.maximum(i - ctx_tiles, 0), 0)),
            _mod_spec(layer, row),
            weight(MERGE_ROWS, D), weight(1, D), weight(1, D),
            pl.BlockSpec((D, LANE), lambda i: (0, 0)),
            pl.BlockSpec((N_EXPERTS, 1), lambda i: (0, 0)),
        ],
        out_specs=(tile(D), tile(D), pl.BlockSpec((N_EXPERTS, tm), lambda i: (0, i))),
        compiler_params=_params("parallel"),
        name="merge",
    )(ya, yb, yc, yd, gates, x_ctx, x_lat, mods, w_merge,
      ln1_g.reshape(DEPTH, 1, D), ln1_b.reshape(DEPTH, 1, D), router_w,
      router_b.reshape(N_EXPERTS, 1))


MOE_EXPERTS_PER_STEP = 2


def _moe_kernel(*refs, next_h):
    if next_h:
        (h_ref, c_ref, x1_ref, m_ref, wg_ref, wu_ref, wd_ref, g_ref, b_ref, mn_ref,
         o_ref, hn_ref, acc_ref) = refs[-13:]
    else:
        h_ref, c_ref, x1_ref, m_ref, wg_ref, wu_ref, wd_ref, g_ref, b_ref, o_ref, acc_ref = refs
    eg = pl.program_id(1)

    @pl.when(eg == 0)
    def _():
        acc_ref[...] = jnp.zeros_like(acc_ref)

    h = h_ref[...]
    cmb = c_ref[...]
    lane = lax.broadcasted_iota(jnp.int32, cmb.shape, 1)
    hid = []
    for k in range(MOE_EXPERTS_PER_STEP):
        gate = jnp.dot(h, wg_ref[k].astype(BF16), preferred_element_type=F32)
        up = jnp.dot(h, wu_ref[k].astype(BF16), preferred_element_type=F32)
        e = eg * MOE_EXPERTS_PER_STEP + k
        ce = jnp.sum(jnp.where(lane == e, cmb, 0.0), -1, keepdims=True)
        sig = 0.5 * jnp.tanh(0.5 * gate) + 0.5
        hid.append((gate * sig * (up * ce)).astype(BF16))
    wd = wd_ref[...].reshape(MOE_EXPERTS_PER_STEP * D_EXPERT, D_MODEL).astype(BF16)
    acc_ref[...] += jnp.dot(jnp.concatenate(hid, 1), wd, preferred_element_type=F32)

    @pl.when(eg == N_EXPERTS // MOE_EXPERTS_PER_STEP - 1)
    def _():
        g2 = m_ref[:, 5 * D_MODEL:6 * D_MODEL]
        y = _layer_norm(ALPHA * x1_ref[...] + g2 * acc_ref[...])
        y = y * g_ref[...] + b_ref[...]
        o_ref[...] = y
        if next_h:
            s1 = mn_ref[:, 0:D_MODEL]
            sc1 = mn_ref[:, D_MODEL:2 * D_MODEL]
            hn_ref[...] = (_layer_norm(y) * (1.0 + sc1) + s1).astype(hn_ref.dtype)


def _moe(h2, combine, x1, mods, w_gate, w_up, w_down, ln2_g, ln2_b, layer, row0, n_rows,
         next_h=False, prev_h=None):
    tm = 1024
    row = _mod_row(tm)
    D = D_MODEL
    t0 = row0 // tm
    eps = MOE_EXPERTS_PER_STEP
    mod_spec = _mod_spec(layer, lambda i, e: row(t0 + i))
    in_specs = [
        pl.BlockSpec((tm, D), lambda i, e: (t0 + i, 0)),
        pl.BlockSpec((tm, N_EXPERTS), lambda i, e: (t0 + i, 0)),
        pl.BlockSpec((tm, D), lambda i, e: (t0 + i, 0)),
        mod_spec,
        pl.BlockSpec((None, eps, D, D_EXPERT), lambda i, e: (layer, e, 0, 0)),
        pl.BlockSpec((None, eps, D, D_EXPERT), lambda i, e: (layer, e, 0, 0)),
        pl.BlockSpec((None, eps, D_EXPERT, D), lambda i, e: (layer, e, 0, 0)),
        pl.BlockSpec((None, 1, D), lambda i, e: (layer, 0, 0)),
        pl.BlockSpec((None, 1, D), lambda i, e: (layer, 0, 0)),
    ]
    args = [h2, combine, x1, mods, w_gate, w_up, w_down,
            ln2_g.reshape(DEPTH, 1, D), ln2_b.reshape(DEPTH, 1, D)]
    out_shape = jax.ShapeDtypeStruct((n_rows, D), F32)
    out_specs = pl.BlockSpec((tm, D), lambda i, e: (i, 0))
    aliases = {}
    if next_h:
        in_specs.append(_mod_spec(layer + 1, lambda i, e: row(t0 + i)))
        args.append(mods)
        out_shape = (out_shape, jax.ShapeDtypeStruct((T_ALL, D), BF16))
        out_specs = (out_specs, pl.BlockSpec((tm, D), lambda i, e: (t0 + i, 0)))
        if prev_h is not None:
            in_specs = [pl.BlockSpec(memory_space=pl.ANY)] + in_specs
            args = [prev_h] + args
            aliases = {0: 1}
    return pl.pallas_call(
        functools.partial(_moe_kernel, next_h=next_h),
        out_shape=out_shape,
        grid=(n_rows // tm, N_EXPERTS // eps),
        in_specs=in_specs,
        out_specs=out_specs,
        input_output_aliases=aliases,
        scratch_shapes=[pltpu.VMEM((tm, D), F32)],
        compiler_params=_params("parallel", "arbitrary"),
        name="moe",
    )(*args)


def kernel(x_prompt, x_sample, cache_win_k, cache_win_v, cache_mla_ckv, cache_mla_krope,
           state_ret_fwd, state_ret_bwd, c, c_ctx, w_ada, b_ada, w_in,
           hy_conv_w, hy_conv_b, hy_w1, hy_b1, hy_w2, hy_b2, hy_w3, hy_bias,
           win_sink, ret_decay_fwd, ret_decay_bwd, mla_q_norm, mla_kv_norm, mla_w_uq, mla_w_ukv,
           w_br_a, w_br_b, w_br_c, w_br_d, w_out, ln1_g, ln1_b, ln2_g, ln2_b,
           router_w, router_b, moe_w_gate, moe_w_up, moe_w_down):
    D = D_MODEL
    x_ctx = x_prompt.reshape(T_CTX, D)
    x_lat = x_sample.reshape(T_LAT, D)

    mods = _ada_mods(c_ctx, c, w_ada, b_ada)

    w_in_t = jnp.swapaxes(w_in, 1, 2).reshape(DEPTH * IN_COLS, D)
    cache_k = cache_win_k.reshape(DEC_BATCH, DEPTH, PAST_LEN, WIN_KV_HEADS * WIN_HD)
    cache_v = cache_win_v.reshape(DEC_BATCH, DEPTH, PAST_LEN, WIN_KV_HEADS * WIN_HD)
    cache_kr = jnp.pad(cache_mla_krope, ((0, 0), (0, 0), (0, 0), (0, LANE - MLA_ROPE)))

    uq = mla_w_uq.reshape(DEPTH, MLA_Q_LORA, MLA_HEADS, MLA_NOPE + MLA_ROPE)
    ukv = mla_w_ukv.reshape(DEPTH, MLA_KV_LORA, MLA_HEADS, MLA_NOPE + MLA_V)
    mla_weights = (
        mla_q_norm.reshape(DEPTH, 1, MLA_Q_LORA),
        mla_kv_norm.reshape(DEPTH, 1, MLA_KV_LORA),
        jnp.concatenate([uq[..., :MLA_NOPE].reshape(DEPTH, MLA_Q_LORA, MLA_HEADS * MLA_NOPE),
                         uq[..., MLA_NOPE:].reshape(DEPTH, MLA_Q_LORA, MLA_HEADS * MLA_ROPE)], -1),
        jnp.concatenate([ukv[..., :MLA_NOPE].reshape(DEPTH, MLA_KV_LORA, MLA_HEADS * MLA_NOPE),
                         ukv[..., MLA_NOPE:].reshape(DEPTH, MLA_KV_LORA, MLA_HEADS * MLA_V)], -1),
    )

    hy_w1p = jnp.pad(hy_w1, ((0, 0), (0, LANE - HY_EMB), (0, 0)))
    dft = {}
    for L in (SEQ, DEC_SEQ):
        fwd, inv = _dft_tables(L)
        dft[L] = (jnp.asarray(fwd).astype(BF16), jnp.asarray(inv).astype(BF16))
    router_w_pad = jnp.pad(router_w, ((0, 0), (0, LANE - N_EXPERTS)))
    w_merge = jnp.concatenate([w_br_a, w_br_b, w_br_c, w_br_d, w_out], 1).astype(BF16)

    new_k, new_v, new_ckv, new_kr, new_sf, new_sb = [], [], [], [], [], []
    for l in range(DEPTH):
        if l == 0:
            h = _ln_mod(x_ctx, None, mods, l, 0)
            h = _ln_mod(x_lat, h, mods, l, T_CTX)
        z = _in_proj(h, w_in_t, l, 0, Z_MAIN, Z_MAIN // 2, BF16, gate=False)
        gates = _in_proj(h, w_in_t, l, COL_GATE, 4 * D, D, BF16, gate=True)

        filters = {L: _hy_filters(L, hy_w1p[l], hy_b1[l][None], hy_w2[l], hy_b2[l][None], hy_w3[l],
                                  dft[L][0]) for L in (SEQ, DEC_SEQ)}
        ya = _hyena(z, l, hy_conv_w, hy_conv_b, hy_bias, filters, dft)
        yb = _win(z, win_sink, cache_k, cache_v, l)
        yc, sf, sb = _retention(z, ret_decay_fwd, ret_decay_bwd, state_ret_fwd, state_ret_bwd, l)
        yd, ckvn = _mla(z, cache_mla_ckv, cache_kr, mla_weights, l)

        x1, h2, combine_t = _merge(ya, yb, yc, yd, gates, x_ctx, x_lat, mods, w_merge,
                                   ln1_g, ln1_b, router_w_pad, router_b, l)
        moe_args = (h2, combine_t.T, x1, mods, moe_w_gate, moe_w_up, moe_w_down, ln2_g, ln2_b, l)
        if l + 1 < DEPTH:
            x_ctx, h = _moe(*moe_args, 0, T_CTX, next_h=True)
            x_lat, h = _moe(*moe_args, T_CTX, T_LAT, next_h=True, prev_h=h)
        else:
            x_ctx = _moe(*moe_args, 0, T_CTX)
            x_lat = _moe(*moe_args, T_CTX, T_LAT)

        def ctx_cols(col, width):
            return z[:T_CTX, col:col + width].astype(F32)

        new_k.append(ctx_cols(COL_WK, 128).reshape(BATCH, SEQ, WIN_KV_HEADS, WIN_HD))
        new_v.append(ctx_cols(COL_WV, 128).reshape(BATCH, SEQ, WIN_KV_HEADS, WIN_HD))
        new_ckv.append(ckvn[:T_CTX].reshape(BATCH, SEQ, MLA_KV_LORA))
        new_kr.append(ctx_cols(COL_KROPE, MLA_ROPE).reshape(BATCH, SEQ, MLA_ROPE))
        new_sf.append(sf[:BATCH])
        new_sb.append(sb[:BATCH])

    y_prompt = x_ctx.reshape(BATCH, SEQ, D)
    y_sample = x_lat.reshape(DEC_BATCH, DEC_SEQ, D)
    return (y_prompt, y_sample, jnp.stack(new_k, 1), jnp.stack(new_v, 1), jnp.stack(new_ckv, 1),
            jnp.stack(new_kr, 1), jnp.stack(new_sf, 1), jnp.stack(new_sb, 1))
```

```python
import functools
import math

import numpy as np
import jax
import jax.numpy as jnp
from jax import lax
from jax.experimental import pallas as pl
from jax.experimental.pallas import tpu as pltpu

F32 = jnp.float32
BF16 = jnp.bfloat16

D_MODEL = 1024
BATCH = 16
SEQ = 256
DEPTH = 2
DEC_BATCH = 2
DEC_SEQ = 1024
PAST_LEN = 256
GRID_W = 64
CHUNK = 128
ROPE_BASE = 10000.0
NEG = -1e30
LN_EPS = 1e-5
RMS_EPS = 1e-6
LOG2E = math.log2(math.e)

HY_W = 256
HY_BANDS = 16
HY_EMB = 1 + 2 * HY_BANDS
HY_FFN = 64
HY_FAST_DECAY = 0.3
HY_SLOW_DECAY = 1.5
HY_TARGET = 1e-2

WIN_HEADS = 4
WIN_KV_HEADS = 2
WIN_HD = 64
WINDOW = 128

RET_HEADS = 4
RET_DK = 64
RET_DV = 128

MLA_HEADS = 4
MLA_Q_LORA = 256
MLA_KV_LORA = 128
MLA_NOPE = 64
MLA_ROPE = 32
MLA_V = 64

N_EXPERTS = 16
N_GROUPS = 4
EXPERTS_PER_GROUP = N_EXPERTS // N_GROUPS
D_EXPERT = 256
ROUTE_SCALE = 2.5

ALPHA = (2.0 * DEPTH) ** 0.25

T_CTX = BATCH * SEQ
T_LAT = DEC_BATCH * DEC_SEQ
T_ALL = T_CTX + T_LAT

COL_HY = 0
COL_WQ = 768
COL_WK = 1024
COL_WV = 1152
COL_RQ = 1280
COL_RK = 1536
COL_RV = 1792
COL_RG = 2304
COL_CQ = 2816
COL_CKV = 3072
COL_KROPE = 3200
COL_GATE = 3232
IN_COLS = COL_GATE + 4 * D_MODEL
Z_MAIN = 3328

LANE = 128
STEP_ROWS = 1024
CTX_SEQS_PER_STEP = STEP_ROWS // SEQ
CTX_STEPS = T_CTX // STEP_ROWS
MIXER_STEPS = T_ALL // STEP_ROWS
VMEM_LIMIT = 56 * 1024 * 1024


def _params(*sem):
    return pltpu.CompilerParams(dimension_semantics=sem, vmem_limit_bytes=VMEM_LIMIT)


def _dot(a, b):
    return jnp.dot(a.astype(BF16), b.astype(BF16), preferred_element_type=F32)


def _dot_split(a, b):
    a_hi = a.astype(BF16)
    a_lo = (a - a_hi.astype(F32)).astype(BF16)
    b_hi = b.astype(BF16)
    b_lo = (b - b_hi.astype(F32)).astype(BF16)

    def mm(x, y):
        return jnp.dot(x, y, preferred_element_type=F32)

    return mm(a_hi, b_hi) + (mm(a_lo, b_hi) + mm(a_hi, b_lo))


def _dot_nt(a, b):
    return lax.dot_general(a.astype(BF16), b.astype(BF16), (((1,), (1,)), ((), ())),
                           preferred_element_type=F32)


def _dot_tn(a, b):
    return lax.dot_general(a.astype(BF16), b.astype(BF16), (((0,), (0,)), ((), ())),
                           preferred_element_type=F32)


def _layer_norm(x):
    mu = jnp.mean(x, -1, keepdims=True)
    xc = x - mu
    var = jnp.mean(xc * xc, -1, keepdims=True)
    return xc * lax.rsqrt(var + LN_EPS)


def _mod_row(tile_rows):
    def row(i):
        start = i * tile_rows
        return jnp.where(start < T_CTX, 0, 1 + (start - T_CTX) // DEC_SEQ)
    return row


@functools.lru_cache(maxsize=None)
def _dft_tables(L):
    f = np.arange(L, dtype=np.int64)[:, None]
    s = np.arange(L, dtype=np.int64)[None, :]
    ang = np.pi * ((f * s) % (2 * L)).astype(np.float64) / L
    cos = np.cos(ang)
    sin = np.sin(ang)
    alt = np.where(np.arange(L) % 2 == 0, 1.0, -1.0)
    fwd_im = -sin
    fwd_im[0, :] = alt
    fwd = np.concatenate([cos, fwd_im], 0)
    inv_re = cos.T / L
    inv_re[:, 0] = 1.0 / (2 * L)
    inv_im = -sin.T / L
    inv_im[:, 0] = alt / (2 * L)
    inv = np.concatenate([inv_re, inv_im], 1)
    return fwd.astype(np.float32), inv.astype(np.float32)


@functools.lru_cache(maxsize=None)
def _hyena_embedding(L):
    t01 = np.linspace(0.0, 1.0, L, dtype=np.float64)[:, None]
    bands = np.linspace(1e-4, HY_BANDS - 1, HY_BANDS, dtype=np.float64)
    ang = (2.0 * math.pi / L) * np.arange(L, dtype=np.float64)[:, None] * bands[None, :]
    z = np.concatenate([t01, np.cos(ang), -np.sin(ang)], -1)
    zp = np.zeros((L, LANE), np.float64)
    zp[:, :HY_EMB] = z
    deltas = np.abs(np.linspace(math.log(HY_TARGET) / HY_SLOW_DECAY,
                                math.log(HY_TARGET) / HY_FAST_DECAY, HY_W, dtype=np.float64))
    return zp.astype(np.float32), deltas[None, :].astype(np.float32)


@functools.lru_cache(maxsize=None)
def _rope_tables(L, rot_dim, width):
    rows = L // GRID_W
    n_freq = rot_dim // 4
    half = rot_dim // 2
    inv = ROPE_BASE ** (-np.arange(n_freq, dtype=np.float64) / n_freq)
    pos = np.arange(L)
    row = (pos // GRID_W).astype(np.float64)
    col = (pos % GRID_W).astype(np.float64)
    ang = np.concatenate([row[:, None] * inv, col[:, None] * inv], -1)
    cos, sin = np.cos(ang), np.sin(ang)
    zero = np.zeros_like(sin)
    c = np.tile(np.concatenate([cos, cos], -1), (1, width // rot_dim))
    s_up = np.tile(np.concatenate([-sin, zero], -1), (1, width // rot_dim))
    s_dn = np.tile(np.concatenate([zero, sin], -1), (1, width // rot_dim))
    return c.astype(np.float32), s_up.astype(np.float32), s_dn.astype(np.float32), half


def _rope128(x, c, s_up, s_dn, half):
    up = pltpu.roll(x, LANE - half, axis=1)
    dn = pltpu.roll(x, half, axis=1)
    return x * c + up * s_up + dn * s_dn


MOD_ROWS = 1 + DEC_BATCH


def _ada_kernel(ct_ref, w_ref, b_ref, o_ref):
    @pl.when(pl.program_id(1) == 0)
    def _():
        for r in range(MOD_ROWS):
            o_ref[r] = b_ref[...]

    ct = ct_ref[...]
    s = ct * jax.nn.sigmoid(ct)
    tk, n = w_ref.shape
    rows, cols = 64, 4 * LANE
    s_cols = [[s[k0:k0 + rows, r:r + 1] for k0 in range(0, tk, rows)] for r in range(MOD_ROWS)]
    for c0 in range(0, n, cols):
        acc = [None] * MOD_ROWS
        for ki, k0 in enumerate(range(0, tk, rows)):
            wc = w_ref[k0:k0 + rows, c0:c0 + cols]
            for r in range(MOD_ROWS):
                part = jnp.sum((wc * s_cols[r][ki]).reshape(rows // 8, 8, cols), axis=0)
                acc[r] = part if acc[r] is None else acc[r] + part
        for r in range(MOD_ROWS):
            o_ref[r, :, c0:c0 + cols] += jnp.sum(acc[r], axis=0, keepdims=True)


def _ada_mods(c_ctx, c, w_ada, b_ada):
    tk = 256
    n = 6 * D_MODEL
    c_cols = jnp.concatenate([c_ctx[:, None], c.T], 1)
    return pl.pallas_call(
        _ada_kernel,
        out_shape=jax.ShapeDtypeStruct((DEPTH, MOD_ROWS, 1, n), F32),
        grid=(DEPTH, D_MODEL // tk),
        in_specs=[
            pl.BlockSpec((tk, MOD_ROWS), lambda l, k: (k, 0)),
            pl.BlockSpec((None, tk, n), lambda l, k: (l, k, 0)),
            pl.BlockSpec((None, 1, n), lambda l, k: (l, 0, 0)),
        ],
        out_specs=pl.BlockSpec((None, MOD_ROWS, 1, n), lambda l, k: (l, 0, 0, 0)),
        compiler_params=_params("parallel", "arbitrary"),
        name="ada_mods",
    )(c_cols, w_ada, b_ada.reshape(DEPTH, 1, n))


def _mod_spec(layer, row_of_step):
    return pl.BlockSpec((None, None, 1, 6 * D_MODEL), lambda *g: (layer, row_of_step(*g), 0, 0))


def _lnmod_kernel(xc_ref, xl_ref, m_ref, h_ref, *, ctx_tiles):
    x = jnp.where(pl.program_id(0) < ctx_tiles, xc_ref[...], xl_ref[...])
    y = _layer_norm(x)
    s1 = m_ref[:, 0:D_MODEL]
    sc1 = m_ref[:, D_MODEL:2 * D_MODEL]
    h_ref[...] = (y * (1.0 + sc1) + s1).astype(h_ref.dtype)


def _x_specs(tm, ctx_tiles, lat_block0=0):
    return [pl.BlockSpec((tm, D_MODEL), lambda i: (jnp.minimum(i, ctx_tiles - 1), 0)),
            pl.BlockSpec((tm, D_MODEL), lambda i: (jnp.maximum(i - ctx_tiles, 0) + lat_block0, 0))]


def _ln_mod(x_ctx, x_lat, mods, layer):
    tm = 512
    ctx_tiles = T_CTX // tm
    return pl.pallas_call(
        functools.partial(_lnmod_kernel, ctx_tiles=ctx_tiles),
        out_shape=jax.ShapeDtypeStruct((T_ALL, D_MODEL), BF16),
        grid=(T_ALL // tm,),
        in_specs=_x_specs(tm, ctx_tiles) + [_mod_spec(layer, _mod_row(tm))],
        out_specs=pl.BlockSpec((tm, D_MODEL), lambda i: (i, 0)),
        compiler_params=_params("parallel"),
        name="ln_mod",
    )(x_ctx, x_lat, mods)


def _proj_kernel(h_ref, w_ref, o_ref, wb_ref, *, gate):
    @pl.when(pl.program_id(1) == 0)
    def _():
        wb_ref[...] = w_ref[...].T.astype(BF16)

    if not gate:
        o_ref[...] = jnp.dot(h_ref[...], wb_ref[...], preferred_element_type=F32).astype(o_ref.dtype)
        return
    sub = 2 * LANE
    for c0 in range(0, o_ref.shape[1], sub):
        r = jnp.dot(h_ref[...], wb_ref[:, c0:c0 + sub], preferred_element_type=F32)
        rb = r.astype(o_ref.dtype)
        o_ref[:, c0:c0 + sub] = 0.5 * jnp.tanh(0.5 * rb) + 0.5


def _in_proj(h, w_t, layer, col0, n_cols, tn, out_dtype, gate):
    tm = 2048
    return pl.pallas_call(
        functools.partial(_proj_kernel, gate=gate),
        out_shape=jax.ShapeDtypeStruct((T_ALL, n_cols), out_dtype),
        grid=(n_cols // tn, T_ALL // tm),
        in_specs=[
            pl.BlockSpec((tm, D_MODEL), lambda j, i: (i, 0)),
            pl.BlockSpec((pl.Element(tn), pl.Element(D_MODEL)),
                         lambda j, i: (pl.multiple_of(layer * IN_COLS + col0 + j * tn, 8), 0)),
        ],
        out_specs=pl.BlockSpec((tm, tn), lambda j, i: (i, j)),
        scratch_shapes=[pltpu.VMEM((D_MODEL, tn), BF16)],
        compiler_params=_params("parallel", "arbitrary"),
        name="gate_proj" if gate else "in_proj",
    )(h, w_t)


def _hy_filter_kernel(z_ref, dl_ref, w1_ref, b1_ref, w2_ref, b2_ref, w3_ref, fwd_ref,
                      kre_ref, kim_ref, *, L):
    z = z_ref[...]
    a = jnp.sin(_dot_split(z, w1_ref[...]) + b1_ref[...])
    a = jnp.sin(_dot_split(a, w2_ref[...]) + b2_ref[...])
    h = _dot_split(a, w3_ref[...])
    decay = jnp.exp(-z[:, 0:1] * dl_ref[...])
    not_first = lax.broadcasted_iota(jnp.int32, (L, HY_W), 0) > 0
    sums, diffs = [], []
    for o in range(2):
        fw = h[:, (2 * o) * HY_W:(2 * o + 1) * HY_W] * decay
        bw = jnp.where(not_first, h[:, (2 * o + 1) * HY_W:(2 * o + 2) * HY_W] * decay, 0.0)
        sums.append(fw + bw)
        diffs.append(fw - bw)
    p = _dot(fwd_ref[...], jnp.concatenate(sums, 1))
    q = _dot(fwd_ref[L:2 * L, :], jnp.concatenate(diffs, 1))
    kre_ref[...] = p[0:L]
    first = lax.broadcasted_iota(jnp.int32, (L, 2 * HY_W), 0) == 0
    kim_ref[...] = jnp.where(first, p[L:L + 1], q)


def _hy_filters(L, w1p, b1, w2, b2, w3, fwd):
    zemb, deltas = _hyena_embedding(L)
    out = jax.ShapeDtypeStruct((L, 2 * HY_W), F32)
    return pl.pallas_call(
        functools.partial(_hy_filter_kernel, L=L),
        out_shape=(out, out),
        compiler_params=pltpu.CompilerParams(vmem_limit_bytes=VMEM_LIMIT),
        name=f"hy_filters_{L}",
    )(jnp.asarray(zemb), jnp.asarray(deltas), w1p, b1, w2, b2, w3, fwd)


def _group_step(ctx_body, lat_body):
    i = pl.program_id(0)
    pl.when(i < CTX_STEPS)(ctx_body)
    pl.when(i >= CTX_STEPS)(lat_body)


def _lat_index(i):
    return jnp.maximum(i - CTX_STEPS, 0)


def _hyena_kernel(hy_ref, cw_ref, cb_ref, bias_ref, kre_c, kim_c, fwd_c, inv_c,
                  kre_l, kim_l, fwd_l, inv_l, o_ref):
    _group_step(
        lambda: _hyena_body(hy_ref, cw_ref, cb_ref, bias_ref, kre_c, kim_c, fwd_c, inv_c, o_ref,
                            SEQ, CTX_SEQS_PER_STEP),
        lambda: _hyena_body(hy_ref, cw_ref, cb_ref, bias_ref, kre_l, kim_l, fwd_l, inv_l, o_ref,
                            DEC_SEQ, 1))


def _hyena_body(hy_ref, cw_ref, cb_ref, bias_ref, kre_ref, kim_ref, fwd_ref, inv_ref, o_ref, L, seqs):
    first = lax.broadcasted_iota(jnp.int32, (L, HY_W), 0) == 0

    def long_conv(u, o):
        uf = _dot(fwd_ref[...], u)
        ure, uim = uf[0:L], uf[L:2 * L]
        kre = kre_ref[:, o * HY_W:(o + 1) * HY_W]
        kim = kim_ref[:, o * HY_W:(o + 1) * HY_W]
        yre = jnp.where(first, ure * kre, ure * kre - uim * kim)
        yim = jnp.where(first, uim * kim, ure * kim + uim * kre)
        y = _dot(inv_ref[...], jnp.concatenate([yre, yim], 0))
        return y + u * bias_ref[o:o + 1, :]

    for g in range(seqs):
        sl = slice(g * L, (g + 1) * L)
        x = hy_ref[sl, :].astype(F32)
        rows = lax.broadcasted_iota(jnp.int32, x.shape, 0)
        prev = jnp.where(rows == 0, 0.0, pltpu.roll(x, 1, axis=0))
        nxt = jnp.where(rows == L - 1, 0.0, pltpu.roll(x, L - 1, axis=0))
        z = prev * cw_ref[0:1, :] + x * cw_ref[1:2, :] + nxt * cw_ref[2:3, :] + cb_ref[...]
        v, x1, x2 = z[:, 0:HY_W], z[:, HY_W:2 * HY_W], z[:, 2 * HY_W:3 * HY_W]
        u = x1 * long_conv(v, 0)
        o_ref[sl, :] = (x2 * long_conv(u, 1)).astype(o_ref.dtype)


def _const_spec(shape):
    return pl.BlockSpec(shape, lambda i: (0,) * len(shape))


def _hyena(z_main, layer, conv_w, conv_b, bias, filters, dft):
    tables, table_specs = [], []
    for L in (SEQ, DEC_SEQ):
        tables += [*filters[L], *dft[L]]
        table_specs += [_const_spec((L, 2 * HY_W)), _const_spec((L, 2 * HY_W)),
                        _const_spec((2 * L, L)), _const_spec((L, 2 * L))]
    return pl.pallas_call(
        _hyena_kernel,
        out_shape=jax.ShapeDtypeStruct((T_ALL, HY_W), BF16),
        grid=(MIXER_STEPS,),
        in_specs=[
            pl.BlockSpec((STEP_ROWS, 3 * HY_W), lambda i: (i, 0)),
            pl.BlockSpec((None, 3, 3 * HY_W), lambda i: (layer, 0, 0)),
            pl.BlockSpec((None, 1, 3 * HY_W), lambda i: (layer, 0, 0)),
            pl.BlockSpec((None, 2, HY_W), lambda i: (layer, 0, 0)),
        ] + table_specs,
        out_specs=pl.BlockSpec((STEP_ROWS, HY_W), lambda i: (i, 0)),
        compiler_params=_params("parallel"),
        name="hyena",
    )(z_main, conv_w, conv_b.reshape(DEPTH, 1, 3 * HY_W), bias, *tables)


def _win_masks():
    lane = lax.broadcasted_iota(jnp.int32, (1, LANE), 1)
    return lane < WIN_HD, lane >= WIN_HD


def _win_head_operands(q, k, v, h):
    lo_mask, hi_mask = _win_masks()
    col = h // 2
    lo = h % 2 == 0
    q128 = jnp.where(lo_mask if lo else hi_mask, q[:, col * LANE:(col + 1) * LANE], 0.0)
    swap = h in (1, 2)
    if swap:
        k = pltpu.roll(k, WIN_HD, axis=1)
        v = pltpu.roll(v, WIN_HD, axis=1)
    return q128, k, v, lo


def _win_kernel(sink_ref, q_ref, kv_ref, ck_ref, cv_ref, c_ref, su_ref, sd_ref, o_ref, *, layer):
    _group_step(
        lambda: _win_ctx_body(sink_ref, q_ref, kv_ref, o_ref, layer),
        lambda: _win_lat_body(sink_ref, q_ref, kv_ref, ck_ref, cv_ref, c_ref, su_ref, sd_ref, o_ref, layer))


def _win_ctx_body(sink_ref, q_ref, kv_ref, o_ref, layer):
    lo_mask, hi_mask = _win_masks()
    qscale = WIN_HD ** -0.5 * LOG2E
    for g in range(CTX_SEQS_PER_STEP):
        sl = slice(g * SEQ, (g + 1) * SEQ)
        q = q_ref[sl, :].astype(F32) * qscale
        k = kv_ref[sl, 0:LANE].astype(F32)
        v = kv_ref[sl, LANE:2 * LANE].astype(F32)
        cols = []
        for col in range(2):
            acc = None
            for h in (2 * col, 2 * col + 1):
                q128, kk, vv, lo = _win_head_operands(q, k, v, h)
                s = _dot_nt(q128, kk)
                sink = sink_ref[layer, h] * LOG2E
                m = jnp.maximum(jnp.max(s, -1, keepdims=True), sink)
                p = jnp.exp2(s - m)
                den = jnp.sum(p, -1, keepdims=True) + jnp.exp2(sink - m)
                o = _dot(p, vv) / den
                o = jnp.where(lo_mask if lo else hi_mask, o, 0.0)
                acc = o if acc is None else acc + o
            cols.append(acc)
        o_ref[sl, :] = jnp.concatenate(cols, 1).astype(o_ref.dtype)


def _win_lat_body(sink_ref, q_ref, kv_ref, ck_ref, cv_ref, c_ref, su_ref, sd_ref, o_ref, layer):
    L = DEC_SEQ
    half = WIN_HD // 2
    c, su, sd = c_ref[...], su_ref[...], sd_ref[...]
    qscale = WIN_HD ** -0.5 * LOG2E
    q = jnp.concatenate(
        [_rope128(q_ref[:, i * LANE:(i + 1) * LANE].astype(F32), c, su, sd, half) for i in range(2)],
        1) * qscale
    k = _rope128(kv_ref[:, 0:LANE].astype(F32), c, su, sd, half)
    v = kv_ref[:, LANE:2 * LANE].astype(F32)
    ck = ck_ref[...]
    cv = cv_ref[...]
    lo_mask, hi_mask = _win_masks()
    nb = L // CHUNK
    assert WINDOW == CHUNK
    rr = lax.broadcasted_iota(jnp.int32, (CHUNK, CHUNK), 0)
    cc = lax.broadcasted_iota(jnp.int32, (CHUNK, CHUNK), 1)
    band = {-1: jnp.where(cc >= rr, 0.0, NEG), 0: jnp.zeros((CHUNK, CHUNK), F32),
            1: jnp.where(cc <= rr, 0.0, NEG)}
    cols = []
    for col in range(2):
        acc_blocks = [None] * nb
        for h in (2 * col, 2 * col + 1):
            q128, kk, vv, lo = _win_head_operands(q, k, v, h)
            _, ckk, cvv, _ = _win_head_operands(q, ck, cv, h)
            sink = sink_ref[layer, h] * LOG2E
            for n in range(nb):
                blocks = [d for d in (-1, 0, 1) if 0 <= n + d < nb]
                k0 = (n + blocks[0]) * CHUNK
                k1 = (n + blocks[-1] + 1) * CHUNK
                qn = q128[n * CHUNK:(n + 1) * CHUNK]
                s_loc = _dot_nt(qn, kk[k0:k1]) + jnp.concatenate([band[d] for d in blocks], 1)
                s_ctx = _dot_nt(qn, ckk)
                m = jnp.maximum(jnp.maximum(jnp.max(s_loc, -1, keepdims=True),
                                            jnp.max(s_ctx, -1, keepdims=True)), sink)
                p_loc = jnp.exp2(s_loc - m)
                p_ctx = jnp.exp2(s_ctx - m)
                den = (jnp.sum(p_loc, -1, keepdims=True) + jnp.sum(p_ctx, -1, keepdims=True)
                       + jnp.exp2(sink - m))
                o = (_dot(p_loc, vv[k0:k1]) + _dot(p_ctx, cvv)) / den
                o = jnp.where(lo_mask if lo else hi_mask, o, 0.0)
                acc_blocks[n] = o if acc_blocks[n] is None else acc_blocks[n] + o
        cols.append(jnp.concatenate(acc_blocks, 0))
    o_ref[...] = jnp.concatenate(cols, 1).astype(o_ref.dtype)


def _win(z_main, sink, cache_k, cache_v, layer):
    c, su, sd, _ = _rope_tables(DEC_SEQ, WIN_HD, LANE)
    tab = _const_spec((DEC_SEQ, LANE))
    cache = pl.BlockSpec((None, None, PAST_LEN, LANE), lambda i: (_lat_index(i), layer, 0, 0))
    return pl.pallas_call(
        functools.partial(_win_kernel, layer=layer),
        out_shape=jax.ShapeDtypeStruct((T_ALL, WIN_HEADS * WIN_HD), BF16),
        grid=(MIXER_STEPS,),
        in_specs=[
            pl.BlockSpec(memory_space=pltpu.SMEM),
            pl.BlockSpec((STEP_ROWS, 256), lambda i: (i, COL_WQ // 256)),
            pl.BlockSpec((STEP_ROWS, 256), lambda i: (i, COL_WK // 256)),
            cache, cache, tab, tab, tab,
        ],
        out_specs=pl.BlockSpec((STEP_ROWS, 256), lambda i: (i, 0)),
        compiler_params=_params("parallel"),
        name="win",
    )(sink, z_main, z_main, cache_k, cache_v, jnp.asarray(c), jnp.asarray(su), jnp.asarray(sd))


def _ret_kernel(*refs, layer):
    _group_step(lambda: _ret_body(*refs, L=SEQ, layer=layer, ctx=True),
                lambda: _ret_body(*refs, L=DEC_SEQ, layer=layer, ctx=False))


def _ret_body(df_ref, db_ref, q_ref, k_ref, v0_ref, v1_ref, g0_ref, g1_ref, s0f_ref, s0b_ref,
              o_ref, sf_out, sb_out, s_ref, cross_ref, *, L, layer, ctx):
    seqs = STEP_ROWS // L
    if not ctx:
        sf_out[...] = jnp.zeros_like(sf_out)
        sb_out[...] = jnp.zeros_like(sb_out)
    C = CHUNK
    nc = L // C
    H = RET_HEADS
    qw = H * RET_DK
    vw = H * RET_DV

    def lane_table(width, per_head, fn):
        pos = lax.broadcasted_iota(jnp.int32, (C, per_head), 0).astype(F32)
        return jnp.concatenate([fn(h, pos) for h in range(H)], 1)

    def log_gamma(ref, h):
        d = jnp.full((1, 1), ref[layer, h], F32)
        return jnp.log(jax.nn.sigmoid(d))

    lgf = [log_gamma(df_ref, h) for h in range(H)]
    lgb = [log_gamma(db_ref, h) for h in range(H)]

    def tables(lg, reverse):
        if reverse:
            dq = lane_table(vw, RET_DV, lambda h, pos: jnp.exp((C - pos) * lg[h]))
            dk = lane_table(qw, RET_DK, lambda h, pos: jnp.exp(pos * lg[h]))
        else:
            dq = lane_table(vw, RET_DV, lambda h, pos: jnp.exp((pos + 1.0) * lg[h]))
            dk = lane_table(qw, RET_DK, lambda h, pos: jnp.exp((C - 1.0 - pos) * lg[h]))
        dc = jnp.concatenate([jnp.broadcast_to(jnp.exp(C * lg[h]), (1, RET_DV)) for h in range(H)], 1)
        return dq, dk, dc

    tab_f = tables(lgf, False)
    tab_b = tables(lgb, True)
    ii = lax.broadcasted_iota(jnp.int32, (C, C), 0)
    jj = lax.broadcasted_iota(jnp.int32, (C, C), 1)
    diff = (ii - jj).astype(F32)
    dmats = [jnp.where(diff >= 0, jnp.exp(jnp.maximum(diff, 0.0) * lgf[h]), 0.0)
             + jnp.where(diff <= 0, jnp.exp(jnp.maximum(-diff, 0.0) * lgb[h]), 0.0) for h in range(H)]
    dmat_stack = jnp.concatenate(dmats, 0)
    lane_q = lax.broadcasted_iota(jnp.int32, (1, qw), 1) // RET_DK

    srow = lax.broadcasted_iota(jnp.int32, (qw, vw), 0) // RET_DK
    scol = lax.broadcasted_iota(jnp.int32, (qw, vw), 1) // RET_DV
    diag = srow == scol

    for g in range(seqs):
        base = g * L
        rows_all = slice(base, base + L)
        q_all = q_ref[rows_all, :].astype(F32)
        k_all = k_ref[rows_all, :].astype(F32) * (RET_DK ** -0.5)
        v_all = jnp.concatenate([v0_ref[rows_all, :], v1_ref[rows_all, :]], 1).astype(F32)
        g_all = jnp.concatenate([g0_ref[rows_all, :], g1_ref[rows_all, :]], 1).astype(F32)

        def scan(tabs, reverse, s0_ref, s_out):
            dq, dk, dc = tabs
            if s0_ref is not None:
                s_ref[g] = jnp.zeros((qw, vw), F32)
                for h in range(H):
                    s_ref[g, h * RET_DK:(h + 1) * RET_DK, h * RET_DV:(h + 1) * RET_DV] = s0_ref[h]
            order = range(nc - 1, -1, -1) if reverse else range(nc)
            for step, ci in enumerate(order):
                sl = slice(ci * C, (ci + 1) * C)
                rs = slice(base + ci * C, base + (ci + 1) * C)
                qc, kc, vc = q_all[sl], k_all[sl], v_all[sl]
                upd = jnp.where(diag, _dot_tn(kc * dk, vc), 0.0)
                if s0_ref is None and step == 0:
                    if not reverse:
                        cross_ref[rs, :] = jnp.zeros((C, vw), F32)
                    s_ref[g] = upd
                    continue
                st = s_ref[g]
                cross = _dot(qc, st) * dq
                if reverse:
                    cross_ref[rs, :] = cross_ref[rs, :] + cross
                else:
                    cross_ref[rs, :] = cross
                s_ref[g] = st * dc + upd
            if s_out is not None:
                for h in range(H):
                    s_out[g, h] = s_ref[g, h * RET_DK:(h + 1) * RET_DK, h * RET_DV:(h + 1) * RET_DV]

        scan(tab_f, False, None if ctx else s0f_ref, sf_out if ctx else None)
        scan(tab_b, True, None if ctx else s0b_ref, sb_out if ctx else None)

        for ci in range(nc):
            sl = slice(ci * C, (ci + 1) * C)
            rs = slice(base + ci * C, base + (ci + 1) * C)
            q_stack = jnp.concatenate([jnp.where(lane_q == h, q_all[sl], 0.0) for h in range(H)], 0)
            att = _dot_nt(q_stack, k_all[sl]) * dmat_stack
            ov = _dot(att, v_all[sl])
            for h in range(H):
                hv = slice(h * RET_DV, (h + 1) * RET_DV)
                o = ov[h * C:(h + 1) * C, hv] + cross_ref[rs, hv]
                gt = g_all[sl, hv]
                o_ref[rs, hv] = ((gt * jax.nn.sigmoid(gt)) * _layer_norm(o)).astype(o_ref.dtype)


def _retention(z_main, dec_f, dec_b, s0f, s0b, layer):
    def zcol(col):
        return pl.BlockSpec((STEP_ROWS, 256), lambda i: (i, col // 256))

    smem = pl.BlockSpec(memory_space=pltpu.SMEM)
    z_specs = [zcol(COL_RQ), zcol(COL_RK), zcol(COL_RV), zcol(COL_RV + 256),
               zcol(COL_RG), zcol(COL_RG + 256)]
    s0_spec = pl.BlockSpec((None, None, RET_HEADS, RET_DK, RET_DV),
                           lambda i: (_lat_index(i), layer, 0, 0, 0))
    st_shape = jax.ShapeDtypeStruct((MIXER_STEPS * CTX_SEQS_PER_STEP, RET_HEADS, RET_DK, RET_DV), F32)
    st_spec = pl.BlockSpec((CTX_SEQS_PER_STEP, RET_HEADS, RET_DK, RET_DV), lambda i: (i, 0, 0, 0))
    return pl.pallas_call(
        functools.partial(_ret_kernel, layer=layer),
        out_shape=(jax.ShapeDtypeStruct((T_ALL, RET_HEADS * RET_DV), BF16), st_shape, st_shape),
        grid=(MIXER_STEPS,),
        in_specs=[smem, smem] + z_specs + [s0_spec, s0_spec],
        out_specs=(pl.BlockSpec((STEP_ROWS, RET_HEADS * RET_DV), lambda i: (i, 0)), st_spec, st_spec),
        scratch_shapes=[pltpu.VMEM((CTX_SEQS_PER_STEP, RET_HEADS * RET_DK, RET_HEADS * RET_DV), F32),
                        pltpu.VMEM((STEP_ROWS, RET_HEADS * RET_DV), F32)],
        compiler_params=_params("parallel"),
        name="retention",
    )(dec_f, dec_b, *([z_main] * 6), s0f, s0b)


def _rms_norm(x, g):
    return x * lax.rsqrt(jnp.mean(x * x, -1, keepdims=True) + RMS_EPS) * g


def _mla_keys(kn, kr):
    lane_r = lax.broadcasted_iota(jnp.int32, (1, LANE), 1)
    return jnp.concatenate([kn, jnp.where(lane_r < MLA_ROPE, kr, 0.0)], 1).astype(BF16)


def _mla_attend(qn, qr, k_cat, vv, o_ref, row0):
    qscale = (MLA_NOPE + MLA_ROPE) ** -0.5 * LOG2E
    lane_n = lax.broadcasted_iota(jnp.int32, (1, MLA_HEADS * MLA_NOPE), 1) // MLA_NOPE
    lane_r = lax.broadcasted_iota(jnp.int32, (1, LANE), 1)
    qn = qn * qscale
    qr = qr * qscale
    lq = qn.shape[0]
    heads = []
    for h in range(MLA_HEADS):
        qnh = jnp.where(lane_n == h, qn, 0.0)
        qrh = qr if h == 0 else pltpu.roll(qr, LANE - h * MLA_ROPE, axis=1)
        qrh = jnp.where(lane_r < MLA_ROPE, qrh, 0.0)
        heads.append(jnp.concatenate([qnh, qrh], 1).astype(BF16))
    s = _dot_nt(jnp.concatenate(heads, 0), k_cat)
    m = jnp.max(s, -1, keepdims=True)
    p = jnp.exp2(s - m)
    den = jnp.sum(p, -1, keepdims=True)
    o = _dot(p, vv) / den
    acc = None
    for h in range(MLA_HEADS):
        oh = jnp.where(lane_n == h, o[h * lq:(h + 1) * lq], 0.0)
        acc = oh if acc is None else acc + oh
    o_ref[row0:row0 + lq, :] = acc.astype(o_ref.dtype)


MLA_QN = MLA_HEADS * MLA_NOPE


def _mla_kernel(cq_ref, ckv_ref, kr_ref, cckv_ref, ckr_ref, c_ref, su_ref, sd_ref,
                qg_ref, kg_ref, wq_ref, wkv_ref, o_ref, ckvn_ref):
    weights = (qg_ref, kg_ref, wq_ref, wkv_ref)
    _group_step(
        lambda: _mla_ctx_body(cq_ref, ckv_ref, kr_ref, *weights, o_ref, ckvn_ref),
        lambda: _mla_lat_body(cq_ref, ckv_ref, kr_ref, cckv_ref, ckr_ref, c_ref, su_ref, sd_ref,
                              *weights, o_ref, ckvn_ref))


def _mla_ctx_body(cq_ref, ckv_ref, kr_ref, qg_ref, kg_ref, wq_ref, wkv_ref, o_ref, ckvn_ref):
    q = _dot(_rms_norm(cq_ref[...].astype(F32), qg_ref[...]), wq_ref[...])
    qn, qr = q[:, 0:MLA_QN], q[:, MLA_QN:]
    ckvn = _rms_norm(ckv_ref[...].astype(F32), kg_ref[...])
    ckvn_ref[...] = ckvn
    kv = _dot(ckvn, wkv_ref[...])
    k_cat = _mla_keys(kv[:, 0:MLA_QN], kr_ref[...].astype(F32))
    vv = kv[:, MLA_QN:].astype(BF16)
    for g in range(CTX_SEQS_PER_STEP):
        sl = slice(g * SEQ, (g + 1) * SEQ)
        _mla_attend(qn[sl], qr[sl], k_cat[sl], vv[sl], o_ref, g * SEQ)


def _mla_lat_body(cq_ref, ckv_ref, kr_ref, cckv_ref, ckr_ref, c_ref, su_ref, sd_ref,
                  qg_ref, kg_ref, wq_ref, wkv_ref, o_ref, ckvn_ref):
    half = MLA_ROPE // 2
    c, su, sd = c_ref[...], su_ref[...], sd_ref[...]
    q = _dot(_rms_norm(cq_ref[...].astype(F32), qg_ref[...]), wq_ref[...])
    qn = q[:, 0:MLA_QN]
    qr = _rope128(q[:, MLA_QN:], c, su, sd, half)
    ckvn = _rms_norm(ckv_ref[...].astype(F32), kg_ref[...])
    ckvn_ref[...] = ckvn
    ckv_all = jnp.concatenate([ckvn, cckv_ref[...]], 0)
    kv = _dot(ckv_all, wkv_ref[...])
    vv = kv[:, MLA_QN:].astype(BF16)
    kr = jnp.concatenate([_rope128(kr_ref[...].astype(F32), c, su, sd, half), ckr_ref[...]], 0)
    k_cat = _mla_keys(kv[:, 0:MLA_QN], kr)
    rows_per_call = 256
    for n in range(DEC_SEQ // rows_per_call):
        rows = slice(n * rows_per_call, (n + 1) * rows_per_call)
        _mla_attend(qn[rows], qr[rows], k_cat, vv, o_ref, n * rows_per_call)


def _mla(z_main, cache_ckv, cache_kr_pad, weights, layer):
    c, su, sd, _ = _rope_tables(DEC_SEQ, MLA_ROPE, LANE)
    tab = _const_spec((DEC_SEQ, LANE))
    cache = pl.BlockSpec((None, None, PAST_LEN, LANE), lambda i: (_lat_index(i), layer, 0, 0))

    def weight(*shape):
        return pl.BlockSpec((None,) + shape, lambda i: (layer, 0, 0))

    return pl.pallas_call(
        _mla_kernel,
        out_shape=(jax.ShapeDtypeStruct((T_ALL, MLA_HEADS * MLA_V), BF16),
                   jax.ShapeDtypeStruct((T_ALL, MLA_KV_LORA), F32)),
        grid=(MIXER_STEPS,),
        in_specs=[
            pl.BlockSpec((STEP_ROWS, 256), lambda i: (i, COL_CQ // 256)),
            pl.BlockSpec((STEP_ROWS, LANE), lambda i: (i, COL_CKV // LANE)),
            pl.BlockSpec((STEP_ROWS, LANE), lambda i: (i, COL_KROPE // LANE)),
            cache, cache, tab, tab, tab,
            weight(1, MLA_Q_LORA), weight(1, MLA_KV_LORA),
            weight(MLA_Q_LORA, MLA_QN + MLA_HEADS * MLA_ROPE),
            weight(MLA_KV_LORA, MLA_QN + MLA_HEADS * MLA_V),
        ],
        out_specs=(pl.BlockSpec((STEP_ROWS, 256), lambda i: (i, 0)),
                   pl.BlockSpec((STEP_ROWS, MLA_KV_LORA), lambda i: (i, 0))),
        compiler_params=_params("parallel"),
        name="mla",
    )(z_main, z_main, z_main, cache_ckv, cache_kr_pad,
      jnp.asarray(c), jnp.asarray(su), jnp.asarray(sd), *weights)


def _route(logits_t, rb):
    scores = jax.nn.sigmoid(logits_t)
    biased = scores + rb
    sc = [scores[e:e + 1, :] for e in range(N_EXPERTS)]
    bi = [biased[e:e + 1, :] for e in range(N_EXPERTS)]
    epg = EXPERTS_PER_GROUP
    gsum = []
    for g in range(N_GROUPS):
        v = bi[g * epg:(g + 1) * epg]
        best = None
        for a in range(epg):
            for b in range(a + 1, epg):
                pair = v[a] + v[b]
                best = pair if best is None else jnp.maximum(best, pair)
        gsum.append(best)
    combine = []
    sel = []
    for g in range(N_GROUPS):
        is_best = None
        for g2 in range(N_GROUPS):
            if g2 == g:
                continue
            c = gsum[g] > gsum[g2] if g2 < g else gsum[g] >= gsum[g2]
            is_best = c if is_best is None else jnp.logical_and(is_best, c)
        for a in range(epg):
            e = g * epg + a
            rank = jnp.zeros_like(bi[e])
            for b in range(epg):
                if b == a:
                    continue
                e2 = g * epg + b
                ahead = bi[e2] >= bi[e] if b < a else bi[e2] > bi[e]
                rank = rank + jnp.where(ahead, 1.0, 0.0)
            sel.append(jnp.logical_and(is_best, rank < 2.0))
    wsum = None
    for e in range(N_EXPERTS):
        w = jnp.where(sel[e], sc[e], 0.0)
        wsum = w if wsum is None else wsum + w
    for e in range(N_EXPERTS):
        combine.append(jnp.where(sel[e], ROUTE_SCALE * sc[e] / wsum, 0.0))
    return jnp.concatenate(combine, 0)


MERGE_TILE = 512

MERGE_BRANCH_ROWS = (HY_W, WIN_HEADS * WIN_HD, RET_HEADS * RET_DV, MLA_HEADS * MLA_V)
MERGE_ROWS = sum(MERGE_BRANCH_ROWS) + D_MODEL


def _merge_kernel(ya_ref, yb_ref, yc_ref, yd_ref, gt_ref, xc_ref, xl_ref, m_ref,
                  w_ref, g_ref, b_ref, rw_ref, rb_ref,
                  x1_ref, h2_ref, cmb_ref, *, ctx_tiles, sub_rows):
    D = D_MODEL
    rw = rw_ref[...]
    rw_hi = rw.astype(BF16)
    rw_lo = (rw - rw_hi.astype(F32)).astype(BF16)
    g1 = m_ref[:, 2 * D:3 * D]
    s2 = m_ref[:, 3 * D:4 * D]
    sc2 = m_ref[:, 4 * D:5 * D]
    is_ctx = pl.program_id(0) < ctx_tiles
    offs = np.cumsum((0,) + MERGE_BRANCH_ROWS)
    branches = tuple((y_ref, slice(int(offs[i]), int(offs[i + 1])))
                     for i, y_ref in enumerate((ya_ref, yb_ref, yc_ref, yd_ref)))
    w_out_rows = slice(int(offs[-1]), MERGE_ROWS)
    for r0 in range(0, x1_ref.shape[0], sub_rows):
        rows = slice(r0, r0 + sub_rows)
        merged = None
        for i, (y_ref, w_rows) in enumerate(branches):
            t = gt_ref[rows, i * D:(i + 1) * D] * jnp.dot(
                y_ref[rows, :], w_ref[w_rows, :], preferred_element_type=F32).astype(BF16)
            merged = t if merged is None else merged + t
        out1 = jnp.dot(merged, w_ref[w_out_rows, :], preferred_element_type=F32)
        x = jnp.where(is_ctx, xc_ref[rows, :], xl_ref[rows, :])
        x1 = _layer_norm(ALPHA * x + g1 * out1) * g_ref[...] + b_ref[...]
        x1_ref[rows, :] = x1
        h2 = _layer_norm(x1) * (1.0 + sc2) + s2
        h2_hi = h2.astype(BF16)
        h2_ref[rows, :] = h2_hi
        h2_lo = (h2 - h2_hi.astype(F32)).astype(BF16)
        logits = (jnp.dot(h2_hi, rw_hi, preferred_element_type=F32)
                  + (jnp.dot(h2_lo, rw_hi, preferred_element_type=F32)
                     + jnp.dot(h2_hi, rw_lo, preferred_element_type=F32)))
        cmb_ref[:, rows] = _route(logits.T[0:N_EXPERTS], rb_ref[...])


def _merge(ya, yb, yc, yd, gates, x_ctx, x_lat, lat_block0, mods, w_merge, ln1_g, ln1_b,
           router_w, router_b, layer):
    tm = MERGE_TILE
    row = _mod_row(tm)
    D = D_MODEL
    ctx_tiles = T_CTX // tm

    def tile(w):
        return pl.BlockSpec((tm, w), lambda i: (i, 0))

    def weight(k, n):
        return pl.BlockSpec((None, k, n), lambda i: (layer, 0, 0))

    return pl.pallas_call(
        functools.partial(_merge_kernel, ctx_tiles=ctx_tiles, sub_rows=256),
        out_shape=(jax.ShapeDtypeStruct((T_ALL, D), F32),
                   jax.ShapeDtypeStruct((T_ALL, D), BF16),
                   jax.ShapeDtypeStruct((N_EXPERTS, T_ALL), F32)),
        grid=(T_ALL // tm,),
        in_specs=[
            tile(256), tile(256), tile(512), tile(256), tile(4 * D),
            *_x_specs(tm, ctx_tiles, lat_block0),
            _mod_spec(layer, row),
            weight(MERGE_ROWS, D), weight(1, D), weight(1, D),
            pl.BlockSpec((D, LANE), lambda i: (0, 0)),
            pl.BlockSpec((N_EXPERTS, 1), lambda i: (0, 0)),
        ],
        out_specs=(tile(D), tile(D), pl.BlockSpec((N_EXPERTS, tm), lambda i: (0, i))),
        compiler_params=_params("parallel"),
        name="merge",
    )(ya, yb, yc, yd, gates, x_ctx, x_lat, mods, w_merge,
      ln1_g.reshape(DEPTH, 1, D), ln1_b.reshape(DEPTH, 1, D), router_w,
      router_b.reshape(N_EXPERTS, 1))


MOE_EXPERTS_PER_STEP = 2


def _moe_kernel(*refs, next_h):
    if next_h:
        (h_ref, c_ref, x1_ref, m_ref, wg_ref, wu_ref, wd_ref, g_ref, b_ref, mn_ref,
         o_ref, hn_ref, acc_ref) = refs
    else:
        h_ref, c_ref, x1_ref, m_ref, wg_ref, wu_ref, wd_ref, g_ref, b_ref, o_ref, acc_ref = refs
    eg = pl.program_id(1)

    @pl.when(eg == 0)
    def _():
        acc_ref[...] = jnp.zeros_like(acc_ref)

    h = h_ref[...]
    cmb = c_ref[...]
    lane = lax.broadcasted_iota(jnp.int32, cmb.shape, 1)
    hid = []
    for k in range(MOE_EXPERTS_PER_STEP):
        gate = jnp.dot(h, wg_ref[k].astype(BF16), preferred_element_type=F32)
        up = jnp.dot(h, wu_ref[k].astype(BF16), preferred_element_type=F32)
        e = eg * MOE_EXPERTS_PER_STEP + k
        ce = jnp.sum(jnp.where(lane == e, cmb, 0.0), -1, keepdims=True)
        sig = 0.5 * jnp.tanh(0.5 * gate) + 0.5
        hid.append((gate * sig * (up * ce)).astype(BF16))
    wd = wd_ref[...].reshape(MOE_EXPERTS_PER_STEP * D_EXPERT, D_MODEL).astype(BF16)
    acc_ref[...] += jnp.dot(jnp.concatenate(hid, 1), wd, preferred_element_type=F32)

    @pl.when(eg == N_EXPERTS // MOE_EXPERTS_PER_STEP - 1)
    def _():
        g2 = m_ref[:, 5 * D_MODEL:6 * D_MODEL]
        y = _layer_norm(ALPHA * x1_ref[...] + g2 * acc_ref[...])
        y = y * g_ref[...] + b_ref[...]
        o_ref[...] = y
        if next_h:
            s1 = mn_ref[:, 0:D_MODEL]
            sc1 = mn_ref[:, D_MODEL:2 * D_MODEL]
            hn_ref[...] = (_layer_norm(y) * (1.0 + sc1) + s1).astype(hn_ref.dtype)


def _moe(h2, combine, x1, mods, w_gate, w_up, w_down, ln2_g, ln2_b, layer, row0, n_rows,
         next_h=False):
    tm = 1024
    row = _mod_row(tm)
    D = D_MODEL
    t0 = row0 // tm
    eps = MOE_EXPERTS_PER_STEP
    mod_spec = _mod_spec(layer, lambda i, e: row(t0 + i))
    in_specs = [
        pl.BlockSpec((tm, D), lambda i, e: (t0 + i, 0)),
        pl.BlockSpec((tm, N_EXPERTS), lambda i, e: (t0 + i, 0)),
        pl.BlockSpec((tm, D), lambda i, e: (t0 + i, 0)),
        mod_spec,
        pl.BlockSpec((None, eps, D, D_EXPERT), lambda i, e: (layer, e, 0, 0)),
        pl.BlockSpec((None, eps, D, D_EXPERT), lambda i, e: (layer, e, 0, 0)),
        pl.BlockSpec((None, eps, D_EXPERT, D), lambda i, e: (layer, e, 0, 0)),
        pl.BlockSpec((None, 1, D), lambda i, e: (layer, 0, 0)),
        pl.BlockSpec((None, 1, D), lambda i, e: (layer, 0, 0)),
    ]
    args = [h2, combine, x1, mods, w_gate, w_up, w_down,
            ln2_g.reshape(DEPTH, 1, D), ln2_b.reshape(DEPTH, 1, D)]
    out_shape = jax.ShapeDtypeStruct((n_rows, D), F32)
    out_specs = pl.BlockSpec((tm, D), lambda i, e: (i, 0))
    if next_h:
        in_specs.append(_mod_spec(layer + 1, lambda i, e: row(t0 + i)))
        args.append(mods)
        out_shape = (out_shape, jax.ShapeDtypeStruct((T_ALL, D), BF16))
        assert n_rows == T_ALL
        out_specs = (out_specs, pl.BlockSpec((tm, D), lambda i, e: (i, 0)))
    return pl.pallas_call(
        functools.partial(_moe_kernel, next_h=next_h),
        out_shape=out_shape,
        grid=(n_rows // tm, N_EXPERTS // eps),
        in_specs=in_specs,
        out_specs=out_specs,
        scratch_shapes=[pltpu.VMEM((tm, D), F32)],
        compiler_params=_params("parallel", "arbitrary"),
        name="moe",
    )(*args)


def kernel(x_prompt, x_sample, cache_win_k, cache_win_v, cache_mla_ckv, cache_mla_krope,
           state_ret_fwd, state_ret_bwd, c, c_ctx, w_ada, b_ada, w_in,
           hy_conv_w, hy_conv_b, hy_w1, hy_b1, hy_w2, hy_b2, hy_w3, hy_bias,
           win_sink, ret_decay_fwd, ret_decay_bwd, mla_q_norm, mla_kv_norm, mla_w_uq, mla_w_ukv,
           w_br_a, w_br_b, w_br_c, w_br_d, w_out, ln1_g, ln1_b, ln2_g, ln2_b,
           router_w, router_b, moe_w_gate, moe_w_up, moe_w_down):
    D = D_MODEL
    x_ctx = x_prompt.reshape(T_CTX, D)
    x_lat = x_sample.reshape(T_LAT, D)
    lat_block0 = 0

    mods = _ada_mods(c_ctx, c, w_ada, b_ada)

    w_in_t = jnp.swapaxes(w_in, 1, 2).reshape(DEPTH * IN_COLS, D)
    cache_k = cache_win_k.reshape(DEC_BATCH, DEPTH, PAST_LEN, WIN_KV_HEADS * WIN_HD)
    cache_v = cache_win_v.reshape(DEC_BATCH, DEPTH, PAST_LEN, WIN_KV_HEADS * WIN_HD)
    cache_kr = jnp.pad(cache_mla_krope, ((0, 0), (0, 0), (0, 0), (0, LANE - MLA_ROPE)))

    uq = mla_w_uq.reshape(DEPTH, MLA_Q_LORA, MLA_HEADS, MLA_NOPE + MLA_ROPE)
    ukv = mla_w_ukv.reshape(DEPTH, MLA_KV_LORA, MLA_HEADS, MLA_NOPE + MLA_V)
    mla_weights = (
        mla_q_norm.reshape(DEPTH, 1, MLA_Q_LORA),
        mla_kv_norm.reshape(DEPTH, 1, MLA_KV_LORA),
        jnp.concatenate([uq[..., :MLA_NOPE].reshape(DEPTH, MLA_Q_LORA, MLA_HEADS * MLA_NOPE),
                         uq[..., MLA_NOPE:].reshape(DEPTH, MLA_Q_LORA, MLA_HEADS * MLA_ROPE)], -1),
        jnp.concatenate([ukv[..., :MLA_NOPE].reshape(DEPTH, MLA_KV_LORA, MLA_HEADS * MLA_NOPE),
                         ukv[..., MLA_NOPE:].reshape(DEPTH, MLA_KV_LORA, MLA_HEADS * MLA_V)], -1),
    )

    hy_w1p = jnp.pad(hy_w1, ((0, 0), (0, LANE - HY_EMB), (0, 0)))
    dft = {}
    for L in (SEQ, DEC_SEQ):
        fwd, inv = _dft_tables(L)
        dft[L] = (jnp.asarray(fwd).astype(BF16), jnp.asarray(inv).astype(BF16))
    router_w_pad = jnp.pad(router_w, ((0, 0), (0, LANE - N_EXPERTS)))
    w_merge = jnp.concatenate([w_br_a, w_br_b, w_br_c, w_br_d, w_out], 1).astype(BF16)

    new_k, new_v, new_ckv, new_kr, new_sf, new_sb = [], [], [], [], [], []
    for l in range(DEPTH):
        if l == 0:
            h = _ln_mod(x_ctx, x_lat, mods, l)
        z = _in_proj(h, w_in_t, l, 0, Z_MAIN, Z_MAIN // 2, BF16, gate=False)
        gates = _in_proj(h, w_in_t, l, COL_GATE, 4 * D, D, BF16, gate=True)

        filters = {L: _hy_filters(L, hy_w1p[l], hy_b1[l][None], hy_w2[l], hy_b2[l][None], hy_w3[l],
                                  dft[L][0]) for L in (SEQ, DEC_SEQ)}
        ya = _hyena(z, l, hy_conv_w, hy_conv_b, hy_bias, filters, dft)
        yb = _win(z, win_sink, cache_k, cache_v, l)
        yc, sf, sb = _retention(z, ret_decay_fwd, ret_decay_bwd, state_ret_fwd, state_ret_bwd, l)
        yd, ckvn = _mla(z, cache_mla_ckv, cache_kr, mla_weights, l)

        x1, h2, combine_t = _merge(ya, yb, yc, yd, gates, x_ctx, x_lat, lat_block0, mods, w_merge,
                                   ln1_g, ln1_b, router_w_pad, router_b, l)
        moe_args = (h2, combine_t.T, x1, mods, moe_w_gate, moe_w_up, moe_w_down, ln2_g, ln2_b, l)
        if l + 1 < DEPTH:
            x_ctx, h = _moe(*moe_args, 0, T_ALL, next_h=True)
            x_lat, lat_block0 = x_ctx, T_CTX // MERGE_TILE
        else:
            x_ctx = _moe(*moe_args, 0, T_CTX)
            x_lat = _moe(*moe_args, T_CTX, T_LAT)

        def ctx_cols(col, width):
            return z[:T_CTX, col:col + width].astype(F32)

        new_k.append(ctx_cols(COL_WK, 128).reshape(BATCH, SEQ, WIN_KV_HEADS, WIN_HD))
        new_v.append(ctx_cols(COL_WV, 128).reshape(BATCH, SEQ, WIN_KV_HEADS, WIN_HD))
        new_ckv.append(ckvn[:T_CTX].reshape(BATCH, SEQ, MLA_KV_LORA))
        new_kr.append(ctx_cols(COL_KROPE, MLA_ROPE).reshape(BATCH, SEQ, MLA_ROPE))
        new_sf.append(sf[:BATCH])
        new_sb.append(sb[:BATCH])

    y_prompt = x_ctx.reshape(BATCH, SEQ, D)
    y_sample = x_lat.reshape(DEC_BATCH, DEC_SEQ, D)
    return (y_prompt, y_sample, jnp.stack(new_k, 1), jnp.stack(new_v, 1), jnp.stack(new_ckv, 1),
            jnp.stack(new_kr, 1), jnp.stack(new_sf, 1), jnp.stack(new_sb, 1))
```

```python
import functools
import math

import numpy as np
import jax
import jax.numpy as jnp
from jax import lax
from jax.experimental import pallas as pl
from jax.experimental.pallas import tpu as pltpu

F32 = jnp.float32
BF16 = jnp.bfloat16

D_MODEL = 1024
BATCH = 16
SEQ = 256
DEPTH = 2
DEC_BATCH = 2
DEC_SEQ = 1024
PAST_LEN = 256
GRID_W = 64
CHUNK = 128
ROPE_BASE = 10000.0
NEG = -1e30
LN_EPS = 1e-5
RMS_EPS = 1e-6
LOG2E = math.log2(math.e)

HY_W = 256
HY_BANDS = 16
HY_EMB = 1 + 2 * HY_BANDS
HY_FFN = 64
HY_FAST_DECAY = 0.3
HY_SLOW_DECAY = 1.5
HY_TARGET = 1e-2

WIN_HEADS = 4
WIN_KV_HEADS = 2
WIN_HD = 64
WINDOW = 128

RET_HEADS = 4
RET_DK = 64
RET_DV = 128

MLA_HEADS = 4
MLA_Q_LORA = 256
MLA_KV_LORA = 128
MLA_NOPE = 64
MLA_ROPE = 32
MLA_V = 64

N_EXPERTS = 16
N_GROUPS = 4
EXPERTS_PER_GROUP = N_EXPERTS // N_GROUPS
D_EXPERT = 256
ROUTE_SCALE = 2.5

ALPHA = (2.0 * DEPTH) ** 0.25

T_CTX = BATCH * SEQ
T_LAT = DEC_BATCH * DEC_SEQ
T_ALL = T_CTX + T_LAT

COL_HY = 0
COL_WQ = 768
COL_WK = 1024
COL_WV = 1152
COL_RQ = 1280
COL_RK = 1536
COL_RV = 1792
COL_RG = 2304
COL_CQ = 2816
COL_CKV = 3072
COL_KROPE = 3200
COL_GATE = 3232
IN_COLS = COL_GATE + 4 * D_MODEL
Z_MAIN = 3328

LANE = 128
STEP_ROWS = 1024
CTX_SEQS_PER_STEP = STEP_ROWS // SEQ
CTX_STEPS = T_CTX // STEP_ROWS
MIXER_STEPS = T_ALL // STEP_ROWS
VMEM_LIMIT = 56 * 1024 * 1024


def _params(*sem):
    return pltpu.CompilerParams(dimension_semantics=sem, vmem_limit_bytes=VMEM_LIMIT)


def _dot(a, b):
    return jnp.dot(a.astype(BF16), b.astype(BF16), preferred_element_type=F32)


def _dot_split(a, b):
    a_hi = a.astype(BF16)
    a_lo = (a - a_hi.astype(F32)).astype(BF16)
    b_hi = b.astype(BF16)
    b_lo = (b - b_hi.astype(F32)).astype(BF16)

    def mm(x, y):
        return jnp.dot(x, y, preferred_element_type=F32)

    return mm(a_hi, b_hi) + (mm(a_lo, b_hi) + mm(a_hi, b_lo))


def _dot_nt(a, b):
    return lax.dot_general(a.astype(BF16), b.astype(BF16), (((1,), (1,)), ((), ())),
                           preferred_element_type=F32)


def _dot_tn(a, b):
    return lax.dot_general(a.astype(BF16), b.astype(BF16), (((0,), (0,)), ((), ())),
                           preferred_element_type=F32)


def _layer_norm(x):
    mu = jnp.mean(x, -1, keepdims=True)
    xc = x - mu
    var = jnp.mean(xc * xc, -1, keepdims=True)
    return xc * lax.rsqrt(var + LN_EPS)


def _mod_row(tile_rows):
    def row(i):
        start = i * tile_rows
        return jnp.where(start < T_CTX, 0, 1 + (start - T_CTX) // DEC_SEQ)
    return row


@functools.lru_cache(maxsize=None)
def _dft_tables(L):
    f = np.arange(L, dtype=np.int64)[:, None]
    s = np.arange(L, dtype=np.int64)[None, :]
    ang = np.pi * ((f * s) % (2 * L)).astype(np.float64) / L
    cos = np.cos(ang)
    sin = np.sin(ang)
    alt = np.where(np.arange(L) % 2 == 0, 1.0, -1.0)
    fwd_im = -sin
    fwd_im[0, :] = alt
    fwd = np.concatenate([cos, fwd_im], 0)
    inv_re = cos.T / L
    inv_re[:, 0] = 1.0 / (2 * L)
    inv_im = -sin.T / L
    inv_im[:, 0] = alt / (2 * L)
    inv = np.concatenate([inv_re, inv_im], 1)
    return fwd.astype(np.float32), inv.astype(np.float32)


@functools.lru_cache(maxsize=None)
def _hyena_embedding(L):
    t01 = np.linspace(0.0, 1.0, L, dtype=np.float64)[:, None]
    bands = np.linspace(1e-4, HY_BANDS - 1, HY_BANDS, dtype=np.float64)
    ang = (2.0 * math.pi / L) * np.arange(L, dtype=np.float64)[:, None] * bands[None, :]
    z = np.concatenate([t01, np.cos(ang), -np.sin(ang)], -1)
    zp = np.zeros((L, LANE), np.float64)
    zp[:, :HY_EMB] = z
    deltas = np.abs(np.linspace(math.log(HY_TARGET) / HY_SLOW_DECAY,
                                math.log(HY_TARGET) / HY_FAST_DECAY, HY_W, dtype=np.float64))
    return zp.astype(np.float32), deltas[None, :].astype(np.float32)


@functools.lru_cache(maxsize=None)
def _rope_tables(L, rot_dim, width):
    rows = L // GRID_W
    n_freq = rot_dim // 4
    half = rot_dim // 2
    inv = ROPE_BASE ** (-np.arange(n_freq, dtype=np.float64) / n_freq)
    pos = np.arange(L)
    row = (pos // GRID_W).astype(np.float64)
    col = (pos % GRID_W).astype(np.float64)
    ang = np.concatenate([row[:, None] * inv, col[:, None] * inv], -1)
    cos, sin = np.cos(ang), np.sin(ang)
    zero = np.zeros_like(sin)
    c = np.tile(np.concatenate([cos, cos], -1), (1, width // rot_dim))
    s_up = np.tile(np.concatenate([-sin, zero], -1), (1, width // rot_dim))
    s_dn = np.tile(np.concatenate([zero, sin], -1), (1, width // rot_dim))
    return c.astype(np.float32), s_up.astype(np.float32), s_dn.astype(np.float32), half


def _rope128(x, c, s_up, s_dn, half):
    up = pltpu.roll(x, LANE - half, axis=1)
    dn = pltpu.roll(x, half, axis=1)
    return x * c + up * s_up + dn * s_dn


MOD_ROWS = 1 + DEC_BATCH


def _ada_kernel(ct_ref, w_ref, b_ref, o_ref):
    @pl.when(pl.program_id(1) == 0)
    def _():
        for r in range(MOD_ROWS):
            o_ref[r] = b_ref[...]

    ct = ct_ref[...]
    s = ct * jax.nn.sigmoid(ct)
    tk, n = w_ref.shape
    rows, cols = 64, 4 * LANE
    s_cols = [[s[k0:k0 + rows, r:r + 1] for k0 in range(0, tk, rows)] for r in range(MOD_ROWS)]
    for c0 in range(0, n, cols):
        acc = [None] * MOD_ROWS
        for ki, k0 in enumerate(range(0, tk, rows)):
            wc = w_ref[k0:k0 + rows, c0:c0 + cols]
            for r in range(MOD_ROWS):
                part = jnp.sum((wc * s_cols[r][ki]).reshape(rows // 8, 8, cols), axis=0)
                acc[r] = part if acc[r] is None else acc[r] + part
        for r in range(MOD_ROWS):
            o_ref[r, :, c0:c0 + cols] += jnp.sum(acc[r], axis=0, keepdims=True)


def _ada_mods(c_ctx, c, w_ada, b_ada):
    tk = 256
    n = 6 * D_MODEL
    c_cols = jnp.concatenate([c_ctx[:, None], c.T], 1)
    return pl.pallas_call(
        _ada_kernel,
        out_shape=jax.ShapeDtypeStruct((DEPTH, MOD_ROWS, 1, n), F32),
        grid=(DEPTH, D_MODEL // tk),
        in_specs=[
            pl.BlockSpec((tk, MOD_ROWS), lambda l, k: (k, 0)),
            pl.BlockSpec((None, tk, n), lambda l, k: (l, k, 0)),
            pl.BlockSpec((None, 1, n), lambda l, k: (l, 0, 0)),
        ],
        out_specs=pl.BlockSpec((None, MOD_ROWS, 1, n), lambda l, k: (l, 0, 0, 0)),
        compiler_params=_params("parallel", "arbitrary"),
        name="ada_mods",
    )(c_cols, w_ada, b_ada.reshape(DEPTH, 1, n))


def _mod_spec(layer, row_of_step):
    return pl.BlockSpec((None, None, 1, 6 * D_MODEL), lambda *g: (layer, row_of_step(*g), 0, 0))


def _lnmod_kernel(xc_ref, xl_ref, m_ref, h_ref, *, ctx_tiles):
    x = jnp.where(pl.program_id(0) < ctx_tiles, xc_ref[...], xl_ref[...])
    y = _layer_norm(x)
    s1 = m_ref[:, 0:D_MODEL]
    sc1 = m_ref[:, D_MODEL:2 * D_MODEL]
    h_ref[...] = (y * (1.0 + sc1) + s1).astype(h_ref.dtype)


def _x_specs(tm, ctx_tiles, lat_block0=0):
    return [pl.BlockSpec((tm, D_MODEL), lambda i: (jnp.minimum(i, ctx_tiles - 1), 0)),
            pl.BlockSpec((tm, D_MODEL), lambda i: (jnp.maximum(i - ctx_tiles, 0) + lat_block0, 0))]


def _ln_mod(x_ctx, x_lat, mods, layer):
    tm = 512
    ctx_tiles = T_CTX // tm
    return pl.pallas_call(
        functools.partial(_lnmod_kernel, ctx_tiles=ctx_tiles),
        out_shape=jax.ShapeDtypeStruct((T_ALL, D_MODEL), BF16),
        grid=(T_ALL // tm,),
        in_specs=_x_specs(tm, ctx_tiles) + [_mod_spec(layer, _mod_row(tm))],
        out_specs=pl.BlockSpec((tm, D_MODEL), lambda i: (i, 0)),
        compiler_params=_params("parallel"),
        name="ln_mod",
    )(x_ctx, x_lat, mods)


def _proj_kernel(h_ref, w_ref, o_ref, wb_ref, *, gate):
    @pl.when(pl.program_id(1) == 0)
    def _():
        wb_ref[...] = w_ref[...].T.astype(BF16)

    if not gate:
        o_ref[...] = jnp.dot(h_ref[...], wb_ref[...], preferred_element_type=F32).astype(o_ref.dtype)
        return
    sub = 2 * LANE
    for c0 in range(0, o_ref.shape[1], sub):
        r = jnp.dot(h_ref[...], wb_ref[:, c0:c0 + sub], preferred_element_type=F32)
        rb = r.astype(o_ref.dtype)
        o_ref[:, c0:c0 + sub] = 0.5 * jnp.tanh(0.5 * rb) + 0.5


def _in_proj(h, w_t, layer, col0, n_cols, tn, out_dtype, gate):
    tm = 2048
    return pl.pallas_call(
        functools.partial(_proj_kernel, gate=gate),
        out_shape=jax.ShapeDtypeStruct((T_ALL, n_cols), out_dtype),
        grid=(n_cols // tn, T_ALL // tm),
        in_specs=[
            pl.BlockSpec((tm, D_MODEL), lambda j, i: (i, 0)),
            pl.BlockSpec((pl.Element(tn), pl.Element(D_MODEL)),
                         lambda j, i: (pl.multiple_of(layer * IN_COLS + col0 + j * tn, 8), 0)),
        ],
        out_specs=pl.BlockSpec((tm, tn), lambda j, i: (i, j)),
        scratch_shapes=[pltpu.VMEM((D_MODEL, tn), BF16)],
        compiler_params=_params("parallel", "arbitrary"),
        name="gate_proj" if gate else "in_proj",
    )(h, w_t)


def _hy_filter_kernel(z_ref, dl_ref, w1_ref, b1_ref, w2_ref, b2_ref, w3_ref, fwd_ref,
                      kre_ref, kim_ref, *, L):
    z = z_ref[...]
    a = jnp.sin(_dot_split(z, w1_ref[...]) + b1_ref[...])
    a = jnp.sin(_dot_split(a, w2_ref[...]) + b2_ref[...])
    h = _dot_split(a, w3_ref[...])
    decay = jnp.exp(-z[:, 0:1] * dl_ref[...])
    not_first = lax.broadcasted_iota(jnp.int32, (L, HY_W), 0) > 0
    sums, diffs = [], []
    for o in range(2):
        fw = h[:, (2 * o) * HY_W:(2 * o + 1) * HY_W] * decay
        bw = jnp.where(not_first, h[:, (2 * o + 1) * HY_W:(2 * o + 2) * HY_W] * decay, 0.0)
        sums.append(fw + bw)
        diffs.append(fw - bw)
    p = _dot(fwd_ref[...], jnp.concatenate(sums, 1))
    q = _dot(fwd_ref[L:2 * L, :], jnp.concatenate(diffs, 1))
    kre_ref[...] = p[0:L]
    first = lax.broadcasted_iota(jnp.int32, (L, 2 * HY_W), 0) == 0
    kim_ref[...] = jnp.where(first, p[L:L + 1], q)


def _hy_filters(L, w1p, b1, w2, b2, w3, fwd):
    zemb, deltas = _hyena_embedding(L)
    out = jax.ShapeDtypeStruct((L, 2 * HY_W), F32)
    return pl.pallas_call(
        functools.partial(_hy_filter_kernel, L=L),
        out_shape=(out, out),
        compiler_params=pltpu.CompilerParams(vmem_limit_bytes=VMEM_LIMIT),
        name=f"hy_filters_{L}",
    )(jnp.asarray(zemb), jnp.asarray(deltas), w1p, b1, w2, b2, w3, fwd)


def _group_step(ctx_body, lat_body):
    i = pl.program_id(0)
    pl.when(i < CTX_STEPS)(ctx_body)
    pl.when(i >= CTX_STEPS)(lat_body)


def _lat_index(i):
    return jnp.maximum(i - CTX_STEPS, 0)


def _hyena_kernel(hy_ref, cw_ref, cb_ref, bias_ref, kre_c, kim_c, fwd_c, inv_c,
                  kre_l, kim_l, fwd_l, inv_l, o_ref):
    _group_step(
        lambda: _hyena_body(hy_ref, cw_ref, cb_ref, bias_ref, kre_c, kim_c, fwd_c, inv_c, o_ref,
                            SEQ, CTX_SEQS_PER_STEP),
        lambda: _hyena_body(hy_ref, cw_ref, cb_ref, bias_ref, kre_l, kim_l, fwd_l, inv_l, o_ref,
                            DEC_SEQ, 1))


def _hyena_body(hy_ref, cw_ref, cb_ref, bias_ref, kre_ref, kim_ref, fwd_ref, inv_ref, o_ref, L, seqs):
    first = lax.broadcasted_iota(jnp.int32, (L, HY_W), 0) == 0

    def long_conv(u, o):
        uf = _dot(fwd_ref[...], u)
        ure, uim = uf[0:L], uf[L:2 * L]
        kre = kre_ref[:, o * HY_W:(o + 1) * HY_W]
        kim = kim_ref[:, o * HY_W:(o + 1) * HY_W]
        yre = jnp.where(first, ure * kre, ure * kre - uim * kim)
        yim = jnp.where(first, uim * kim, ure * kim + uim * kre)
        y = _dot(inv_ref[...], jnp.concatenate([yre, yim], 0))
        return y + u * bias_ref[o:o + 1, :]

    for g in range(seqs):
        sl = slice(g * L, (g + 1) * L)
        x = hy_ref[sl, :].astype(F32)
        rows = lax.broadcasted_iota(jnp.int32, x.shape, 0)
        prev = jnp.where(rows == 0, 0.0, pltpu.roll(x, 1, axis=0))
        nxt = jnp.where(rows == L - 1, 0.0, pltpu.roll(x, L - 1, axis=0))
        z = prev * cw_ref[0:1, :] + x * cw_ref[1:2, :] + nxt * cw_ref[2:3, :] + cb_ref[...]
        v, x1, x2 = z[:, 0:HY_W], z[:, HY_W:2 * HY_W], z[:, 2 * HY_W:3 * HY_W]
        u = x1 * long_conv(v, 0)
        o_ref[sl, :] = (x2 * long_conv(u, 1)).astype(o_ref.dtype)


def _const_spec(shape):
    return pl.BlockSpec(shape, lambda i: (0,) * len(shape))


def _hyena(z_main, layer, conv_w, conv_b, bias, filters, dft):
    tables, table_specs = [], []
    for L in (SEQ, DEC_SEQ):
        tables += [*filters[L], *dft[L]]
        table_specs += [_const_spec((L, 2 * HY_W)), _const_spec((L, 2 * HY_W)),
                        _const_spec((2 * L, L)), _const_spec((L, 2 * L))]
    return pl.pallas_call(
        _hyena_kernel,
        out_shape=jax.ShapeDtypeStruct((T_ALL, HY_W), BF16),
        grid=(MIXER_STEPS,),
        in_specs=[
            pl.BlockSpec((STEP_ROWS, 3 * HY_W), lambda i: (i, 0)),
            pl.BlockSpec((None, 3, 3 * HY_W), lambda i: (layer, 0, 0)),
            pl.BlockSpec((None, 1, 3 * HY_W), lambda i: (layer, 0, 0)),
            pl.BlockSpec((None, 2, HY_W), lambda i: (layer, 0, 0)),
        ] + table_specs,
        out_specs=pl.BlockSpec((STEP_ROWS, HY_W), lambda i: (i, 0)),
        compiler_params=_params("parallel"),
        name="hyena",
    )(z_main, conv_w, conv_b.reshape(DEPTH, 1, 3 * HY_W), bias, *tables)


def _win_masks():
    lane = lax.broadcasted_iota(jnp.int32, (1, LANE), 1)
    return lane < WIN_HD, lane >= WIN_HD


def _win_head_operands(q, k, v, h):
    lo_mask, hi_mask = _win_masks()
    col = h // 2
    lo = h % 2 == 0
    q128 = jnp.where(lo_mask if lo else hi_mask, q[:, col * LANE:(col + 1) * LANE], 0.0)
    swap = h in (1, 2)
    if swap:
        k = pltpu.roll(k, WIN_HD, axis=1)
        v = pltpu.roll(v, WIN_HD, axis=1)
    return q128, k, v, lo


def _win_kernel(sink_ref, q_ref, kv_ref, ck_ref, cv_ref, c_ref, su_ref, sd_ref, o_ref, *, layer):
    _group_step(
        lambda: _win_ctx_body(sink_ref, q_ref, kv_ref, o_ref, layer),
        lambda: _win_lat_body(sink_ref, q_ref, kv_ref, ck_ref, cv_ref, c_ref, su_ref, sd_ref, o_ref, layer))


def _win_ctx_body(sink_ref, q_ref, kv_ref, o_ref, layer):
    lo_mask, hi_mask = _win_masks()
    qscale = WIN_HD ** -0.5 * LOG2E
    for g in range(CTX_SEQS_PER_STEP):
        sl = slice(g * SEQ, (g + 1) * SEQ)
        q = q_ref[sl, :].astype(F32) * qscale
        k = kv_ref[sl, 0:LANE].astype(F32)
        v = kv_ref[sl, LANE:2 * LANE].astype(F32)
        cols = []
        for col in range(2):
            acc = None
            for h in (2 * col, 2 * col + 1):
                q128, kk, vv, lo = _win_head_operands(q, k, v, h)
                s = _dot_nt(q128, kk)
                sink = sink_ref[layer, h] * LOG2E
                m = jnp.maximum(jnp.max(s, -1, keepdims=True), sink)
                p = jnp.exp2(s - m)
                den = jnp.sum(p, -1, keepdims=True) + jnp.exp2(sink - m)
                o = _dot(p, vv) / den
                o = jnp.where(lo_mask if lo else hi_mask, o, 0.0)
                acc = o if acc is None else acc + o
            cols.append(acc)
        o_ref[sl, :] = jnp.concatenate(cols, 1).astype(o_ref.dtype)


def _win_lat_body(sink_ref, q_ref, kv_ref, ck_ref, cv_ref, c_ref, su_ref, sd_ref, o_ref, layer):
    L = DEC_SEQ
    half = WIN_HD // 2
    c, su, sd = c_ref[...], su_ref[...], sd_ref[...]
    qscale = WIN_HD ** -0.5 * LOG2E
    q = jnp.concatenate(
        [_rope128(q_ref[:, i * LANE:(i + 1) * LANE].astype(F32), c, su, sd, half) for i in range(2)],
        1) * qscale
    k = _rope128(kv_ref[:, 0:LANE].astype(F32), c, su, sd, half)
    v = kv_ref[:, LANE:2 * LANE].astype(F32)
    ck = ck_ref[...]
    cv = cv_ref[...]
    lo_mask, hi_mask = _win_masks()
    nb = L // CHUNK
    assert WINDOW == CHUNK
    rr = lax.broadcasted_iota(jnp.int32, (CHUNK, CHUNK), 0)
    cc = lax.broadcasted_iota(jnp.int32, (CHUNK, CHUNK), 1)
    band = {-1: jnp.where(cc >= rr, 0.0, NEG), 0: jnp.zeros((CHUNK, CHUNK), F32),
            1: jnp.where(cc <= rr, 0.0, NEG)}
    cols = []
    for col in range(2):
        acc_blocks = [None] * nb
        for h in (2 * col, 2 * col + 1):
            q128, kk, vv, lo = _win_head_operands(q, k, v, h)
            _, ckk, cvv, _ = _win_head_operands(q, ck, cv, h)
            sink = sink_ref[layer, h] * LOG2E
            for n in range(nb):
                blocks = [d for d in (-1, 0, 1) if 0 <= n + d < nb]
                k0 = (n + blocks[0]) * CHUNK
                k1 = (n + blocks[-1] + 1) * CHUNK
                qn = q128[n * CHUNK:(n + 1) * CHUNK]
                s_loc = _dot_nt(qn, kk[k0:k1]) + jnp.concatenate([band[d] for d in blocks], 1)
                s_ctx = _dot_nt(qn, ckk)
                m = jnp.maximum(jnp.maximum(jnp.max(s_loc, -1, keepdims=True),
                                            jnp.max(s_ctx, -1, keepdims=True)), sink)
                p_loc = jnp.exp2(s_loc - m)
                p_ctx = jnp.exp2(s_ctx - m)
                den = (jnp.sum(p_loc, -1, keepdims=True) + jnp.sum(p_ctx, -1, keepdims=True)
                       + jnp.exp2(sink - m))
                o = (_dot(p_loc, vv[k0:k1]) + _dot(p_ctx, cvv)) / den
                o = jnp.where(lo_mask if lo else hi_mask, o, 0.0)
                acc_blocks[n] = o if acc_blocks[n] is None else acc_blocks[n] + o
        cols.append(jnp.concatenate(acc_blocks, 0))
    o_ref[...] = jnp.concatenate(cols, 1).astype(o_ref.dtype)


def _win(z_main, sink, cache_k, cache_v, layer):
    c, su, sd, _ = _rope_tables(DEC_SEQ, WIN_HD, LANE)
    tab = _const_spec((DEC_SEQ, LANE))
    cache = pl.BlockSpec((None, None, PAST_LEN, LANE), lambda i: (_lat_index(i), layer, 0, 0))
    return pl.pallas_call(
        functools.partial(_win_kernel, layer=layer),
        out_shape=jax.ShapeDtypeStruct((T_ALL, WIN_HEADS * WIN_HD), BF16),
        grid=(MIXER_STEPS,),
        in_specs=[
            pl.BlockSpec(memory_space=pltpu.SMEM),
            pl.BlockSpec((STEP_ROWS, 256), lambda i: (i, COL_WQ // 256)),
            pl.BlockSpec((STEP_ROWS, 256), lambda i: (i, COL_WK // 256)),
            cache, cache, tab, tab, tab,
        ],
        out_specs=pl.BlockSpec((STEP_ROWS, 256), lambda i: (i, 0)),
        compiler_params=_params("parallel"),
        name="win",
    )(sink, z_main, z_main, cache_k, cache_v, jnp.asarray(c), jnp.asarray(su), jnp.asarray(sd))


def _ret_kernel(*refs, layer):
    _group_step(lambda: _ret_body(*refs, L=SEQ, layer=layer, ctx=True),
                lambda: _ret_body(*refs, L=DEC_SEQ, layer=layer, ctx=False))


def _ret_body(df_ref, db_ref, q_ref, k_ref, v0_ref, v1_ref, g0_ref, g1_ref, s0f_ref, s0b_ref,
              o_ref, sf_out, sb_out, s_ref, cross_ref, *, L, layer, ctx):
    seqs = STEP_ROWS // L
    if not ctx:
        sf_out[...] = jnp.zeros_like(sf_out)
        sb_out[...] = jnp.zeros_like(sb_out)
    C = CHUNK
    nc = L // C
    H = RET_HEADS
    qw = H * RET_DK
    vw = H * RET_DV

    def lane_table(width, per_head, fn):
        pos = lax.broadcasted_iota(jnp.int32, (C, per_head), 0).astype(F32)
        return jnp.concatenate([fn(h, pos) for h in range(H)], 1)

    def log_gamma(ref, h):
        d = jnp.full((1, 1), ref[layer, h], F32)
        return jnp.log(jax.nn.sigmoid(d))

    lgf = [log_gamma(df_ref, h) for h in range(H)]
    lgb = [log_gamma(db_ref, h) for h in range(H)]

    def tables(lg, reverse):
        if reverse:
            dq = lane_table(vw, RET_DV, lambda h, pos: jnp.exp((C - pos) * lg[h]))
            dk = lane_table(qw, RET_DK, lambda h, pos: jnp.exp(pos * lg[h]))
        else:
            dq = lane_table(vw, RET_DV, lambda h, pos: jnp.exp((pos + 1.0) * lg[h]))
            dk = lane_table(qw, RET_DK, lambda h, pos: jnp.exp((C - 1.0 - pos) * lg[h]))
        dc = jnp.concatenate([jnp.broadcast_to(jnp.exp(C * lg[h]), (1, RET_DV)) for h in range(H)], 1)
        return dq, dk, dc

    tab_f = tables(lgf, False)
    tab_b = tables(lgb, True)
    ii = lax.broadcasted_iota(jnp.int32, (C, C), 0)
    jj = lax.broadcasted_iota(jnp.int32, (C, C), 1)
    diff = (ii - jj).astype(F32)
    dmats = [jnp.where(diff >= 0, jnp.exp(jnp.maximum(diff, 0.0) * lgf[h]), 0.0)
             + jnp.where(diff <= 0, jnp.exp(jnp.maximum(-diff, 0.0) * lgb[h]), 0.0) for h in range(H)]
    dmat_stack = jnp.concatenate(dmats, 0)
    lane_q = lax.broadcasted_iota(jnp.int32, (1, qw), 1) // RET_DK

    srow = lax.broadcasted_iota(jnp.int32, (qw, vw), 0) // RET_DK
    scol = lax.broadcasted_iota(jnp.int32, (qw, vw), 1) // RET_DV
    diag = srow == scol

    for g in range(seqs):
        base = g * L
        rows_all = slice(base, base + L)
        q_all = q_ref[rows_all, :].astype(F32)
        k_all = k_ref[rows_all, :].astype(F32) * (RET_DK ** -0.5)
        v_all = jnp.concatenate([v0_ref[rows_all, :], v1_ref[rows_all, :]], 1).astype(F32)
        g_all = jnp.concatenate([g0_ref[rows_all, :], g1_ref[rows_all, :]], 1).astype(F32)

        def scan(tabs, reverse, s0_ref, s_out):
            dq, dk, dc = tabs
            if s0_ref is not None:
                s_ref[g] = jnp.zeros((qw, vw), F32)
                for h in range(H):
                    s_ref[g, h * RET_DK:(h + 1) * RET_DK, h * RET_DV:(h + 1) * RET_DV] = s0_ref[h]
            order = range(nc - 1, -1, -1) if reverse else range(nc)
            for step, ci in enumerate(order):
                sl = slice(ci * C, (ci + 1) * C)
                rs = slice(base + ci * C, base + (ci + 1) * C)
                qc, kc, vc = q_all[sl], k_all[sl], v_all[sl]
                upd = jnp.where(diag, _dot_tn(kc * dk, vc), 0.0)
                if s0_ref is None and step == 0:
                    if not reverse:
                        cross_ref[rs, :] = jnp.zeros((C, vw), F32)
                    s_ref[g] = upd
                    continue
                st = s_ref[g]
                cross = _dot(qc, st) * dq
                if reverse:
                    cross_ref[rs, :] = cross_ref[rs, :] + cross
                else:
                    cross_ref[rs, :] = cross
                s_ref[g] = st * dc + upd
            if s_out is not None:
                for h in range(H):
                    s_out[g, h] = s_ref[g, h * RET_DK:(h + 1) * RET_DK, h * RET_DV:(h + 1) * RET_DV]

        scan(tab_f, False, None if ctx else s0f_ref, sf_out if ctx else None)
        scan(tab_b, True, None if ctx else s0b_ref, sb_out if ctx else None)

        for ci in range(nc):
            sl = slice(ci * C, (ci + 1) * C)
            rs = slice(base + ci * C, base + (ci + 1) * C)
            q_stack = jnp.concatenate([jnp.where(lane_q == h, q_all[sl], 0.0) for h in range(H)], 0)
            att = _dot_nt(q_stack, k_all[sl]) * dmat_stack
            ov = _dot(att, v_all[sl])
            for h in range(H):
                hv = slice(h * RET_DV, (h + 1) * RET_DV)
                o = ov[h * C:(h + 1) * C, hv] + cross_ref[rs, hv]
                gt = g_all[sl, hv]
                o_ref[rs, hv] = ((gt * jax.nn.sigmoid(gt)) * _layer_norm(o)).astype(o_ref.dtype)


def _retention(z_main, dec_f, dec_b, s0f, s0b, layer):
    def zcol(col):
        return pl.BlockSpec((STEP_ROWS, 256), lambda i: (i, col // 256))

    smem = pl.BlockSpec(memory_space=pltpu.SMEM)
    z_specs = [zcol(COL_RQ), zcol(COL_RK), zcol(COL_RV), zcol(COL_RV + 256),
               zcol(COL_RG), zcol(COL_RG + 256)]
    s0_spec = pl.BlockSpec((None, None, RET_HEADS, RET_DK, RET_DV),
                           lambda i: (_lat_index(i), layer, 0, 0, 0))
    st_shape = jax.ShapeDtypeStruct((MIXER_STEPS * CTX_SEQS_PER_STEP, RET_HEADS, RET_DK, RET_DV), F32)
    st_spec = pl.BlockSpec((CTX_SEQS_PER_STEP, RET_HEADS, RET_DK, RET_DV), lambda i: (i, 0, 0, 0))
    return pl.pallas_call(
        functools.partial(_ret_kernel, layer=layer),
        out_shape=(jax.ShapeDtypeStruct((T_ALL, RET_HEADS * RET_DV), BF16), st_shape, st_shape),
        grid=(MIXER_STEPS,),
        in_specs=[smem, smem] + z_specs + [s0_spec, s0_spec],
        out_specs=(pl.BlockSpec((STEP_ROWS, RET_HEADS * RET_DV), lambda i: (i, 0)), st_spec, st_spec),
        scratch_shapes=[pltpu.VMEM((CTX_SEQS_PER_STEP, RET_HEADS * RET_DK, RET_HEADS * RET_DV), F32),
                        pltpu.VMEM((STEP_ROWS, RET_HEADS * RET_DV), F32)],
        compiler_params=_params("parallel"),
        name="retention",
    )(dec_f, dec_b, *([z_main] * 6), s0f, s0b)


def _rms_norm(x, g):
    return x * lax.rsqrt(jnp.mean(x * x, -1, keepdims=True) + RMS_EPS) * g


def _mla_keys(kn, kr):
    lane_r = lax.broadcasted_iota(jnp.int32, (1, LANE), 1)
    return jnp.concatenate([kn, jnp.where(lane_r < MLA_ROPE, kr, 0.0)], 1).astype(BF16)


def _mla_attend(qn, qr, k_cat, vv, o_ref, row0):
    qscale = (MLA_NOPE + MLA_ROPE) ** -0.5 * LOG2E
    lane_n = lax.broadcasted_iota(jnp.int32, (1, MLA_HEADS * MLA_NOPE), 1) // MLA_NOPE
    lane_r = lax.broadcasted_iota(jnp.int32, (1, LANE), 1)
    qn = qn * qscale
    qr = qr * qscale
    lq = qn.shape[0]
    heads = []
    for h in range(MLA_HEADS):
        qnh = jnp.where(lane_n == h, qn, 0.0)
        qrh = qr if h == 0 else pltpu.roll(qr, LANE - h * MLA_ROPE, axis=1)
        qrh = jnp.where(lane_r < MLA_ROPE, qrh, 0.0)
        heads.append(jnp.concatenate([qnh, qrh], 1).astype(BF16))
    s = _dot_nt(jnp.concatenate(heads, 0), k_cat)
    m = jnp.max(s, -1, keepdims=True)
    p = jnp.exp2(s - m)
    den = jnp.sum(p, -1, keepdims=True)
    o = _dot(p, vv) / den
    acc = None
    for h in range(MLA_HEADS):
        oh = jnp.where(lane_n == h, o[h * lq:(h + 1) * lq], 0.0)
        acc = oh if acc is None else acc + oh
    o_ref[row0:row0 + lq, :] = acc.astype(o_ref.dtype)


MLA_QN = MLA_HEADS * MLA_NOPE


def _mla_kernel(cq_ref, ckv_ref, kr_ref, cckv_ref, ckr_ref, c_ref, su_ref, sd_ref,
                qg_ref, kg_ref, wq_ref, wkv_ref, o_ref, ckvn_ref):
    weights = (qg_ref, kg_ref, wq_ref, wkv_ref)
    _group_step(
        lambda: _mla_ctx_body(cq_ref, ckv_ref, kr_ref, *weights, o_ref, ckvn_ref),
        lambda: _mla_lat_body(cq_ref, ckv_ref, kr_ref, cckv_ref, ckr_ref, c_ref, su_ref, sd_ref,
                              *weights, o_ref, ckvn_ref))


def _mla_ctx_body(cq_ref, ckv_ref, kr_ref, qg_ref, kg_ref, wq_ref, wkv_ref, o_ref, ckvn_ref):
    q = _dot(_rms_norm(cq_ref[...].astype(F32), qg_ref[...]), wq_ref[...])
    qn, qr = q[:, 0:MLA_QN], q[:, MLA_QN:]
    ckvn = _rms_norm(ckv_ref[...].astype(F32), kg_ref[...])
    ckvn_ref[...] = ckvn
    kv = _dot(ckvn, wkv_ref[...])
    k_cat = _mla_keys(kv[:, 0:MLA_QN], kr_ref[...].astype(F32))
    vv = kv[:, MLA_QN:].astype(BF16)
    for g in range(CTX_SEQS_PER_STEP):
        sl = slice(g * SEQ, (g + 1) * SEQ)
        _mla_attend(qn[sl], qr[sl], k_cat[sl], vv[sl], o_ref, g * SEQ)


def _mla_lat_body(cq_ref, ckv_ref, kr_ref, cckv_ref, ckr_ref, c_ref, su_ref, sd_ref,
                  qg_ref, kg_ref, wq_ref, wkv_ref, o_ref, ckvn_ref):
    half = MLA_ROPE // 2
    c, su, sd = c_ref[...], su_ref[...], sd_ref[...]
    q = _dot(_rms_norm(cq_ref[...].astype(F32), qg_ref[...]), wq_ref[...])
    qn = q[:, 0:MLA_QN]
    qr = _rope128(q[:, MLA_QN:], c, su, sd, half)
    ckvn = _rms_norm(ckv_ref[...].astype(F32), kg_ref[...])
    ckvn_ref[...] = ckvn
    ckv_all = jnp.concatenate([ckvn, cckv_ref[...]], 0)
    kv = _dot(ckv_all, wkv_ref[...])
    vv = kv[:, MLA_QN:].astype(BF16)
    kr = jnp.concatenate([_rope128(kr_ref[...].astype(F32), c, su, sd, half), ckr_ref[...]], 0)
    k_cat = _mla_keys(kv[:, 0:MLA_QN], kr)
    rows_per_call = 256
    for n in range(DEC_SEQ // rows_per_call):
        rows = slice(n * rows_per_call, (n + 1) * rows_per_call)
        _mla_attend(qn[rows], qr[rows], k_cat, vv, o_ref, n * rows_per_call)


def _mla(z_main, cache_ckv, cache_kr_pad, weights, layer):
    c, su, sd, _ = _rope_tables(DEC_SEQ, MLA_ROPE, LANE)
    tab = _const_spec((DEC_SEQ, LANE))
    cache = pl.BlockSpec((None, None, PAST_LEN, LANE), lambda i: (_lat_index(i), layer, 0, 0))

    def weight(*shape):
        return pl.BlockSpec((None,) + shape, lambda i: (layer, 0, 0))

    return pl.pallas_call(
        _mla_kernel,
        out_shape=(jax.ShapeDtypeStruct((T_ALL, MLA_HEADS * MLA_V), BF16),
                   jax.ShapeDtypeStruct((T_ALL, MLA_KV_LORA), F32)),
        grid=(MIXER_STEPS,),
        in_specs=[
            pl.BlockSpec((STEP_ROWS, 256), lambda i: (i, COL_CQ // 256)),
            pl.BlockSpec((STEP_ROWS, LANE), lambda i: (i, COL_CKV // LANE)),
            pl.BlockSpec((STEP_ROWS, LANE), lambda i: (i, COL_KROPE // LANE)),
            cache, cache, tab, tab, tab,
            weight(1, MLA_Q_LORA), weight(1, MLA_KV_LORA),
            weight(MLA_Q_LORA, MLA_QN + MLA_HEADS * MLA_ROPE),
            weight(MLA_KV_LORA, MLA_QN + MLA_HEADS * MLA_V),
        ],
        out_specs=(pl.BlockSpec((STEP_ROWS, 256), lambda i: (i, 0)),
                   pl.BlockSpec((STEP_ROWS, MLA_KV_LORA), lambda i: (i, 0))),
        compiler_params=_params("parallel"),
        name="mla",
    )(z_main, z_main, z_main, cache_ckv, cache_kr_pad,
      jnp.asarray(c), jnp.asarray(su), jnp.asarray(sd), *weights)


def _route(logits_t, rb):
    scores = jax.nn.sigmoid(logits_t)
    biased = scores + rb
    sc = [scores[e:e + 1, :] for e in range(N_EXPERTS)]
    bi = [biased[e:e + 1, :] for e in range(N_EXPERTS)]
    epg = EXPERTS_PER_GROUP
    gsum = []
    for g in range(N_GROUPS):
        v = bi[g * epg:(g + 1) * epg]
        best = None
        for a in range(epg):
            for b in range(a + 1, epg):
                pair = v[a] + v[b]
                best = pair if best is None else jnp.maximum(best, pair)
        gsum.append(best)
    combine = []
    sel = []
    for g in range(N_GROUPS):
        is_best = None
        for g2 in range(N_GROUPS):
            if g2 == g:
                continue
            c = gsum[g] > gsum[g2] if g2 < g else gsum[g] >= gsum[g2]
            is_best = c if is_best is None else jnp.logical_and(is_best, c)
        for a in range(epg):
            e = g * epg + a
            rank = jnp.zeros_like(bi[e])
            for b in range(epg):
                if b == a:
                    continue
                e2 = g * epg + b
                ahead = bi[e2] >= bi[e] if b < a else bi[e2] > bi[e]
                rank = rank + jnp.where(ahead, 1.0, 0.0)
            sel.append(jnp.logical_and(is_best, rank < 2.0))
    wsum = None
    for e in range(N_EXPERTS):
        w = jnp.where(sel[e], sc[e], 0.0)
        wsum = w if wsum is None else wsum + w
    for e in range(N_EXPERTS):
        combine.append(jnp.where(sel[e], ROUTE_SCALE * sc[e] / wsum, 0.0))
    return jnp.concatenate(combine, 0)


MERGE_TILE = 512

MERGE_BRANCH_ROWS = (HY_W, WIN_HEADS * WIN_HD, RET_HEADS * RET_DV, MLA_HEADS * MLA_V)
MERGE_ROWS = sum(MERGE_BRANCH_ROWS) + D_MODEL


def _merge_kernel(ya_ref, yb_ref, yc_ref, yd_ref, gt_ref, xc_ref, xl_ref, m_ref,
                  w_ref, g_ref, b_ref, rw_ref, rb_ref,
                  x1_ref, h2_ref, cmb_ref, *, ctx_tiles, sub_rows):
    D = D_MODEL
    rw = rw_ref[...]
    rw_hi = rw.astype(BF16)
    rw_lo = (rw - rw_hi.astype(F32)).astype(BF16)
    g1 = m_ref[:, 2 * D:3 * D]
    s2 = m_ref[:, 3 * D:4 * D]
    sc2 = m_ref[:, 4 * D:5 * D]
    is_ctx = pl.program_id(0) < ctx_tiles
    offs = np.cumsum((0,) + MERGE_BRANCH_ROWS)
    branches = tuple((y_ref, slice(int(offs[i]), int(offs[i + 1])))
                     for i, y_ref in enumerate((ya_ref, yb_ref, yc_ref, yd_ref)))
    w_out_rows = slice(int(offs[-1]), MERGE_ROWS)
    for r0 in range(0, x1_ref.shape[0], sub_rows):
        rows = slice(r0, r0 + sub_rows)
        merged = None
        for i, (y_ref, w_rows) in enumerate(branches):
            t = gt_ref[rows, i * D:(i + 1) * D] * jnp.dot(
                y_ref[rows, :], w_ref[w_rows, :], preferred_element_type=F32).astype(BF16)
            merged = t if merged is None else merged + t
        out1 = jnp.dot(merged, w_ref[w_out_rows, :], preferred_element_type=F32)
        x = jnp.where(is_ctx, xc_ref[rows, :], xl_ref[rows, :])
        x1 = _layer_norm(ALPHA * x + g1 * out1) * g_ref[...] + b_ref[...]
        x1_ref[rows, :] = x1
        h2 = _layer_norm(x1) * (1.0 + sc2) + s2
        h2_hi = h2.astype(BF16)
        h2_ref[rows, :] = h2_hi
        h2_lo = (h2 - h2_hi.astype(F32)).astype(BF16)
        logits = (jnp.dot(h2_hi, rw_hi, preferred_element_type=F32)
                  + (jnp.dot(h2_lo, rw_hi, preferred_element_type=F32)
                     + jnp.dot(h2_hi, rw_lo, preferred_element_type=F32)))
        combine_t = _route(logits.T[0:N_EXPERTS], rb_ref[...])
        cmb_ref[rows, :] = jnp.concatenate(
            [combine_t, jnp.zeros((LANE - N_EXPERTS, sub_rows), F32)], 0).T


def _merge(ya, yb, yc, yd, gates, x_ctx, x_lat, lat_block0, mods, w_merge, ln1_g, ln1_b,
           router_w, router_b, layer):
    tm = MERGE_TILE
    row = _mod_row(tm)
    D = D_MODEL
    ctx_tiles = T_CTX // tm

    def tile(w):
        return pl.BlockSpec((tm, w), lambda i: (i, 0))

    def weight(k, n):
        return pl.BlockSpec((None, k, n), lambda i: (layer, 0, 0))

    return pl.pallas_call(
        functools.partial(_merge_kernel, ctx_tiles=ctx_tiles, sub_rows=256),
        out_shape=(jax.ShapeDtypeStruct((T_ALL, D), F32),
                   jax.ShapeDtypeStruct((T_ALL, D), BF16),
                   jax.ShapeDtypeStruct((T_ALL, LANE), F32)),
        grid=(T_ALL // tm,),
        in_specs=[
            tile(256), tile(256), tile(512), tile(256), tile(4 * D),
            *_x_specs(tm, ctx_tiles, lat_block0),
            _mod_spec(layer, row),
            weight(MERGE_ROWS, D), weight(1, D), weight(1, D),
            pl.BlockSpec((D, LANE), lambda i: (0, 0)),
            pl.BlockSpec((N_EXPERTS, 1), lambda i: (0, 0)),
        ],
        out_specs=(tile(D), tile(D), tile(LANE)),
        compiler_params=_params("parallel"),
        name="merge",
    )(ya, yb, yc, yd, gates, x_ctx, x_lat, mods, w_merge,
      ln1_g.reshape(DEPTH, 1, D), ln1_b.reshape(DEPTH, 1, D), router_w,
      router_b.reshape(N_EXPERTS, 1))


MOE_EXPERTS_PER_STEP = 2


def _moe_kernel(*refs, next_h):
    if next_h:
        (h_ref, c_ref, x1_ref, m_ref, wg_ref, wu_ref, wd_ref, g_ref, b_ref, mn_ref,
         o_ref, hn_ref, acc_ref) = refs
    else:
        h_ref, c_ref, x1_ref, m_ref, wg_ref, wu_ref, wd_ref, g_ref, b_ref, o_ref, acc_ref = refs
    eg = pl.program_id(1)

    @pl.when(eg == 0)
    def _():
        acc_ref[...] = jnp.zeros_like(acc_ref)

    h = h_ref[...]
    cmb = c_ref[...]
    lane = lax.broadcasted_iota(jnp.int32, cmb.shape, 1)
    hid = []
    for k in range(MOE_EXPERTS_PER_STEP):
        gate = jnp.dot(h, wg_ref[k].astype(BF16), preferred_element_type=F32)
        up = jnp.dot(h, wu_ref[k].astype(BF16), preferred_element_type=F32)
        e = eg * MOE_EXPERTS_PER_STEP + k
        ce = jnp.sum(jnp.where(lane == e, cmb, 0.0), -1, keepdims=True)
        sig = 0.5 * jnp.tanh(0.5 * gate) + 0.5
        hid.append((gate * sig * (up * ce)).astype(BF16))
    wd = wd_ref[...].reshape(MOE_EXPERTS_PER_STEP * D_EXPERT, D_MODEL).astype(BF16)
    acc_ref[...] += jnp.dot(jnp.concatenate(hid, 1), wd, preferred_element_type=F32)

    @pl.when(eg == N_EXPERTS // MOE_EXPERTS_PER_STEP - 1)
    def _():
        g2 = m_ref[:, 5 * D_MODEL:6 * D_MODEL]
        y = _layer_norm(ALPHA * x1_ref[...] + g2 * acc_ref[...])
        y = y * g_ref[...] + b_ref[...]
        o_ref[...] = y
        if next_h:
            s1 = mn_ref[:, 0:D_MODEL]
            sc1 = mn_ref[:, D_MODEL:2 * D_MODEL]
            hn_ref[...] = (_layer_norm(y) * (1.0 + sc1) + s1).astype(hn_ref.dtype)


def _moe(h2, combine, x1, mods, w_gate, w_up, w_down, ln2_g, ln2_b, layer, row0, n_rows,
         next_h=False):
    tm = 1024
    row = _mod_row(tm)
    D = D_MODEL
    t0 = row0 // tm
    eps = MOE_EXPERTS_PER_STEP
    mod_spec = _mod_spec(layer, lambda i, e: row(t0 + i))
    in_specs = [
        pl.BlockSpec((tm, D), lambda i, e: (t0 + i, 0)),
        pl.BlockSpec((tm, LANE), lambda i, e: (t0 + i, 0)),
        pl.BlockSpec((tm, D), lambda i, e: (t0 + i, 0)),
        mod_spec,
        pl.BlockSpec((None, eps, D, D_EXPERT), lambda i, e: (layer, e, 0, 0)),
        pl.BlockSpec((None, eps, D, D_EXPERT), lambda i, e: (layer, e, 0, 0)),
        pl.BlockSpec((None, eps, D_EXPERT, D), lambda i, e: (layer, e, 0, 0)),
        pl.BlockSpec((None, 1, D), lambda i, e: (layer, 0, 0)),
        pl.BlockSpec((None, 1, D), lambda i, e: (layer, 0, 0)),
    ]
    args = [h2, combine, x1, mods, w_gate, w_up, w_down,
            ln2_g.reshape(DEPTH, 1, D), ln2_b.reshape(DEPTH, 1, D)]
    out_shape = jax.ShapeDtypeStruct((n_rows, D), F32)
    out_specs = pl.BlockSpec((tm, D), lambda i, e: (i, 0))
    if next_h:
        in_specs.append(_mod_spec(layer + 1, lambda i, e: row(t0 + i)))
        args.append(mods)
        out_shape = (out_shape, jax.ShapeDtypeStruct((T_ALL, D), BF16))
        assert n_rows == T_ALL
        out_specs = (out_specs, pl.BlockSpec((tm, D), lambda i, e: (i, 0)))
    return pl.pallas_call(
        functools.partial(_moe_kernel, next_h=next_h),
        out_shape=out_shape,
        grid=(n_rows // tm, N_EXPERTS // eps),
        in_specs=in_specs,
        out_specs=out_specs,
        scratch_shapes=[pltpu.VMEM((tm, D), F32)],
        compiler_params=_params("parallel", "arbitrary"),
        name="moe",
    )(*args)


def kernel(x_prompt, x_sample, cache_win_k, cache_win_v, cache_mla_ckv, cache_mla_krope,
           state_ret_fwd, state_ret_bwd, c, c_ctx, w_ada, b_ada, w_in,
           hy_conv_w, hy_conv_b, hy_w1, hy_b1, hy_w2, hy_b2, hy_w3, hy_bias,
           win_sink, ret_decay_fwd, ret_decay_bwd, mla_q_norm, mla_kv_norm, mla_w_uq, mla_w_ukv,
           w_br_a, w_br_b, w_br_c, w_br_d, w_out, ln1_g, ln1_b, ln2_g, ln2_b,
           router_w, router_b, moe_w_gate, moe_w_up, moe_w_down):
    D = D_MODEL
    x_ctx = x_prompt.reshape(T_CTX, D)
    x_lat = x_sample.reshape(T_LAT, D)
    lat_block0 = 0

    mods = _ada_mods(c_ctx, c, w_ada, b_ada)

    w_in_t = jnp.swapaxes(w_in, 1, 2).reshape(DEPTH * IN_COLS, D)
    cache_k = cache_win_k.reshape(DEC_BATCH, DEPTH, PAST_LEN, WIN_KV_HEADS * WIN_HD)
    cache_v = cache_win_v.reshape(DEC_BATCH, DEPTH, PAST_LEN, WIN_KV_HEADS * WIN_HD)
    cache_kr = jnp.pad(cache_mla_krope, ((0, 0), (0, 0), (0, 0), (0, LANE - MLA_ROPE)))

    uq = mla_w_uq.reshape(DEPTH, MLA_Q_LORA, MLA_HEADS, MLA_NOPE + MLA_ROPE)
    ukv = mla_w_ukv.reshape(DEPTH, MLA_KV_LORA, MLA_HEADS, MLA_NOPE + MLA_V)
    mla_weights = (
        mla_q_norm.reshape(DEPTH, 1, MLA_Q_LORA),
        mla_kv_norm.reshape(DEPTH, 1, MLA_KV_LORA),
        jnp.concatenate([uq[..., :MLA_NOPE].reshape(DEPTH, MLA_Q_LORA, MLA_HEADS * MLA_NOPE),
                         uq[..., MLA_NOPE:].reshape(DEPTH, MLA_Q_LORA, MLA_HEADS * MLA_ROPE)], -1),
        jnp.concatenate([ukv[..., :MLA_NOPE].reshape(DEPTH, MLA_KV_LORA, MLA_HEADS * MLA_NOPE),
                         ukv[..., MLA_NOPE:].reshape(DEPTH, MLA_KV_LORA, MLA_HEADS * MLA_V)], -1),
    )

    hy_w1p = jnp.pad(hy_w1, ((0, 0), (0, LANE - HY_EMB), (0, 0)))
    dft = {}
    for L in (SEQ, DEC_SEQ):
        fwd, inv = _dft_tables(L)
        dft[L] = (jnp.asarray(fwd).astype(BF16), jnp.asarray(inv).astype(BF16))
    router_w_pad = jnp.pad(router_w, ((0, 0), (0, LANE - N_EXPERTS)))
    w_merge = jnp.concatenate([w_br_a, w_br_b, w_br_c, w_br_d, w_out], 1).astype(BF16)

    new_k, new_v, new_ckv, new_kr, new_sf, new_sb = [], [], [], [], [], []
    for l in range(DEPTH):
        if l == 0:
            h = _ln_mod(x_ctx, x_lat, mods, l)
        z = _in_proj(h, w_in_t, l, 0, Z_MAIN, Z_MAIN // 2, BF16, gate=False)
        gates = _in_proj(h, w_in_t, l, COL_GATE, 4 * D, D, BF16, gate=True)

        filters = {L: _hy_filters(L, hy_w1p[l], hy_b1[l][None], hy_w2[l], hy_b2[l][None], hy_w3[l],
                                  dft[L][0]) for L in (SEQ, DEC_SEQ)}
        ya = _hyena(z, l, hy_conv_w, hy_conv_b, hy_bias, filters, dft)
        yb = _win(z, win_sink, cache_k, cache_v, l)
        yc, sf, sb = _retention(z, ret_decay_fwd, ret_decay_bwd, state_ret_fwd, state_ret_bwd, l)
        yd, ckvn = _mla(z, cache_mla_ckv, cache_kr, mla_weights, l)

        x1, h2, combine = _merge(ya, yb, yc, yd, gates, x_ctx, x_lat, lat_block0, mods, w_merge,
                                   ln1_g, ln1_b, router_w_pad, router_b, l)
        moe_args = (h2, combine, x1, mods, moe_w_gate, moe_w_up, moe_w_down, ln2_g, ln2_b, l)
        if l + 1 < DEPTH:
            x_ctx, h = _moe(*moe_args, 0, T_ALL, next_h=True)
            x_lat, lat_block0 = x_ctx, T_CTX // MERGE_TILE
        else:
            x_ctx = _moe(*moe_args, 0, T_CTX)
            x_lat = _moe(*moe_args, T_CTX, T_LAT)

        def ctx_cols(col, width):
            return z[:T_CTX, col:col + width].astype(F32)

        new_k.append(ctx_cols(COL_WK, 128).reshape(BATCH, SEQ, WIN_KV_HEADS, WIN_HD))
        new_v.append(ctx_cols(COL_WV, 128).reshape(BATCH, SEQ, WIN_KV_HEADS, WIN_HD))
        new_ckv.append(ckvn[:T_CTX].reshape(BATCH, SEQ, MLA_KV_LORA))
        new_kr.append(ctx_cols(COL_KROPE, MLA_ROPE).reshape(BATCH, SEQ, MLA_ROPE))
        new_sf.append(sf[:BATCH])
        new_sb.append(sb[:BATCH])

    y_prompt = x_ctx.reshape(BATCH, SEQ, D)
    y_sample = x_lat.reshape(DEC_BATCH, DEC_SEQ, D)
    return (y_prompt, y_sample, jnp.stack(new_k, 1), jnp.stack(new_v, 1), jnp.stack(new_ckv, 1),
            jnp.stack(new_kr, 1), jnp.stack(new_sf, 1), jnp.stack(new_sb, 1))
```

```python
import functools
import math

import numpy as np
import jax
import jax.numpy as jnp
from jax import lax
from jax.experimental import pallas as pl
from jax.experimental.pallas import tpu as pltpu

F32 = jnp.float32
BF16 = jnp.bfloat16

D_MODEL = 1024
BATCH = 16
SEQ = 256
DEPTH = 2
DEC_BATCH = 2
DEC_SEQ = 1024
PAST_LEN = 256
GRID_W = 64
CHUNK = 128
ROPE_BASE = 10000.0
NEG = -1e30
LN_EPS = 1e-5
RMS_EPS = 1e-6
LOG2E = math.log2(math.e)

HY_W = 256
HY_BANDS = 16
HY_EMB = 1 + 2 * HY_BANDS
HY_FFN = 64
HY_FAST_DECAY = 0.3
HY_SLOW_DECAY = 1.5
HY_TARGET = 1e-2

WIN_HEADS = 4
WIN_KV_HEADS = 2
WIN_HD = 64
WINDOW = 128

RET_HEADS = 4
RET_DK = 64
RET_DV = 128

MLA_HEADS = 4
MLA_Q_LORA = 256
MLA_KV_LORA = 128
MLA_NOPE = 64
MLA_ROPE = 32
MLA_V = 64

N_EXPERTS = 16
N_GROUPS = 4
EXPERTS_PER_GROUP = N_EXPERTS // N_GROUPS
D_EXPERT = 256
ROUTE_SCALE = 2.5

ALPHA = (2.0 * DEPTH) ** 0.25

T_CTX = BATCH * SEQ
T_LAT = DEC_BATCH * DEC_SEQ
T_ALL = T_CTX + T_LAT

COL_HY = 0
COL_WQ = 768
COL_WK = 1024
COL_WV = 1152
COL_RQ = 1280
COL_RK = 1536
COL_RV = 1792
COL_RG = 2304
COL_CQ = 2816
COL_CKV = 3072
COL_KROPE = 3200
COL_GATE = 3232
IN_COLS = COL_GATE + 4 * D_MODEL
Z_MAIN = 3328

LANE = 128
STEP_ROWS = 1024
CTX_SEQS_PER_STEP = STEP_ROWS // SEQ
CTX_STEPS = T_CTX // STEP_ROWS
MIXER_STEPS = T_ALL // STEP_ROWS
VMEM_LIMIT = 56 * 1024 * 1024


def _params(*sem):
    return pltpu.CompilerParams(dimension_semantics=sem, vmem_limit_bytes=VMEM_LIMIT)


def _dot(a, b):
    return jnp.dot(a.astype(BF16), b.astype(BF16), preferred_element_type=F32)


def _dot_split(a, b):
    a_hi = a.astype(BF16)
    a_lo = (a - a_hi.astype(F32)).astype(BF16)
    b_hi = b.astype(BF16)
    b_lo = (b - b_hi.astype(F32)).astype(BF16)

    def mm(x, y):
        return jnp.dot(x, y, preferred_element_type=F32)

    return mm(a_hi, b_hi) + (mm(a_lo, b_hi) + mm(a_hi, b_lo))


def _dot_nt(a, b):
    return lax.dot_general(a.astype(BF16), b.astype(BF16), (((1,), (1,)), ((), ())),
                           preferred_element_type=F32)


def _dot_tn(a, b):
    return lax.dot_general(a.astype(BF16), b.astype(BF16), (((0,), (0,)), ((), ())),
                           preferred_element_type=F32)


def _layer_norm(x):
    mu = jnp.mean(x, -1, keepdims=True)
    xc = x - mu
    var = jnp.mean(xc * xc, -1, keepdims=True)
    return xc * lax.rsqrt(var + LN_EPS)


def _mod_row(tile_rows):
    def row(i):
        start = i * tile_rows
        return jnp.where(start < T_CTX, 0, 1 + (start - T_CTX) // DEC_SEQ)
    return row


@functools.lru_cache(maxsize=None)
def _dft_tables(L):
    f = np.arange(L, dtype=np.int64)[:, None]
    s = np.arange(L, dtype=np.int64)[None, :]
    ang = np.pi * ((f * s) % (2 * L)).astype(np.float64) / L
    cos = np.cos(ang)
    sin = np.sin(ang)
    alt = np.where(np.arange(L) % 2 == 0, 1.0, -1.0)
    fwd_im = -sin
    fwd_im[0, :] = alt
    fwd = np.concatenate([cos, fwd_im], 0)
    inv_re = cos.T / L
    inv_re[:, 0] = 1.0 / (2 * L)
    inv_im = -sin.T / L
    inv_im[:, 0] = alt / (2 * L)
    inv = np.concatenate([inv_re, inv_im], 1)
    return fwd.astype(np.float32), inv.astype(np.float32)


@functools.lru_cache(maxsize=None)
def _hyena_embedding(L):
    t01 = np.linspace(0.0, 1.0, L, dtype=np.float64)[:, None]
    bands = np.linspace(1e-4, HY_BANDS - 1, HY_BANDS, dtype=np.float64)
    ang = (2.0 * math.pi / L) * np.arange(L, dtype=np.float64)[:, None] * bands[None, :]
    z = np.concatenate([t01, np.cos(ang), -np.sin(ang)], -1)
    zp = np.zeros((L, LANE), np.float64)
    zp[:, :HY_EMB] = z
    deltas = np.abs(np.linspace(math.log(HY_TARGET) / HY_SLOW_DECAY,
                                math.log(HY_TARGET) / HY_FAST_DECAY, HY_W, dtype=np.float64))
    return zp.astype(np.float32), deltas[None, :].astype(np.float32)


@functools.lru_cache(maxsize=None)
def _rope_tables(L, rot_dim, width):
    rows = L // GRID_W
    n_freq = rot_dim // 4
    half = rot_dim // 2
    inv = ROPE_BASE ** (-np.arange(n_freq, dtype=np.float64) / n_freq)
    pos = np.arange(L)
    row = (pos // GRID_W).astype(np.float64)
    col = (pos % GRID_W).astype(np.float64)
    ang = np.concatenate([row[:, None] * inv, col[:, None] * inv], -1)
    cos, sin = np.cos(ang), np.sin(ang)
    zero = np.zeros_like(sin)
    c = np.tile(np.concatenate([cos, cos], -1), (1, width // rot_dim))
    s_up = np.tile(np.concatenate([-sin, zero], -1), (1, width // rot_dim))
    s_dn = np.tile(np.concatenate([zero, sin], -1), (1, width // rot_dim))
    return c.astype(np.float32), s_up.astype(np.float32), s_dn.astype(np.float32), half


def _rope128(x, c, s_up, s_dn, half):
    up = pltpu.roll(x, LANE - half, axis=1)
    dn = pltpu.roll(x, half, axis=1)
    return x * c + up * s_up + dn * s_dn


MOD_ROWS = 1 + DEC_BATCH


def _ada_kernel(ct_ref, w_ref, b_ref, o_ref):
    @pl.when(pl.program_id(1) == 0)
    def _():
        for r in range(MOD_ROWS):
            o_ref[r] = b_ref[...]

    ct = ct_ref[...]
    s = ct * jax.nn.sigmoid(ct)
    tk, n = w_ref.shape
    rows, cols = 64, 4 * LANE
    s_cols = [[s[k0:k0 + rows, r:r + 1] for k0 in range(0, tk, rows)] for r in range(MOD_ROWS)]
    for c0 in range(0, n, cols):
        acc = [None] * MOD_ROWS
        for ki, k0 in enumerate(range(0, tk, rows)):
            wc = w_ref[k0:k0 + rows, c0:c0 + cols]
            for r in range(MOD_ROWS):
                part = jnp.sum((wc * s_cols[r][ki]).reshape(rows // 8, 8, cols), axis=0)
                acc[r] = part if acc[r] is None else acc[r] + part
        for r in range(MOD_ROWS):
            o_ref[r, :, c0:c0 + cols] += jnp.sum(acc[r], axis=0, keepdims=True)


def _ada_mods(c_ctx, c, w_ada, b_ada):
    tk = 256
    n = 6 * D_MODEL
    c_cols = jnp.concatenate([c_ctx[:, None], c.T], 1)
    return pl.pallas_call(
        _ada_kernel,
        out_shape=jax.ShapeDtypeStruct((DEPTH, MOD_ROWS, 1, n), F32),
        grid=(DEPTH, D_MODEL // tk),
        in_specs=[
            pl.BlockSpec((tk, MOD_ROWS), lambda l, k: (k, 0)),
            pl.BlockSpec((None, tk, n), lambda l, k: (l, k, 0)),
            pl.BlockSpec((None, 1, n), lambda l, k: (l, 0, 0)),
        ],
        out_specs=pl.BlockSpec((None, MOD_ROWS, 1, n), lambda l, k: (l, 0, 0, 0)),
        compiler_params=_params("parallel", "arbitrary"),
        name="ada_mods",
    )(c_cols, w_ada, b_ada.reshape(DEPTH, 1, n))


def _mod_spec(layer, row_of_step):
    return pl.BlockSpec((None, None, 1, 6 * D_MODEL), lambda *g: (layer, row_of_step(*g), 0, 0))


def _lnmod_kernel(xc_ref, xl_ref, m_ref, h_ref, *, ctx_tiles):
    x = jnp.where(pl.program_id(0) < ctx_tiles, xc_ref[...], xl_ref[...])
    y = _layer_norm(x)
    s1 = m_ref[:, 0:D_MODEL]
    sc1 = m_ref[:, D_MODEL:2 * D_MODEL]
    h_ref[...] = (y * (1.0 + sc1) + s1).astype(h_ref.dtype)


def _x_specs(tm, ctx_tiles, lat_block0=0):
    return [pl.BlockSpec((tm, D_MODEL), lambda i: (jnp.minimum(i, ctx_tiles - 1), 0)),
            pl.BlockSpec((tm, D_MODEL), lambda i: (jnp.maximum(i - ctx_tiles, 0) + lat_block0, 0))]


def _ln_mod(x_ctx, x_lat, mods, layer):
    tm = 512
    ctx_tiles = T_CTX // tm
    return pl.pallas_call(
        functools.partial(_lnmod_kernel, ctx_tiles=ctx_tiles),
        out_shape=jax.ShapeDtypeStruct((T_ALL, D_MODEL), BF16),
        grid=(T_ALL // tm,),
        in_specs=_x_specs(tm, ctx_tiles) + [_mod_spec(layer, _mod_row(tm))],
        out_specs=pl.BlockSpec((tm, D_MODEL), lambda i: (i, 0)),
        compiler_params=_params("parallel"),
        name="ln_mod",
    )(x_ctx, x_lat, mods)


def _proj_kernel(h_ref, w_ref, o_ref, wb_ref, *, gate):
    @pl.when(pl.program_id(1) == 0)
    def _():
        wb_ref[...] = w_ref[...].T.astype(BF16)

    if not gate:
        o_ref[...] = jnp.dot(h_ref[...], wb_ref[...], preferred_element_type=F32).astype(o_ref.dtype)
        return
    sub = 4 * LANE
    for c0 in range(0, o_ref.shape[1], sub):
        r = jnp.dot(h_ref[...], wb_ref[:, c0:c0 + sub], preferred_element_type=F32)
        rb = r.astype(o_ref.dtype)
        o_ref[:, c0:c0 + sub] = 0.5 * jnp.tanh(0.5 * rb) + 0.5


def _in_proj(h, w_t, layer, col0, n_cols, tn, out_dtype, gate):
    tm = 2048
    return pl.pallas_call(
        functools.partial(_proj_kernel, gate=gate),
        out_shape=jax.ShapeDtypeStruct((T_ALL, n_cols), out_dtype),
        grid=(n_cols // tn, T_ALL // tm),
        in_specs=[
            pl.BlockSpec((tm, D_MODEL), lambda j, i: (i, 0)),
            pl.BlockSpec((pl.Element(tn), pl.Element(D_MODEL)),
                         lambda j, i: (pl.multiple_of(layer * IN_COLS + col0 + j * tn, 8), 0)),
        ],
        out_specs=pl.BlockSpec((tm, tn), lambda j, i: (i, j)),
        scratch_shapes=[pltpu.VMEM((D_MODEL, tn), BF16)],
        compiler_params=_params("parallel", "arbitrary"),
        name="gate_proj" if gate else "in_proj",
    )(h, w_t)


def _hy_filter_kernel(z_ref, dl_ref, w1_ref, b1_ref, w2_ref, b2_ref, w3_ref, fwd_ref,
                      kre_ref, kim_ref, knq_ref, *, L):
    z = z_ref[...]
    a = jnp.sin(_dot_split(z, w1_ref[...]) + b1_ref[...])
    a = jnp.sin(_dot_split(a, w2_ref[...]) + b2_ref[...])
    h = _dot_split(a, w3_ref[...])
    decay = jnp.exp(-z[:, 0:1] * dl_ref[...])
    not_first = lax.broadcasted_iota(jnp.int32, (L, HY_W), 0) > 0
    sums, diffs = [], []
    for o in range(2):
        fw = h[:, (2 * o) * HY_W:(2 * o + 1) * HY_W] * decay
        bw = jnp.where(not_first, h[:, (2 * o + 1) * HY_W:(2 * o + 2) * HY_W] * decay, 0.0)
        sums.append(fw + bw)
        diffs.append(fw - bw)
    p = _dot(fwd_ref[0:L + 16, :], jnp.concatenate(sums, 1))
    q = _dot(fwd_ref[L:2 * L, :], jnp.concatenate(diffs, 1))
    first = lax.broadcasted_iota(jnp.int32, (L, 2 * HY_W), 0) == 0
    kre_ref[...] = p[0:L]
    kim_ref[...] = jnp.where(first, 0.0, q)
    knq_ref[...] = jnp.where(first, p[L:L + 1], p[0:L])


def _hy_filters(L, w1p, b1, w2, b2, w3, fwd):
    zemb, deltas = _hyena_embedding(L)
    out = jax.ShapeDtypeStruct((L, 2 * HY_W), F32)
    return pl.pallas_call(
        functools.partial(_hy_filter_kernel, L=L),
        out_shape=(out, out, out),
        compiler_params=pltpu.CompilerParams(vmem_limit_bytes=VMEM_LIMIT),
        name=f"hy_filters_{L}",
    )(jnp.asarray(zemb), jnp.asarray(deltas), w1p, b1, w2, b2, w3, fwd)


def _group_step(ctx_body, lat_body):
    i = pl.program_id(0)
    pl.when(i < CTX_STEPS)(ctx_body)
    pl.when(i >= CTX_STEPS)(lat_body)


def _lat_index(i):
    return jnp.maximum(i - CTX_STEPS, 0)


def _hyena_kernel(hy_ref, cw_ref, cb_ref, bias_ref, kre_c, kim_c, knq_c, fwd_c, inv_c,
                  kre_l, kim_l, knq_l, fwd_l, inv_l, o_ref):
    _group_step(
        lambda: _hyena_body(hy_ref, cw_ref, cb_ref, bias_ref, kre_c, kim_c, knq_c, fwd_c, inv_c, o_ref,
                            SEQ, CTX_SEQS_PER_STEP),
        lambda: _hyena_body(hy_ref, cw_ref, cb_ref, bias_ref, kre_l, kim_l, knq_l, fwd_l, inv_l, o_ref,
                            DEC_SEQ, 1))


def _hyena_body(hy_ref, cw_ref, cb_ref, bias_ref, kre_ref, kim_ref, knq_ref, fwd_ref, inv_ref, o_ref,
                L, seqs):
    def long_conv(u, o):
        uf = _dot(fwd_ref[...], u)
        ure, uim = uf[0:L], uf[L:2 * L]
        cols = slice(o * HY_W, (o + 1) * HY_W)
        kre, kim, knq = kre_ref[:, cols], kim_ref[:, cols], knq_ref[:, cols]
        yre = ure * kre - uim * kim
        yim = ure * kim + uim * knq
        y = _dot(inv_ref[...], jnp.concatenate([yre, yim], 0))
        return y + u * bias_ref[o:o + 1, :]

    for g in range(seqs):
        sl = slice(g * L, (g + 1) * L)
        x = hy_ref[sl, :].astype(F32)
        rows = lax.broadcasted_iota(jnp.int32, x.shape, 0)
        prev = jnp.where(rows == 0, 0.0, pltpu.roll(x, 1, axis=0))
        nxt = jnp.where(rows == L - 1, 0.0, pltpu.roll(x, L - 1, axis=0))
        z = prev * cw_ref[0:1, :] + x * cw_ref[1:2, :] + nxt * cw_ref[2:3, :] + cb_ref[...]
        v, x1, x2 = z[:, 0:HY_W], z[:, HY_W:2 * HY_W], z[:, 2 * HY_W:3 * HY_W]
        u = x1 * long_conv(v, 0)
        o_ref[sl, :] = (x2 * long_conv(u, 1)).astype(o_ref.dtype)


def _const_spec(shape):
    return pl.BlockSpec(shape, lambda i: (0,) * len(shape))


def _hyena(z_main, layer, conv_w, conv_b, bias, filters, dft):
    tables, table_specs = [], []
    for L in (SEQ, DEC_SEQ):
        tables += [*filters[L], *dft[L]]
        table_specs += [_const_spec((L, 2 * HY_W))] * 3 + [_const_spec((2 * L, L)), _const_spec((L, 2 * L))]
    return pl.pallas_call(
        _hyena_kernel,
        out_shape=jax.ShapeDtypeStruct((T_ALL, HY_W), BF16),
        grid=(MIXER_STEPS,),
        in_specs=[
            pl.BlockSpec((STEP_ROWS, 3 * HY_W), lambda i: (i, 0)),
            pl.BlockSpec((None, 3, 3 * HY_W), lambda i: (layer, 0, 0)),
            pl.BlockSpec((None, 1, 3 * HY_W), lambda i: (layer, 0, 0)),
            pl.BlockSpec((None, 2, HY_W), lambda i: (layer, 0, 0)),
        ] + table_specs,
        out_specs=pl.BlockSpec((STEP_ROWS, HY_W), lambda i: (i, 0)),
        compiler_params=_params("parallel"),
        name="hyena",
    )(z_main, conv_w, conv_b.reshape(DEPTH, 1, 3 * HY_W), bias, *tables)


def _win_masks():
    lane = lax.broadcasted_iota(jnp.int32, (1, LANE), 1)
    return lane < WIN_HD, lane >= WIN_HD


def _win_head_operands(q, k, v, h):
    lo_mask, hi_mask = _win_masks()
    col = h // 2
    lo = h % 2 == 0
    q128 = jnp.where(lo_mask if lo else hi_mask, q[:, col * LANE:(col + 1) * LANE], 0.0)
    swap = h in (1, 2)
    if swap:
        k = pltpu.roll(k, WIN_HD, axis=1)
        v = pltpu.roll(v, WIN_HD, axis=1)
    return q128, k, v, lo


def _win_kernel(sink_ref, q_ref, kv_ref, ck_ref, cv_ref, c_ref, su_ref, sd_ref, o_ref, *, layer):
    _group_step(
        lambda: _win_ctx_body(sink_ref, q_ref, kv_ref, o_ref, layer),
        lambda: _win_lat_body(sink_ref, q_ref, kv_ref, ck_ref, cv_ref, c_ref, su_ref, sd_ref, o_ref, layer))


def _win_ctx_body(sink_ref, q_ref, kv_ref, o_ref, layer):
    lo_mask, hi_mask = _win_masks()
    qscale = WIN_HD ** -0.5 * LOG2E
    for g in range(CTX_SEQS_PER_STEP):
        sl = slice(g * SEQ, (g + 1) * SEQ)
        q = q_ref[sl, :].astype(F32) * qscale
        k = kv_ref[sl, 0:LANE].astype(F32)
        v = kv_ref[sl, LANE:2 * LANE].astype(F32)
        cols = []
        for col in range(2):
            acc = None
            for h in (2 * col, 2 * col + 1):
                q128, kk, vv, lo = _win_head_operands(q, k, v, h)
                s = _dot_nt(q128, kk)
                sink = sink_ref[layer, h] * LOG2E
                m = jnp.maximum(jnp.max(s, -1, keepdims=True), sink)
                p = jnp.exp2(s - m)
                den = jnp.sum(p, -1, keepdims=True) + jnp.exp2(sink - m)
                o = _dot(p, vv) / den
                o = jnp.where(lo_mask if lo else hi_mask, o, 0.0)
                acc = o if acc is None else acc + o
            cols.append(acc)
        o_ref[sl, :] = jnp.concatenate(cols, 1).astype(o_ref.dtype)


def _win_lat_body(sink_ref, q_ref, kv_ref, ck_ref, cv_ref, c_ref, su_ref, sd_ref, o_ref, layer):
    L = DEC_SEQ
    half = WIN_HD // 2
    c, su, sd = c_ref[...], su_ref[...], sd_ref[...]
    qscale = WIN_HD ** -0.5 * LOG2E
    q = jnp.concatenate(
        [_rope128(q_ref[:, i * LANE:(i + 1) * LANE].astype(F32), c, su, sd, half) for i in range(2)],
        1) * qscale
    k = _rope128(kv_ref[:, 0:LANE].astype(F32), c, su, sd, half)
    v = kv_ref[:, LANE:2 * LANE].astype(F32)
    ck = ck_ref[...]
    cv = cv_ref[...]
    lo_mask, hi_mask = _win_masks()
    nb = L // CHUNK
    assert WINDOW == CHUNK
    rr = lax.broadcasted_iota(jnp.int32, (CHUNK, CHUNK), 0)
    cc = lax.broadcasted_iota(jnp.int32, (CHUNK, CHUNK), 1)
    band = {-1: jnp.where(cc >= rr, 0.0, NEG), 0: jnp.zeros((CHUNK, CHUNK), F32),
            1: jnp.where(cc <= rr, 0.0, NEG)}
    cols = []
    for col in range(2):
        acc_blocks = [None] * nb
        for h in (2 * col, 2 * col + 1):
            q128, kk, vv, lo = _win_head_operands(q, k, v, h)
            _, ckk, cvv, _ = _win_head_operands(q, ck, cv, h)
            sink = sink_ref[layer, h] * LOG2E
            for n in range(nb):
                blocks = [d for d in (-1, 0, 1) if 0 <= n + d < nb]
                k0 = (n + blocks[0]) * CHUNK
                k1 = (n + blocks[-1] + 1) * CHUNK
                qn = q128[n * CHUNK:(n + 1) * CHUNK]
                s_loc = _dot_nt(qn, kk[k0:k1]) + jnp.concatenate([band[d] for d in blocks], 1)
                s_ctx = _dot_nt(qn, ckk)
                m = jnp.maximum(jnp.maximum(jnp.max(s_loc, -1, keepdims=True),
                                            jnp.max(s_ctx, -1, keepdims=True)), sink)
                p_loc = jnp.exp2(s_loc - m)
                p_ctx = jnp.exp2(s_ctx - m)
                den = (jnp.sum(p_loc, -1, keepdims=True) + jnp.sum(p_ctx, -1, keepdims=True)
                       + jnp.exp2(sink - m))
                o = (_dot(p_loc, vv[k0:k1]) + _dot(p_ctx, cvv)) / den
                o = jnp.where(lo_mask if lo else hi_mask, o, 0.0)
                acc_blocks[n] = o if acc_blocks[n] is None else acc_blocks[n] + o
        cols.append(jnp.concatenate(acc_blocks, 0))
    o_ref[...] = jnp.concatenate(cols, 1).astype(o_ref.dtype)


def _win(z_main, sink, cache_k, cache_v, layer):
    c, su, sd, _ = _rope_tables(DEC_SEQ, WIN_HD, LANE)
    tab = _const_spec((DEC_SEQ, LANE))
    cache = pl.BlockSpec((None, None, PAST_LEN, LANE), lambda i: (_lat_index(i), layer, 0, 0))
    return pl.pallas_call(
        functools.partial(_win_kernel, layer=layer),
        out_shape=jax.ShapeDtypeStruct((T_ALL, WIN_HEADS * WIN_HD), BF16),
        grid=(MIXER_STEPS,),
        in_specs=[
            pl.BlockSpec(memory_space=pltpu.SMEM),
            pl.BlockSpec((STEP_ROWS, 256), lambda i: (i, COL_WQ // 256)),
            pl.BlockSpec((STEP_ROWS, 256), lambda i: (i, COL_WK // 256)),
            cache, cache, tab, tab, tab,
        ],
        out_specs=pl.BlockSpec((STEP_ROWS, 256), lambda i: (i, 0)),
        compiler_params=_params("parallel"),
        name="win",
    )(sink, z_main, z_main, cache_k, cache_v, jnp.asarray(c), jnp.asarray(su), jnp.asarray(sd))


def _ret_kernel(*refs, layer):
    _group_step(lambda: _ret_body(*refs, L=SEQ, layer=layer, ctx=True),
                lambda: _ret_body(*refs, L=DEC_SEQ, layer=layer, ctx=False))


def _ret_body(df_ref, db_ref, q_ref, k_ref, v0_ref, v1_ref, g0_ref, g1_ref, s0f_ref, s0b_ref,
              o_ref, sf_out, sb_out, s_ref, cross_ref, *, L, layer, ctx):
    seqs = STEP_ROWS // L
    if not ctx:
        sf_out[...] = jnp.zeros_like(sf_out)
        sb_out[...] = jnp.zeros_like(sb_out)
    C = CHUNK
    nc = L // C
    H = RET_HEADS
    qw = H * RET_DK
    vw = H * RET_DV

    def lane_table(width, per_head, fn):
        pos = lax.broadcasted_iota(jnp.int32, (C, per_head), 0).astype(F32)
        return jnp.concatenate([fn(h, pos) for h in range(H)], 1)

    def log_gamma(ref, h):
        d = jnp.full((1, 1), ref[layer, h], F32)
        return jnp.log(jax.nn.sigmoid(d))

    lgf = [log_gamma(df_ref, h) for h in range(H)]
    lgb = [log_gamma(db_ref, h) for h in range(H)]

    def tables(lg, reverse):
        if reverse:
            dq = lane_table(vw, RET_DV, lambda h, pos: jnp.exp((C - pos) * lg[h]))
            dk = lane_table(qw, RET_DK, lambda h, pos: jnp.exp(pos * lg[h]))
        else:
            dq = lane_table(vw, RET_DV, lambda h, pos: jnp.exp((pos + 1.0) * lg[h]))
            dk = lane_table(qw, RET_DK, lambda h, pos: jnp.exp((C - 1.0 - pos) * lg[h]))
        dc = jnp.concatenate([jnp.broadcast_to(jnp.exp(C * lg[h]), (1, RET_DV)) for h in range(H)], 1)
        return dq, dk, dc

    tab_f = tables(lgf, False)
    tab_b = tables(lgb, True)
    ii = lax.broadcasted_iota(jnp.int32, (C, C), 0)
    jj = lax.broadcasted_iota(jnp.int32, (C, C), 1)
    diff = (ii - jj).astype(F32)
    dmats = [jnp.where(diff >= 0, jnp.exp(jnp.maximum(diff, 0.0) * lgf[h]), 0.0)
             + jnp.where(diff <= 0, jnp.exp(jnp.maximum(-diff, 0.0) * lgb[h]), 0.0) for h in range(H)]
    dmat_stack = jnp.concatenate(dmats, 0)
    lane_q = lax.broadcasted_iota(jnp.int32, (1, qw), 1) // RET_DK

    srow = lax.broadcasted_iota(jnp.int32, (qw, vw), 0) // RET_DK
    scol = lax.broadcasted_iota(jnp.int32, (qw, vw), 1) // RET_DV
    diag = srow == scol

    for g in range(seqs):
        base = g * L
        rows_all = slice(base, base + L)
        q_all = q_ref[rows_all, :].astype(F32)
        k_all = k_ref[rows_all, :].astype(F32) * (RET_DK ** -0.5)
        v_all = jnp.concatenate([v0_ref[rows_all, :], v1_ref[rows_all, :]], 1).astype(F32)
        g_all = jnp.concatenate([g0_ref[rows_all, :], g1_ref[rows_all, :]], 1).astype(F32)

        def scan(tabs, reverse, s0_ref, s_out):
            dq, dk, dc = tabs
            if s0_ref is not None:
                s_ref[g] = jnp.zeros((qw, vw), F32)
                for h in range(H):
                    s_ref[g, h * RET_DK:(h + 1) * RET_DK, h * RET_DV:(h + 1) * RET_DV] = s0_ref[h]
            order = range(nc - 1, -1, -1) if reverse else range(nc)
            for step, ci in enumerate(order):
                sl = slice(ci * C, (ci + 1) * C)
                rs = slice(base + ci * C, base + (ci + 1) * C)
                qc, kc, vc = q_all[sl], k_all[sl], v_all[sl]
                upd = jnp.where(diag, _dot_tn(kc * dk, vc), 0.0)
                if s0_ref is None and step == 0:
                    if not reverse:
                        cross_ref[rs, :] = jnp.zeros((C, vw), F32)
                    s_ref[g] = upd
                    continue
                st = s_ref[g]
                cross = _dot(qc, st) * dq
                if reverse:
                    cross_ref[rs, :] = cross_ref[rs, :] + cross
                else:
                    cross_ref[rs, :] = cross
                s_ref[g] = st * dc + upd
            if s_out is not None:
                for h in range(H):
                    s_out[g, h] = s_ref[g, h * RET_DK:(h + 1) * RET_DK, h * RET_DV:(h + 1) * RET_DV]

        scan(tab_f, False, None if ctx else s0f_ref, sf_out if ctx else None)
        scan(tab_b, True, None if ctx else s0b_ref, sb_out if ctx else None)

        for ci in range(nc):
            sl = slice(ci * C, (ci + 1) * C)
            rs = slice(base + ci * C, base + (ci + 1) * C)
            q_stack = jnp.concatenate([jnp.where(lane_q == h, q_all[sl], 0.0) for h in range(H)], 0)
            att = _dot_nt(q_stack, k_all[sl]) * dmat_stack
            ov = _dot(att, v_all[sl])
            for h in range(H):
                hv = slice(h * RET_DV, (h + 1) * RET_DV)
                o = ov[h * C:(h + 1) * C, hv] + cross_ref[rs, hv]
                gt = g_all[sl, hv]
                o_ref[rs, hv] = ((gt * jax.nn.sigmoid(gt)) * _layer_norm(o)).astype(o_ref.dtype)


def _retention(z_main, dec_f, dec_b, s0f, s0b, layer):
    def zcol(col):
        return pl.BlockSpec((STEP_ROWS, 256), lambda i: (i, col // 256))

    smem = pl.BlockSpec(memory_space=pltpu.SMEM)
    z_specs = [zcol(COL_RQ), zcol(COL_RK), zcol(COL_RV), zcol(COL_RV + 256),
               zcol(COL_RG), zcol(COL_RG + 256)]
    s0_spec = pl.BlockSpec((None, None, RET_HEADS, RET_DK, RET_DV),
                           lambda i: (_lat_index(i), layer, 0, 0, 0))
    st_shape = jax.ShapeDtypeStruct((MIXER_STEPS * CTX_SEQS_PER_STEP, RET_HEADS, RET_DK, RET_DV), F32)
    st_spec = pl.BlockSpec((CTX_SEQS_PER_STEP, RET_HEADS, RET_DK, RET_DV), lambda i: (i, 0, 0, 0))
    return pl.pallas_call(
        functools.partial(_ret_kernel, layer=layer),
        out_shape=(jax.ShapeDtypeStruct((T_ALL, RET_HEADS * RET_DV), BF16), st_shape, st_shape),
        grid=(MIXER_STEPS,),
        in_specs=[smem, smem] + z_specs + [s0_spec, s0_spec],
        out_specs=(pl.BlockSpec((STEP_ROWS, RET_HEADS * RET_DV), lambda i: (i, 0)), st_spec, st_spec),
        scratch_shapes=[pltpu.VMEM((CTX_SEQS_PER_STEP, RET_HEADS * RET_DK, RET_HEADS * RET_DV), F32),
                        pltpu.VMEM((STEP_ROWS, RET_HEADS * RET_DV), F32)],
        compiler_params=_params("parallel"),
        name="retention",
    )(dec_f, dec_b, *([z_main] * 6), s0f, s0b)


def _rms_norm(x, g):
    return x * lax.rsqrt(jnp.mean(x * x, -1, keepdims=True) + RMS_EPS) * g


def _mla_keys(kn, kr):
    lane_r = lax.broadcasted_iota(jnp.int32, (1, LANE), 1)
    return jnp.concatenate([kn, jnp.where(lane_r < MLA_ROPE, kr, 0.0)], 1).astype(BF16)


def _mla_attend(qn, qr, k_cat, vv, o_ref, row0):
    qscale = (MLA_NOPE + MLA_ROPE) ** -0.5 * LOG2E
    lane_n = lax.broadcasted_iota(jnp.int32, (1, MLA_HEADS * MLA_NOPE), 1) // MLA_NOPE
    lane_r = lax.broadcasted_iota(jnp.int32, (1, LANE), 1)
    qn = qn * qscale
    qr = qr * qscale
    lq = qn.shape[0]
    heads = []
    for h in range(MLA_HEADS):
        qnh = jnp.where(lane_n == h, qn, 0.0)
        qrh = qr if h == 0 else pltpu.roll(qr, LANE - h * MLA_ROPE, axis=1)
        qrh = jnp.where(lane_r < MLA_ROPE, qrh, 0.0)
        heads.append(jnp.concatenate([qnh, qrh], 1).astype(BF16))
    s = _dot_nt(jnp.concatenate(heads, 0), k_cat)
    m = jnp.max(s, -1, keepdims=True)
    p = jnp.exp2(s - m)
    den = jnp.sum(p, -1, keepdims=True)
    o = _dot(p, vv) / den
    acc = None
    for h in range(MLA_HEADS):
        oh = jnp.where(lane_n == h, o[h * lq:(h + 1) * lq], 0.0)
        acc = oh if acc is None else acc + oh
    o_ref[row0:row0 + lq, :] = acc.astype(o_ref.dtype)


MLA_QN = MLA_HEADS * MLA_NOPE


def _mla_kernel(cq_ref, ckv_ref, kr_ref, cckv_ref, ckr_ref, c_ref, su_ref, sd_ref,
                qg_ref, kg_ref, wq_ref, wkv_ref, o_ref, ckvn_ref):
    weights = (qg_ref, kg_ref, wq_ref, wkv_ref)
    _group_step(
        lambda: _mla_ctx_body(cq_ref, ckv_ref, kr_ref, *weights, o_ref, ckvn_ref),
        lambda: _mla_lat_body(cq_ref, ckv_ref, kr_ref, cckv_ref, ckr_ref, c_ref, su_ref, sd_ref,
                              *weights, o_ref, ckvn_ref))


def _mla_ctx_body(cq_ref, ckv_ref, kr_ref, qg_ref, kg_ref, wq_ref, wkv_ref, o_ref, ckvn_ref):
    q = _dot(_rms_norm(cq_ref[...].astype(F32), qg_ref[...]), wq_ref[...])
    qn, qr = q[:, 0:MLA_QN], q[:, MLA_QN:]
    ckvn = _rms_norm(ckv_ref[...].astype(F32), kg_ref[...])
    ckvn_ref[...] = ckvn
    kv = _dot(ckvn, wkv_ref[...])
    k_cat = _mla_keys(kv[:, 0:MLA_QN], kr_ref[...].astype(F32))
    vv = kv[:, MLA_QN:].astype(BF16)
    for g in range(CTX_SEQS_PER_STEP):
        sl = slice(g * SEQ, (g + 1) * SEQ)
        _mla_attend(qn[sl], qr[sl], k_cat[sl], vv[sl], o_ref, g * SEQ)


def _mla_lat_body(cq_ref, ckv_ref, kr_ref, cckv_ref, ckr_ref, c_ref, su_ref, sd_ref,
                  qg_ref, kg_ref, wq_ref, wkv_ref, o_ref, ckvn_ref):
    half = MLA_ROPE // 2
    c, su, sd = c_ref[...], su_ref[...], sd_ref[...]
    q = _dot(_rms_norm(cq_ref[...].astype(F32), qg_ref[...]), wq_ref[...])
    qn = q[:, 0:MLA_QN]
    qr = _rope128(q[:, MLA_QN:], c, su, sd, half)
    ckvn = _rms_norm(ckv_ref[...].astype(F32), kg_ref[...])
    ckvn_ref[...] = ckvn
    ckv_all = jnp.concatenate([ckvn, cckv_ref[...]], 0)
    kv = _dot(ckv_all, wkv_ref[...])
    vv = kv[:, MLA_QN:].astype(BF16)
    kr = jnp.concatenate([_rope128(kr_ref[...].astype(F32), c, su, sd, half), ckr_ref[...]], 0)
    k_cat = _mla_keys(kv[:, 0:MLA_QN], kr)
    rows_per_call = 256
    for n in range(DEC_SEQ // rows_per_call):
        rows = slice(n * rows_per_call, (n + 1) * rows_per_call)
        _mla_attend(qn[rows], qr[rows], k_cat, vv, o_ref, n * rows_per_call)


def _mla(z_main, cache_ckv, cache_kr_pad, weights, layer):
    c, su, sd, _ = _rope_tables(DEC_SEQ, MLA_ROPE, LANE)
    tab = _const_spec((DEC_SEQ, LANE))
    cache = pl.BlockSpec((None, None, PAST_LEN, LANE), lambda i: (_lat_index(i), layer, 0, 0))

    def weight(*shape):
        return pl.BlockSpec((None,) + shape, lambda i: (layer, 0, 0))

    return pl.pallas_call(
        _mla_kernel,
        out_shape=(jax.ShapeDtypeStruct((T_ALL, MLA_HEADS * MLA_V), BF16),
                   jax.ShapeDtypeStruct((T_ALL, MLA_KV_LORA), F32)),
        grid=(MIXER_STEPS,),
        in_specs=[
            pl.BlockSpec((STEP_ROWS, 256), lambda i: (i, COL_CQ // 256)),
            pl.BlockSpec((STEP_ROWS, LANE), lambda i: (i, COL_CKV // LANE)),
            pl.BlockSpec((STEP_ROWS, LANE), lambda i: (i, COL_KROPE // LANE)),
            cache, cache, tab, tab, tab,
            weight(1, MLA_Q_LORA), weight(1, MLA_KV_LORA),
            weight(MLA_Q_LORA, MLA_QN + MLA_HEADS * MLA_ROPE),
            weight(MLA_KV_LORA, MLA_QN + MLA_HEADS * MLA_V),
        ],
        out_specs=(pl.BlockSpec((STEP_ROWS, 256), lambda i: (i, 0)),
                   pl.BlockSpec((STEP_ROWS, MLA_KV_LORA), lambda i: (i, 0))),
        compiler_params=_params("parallel"),
        name="mla",
    )(z_main, z_main, z_main, cache_ckv, cache_kr_pad,
      jnp.asarray(c), jnp.asarray(su), jnp.asarray(sd), *weights)


def _route(logits_t, rb):
    scores = jax.nn.sigmoid(logits_t)
    biased = scores + rb
    sc = [scores[e:e + 1, :] for e in range(N_EXPERTS)]
    bi = [biased[e:e + 1, :] for e in range(N_EXPERTS)]
    epg = EXPERTS_PER_GROUP
    gsum = []
    for g in range(N_GROUPS):
        v = bi[g * epg:(g + 1) * epg]
        best = None
        for a in range(epg):
            for b in range(a + 1, epg):
                pair = v[a] + v[b]
                best = pair if best is None else jnp.maximum(best, pair)
        gsum.append(best)
    combine = []
    sel = []
    for g in range(N_GROUPS):
        is_best = None
        for g2 in range(N_GROUPS):
            if g2 == g:
                continue
            c = gsum[g] > gsum[g2] if g2 < g else gsum[g] >= gsum[g2]
            is_best = c if is_best is None else jnp.logical_and(is_best, c)
        for a in range(epg):
            e = g * epg + a
            rank = jnp.zeros_like(bi[e])
            for b in range(epg):
                if b == a:
                    continue
                e2 = g * epg + b
                ahead = bi[e2] >= bi[e] if b < a else bi[e2] > bi[e]
                rank = rank + jnp.where(ahead, 1.0, 0.0)
            sel.append(jnp.logical_and(is_best, rank < 2.0))
    wsum = None
    for e in range(N_EXPERTS):
        w = jnp.where(sel[e], sc[e], 0.0)
        wsum = w if wsum is None else wsum + w
    for e in range(N_EXPERTS):
        combine.append(jnp.where(sel[e], ROUTE_SCALE * sc[e] / wsum, 0.0))
    return jnp.concatenate(combine, 0)


MERGE_TILE = 512

MERGE_BRANCH_ROWS = (HY_W, WIN_HEADS * WIN_HD, RET_HEADS * RET_DV, MLA_HEADS * MLA_V)
MERGE_ROWS = sum(MERGE_BRANCH_ROWS) + D_MODEL


def _merge_kernel(ya_ref, yb_ref, yc_ref, yd_ref, gt_ref, xc_ref, xl_ref, m_ref,
                  w_ref, g_ref, b_ref, rw_ref, rb_ref,
                  x1_ref, h2_ref, cmb_ref, *, ctx_tiles, sub_rows):
    D = D_MODEL
    rw = rw_ref[...]
    rw_hi = rw.astype(BF16)
    rw_lo = (rw - rw_hi.astype(F32)).astype(BF16)
    g1 = m_ref[:, 2 * D:3 * D]
    s2 = m_ref[:, 3 * D:4 * D]
    sc2 = m_ref[:, 4 * D:5 * D]
    is_ctx = pl.program_id(0) < ctx_tiles
    offs = np.cumsum((0,) + MERGE_BRANCH_ROWS)
    branches = tuple((y_ref, slice(int(offs[i]), int(offs[i + 1])))
                     for i, y_ref in enumerate((ya_ref, yb_ref, yc_ref, yd_ref)))
    w_out_rows = slice(int(offs[-1]), MERGE_ROWS)
    for r0 in range(0, x1_ref.shape[0], sub_rows):
        rows = slice(r0, r0 + sub_rows)
        merged = None
        for i, (y_ref, w_rows) in enumerate(branches):
            t = gt_ref[rows, i * D:(i + 1) * D] * jnp.dot(
                y_ref[rows, :], w_ref[w_rows, :], preferred_element_type=F32).astype(BF16)
            merged = t if merged is None else merged + t
        out1 = jnp.dot(merged, w_ref[w_out_rows, :], preferred_element_type=F32)
        x = jnp.where(is_ctx, xc_ref[rows, :], xl_ref[rows, :])
        x1 = _layer_norm(ALPHA * x + g1 * out1) * g_ref[...] + b_ref[...]
        x1_ref[rows, :] = x1
        h2 = _layer_norm(x1) * (1.0 + sc2) + s2
        h2_hi = h2.astype(BF16)
        h2_ref[rows, :] = h2_hi
        h2_lo = (h2 - h2_hi.astype(F32)).astype(BF16)
        logits = (jnp.dot(h2_hi, rw_hi, preferred_element_type=F32)
                  + (jnp.dot(h2_lo, rw_hi, preferred_element_type=F32)
                     + jnp.dot(h2_hi, rw_lo, preferred_element_type=F32)))
        combine_t = _route(logits.T[0:N_EXPERTS], rb_ref[...])
        cmb_ref[rows, :] = jnp.concatenate(
            [combine_t, jnp.zeros((LANE - N_EXPERTS, sub_rows), F32)], 0).T


def _merge(ya, yb, yc, yd, gates, x_ctx, x_lat, lat_block0, mods, w_merge, ln1_g, ln1_b,
           router_w, router_b, layer):
    tm = MERGE_TILE
    row = _mod_row(tm)
    D = D_MODEL
    ctx_tiles = T_CTX // tm

    def tile(w):
        return pl.BlockSpec((tm, w), lambda i: (i, 0))

    def weight(k, n):
        return pl.BlockSpec((None, k, n), lambda i: (layer, 0, 0))

    return pl.pallas_call(
        functools.partial(_merge_kernel, ctx_tiles=ctx_tiles, sub_rows=256),
        out_shape=(jax.ShapeDtypeStruct((T_ALL, D), F32),
                   jax.ShapeDtypeStruct((T_ALL, D), BF16),
                   jax.ShapeDtypeStruct((T_ALL, LANE), F32)),
        grid=(T_ALL // tm,),
        in_specs=[
            tile(256), tile(256), tile(512), tile(256), tile(4 * D),
            *_x_specs(tm, ctx_tiles, lat_block0),
            _mod_spec(layer, row),
            weight(MERGE_ROWS, D), weight(1, D), weight(1, D),
            pl.BlockSpec((D, LANE), lambda i: (0, 0)),
            pl.BlockSpec((N_EXPERTS, 1), lambda i: (0, 0)),
        ],
        out_specs=(tile(D), tile(D), tile(LANE)),
        compiler_params=_params("parallel"),
        name="merge",
    )(ya, yb, yc, yd, gates, x_ctx, x_lat, mods, w_merge,
      ln1_g.reshape(DEPTH, 1, D), ln1_b.reshape(DEPTH, 1, D), router_w,
      router_b.reshape(N_EXPERTS, 1))


MOE_EXPERTS_PER_STEP = 2


def _moe_kernel(*refs, next_h):
    if next_h:
        (h_ref, c_ref, x1_ref, m_ref, wg_ref, wu_ref, wd_ref, g_ref, b_ref, mn_ref,
         o_ref, hn_ref, acc_ref) = refs
    else:
        h_ref, c_ref, x1_ref, m_ref, wg_ref, wu_ref, wd_ref, g_ref, b_ref, o_ref, acc_ref = refs
    eg = pl.program_id(1)

    @pl.when(eg == 0)
    def _():
        acc_ref[...] = jnp.zeros_like(acc_ref)

    h = h_ref[...]
    cmb = c_ref[...]
    lane = lax.broadcasted_iota(jnp.int32, cmb.shape, 1)
    hid = []
    for k in range(MOE_EXPERTS_PER_STEP):
        gate = jnp.dot(h, wg_ref[k].astype(BF16), preferred_element_type=F32)
        up = jnp.dot(h, wu_ref[k].astype(BF16), preferred_element_type=F32)
        e = eg * MOE_EXPERTS_PER_STEP + k
        ce = jnp.sum(jnp.where(lane == e, cmb, 0.0), -1, keepdims=True)
        sig = 0.5 * jnp.tanh(0.5 * gate) + 0.5
        hid.append((gate * sig * (up * ce)).astype(BF16))
    wd = wd_ref[...].reshape(MOE_EXPERTS_PER_STEP * D_EXPERT, D_MODEL).astype(BF16)
    acc_ref[...] += jnp.dot(jnp.concatenate(hid, 1), wd, preferred_element_type=F32)

    @pl.when(eg == N_EXPERTS // MOE_EXPERTS_PER_STEP - 1)
    def _():
        g2 = m_ref[:, 5 * D_MODEL:6 * D_MODEL]
        y = _layer_norm(ALPHA * x1_ref[...] + g2 * acc_ref[...])
        y = y * g_ref[...] + b_ref[...]
        o_ref[...] = y
        if next_h:
            s1 = mn_ref[:, 0:D_MODEL]
            sc1 = mn_ref[:, D_MODEL:2 * D_MODEL]
            hn_ref[...] = (_layer_norm(y) * (1.0 + sc1) + s1).astype(hn_ref.dtype)


def _moe(h2, combine, x1, mods, w_gate, w_up, w_down, ln2_g, ln2_b, layer, row0, n_rows,
         next_h=False):
    tm = 1024
    row = _mod_row(tm)
    D = D_MODEL
    t0 = row0 // tm
    eps = MOE_EXPERTS_PER_STEP
    mod_spec = _mod_spec(layer, lambda i, e: row(t0 + i))
    in_specs = [
        pl.BlockSpec((tm, D), lambda i, e: (t0 + i, 0)),
        pl.BlockSpec((tm, LANE), lambda i, e: (t0 + i, 0)),
        pl.BlockSpec((tm, D), lambda i, e: (t0 + i, 0)),
        mod_spec,
        pl.BlockSpec((None, eps, D, D_EXPERT), lambda i, e: (layer, e, 0, 0)),
        pl.BlockSpec((None, eps, D, D_EXPERT), lambda i, e: (layer, e, 0, 0)),
        pl.BlockSpec((None, eps, D_EXPERT, D), lambda i, e: (layer, e, 0, 0)),
        pl.BlockSpec((None, 1, D), lambda i, e: (layer, 0, 0)),
        pl.BlockSpec((None, 1, D), lambda i, e: (layer, 0, 0)),
    ]
    args = [h2, combine, x1, mods, w_gate, w_up, w_down,
            ln2_g.reshape(DEPTH, 1, D), ln2_b.reshape(DEPTH, 1, D)]
    out_shape = jax.ShapeDtypeStruct((n_rows, D), F32)
    out_specs = pl.BlockSpec((tm, D), lambda i, e: (i, 0))
    if next_h:
        in_specs.append(_mod_spec(layer + 1, lambda i, e: row(t0 + i)))
        args.append(mods)
        out_shape = (out_shape, jax.ShapeDtypeStruct((T_ALL, D), BF16))
        assert n_rows == T_ALL
        out_specs = (out_specs, pl.BlockSpec((tm, D), lambda i, e: (i, 0)))
    return pl.pallas_call(
        functools.partial(_moe_kernel, next_h=next_h),
        out_shape=out_shape,
        grid=(n_rows // tm, N_EXPERTS // eps),
        in_specs=in_specs,
        out_specs=out_specs,
        scratch_shapes=[pltpu.VMEM((tm, D), F32)],
        compiler_params=_params("parallel", "arbitrary"),
        name="moe",
    )(*args)


def kernel(x_prompt, x_sample, cache_win_k, cache_win_v, cache_mla_ckv, cache_mla_krope,
           state_ret_fwd, state_ret_bwd, c, c_ctx, w_ada, b_ada, w_in,
           hy_conv_w, hy_conv_b, hy_w1, hy_b1, hy_w2, hy_b2, hy_w3, hy_bias,
           win_sink, ret_decay_fwd, ret_decay_bwd, mla_q_norm, mla_kv_norm, mla_w_uq, mla_w_ukv,
           w_br_a, w_br_b, w_br_c, w_br_d, w_out, ln1_g, ln1_b, ln2_g, ln2_b,
           router_w, router_b, moe_w_gate, moe_w_up, moe_w_down):
    D = D_MODEL
    x_ctx = x_prompt.reshape(T_CTX, D)
    x_lat = x_sample.reshape(T_LAT, D)
    lat_block0 = 0

    mods = _ada_mods(c_ctx, c, w_ada, b_ada)

    w_in_t = jnp.swapaxes(w_in, 1, 2).reshape(DEPTH * IN_COLS, D)
    cache_k = cache_win_k.reshape(DEC_BATCH, DEPTH, PAST_LEN, WIN_KV_HEADS * WIN_HD)
    cache_v = cache_win_v.reshape(DEC_BATCH, DEPTH, PAST_LEN, WIN_KV_HEADS * WIN_HD)
    cache_kr = jnp.pad(cache_mla_krope, ((0, 0), (0, 0), (0, 0), (0, LANE - MLA_ROPE)))

    uq = mla_w_uq.reshape(DEPTH, MLA_Q_LORA, MLA_HEADS, MLA_NOPE + MLA_ROPE)
    ukv = mla_w_ukv.reshape(DEPTH, MLA_KV_LORA, MLA_HEADS, MLA_NOPE + MLA_V)
    mla_weights = (
        mla_q_norm.reshape(DEPTH, 1, MLA_Q_LORA),
        mla_kv_norm.reshape(DEPTH, 1, MLA_KV_LORA),
        jnp.concatenate([uq[..., :MLA_NOPE].reshape(DEPTH, MLA_Q_LORA, MLA_HEADS * MLA_NOPE),
                         uq[..., MLA_NOPE:].reshape(DEPTH, MLA_Q_LORA, MLA_HEADS * MLA_ROPE)], -1),
        jnp.concatenate([ukv[..., :MLA_NOPE].reshape(DEPTH, MLA_KV_LORA, MLA_HEADS * MLA_NOPE),
                         ukv[..., MLA_NOPE:].reshape(DEPTH, MLA_KV_LORA, MLA_HEADS * MLA_V)], -1),
    )

    hy_w1p = jnp.pad(hy_w1, ((0, 0), (0, LANE - HY_EMB), (0, 0)))
    dft = {}
    for L in (SEQ, DEC_SEQ):
        fwd, inv = _dft_tables(L)
        dft[L] = (jnp.asarray(fwd).astype(BF16), jnp.asarray(inv).astype(BF16))
    router_w_pad = jnp.pad(router_w, ((0, 0), (0, LANE - N_EXPERTS)))
    w_merge = jnp.concatenate([w_br_a, w_br_b, w_br_c, w_br_d, w_out], 1).astype(BF16)

    new_k, new_v, new_ckv, new_kr, new_sf, new_sb = [], [], [], [], [], []
    for l in range(DEPTH):
        if l == 0:
            h = _ln_mod(x_ctx, x_lat, mods, l)
        z = _in_proj(h, w_in_t, l, 0, Z_MAIN, Z_MAIN // 2, BF16, gate=False)
        gates = _in_proj(h, w_in_t, l, COL_GATE, 4 * D, D, BF16, gate=True)

        filters = {L: _hy_filters(L, hy_w1p[l], hy_b1[l][None], hy_w2[l], hy_b2[l][None], hy_w3[l],
                                  dft[L][0]) for L in (SEQ, DEC_SEQ)}
        ya = _hyena(z, l, hy_conv_w, hy_conv_b, hy_bias, filters, dft)
        yb = _win(z, win_sink, cache_k, cache_v, l)
        yc, sf, sb = _retention(z, ret_decay_fwd, ret_decay_bwd, state_ret_fwd, state_ret_bwd, l)
        yd, ckvn = _mla(z, cache_mla_ckv, cache_kr, mla_weights, l)

        x1, h2, combine = _merge(ya, yb, yc, yd, gates, x_ctx, x_lat, lat_block0, mods, w_merge,
                                   ln1_g, ln1_b, router_w_pad, router_b, l)
        moe_args = (h2, combine, x1, mods, moe_w_gate, moe_w_up, moe_w_down, ln2_g, ln2_b, l)
        if l + 1 < DEPTH:
            x_ctx, h = _moe(*moe_args, 0, T_ALL, next_h=True)
            x_lat, lat_block0 = x_ctx, T_CTX // MERGE_TILE
        else:
            x_ctx = _moe(*moe_args, 0, T_CTX)
            x_lat = _moe(*moe_args, T_CTX, T_LAT)

        def ctx_cols(col, width):
            return z[:T_CTX, col:col + width].astype(F32)

        new_k.append(ctx_cols(COL_WK, 128).reshape(BATCH, SEQ, WIN_KV_HEADS, WIN_HD))
        new_v.append(ctx_cols(COL_WV, 128).reshape(BATCH, SEQ, WIN_KV_HEADS, WIN_HD))
        new_ckv.append(ckvn[:T_CTX].reshape(BATCH, SEQ, MLA_KV_LORA))
        new_kr.append(ctx_cols(COL_KROPE, MLA_ROPE).reshape(BATCH, SEQ, MLA_ROPE))
        new_sf.append(sf[:BATCH])
        new_sb.append(sb[:BATCH])

    y_prompt = x_ctx.reshape(BATCH, SEQ, D)
    y_sample = x_lat.reshape(DEC_BATCH, DEC_SEQ, D)
    return (y_prompt, y_sample, jnp.stack(new_k, 1), jnp.stack(new_v, 1), jnp.stack(new_ckv, 1),
            jnp.stack(new_kr, 1), jnp.stack(new_sf, 1), jnp.stack(new_sb, 1))
```

```python
import functools
import math

import numpy as np
import jax
import jax.numpy as jnp
from jax import lax
from jax.experimental import pallas as pl
from jax.experimental.pallas import tpu as pltpu

F32 = jnp.float32
BF16 = jnp.bfloat16

D_MODEL = 1024
BATCH = 16
SEQ = 256
DEPTH = 2
DEC_BATCH = 2
DEC_SEQ = 1024
PAST_LEN = 256
GRID_W = 64
CHUNK = 128
ROPE_BASE = 10000.0
NEG = -1e30
LN_EPS = 1e-5
RMS_EPS = 1e-6
LOG2E = math.log2(math.e)

HY_W = 256
HY_BANDS = 16
HY_EMB = 1 + 2 * HY_BANDS
HY_FFN = 64
HY_FAST_DECAY = 0.3
HY_SLOW_DECAY = 1.5
HY_TARGET = 1e-2

WIN_HEADS = 4
WIN_KV_HEADS = 2
WIN_HD = 64
WINDOW = 128

RET_HEADS = 4
RET_DK = 64
RET_DV = 128

MLA_HEADS = 4
MLA_Q_LORA = 256
MLA_KV_LORA = 128
MLA_NOPE = 64
MLA_ROPE = 32
MLA_V = 64

N_EXPERTS = 16
N_GROUPS = 4
EXPERTS_PER_GROUP = N_EXPERTS // N_GROUPS
D_EXPERT = 256
ROUTE_SCALE = 2.5

ALPHA = (2.0 * DEPTH) ** 0.25

T_CTX = BATCH * SEQ
T_LAT = DEC_BATCH * DEC_SEQ
T_ALL = T_CTX + T_LAT

COL_HY = 0
COL_WQ = 768
COL_WK = 1024
COL_WV = 1152
COL_RQ = 1280
COL_RK = 1536
COL_RV = 1792
COL_RG = 2304
COL_CQ = 2816
COL_CKV = 3072
COL_KROPE = 3200
COL_GATE = 3232
IN_COLS = COL_GATE + 4 * D_MODEL
Z_MAIN = 3328

LANE = 128
STEP_ROWS = 1024
CTX_SEQS_PER_STEP = STEP_ROWS // SEQ
CTX_STEPS = T_CTX // STEP_ROWS
MIXER_STEPS = T_ALL // STEP_ROWS
VMEM_LIMIT = 56 * 1024 * 1024


def _params(*sem):
    return pltpu.CompilerParams(dimension_semantics=sem, vmem_limit_bytes=VMEM_LIMIT)


def _dot(a, b):
    return jnp.dot(a.astype(BF16), b.astype(BF16), preferred_element_type=F32)


def _dot_split(a, b):
    a_hi = a.astype(BF16)
    a_lo = (a - a_hi.astype(F32)).astype(BF16)
    b_hi = b.astype(BF16)
    b_lo = (b - b_hi.astype(F32)).astype(BF16)

    def mm(x, y):
        return jnp.dot(x, y, preferred_element_type=F32)

    return mm(a_hi, b_hi) + (mm(a_lo, b_hi) + mm(a_hi, b_lo))


def _dot_nt(a, b):
    return lax.dot_general(a.astype(BF16), b.astype(BF16), (((1,), (1,)), ((), ())),
                           preferred_element_type=F32)


def _dot_tn(a, b):
    return lax.dot_general(a.astype(BF16), b.astype(BF16), (((0,), (0,)), ((), ())),
                           preferred_element_type=F32)


def _layer_norm(x):
    mu = jnp.mean(x, -1, keepdims=True)
    xc = x - mu
    var = jnp.mean(xc * xc, -1, keepdims=True)
    return xc * lax.rsqrt(var + LN_EPS)


def _mod_row(tile_rows):
    def row(i):
        start = i * tile_rows
        return jnp.where(start < T_CTX, 0, 1 + (start - T_CTX) // DEC_SEQ)
    return row


@functools.lru_cache(maxsize=None)
def _dft_tables(L):
    f = np.arange(L, dtype=np.int64)[:, None]
    s = np.arange(L, dtype=np.int64)[None, :]
    ang = np.pi * ((f * s) % (2 * L)).astype(np.float64) / L
    cos = np.cos(ang)
    sin = np.sin(ang)
    alt = np.where(np.arange(L) % 2 == 0, 1.0, -1.0)
    fwd_im = -sin
    fwd_im[0, :] = alt
    fwd = np.concatenate([cos, fwd_im], 0)
    inv_re = cos.T / L
    inv_re[:, 0] = 1.0 / (2 * L)
    inv_im = -sin.T / L
    inv_im[:, 0] = alt / (2 * L)
    inv = np.concatenate([inv_re, inv_im], 1)
    return fwd.astype(np.float32), inv.astype(np.float32)


@functools.lru_cache(maxsize=None)
def _hyena_embedding(L):
    t01 = np.linspace(0.0, 1.0, L, dtype=np.float64)[:, None]
    bands = np.linspace(1e-4, HY_BANDS - 1, HY_BANDS, dtype=np.float64)
    ang = (2.0 * math.pi / L) * np.arange(L, dtype=np.float64)[:, None] * bands[None, :]
    z = np.concatenate([t01, np.cos(ang), -np.sin(ang)], -1)
    zp = np.zeros((L, LANE), np.float64)
    zp[:, :HY_EMB] = z
    deltas = np.abs(np.linspace(math.log(HY_TARGET) / HY_SLOW_DECAY,
                                math.log(HY_TARGET) / HY_FAST_DECAY, HY_W, dtype=np.float64))
    return zp.astype(np.float32), deltas[None, :].astype(np.float32)


@functools.lru_cache(maxsize=None)
def _rope_tables(L, rot_dim, width):
    rows = L // GRID_W
    n_freq = rot_dim // 4
    half = rot_dim // 2
    inv = ROPE_BASE ** (-np.arange(n_freq, dtype=np.float64) / n_freq)
    pos = np.arange(L)
    row = (pos // GRID_W).astype(np.float64)
    col = (pos % GRID_W).astype(np.float64)
    ang = np.concatenate([row[:, None] * inv, col[:, None] * inv], -1)
    cos, sin = np.cos(ang), np.sin(ang)
    zero = np.zeros_like(sin)
    c = np.tile(np.concatenate([cos, cos], -1), (1, width // rot_dim))
    s_up = np.tile(np.concatenate([-sin, zero], -1), (1, width // rot_dim))
    s_dn = np.tile(np.concatenate([zero, sin], -1), (1, width // rot_dim))
    return c.astype(np.float32), s_up.astype(np.float32), s_dn.astype(np.float32), half


def _rope128(x, c, s_up, s_dn, half):
    up = pltpu.roll(x, LANE - half, axis=1)
    dn = pltpu.roll(x, half, axis=1)
    return x * c + up * s_up + dn * s_dn


MOD_ROWS = 1 + DEC_BATCH


def _ada_kernel(ct_ref, w_ref, b_ref, o_ref):
    @pl.when(pl.program_id(1) == 0)
    def _():
        for r in range(MOD_ROWS):
            o_ref[r] = b_ref[...]

    ct = ct_ref[...]
    s = ct * jax.nn.sigmoid(ct)
    tk, n = w_ref.shape
    rows, cols = 64, 4 * LANE
    s_cols = [[s[k0:k0 + rows, r:r + 1] for k0 in range(0, tk, rows)] for r in range(MOD_ROWS)]
    for c0 in range(0, n, cols):
        acc = [None] * MOD_ROWS
        for ki, k0 in enumerate(range(0, tk, rows)):
            wc = w_ref[k0:k0 + rows, c0:c0 + cols]
            for r in range(MOD_ROWS):
                part = jnp.sum((wc * s_cols[r][ki]).reshape(rows // 8, 8, cols), axis=0)
                acc[r] = part if acc[r] is None else acc[r] + part
        for r in range(MOD_ROWS):
            o_ref[r, :, c0:c0 + cols] += jnp.sum(acc[r], axis=0, keepdims=True)


def _ada_mods(c_ctx, c, w_ada, b_ada):
    tk = 256
    n = 6 * D_MODEL
    c_cols = jnp.concatenate([c_ctx[:, None], c.T], 1)
    return pl.pallas_call(
        _ada_kernel,
        out_shape=jax.ShapeDtypeStruct((DEPTH, MOD_ROWS, 1, n), F32),
        grid=(DEPTH, D_MODEL // tk),
        in_specs=[
            pl.BlockSpec((tk, MOD_ROWS), lambda l, k: (k, 0)),
            pl.BlockSpec((None, tk, n), lambda l, k: (l, k, 0)),
            pl.BlockSpec((None, 1, n), lambda l, k: (l, 0, 0)),
        ],
        out_specs=pl.BlockSpec((None, MOD_ROWS, 1, n), lambda l, k: (l, 0, 0, 0)),
        compiler_params=_params("parallel", "arbitrary"),
        name="ada_mods",
    )(c_cols, w_ada, b_ada.reshape(DEPTH, 1, n))


def _mod_spec(layer, row_of_step):
    return pl.BlockSpec((None, None, 1, 6 * D_MODEL), lambda *g: (layer, row_of_step(*g), 0, 0))


def _lnmod_kernel(xc_ref, xl_ref, m_ref, h_ref, *, ctx_tiles):
    x = jnp.where(pl.program_id(0) < ctx_tiles, xc_ref[...], xl_ref[...])
    y = _layer_norm(x)
    s1 = m_ref[:, 0:D_MODEL]
    sc1 = m_ref[:, D_MODEL:2 * D_MODEL]
    h_ref[...] = (y * (1.0 + sc1) + s1).astype(h_ref.dtype)


def _x_specs(tm, ctx_tiles, lat_block0=0):
    return [pl.BlockSpec((tm, D_MODEL), lambda i: (jnp.minimum(i, ctx_tiles - 1), 0)),
            pl.BlockSpec((tm, D_MODEL), lambda i: (jnp.maximum(i - ctx_tiles, 0) + lat_block0, 0))]


def _ln_mod(x_ctx, x_lat, mods, layer):
    tm = 512
    ctx_tiles = T_CTX // tm
    return pl.pallas_call(
        functools.partial(_lnmod_kernel, ctx_tiles=ctx_tiles),
        out_shape=jax.ShapeDtypeStruct((T_ALL, D_MODEL), BF16),
        grid=(T_ALL // tm,),
        in_specs=_x_specs(tm, ctx_tiles) + [_mod_spec(layer, _mod_row(tm))],
        out_specs=pl.BlockSpec((tm, D_MODEL), lambda i: (i, 0)),
        compiler_params=_params("parallel"),
        name="ln_mod",
    )(x_ctx, x_lat, mods)


def _proj_kernel(h_ref, w_ref, o_ref, wb_ref, *, gate):
    @pl.when(pl.program_id(1) == 0)
    def _():
        wb_ref[...] = w_ref[...].T.astype(BF16)

    if not gate:
        o_ref[...] = jnp.dot(h_ref[...], wb_ref[...], preferred_element_type=F32).astype(o_ref.dtype)
        return
    sub = 2 * LANE
    for c0 in range(0, o_ref.shape[1], sub):
        r = jnp.dot(h_ref[...], wb_ref[:, c0:c0 + sub], preferred_element_type=F32)
        rb = r.astype(o_ref.dtype)
        o_ref[:, c0:c0 + sub] = 0.5 * jnp.tanh(0.5 * rb) + 0.5


def _in_proj(h, w_t, layer, col0, n_cols, tn, out_dtype, gate):
    tm = 2048
    return pl.pallas_call(
        functools.partial(_proj_kernel, gate=gate),
        out_shape=jax.ShapeDtypeStruct((T_ALL, n_cols), out_dtype),
        grid=(n_cols // tn, T_ALL // tm),
        in_specs=[
            pl.BlockSpec((tm, D_MODEL), lambda j, i: (i, 0)),
            pl.BlockSpec((pl.Element(tn), pl.Element(D_MODEL)),
                         lambda j, i: (pl.multiple_of(layer * IN_COLS + col0 + j * tn, 8), 0)),
        ],
        out_specs=pl.BlockSpec((tm, tn), lambda j, i: (i, j)),
        scratch_shapes=[pltpu.VMEM((D_MODEL, tn), BF16)],
        compiler_params=_params("parallel", "arbitrary"),
        name="gate_proj" if gate else "in_proj",
    )(h, w_t)


def _hy_filter_kernel(z_ref, dl_ref, w1_ref, b1_ref, w2_ref, b2_ref, w3_ref, fwd_ref,
                      kre_ref, kim_ref, *, L):
    z = z_ref[...]
    a = jnp.sin(_dot_split(z, w1_ref[...]) + b1_ref[...])
    a = jnp.sin(_dot_split(a, w2_ref[...]) + b2_ref[...])
    h = _dot_split(a, w3_ref[...])
    decay = jnp.exp(-z[:, 0:1] * dl_ref[...])
    not_first = lax.broadcasted_iota(jnp.int32, (L, HY_W), 0) > 0
    sums, diffs = [], []
    for o in range(2):
        fw = h[:, (2 * o) * HY_W:(2 * o + 1) * HY_W] * decay
        bw = jnp.where(not_first, h[:, (2 * o + 1) * HY_W:(2 * o + 2) * HY_W] * decay, 0.0)
        sums.append(fw + bw)
        diffs.append(fw - bw)
    p = _dot(fwd_ref[...], jnp.concatenate(sums, 1))
    q = _dot(fwd_ref[L:2 * L, :], jnp.concatenate(diffs, 1))
    kre_ref[...] = p[0:L]
    first = lax.broadcasted_iota(jnp.int32, (L, 2 * HY_W), 0) == 0
    kim_ref[...] = jnp.where(first, p[L:L + 1], q)


def _hy_filters(L, w1p, b1, w2, b2, w3, fwd):
    zemb, deltas = _hyena_embedding(L)
    out = jax.ShapeDtypeStruct((L, 2 * HY_W), F32)
    return pl.pallas_call(
        functools.partial(_hy_filter_kernel, L=L),
        out_shape=(out, out),
        compiler_params=pltpu.CompilerParams(vmem_limit_bytes=VMEM_LIMIT),
        name=f"hy_filters_{L}",
    )(jnp.asarray(zemb), jnp.asarray(deltas), w1p, b1, w2, b2, w3, fwd)


def _group_step(ctx_body, lat_body):
    i = pl.program_id(0)
    pl.when(i < CTX_STEPS)(ctx_body)
    pl.when(i >= CTX_STEPS)(lat_body)


def _lat_index(i):
    return jnp.maximum(i - CTX_STEPS, 0)


def _hyena_kernel(hy_ref, cw_ref, cb_ref, bias_ref, kre_c, kim_c, fwd_c, inv_c,
                  kre_l, kim_l, fwd_l, inv_l, o_ref):
    _group_step(
        lambda: _hyena_body(hy_ref, cw_ref, cb_ref, bias_ref, kre_c, kim_c, fwd_c, inv_c, o_ref,
                            SEQ, CTX_SEQS_PER_STEP),
        lambda: _hyena_body(hy_ref, cw_ref, cb_ref, bias_ref, kre_l, kim_l, fwd_l, inv_l, o_ref,
                            DEC_SEQ, 1))


def _hyena_body(hy_ref, cw_ref, cb_ref, bias_ref, kre_ref, kim_ref, fwd_ref, inv_ref, o_ref, L, seqs):
    width = seqs * HY_W
    first = lax.broadcasted_iota(jnp.int32, (L, width), 0) == 0

    def tiled(t):
        return jnp.concatenate([t] * seqs, 1)

    def long_conv(u, o):
        uf = _dot(fwd_ref[...], u)
        ure, uim = uf[0:L], uf[L:2 * L]
        kre = tiled(kre_ref[:, o * HY_W:(o + 1) * HY_W])
        kim = tiled(kim_ref[:, o * HY_W:(o + 1) * HY_W])
        yre = jnp.where(first, ure * kre, ure * kre - uim * kim)
        yim = jnp.where(first, uim * kim, ure * kim + uim * kre)
        y = _dot(inv_ref[...], jnp.concatenate([yre, yim], 0))
        return y + u * tiled(bias_ref[o:o + 1, :])

    vs, x1s, x2s = [], [], []
    for g in range(seqs):
        x = hy_ref[g * L:(g + 1) * L, :].astype(F32)
        rows = lax.broadcasted_iota(jnp.int32, x.shape, 0)
        prev = jnp.where(rows == 0, 0.0, pltpu.roll(x, 1, axis=0))
        nxt = jnp.where(rows == L - 1, 0.0, pltpu.roll(x, L - 1, axis=0))
        z = prev * cw_ref[0:1, :] + x * cw_ref[1:2, :] + nxt * cw_ref[2:3, :] + cb_ref[...]
        vs.append(z[:, 0:HY_W])
        x1s.append(z[:, HY_W:2 * HY_W])
        x2s.append(z[:, 2 * HY_W:3 * HY_W])
    u = jnp.concatenate(x1s, 1) * long_conv(jnp.concatenate(vs, 1), 0)
    y = jnp.concatenate(x2s, 1) * long_conv(u, 1)
    for g in range(seqs):
        o_ref[g * L:(g + 1) * L, :] = y[:, g * HY_W:(g + 1) * HY_W].astype(o_ref.dtype)


def _const_spec(shape):
    return pl.BlockSpec(shape, lambda i: (0,) * len(shape))


def _hyena(z_main, layer, conv_w, conv_b, bias, filters, dft):
    tables, table_specs = [], []
    for L in (SEQ, DEC_SEQ):
        tables += [*filters[L], *dft[L]]
        table_specs += [_const_spec((L, 2 * HY_W)), _const_spec((L, 2 * HY_W)),
                        _const_spec((2 * L, L)), _const_spec((L, 2 * L))]
    return pl.pallas_call(
        _hyena_kernel,
        out_shape=jax.ShapeDtypeStruct((T_ALL, HY_W), BF16),
        grid=(MIXER_STEPS,),
        in_specs=[
            pl.BlockSpec((STEP_ROWS, 3 * HY_W), lambda i: (i, 0)),
            pl.BlockSpec((None, 3, 3 * HY_W), lambda i: (layer, 0, 0)),
            pl.BlockSpec((None, 1, 3 * HY_W), lambda i: (layer, 0, 0)),
            pl.BlockSpec((None, 2, HY_W), lambda i: (layer, 0, 0)),
        ] + table_specs,
        out_specs=pl.BlockSpec((STEP_ROWS, HY_W), lambda i: (i, 0)),
        compiler_params=_params("parallel"),
        name="hyena",
    )(z_main, conv_w, conv_b.reshape(DEPTH, 1, 3 * HY_W), bias, *tables)


def _win_masks():
    lane = lax.broadcasted_iota(jnp.int32, (1, LANE), 1)
    return lane < WIN_HD, lane >= WIN_HD


def _win_head_operands(q, k, v, h):
    lo_mask, hi_mask = _win_masks()
    col = h // 2
    lo = h % 2 == 0
    q128 = jnp.where(lo_mask if lo else hi_mask, q[:, col * LANE:(col + 1) * LANE], 0.0)
    swap = h in (1, 2)
    if swap:
        k = pltpu.roll(k, WIN_HD, axis=1)
        v = pltpu.roll(v, WIN_HD, axis=1)
    return q128, k, v, lo


def _win_kernel(sink_ref, q_ref, kv_ref, ck_ref, cv_ref, c_ref, su_ref, sd_ref, o_ref, *, layer):
    _group_step(
        lambda: _win_ctx_body(sink_ref, q_ref, kv_ref, o_ref, layer),
        lambda: _win_lat_body(sink_ref, q_ref, kv_ref, ck_ref, cv_ref, c_ref, su_ref, sd_ref, o_ref, layer))


def _win_ctx_body(sink_ref, q_ref, kv_ref, o_ref, layer):
    lo_mask, hi_mask = _win_masks()
    qscale = WIN_HD ** -0.5 * LOG2E
    for g in range(CTX_SEQS_PER_STEP):
        sl = slice(g * SEQ, (g + 1) * SEQ)
        q = q_ref[sl, :].astype(F32) * qscale
        k = kv_ref[sl, 0:LANE].astype(F32)
        v = kv_ref[sl, LANE:2 * LANE].astype(F32)
        cols = []
        for col in range(2):
            acc = None
            for h in (2 * col, 2 * col + 1):
                q128, kk, vv, lo = _win_head_operands(q, k, v, h)
                s = _dot_nt(q128, kk)
                sink = sink_ref[layer, h] * LOG2E
                m = jnp.maximum(jnp.max(s, -1, keepdims=True), sink)
                p = jnp.exp2(s - m)
                den = jnp.sum(p, -1, keepdims=True) + jnp.exp2(sink - m)
                o = _dot(p, vv) / den
                o = jnp.where(lo_mask if lo else hi_mask, o, 0.0)
                acc = o if acc is None else acc + o
            cols.append(acc)
        o_ref[sl, :] = jnp.concatenate(cols, 1).astype(o_ref.dtype)


def _win_lat_body(sink_ref, q_ref, kv_ref, ck_ref, cv_ref, c_ref, su_ref, sd_ref, o_ref, layer):
    L = DEC_SEQ
    half = WIN_HD // 2
    c, su, sd = c_ref[...], su_ref[...], sd_ref[...]
    qscale = WIN_HD ** -0.5 * LOG2E
    q = jnp.concatenate(
        [_rope128(q_ref[:, i * LANE:(i + 1) * LANE].astype(F32), c, su, sd, half) for i in range(2)],
        1) * qscale
    k = _rope128(kv_ref[:, 0:LANE].astype(F32), c, su, sd, half)
    v = kv_ref[:, LANE:2 * LANE].astype(F32)
    ck = ck_ref[...]
    cv = cv_ref[...]
    lo_mask, hi_mask = _win_masks()
    nb = L // CHUNK
    assert WINDOW == CHUNK
    rr = lax.broadcasted_iota(jnp.int32, (CHUNK, CHUNK), 0)
    cc = lax.broadcasted_iota(jnp.int32, (CHUNK, CHUNK), 1)
    band = {-1: jnp.where(cc >= rr, 0.0, NEG), 0: jnp.zeros((CHUNK, CHUNK), F32),
            1: jnp.where(cc <= rr, 0.0, NEG)}
    cols = []
    for col in range(2):
        acc_blocks = [None] * nb
        for h in (2 * col, 2 * col + 1):
            q128, kk, vv, lo = _win_head_operands(q, k, v, h)
            _, ckk, cvv, _ = _win_head_operands(q, ck, cv, h)
            sink = sink_ref[layer, h] * LOG2E
            for n in range(nb):
                blocks = [d for d in (-1, 0, 1) if 0 <= n + d < nb]
                k0 = (n + blocks[0]) * CHUNK
                k1 = (n + blocks[-1] + 1) * CHUNK
                qn = q128[n * CHUNK:(n + 1) * CHUNK]
                s_loc = _dot_nt(qn, kk[k0:k1]) + jnp.concatenate([band[d] for d in blocks], 1)
                s_ctx = _dot_nt(qn, ckk)
                m = jnp.maximum(jnp.maximum(jnp.max(s_loc, -1, keepdims=True),
                                            jnp.max(s_ctx, -1, keepdims=True)), sink)
                p_loc = jnp.exp2(s_loc - m)
                p_ctx = jnp.exp2(s_ctx - m)
                den = (jnp.sum(p_loc, -1, keepdims=True) + jnp.sum(p_ctx, -1, keepdims=True)
                       + jnp.exp2(sink - m))
                o = (_dot(p_loc, vv[k0:k1]) + _dot(p_ctx, cvv)) / den
                o = jnp.where(lo_mask if lo else hi_mask, o, 0.0)
                acc_blocks[n] = o if acc_blocks[n] is None else acc_blocks[n] + o
        cols.append(jnp.concatenate(acc_blocks, 0))
    o_ref[...] = jnp.concatenate(cols, 1).astype(o_ref.dtype)


def _win(z_main, sink, cache_k, cache_v, layer):
    c, su, sd, _ = _rope_tables(DEC_SEQ, WIN_HD, LANE)
    tab = _const_spec((DEC_SEQ, LANE))
    cache = pl.BlockSpec((None, None, PAST_LEN, LANE), lambda i: (_lat_index(i), layer, 0, 0))
    return pl.pallas_call(
        functools.partial(_win_kernel, layer=layer),
        out_shape=jax.ShapeDtypeStruct((T_ALL, WIN_HEADS * WIN_HD), BF16),
        grid=(MIXER_STEPS,),
        in_specs=[
            pl.BlockSpec(memory_space=pltpu.SMEM),
            pl.BlockSpec((STEP_ROWS, 256), lambda i: (i, COL_WQ // 256)),
            pl.BlockSpec((STEP_ROWS, 256), lambda i: (i, COL_WK // 256)),
            cache, cache, tab, tab, tab,
        ],
        out_specs=pl.BlockSpec((STEP_ROWS, 256), lambda i: (i, 0)),
        compiler_params=_params("parallel"),
        name="win",
    )(sink, z_main, z_main, cache_k, cache_v, jnp.asarray(c), jnp.asarray(su), jnp.asarray(sd))


def _ret_kernel(*refs, layer):
    _group_step(lambda: _ret_body(*refs, L=SEQ, layer=layer, ctx=True),
                lambda: _ret_body(*refs, L=DEC_SEQ, layer=layer, ctx=False))


def _ret_body(df_ref, db_ref, q_ref, k_ref, v0_ref, v1_ref, g0_ref, g1_ref, s0f_ref, s0b_ref,
              o_ref, sf_out, sb_out, s_ref, cross_ref, *, L, layer, ctx):
    seqs = STEP_ROWS // L
    if not ctx:
        sf_out[...] = jnp.zeros_like(sf_out)
        sb_out[...] = jnp.zeros_like(sb_out)
    C = CHUNK
    nc = L // C
    H = RET_HEADS
    qw = H * RET_DK
    vw = H * RET_DV

    def lane_table(width, per_head, fn):
        pos = lax.broadcasted_iota(jnp.int32, (C, per_head), 0).astype(F32)
        return jnp.concatenate([fn(h, pos) for h in range(H)], 1)

    def log_gamma(ref, h):
        d = jnp.full((1, 1), ref[layer, h], F32)
        return jnp.log(jax.nn.sigmoid(d))

    lgf = [log_gamma(df_ref, h) for h in range(H)]
    lgb = [log_gamma(db_ref, h) for h in range(H)]

    def tables(lg, reverse):
        if reverse:
            dq = lane_table(vw, RET_DV, lambda h, pos: jnp.exp((C - pos) * lg[h]))
            dk = lane_table(qw, RET_DK, lambda h, pos: jnp.exp(pos * lg[h]))
        else:
            dq = lane_table(vw, RET_DV, lambda h, pos: jnp.exp((pos + 1.0) * lg[h]))
            dk = lane_table(qw, RET_DK, lambda h, pos: jnp.exp((C - 1.0 - pos) * lg[h]))
        dc = jnp.concatenate([jnp.broadcast_to(jnp.exp(C * lg[h]), (1, RET_DV)) for h in range(H)], 1)
        return dq, dk, dc

    tab_f = tables(lgf, False)
    tab_b = tables(lgb, True)
    ii = lax.broadcasted_iota(jnp.int32, (C, C), 0)
    jj = lax.broadcasted_iota(jnp.int32, (C, C), 1)
    diff = (ii - jj).astype(F32)
    dmats = [jnp.where(diff >= 0, jnp.exp(jnp.maximum(diff, 0.0) * lgf[h]), 0.0)
             + jnp.where(diff <= 0, jnp.exp(jnp.maximum(-diff, 0.0) * lgb[h]), 0.0) for h in range(H)]
    dmat_stack = jnp.concatenate(dmats, 0)
    lane_q = lax.broadcasted_iota(jnp.int32, (1, qw), 1) // RET_DK

    srow = lax.broadcasted_iota(jnp.int32, (qw, vw), 0) // RET_DK
    scol = lax.broadcasted_iota(jnp.int32, (qw, vw), 1) // RET_DV
    diag = srow == scol

    for g in range(seqs):
        base = g * L
        rows_all = slice(base, base + L)
        q_all = q_ref[rows_all, :].astype(F32)
        k_all = k_ref[rows_all, :].astype(F32) * (RET_DK ** -0.5)
        v_all = jnp.concatenate([v0_ref[rows_all, :], v1_ref[rows_all, :]], 1).astype(F32)
        g_all = jnp.concatenate([g0_ref[rows_all, :], g1_ref[rows_all, :]], 1).astype(F32)

        def scan(tabs, reverse, s0_ref, s_out):
            dq, dk, dc = tabs
            if s0_ref is not None:
                s_ref[g] = jnp.zeros((qw, vw), F32)
                for h in range(H):
                    s_ref[g, h * RET_DK:(h + 1) * RET_DK, h * RET_DV:(h + 1) * RET_DV] = s0_ref[h]
            order = range(nc - 1, -1, -1) if reverse else range(nc)
            for step, ci in enumerate(order):
                sl = slice(ci * C, (ci + 1) * C)
                rs = slice(base + ci * C, base + (ci + 1) * C)
                qc, kc, vc = q_all[sl], k_all[sl], v_all[sl]
                upd = jnp.where(diag, _dot_tn(kc * dk, vc), 0.0)
                if s0_ref is None and step == 0:
                    if not reverse:
                        cross_ref[rs, :] = jnp.zeros((C, vw), F32)
                    s_ref[g] = upd
                    continue
                st = s_ref[g]
                cross = _dot(qc, st) * dq
                if reverse:
                    cross_ref[rs, :] = cross_ref[rs, :] + cross
                else:
                    cross_ref[rs, :] = cross
                s_ref[g] = st * dc + upd
            if s_out is not None:
                for h in range(H):
                    s_out[g, h] = s_ref[g, h * RET_DK:(h + 1) * RET_DK, h * RET_DV:(h + 1) * RET_DV]

        scan(tab_f, False, None if ctx else s0f_ref, sf_out if ctx else None)
        scan(tab_b, True, None if ctx else s0b_ref, sb_out if ctx else None)

        for ci in range(nc):
            sl = slice(ci * C, (ci + 1) * C)
            rs = slice(base + ci * C, base + (ci + 1) * C)
            q_stack = jnp.concatenate([jnp.where(lane_q == h, q_all[sl], 0.0) for h in range(H)], 0)
            att = _dot_nt(q_stack, k_all[sl]) * dmat_stack
            ov = _dot(att, v_all[sl])
            for h in range(H):
                hv = slice(h * RET_DV, (h + 1) * RET_DV)
                o = ov[h * C:(h + 1) * C, hv] + cross_ref[rs, hv]
                gt = g_all[sl, hv]
                o_ref[rs, hv] = ((gt * jax.nn.sigmoid(gt)) * _layer_norm(o)).astype(o_ref.dtype)


def _retention(z_main, dec_f, dec_b, s0f, s0b, layer):
    def zcol(col):
        return pl.BlockSpec((STEP_ROWS, 256), lambda i: (i, col // 256))

    smem = pl.BlockSpec(memory_space=pltpu.SMEM)
    z_specs = [zcol(COL_RQ), zcol(COL_RK), zcol(COL_RV), zcol(COL_RV + 256),
               zcol(COL_RG), zcol(COL_RG + 256)]
    s0_spec = pl.BlockSpec((None, None, RET_HEADS, RET_DK, RET_DV),
                           lambda i: (_lat_index(i), layer, 0, 0, 0))
    st_shape = jax.ShapeDtypeStruct((MIXER_STEPS * CTX_SEQS_PER_STEP, RET_HEADS, RET_DK, RET_DV), F32)
    st_spec = pl.BlockSpec((CTX_SEQS_PER_STEP, RET_HEADS, RET_DK, RET_DV), lambda i: (i, 0, 0, 0))
    return pl.pallas_call(
        functools.partial(_ret_kernel, layer=layer),
        out_shape=(jax.ShapeDtypeStruct((T_ALL, RET_HEADS * RET_DV), BF16), st_shape, st_shape),
        grid=(MIXER_STEPS,),
        in_specs=[smem, smem] + z_specs + [s0_spec, s0_spec],
        out_specs=(pl.BlockSpec((STEP_ROWS, RET_HEADS * RET_DV), lambda i: (i, 0)), st_spec, st_spec),
        scratch_shapes=[pltpu.VMEM((CTX_SEQS_PER_STEP, RET_HEADS * RET_DK, RET_HEADS * RET_DV), F32),
                        pltpu.VMEM((STEP_ROWS, RET_HEADS * RET_DV), F32)],
        compiler_params=_params("parallel"),
        name="retention",
    )(dec_f, dec_b, *([z_main] * 6), s0f, s0b)


def _rms_norm(x, g):
    return x * lax.rsqrt(jnp.mean(x * x, -1, keepdims=True) + RMS_EPS) * g


def _mla_keys(kn, kr):
    lane_r = lax.broadcasted_iota(jnp.int32, (1, LANE), 1)
    return jnp.concatenate([kn, jnp.where(lane_r < MLA_ROPE, kr, 0.0)], 1).astype(BF16)


def _mla_attend(qn, qr, k_cat, vv, o_ref, row0):
    qscale = (MLA_NOPE + MLA_ROPE) ** -0.5 * LOG2E
    lane_n = lax.broadcasted_iota(jnp.int32, (1, MLA_HEADS * MLA_NOPE), 1) // MLA_NOPE
    lane_r = lax.broadcasted_iota(jnp.int32, (1, LANE), 1)
    qn = qn * qscale
    qr = qr * qscale
    lq = qn.shape[0]
    heads = []
    for h in range(MLA_HEADS):
        qnh = jnp.where(lane_n == h, qn, 0.0)
        qrh = qr if h == 0 else pltpu.roll(qr, LANE - h * MLA_ROPE, axis=1)
        qrh = jnp.where(lane_r < MLA_ROPE, qrh, 0.0)
        heads.append(jnp.concatenate([qnh, qrh], 1).astype(BF16))
    s = _dot_nt(jnp.concatenate(heads, 0), k_cat)
    m = jnp.max(s, -1, keepdims=True)
    p = jnp.exp2(s - m)
    den = jnp.sum(p, -1, keepdims=True)
    o = _dot(p, vv) / den
    acc = None
    for h in range(MLA_HEADS):
        oh = jnp.where(lane_n == h, o[h * lq:(h + 1) * lq], 0.0)
        acc = oh if acc is None else acc + oh
    o_ref[row0:row0 + lq, :] = acc.astype(o_ref.dtype)


MLA_QN = MLA_HEADS * MLA_NOPE


def _mla_kernel(cq_ref, ckv_ref, kr_ref, cckv_ref, ckr_ref, c_ref, su_ref, sd_ref,
                qg_ref, kg_ref, wq_ref, wkv_ref, o_ref, ckvn_ref):
    weights = (qg_ref, kg_ref, wq_ref, wkv_ref)
    _group_step(
        lambda: _mla_ctx_body(cq_ref, ckv_ref, kr_ref, *weights, o_ref, ckvn_ref),
        lambda: _mla_lat_body(cq_ref, ckv_ref, kr_ref, cckv_ref, ckr_ref, c_ref, su_ref, sd_ref,
                              *weights, o_ref, ckvn_ref))


def _mla_ctx_body(cq_ref, ckv_ref, kr_ref, qg_ref, kg_ref, wq_ref, wkv_ref, o_ref, ckvn_ref):
    q = _dot(_rms_norm(cq_ref[...].astype(F32), qg_ref[...]), wq_ref[...])
    qn, qr = q[:, 0:MLA_QN], q[:, MLA_QN:]
    ckvn = _rms_norm(ckv_ref[...].astype(F32), kg_ref[...])
    ckvn_ref[...] = ckvn
    kv = _dot(ckvn, wkv_ref[...])
    k_cat = _mla_keys(kv[:, 0:MLA_QN], kr_ref[...].astype(F32))
    vv = kv[:, MLA_QN:].astype(BF16)
    for g in range(CTX_SEQS_PER_STEP):
        sl = slice(g * SEQ, (g + 1) * SEQ)
        _mla_attend(qn[sl], qr[sl], k_cat[sl], vv[sl], o_ref, g * SEQ)


def _mla_lat_body(cq_ref, ckv_ref, kr_ref, cckv_ref, ckr_ref, c_ref, su_ref, sd_ref,
                  qg_ref, kg_ref, wq_ref, wkv_ref, o_ref, ckvn_ref):
    half = MLA_ROPE // 2
    c, su, sd = c_ref[...], su_ref[...], sd_ref[...]
    q = _dot(_rms_norm(cq_ref[...].astype(F32), qg_ref[...]), wq_ref[...])
    qn = q[:, 0:MLA_QN]
    qr = _rope128(q[:, MLA_QN:], c, su, sd, half)
    ckvn = _rms_norm(ckv_ref[...].astype(F32), kg_ref[...])
    ckvn_ref[...] = ckvn
    ckv_all = jnp.concatenate([ckvn, cckv_ref[...]], 0)
    kv = _dot(ckv_all, wkv_ref[...])
    vv = kv[:, MLA_QN:].astype(BF16)
    kr = jnp.concatenate([_rope128(kr_ref[...].astype(F32), c, su, sd, half), ckr_ref[...]], 0)
    k_cat = _mla_keys(kv[:, 0:MLA_QN], kr)
    rows_per_call = 256
    for n in range(DEC_SEQ // rows_per_call):
        rows = slice(n * rows_per_call, (n + 1) * rows_per_call)
        _mla_attend(qn[rows], qr[rows], k_cat, vv, o_ref, n * rows_per_call)


def _mla(z_main, cache_ckv, cache_kr_pad, weights, layer):
    c, su, sd, _ = _rope_tables(DEC_SEQ, MLA_ROPE, LANE)
    tab = _const_spec((DEC_SEQ, LANE))
    cache = pl.BlockSpec((None, None, PAST_LEN, LANE), lambda i: (_lat_index(i), layer, 0, 0))

    def weight(*shape):
        return pl.BlockSpec((None,) + shape, lambda i: (layer, 0, 0))

    return pl.pallas_call(
        _mla_kernel,
        out_shape=(jax.ShapeDtypeStruct((T_ALL, MLA_HEADS * MLA_V), BF16),
                   jax.ShapeDtypeStruct((T_ALL, MLA_KV_LORA), F32)),
        grid=(MIXER_STEPS,),
        in_specs=[
            pl.BlockSpec((STEP_ROWS, 256), lambda i: (i, COL_CQ // 256)),
            pl.BlockSpec((STEP_ROWS, LANE), lambda i: (i, COL_CKV // LANE)),
            pl.BlockSpec((STEP_ROWS, LANE), lambda i: (i, COL_KROPE // LANE)),
            cache, cache, tab, tab, tab,
            weight(1, MLA_Q_LORA), weight(1, MLA_KV_LORA),
            weight(MLA_Q_LORA, MLA_QN + MLA_HEADS * MLA_ROPE),
            weight(MLA_KV_LORA, MLA_QN + MLA_HEADS * MLA_V),
        ],
        out_specs=(pl.BlockSpec((STEP_ROWS, 256), lambda i: (i, 0)),
                   pl.BlockSpec((STEP_ROWS, MLA_KV_LORA), lambda i: (i, 0))),
        compiler_params=_params("parallel"),
        name="mla",
    )(z_main, z_main, z_main, cache_ckv, cache_kr_pad,
      jnp.asarray(c), jnp.asarray(su), jnp.asarray(sd), *weights)


def _route(logits_t, rb):
    scores = jax.nn.sigmoid(logits_t)
    biased = scores + rb
    sc = [scores[e:e + 1, :] for e in range(N_EXPERTS)]
    bi = [biased[e:e + 1, :] for e in range(N_EXPERTS)]
    epg = EXPERTS_PER_GROUP
    gsum = []
    for g in range(N_GROUPS):
        v = bi[g * epg:(g + 1) * epg]
        best = None
        for a in range(epg):
            for b in range(a + 1, epg):
                pair = v[a] + v[b]
                best = pair if best is None else jnp.maximum(best, pair)
        gsum.append(best)
    combine = []
    sel = []
    for g in range(N_GROUPS):
        is_best = None
        for g2 in range(N_GROUPS):
            if g2 == g:
                continue
            c = gsum[g] > gsum[g2] if g2 < g else gsum[g] >= gsum[g2]
            is_best = c if is_best is None else jnp.logical_and(is_best, c)
        for a in range(epg):
            e = g * epg + a
            rank = jnp.zeros_like(bi[e])
            for b in range(epg):
                if b == a:
                    continue
                e2 = g * epg + b
                ahead = bi[e2] >= bi[e] if b < a else bi[e2] > bi[e]
                rank = rank + jnp.where(ahead, 1.0, 0.0)
            sel.append(jnp.logical_and(is_best, rank < 2.0))
    wsum = None
    for e in range(N_EXPERTS):
        w = jnp.where(sel[e], sc[e], 0.0)
        wsum = w if wsum is None else wsum + w
    for e in range(N_EXPERTS):
        combine.append(jnp.where(sel[e], ROUTE_SCALE * sc[e] / wsum, 0.0))
    return jnp.concatenate(combine, 0)


MERGE_TILE = 512

MERGE_BRANCH_ROWS = (HY_W, WIN_HEADS * WIN_HD, RET_HEADS * RET_DV, MLA_HEADS * MLA_V)
MERGE_ROWS = sum(MERGE_BRANCH_ROWS) + D_MODEL


def _merge_kernel(ya_ref, yb_ref, yc_ref, yd_ref, gt_ref, xc_ref, xl_ref, m_ref,
                  w_ref, g_ref, b_ref, rw_ref, rb_ref,
                  x1_ref, h2_ref, cmb_ref, *, ctx_tiles, sub_rows):
    D = D_MODEL
    rw = rw_ref[...]
    rw_hi = rw.astype(BF16)
    rw_lo = (rw - rw_hi.astype(F32)).astype(BF16)
    g1 = m_ref[:, 2 * D:3 * D]
    s2 = m_ref[:, 3 * D:4 * D]
    sc2 = m_ref[:, 4 * D:5 * D]
    is_ctx = pl.program_id(0) < ctx_tiles
    offs = np.cumsum((0,) + MERGE_BRANCH_ROWS)
    branches = tuple((y_ref, slice(int(offs[i]), int(offs[i + 1])))
                     for i, y_ref in enumerate((ya_ref, yb_ref, yc_ref, yd_ref)))
    w_out_rows = slice(int(offs[-1]), MERGE_ROWS)
    for r0 in range(0, x1_ref.shape[0], sub_rows):
        rows = slice(r0, r0 + sub_rows)
        merged = None
        for i, (y_ref, w_rows) in enumerate(branches):
            t = gt_ref[rows, i * D:(i + 1) * D] * jnp.dot(
                y_ref[rows, :], w_ref[w_rows, :], preferred_element_type=F32).astype(BF16)
            merged = t if merged is None else merged + t
        out1 = jnp.dot(merged, w_ref[w_out_rows, :], preferred_element_type=F32)
        x = jnp.where(is_ctx, xc_ref[rows, :], xl_ref[rows, :])
        x1 = _layer_norm(ALPHA * x + g1 * out1) * g_ref[...] + b_ref[...]
        x1_ref[rows, :] = x1
        h2 = _layer_norm(x1) * (1.0 + sc2) + s2
        h2_hi = h2.astype(BF16)
        h2_ref[rows, :] = h2_hi
        h2_lo = (h2 - h2_hi.astype(F32)).astype(BF16)
        logits = (jnp.dot(h2_hi, rw_hi, preferred_element_type=F32)
                  + (jnp.dot(h2_lo, rw_hi, preferred_element_type=F32)
                     + jnp.dot(h2_hi, rw_lo, preferred_element_type=F32)))
        combine_t = _route(logits.T[0:N_EXPERTS], rb_ref[...])
        cmb_ref[rows, :] = jnp.concatenate(
            [combine_t, jnp.zeros((LANE - N_EXPERTS, sub_rows), F32)], 0).T


def _merge(ya, yb, yc, yd, gates, x_ctx, x_lat, lat_block0, mods, w_merge, ln1_g, ln1_b,
           router_w, router_b, layer):
    tm = MERGE_TILE
    row = _mod_row(tm)
    D = D_MODEL
    ctx_tiles = T_CTX // tm

    def tile(w):
        return pl.BlockSpec((tm, w), lambda i: (i, 0))

    def weight(k, n):
        return pl.BlockSpec((None, k, n), lambda i: (layer, 0, 0))

    return pl.pallas_call(
        functools.partial(_merge_kernel, ctx_tiles=ctx_tiles, sub_rows=256),
        out_shape=(jax.ShapeDtypeStruct((T_ALL, D), F32),
                   jax.ShapeDtypeStruct((T_ALL, D), BF16),
                   jax.ShapeDtypeStruct((T_ALL, LANE), F32)),
        grid=(T_ALL // tm,),
        in_specs=[
            tile(256), tile(256), tile(512), tile(256), tile(4 * D),
            *_x_specs(tm, ctx_tiles, lat_block0),
            _mod_spec(layer, row),
            weight(MERGE_ROWS, D), weight(1, D), weight(1, D),
            pl.BlockSpec((D, LANE), lambda i: (0, 0)),
            pl.BlockSpec((N_EXPERTS, 1), lambda i: (0, 0)),
        ],
        out_specs=(tile(D), tile(D), tile(LANE)),
        compiler_params=_params("parallel"),
        name="merge",
    )(ya, yb, yc, yd, gates, x_ctx, x_lat, mods, w_merge,
      ln1_g.reshape(DEPTH, 1, D), ln1_b.reshape(DEPTH, 1, D), router_w,
      router_b.reshape(N_EXPERTS, 1))


MOE_EXPERTS_PER_STEP = 2


def _moe_kernel(*refs, next_h):
    if next_h:
        (h_ref, c_ref, x1_ref, m_ref, wg_ref, wu_ref, wd_ref, g_ref, b_ref, mn_ref,
         o_ref, hn_ref, acc_ref) = refs
    else:
        h_ref, c_ref, x1_ref, m_ref, wg_ref, wu_ref, wd_ref, g_ref, b_ref, o_ref, acc_ref = refs
    eg = pl.program_id(1)

    @pl.when(eg == 0)
    def _():
        acc_ref[...] = jnp.zeros_like(acc_ref)

    h = h_ref[...]
    cmb = c_ref[...]
    lane = lax.broadcasted_iota(jnp.int32, cmb.shape, 1)
    hid = []
    for k in range(MOE_EXPERTS_PER_STEP):
        gate = jnp.dot(h, wg_ref[k].astype(BF16), preferred_element_type=F32)
        up = jnp.dot(h, wu_ref[k].astype(BF16), preferred_element_type=F32)
        e = eg * MOE_EXPERTS_PER_STEP + k
        ce = jnp.sum(jnp.where(lane == e, cmb, 0.0), -1, keepdims=True)
        sig = 0.5 * jnp.tanh(0.5 * gate) + 0.5
        hid.append((gate * sig * (up * ce)).astype(BF16))
    wd = wd_ref[...].reshape(MOE_EXPERTS_PER_STEP * D_EXPERT, D_MODEL).astype(BF16)
    acc_ref[...] += jnp.dot(jnp.concatenate(hid, 1), wd, preferred_element_type=F32)

    @pl.when(eg == N_EXPERTS // MOE_EXPERTS_PER_STEP - 1)
    def _():
        g2 = m_ref[:, 5 * D_MODEL:6 * D_MODEL]
        y = _layer_norm(ALPHA * x1_ref[...] + g2 * acc_ref[...])
        y = y * g_ref[...] + b_ref[...]
        o_ref[...] = y
        if next_h:
            s1 = mn_ref[:, 0:D_MODEL]
            sc1 = mn_ref[:, D_MODEL:2 * D_MODEL]
            hn_ref[...] = (_layer_norm(y) * (1.0 + sc1) + s1).astype(hn_ref.dtype)


def _moe(h2, combine, x1, mods, w_gate, w_up, w_down, ln2_g, ln2_b, layer, row0, n_rows,
         next_h=False):
    tm = 1024
    row = _mod_row(tm)
    D = D_MODEL
    t0 = row0 // tm
    eps = MOE_EXPERTS_PER_STEP
    mod_spec = _mod_spec(layer, lambda i, e: row(t0 + i))
    in_specs = [
        pl.BlockSpec((tm, D), lambda i, e: (t0 + i, 0)),
        pl.BlockSpec((tm, LANE), lambda i, e: (t0 + i, 0)),
        pl.BlockSpec((tm, D), lambda i, e: (t0 + i, 0)),
        mod_spec,
        pl.BlockSpec((None, eps, D, D_EXPERT), lambda i, e: (layer, e, 0, 0)),
        pl.BlockSpec((None, eps, D, D_EXPERT), lambda i, e: (layer, e, 0, 0)),
        pl.BlockSpec((None, eps, D_EXPERT, D), lambda i, e: (layer, e, 0, 0)),
        pl.BlockSpec((None, 1, D), lambda i, e: (layer, 0, 0)),
        pl.BlockSpec((None, 1, D), lambda i, e: (layer, 0, 0)),
    ]
    args = [h2, combine, x1, mods, w_gate, w_up, w_down,
            ln2_g.reshape(DEPTH, 1, D), ln2_b.reshape(DEPTH, 1, D)]
    out_shape = jax.ShapeDtypeStruct((n_rows, D), F32)
    out_specs = pl.BlockSpec((tm, D), lambda i, e: (i, 0))
    if next_h:
        in_specs.append(_mod_spec(layer + 1, lambda i, e: row(t0 + i)))
        args.append(mods)
        out_shape = (out_shape, jax.ShapeDtypeStruct((T_ALL, D), BF16))
        assert n_rows == T_ALL
        out_specs = (out_specs, pl.BlockSpec((tm, D), lambda i, e: (i, 0)))
    return pl.pallas_call(
        functools.partial(_moe_kernel, next_h=next_h),
        out_shape=out_shape,
        grid=(n_rows // tm, N_EXPERTS // eps),
        in_specs=in_specs,
        out_specs=out_specs,
        scratch_shapes=[pltpu.VMEM((tm, D), F32)],
        compiler_params=_params("parallel", "arbitrary"),
        name="moe",
    )(*args)


def kernel(x_prompt, x_sample, cache_win_k, cache_win_v, cache_mla_ckv, cache_mla_krope,
           state_ret_fwd, state_ret_bwd, c, c_ctx, w_ada, b_ada, w_in,
           hy_conv_w, hy_conv_b, hy_w1, hy_b1, hy_w2, hy_b2, hy_w3, hy_bias,
           win_sink, ret_decay_fwd, ret_decay_bwd, mla_q_norm, mla_kv_norm, mla_w_uq, mla_w_ukv,
           w_br_a, w_br_b, w_br_c, w_br_d, w_out, ln1_g, ln1_b, ln2_g, ln2_b,
           router_w, router_b, moe_w_gate, moe_w_up, moe_w_down):
    D = D_MODEL
    x_ctx = x_prompt.reshape(T_CTX, D)
    x_lat = x_sample.reshape(T_LAT, D)
    lat_block0 = 0

    mods = _ada_mods(c_ctx, c, w_ada, b_ada)

    w_in_t = jnp.swapaxes(w_in, 1, 2).reshape(DEPTH * IN_COLS, D)
    cache_k = cache_win_k.reshape(DEC_BATCH, DEPTH, PAST_LEN, WIN_KV_HEADS * WIN_HD)
    cache_v = cache_win_v.reshape(DEC_BATCH, DEPTH, PAST_LEN, WIN_KV_HEADS * WIN_HD)
    cache_kr = jnp.pad(cache_mla_krope, ((0, 0), (0, 0), (0, 0), (0, LANE - MLA_ROPE)))

    uq = mla_w_uq.reshape(DEPTH, MLA_Q_LORA, MLA_HEADS, MLA_NOPE + MLA_ROPE)
    ukv = mla_w_ukv.reshape(DEPTH, MLA_KV_LORA, MLA_HEADS, MLA_NOPE + MLA_V)
    mla_weights = (
        mla_q_norm.reshape(DEPTH, 1, MLA_Q_LORA),
        mla_kv_norm.reshape(DEPTH, 1, MLA_KV_LORA),
        jnp.concatenate([uq[..., :MLA_NOPE].reshape(DEPTH, MLA_Q_LORA, MLA_HEADS * MLA_NOPE),
                         uq[..., MLA_NOPE:].reshape(DEPTH, MLA_Q_LORA, MLA_HEADS * MLA_ROPE)], -1),
        jnp.concatenate([ukv[..., :MLA_NOPE].reshape(DEPTH, MLA_KV_LORA, MLA_HEADS * MLA_NOPE),
                         ukv[..., MLA_NOPE:].reshape(DEPTH, MLA_KV_LORA, MLA_HEADS * MLA_V)], -1),
    )

    hy_w1p = jnp.pad(hy_w1, ((0, 0), (0, LANE - HY_EMB), (0, 0)))
    dft = {}
    for L in (SEQ, DEC_SEQ):
        fwd, inv = _dft_tables(L)
        dft[L] = (jnp.asarray(fwd).astype(BF16), jnp.asarray(inv).astype(BF16))
    router_w_pad = jnp.pad(router_w, ((0, 0), (0, LANE - N_EXPERTS)))
    w_merge = jnp.concatenate([w_br_a, w_br_b, w_br_c, w_br_d, w_out], 1).astype(BF16)

    new_k, new_v, new_ckv, new_kr, new_sf, new_sb = [], [], [], [], [], []
    for l in range(DEPTH):
        if l == 0:
            h = _ln_mod(x_ctx, x_lat, mods, l)
        z = _in_proj(h, w_in_t, l, 0, Z_MAIN, Z_MAIN // 2, BF16, gate=False)
        gates = _in_proj(h, w_in_t, l, COL_GATE, 4 * D, D, BF16, gate=True)

        filters = {L: _hy_filters(L, hy_w1p[l], hy_b1[l][None], hy_w2[l], hy_b2[l][None], hy_w3[l],
                                  dft[L][0]) for L in (SEQ, DEC_SEQ)}
        ya = _hyena(z, l, hy_conv_w, hy_conv_b, hy_bias, filters, dft)
        yb = _win(z, win_sink, cache_k, cache_v, l)
        yc, sf, sb = _retention(z, ret_decay_fwd, ret_decay_bwd, state_ret_fwd, state_ret_bwd, l)
        yd, ckvn = _mla(z, cache_mla_ckv, cache_kr, mla_weights, l)

        x1, h2, combine = _merge(ya, yb, yc, yd, gates, x_ctx, x_lat, lat_block0, mods, w_merge,
                                   ln1_g, ln1_b, router_w_pad, router_b, l)
        moe_args = (h2, combine, x1, mods, moe_w_gate, moe_w_up, moe_w_down, ln2_g, ln2_b, l)
        if l + 1 < DEPTH:
            x_ctx, h = _moe(*moe_args, 0, T_ALL, next_h=True)
            x_lat, lat_block0 = x_ctx, T_CTX // MERGE_TILE
        else:
            x_ctx = _moe(*moe_args, 0, T_CTX)
            x_lat = _moe(*moe_args, T_CTX, T_LAT)

        def ctx_cols(col, width):
            return z[:T_CTX, col:col + width].astype(F32)

        new_k.append(ctx_cols(COL_WK, 128).reshape(BATCH, SEQ, WIN_KV_HEADS, WIN_HD))
        new_v.append(ctx_cols(COL_WV, 128).reshape(BATCH, SEQ, WIN_KV_HEADS, WIN_HD))
        new_ckv.append(ckvn[:T_CTX].reshape(BATCH, SEQ, MLA_KV_LORA))
        new_kr.append(ctx_cols(COL_KROPE, MLA_ROPE).reshape(BATCH, SEQ, MLA_ROPE))
        new_sf.append(sf[:BATCH])
        new_sb.append(sb[:BATCH])

    y_prompt = x_ctx.reshape(BATCH, SEQ, D)
    y_sample = x_lat.reshape(DEC_BATCH, DEC_SEQ, D)
    return (y_prompt, y_sample, jnp.stack(new_k, 1), jnp.stack(new_v, 1), jnp.stack(new_ckv, 1),
            jnp.stack(new_kr, 1), jnp.stack(new_sf, 1), jnp.stack(new_sb, 1))
```

```python
import functools
import math

import numpy as np
import jax
import jax.numpy as jnp
from jax import lax
from jax.experimental import pallas as pl
from jax.experimental.pallas import tpu as pltpu

F32 = jnp.float32
BF16 = jnp.bfloat16

D_MODEL = 1024
BATCH = 16
SEQ = 256
DEPTH = 2
DEC_BATCH = 2
DEC_SEQ = 1024
PAST_LEN = 256
GRID_W = 64
CHUNK = 128
ROPE_BASE = 10000.0
NEG = -1e30
LN_EPS = 1e-5
RMS_EPS = 1e-6
LOG2E = math.log2(math.e)

HY_W = 256
HY_BANDS = 16
HY_EMB = 1 + 2 * HY_BANDS
HY_FFN = 64
HY_FAST_DECAY = 0.3
HY_SLOW_DECAY = 1.5
HY_TARGET = 1e-2

WIN_HEADS = 4
WIN_KV_HEADS = 2
WIN_HD = 64
WINDOW = 128

RET_HEADS = 4
RET_DK = 64
RET_DV = 128

MLA_HEADS = 4
MLA_Q_LORA = 256
MLA_KV_LORA = 128
MLA_NOPE = 64
MLA_ROPE = 32
MLA_V = 64

N_EXPERTS = 16
N_GROUPS = 4
EXPERTS_PER_GROUP = N_EXPERTS // N_GROUPS
D_EXPERT = 256
ROUTE_SCALE = 2.5

ALPHA = (2.0 * DEPTH) ** 0.25

T_CTX = BATCH * SEQ
T_LAT = DEC_BATCH * DEC_SEQ
T_ALL = T_CTX + T_LAT

COL_HY = 0
COL_WQ = 768
COL_WK = 1024
COL_WV = 1152
COL_RQ = 1280
COL_RK = 1536
COL_RV = 1792
COL_RG = 2304
COL_CQ = 2816
COL_CKV = 3072
COL_KROPE = 3200
COL_GATE = 3232
IN_COLS = COL_GATE + 4 * D_MODEL
Z_MAIN = 3328

LANE = 128
STEP_ROWS = 1024
CTX_SEQS_PER_STEP = STEP_ROWS // SEQ
CTX_STEPS = T_CTX // STEP_ROWS
MIXER_STEPS = T_ALL // STEP_ROWS
VMEM_LIMIT = 56 * 1024 * 1024


def _params(*sem):
    return pltpu.CompilerParams(dimension_semantics=sem, vmem_limit_bytes=VMEM_LIMIT)


def _dot(a, b):
    return jnp.dot(a.astype(BF16), b.astype(BF16), preferred_element_type=F32)


def _dot_split(a, b):
    a_hi = a.astype(BF16)
    a_lo = (a - a_hi.astype(F32)).astype(BF16)
    b_hi = b.astype(BF16)
    b_lo = (b - b_hi.astype(F32)).astype(BF16)

    def mm(x, y):
        return jnp.dot(x, y, preferred_element_type=F32)

    return mm(a_hi, b_hi) + (mm(a_lo, b_hi) + mm(a_hi, b_lo))


def _dot_nt(a, b):
    return lax.dot_general(a.astype(BF16), b.astype(BF16), (((1,), (1,)), ((), ())),
                           preferred_element_type=F32)


def _dot_tn(a, b):
    return lax.dot_general(a.astype(BF16), b.astype(BF16), (((0,), (0,)), ((), ())),
                           preferred_element_type=F32)


def _layer_norm(x):
    mu = jnp.mean(x, -1, keepdims=True)
    xc = x - mu
    var = jnp.mean(xc * xc, -1, keepdims=True)
    return xc * lax.rsqrt(var + LN_EPS)


def _mod_row(tile_rows):
    def row(i):
        start = i * tile_rows
        return jnp.where(start < T_CTX, 0, 1 + (start - T_CTX) // DEC_SEQ)
    return row


@functools.lru_cache(maxsize=None)
def _dft_tables(L):
    f = np.arange(L, dtype=np.int64)[:, None]
    s = np.arange(L, dtype=np.int64)[None, :]
    ang = np.pi * ((f * s) % (2 * L)).astype(np.float64) / L
    cos = np.cos(ang)
    sin = np.sin(ang)
    alt = np.where(np.arange(L) % 2 == 0, 1.0, -1.0)
    fwd_im = -sin
    fwd_im[0, :] = alt
    fwd = np.concatenate([cos, fwd_im], 0)
    inv_re = cos.T / L
    inv_re[:, 0] = 1.0 / (2 * L)
    inv_im = -sin.T / L
    inv_im[:, 0] = alt / (2 * L)
    inv = np.concatenate([inv_re, inv_im], 1)
    return fwd.astype(np.float32), inv.astype(np.float32)


@functools.lru_cache(maxsize=None)
def _hyena_embedding(L):
    t01 = np.linspace(0.0, 1.0, L, dtype=np.float64)[:, None]
    bands = np.linspace(1e-4, HY_BANDS - 1, HY_BANDS, dtype=np.float64)
    ang = (2.0 * math.pi / L) * np.arange(L, dtype=np.float64)[:, None] * bands[None, :]
    z = np.concatenate([t01, np.cos(ang), -np.sin(ang)], -1)
    zp = np.zeros((L, LANE), np.float64)
    zp[:, :HY_EMB] = z
    deltas = np.abs(np.linspace(math.log(HY_TARGET) / HY_SLOW_DECAY,
                                math.log(HY_TARGET) / HY_FAST_DECAY, HY_W, dtype=np.float64))
    return zp.astype(np.float32), deltas[None, :].astype(np.float32)


@functools.lru_cache(maxsize=None)
def _rope_tables(L, rot_dim, width):
    rows = L // GRID_W
    n_freq = rot_dim // 4
    half = rot_dim // 2
    inv = ROPE_BASE ** (-np.arange(n_freq, dtype=np.float64) / n_freq)
    pos = np.arange(L)
    row = (pos // GRID_W).astype(np.float64)
    col = (pos % GRID_W).astype(np.float64)
    ang = np.concatenate([row[:, None] * inv, col[:, None] * inv], -1)
    cos, sin = np.cos(ang), np.sin(ang)
    zero = np.zeros_like(sin)
    c = np.tile(np.concatenate([cos, cos], -1), (1, width // rot_dim))
    s_up = np.tile(np.concatenate([-sin, zero], -1), (1, width // rot_dim))
    s_dn = np.tile(np.concatenate([zero, sin], -1), (1, width // rot_dim))
    return c.astype(np.float32), s_up.astype(np.float32), s_dn.astype(np.float32), half


def _rope128(x, c, s_up, s_dn, half):
    up = pltpu.roll(x, LANE - half, axis=1)
    dn = pltpu.roll(x, half, axis=1)
    return x * c + up * s_up + dn * s_dn


MOD_ROWS = 1 + DEC_BATCH


def _ada_kernel(ct_ref, w_ref, b_ref, o_ref):
    @pl.when(pl.program_id(1) == 0)
    def _():
        for r in range(MOD_ROWS):
            o_ref[r] = b_ref[...]

    ct = ct_ref[...]
    s = ct * jax.nn.sigmoid(ct)
    tk, n = w_ref.shape
    rows, cols = 64, 4 * LANE
    s_cols = [[s[k0:k0 + rows, r:r + 1] for k0 in range(0, tk, rows)] for r in range(MOD_ROWS)]
    for c0 in range(0, n, cols):
        acc = [None] * MOD_ROWS
        for ki, k0 in enumerate(range(0, tk, rows)):
            wc = w_ref[k0:k0 + rows, c0:c0 + cols]
            for r in range(MOD_ROWS):
                part = jnp.sum((wc * s_cols[r][ki]).reshape(rows // 8, 8, cols), axis=0)
                acc[r] = part if acc[r] is None else acc[r] + part
        for r in range(MOD_ROWS):
            o_ref[r, :, c0:c0 + cols] += jnp.sum(acc[r], axis=0, keepdims=True)


def _ada_mods(c_ctx, c, w_ada, b_ada):
    tk = 256
    n = 6 * D_MODEL
    c_cols = jnp.concatenate([c_ctx[:, None], c.T], 1)
    return pl.pallas_call(
        _ada_kernel,
        out_shape=jax.ShapeDtypeStruct((DEPTH, MOD_ROWS, 1, n), F32),
        grid=(DEPTH, D_MODEL // tk),
        in_specs=[
            pl.BlockSpec((tk, MOD_ROWS), lambda l, k: (k, 0)),
            pl.BlockSpec((None, tk, n), lambda l, k: (l, k, 0)),
            pl.BlockSpec((None, 1, n), lambda l, k: (l, 0, 0)),
        ],
        out_specs=pl.BlockSpec((None, MOD_ROWS, 1, n), lambda l, k: (l, 0, 0, 0)),
        compiler_params=_params("parallel", "arbitrary"),
        name="ada_mods",
    )(c_cols, w_ada, b_ada.reshape(DEPTH, 1, n))


def _mod_spec(layer, row_of_step):
    return pl.BlockSpec((None, None, 1, 6 * D_MODEL), lambda *g: (layer, row_of_step(*g), 0, 0))


def _lnmod_kernel(xc_ref, xl_ref, m_ref, h_ref, *, ctx_tiles):
    x = jnp.where(pl.program_id(0) < ctx_tiles, xc_ref[...], xl_ref[...])
    y = _layer_norm(x)
    s1 = m_ref[:, 0:D_MODEL]
    sc1 = m_ref[:, D_MODEL:2 * D_MODEL]
    h_ref[...] = (y * (1.0 + sc1) + s1).astype(h_ref.dtype)


def _x_specs(tm, ctx_tiles, lat_block0=0):
    return [pl.BlockSpec((tm, D_MODEL), lambda i: (jnp.minimum(i, ctx_tiles - 1), 0)),
            pl.BlockSpec((tm, D_MODEL), lambda i: (jnp.maximum(i - ctx_tiles, 0) + lat_block0, 0))]


def _ln_mod(x_ctx, x_lat, mods, layer):
    tm = 512
    ctx_tiles = T_CTX // tm
    return pl.pallas_call(
        functools.partial(_lnmod_kernel, ctx_tiles=ctx_tiles),
        out_shape=jax.ShapeDtypeStruct((T_ALL, D_MODEL), BF16),
        grid=(T_ALL // tm,),
        in_specs=_x_specs(tm, ctx_tiles) + [_mod_spec(layer, _mod_row(tm))],
        out_specs=pl.BlockSpec((tm, D_MODEL), lambda i: (i, 0)),
        compiler_params=_params("parallel"),
        name="ln_mod",
    )(x_ctx, x_lat, mods)


def _proj_kernel(h_ref, w_ref, o_ref, wb_ref, *, gate):
    @pl.when(pl.program_id(1) == 0)
    def _():
        wb_ref[...] = w_ref[...].T.astype(BF16)

    if not gate:
        o_ref[...] = jnp.dot(h_ref[...], wb_ref[...], preferred_element_type=F32).astype(o_ref.dtype)
        return
    sub = 2 * LANE
    for c0 in range(0, o_ref.shape[1], sub):
        r = jnp.dot(h_ref[...], wb_ref[:, c0:c0 + sub], preferred_element_type=F32)
        rb = r.astype(o_ref.dtype)
        o_ref[:, c0:c0 + sub] = 0.5 * jnp.tanh(0.5 * rb) + 0.5


def _in_proj(h, w_t, layer, col0, n_cols, tn, out_dtype, gate):
    tm = 2048
    return pl.pallas_call(
        functools.partial(_proj_kernel, gate=gate),
        out_shape=jax.ShapeDtypeStruct((T_ALL, n_cols), out_dtype),
        grid=(n_cols // tn, T_ALL // tm),
        in_specs=[
            pl.BlockSpec((tm, D_MODEL), lambda j, i: (i, 0)),
            pl.BlockSpec((pl.Element(tn), pl.Element(D_MODEL)),
                         lambda j, i: (pl.multiple_of(layer * IN_COLS + col0 + j * tn, 8), 0)),
        ],
        out_specs=pl.BlockSpec((tm, tn), lambda j, i: (i, j)),
        scratch_shapes=[pltpu.VMEM((D_MODEL, tn), BF16)],
        compiler_params=_params("parallel", "arbitrary"),
        name="gate_proj" if gate else "in_proj",
    )(h, w_t)


def _hy_filter_kernel(z_ref, dl_ref, w1_ref, b1_ref, w2_ref, b2_ref, w3_ref, fwd_ref,
                      kre_ref, kim_ref, *, L):
    z = z_ref[...]
    a = jnp.sin(_dot_split(z, w1_ref[...]) + b1_ref[...])
    a = jnp.sin(_dot_split(a, w2_ref[...]) + b2_ref[...])
    h = _dot_split(a, w3_ref[...])
    decay = jnp.exp(-z[:, 0:1] * dl_ref[...])
    not_first = lax.broadcasted_iota(jnp.int32, (L, HY_W), 0) > 0
    sums, diffs = [], []
    for o in range(2):
        fw = h[:, (2 * o) * HY_W:(2 * o + 1) * HY_W] * decay
        bw = jnp.where(not_first, h[:, (2 * o + 1) * HY_W:(2 * o + 2) * HY_W] * decay, 0.0)
        sums.append(fw + bw)
        diffs.append(fw - bw)
    p = _dot(fwd_ref[...], jnp.concatenate(sums, 1))
    q = _dot(fwd_ref[L:2 * L, :], jnp.concatenate(diffs, 1))
    kre_ref[...] = p[0:L]
    first = lax.broadcasted_iota(jnp.int32, (L, 2 * HY_W), 0) == 0
    kim_ref[...] = jnp.where(first, p[L:L + 1], q)


def _hy_filters(L, w1p, b1, w2, b2, w3, fwd):
    zemb, deltas = _hyena_embedding(L)
    out = jax.ShapeDtypeStruct((L, 2 * HY_W), F32)
    return pl.pallas_call(
        functools.partial(_hy_filter_kernel, L=L),
        out_shape=(out, out),
        compiler_params=pltpu.CompilerParams(vmem_limit_bytes=VMEM_LIMIT),
        name=f"hy_filters_{L}",
    )(jnp.asarray(zemb), jnp.asarray(deltas), w1p, b1, w2, b2, w3, fwd)


def _group_step(ctx_body, lat_body, ctx_steps=CTX_STEPS):
    i = pl.program_id(0)
    pl.when(i < ctx_steps)(ctx_body)
    pl.when(i >= ctx_steps)(lat_body)


def _lat_index(i):
    return jnp.maximum(i - CTX_STEPS, 0)


HYENA_STEP_ROWS = T_LAT


def _hyena_kernel(hy_ref, cw_ref, cb_ref, bias_ref, kre_c, kim_c, fwd_c, inv_c,
                  kre_l, kim_l, fwd_l, inv_l, o_ref):
    _group_step(
        lambda: _hyena_body(hy_ref, cw_ref, cb_ref, bias_ref, kre_c, kim_c, fwd_c, inv_c, o_ref,
                            SEQ, HYENA_STEP_ROWS // SEQ),
        lambda: _hyena_body(hy_ref, cw_ref, cb_ref, bias_ref, kre_l, kim_l, fwd_l, inv_l, o_ref,
                            DEC_SEQ, HYENA_STEP_ROWS // DEC_SEQ),
        ctx_steps=T_CTX // HYENA_STEP_ROWS)


def _hyena_body(hy_ref, cw_ref, cb_ref, bias_ref, kre_ref, kim_ref, fwd_ref, inv_ref, o_ref, L, seqs):
    width = seqs * HY_W
    first = lax.broadcasted_iota(jnp.int32, (L, width), 0) == 0

    def tiled(t):
        return jnp.concatenate([t] * seqs, 1)

    def long_conv(u, o):
        uf = _dot(fwd_ref[...], u)
        ure, uim = uf[0:L], uf[L:2 * L]
        kre = tiled(kre_ref[:, o * HY_W:(o + 1) * HY_W])
        kim = tiled(kim_ref[:, o * HY_W:(o + 1) * HY_W])
        yre = jnp.where(first, ure * kre, ure * kre - uim * kim)
        yim = jnp.where(first, uim * kim, ure * kim + uim * kre)
        y = _dot(inv_ref[...], jnp.concatenate([yre, yim], 0))
        return y + u * tiled(bias_ref[o:o + 1, :])

    vs, x1s, x2s = [], [], []
    for g in range(seqs):
        x = hy_ref[g * L:(g + 1) * L, :].astype(F32)
        rows = lax.broadcasted_iota(jnp.int32, x.shape, 0)
        prev = jnp.where(rows == 0, 0.0, pltpu.roll(x, 1, axis=0))
        nxt = jnp.where(rows == L - 1, 0.0, pltpu.roll(x, L - 1, axis=0))
        z = prev * cw_ref[0:1, :] + x * cw_ref[1:2, :] + nxt * cw_ref[2:3, :] + cb_ref[...]
        vs.append(z[:, 0:HY_W])
        x1s.append(z[:, HY_W:2 * HY_W])
        x2s.append(z[:, 2 * HY_W:3 * HY_W])
    u = jnp.concatenate(x1s, 1) * long_conv(jnp.concatenate(vs, 1), 0)
    y = jnp.concatenate(x2s, 1) * long_conv(u, 1)
    for g in range(seqs):
        o_ref[g * L:(g + 1) * L, :] = y[:, g * HY_W:(g + 1) * HY_W].astype(o_ref.dtype)


def _const_spec(shape):
    return pl.BlockSpec(shape, lambda i: (0,) * len(shape))


def _hyena(z_main, layer, conv_w, conv_b, bias, filters, dft):
    tables, table_specs = [], []
    for L in (SEQ, DEC_SEQ):
        tables += [*filters[L], *dft[L]]
        table_specs += [_const_spec((L, 2 * HY_W)), _const_spec((L, 2 * HY_W)),
                        _const_spec((2 * L, L)), _const_spec((L, 2 * L))]
    return pl.pallas_call(
        _hyena_kernel,
        out_shape=jax.ShapeDtypeStruct((T_ALL, HY_W), BF16),
        grid=(T_ALL // HYENA_STEP_ROWS,),
        in_specs=[
            pl.BlockSpec((HYENA_STEP_ROWS, 3 * HY_W), lambda i: (i, 0)),
            pl.BlockSpec((None, 3, 3 * HY_W), lambda i: (layer, 0, 0)),
            pl.BlockSpec((None, 1, 3 * HY_W), lambda i: (layer, 0, 0)),
            pl.BlockSpec((None, 2, HY_W), lambda i: (layer, 0, 0)),
        ] + table_specs,
        out_specs=pl.BlockSpec((HYENA_STEP_ROWS, HY_W), lambda i: (i, 0)),
        compiler_params=_params("parallel"),
        name="hyena",
    )(z_main, conv_w, conv_b.reshape(DEPTH, 1, 3 * HY_W), bias, *tables)


def _win_masks():
    lane = lax.broadcasted_iota(jnp.int32, (1, LANE), 1)
    return lane < WIN_HD, lane >= WIN_HD


def _win_head_operands(q, k, v, h):
    lo_mask, hi_mask = _win_masks()
    col = h // 2
    lo = h % 2 == 0
    q128 = jnp.where(lo_mask if lo else hi_mask, q[:, col * LANE:(col + 1) * LANE], 0.0)
    swap = h in (1, 2)
    if swap:
        k = pltpu.roll(k, WIN_HD, axis=1)
        v = pltpu.roll(v, WIN_HD, axis=1)
    return q128, k, v, lo


def _win_kernel(sink_ref, q_ref, kv_ref, ck_ref, cv_ref, c_ref, su_ref, sd_ref, o_ref, *, layer):
    _group_step(
        lambda: _win_ctx_body(sink_ref, q_ref, kv_ref, o_ref, layer),
        lambda: _win_lat_body(sink_ref, q_ref, kv_ref, ck_ref, cv_ref, c_ref, su_ref, sd_ref, o_ref, layer))


def _win_ctx_body(sink_ref, q_ref, kv_ref, o_ref, layer):
    lo_mask, hi_mask = _win_masks()
    qscale = WIN_HD ** -0.5 * LOG2E
    for g in range(CTX_SEQS_PER_STEP):
        sl = slice(g * SEQ, (g + 1) * SEQ)
        q = q_ref[sl, :].astype(F32) * qscale
        k = kv_ref[sl, 0:LANE].astype(F32)
        v = kv_ref[sl, LANE:2 * LANE].astype(F32)
        cols = []
        for col in range(2):
            acc = None
            for h in (2 * col, 2 * col + 1):
                q128, kk, vv, lo = _win_head_operands(q, k, v, h)
                s = _dot_nt(q128, kk)
                sink = sink_ref[layer, h] * LOG2E
                m = jnp.maximum(jnp.max(s, -1, keepdims=True), sink)
                p = jnp.exp2(s - m)
                den = jnp.sum(p, -1, keepdims=True) + jnp.exp2(sink - m)
                o = _dot(p, vv) / den
                o = jnp.where(lo_mask if lo else hi_mask, o, 0.0)
                acc = o if acc is None else acc + o
            cols.append(acc)
        o_ref[sl, :] = jnp.concatenate(cols, 1).astype(o_ref.dtype)


def _win_lat_body(sink_ref, q_ref, kv_ref, ck_ref, cv_ref, c_ref, su_ref, sd_ref, o_ref, layer):
    L = DEC_SEQ
    half = WIN_HD // 2
    c, su, sd = c_ref[...], su_ref[...], sd_ref[...]
    qscale = WIN_HD ** -0.5 * LOG2E
    q = jnp.concatenate(
        [_rope128(q_ref[:, i * LANE:(i + 1) * LANE].astype(F32), c, su, sd, half) for i in range(2)],
        1) * qscale
    k = _rope128(kv_ref[:, 0:LANE].astype(F32), c, su, sd, half)
    v = kv_ref[:, LANE:2 * LANE].astype(F32)
    ck = ck_ref[...]
    cv = cv_ref[...]
    lo_mask, hi_mask = _win_masks()
    nb = L // CHUNK
    assert WINDOW == CHUNK
    rr = lax.broadcasted_iota(jnp.int32, (CHUNK, CHUNK), 0)
    cc = lax.broadcasted_iota(jnp.int32, (CHUNK, CHUNK), 1)
    band = {-1: jnp.where(cc >= rr, 0.0, NEG), 0: jnp.zeros((CHUNK, CHUNK), F32),
            1: jnp.where(cc <= rr, 0.0, NEG)}
    cols = []
    for col in range(2):
        acc_blocks = [None] * nb
        for h in (2 * col, 2 * col + 1):
            q128, kk, vv, lo = _win_head_operands(q, k, v, h)
            _, ckk, cvv, _ = _win_head_operands(q, ck, cv, h)
            sink = sink_ref[layer, h] * LOG2E
            for n in range(nb):
                blocks = [d for d in (-1, 0, 1) if 0 <= n + d < nb]
                k0 = (n + blocks[0]) * CHUNK
                k1 = (n + blocks[-1] + 1) * CHUNK
                qn = q128[n * CHUNK:(n + 1) * CHUNK]
                s_loc = _dot_nt(qn, kk[k0:k1]) + jnp.concatenate([band[d] for d in blocks], 1)
                s_ctx = _dot_nt(qn, ckk)
                m = jnp.maximum(jnp.maximum(jnp.max(s_loc, -1, keepdims=True),
                                            jnp.max(s_ctx, -1, keepdims=True)), sink)
                p_loc = jnp.exp2(s_loc - m)
                p_ctx = jnp.exp2(s_ctx - m)
                den = (jnp.sum(p_loc, -1, keepdims=True) + jnp.sum(p_ctx, -1, keepdims=True)
                       + jnp.exp2(sink - m))
                o = (_dot(p_loc, vv[k0:k1]) + _dot(p_ctx, cvv)) / den
                o = jnp.where(lo_mask if lo else hi_mask, o, 0.0)
                acc_blocks[n] = o if acc_blocks[n] is None else acc_blocks[n] + o
        cols.append(jnp.concatenate(acc_blocks, 0))
    o_ref[...] = jnp.concatenate(cols, 1).astype(o_ref.dtype)


def _win(z_main, sink, cache_k, cache_v, layer):
    c, su, sd, _ = _rope_tables(DEC_SEQ, WIN_HD, LANE)
    tab = _const_spec((DEC_SEQ, LANE))
    cache = pl.BlockSpec((None, None, PAST_LEN, LANE), lambda i: (_lat_index(i), layer, 0, 0))
    return pl.pallas_call(
        functools.partial(_win_kernel, layer=layer),
        out_shape=jax.ShapeDtypeStruct((T_ALL, WIN_HEADS * WIN_HD), BF16),
        grid=(MIXER_STEPS,),
        in_specs=[
            pl.BlockSpec(memory_space=pltpu.SMEM),
            pl.BlockSpec((STEP_ROWS, 256), lambda i: (i, COL_WQ // 256)),
            pl.BlockSpec((STEP_ROWS, 256), lambda i: (i, COL_WK // 256)),
            cache, cache, tab, tab, tab,
        ],
        out_specs=pl.BlockSpec((STEP_ROWS, 256), lambda i: (i, 0)),
        compiler_params=_params("parallel"),
        name="win",
    )(sink, z_main, z_main, cache_k, cache_v, jnp.asarray(c), jnp.asarray(su), jnp.asarray(sd))


def _ret_kernel(*refs, layer):
    _group_step(lambda: _ret_body(*refs, L=SEQ, layer=layer, ctx=True),
                lambda: _ret_body(*refs, L=DEC_SEQ, layer=layer, ctx=False))


def _ret_body(df_ref, db_ref, q_ref, k_ref, v0_ref, v1_ref, g0_ref, g1_ref, s0f_ref, s0b_ref,
              o_ref, sf_out, sb_out, s_ref, cross_ref, *, L, layer, ctx):
    seqs = STEP_ROWS // L
    if not ctx:
        sf_out[...] = jnp.zeros_like(sf_out)
        sb_out[...] = jnp.zeros_like(sb_out)
    C = CHUNK
    nc = L // C
    H = RET_HEADS
    qw = H * RET_DK
    vw = H * RET_DV

    def lane_table(width, per_head, fn):
        pos = lax.broadcasted_iota(jnp.int32, (C, per_head), 0).astype(F32)
        return jnp.concatenate([fn(h, pos) for h in range(H)], 1)

    def log_gamma(ref, h):
        d = jnp.full((1, 1), ref[layer, h], F32)
        return jnp.log(jax.nn.sigmoid(d))

    lgf = [log_gamma(df_ref, h) for h in range(H)]
    lgb = [log_gamma(db_ref, h) for h in range(H)]

    def tables(lg, reverse):
        if reverse:
            dq = lane_table(vw, RET_DV, lambda h, pos: jnp.exp((C - pos) * lg[h]))
            dk = lane_table(qw, RET_DK, lambda h, pos: jnp.exp(pos * lg[h]))
        else:
            dq = lane_table(vw, RET_DV, lambda h, pos: jnp.exp((pos + 1.0) * lg[h]))
            dk = lane_table(qw, RET_DK, lambda h, pos: jnp.exp((C - 1.0 - pos) * lg[h]))
        dc = jnp.concatenate([jnp.broadcast_to(jnp.exp(C * lg[h]), (1, RET_DV)) for h in range(H)], 1)
        return dq, dk, dc

    tab_f = tables(lgf, False)
    tab_b = tables(lgb, True)
    ii = lax.broadcasted_iota(jnp.int32, (C, C), 0)
    jj = lax.broadcasted_iota(jnp.int32, (C, C), 1)
    diff = (ii - jj).astype(F32)
    dmats = [jnp.where(diff >= 0, jnp.exp(jnp.maximum(diff, 0.0) * lgf[h]), 0.0)
             + jnp.where(diff <= 0, jnp.exp(jnp.maximum(-diff, 0.0) * lgb[h]), 0.0) for h in range(H)]
    dmat_stack = jnp.concatenate(dmats, 0)
    lane_q = lax.broadcasted_iota(jnp.int32, (1, qw), 1) // RET_DK

    srow = lax.broadcasted_iota(jnp.int32, (qw, vw), 0) // RET_DK
    scol = lax.broadcasted_iota(jnp.int32, (qw, vw), 1) // RET_DV
    diag = srow == scol

    for g in range(seqs):
        base = g * L
        rows_all = slice(base, base + L)
        q_all = q_ref[rows_all, :].astype(F32)
        k_all = k_ref[rows_all, :].astype(F32) * (RET_DK ** -0.5)
        v_all = jnp.concatenate([v0_ref[rows_all, :], v1_ref[rows_all, :]], 1).astype(F32)
        g_all = jnp.concatenate([g0_ref[rows_all, :], g1_ref[rows_all, :]], 1).astype(F32)

        def scan(tabs, reverse, s0_ref, s_out):
            dq, dk, dc = tabs
            if s0_ref is not None:
                s_ref[g] = jnp.zeros((qw, vw), F32)
                for h in range(H):
                    s_ref[g, h * RET_DK:(h + 1) * RET_DK, h * RET_DV:(h + 1) * RET_DV] = s0_ref[h]
            order = range(nc - 1, -1, -1) if reverse else range(nc)
            for step, ci in enumerate(order):
                sl = slice(ci * C, (ci + 1) * C)
                rs = slice(base + ci * C, base + (ci + 1) * C)
                qc, kc, vc = q_all[sl], k_all[sl], v_all[sl]
                upd = jnp.where(diag, _dot_tn(kc * dk, vc), 0.0)
                if s0_ref is None and step == 0:
                    if not reverse:
                        cross_ref[rs, :] = jnp.zeros((C, vw), F32)
                    s_ref[g] = upd
                    continue
                st = s_ref[g]
                cross = _dot(qc, st) * dq
                if reverse:
                    cross_ref[rs, :] = cross_ref[rs, :] + cross
                else:
                    cross_ref[rs, :] = cross
                s_ref[g] = st * dc + upd
            if s_out is not None:
                for h in range(H):
                    s_out[g, h] = s_ref[g, h * RET_DK:(h + 1) * RET_DK, h * RET_DV:(h + 1) * RET_DV]

        scan(tab_f, False, None if ctx else s0f_ref, sf_out if ctx else None)
        scan(tab_b, True, None if ctx else s0b_ref, sb_out if ctx else None)

        for ci in range(nc):
            sl = slice(ci * C, (ci + 1) * C)
            rs = slice(base + ci * C, base + (ci + 1) * C)
            q_stack = jnp.concatenate([jnp.where(lane_q == h, q_all[sl], 0.0) for h in range(H)], 0)
            att = _dot_nt(q_stack, k_all[sl]) * dmat_stack
            ov = _dot(att, v_all[sl])
            for h in range(H):
                hv = slice(h * RET_DV, (h + 1) * RET_DV)
                o = ov[h * C:(h + 1) * C, hv] + cross_ref[rs, hv]
                gt = g_all[sl, hv]
                o_ref[rs, hv] = ((gt * jax.nn.sigmoid(gt)) * _layer_norm(o)).astype(o_ref.dtype)


def _retention(z_main, dec_f, dec_b, s0f, s0b, layer):
    def zcol(col):
        return pl.BlockSpec((STEP_ROWS, 256), lambda i: (i, col // 256))

    smem = pl.BlockSpec(memory_space=pltpu.SMEM)
    z_specs = [zcol(COL_RQ), zcol(COL_RK), zcol(COL_RV), zcol(COL_RV + 256),
               zcol(COL_RG), zcol(COL_RG + 256)]
    s0_spec = pl.BlockSpec((None, None, RET_HEADS, RET_DK, RET_DV),
                           lambda i: (_lat_index(i), layer, 0, 0, 0))
    st_shape = jax.ShapeDtypeStruct((MIXER_STEPS * CTX_SEQS_PER_STEP, RET_HEADS, RET_DK, RET_DV), F32)
    st_spec = pl.BlockSpec((CTX_SEQS_PER_STEP, RET_HEADS, RET_DK, RET_DV), lambda i: (i, 0, 0, 0))
    return pl.pallas_call(
        functools.partial(_ret_kernel, layer=layer),
        out_shape=(jax.ShapeDtypeStruct((T_ALL, RET_HEADS * RET_DV), BF16), st_shape, st_shape),
        grid=(MIXER_STEPS,),
        in_specs=[smem, smem] + z_specs + [s0_spec, s0_spec],
        out_specs=(pl.BlockSpec((STEP_ROWS, RET_HEADS * RET_DV), lambda i: (i, 0)), st_spec, st_spec),
        scratch_shapes=[pltpu.VMEM((CTX_SEQS_PER_STEP, RET_HEADS * RET_DK, RET_HEADS * RET_DV), F32),
                        pltpu.VMEM((STEP_ROWS, RET_HEADS * RET_DV), F32)],
        compiler_params=_params("parallel"),
        name="retention",
    )(dec_f, dec_b, *([z_main] * 6), s0f, s0b)


def _rms_norm(x, g):
    return x * lax.rsqrt(jnp.mean(x * x, -1, keepdims=True) + RMS_EPS) * g


def _mla_keys(kn, kr):
    lane_r = lax.broadcasted_iota(jnp.int32, (1, LANE), 1)
    return jnp.concatenate([kn, jnp.where(lane_r < MLA_ROPE, kr, 0.0)], 1).astype(BF16)


def _mla_attend(qn, qr, k_cat, vv, o_ref, row0):
    qscale = (MLA_NOPE + MLA_ROPE) ** -0.5 * LOG2E
    lane_n = lax.broadcasted_iota(jnp.int32, (1, MLA_HEADS * MLA_NOPE), 1) // MLA_NOPE
    lane_r = lax.broadcasted_iota(jnp.int32, (1, LANE), 1)
    qn = qn * qscale
    qr = qr * qscale
    lq = qn.shape[0]
    heads = []
    for h in range(MLA_HEADS):
        qnh = jnp.where(lane_n == h, qn, 0.0)
        qrh = qr if h == 0 else pltpu.roll(qr, LANE - h * MLA_ROPE, axis=1)
        qrh = jnp.where(lane_r < MLA_ROPE, qrh, 0.0)
        heads.append(jnp.concatenate([qnh, qrh], 1).astype(BF16))
    s = _dot_nt(jnp.concatenate(heads, 0), k_cat)
    m = jnp.max(s, -1, keepdims=True)
    p = jnp.exp2(s - m)
    den = jnp.sum(p, -1, keepdims=True)
    o = _dot(p, vv) / den
    acc = None
    for h in range(MLA_HEADS):
        oh = jnp.where(lane_n == h, o[h * lq:(h + 1) * lq], 0.0)
        acc = oh if acc is None else acc + oh
    o_ref[row0:row0 + lq, :] = acc.astype(o_ref.dtype)


MLA_QN = MLA_HEADS * MLA_NOPE


def _mla_kernel(cq_ref, ckv_ref, kr_ref, cckv_ref, ckr_ref, c_ref, su_ref, sd_ref,
                qg_ref, kg_ref, wq_ref, wkv_ref, o_ref, ckvn_ref):
    weights = (qg_ref, kg_ref, wq_ref, wkv_ref)
    _group_step(
        lambda: _mla_ctx_body(cq_ref, ckv_ref, kr_ref, *weights, o_ref, ckvn_ref),
        lambda: _mla_lat_body(cq_ref, ckv_ref, kr_ref, cckv_ref, ckr_ref, c_ref, su_ref, sd_ref,
                              *weights, o_ref, ckvn_ref))


def _mla_ctx_body(cq_ref, ckv_ref, kr_ref, qg_ref, kg_ref, wq_ref, wkv_ref, o_ref, ckvn_ref):
    q = _dot(_rms_norm(cq_ref[...].astype(F32), qg_ref[...]), wq_ref[...])
    qn, qr = q[:, 0:MLA_QN], q[:, MLA_QN:]
    ckvn = _rms_norm(ckv_ref[...].astype(F32), kg_ref[...])
    ckvn_ref[...] = ckvn
    kv = _dot(ckvn, wkv_ref[...])
    k_cat = _mla_keys(kv[:, 0:MLA_QN], kr_ref[...].astype(F32))
    vv = kv[:, MLA_QN:].astype(BF16)
    for g in range(CTX_SEQS_PER_STEP):
        sl = slice(g * SEQ, (g + 1) * SEQ)
        _mla_attend(qn[sl], qr[sl], k_cat[sl], vv[sl], o_ref, g * SEQ)


def _mla_lat_body(cq_ref, ckv_ref, kr_ref, cckv_ref, ckr_ref, c_ref, su_ref, sd_ref,
                  qg_ref, kg_ref, wq_ref, wkv_ref, o_ref, ckvn_ref):
    half = MLA_ROPE // 2
    c, su, sd = c_ref[...], su_ref[...], sd_ref[...]
    q = _dot(_rms_norm(cq_ref[...].astype(F32), qg_ref[...]), wq_ref[...])
    qn = q[:, 0:MLA_QN]
    qr = _rope128(q[:, MLA_QN:], c, su, sd, half)
    ckvn = _rms_norm(ckv_ref[...].astype(F32), kg_ref[...])
    ckvn_ref[...] = ckvn
    ckv_all = jnp.concatenate([ckvn, cckv_ref[...]], 0)
    kv = _dot(ckv_all, wkv_ref[...])
    vv = kv[:, MLA_QN:].astype(BF16)
    kr = jnp.concatenate([_rope128(kr_ref[...].astype(F32), c, su, sd, half), ckr_ref[...]], 0)
    k_cat = _mla_keys(kv[:, 0:MLA_QN], kr)
    rows_per_call = 256
    for n in range(DEC_SEQ // rows_per_call):
        rows = slice(n * rows_per_call, (n + 1) * rows_per_call)
        _mla_attend(qn[rows], qr[rows], k_cat, vv, o_ref, n * rows_per_call)


def _mla(z_main, cache_ckv, cache_kr_pad, weights, layer):
    c, su, sd, _ = _rope_tables(DEC_SEQ, MLA_ROPE, LANE)
    tab = _const_spec((DEC_SEQ, LANE))
    cache = pl.BlockSpec((None, None, PAST_LEN, LANE), lambda i: (_lat_index(i), layer, 0, 0))

    def weight(*shape):
        return pl.BlockSpec((None,) + shape, lambda i: (layer, 0, 0))

    return pl.pallas_call(
        _mla_kernel,
        out_shape=(jax.ShapeDtypeStruct((T_ALL, MLA_HEADS * MLA_V), BF16),
                   jax.ShapeDtypeStruct((T_ALL, MLA_KV_LORA), F32)),
        grid=(MIXER_STEPS,),
        in_specs=[
            pl.BlockSpec((STEP_ROWS, 256), lambda i: (i, COL_CQ // 256)),
            pl.BlockSpec((STEP_ROWS, LANE), lambda i: (i, COL_CKV // LANE)),
            pl.BlockSpec((STEP_ROWS, LANE), lambda i: (i, COL_KROPE // LANE)),
            cache, cache, tab, tab, tab,
            weight(1, MLA_Q_LORA), weight(1, MLA_KV_LORA),
            weight(MLA_Q_LORA, MLA_QN + MLA_HEADS * MLA_ROPE),
            weight(MLA_KV_LORA, MLA_QN + MLA_HEADS * MLA_V),
        ],
        out_specs=(pl.BlockSpec((STEP_ROWS, 256), lambda i: (i, 0)),
                   pl.BlockSpec((STEP_ROWS, MLA_KV_LORA), lambda i: (i, 0))),
        compiler_params=_params("parallel"),
        name="mla",
    )(z_main, z_main, z_main, cache_ckv, cache_kr_pad,
      jnp.asarray(c), jnp.asarray(su), jnp.asarray(sd), *weights)


def _route(logits_t, rb):
    scores = jax.nn.sigmoid(logits_t)
    biased = scores + rb
    sc = [scores[e:e + 1, :] for e in range(N_EXPERTS)]
    bi = [biased[e:e + 1, :] for e in range(N_EXPERTS)]
    epg = EXPERTS_PER_GROUP
    gsum = []
    for g in range(N_GROUPS):
        v = bi[g * epg:(g + 1) * epg]
        best = None
        for a in range(epg):
            for b in range(a + 1, epg):
                pair = v[a] + v[b]
                best = pair if best is None else jnp.maximum(best, pair)
        gsum.append(best)
    combine = []
    sel = []
    for g in range(N_GROUPS):
        is_best = None
        for g2 in range(N_GROUPS):
            if g2 == g:
                continue
            c = gsum[g] > gsum[g2] if g2 < g else gsum[g] >= gsum[g2]
            is_best = c if is_best is None else jnp.logical_and(is_best, c)
        for a in range(epg):
            e = g * epg + a
            rank = jnp.zeros_like(bi[e])
            for b in range(epg):
                if b == a:
                    continue
                e2 = g * epg + b
                ahead = bi[e2] >= bi[e] if b < a else bi[e2] > bi[e]
                rank = rank + jnp.where(ahead, 1.0, 0.0)
            sel.append(jnp.logical_and(is_best, rank < 2.0))
    wsum = None
    for e in range(N_EXPERTS):
        w = jnp.where(sel[e], sc[e], 0.0)
        wsum = w if wsum is None else wsum + w
    for e in range(N_EXPERTS):
        combine.append(jnp.where(sel[e], ROUTE_SCALE * sc[e] / wsum, 0.0))
    return jnp.concatenate(combine, 0)


MERGE_TILE = 512

MERGE_BRANCH_ROWS = (HY_W, WIN_HEADS * WIN_HD, RET_HEADS * RET_DV, MLA_HEADS * MLA_V)
MERGE_ROWS = sum(MERGE_BRANCH_ROWS) + D_MODEL


def _merge_kernel(ya_ref, yb_ref, yc_ref, yd_ref, gt_ref, xc_ref, xl_ref, m_ref,
                  w_ref, g_ref, b_ref, rw_ref, rb_ref,
                  x1_ref, h2_ref, cmb_ref, *, ctx_tiles, sub_rows):
    D = D_MODEL
    rw = rw_ref[...]
    rw_hi = rw.astype(BF16)
    rw_lo = (rw - rw_hi.astype(F32)).astype(BF16)
    g1 = m_ref[:, 2 * D:3 * D]
    s2 = m_ref[:, 3 * D:4 * D]
    sc2 = m_ref[:, 4 * D:5 * D]
    is_ctx = pl.program_id(0) < ctx_tiles
    offs = np.cumsum((0,) + MERGE_BRANCH_ROWS)
    branches = tuple((y_ref, slice(int(offs[i]), int(offs[i + 1])))
                     for i, y_ref in enumerate((ya_ref, yb_ref, yc_ref, yd_ref)))
    w_out_rows = slice(int(offs[-1]), MERGE_ROWS)
    for r0 in range(0, x1_ref.shape[0], sub_rows):
        rows = slice(r0, r0 + sub_rows)
        merged = None
        for i, (y_ref, w_rows) in enumerate(branches):
            t = gt_ref[rows, i * D:(i + 1) * D] * jnp.dot(
                y_ref[rows, :], w_ref[w_rows, :], preferred_element_type=F32).astype(BF16)
            merged = t if merged is None else merged + t
        out1 = jnp.dot(merged, w_ref[w_out_rows, :], preferred_element_type=F32)
        x = jnp.where(is_ctx, xc_ref[rows, :], xl_ref[rows, :])
        x1 = _layer_norm(ALPHA * x + g1 * out1) * g_ref[...] + b_ref[...]
        x1_ref[rows, :] = x1
        h2 = _layer_norm(x1) * (1.0 + sc2) + s2
        h2_hi = h2.astype(BF16)
        h2_ref[rows, :] = h2_hi
        h2_lo = (h2 - h2_hi.astype(F32)).astype(BF16)
        logits = (jnp.dot(h2_hi, rw_hi, preferred_element_type=F32)
                  + (jnp.dot(h2_lo, rw_hi, preferred_element_type=F32)
                     + jnp.dot(h2_hi, rw_lo, preferred_element_type=F32)))
        combine_t = _route(logits.T[0:N_EXPERTS], rb_ref[...])
        cmb_ref[rows, :] = jnp.concatenate(
            [combine_t, jnp.zeros((LANE - N_EXPERTS, sub_rows), F32)], 0).T


def _merge(ya, yb, yc, yd, gates, x_ctx, x_lat, lat_block0, mods, w_merge, ln1_g, ln1_b,
           router_w, router_b, layer):
    tm = MERGE_TILE
    row = _mod_row(tm)
    D = D_MODEL
    ctx_tiles = T_CTX // tm

    def tile(w):
        return pl.BlockSpec((tm, w), lambda i: (i, 0))

    def weight(k, n):
        return pl.BlockSpec((None, k, n), lambda i: (layer, 0, 0))

    return pl.pallas_call(
        functools.partial(_merge_kernel, ctx_tiles=ctx_tiles, sub_rows=256),
        out_shape=(jax.ShapeDtypeStruct((T_ALL, D), F32),
                   jax.ShapeDtypeStruct((T_ALL, D), BF16),
                   jax.ShapeDtypeStruct((T_ALL, LANE), F32)),
        grid=(T_ALL // tm,),
        in_specs=[
            tile(256), tile(256), tile(512), tile(256), tile(4 * D),
            *_x_specs(tm, ctx_tiles, lat_block0),
            _mod_spec(layer, row),
            weight(MERGE_ROWS, D), weight(1, D), weight(1, D),
            pl.BlockSpec((D, LANE), lambda i: (0, 0)),
            pl.BlockSpec((N_EXPERTS, 1), lambda i: (0, 0)),
        ],
        out_specs=(tile(D), tile(D), tile(LANE)),
        compiler_params=_params("parallel"),
        name="merge",
    )(ya, yb, yc, yd, gates, x_ctx, x_lat, mods, w_merge,
      ln1_g.reshape(DEPTH, 1, D), ln1_b.reshape(DEPTH, 1, D), router_w,
      router_b.reshape(N_EXPERTS, 1))


MOE_EXPERTS_PER_STEP = 2


def _moe_kernel(*refs, next_h):
    if next_h:
        (h_ref, c_ref, x1_ref, m_ref, wg_ref, wu_ref, wd_ref, g_ref, b_ref, mn_ref,
         o_ref, hn_ref, acc_ref) = refs
    else:
        h_ref, c_ref, x1_ref, m_ref, wg_ref, wu_ref, wd_ref, g_ref, b_ref, o_ref, acc_ref = refs
    eg = pl.program_id(1)

    @pl.when(eg == 0)
    def _():
        acc_ref[...] = jnp.zeros_like(acc_ref)

    h = h_ref[...]
    cmb = c_ref[...]
    lane = lax.broadcasted_iota(jnp.int32, cmb.shape, 1)
    hid = []
    for k in range(MOE_EXPERTS_PER_STEP):
        gate = jnp.dot(h, wg_ref[k].astype(BF16), preferred_element_type=F32)
        up = jnp.dot(h, wu_ref[k].astype(BF16), preferred_element_type=F32)
        e = eg * MOE_EXPERTS_PER_STEP + k
        ce = jnp.sum(jnp.where(lane == e, cmb, 0.0), -1, keepdims=True)
        g16 = gate.astype(BF16)
        sig = 0.5 * jnp.tanh(0.5 * g16) + 0.5
        hid.append(g16 * sig * (up * ce).astype(BF16))
    wd = wd_ref[...].reshape(MOE_EXPERTS_PER_STEP * D_EXPERT, D_MODEL).astype(BF16)
    acc_ref[...] += jnp.dot(jnp.concatenate(hid, 1), wd, preferred_element_type=F32)

    @pl.when(eg == N_EXPERTS // MOE_EXPERTS_PER_STEP - 1)
    def _():
        g2 = m_ref[:, 5 * D_MODEL:6 * D_MODEL]
        y = _layer_norm(ALPHA * x1_ref[...] + g2 * acc_ref[...])
        y = y * g_ref[...] + b_ref[...]
        o_ref[...] = y
        if next_h:
            s1 = mn_ref[:, 0:D_MODEL]
            sc1 = mn_ref[:, D_MODEL:2 * D_MODEL]
            hn_ref[...] = (_layer_norm(y) * (1.0 + sc1) + s1).astype(hn_ref.dtype)


def _moe(h2, combine, x1, mods, w_gate, w_up, w_down, ln2_g, ln2_b, layer, row0, n_rows,
         next_h=False):
    tm = 1024
    row = _mod_row(tm)
    D = D_MODEL
    t0 = row0 // tm
    eps = MOE_EXPERTS_PER_STEP
    mod_spec = _mod_spec(layer, lambda i, e: row(t0 + i))
    in_specs = [
        pl.BlockSpec((tm, D), lambda i, e: (t0 + i, 0)),
        pl.BlockSpec((tm, LANE), lambda i, e: (t0 + i, 0)),
        pl.BlockSpec((tm, D), lambda i, e: (t0 + i, 0)),
        mod_spec,
        pl.BlockSpec((None, eps, D, D_EXPERT), lambda i, e: (layer, e, 0, 0)),
        pl.BlockSpec((None, eps, D, D_EXPERT), lambda i, e: (layer, e, 0, 0)),
        pl.BlockSpec((None, eps, D_EXPERT, D), lambda i, e: (layer, e, 0, 0)),
        pl.BlockSpec((None, 1, D), lambda i, e: (layer, 0, 0)),
        pl.BlockSpec((None, 1, D), lambda i, e: (layer, 0, 0)),
    ]
    args = [h2, combine, x1, mods, w_gate, w_up, w_down,
            ln2_g.reshape(DEPTH, 1, D), ln2_b.reshape(DEPTH, 1, D)]
    out_shape = jax.ShapeDtypeStruct((n_rows, D), F32)
    out_specs = pl.BlockSpec((tm, D), lambda i, e: (i, 0))
    if next_h:
        in_specs.append(_mod_spec(layer + 1, lambda i, e: row(t0 + i)))
        args.append(mods)
        out_shape = (out_shape, jax.ShapeDtypeStruct((T_ALL, D), BF16))
        assert n_rows == T_ALL
        out_specs = (out_specs, pl.BlockSpec((tm, D), lambda i, e: (i, 0)))
    return pl.pallas_call(
        functools.partial(_moe_kernel, next_h=next_h),
        out_shape=out_shape,
        grid=(n_rows // tm, N_EXPERTS // eps),
        in_specs=in_specs,
        out_specs=out_specs,
        scratch_shapes=[pltpu.VMEM((tm, D), F32)],
        compiler_params=_params("parallel", "arbitrary"),
        name="moe",
    )(*args)


def kernel(x_prompt, x_sample, cache_win_k, cache_win_v, cache_mla_ckv, cache_mla_krope,
           state_ret_fwd, state_ret_bwd, c, c_ctx, w_ada, b_ada, w_in,
           hy_conv_w, hy_conv_b, hy_w1, hy_b1, hy_w2, hy_b2, hy_w3, hy_bias,
           win_sink, ret_decay_fwd, ret_decay_bwd, mla_q_norm, mla_kv_norm, mla_w_uq, mla_w_ukv,
           w_br_a, w_br_b, w_br_c, w_br_d, w_out, ln1_g, ln1_b, ln2_g, ln2_b,
           router_w, router_b, moe_w_gate, moe_w_up, moe_w_down):
    D = D_MODEL
    x_ctx = x_prompt.reshape(T_CTX, D)
    x_lat = x_sample.reshape(T_LAT, D)
    lat_block0 = 0

    mods = _ada_mods(c_ctx, c, w_ada, b_ada)

    w_in_t = jnp.swapaxes(w_in, 1, 2).reshape(DEPTH * IN_COLS, D)
    cache_k = cache_win_k.reshape(DEC_BATCH, DEPTH, PAST_LEN, WIN_KV_HEADS * WIN_HD)
    cache_v = cache_win_v.reshape(DEC_BATCH, DEPTH, PAST_LEN, WIN_KV_HEADS * WIN_HD)
    cache_kr = jnp.pad(cache_mla_krope, ((0, 0), (0, 0), (0, 0), (0, LANE - MLA_ROPE)))

    uq = mla_w_uq.reshape(DEPTH, MLA_Q_LORA, MLA_HEADS, MLA_NOPE + MLA_ROPE)
    ukv = mla_w_ukv.reshape(DEPTH, MLA_KV_LORA, MLA_HEADS, MLA_NOPE + MLA_V)
    mla_weights = (
        mla_q_norm.reshape(DEPTH, 1, MLA_Q_LORA),
        mla_kv_norm.reshape(DEPTH, 1, MLA_KV_LORA),
        jnp.concatenate([uq[..., :MLA_NOPE].reshape(DEPTH, MLA_Q_LORA, MLA_HEADS * MLA_NOPE),
                         uq[..., MLA_NOPE:].reshape(DEPTH, MLA_Q_LORA, MLA_HEADS * MLA_ROPE)], -1),
        jnp.concatenate([ukv[..., :MLA_NOPE].reshape(DEPTH, MLA_KV_LORA, MLA_HEADS * MLA_NOPE),
                         ukv[..., MLA_NOPE:].reshape(DEPTH, MLA_KV_LORA, MLA_HEADS * MLA_V)], -1),
    )

    hy_w1p = jnp.pad(hy_w1, ((0, 0), (0, LANE - HY_EMB), (0, 0)))
    dft = {}
    for L in (SEQ, DEC_SEQ):
        fwd, inv = _dft_tables(L)
        dft[L] = (jnp.asarray(fwd).astype(BF16), jnp.asarray(inv).astype(BF16))
    router_w_pad = jnp.pad(router_w, ((0, 0), (0, LANE - N_EXPERTS)))
    w_merge = jnp.concatenate([w_br_a, w_br_b, w_br_c, w_br_d, w_out], 1).astype(BF16)

    new_k, new_v, new_ckv, new_kr, new_sf, new_sb = [], [], [], [], [], []
    for l in range(DEPTH):
        if l == 0:
            h = _ln_mod(x_ctx, x_lat, mods, l)
        z = _in_proj(h, w_in_t, l, 0, Z_MAIN, Z_MAIN // 2, BF16, gate=False)
        gates = _in_proj(h, w_in_t, l, COL_GATE, 4 * D, D, BF16, gate=True)

        filters = {L: _hy_filters(L, hy_w1p[l], hy_b1[l][None], hy_w2[l], hy_b2[l][None], hy_w3[l],
                                  dft[L][0]) for L in (SEQ, DEC_SEQ)}
        ya = _hyena(z, l, hy_conv_w, hy_conv_b, hy_bias, filters, dft)
        yb = _win(z, win_sink, cache_k, cache_v, l)
        yc, sf, sb = _retention(z, ret_decay_fwd, ret_decay_bwd, state_ret_fwd, state_ret_bwd, l)
        yd, ckvn = _mla(z, cache_mla_ckv, cache_kr, mla_weights, l)

        x1, h2, combine = _merge(ya, yb, yc, yd, gates, x_ctx, x_lat, lat_block0, mods, w_merge,
                                   ln1_g, ln1_b, router_w_pad, router_b, l)
        moe_args = (h2, combine, x1, mods, moe_w_gate, moe_w_up, moe_w_down, ln2_g, ln2_b, l)
        if l + 1 < DEPTH:
            x_ctx, h = _moe(*moe_args, 0, T_ALL, next_h=True)
            x_lat, lat_block0 = x_ctx, T_CTX // MERGE_TILE
        else:
            x_ctx = _moe(*moe_args, 0, T_CTX)
            x_lat = _moe(*moe_args, T_CTX, T_LAT)

        def ctx_cols(col, width):
            return z[:T_CTX, col:col + width].astype(F32)

        new_k.append(ctx_cols(COL_WK, 128).reshape(BATCH, SEQ, WIN_KV_HEADS, WIN_HD))
        new_v.append(ctx_cols(COL_WV, 128).reshape(BATCH, SEQ, WIN_KV_HEADS, WIN_HD))
        new_ckv.append(ckvn[:T_CTX].reshape(BATCH, SEQ, MLA_KV_LORA))
        new_kr.append(ctx_cols(COL_KROPE, MLA_ROPE).reshape(BATCH, SEQ, MLA_ROPE))
        new_sf.append(sf[:BATCH])
        new_sb.append(sb[:BATCH])

    y_prompt = x_ctx.reshape(BATCH, SEQ, D)
    y_sample = x_lat.reshape(DEC_BATCH, DEC_SEQ, D)
    return (y_prompt, y_sample, jnp.stack(new_k, 1), jnp.stack(new_v, 1), jnp.stack(new_ckv, 1),
            jnp.stack(new_kr, 1), jnp.stack(new_sf, 1), jnp.stack(new_sb, 1))
```

```python
import functools
import math

import numpy as np
import jax
import jax.numpy as jnp
from jax import lax
from jax.experimental import pallas as pl
from jax.experimental.pallas import tpu as pltpu

F32 = jnp.float32
BF16 = jnp.bfloat16

D_MODEL = 1024
BATCH = 16
SEQ = 256
DEPTH = 2
DEC_BATCH = 2
DEC_SEQ = 1024
PAST_LEN = 256
GRID_W = 64
CHUNK = 128
ROPE_BASE = 10000.0
NEG = -1e30
LN_EPS = 1e-5
RMS_EPS = 1e-6
LOG2E = math.log2(math.e)

HY_W = 256
HY_BANDS = 16
HY_EMB = 1 + 2 * HY_BANDS
HY_FFN = 64
HY_FAST_DECAY = 0.3
HY_SLOW_DECAY = 1.5
HY_TARGET = 1e-2

WIN_HEADS = 4
WIN_KV_HEADS = 2
WIN_HD = 64
WINDOW = 128

RET_HEADS = 4
RET_DK = 64
RET_DV = 128

MLA_HEADS = 4
MLA_Q_LORA = 256
MLA_KV_LORA = 128
MLA_NOPE = 64
MLA_ROPE = 32
MLA_V = 64

N_EXPERTS = 16
N_GROUPS = 4
EXPERTS_PER_GROUP = N_EXPERTS // N_GROUPS
D_EXPERT = 256
ROUTE_SCALE = 2.5

ALPHA = (2.0 * DEPTH) ** 0.25

T_CTX = BATCH * SEQ
T_LAT = DEC_BATCH * DEC_SEQ
T_ALL = T_CTX + T_LAT

COL_HY = 0
COL_WQ = 768
COL_WK = 1024
COL_WV = 1152
COL_RQ = 1280
COL_RK = 1536
COL_RV = 1792
COL_RG = 2304
COL_CQ = 2816
COL_CKV = 3072
COL_KROPE = 3200
COL_GATE = 3232
IN_COLS = COL_GATE + 4 * D_MODEL
Z_MAIN = 3328

LANE = 128
STEP_ROWS = 1024
CTX_SEQS_PER_STEP = STEP_ROWS // SEQ
CTX_STEPS = T_CTX // STEP_ROWS
MIXER_STEPS = T_ALL // STEP_ROWS
VMEM_LIMIT = 56 * 1024 * 1024


def _params(*sem):
    return pltpu.CompilerParams(dimension_semantics=sem, vmem_limit_bytes=VMEM_LIMIT)


def _dot(a, b):
    return jnp.dot(a.astype(BF16), b.astype(BF16), preferred_element_type=F32)


def _dot_split(a, b):
    a_hi = a.astype(BF16)
    a_lo = (a - a_hi.astype(F32)).astype(BF16)
    b_hi = b.astype(BF16)
    b_lo = (b - b_hi.astype(F32)).astype(BF16)

    def mm(x, y):
        return jnp.dot(x, y, preferred_element_type=F32)

    return mm(a_hi, b_hi) + (mm(a_lo, b_hi) + mm(a_hi, b_lo))


def _dot_nt(a, b):
    return lax.dot_general(a.astype(BF16), b.astype(BF16), (((1,), (1,)), ((), ())),
                           preferred_element_type=F32)


def _dot_tn(a, b):
    return lax.dot_general(a.astype(BF16), b.astype(BF16), (((0,), (0,)), ((), ())),
                           preferred_element_type=F32)


def _layer_norm(x):
    mu = jnp.mean(x, -1, keepdims=True)
    xc = x - mu
    var = jnp.mean(xc * xc, -1, keepdims=True)
    return xc * lax.rsqrt(var + LN_EPS)


def _mod_row(tile_rows):
    def row(i):
        start = i * tile_rows
        return jnp.where(start < T_CTX, 0, 1 + (start - T_CTX) // DEC_SEQ)
    return row


@functools.lru_cache(maxsize=None)
def _dft_tables(L):
    f = np.arange(L, dtype=np.int64)[:, None]
    s = np.arange(L, dtype=np.int64)[None, :]
    ang = np.pi * ((f * s) % (2 * L)).astype(np.float64) / L
    cos = np.cos(ang)
    sin = np.sin(ang)
    alt = np.where(np.arange(L) % 2 == 0, 1.0, -1.0)
    fwd_im = -sin
    fwd_im[0, :] = alt
    fwd = np.concatenate([cos, fwd_im], 0)
    inv_re = cos.T / L
    inv_re[:, 0] = 1.0 / (2 * L)
    inv_im = -sin.T / L
    inv_im[:, 0] = alt / (2 * L)
    inv = np.concatenate([inv_re, inv_im], 1)
    return fwd.astype(np.float32), inv.astype(np.float32)


@functools.lru_cache(maxsize=None)
def _hyena_embedding(L):
    t01 = np.linspace(0.0, 1.0, L, dtype=np.float64)[:, None]
    bands = np.linspace(1e-4, HY_BANDS - 1, HY_BANDS, dtype=np.float64)
    ang = (2.0 * math.pi / L) * np.arange(L, dtype=np.float64)[:, None] * bands[None, :]
    z = np.concatenate([t01, np.cos(ang), -np.sin(ang)], -1)
    zp = np.zeros((L, LANE), np.float64)
    zp[:, :HY_EMB] = z
    deltas = np.abs(np.linspace(math.log(HY_TARGET) / HY_SLOW_DECAY,
                                math.log(HY_TARGET) / HY_FAST_DECAY, HY_W, dtype=np.float64))
    return zp.astype(np.float32), deltas[None, :].astype(np.float32)


@functools.lru_cache(maxsize=None)
def _rope_tables(L, rot_dim, width):
    rows = L // GRID_W
    n_freq = rot_dim // 4
    half = rot_dim // 2
    inv = ROPE_BASE ** (-np.arange(n_freq, dtype=np.float64) / n_freq)
    pos = np.arange(L)
    row = (pos // GRID_W).astype(np.float64)
    col = (pos % GRID_W).astype(np.float64)
    ang = np.concatenate([row[:, None] * inv, col[:, None] * inv], -1)
    cos, sin = np.cos(ang), np.sin(ang)
    zero = np.zeros_like(sin)
    c = np.tile(np.concatenate([cos, cos], -1), (1, width // rot_dim))
    s_up = np.tile(np.concatenate([-sin, zero], -1), (1, width // rot_dim))
    s_dn = np.tile(np.concatenate([zero, sin], -1), (1, width // rot_dim))
    return c.astype(np.float32), s_up.astype(np.float32), s_dn.astype(np.float32), half


def _rope128(x, c, s_up, s_dn, half):
    up = pltpu.roll(x, LANE - half, axis=1)
    dn = pltpu.roll(x, half, axis=1)
    return x * c + up * s_up + dn * s_dn


MOD_ROWS = 1 + DEC_BATCH


def _ada_kernel(ct_ref, w_ref, b_ref, o_ref):
    @pl.when(pl.program_id(1) == 0)
    def _():
        for r in range(MOD_ROWS):
            o_ref[r] = b_ref[...]

    ct = ct_ref[...]
    s = ct * jax.nn.sigmoid(ct)
    tk, n = w_ref.shape
    rows, cols = 64, 4 * LANE
    s_cols = [[s[k0:k0 + rows, r:r + 1] for k0 in range(0, tk, rows)] for r in range(MOD_ROWS)]
    for c0 in range(0, n, cols):
        acc = [None] * MOD_ROWS
        for ki, k0 in enumerate(range(0, tk, rows)):
            wc = w_ref[k0:k0 + rows, c0:c0 + cols]
            for r in range(MOD_ROWS):
                part = jnp.sum((wc * s_cols[r][ki]).reshape(rows // 8, 8, cols), axis=0)
                acc[r] = part if acc[r] is None else acc[r] + part
        for r in range(MOD_ROWS):
            o_ref[r, :, c0:c0 + cols] += jnp.sum(acc[r], axis=0, keepdims=True)


def _ada_mods(c_ctx, c, w_ada, b_ada):
    tk = 256
    n = 6 * D_MODEL
    c_cols = jnp.concatenate([c_ctx[:, None], c.T], 1)
    return pl.pallas_call(
        _ada_kernel,
        out_shape=jax.ShapeDtypeStruct((DEPTH, MOD_ROWS, 1, n), F32),
        grid=(DEPTH, D_MODEL // tk),
        in_specs=[
            pl.BlockSpec((tk, MOD_ROWS), lambda l, k: (k, 0)),
            pl.BlockSpec((None, tk, n), lambda l, k: (l, k, 0)),
            pl.BlockSpec((None, 1, n), lambda l, k: (l, 0, 0)),
        ],
        out_specs=pl.BlockSpec((None, MOD_ROWS, 1, n), lambda l, k: (l, 0, 0, 0)),
        compiler_params=_params("parallel", "arbitrary"),
        name="ada_mods",
    )(c_cols, w_ada, b_ada.reshape(DEPTH, 1, n))


def _mod_spec(layer, row_of_step):
    return pl.BlockSpec((None, None, 1, 6 * D_MODEL), lambda *g: (layer, row_of_step(*g), 0, 0))


def _lnmod_kernel(xc_ref, xl_ref, m_ref, h_ref, *, ctx_tiles):
    x = jnp.where(pl.program_id(0) < ctx_tiles, xc_ref[...], xl_ref[...])
    y = _layer_norm(x)
    s1 = m_ref[:, 0:D_MODEL]
    sc1 = m_ref[:, D_MODEL:2 * D_MODEL]
    h_ref[...] = (y * (1.0 + sc1) + s1).astype(h_ref.dtype)


def _x_specs(tm, ctx_tiles, lat_block0=0):
    return [pl.BlockSpec((tm, D_MODEL), lambda i: (jnp.minimum(i, ctx_tiles - 1), 0)),
            pl.BlockSpec((tm, D_MODEL), lambda i: (jnp.maximum(i - ctx_tiles, 0) + lat_block0, 0))]


def _ln_mod(x_ctx, x_lat, mods, layer):
    tm = 512
    ctx_tiles = T_CTX // tm
    return pl.pallas_call(
        functools.partial(_lnmod_kernel, ctx_tiles=ctx_tiles),
        out_shape=jax.ShapeDtypeStruct((T_ALL, D_MODEL), BF16),
        grid=(T_ALL // tm,),
        in_specs=_x_specs(tm, ctx_tiles) + [_mod_spec(layer, _mod_row(tm))],
        out_specs=pl.BlockSpec((tm, D_MODEL), lambda i: (i, 0)),
        compiler_params=_params("parallel"),
        name="ln_mod",
    )(x_ctx, x_lat, mods)


def _proj_kernel(h_ref, w_ref, o_ref, wb_ref, *, gate):
    @pl.when(pl.program_id(1) == 0)
    def _():
        wb_ref[...] = w_ref[...].astype(BF16)

    if not gate:
        o_ref[...] = _dot_nt(h_ref[...], wb_ref[...]).astype(o_ref.dtype)
        return
    sub = 2 * LANE
    for c0 in range(0, o_ref.shape[1], sub):
        r = _dot_nt(h_ref[...], wb_ref[c0:c0 + sub, :])
        rb = r.astype(o_ref.dtype)
        o_ref[:, c0:c0 + sub] = 0.5 * jnp.tanh(0.5 * rb) + 0.5


def _in_proj(h, w_t, layer, col0, n_cols, tn, out_dtype, gate):
    tm = 2048
    return pl.pallas_call(
        functools.partial(_proj_kernel, gate=gate),
        out_shape=jax.ShapeDtypeStruct((T_ALL, n_cols), out_dtype),
        grid=(n_cols // tn, T_ALL // tm),
        in_specs=[
            pl.BlockSpec((tm, D_MODEL), lambda j, i: (i, 0)),
            pl.BlockSpec((pl.Element(tn), pl.Element(D_MODEL)),
                         lambda j, i: (pl.multiple_of(layer * IN_COLS + col0 + j * tn, 8), 0)),
        ],
        out_specs=pl.BlockSpec((tm, tn), lambda j, i: (i, j)),
        scratch_shapes=[pltpu.VMEM((tn, D_MODEL), BF16)],
        compiler_params=_params("parallel", "arbitrary"),
        name="gate_proj" if gate else "in_proj",
    )(h, w_t)


def _hy_filter_kernel(z_ref, dl_ref, w1_ref, b1_ref, w2_ref, b2_ref, w3_ref, fwd_ref,
                      kre_ref, kim_ref, *, L):
    z = z_ref[...]
    a = jnp.sin(_dot_split(z, w1_ref[...]) + b1_ref[...])
    a = jnp.sin(_dot_split(a, w2_ref[...]) + b2_ref[...])
    h = _dot_split(a, w3_ref[...])
    decay = jnp.exp(-z[:, 0:1] * dl_ref[...])
    not_first = lax.broadcasted_iota(jnp.int32, (L, HY_W), 0) > 0
    sums, diffs = [], []
    for o in range(2):
        fw = h[:, (2 * o) * HY_W:(2 * o + 1) * HY_W] * decay
        bw = jnp.where(not_first, h[:, (2 * o + 1) * HY_W:(2 * o + 2) * HY_W] * decay, 0.0)
        sums.append(fw + bw)
        diffs.append(fw - bw)
    p = _dot(fwd_ref[...], jnp.concatenate(sums, 1))
    q = _dot(fwd_ref[L:2 * L, :], jnp.concatenate(diffs, 1))
    kre_ref[...] = p[0:L]
    first = lax.broadcasted_iota(jnp.int32, (L, 2 * HY_W), 0) == 0
    kim_ref[...] = jnp.where(first, p[L:L + 1], q)


def _hy_filters(L, w1p, b1, w2, b2, w3, fwd):
    zemb, deltas = _hyena_embedding(L)
    out = jax.ShapeDtypeStruct((L, 2 * HY_W), F32)
    return pl.pallas_call(
        functools.partial(_hy_filter_kernel, L=L),
        out_shape=(out, out),
        compiler_params=pltpu.CompilerParams(vmem_limit_bytes=VMEM_LIMIT),
        name=f"hy_filters_{L}",
    )(jnp.asarray(zemb), jnp.asarray(deltas), w1p, b1, w2, b2, w3, fwd)


def _group_step(ctx_body, lat_body, ctx_steps=CTX_STEPS):
    i = pl.program_id(0)
    pl.when(i < ctx_steps)(ctx_body)
    pl.when(i >= ctx_steps)(lat_body)


def _lat_index(i):
    return jnp.maximum(i - CTX_STEPS, 0)


HYENA_STEP_ROWS = T_LAT


def _hyena_kernel(hy_ref, cw_ref, cb_ref, bias_ref, kre_c, kim_c, fwd_c, inv_c,
                  kre_l, kim_l, fwd_l, inv_l, o_ref):
    _group_step(
        lambda: _hyena_body(hy_ref, cw_ref, cb_ref, bias_ref, kre_c, kim_c, fwd_c, inv_c, o_ref,
                            SEQ, HYENA_STEP_ROWS // SEQ),
        lambda: _hyena_body(hy_ref, cw_ref, cb_ref, bias_ref, kre_l, kim_l, fwd_l, inv_l, o_ref,
                            DEC_SEQ, HYENA_STEP_ROWS // DEC_SEQ),
        ctx_steps=T_CTX // HYENA_STEP_ROWS)


def _hyena_body(hy_ref, cw_ref, cb_ref, bias_ref, kre_ref, kim_ref, fwd_ref, inv_ref, o_ref, L, seqs):
    width = seqs * HY_W
    first = lax.broadcasted_iota(jnp.int32, (L, width), 0) == 0

    def tiled(t):
        return jnp.concatenate([t] * seqs, 1)

    def long_conv(u, o):
        uf = _dot(fwd_ref[...], u)
        ure, uim = uf[0:L], uf[L:2 * L]
        kre = tiled(kre_ref[:, o * HY_W:(o + 1) * HY_W])
        kim = tiled(kim_ref[:, o * HY_W:(o + 1) * HY_W])
        yre = jnp.where(first, ure * kre, ure * kre - uim * kim)
        yim = jnp.where(first, uim * kim, ure * kim + uim * kre)
        y = _dot(inv_ref[...], jnp.concatenate([yre, yim], 0))
        return y + u * tiled(bias_ref[o:o + 1, :])

    vs, x1s, x2s = [], [], []
    for g in range(seqs):
        x = hy_ref[g * L:(g + 1) * L, :].astype(F32)
        rows = lax.broadcasted_iota(jnp.int32, x.shape, 0)
        prev = jnp.where(rows == 0, 0.0, pltpu.roll(x, 1, axis=0))
        nxt = jnp.where(rows == L - 1, 0.0, pltpu.roll(x, L - 1, axis=0))
        z = prev * cw_ref[0:1, :] + x * cw_ref[1:2, :] + nxt * cw_ref[2:3, :] + cb_ref[...]
        vs.append(z[:, 0:HY_W])
        x1s.append(z[:, HY_W:2 * HY_W])
        x2s.append(z[:, 2 * HY_W:3 * HY_W])
    u = jnp.concatenate(x1s, 1) * long_conv(jnp.concatenate(vs, 1), 0)
    y = jnp.concatenate(x2s, 1) * long_conv(u, 1)
    for g in range(seqs):
        o_ref[g * L:(g + 1) * L, :] = y[:, g * HY_W:(g + 1) * HY_W].astype(o_ref.dtype)


def _const_spec(shape):
    return pl.BlockSpec(shape, lambda i: (0,) * len(shape))


def _hyena(z_main, layer, conv_w, conv_b, bias, filters, dft):
    tables, table_specs = [], []
    for L in (SEQ, DEC_SEQ):
        tables += [*filters[L], *dft[L]]
        table_specs += [_const_spec((L, 2 * HY_W)), _const_spec((L, 2 * HY_W)),
                        _const_spec((2 * L, L)), _const_spec((L, 2 * L))]
    return pl.pallas_call(
        _hyena_kernel,
        out_shape=jax.ShapeDtypeStruct((T_ALL, HY_W), BF16),
        grid=(T_ALL // HYENA_STEP_ROWS,),
        in_specs=[
            pl.BlockSpec((HYENA_STEP_ROWS, 3 * HY_W), lambda i: (i, 0)),
            pl.BlockSpec((None, 3, 3 * HY_W), lambda i: (layer, 0, 0)),
            pl.BlockSpec((None, 1, 3 * HY_W), lambda i: (layer, 0, 0)),
            pl.BlockSpec((None, 2, HY_W), lambda i: (layer, 0, 0)),
        ] + table_specs,
        out_specs=pl.BlockSpec((HYENA_STEP_ROWS, HY_W), lambda i: (i, 0)),
        compiler_params=_params("parallel"),
        name="hyena",
    )(z_main, conv_w, conv_b.reshape(DEPTH, 1, 3 * HY_W), bias, *tables)


def _win_masks():
    lane = lax.broadcasted_iota(jnp.int32, (1, LANE), 1)
    return lane < WIN_HD, lane >= WIN_HD


def _win_head_operands(q, k, v, h):
    lo_mask, hi_mask = _win_masks()
    col = h // 2
    lo = h % 2 == 0
    q128 = jnp.where(lo_mask if lo else hi_mask, q[:, col * LANE:(col + 1) * LANE], 0.0)
    swap = h in (1, 2)
    if swap:
        k = pltpu.roll(k, WIN_HD, axis=1)
        v = pltpu.roll(v, WIN_HD, axis=1)
    return q128, k, v, lo


def _win_kernel(sink_ref, q_ref, kv_ref, ck_ref, cv_ref, c_ref, su_ref, sd_ref, o_ref, *, layer):
    _group_step(
        lambda: _win_ctx_body(sink_ref, q_ref, kv_ref, o_ref, layer),
        lambda: _win_lat_body(sink_ref, q_ref, kv_ref, ck_ref, cv_ref, c_ref, su_ref, sd_ref, o_ref, layer))


def _win_ctx_body(sink_ref, q_ref, kv_ref, o_ref, layer):
    lo_mask, hi_mask = _win_masks()
    qscale = WIN_HD ** -0.5 * LOG2E
    for g in range(CTX_SEQS_PER_STEP):
        sl = slice(g * SEQ, (g + 1) * SEQ)
        q = q_ref[sl, :].astype(F32) * qscale
        k = kv_ref[sl, 0:LANE].astype(F32)
        v = kv_ref[sl, LANE:2 * LANE].astype(F32)
        cols = []
        for col in range(2):
            acc = None
            for h in (2 * col, 2 * col + 1):
                q128, kk, vv, lo = _win_head_operands(q, k, v, h)
                s = _dot_nt(q128, kk)
                sink = sink_ref[layer, h] * LOG2E
                m = jnp.maximum(jnp.max(s, -1, keepdims=True), sink)
                p = jnp.exp2(s - m)
                den = jnp.sum(p, -1, keepdims=True) + jnp.exp2(sink - m)
                o = _dot(p, vv) / den
                o = jnp.where(lo_mask if lo else hi_mask, o, 0.0)
                acc = o if acc is None else acc + o
            cols.append(acc)
        o_ref[sl, :] = jnp.concatenate(cols, 1).astype(o_ref.dtype)


def _win_lat_body(sink_ref, q_ref, kv_ref, ck_ref, cv_ref, c_ref, su_ref, sd_ref, o_ref, layer):
    L = DEC_SEQ
    half = WIN_HD // 2
    c, su, sd = c_ref[...], su_ref[...], sd_ref[...]
    qscale = WIN_HD ** -0.5 * LOG2E
    q = jnp.concatenate(
        [_rope128(q_ref[:, i * LANE:(i + 1) * LANE].astype(F32), c, su, sd, half) for i in range(2)],
        1) * qscale
    k = _rope128(kv_ref[:, 0:LANE].astype(F32), c, su, sd, half)
    v = kv_ref[:, LANE:2 * LANE].astype(F32)
    ck = ck_ref[...]
    cv = cv_ref[...]
    lo_mask, hi_mask = _win_masks()
    nb = L // CHUNK
    assert WINDOW == CHUNK
    rr = lax.broadcasted_iota(jnp.int32, (CHUNK, CHUNK), 0)
    cc = lax.broadcasted_iota(jnp.int32, (CHUNK, CHUNK), 1)
    band = {-1: jnp.where(cc >= rr, 0.0, NEG), 0: jnp.zeros((CHUNK, CHUNK), F32),
            1: jnp.where(cc <= rr, 0.0, NEG)}
    cols = []
    for col in range(2):
        acc_blocks = [None] * nb
        for h in (2 * col, 2 * col + 1):
            q128, kk, vv, lo = _win_head_operands(q, k, v, h)
            _, ckk, cvv, _ = _win_head_operands(q, ck, cv, h)
            sink = sink_ref[layer, h] * LOG2E
            for n in range(nb):
                blocks = [d for d in (-1, 0, 1) if 0 <= n + d < nb]
                k0 = (n + blocks[0]) * CHUNK
                k1 = (n + blocks[-1] + 1) * CHUNK
                qn = q128[n * CHUNK:(n + 1) * CHUNK]
                s_loc = _dot_nt(qn, kk[k0:k1]) + jnp.concatenate([band[d] for d in blocks], 1)
                s_ctx = _dot_nt(qn, ckk)
                m = jnp.maximum(jnp.maximum(jnp.max(s_loc, -1, keepdims=True),
                                            jnp.max(s_ctx, -1, keepdims=True)), sink)
                p_loc = jnp.exp2(s_loc - m)
                p_ctx = jnp.exp2(s_ctx - m)
                den = (jnp.sum(p_loc, -1, keepdims=True) + jnp.sum(p_ctx, -1, keepdims=True)
                       + jnp.exp2(sink - m))
                o = (_dot(p_loc, vv[k0:k1]) + _dot(p_ctx, cvv)) / den
                o = jnp.where(lo_mask if lo else hi_mask, o, 0.0)
                acc_blocks[n] = o if acc_blocks[n] is None else acc_blocks[n] + o
        cols.append(jnp.concatenate(acc_blocks, 0))
    o_ref[...] = jnp.concatenate(cols, 1).astype(o_ref.dtype)


def _win(z_main, sink, cache_k, cache_v, layer):
    c, su, sd, _ = _rope_tables(DEC_SEQ, WIN_HD, LANE)
    tab = _const_spec((DEC_SEQ, LANE))
    cache = pl.BlockSpec((None, None, PAST_LEN, LANE), lambda i: (_lat_index(i), layer, 0, 0))
    return pl.pallas_call(
        functools.partial(_win_kernel, layer=layer),
        out_shape=jax.ShapeDtypeStruct((T_ALL, WIN_HEADS * WIN_HD), BF16),
        grid=(MIXER_STEPS,),
        in_specs=[
            pl.BlockSpec(memory_space=pltpu.SMEM),
            pl.BlockSpec((STEP_ROWS, 256), lambda i: (i, COL_WQ // 256)),
            pl.BlockSpec((STEP_ROWS, 256), lambda i: (i, COL_WK // 256)),
            cache, cache, tab, tab, tab,
        ],
        out_specs=pl.BlockSpec((STEP_ROWS, 256), lambda i: (i, 0)),
        compiler_params=_params("parallel"),
        name="win",
    )(sink, z_main, z_main, cache_k, cache_v, jnp.asarray(c), jnp.asarray(su), jnp.asarray(sd))


def _ret_kernel(*refs, layer):
    _group_step(lambda: _ret_body(*refs, L=SEQ, layer=layer, ctx=True),
                lambda: _ret_body(*refs, L=DEC_SEQ, layer=layer, ctx=False))


def _ret_body(df_ref, db_ref, q_ref, k_ref, v0_ref, v1_ref, g0_ref, g1_ref, s0f_ref, s0b_ref,
              o_ref, sf_out, sb_out, s_ref, cross_ref, *, L, layer, ctx):
    seqs = STEP_ROWS // L
    if not ctx:
        sf_out[...] = jnp.zeros_like(sf_out)
        sb_out[...] = jnp.zeros_like(sb_out)
    C = CHUNK
    nc = L // C
    H = RET_HEADS
    qw = H * RET_DK
    vw = H * RET_DV

    def lane_table(width, per_head, fn):
        pos = lax.broadcasted_iota(jnp.int32, (C, per_head), 0).astype(F32)
        return jnp.concatenate([fn(h, pos) for h in range(H)], 1)

    def log_gamma(ref, h):
        d = jnp.full((1, 1), ref[layer, h], F32)
        return jnp.log(jax.nn.sigmoid(d))

    lgf = [log_gamma(df_ref, h) for h in range(H)]
    lgb = [log_gamma(db_ref, h) for h in range(H)]

    def tables(lg, reverse):
        if reverse:
            dq = lane_table(vw, RET_DV, lambda h, pos: jnp.exp((C - pos) * lg[h]))
            dk = lane_table(qw, RET_DK, lambda h, pos: jnp.exp(pos * lg[h]))
        else:
            dq = lane_table(vw, RET_DV, lambda h, pos: jnp.exp((pos + 1.0) * lg[h]))
            dk = lane_table(qw, RET_DK, lambda h, pos: jnp.exp((C - 1.0 - pos) * lg[h]))
        dc = jnp.concatenate([jnp.broadcast_to(jnp.exp(C * lg[h]), (1, RET_DV)) for h in range(H)], 1)
        return dq, dk, dc

    tab_f = tables(lgf, False)
    tab_b = tables(lgb, True)
    ii = lax.broadcasted_iota(jnp.int32, (C, C), 0)
    jj = lax.broadcasted_iota(jnp.int32, (C, C), 1)
    diff = (ii - jj).astype(F32)
    dmats = [jnp.where(diff >= 0, jnp.exp(jnp.maximum(diff, 0.0) * lgf[h]), 0.0)
             + jnp.where(diff <= 0, jnp.exp(jnp.maximum(-diff, 0.0) * lgb[h]), 0.0) for h in range(H)]
    dmat_stack = jnp.concatenate(dmats, 0)
    lane_q = lax.broadcasted_iota(jnp.int32, (1, qw), 1) // RET_DK

    srow = lax.broadcasted_iota(jnp.int32, (qw, vw), 0) // RET_DK
    scol = lax.broadcasted_iota(jnp.int32, (qw, vw), 1) // RET_DV
    diag = srow == scol

    for g in range(seqs):
        base = g * L
        rows_all = slice(base, base + L)
        q_all = q_ref[rows_all, :].astype(F32)
        k_all = k_ref[rows_all, :].astype(F32) * (RET_DK ** -0.5)
        v_all = jnp.concatenate([v0_ref[rows_all, :], v1_ref[rows_all, :]], 1).astype(F32)
        g_all = jnp.concatenate([g0_ref[rows_all, :], g1_ref[rows_all, :]], 1).astype(F32)

        def scan(tabs, reverse, s0_ref, s_out):
            dq, dk, dc = tabs
            if s0_ref is not None:
                s_ref[g] = jnp.zeros((qw, vw), F32)
                for h in range(H):
                    s_ref[g, h * RET_DK:(h + 1) * RET_DK, h * RET_DV:(h + 1) * RET_DV] = s0_ref[h]
            order = range(nc - 1, -1, -1) if reverse else range(nc)
            for step, ci in enumerate(order):
                sl = slice(ci * C, (ci + 1) * C)
                rs = slice(base + ci * C, base + (ci + 1) * C)
                qc, kc, vc = q_all[sl], k_all[sl], v_all[sl]
                upd = jnp.where(diag, _dot_tn(kc * dk, vc), 0.0)
                if s0_ref is None and step == 0:
                    if not reverse:
                        cross_ref[rs, :] = jnp.zeros((C, vw), F32)
                    s_ref[g] = upd
                    continue
                st = s_ref[g]
                cross = _dot(qc, st) * dq
                if reverse:
                    cross_ref[rs, :] = cross_ref[rs, :] + cross
                else:
                    cross_ref[rs, :] = cross
                s_ref[g] = st * dc + upd
            if s_out is not None:
                for h in range(H):
                    s_out[g, h] = s_ref[g, h * RET_DK:(h + 1) * RET_DK, h * RET_DV:(h + 1) * RET_DV]

        scan(tab_f, False, None if ctx else s0f_ref, sf_out if ctx else None)
        scan(tab_b, True, None if ctx else s0b_ref, sb_out if ctx else None)

        for ci in range(nc):
            sl = slice(ci * C, (ci + 1) * C)
            rs = slice(base + ci * C, base + (ci + 1) * C)
            q_stack = jnp.concatenate([jnp.where(lane_q == h, q_all[sl], 0.0) for h in range(H)], 0)
            att = _dot_nt(q_stack, k_all[sl]) * dmat_stack
            ov = _dot(att, v_all[sl])
            for h in range(H):
                hv = slice(h * RET_DV, (h + 1) * RET_DV)
                o = ov[h * C:(h + 1) * C, hv] + cross_ref[rs, hv]
                gt = g_all[sl, hv]
                o_ref[rs, hv] = ((gt * jax.nn.sigmoid(gt)) * _layer_norm(o)).astype(o_ref.dtype)


def _retention(z_main, dec_f, dec_b, s0f, s0b, layer):
    def zcol(col):
        return pl.BlockSpec((STEP_ROWS, 256), lambda i: (i, col // 256))

    smem = pl.BlockSpec(memory_space=pltpu.SMEM)
    z_specs = [zcol(COL_RQ), zcol(COL_RK), zcol(COL_RV), zcol(COL_RV + 256),
               zcol(COL_RG), zcol(COL_RG + 256)]
    s0_spec = pl.BlockSpec((None, None, RET_HEADS, RET_DK, RET_DV),
                           lambda i: (_lat_index(i), layer, 0, 0, 0))
    st_shape = jax.ShapeDtypeStruct((MIXER_STEPS * CTX_SEQS_PER_STEP, RET_HEADS, RET_DK, RET_DV), F32)
    st_spec = pl.BlockSpec((CTX_SEQS_PER_STEP, RET_HEADS, RET_DK, RET_DV), lambda i: (i, 0, 0, 0))
    return pl.pallas_call(
        functools.partial(_ret_kernel, layer=layer),
        out_shape=(jax.ShapeDtypeStruct((T_ALL, RET_HEADS * RET_DV), BF16), st_shape, st_shape),
        grid=(MIXER_STEPS,),
        in_specs=[smem, smem] + z_specs + [s0_spec, s0_spec],
        out_specs=(pl.BlockSpec((STEP_ROWS, RET_HEADS * RET_DV), lambda i: (i, 0)), st_spec, st_spec),
        scratch_shapes=[pltpu.VMEM((CTX_SEQS_PER_STEP, RET_HEADS * RET_DK, RET_HEADS * RET_DV), F32),
                        pltpu.VMEM((STEP_ROWS, RET_HEADS * RET_DV), F32)],
        compiler_params=_params("parallel"),
        name="retention",
    )(dec_f, dec_b, *([z_main] * 6), s0f, s0b)


def _rms_norm(x, g):
    return x * lax.rsqrt(jnp.mean(x * x, -1, keepdims=True) + RMS_EPS) * g


def _mla_keys(kn, kr):
    lane_r = lax.broadcasted_iota(jnp.int32, (1, LANE), 1)
    return jnp.concatenate([kn, jnp.where(lane_r < MLA_ROPE, kr, 0.0)], 1).astype(BF16)


def _mla_attend(qn, qr, k_cat, vv, o_ref, row0):
    qscale = (MLA_NOPE + MLA_ROPE) ** -0.5 * LOG2E
    lane_n = lax.broadcasted_iota(jnp.int32, (1, MLA_HEADS * MLA_NOPE), 1) // MLA_NOPE
    lane_r = lax.broadcasted_iota(jnp.int32, (1, LANE), 1)
    qn = qn * qscale
    qr = qr * qscale
    lq = qn.shape[0]
    heads = []
    for h in range(MLA_HEADS):
        qnh = jnp.where(lane_n == h, qn, 0.0)
        qrh = qr if h == 0 else pltpu.roll(qr, LANE - h * MLA_ROPE, axis=1)
        qrh = jnp.where(lane_r < MLA_ROPE, qrh, 0.0)
        heads.append(jnp.concatenate([qnh, qrh], 1).astype(BF16))
    s = _dot_nt(jnp.concatenate(heads, 0), k_cat)
    m = jnp.max(s, -1, keepdims=True)
    p = jnp.exp2(s - m)
    den = jnp.sum(p, -1, keepdims=True)
    o = _dot(p, vv) / den
    acc = None
    for h in range(MLA_HEADS):
        oh = jnp.where(lane_n == h, o[h * lq:(h + 1) * lq], 0.0)
        acc = oh if acc is None else acc + oh
    o_ref[row0:row0 + lq, :] = acc.astype(o_ref.dtype)


MLA_QN = MLA_HEADS * MLA_NOPE


def _mla_kernel(cq_ref, ckv_ref, kr_ref, cckv_ref, ckr_ref, c_ref, su_ref, sd_ref,
                qg_ref, kg_ref, wq_ref, wkv_ref, o_ref, ckvn_ref):
    weights = (qg_ref, kg_ref, wq_ref, wkv_ref)
    _group_step(
        lambda: _mla_ctx_body(cq_ref, ckv_ref, kr_ref, *weights, o_ref, ckvn_ref),
        lambda: _mla_lat_body(cq_ref, ckv_ref, kr_ref, cckv_ref, ckr_ref, c_ref, su_ref, sd_ref,
                              *weights, o_ref, ckvn_ref))


def _mla_ctx_body(cq_ref, ckv_ref, kr_ref, qg_ref, kg_ref, wq_ref, wkv_ref, o_ref, ckvn_ref):
    q = _dot(_rms_norm(cq_ref[...].astype(F32), qg_ref[...]), wq_ref[...])
    qn, qr = q[:, 0:MLA_QN], q[:, MLA_QN:]
    ckvn = _rms_norm(ckv_ref[...].astype(F32), kg_ref[...])
    ckvn_ref[...] = ckvn
    kv = _dot(ckvn, wkv_ref[...])
    k_cat = _mla_keys(kv[:, 0:MLA_QN], kr_ref[...].astype(F32))
    vv = kv[:, MLA_QN:].astype(BF16)
    for g in range(CTX_SEQS_PER_STEP):
        sl = slice(g * SEQ, (g + 1) * SEQ)
        _mla_attend(qn[sl], qr[sl], k_cat[sl], vv[sl], o_ref, g * SEQ)


def _mla_lat_body(cq_ref, ckv_ref, kr_ref, cckv_ref, ckr_ref, c_ref, su_ref, sd_ref,
                  qg_ref, kg_ref, wq_ref, wkv_ref, o_ref, ckvn_ref):
    half = MLA_ROPE // 2
    c, su, sd = c_ref[...], su_ref[...], sd_ref[...]
    q = _dot(_rms_norm(cq_ref[...].astype(F32), qg_ref[...]), wq_ref[...])
    qn = q[:, 0:MLA_QN]
    qr = _rope128(q[:, MLA_QN:], c, su, sd, half)
    ckvn = _rms_norm(ckv_ref[...].astype(F32), kg_ref[...])
    ckvn_ref[...] = ckvn
    ckv_all = jnp.concatenate([ckvn, cckv_ref[...]], 0)
    kv = _dot(ckv_all, wkv_ref[...])
    vv = kv[:, MLA_QN:].astype(BF16)
    kr = jnp.concatenate([_rope128(kr_ref[...].astype(F32), c, su, sd, half), ckr_ref[...]], 0)
    k_cat = _mla_keys(kv[:, 0:MLA_QN], kr)
    rows_per_call = 256
    for n in range(DEC_SEQ // rows_per_call):
        rows = slice(n * rows_per_call, (n + 1) * rows_per_call)
        _mla_attend(qn[rows], qr[rows], k_cat, vv, o_ref, n * rows_per_call)


def _mla(z_main, cache_ckv, cache_kr_pad, weights, layer):
    c, su, sd, _ = _rope_tables(DEC_SEQ, MLA_ROPE, LANE)
    tab = _const_spec((DEC_SEQ, LANE))
    cache = pl.BlockSpec((None, None, PAST_LEN, LANE), lambda i: (_lat_index(i), layer, 0, 0))

    def weight(*shape):
        return pl.BlockSpec((None,) + shape, lambda i: (layer, 0, 0))

    return pl.pallas_call(
        _mla_kernel,
        out_shape=(jax.ShapeDtypeStruct((T_ALL, MLA_HEADS * MLA_V), BF16),
                   jax.ShapeDtypeStruct((T_ALL, MLA_KV_LORA), F32)),
        grid=(MIXER_STEPS,),
        in_specs=[
            pl.BlockSpec((STEP_ROWS, 256), lambda i: (i, COL_CQ // 256)),
            pl.BlockSpec((STEP_ROWS, LANE), lambda i: (i, COL_CKV // LANE)),
            pl.BlockSpec((STEP_ROWS, LANE), lambda i: (i, COL_KROPE // LANE)),
            cache, cache, tab, tab, tab,
            weight(1, MLA_Q_LORA), weight(1, MLA_KV_LORA),
            weight(MLA_Q_LORA, MLA_QN + MLA_HEADS * MLA_ROPE),
            weight(MLA_KV_LORA, MLA_QN + MLA_HEADS * MLA_V),
        ],
        out_specs=(pl.BlockSpec((STEP_ROWS, 256), lambda i: (i, 0)),
                   pl.BlockSpec((STEP_ROWS, MLA_KV_LORA), lambda i: (i, 0))),
        compiler_params=_params("parallel"),
        name="mla",
    )(z_main, z_main, z_main, cache_ckv, cache_kr_pad,
      jnp.asarray(c), jnp.asarray(su), jnp.asarray(sd), *weights)


def _route(logits_t, rb):
    scores = jax.nn.sigmoid(logits_t)
    biased = scores + rb
    sc = [scores[e:e + 1, :] for e in range(N_EXPERTS)]
    bi = [biased[e:e + 1, :] for e in range(N_EXPERTS)]
    epg = EXPERTS_PER_GROUP
    gsum = []
    for g in range(N_GROUPS):
        v = bi[g * epg:(g + 1) * epg]
        best = None
        for a in range(epg):
            for b in range(a + 1, epg):
                pair = v[a] + v[b]
                best = pair if best is None else jnp.maximum(best, pair)
        gsum.append(best)
    combine = []
    sel = []
    for g in range(N_GROUPS):
        is_best = None
        for g2 in range(N_GROUPS):
            if g2 == g:
                continue
            c = gsum[g] > gsum[g2] if g2 < g else gsum[g] >= gsum[g2]
            is_best = c if is_best is None else jnp.logical_and(is_best, c)
        for a in range(epg):
            e = g * epg + a
            rank = jnp.zeros_like(bi[e])
            for b in range(epg):
                if b == a:
                    continue
                e2 = g * epg + b
                ahead = bi[e2] >= bi[e] if b < a else bi[e2] > bi[e]
                rank = rank + jnp.where(ahead, 1.0, 0.0)
            sel.append(jnp.logical_and(is_best, rank < 2.0))
    wsum = None
    for e in range(N_EXPERTS):
        w = jnp.where(sel[e], sc[e], 0.0)
        wsum = w if wsum is None else wsum + w
    for e in range(N_EXPERTS):
        combine.append(jnp.where(sel[e], ROUTE_SCALE * sc[e] / wsum, 0.0))
    return jnp.concatenate(combine, 0)


MERGE_TILE = 512

MERGE_BRANCH_ROWS = (HY_W, WIN_HEADS * WIN_HD, RET_HEADS * RET_DV, MLA_HEADS * MLA_V)
MERGE_ROWS = sum(MERGE_BRANCH_ROWS) + D_MODEL


def _merge_kernel(ya_ref, yb_ref, yc_ref, yd_ref, gt_ref, xc_ref, xl_ref, m_ref,
                  w_ref, g_ref, b_ref, rw_ref, rb_ref,
                  x1_ref, h2_ref, cmb_ref, *, ctx_tiles, sub_rows):
    D = D_MODEL
    rw = rw_ref[...]
    rw_hi = rw.astype(BF16)
    rw_lo = (rw - rw_hi.astype(F32)).astype(BF16)
    g1 = m_ref[:, 2 * D:3 * D]
    s2 = m_ref[:, 3 * D:4 * D]
    sc2 = m_ref[:, 4 * D:5 * D]
    is_ctx = pl.program_id(0) < ctx_tiles
    offs = np.cumsum((0,) + MERGE_BRANCH_ROWS)
    branches = tuple((y_ref, slice(int(offs[i]), int(offs[i + 1])))
                     for i, y_ref in enumerate((ya_ref, yb_ref, yc_ref, yd_ref)))
    w_out_rows = slice(int(offs[-1]), MERGE_ROWS)
    for r0 in range(0, x1_ref.shape[0], sub_rows):
        rows = slice(r0, r0 + sub_rows)
        merged = None
        for i, (y_ref, w_rows) in enumerate(branches):
            t = gt_ref[rows, i * D:(i + 1) * D] * jnp.dot(
                y_ref[rows, :], w_ref[w_rows, :], preferred_element_type=F32).astype(BF16)
            merged = t if merged is None else merged + t
        out1 = jnp.dot(merged, w_ref[w_out_rows, :], preferred_element_type=F32)
        x = jnp.where(is_ctx, xc_ref[rows, :], xl_ref[rows, :])
        x1 = _layer_norm(ALPHA * x + g1 * out1) * g_ref[...] + b_ref[...]
        x1_ref[rows, :] = x1
        h2 = _layer_norm(x1) * (1.0 + sc2) + s2
        h2_hi = h2.astype(BF16)
        h2_ref[rows, :] = h2_hi
        h2_lo = (h2 - h2_hi.astype(F32)).astype(BF16)
        logits = (jnp.dot(h2_hi, rw_hi, preferred_element_type=F32)
                  + (jnp.dot(h2_lo, rw_hi, preferred_element_type=F32)
                     + jnp.dot(h2_hi, rw_lo, preferred_element_type=F32)))
        combine_t = _route(logits.T[0:N_EXPERTS], rb_ref[...])
        cmb_ref[rows, :] = jnp.concatenate(
            [combine_t, jnp.zeros((LANE - N_EXPERTS, sub_rows), F32)], 0).T


def _merge(ya, yb, yc, yd, gates, x_ctx, x_lat, lat_block0, mods, w_merge, ln1_g, ln1_b,
           router_w, router_b, layer):
    tm = MERGE_TILE
    row = _mod_row(tm)
    D = D_MODEL
    ctx_tiles = T_CTX // tm

    def tile(w):
        return pl.BlockSpec((tm, w), lambda i: (i, 0))

    def weight(k, n):
        return pl.BlockSpec((None, k, n), lambda i: (layer, 0, 0))

    return pl.pallas_call(
        functools.partial(_merge_kernel, ctx_tiles=ctx_tiles, sub_rows=256),
        out_shape=(jax.ShapeDtypeStruct((T_ALL, D), F32),
                   jax.ShapeDtypeStruct((T_ALL, D), BF16),
                   jax.ShapeDtypeStruct((T_ALL, LANE), F32)),
        grid=(T_ALL // tm,),
        in_specs=[
            tile(256), tile(256), tile(512), tile(256), tile(4 * D),
            *_x_specs(tm, ctx_tiles, lat_block0),
            _mod_spec(layer, row),
            weight(MERGE_ROWS, D), weight(1, D), weight(1, D),
            pl.BlockSpec((D, LANE), lambda i: (0, 0)),
            pl.BlockSpec((N_EXPERTS, 1), lambda i: (0, 0)),
        ],
        out_specs=(tile(D), tile(D), tile(LANE)),
        compiler_params=_params("parallel"),
        name="merge",
    )(ya, yb, yc, yd, gates, x_ctx, x_lat, mods, w_merge,
      ln1_g.reshape(DEPTH, 1, D), ln1_b.reshape(DEPTH, 1, D), router_w,
      router_b.reshape(N_EXPERTS, 1))


MOE_EXPERTS_PER_STEP = 2


def _moe_kernel(*refs, next_h):
    if next_h:
        (h_ref, c_ref, x1_ref, m_ref, wg_ref, wu_ref, wd_ref, g_ref, b_ref, mn_ref,
         o_ref, hn_ref, acc_ref) = refs
    else:
        h_ref, c_ref, x1_ref, m_ref, wg_ref, wu_ref, wd_ref, g_ref, b_ref, o_ref, acc_ref = refs
    eg = pl.program_id(1)

    @pl.when(eg == 0)
    def _():
        acc_ref[...] = jnp.zeros_like(acc_ref)

    h = h_ref[...]
    cmb = c_ref[...]
    lane = lax.broadcasted_iota(jnp.int32, cmb.shape, 1)
    hid = []
    for k in range(MOE_EXPERTS_PER_STEP):
        gate = jnp.dot(h, wg_ref[k].astype(BF16), preferred_element_type=F32)
        up = jnp.dot(h, wu_ref[k].astype(BF16), preferred_element_type=F32)
        e = eg * MOE_EXPERTS_PER_STEP + k
        ce = jnp.sum(jnp.where(lane == e, cmb, 0.0), -1, keepdims=True)
        sig = 0.5 * jnp.tanh(0.5 * gate) + 0.5
        hid.append((gate * sig * (up * ce)).astype(BF16))
    wd = wd_ref[...].reshape(MOE_EXPERTS_PER_STEP * D_EXPERT, D_MODEL).astype(BF16)
    acc_ref[...] += jnp.dot(jnp.concatenate(hid, 1), wd, preferred_element_type=F32)

    @pl.when(eg == N_EXPERTS // MOE_EXPERTS_PER_STEP - 1)
    def _():
        g2 = m_ref[:, 5 * D_MODEL:6 * D_MODEL]
        y = _layer_norm(ALPHA * x1_ref[...] + g2 * acc_ref[...])
        y = y * g_ref[...] + b_ref[...]
        o_ref[...] = y
        if next_h:
            s1 = mn_ref[:, 0:D_MODEL]
            sc1 = mn_ref[:, D_MODEL:2 * D_MODEL]
            hn_ref[...] = (_layer_norm(y) * (1.0 + sc1) + s1).astype(hn_ref.dtype)


def _moe(h2, combine, x1, mods, w_gate, w_up, w_down, ln2_g, ln2_b, layer, row0, n_rows,
         next_h=False):
    tm = 1024
    row = _mod_row(tm)
    D = D_MODEL
    t0 = row0 // tm
    eps = MOE_EXPERTS_PER_STEP
    mod_spec = _mod_spec(layer, lambda i, e: row(t0 + i))
    in_specs = [
        pl.BlockSpec((tm, D), lambda i, e: (t0 + i, 0)),
        pl.BlockSpec((tm, LANE), lambda i, e: (t0 + i, 0)),
        pl.BlockSpec((tm, D), lambda i, e: (t0 + i, 0)),
        mod_spec,
        pl.BlockSpec((None, eps, D, D_EXPERT), lambda i, e: (layer, e, 0, 0)),
        pl.BlockSpec((None, eps, D, D_EXPERT), lambda i, e: (layer, e, 0, 0)),
        pl.BlockSpec((None, eps, D_EXPERT, D), lambda i, e: (layer, e, 0, 0)),
        pl.BlockSpec((None, 1, D), lambda i, e: (layer, 0, 0)),
        pl.BlockSpec((None, 1, D), lambda i, e: (layer, 0, 0)),
    ]
    args = [h2, combine, x1, mods, w_gate, w_up, w_down,
            ln2_g.reshape(DEPTH, 1, D), ln2_b.reshape(DEPTH, 1, D)]
    out_shape = jax.ShapeDtypeStruct((n_rows, D), F32)
    out_specs = pl.BlockSpec((tm, D), lambda i, e: (i, 0))
    if next_h:
        in_specs.append(_mod_spec(layer + 1, lambda i, e: row(t0 + i)))
        args.append(mods)
        out_shape = (out_shape, jax.ShapeDtypeStruct((T_ALL, D), BF16))
        assert n_rows == T_ALL
        out_specs = (out_specs, pl.BlockSpec((tm, D), lambda i, e: (i, 0)))
    return pl.pallas_call(
        functools.partial(_moe_kernel, next_h=next_h),
        out_shape=out_shape,
        grid=(n_rows // tm, N_EXPERTS // eps),
        in_specs=in_specs,
        out_specs=out_specs,
        scratch_shapes=[pltpu.VMEM((tm, D), F32)],
        compiler_params=_params("parallel", "arbitrary"),
        name="moe",
    )(*args)


def kernel(x_prompt, x_sample, cache_win_k, cache_win_v, cache_mla_ckv, cache_mla_krope,
           state_ret_fwd, state_ret_bwd, c, c_ctx, w_ada, b_ada, w_in,
           hy_conv_w, hy_conv_b, hy_w1, hy_b1, hy_w2, hy_b2, hy_w3, hy_bias,
           win_sink, ret_decay_fwd, ret_decay_bwd, mla_q_norm, mla_kv_norm, mla_w_uq, mla_w_ukv,
           w_br_a, w_br_b, w_br_c, w_br_d, w_out, ln1_g, ln1_b, ln2_g, ln2_b,
           router_w, router_b, moe_w_gate, moe_w_up, moe_w_down):
    D = D_MODEL
    x_ctx = x_prompt.reshape(T_CTX, D)
    x_lat = x_sample.reshape(T_LAT, D)
    lat_block0 = 0

    mods = _ada_mods(c_ctx, c, w_ada, b_ada)

    w_in_t = jnp.swapaxes(w_in, 1, 2).reshape(DEPTH * IN_COLS, D)
    cache_k = cache_win_k.reshape(DEC_BATCH, DEPTH, PAST_LEN, WIN_KV_HEADS * WIN_HD)
    cache_v = cache_win_v.reshape(DEC_BATCH, DEPTH, PAST_LEN, WIN_KV_HEADS * WIN_HD)
    cache_kr = jnp.pad(cache_mla_krope, ((0, 0), (0, 0), (0, 0), (0, LANE - MLA_ROPE)))

    uq = mla_w_uq.reshape(DEPTH, MLA_Q_LORA, MLA_HEADS, MLA_NOPE + MLA_ROPE)
    ukv = mla_w_ukv.reshape(DEPTH, MLA_KV_LORA, MLA_HEADS, MLA_NOPE + MLA_V)
    mla_weights = (
        mla_q_norm.reshape(DEPTH, 1, MLA_Q_LORA),
        mla_kv_norm.reshape(DEPTH, 1, MLA_KV_LORA),
        jnp.concatenate([uq[..., :MLA_NOPE].reshape(DEPTH, MLA_Q_LORA, MLA_HEADS * MLA_NOPE),
                         uq[..., MLA_NOPE:].reshape(DEPTH, MLA_Q_LORA, MLA_HEADS * MLA_ROPE)], -1),
        jnp.concatenate([ukv[..., :MLA_NOPE].reshape(DEPTH, MLA_KV_LORA, MLA_HEADS * MLA_NOPE),
                         ukv[..., MLA_NOPE:].reshape(DEPTH, MLA_KV_LORA, MLA_HEADS * MLA_V)], -1),
    )

    hy_w1p = jnp.pad(hy_w1, ((0, 0), (0, LANE - HY_EMB), (0, 0)))
    dft = {}
    for L in (SEQ, DEC_SEQ):
        fwd, inv = _dft_tables(L)
        dft[L] = (jnp.asarray(fwd).astype(BF16), jnp.asarray(inv).astype(BF16))
    router_w_pad = jnp.pad(router_w, ((0, 0), (0, LANE - N_EXPERTS)))
    w_merge = jnp.concatenate([w_br_a, w_br_b, w_br_c, w_br_d, w_out], 1).astype(BF16)

    new_k, new_v, new_ckv, new_kr, new_sf, new_sb = [], [], [], [], [], []
    for l in range(DEPTH):
        if l == 0:
            h = _ln_mod(x_ctx, x_lat, mods, l)
        z = _in_proj(h, w_in_t, l, 0, Z_MAIN, Z_MAIN // 2, BF16, gate=False)
        gates = _in_proj(h, w_in_t, l, COL_GATE, 4 * D, D, BF16, gate=True)

        filters = {L: _hy_filters(L, hy_w1p[l], hy_b1[l][None], hy_w2[l], hy_b2[l][None], hy_w3[l],
                                  dft[L][0]) for L in (SEQ, DEC_SEQ)}
        ya = _hyena(z, l, hy_conv_w, hy_conv_b, hy_bias, filters, dft)
        yb = _win(z, win_sink, cache_k, cache_v, l)
        yc, sf, sb = _retention(z, ret_decay_fwd, ret_decay_bwd, state_ret_fwd, state_ret_bwd, l)
        yd, ckvn = _mla(z, cache_mla_ckv, cache_kr, mla_weights, l)

        x1, h2, combine = _merge(ya, yb, yc, yd, gates, x_ctx, x_lat, lat_block0, mods, w_merge,
                                   ln1_g, ln1_b, router_w_pad, router_b, l)
        moe_args = (h2, combine, x1, mods, moe_w_gate, moe_w_up, moe_w_down, ln2_g, ln2_b, l)
        if l + 1 < DEPTH:
            x_ctx, h = _moe(*moe_args, 0, T_ALL, next_h=True)
            x_lat, lat_block0 = x_ctx, T_CTX // MERGE_TILE
        else:
            x_ctx = _moe(*moe_args, 0, T_CTX)
            x_lat = _moe(*moe_args, T_CTX, T_LAT)

        def ctx_cols(col, width):
            return z[:T_CTX, col:col + width].astype(F32)

        new_k.append(ctx_cols(COL_WK, 128).reshape(BATCH, SEQ, WIN_KV_HEADS, WIN_HD))
        new_v.append(ctx_cols(COL_WV, 128).reshape(BATCH, SEQ, WIN_KV_HEADS, WIN_HD))
        new_ckv.append(ckvn[:T_CTX].reshape(BATCH, SEQ, MLA_KV_LORA))
        new_kr.append(ctx_cols(COL_KROPE, MLA_ROPE).reshape(BATCH, SEQ, MLA_ROPE))
        new_sf.append(sf[:BATCH])
        new_sb.append(sb[:BATCH])

    y_prompt = x_ctx.reshape(BATCH, SEQ, D)
    y_sample = x_lat.reshape(DEC_BATCH, DEC_SEQ, D)
    return (y_prompt, y_sample, jnp.stack(new_k, 1), jnp.stack(new_v, 1), jnp.stack(new_ckv, 1),
            jnp.stack(new_kr, 1), jnp.stack(new_sf, 1), jnp.stack(new_sb, 1))
```

```python
import functools
import math

import numpy as np
import jax
import jax.numpy as jnp
from jax import lax
from jax.experimental import pallas as pl
from jax.experimental.pallas import tpu as pltpu

F32 = jnp.float32
BF16 = jnp.bfloat16

D_MODEL = 1024
BATCH = 16
SEQ = 256
DEPTH = 2
DEC_BATCH = 2
DEC_SEQ = 1024
PAST_LEN = 256
GRID_W = 64
CHUNK = 128
ROPE_BASE = 10000.0
NEG = -1e30
LN_EPS = 1e-5
RMS_EPS = 1e-6
LOG2E = math.log2(math.e)

HY_W = 256
HY_BANDS = 16
HY_EMB = 1 + 2 * HY_BANDS
HY_FFN = 64
HY_FAST_DECAY = 0.3
HY_SLOW_DECAY = 1.5
HY_TARGET = 1e-2

WIN_HEADS = 4
WIN_KV_HEADS = 2
WIN_HD = 64
WINDOW = 128

RET_HEADS = 4
RET_DK = 64
RET_DV = 128

MLA_HEADS = 4
MLA_Q_LORA = 256
MLA_KV_LORA = 128
MLA_NOPE = 64
MLA_ROPE = 32
MLA_V = 64

N_EXPERTS = 16
N_GROUPS = 4
EXPERTS_PER_GROUP = N_EXPERTS // N_GROUPS
D_EXPERT = 256
ROUTE_SCALE = 2.5

ALPHA = (2.0 * DEPTH) ** 0.25

T_CTX = BATCH * SEQ
T_LAT = DEC_BATCH * DEC_SEQ
T_ALL = T_CTX + T_LAT

COL_HY = 0
COL_WQ = 768
COL_WK = 1024
COL_WV = 1152
COL_RQ = 1280
COL_RK = 1536
COL_RV = 1792
COL_RG = 2304
COL_CQ = 2816
COL_CKV = 3072
COL_KROPE = 3200
COL_GATE = 3232
IN_COLS = COL_GATE + 4 * D_MODEL
Z_MAIN = 3328

LANE = 128
STEP_ROWS = 1024
CTX_SEQS_PER_STEP = STEP_ROWS // SEQ
CTX_STEPS = T_CTX // STEP_ROWS
MIXER_STEPS = T_ALL // STEP_ROWS
VMEM_LIMIT = 56 * 1024 * 1024


def _params(*sem):
    return pltpu.CompilerParams(dimension_semantics=sem, vmem_limit_bytes=VMEM_LIMIT)


def _dot(a, b):
    return jnp.dot(a.astype(BF16), b.astype(BF16), preferred_element_type=F32)


def _dot_split(a, b):
    a_hi = a.astype(BF16)
    a_lo = (a - a_hi.astype(F32)).astype(BF16)
    b_hi = b.astype(BF16)
    b_lo = (b - b_hi.astype(F32)).astype(BF16)

    def mm(x, y):
        return jnp.dot(x, y, preferred_element_type=F32)

    return mm(a_hi, b_hi) + (mm(a_lo, b_hi) + mm(a_hi, b_lo))


def _dot_nt(a, b):
    return lax.dot_general(a.astype(BF16), b.astype(BF16), (((1,), (1,)), ((), ())),
                           preferred_element_type=F32)


def _dot_tn(a, b):
    return lax.dot_general(a.astype(BF16), b.astype(BF16), (((0,), (0,)), ((), ())),
                           preferred_element_type=F32)


def _layer_norm(x):
    mu = jnp.mean(x, -1, keepdims=True)
    xc = x - mu
    var = jnp.mean(xc * xc, -1, keepdims=True)
    return xc * lax.rsqrt(var + LN_EPS)


def _mod_row(tile_rows):
    def row(i):
        start = i * tile_rows
        return jnp.where(start < T_CTX, 0, 1 + (start - T_CTX) // DEC_SEQ)
    return row


@functools.lru_cache(maxsize=None)
def _dft_tables(L):
    f = np.arange(L, dtype=np.int64)[:, None]
    s = np.arange(L, dtype=np.int64)[None, :]
    ang = np.pi * ((f * s) % (2 * L)).astype(np.float64) / L
    cos = np.cos(ang)
    sin = np.sin(ang)
    alt = np.where(np.arange(L) % 2 == 0, 1.0, -1.0)
    fwd_im = -sin
    fwd_im[0, :] = alt
    fwd = np.concatenate([cos, fwd_im], 0)
    inv_re = cos.T / L
    inv_re[:, 0] = 1.0 / (2 * L)
    inv_im = -sin.T / L
    inv_im[:, 0] = alt / (2 * L)
    inv = np.concatenate([inv_re, inv_im], 1)
    return fwd.astype(np.float32), inv.astype(np.float32)


@functools.lru_cache(maxsize=None)
def _hyena_embedding(L):
    t01 = np.linspace(0.0, 1.0, L, dtype=np.float64)[:, None]
    bands = np.linspace(1e-4, HY_BANDS - 1, HY_BANDS, dtype=np.float64)
    ang = (2.0 * math.pi / L) * np.arange(L, dtype=np.float64)[:, None] * bands[None, :]
    z = np.concatenate([t01, np.cos(ang), -np.sin(ang)], -1)
    zp = np.zeros((L, LANE), np.float64)
    zp[:, :HY_EMB] = z
    deltas = np.abs(np.linspace(math.log(HY_TARGET) / HY_SLOW_DECAY,
                                math.log(HY_TARGET) / HY_FAST_DECAY, HY_W, dtype=np.float64))
    return zp.astype(np.float32), deltas[None, :].astype(np.float32)


@functools.lru_cache(maxsize=None)
def _rope_tables(L, rot_dim, width):
    rows = L // GRID_W
    n_freq = rot_dim // 4
    half = rot_dim // 2
    inv = ROPE_BASE ** (-np.arange(n_freq, dtype=np.float64) / n_freq)
    pos = np.arange(L)
    row = (pos // GRID_W).astype(np.float64)
    col = (pos % GRID_W).astype(np.float64)
    ang = np.concatenate([row[:, None] * inv, col[:, None] * inv], -1)
    cos, sin = np.cos(ang), np.sin(ang)
    zero = np.zeros_like(sin)
    c = np.tile(np.concatenate([cos, cos], -1), (1, width // rot_dim))
    s_up = np.tile(np.concatenate([-sin, zero], -1), (1, width // rot_dim))
    s_dn = np.tile(np.concatenate([zero, sin], -1), (1, width // rot_dim))
    return c.astype(np.float32), s_up.astype(np.float32), s_dn.astype(np.float32), half


def _rope128(x, c, s_up, s_dn, half):
    up = pltpu.roll(x, LANE - half, axis=1)
    dn = pltpu.roll(x, half, axis=1)
    return x * c + up * s_up + dn * s_dn


MOD_ROWS = 1 + DEC_BATCH


def _ada_kernel(ct_ref, w_ref, b_ref, o_ref):
    @pl.when(pl.program_id(1) == 0)
    def _():
        for r in range(MOD_ROWS):
            o_ref[r] = b_ref[...]

    ct = ct_ref[...]
    s = ct * jax.nn.sigmoid(ct)
    tk, n = w_ref.shape
    rows, cols = 64, 4 * LANE
    s_cols = [[s[k0:k0 + rows, r:r + 1] for k0 in range(0, tk, rows)] for r in range(MOD_ROWS)]
    for c0 in range(0, n, cols):
        acc = [None] * MOD_ROWS
        for ki, k0 in enumerate(range(0, tk, rows)):
            wc = w_ref[k0:k0 + rows, c0:c0 + cols]
            for r in range(MOD_ROWS):
                part = jnp.sum((wc * s_cols[r][ki]).reshape(rows // 8, 8, cols), axis=0)
                acc[r] = part if acc[r] is None else acc[r] + part
        for r in range(MOD_ROWS):
            o_ref[r, :, c0:c0 + cols] += jnp.sum(acc[r], axis=0, keepdims=True)


def _ada_mods(c_ctx, c, w_ada, b_ada):
    tk = 512
    n = 6 * D_MODEL
    c_cols = jnp.concatenate([c_ctx[:, None], c.T], 1)
    return pl.pallas_call(
        _ada_kernel,
        out_shape=jax.ShapeDtypeStruct((DEPTH, MOD_ROWS, 1, n), F32),
        grid=(DEPTH, D_MODEL // tk),
        in_specs=[
            pl.BlockSpec((tk, MOD_ROWS), lambda l, k: (k, 0)),
            pl.BlockSpec((None, tk, n), lambda l, k: (l, k, 0)),
            pl.BlockSpec((None, 1, n), lambda l, k: (l, 0, 0)),
        ],
        out_specs=pl.BlockSpec((None, MOD_ROWS, 1, n), lambda l, k: (l, 0, 0, 0)),
        compiler_params=_params("parallel", "arbitrary"),
        name="ada_mods",
    )(c_cols, w_ada, b_ada.reshape(DEPTH, 1, n))


def _mod_spec(layer, row_of_step):
    return pl.BlockSpec((None, None, 1, 6 * D_MODEL), lambda *g: (layer, row_of_step(*g), 0, 0))


def _lnmod_kernel(xc_ref, xl_ref, m_ref, h_ref, *, ctx_tiles):
    x = jnp.where(pl.program_id(0) < ctx_tiles, xc_ref[...], xl_ref[...])
    y = _layer_norm(x)
    s1 = m_ref[:, 0:D_MODEL]
    sc1 = m_ref[:, D_MODEL:2 * D_MODEL]
    h_ref[...] = (y * (1.0 + sc1) + s1).astype(h_ref.dtype)


def _x_specs(tm, ctx_tiles, lat_block0=0):
    return [pl.BlockSpec((tm, D_MODEL), lambda i: (jnp.minimum(i, ctx_tiles - 1), 0)),
            pl.BlockSpec((tm, D_MODEL), lambda i: (jnp.maximum(i - ctx_tiles, 0) + lat_block0, 0))]


def _ln_mod(x_ctx, x_lat, mods, layer):
    tm = 1024
    ctx_tiles = T_CTX // tm
    return pl.pallas_call(
        functools.partial(_lnmod_kernel, ctx_tiles=ctx_tiles),
        out_shape=jax.ShapeDtypeStruct((T_ALL, D_MODEL), BF16),
        grid=(T_ALL // tm,),
        in_specs=_x_specs(tm, ctx_tiles) + [_mod_spec(layer, _mod_row(tm))],
        out_specs=pl.BlockSpec((tm, D_MODEL), lambda i: (i, 0)),
        compiler_params=_params("parallel"),
        name="ln_mod",
    )(x_ctx, x_lat, mods)


def _proj_kernel(h_ref, w_ref, o_ref, wb_ref, *, gate):
    @pl.when(pl.program_id(1) == 0)
    def _():
        wb_ref[...] = w_ref[...].astype(BF16)

    if not gate:
        o_ref[...] = _dot_nt(h_ref[...], wb_ref[...]).astype(o_ref.dtype)
        return
    sub = 2 * LANE
    for c0 in range(0, o_ref.shape[1], sub):
        r = _dot_nt(h_ref[...], wb_ref[c0:c0 + sub, :])
        rb = r.astype(o_ref.dtype)
        o_ref[:, c0:c0 + sub] = 0.5 * jnp.tanh(0.5 * rb) + 0.5


def _in_proj(h, w_t, layer, col0, n_cols, tn, out_dtype, gate):
    tm = 2048
    return pl.pallas_call(
        functools.partial(_proj_kernel, gate=gate),
        out_shape=jax.ShapeDtypeStruct((T_ALL, n_cols), out_dtype),
        grid=(n_cols // tn, T_ALL // tm),
        in_specs=[
            pl.BlockSpec((tm, D_MODEL), lambda j, i: (i, 0)),
            pl.BlockSpec((pl.Element(tn), pl.Element(D_MODEL)),
                         lambda j, i: (pl.multiple_of(layer * IN_COLS + col0 + j * tn, 8), 0)),
        ],
        out_specs=pl.BlockSpec((tm, tn), lambda j, i: (i, j)),
        scratch_shapes=[pltpu.VMEM((tn, D_MODEL), BF16)],
        compiler_params=_params("parallel", "arbitrary"),
        name="gate_proj" if gate else "in_proj",
    )(h, w_t)


def _hy_filter_kernel(z_ref, dl_ref, w1_ref, b1_ref, w2_ref, b2_ref, w3_ref, fwd_ref,
                      kre_ref, kim_ref, *, L):
    z = z_ref[...]
    a = jnp.sin(_dot_split(z, w1_ref[...]) + b1_ref[...])
    a = jnp.sin(_dot_split(a, w2_ref[...]) + b2_ref[...])
    h = _dot_split(a, w3_ref[...])
    decay = jnp.exp(-z[:, 0:1] * dl_ref[...])
    not_first = lax.broadcasted_iota(jnp.int32, (L, HY_W), 0) > 0
    sums, diffs = [], []
    for o in range(2):
        fw = h[:, (2 * o) * HY_W:(2 * o + 1) * HY_W] * decay
        bw = jnp.where(not_first, h[:, (2 * o + 1) * HY_W:(2 * o + 2) * HY_W] * decay, 0.0)
        sums.append(fw + bw)
        diffs.append(fw - bw)
    p = _dot(fwd_ref[...], jnp.concatenate(sums, 1))
    q = _dot(fwd_ref[L:2 * L, :], jnp.concatenate(diffs, 1))
    kre_ref[...] = p[0:L]
    first = lax.broadcasted_iota(jnp.int32, (L, 2 * HY_W), 0) == 0
    kim_ref[...] = jnp.where(first, p[L:L + 1], q)


def _hy_filters(L, w1p, b1, w2, b2, w3, fwd):
    zemb, deltas = _hyena_embedding(L)
    out = jax.ShapeDtypeStruct((L, 2 * HY_W), F32)
    return pl.pallas_call(
        functools.partial(_hy_filter_kernel, L=L),
        out_shape=(out, out),
        compiler_params=pltpu.CompilerParams(vmem_limit_bytes=VMEM_LIMIT),
        name=f"hy_filters_{L}",
    )(jnp.asarray(zemb), jnp.asarray(deltas), w1p, b1, w2, b2, w3, fwd)


def _group_step(ctx_body, lat_body, ctx_steps=CTX_STEPS):
    i = pl.program_id(0)
    pl.when(i < ctx_steps)(ctx_body)
    pl.when(i >= ctx_steps)(lat_body)


def _lat_index(i):
    return jnp.maximum(i - CTX_STEPS, 0)


HYENA_STEP_ROWS = T_LAT


def _hyena_kernel(hy_ref, cw_ref, cb_ref, bias_ref, kre_c, kim_c, fwd_c, inv_c,
                  kre_l, kim_l, fwd_l, inv_l, o_ref):
    _group_step(
        lambda: _hyena_body(hy_ref, cw_ref, cb_ref, bias_ref, kre_c, kim_c, fwd_c, inv_c, o_ref,
                            SEQ, HYENA_STEP_ROWS // SEQ),
        lambda: _hyena_body(hy_ref, cw_ref, cb_ref, bias_ref, kre_l, kim_l, fwd_l, inv_l, o_ref,
                            DEC_SEQ, HYENA_STEP_ROWS // DEC_SEQ),
        ctx_steps=T_CTX // HYENA_STEP_ROWS)


def _hyena_body(hy_ref, cw_ref, cb_ref, bias_ref, kre_ref, kim_ref, fwd_ref, inv_ref, o_ref, L, seqs):
    width = seqs * HY_W
    first = lax.broadcasted_iota(jnp.int32, (L, width), 0) == 0

    def tiled(t):
        return jnp.concatenate([t] * seqs, 1)

    def long_conv(u, o):
        uf = _dot(fwd_ref[...], u)
        ure, uim = uf[0:L], uf[L:2 * L]
        kre = tiled(kre_ref[:, o * HY_W:(o + 1) * HY_W])
        kim = tiled(kim_ref[:, o * HY_W:(o + 1) * HY_W])
        yre = jnp.where(first, ure * kre, ure * kre - uim * kim)
        yim = jnp.where(first, uim * kim, ure * kim + uim * kre)
        y = _dot(inv_ref[...], jnp.concatenate([yre, yim], 0))
        return y + u * tiled(bias_ref[o:o + 1, :])

    vs, x1s, x2s = [], [], []
    for g in range(seqs):
        x = hy_ref[g * L:(g + 1) * L, :].astype(F32)
        rows = lax.broadcasted_iota(jnp.int32, x.shape, 0)
        prev = jnp.where(rows == 0, 0.0, pltpu.roll(x, 1, axis=0))
        nxt = jnp.where(rows == L - 1, 0.0, pltpu.roll(x, L - 1, axis=0))
        z = prev * cw_ref[0:1, :] + x * cw_ref[1:2, :] + nxt * cw_ref[2:3, :] + cb_ref[...]
        vs.append(z[:, 0:HY_W])
        x1s.append(z[:, HY_W:2 * HY_W])
        x2s.append(z[:, 2 * HY_W:3 * HY_W])
    u = jnp.concatenate(x1s, 1) * long_conv(jnp.concatenate(vs, 1), 0)
    y = jnp.concatenate(x2s, 1) * long_conv(u, 1)
    for g in range(seqs):
        o_ref[g * L:(g + 1) * L, :] = y[:, g * HY_W:(g + 1) * HY_W].astype(o_ref.dtype)


def _const_spec(shape):
    return pl.BlockSpec(shape, lambda i: (0,) * len(shape))


def _hyena(z_main, layer, conv_w, conv_b, bias, filters, dft):
    tables, table_specs = [], []
    for L in (SEQ, DEC_SEQ):
        tables += [*filters[L], *dft[L]]
        table_specs += [_const_spec((L, 2 * HY_W)), _const_spec((L, 2 * HY_W)),
                        _const_spec((2 * L, L)), _const_spec((L, 2 * L))]
    return pl.pallas_call(
        _hyena_kernel,
        out_shape=jax.ShapeDtypeStruct((T_ALL, HY_W), BF16),
        grid=(T_ALL // HYENA_STEP_ROWS,),
        in_specs=[
            pl.BlockSpec((HYENA_STEP_ROWS, 3 * HY_W), lambda i: (i, 0)),
            pl.BlockSpec((None, 3, 3 * HY_W), lambda i: (layer, 0, 0)),
            pl.BlockSpec((None, 1, 3 * HY_W), lambda i: (layer, 0, 0)),
            pl.BlockSpec((None, 2, HY_W), lambda i: (layer, 0, 0)),
        ] + table_specs,
        out_specs=pl.BlockSpec((HYENA_STEP_ROWS, HY_W), lambda i: (i, 0)),
        compiler_params=_params("parallel"),
        name="hyena",
    )(z_main, conv_w, conv_b.reshape(DEPTH, 1, 3 * HY_W), bias, *tables)


def _win_masks():
    lane = lax.broadcasted_iota(jnp.int32, (1, LANE), 1)
    return lane < WIN_HD, lane >= WIN_HD


def _win_head_operands(q, k, v, h):
    lo_mask, hi_mask = _win_masks()
    col = h // 2
    lo = h % 2 == 0
    q128 = jnp.where(lo_mask if lo else hi_mask, q[:, col * LANE:(col + 1) * LANE], 0.0)
    swap = h in (1, 2)
    if swap:
        k = pltpu.roll(k, WIN_HD, axis=1)
        v = pltpu.roll(v, WIN_HD, axis=1)
    return q128, k, v, lo


def _win_kernel(sink_ref, q_ref, kv_ref, ck_ref, cv_ref, c_ref, su_ref, sd_ref, o_ref, *, layer):
    _group_step(
        lambda: _win_ctx_body(sink_ref, q_ref, kv_ref, o_ref, layer),
        lambda: _win_lat_body(sink_ref, q_ref, kv_ref, ck_ref, cv_ref, c_ref, su_ref, sd_ref, o_ref, layer))


def _win_ctx_body(sink_ref, q_ref, kv_ref, o_ref, layer):
    lo_mask, hi_mask = _win_masks()
    qscale = WIN_HD ** -0.5 * LOG2E
    for g in range(CTX_SEQS_PER_STEP):
        sl = slice(g * SEQ, (g + 1) * SEQ)
        q = q_ref[sl, :].astype(F32) * qscale
        k = kv_ref[sl, 0:LANE].astype(F32)
        v = kv_ref[sl, LANE:2 * LANE].astype(F32)
        cols = []
        for col in range(2):
            acc = None
            for h in (2 * col, 2 * col + 1):
                q128, kk, vv, lo = _win_head_operands(q, k, v, h)
                s = _dot_nt(q128, kk)
                sink = sink_ref[layer, h] * LOG2E
                m = jnp.maximum(jnp.max(s, -1, keepdims=True), sink)
                p = jnp.exp2(s - m)
                den = jnp.sum(p, -1, keepdims=True) + jnp.exp2(sink - m)
                o = _dot(p, vv) / den
                o = jnp.where(lo_mask if lo else hi_mask, o, 0.0)
                acc = o if acc is None else acc + o
            cols.append(acc)
        o_ref[sl, :] = jnp.concatenate(cols, 1).astype(o_ref.dtype)


def _win_lat_body(sink_ref, q_ref, kv_ref, ck_ref, cv_ref, c_ref, su_ref, sd_ref, o_ref, layer):
    L = DEC_SEQ
    half = WIN_HD // 2
    c, su, sd = c_ref[...], su_ref[...], sd_ref[...]
    qscale = WIN_HD ** -0.5 * LOG2E
    q = jnp.concatenate(
        [_rope128(q_ref[:, i * LANE:(i + 1) * LANE].astype(F32), c, su, sd, half) for i in range(2)],
        1) * qscale
    k = _rope128(kv_ref[:, 0:LANE].astype(F32), c, su, sd, half)
    v = kv_ref[:, LANE:2 * LANE].astype(F32)
    ck = ck_ref[...]
    cv = cv_ref[...]
    lo_mask, hi_mask = _win_masks()
    nb = L // CHUNK
    assert WINDOW == CHUNK
    rr = lax.broadcasted_iota(jnp.int32, (CHUNK, CHUNK), 0)
    cc = lax.broadcasted_iota(jnp.int32, (CHUNK, CHUNK), 1)
    band = {-1: jnp.where(cc >= rr, 0.0, NEG), 0: jnp.zeros((CHUNK, CHUNK), F32),
            1: jnp.where(cc <= rr, 0.0, NEG)}
    cols = []
    for col in range(2):
        acc_blocks = [None] * nb
        for h in (2 * col, 2 * col + 1):
            q128, kk, vv, lo = _win_head_operands(q, k, v, h)
            _, ckk, cvv, _ = _win_head_operands(q, ck, cv, h)
            sink = sink_ref[layer, h] * LOG2E
            for n in range(nb):
                blocks = [d for d in (-1, 0, 1) if 0 <= n + d < nb]
                k0 = (n + blocks[0]) * CHUNK
                k1 = (n + blocks[-1] + 1) * CHUNK
                qn = q128[n * CHUNK:(n + 1) * CHUNK]
                s_loc = _dot_nt(qn, kk[k0:k1]) + jnp.concatenate([band[d] for d in blocks], 1)
                s_ctx = _dot_nt(qn, ckk)
                m = jnp.maximum(jnp.maximum(jnp.max(s_loc, -1, keepdims=True),
                                            jnp.max(s_ctx, -1, keepdims=True)), sink)
                p_loc = jnp.exp2(s_loc - m)
                p_ctx = jnp.exp2(s_ctx - m)
                den = (jnp.sum(p_loc, -1, keepdims=True) + jnp.sum(p_ctx, -1, keepdims=True)
                       + jnp.exp2(sink - m))
                o = (_dot(p_loc, vv[k0:k1]) + _dot(p_ctx, cvv)) / den
                o = jnp.where(lo_mask if lo else hi_mask, o, 0.0)
                acc_blocks[n] = o if acc_blocks[n] is None else acc_blocks[n] + o
        cols.append(jnp.concatenate(acc_blocks, 0))
    o_ref[...] = jnp.concatenate(cols, 1).astype(o_ref.dtype)


def _win(z_main, sink, cache_k, cache_v, layer):
    c, su, sd, _ = _rope_tables(DEC_SEQ, WIN_HD, LANE)
    tab = _const_spec((DEC_SEQ, LANE))
    cache = pl.BlockSpec((None, None, PAST_LEN, LANE), lambda i: (_lat_index(i), layer, 0, 0))
    return pl.pallas_call(
        functools.partial(_win_kernel, layer=layer),
        out_shape=jax.ShapeDtypeStruct((T_ALL, WIN_HEADS * WIN_HD), BF16),
        grid=(MIXER_STEPS,),
        in_specs=[
            pl.BlockSpec(memory_space=pltpu.SMEM),
            pl.BlockSpec((STEP_ROWS, 256), lambda i: (i, COL_WQ // 256)),
            pl.BlockSpec((STEP_ROWS, 256), lambda i: (i, COL_WK // 256)),
            cache, cache, tab, tab, tab,
        ],
        out_specs=pl.BlockSpec((STEP_ROWS, 256), lambda i: (i, 0)),
        compiler_params=_params("parallel"),
        name="win",
    )(sink, z_main, z_main, cache_k, cache_v, jnp.asarray(c), jnp.asarray(su), jnp.asarray(sd))


def _ret_kernel(*refs, layer):
    _group_step(lambda: _ret_body(*refs, L=SEQ, layer=layer, ctx=True),
                lambda: _ret_body(*refs, L=DEC_SEQ, layer=layer, ctx=False))


def _ret_body(df_ref, db_ref, q_ref, k_ref, v0_ref, v1_ref, g0_ref, g1_ref, s0f_ref, s0b_ref,
              o_ref, sf_out, sb_out, s_ref, cross_ref, *, L, layer, ctx):
    seqs = STEP_ROWS // L
    if not ctx:
        sf_out[...] = jnp.zeros_like(sf_out)
        sb_out[...] = jnp.zeros_like(sb_out)
    C = CHUNK
    nc = L // C
    H = RET_HEADS
    qw = H * RET_DK
    vw = H * RET_DV

    def lane_table(width, per_head, fn):
        pos = lax.broadcasted_iota(jnp.int32, (C, per_head), 0).astype(F32)
        return jnp.concatenate([fn(h, pos) for h in range(H)], 1)

    def log_gamma(ref, h):
        d = jnp.full((1, 1), ref[layer, h], F32)
        return jnp.log(jax.nn.sigmoid(d))

    lgf = [log_gamma(df_ref, h) for h in range(H)]
    lgb = [log_gamma(db_ref, h) for h in range(H)]

    def tables(lg, reverse):
        if reverse:
            dq = lane_table(vw, RET_DV, lambda h, pos: jnp.exp((C - pos) * lg[h]))
            dk = lane_table(qw, RET_DK, lambda h, pos: jnp.exp(pos * lg[h]))
        else:
            dq = lane_table(vw, RET_DV, lambda h, pos: jnp.exp((pos + 1.0) * lg[h]))
            dk = lane_table(qw, RET_DK, lambda h, pos: jnp.exp((C - 1.0 - pos) * lg[h]))
        dc = jnp.concatenate([jnp.broadcast_to(jnp.exp(C * lg[h]), (1, RET_DV)) for h in range(H)], 1)
        return dq, dk, dc

    tab_f = tables(lgf, False)
    tab_b = tables(lgb, True)
    ii = lax.broadcasted_iota(jnp.int32, (C, C), 0)
    jj = lax.broadcasted_iota(jnp.int32, (C, C), 1)
    diff = (ii - jj).astype(F32)
    dmats = [jnp.where(diff >= 0, jnp.exp(jnp.maximum(diff, 0.0) * lgf[h]), 0.0)
             + jnp.where(diff <= 0, jnp.exp(jnp.maximum(-diff, 0.0) * lgb[h]), 0.0) for h in range(H)]
    dmat_stack = jnp.concatenate(dmats, 0)
    lane_q = lax.broadcasted_iota(jnp.int32, (1, qw), 1) // RET_DK

    srow = lax.broadcasted_iota(jnp.int32, (qw, vw), 0) // RET_DK
    scol = lax.broadcasted_iota(jnp.int32, (qw, vw), 1) // RET_DV
    diag = srow == scol

    for g in range(seqs):
        base = g * L
        rows_all = slice(base, base + L)
        q_all = q_ref[rows_all, :].astype(F32)
        k_all = k_ref[rows_all, :].astype(F32) * (RET_DK ** -0.5)
        v_all = jnp.concatenate([v0_ref[rows_all, :], v1_ref[rows_all, :]], 1).astype(F32)
        g_all = jnp.concatenate([g0_ref[rows_all, :], g1_ref[rows_all, :]], 1).astype(F32)

        def scan(tabs, reverse, s0_ref, s_out):
            dq, dk, dc = tabs
            if s0_ref is not None:
                s_ref[g] = jnp.zeros((qw, vw), F32)
                for h in range(H):
                    s_ref[g, h * RET_DK:(h + 1) * RET_DK, h * RET_DV:(h + 1) * RET_DV] = s0_ref[h]
            order = range(nc - 1, -1, -1) if reverse else range(nc)
            for step, ci in enumerate(order):
                sl = slice(ci * C, (ci + 1) * C)
                rs = slice(base + ci * C, base + (ci + 1) * C)
                qc, kc, vc = q_all[sl], k_all[sl], v_all[sl]
                upd = jnp.where(diag, _dot_tn(kc * dk, vc), 0.0)
                if s0_ref is None and step == 0:
                    if not reverse:
                        cross_ref[rs, :] = jnp.zeros((C, vw), F32)
                    s_ref[g] = upd
                    continue
                st = s_ref[g]
                cross = _dot(qc, st) * dq
                if reverse:
                    cross_ref[rs, :] = cross_ref[rs, :] + cross
                else:
                    cross_ref[rs, :] = cross
                s_ref[g] = st * dc + upd
            if s_out is not None:
                for h in range(H):
                    s_out[g, h] = s_ref[g, h * RET_DK:(h + 1) * RET_DK, h * RET_DV:(h + 1) * RET_DV]

        scan(tab_f, False, None if ctx else s0f_ref, sf_out if ctx else None)
        scan(tab_b, True, None if ctx else s0b_ref, sb_out if ctx else None)

        for ci in range(nc):
            sl = slice(ci * C, (ci + 1) * C)
            rs = slice(base + ci * C, base + (ci + 1) * C)
            q_stack = jnp.concatenate([jnp.where(lane_q == h, q_all[sl], 0.0) for h in range(H)], 0)
            att = _dot_nt(q_stack, k_all[sl]) * dmat_stack
            ov = _dot(att, v_all[sl])
            for h in range(H):
                hv = slice(h * RET_DV, (h + 1) * RET_DV)
                o = ov[h * C:(h + 1) * C, hv] + cross_ref[rs, hv]
                gt = g_all[sl, hv]
                o_ref[rs, hv] = ((gt * jax.nn.sigmoid(gt)) * _layer_norm(o)).astype(o_ref.dtype)


def _retention(z_main, dec_f, dec_b, s0f, s0b, layer):
    def zcol(col):
        return pl.BlockSpec((STEP_ROWS, 256), lambda i: (i, col // 256))

    smem = pl.BlockSpec(memory_space=pltpu.SMEM)
    z_specs = [zcol(COL_RQ), zcol(COL_RK), zcol(COL_RV), zcol(COL_RV + 256),
               zcol(COL_RG), zcol(COL_RG + 256)]
    s0_spec = pl.BlockSpec((None, None, RET_HEADS, RET_DK, RET_DV),
                           lambda i: (_lat_index(i), layer, 0, 0, 0))
    st_shape = jax.ShapeDtypeStruct((MIXER_STEPS * CTX_SEQS_PER_STEP, RET_HEADS, RET_DK, RET_DV), F32)
    st_spec = pl.BlockSpec((CTX_SEQS_PER_STEP, RET_HEADS, RET_DK, RET_DV), lambda i: (i, 0, 0, 0))
    return pl.pallas_call(
        functools.partial(_ret_kernel, layer=layer),
        out_shape=(jax.ShapeDtypeStruct((T_ALL, RET_HEADS * RET_DV), BF16), st_shape, st_shape),
        grid=(MIXER_STEPS,),
        in_specs=[smem, smem] + z_specs + [s0_spec, s0_spec],
        out_specs=(pl.BlockSpec((STEP_ROWS, RET_HEADS * RET_DV), lambda i: (i, 0)), st_spec, st_spec),
        scratch_shapes=[pltpu.VMEM((CTX_SEQS_PER_STEP, RET_HEADS * RET_DK, RET_HEADS * RET_DV), F32),
                        pltpu.VMEM((STEP_ROWS, RET_HEADS * RET_DV), F32)],
        compiler_params=_params("parallel"),
        name="retention",
    )(dec_f, dec_b, *([z_main] * 6), s0f, s0b)


def _rms_norm(x, g):
    return x * lax.rsqrt(jnp.mean(x * x, -1, keepdims=True) + RMS_EPS) * g


def _mla_keys(kn, kr):
    lane_r = lax.broadcasted_iota(jnp.int32, (1, LANE), 1)
    return jnp.concatenate([kn, jnp.where(lane_r < MLA_ROPE, kr, 0.0)], 1).astype(BF16)


def _mla_attend(qn, qr, k_cat, vv, o_ref, row0):
    qscale = (MLA_NOPE + MLA_ROPE) ** -0.5 * LOG2E
    lane_n = lax.broadcasted_iota(jnp.int32, (1, MLA_HEADS * MLA_NOPE), 1) // MLA_NOPE
    lane_r = lax.broadcasted_iota(jnp.int32, (1, LANE), 1)
    qn = qn * qscale
    qr = qr * qscale
    lq = qn.shape[0]
    heads = []
    for h in range(MLA_HEADS):
        qnh = jnp.where(lane_n == h, qn, 0.0)
        qrh = qr if h == 0 else pltpu.roll(qr, LANE - h * MLA_ROPE, axis=1)
        qrh = jnp.where(lane_r < MLA_ROPE, qrh, 0.0)
        heads.append(jnp.concatenate([qnh, qrh], 1).astype(BF16))
    s = _dot_nt(jnp.concatenate(heads, 0), k_cat)
    m = jnp.max(s, -1, keepdims=True)
    p = jnp.exp2(s - m)
    den = jnp.sum(p, -1, keepdims=True)
    o = _dot(p, vv) / den
    acc = None
    for h in range(MLA_HEADS):
        oh = jnp.where(lane_n == h, o[h * lq:(h + 1) * lq], 0.0)
        acc = oh if acc is None else acc + oh
    o_ref[row0:row0 + lq, :] = acc.astype(o_ref.dtype)


MLA_QN = MLA_HEADS * MLA_NOPE


def _mla_kernel(cq_ref, ckv_ref, kr_ref, cckv_ref, ckr_ref, c_ref, su_ref, sd_ref,
                qg_ref, kg_ref, wq_ref, wkv_ref, o_ref, ckvn_ref):
    weights = (qg_ref, kg_ref, wq_ref, wkv_ref)
    _group_step(
        lambda: _mla_ctx_body(cq_ref, ckv_ref, kr_ref, *weights, o_ref, ckvn_ref),
        lambda: _mla_lat_body(cq_ref, ckv_ref, kr_ref, cckv_ref, ckr_ref, c_ref, su_ref, sd_ref,
                              *weights, o_ref, ckvn_ref))


def _mla_ctx_body(cq_ref, ckv_ref, kr_ref, qg_ref, kg_ref, wq_ref, wkv_ref, o_ref, ckvn_ref):
    q = _dot(_rms_norm(cq_ref[...].astype(F32), qg_ref[...]), wq_ref[...])
    qn, qr = q[:, 0:MLA_QN], q[:, MLA_QN:]
    ckvn = _rms_norm(ckv_ref[...].astype(F32), kg_ref[...])
    ckvn_ref[...] = ckvn
    kv = _dot(ckvn, wkv_ref[...])
    k_cat = _mla_keys(kv[:, 0:MLA_QN], kr_ref[...].astype(F32))
    vv = kv[:, MLA_QN:].astype(BF16)
    for g in range(CTX_SEQS_PER_STEP):
        sl = slice(g * SEQ, (g + 1) * SEQ)
        _mla_attend(qn[sl], qr[sl], k_cat[sl], vv[sl], o_ref, g * SEQ)


def _mla_lat_body(cq_ref, ckv_ref, kr_ref, cckv_ref, ckr_ref, c_ref, su_ref, sd_ref,
                  qg_ref, kg_ref, wq_ref, wkv_ref, o_ref, ckvn_ref):
    half = MLA_ROPE // 2
    c, su, sd = c_ref[...], su_ref[...], sd_ref[...]
    q = _dot(_rms_norm(cq_ref[...].astype(F32), qg_ref[...]), wq_ref[...])
    qn = q[:, 0:MLA_QN]
    qr = _rope128(q[:, MLA_QN:], c, su, sd, half)
    ckvn = _rms_norm(ckv_ref[...].astype(F32), kg_ref[...])
    ckvn_ref[...] = ckvn
    ckv_all = jnp.concatenate([ckvn, cckv_ref[...]], 0)
    kv = _dot(ckv_all, wkv_ref[...])
    vv = kv[:, MLA_QN:].astype(BF16)
    kr = jnp.concatenate([_rope128(kr_ref[...].astype(F32), c, su, sd, half), ckr_ref[...]], 0)
    k_cat = _mla_keys(kv[:, 0:MLA_QN], kr)
    rows_per_call = 256
    for n in range(DEC_SEQ // rows_per_call):
        rows = slice(n * rows_per_call, (n + 1) * rows_per_call)
        _mla_attend(qn[rows], qr[rows], k_cat, vv, o_ref, n * rows_per_call)


def _mla(z_main, cache_ckv, cache_kr_pad, weights, layer):
    c, su, sd, _ = _rope_tables(DEC_SEQ, MLA_ROPE, LANE)
    tab = _const_spec((DEC_SEQ, LANE))
    cache = pl.BlockSpec((None, None, PAST_LEN, LANE), lambda i: (_lat_index(i), layer, 0, 0))

    def weight(*shape):
        return pl.BlockSpec((None,) + shape, lambda i: (layer, 0, 0))

    return pl.pallas_call(
        _mla_kernel,
        out_shape=(jax.ShapeDtypeStruct((T_ALL, MLA_HEADS * MLA_V), BF16),
                   jax.ShapeDtypeStruct((T_ALL, MLA_KV_LORA), F32)),
        grid=(MIXER_STEPS,),
        in_specs=[
            pl.BlockSpec((STEP_ROWS, 256), lambda i: (i, COL_CQ // 256)),
            pl.BlockSpec((STEP_ROWS, LANE), lambda i: (i, COL_CKV // LANE)),
            pl.BlockSpec((STEP_ROWS, LANE), lambda i: (i, COL_KROPE // LANE)),
            cache, cache, tab, tab, tab,
            weight(1, MLA_Q_LORA), weight(1, MLA_KV_LORA),
            weight(MLA_Q_LORA, MLA_QN + MLA_HEADS * MLA_ROPE),
            weight(MLA_KV_LORA, MLA_QN + MLA_HEADS * MLA_V),
        ],
        out_specs=(pl.BlockSpec((STEP_ROWS, 256), lambda i: (i, 0)),
                   pl.BlockSpec((STEP_ROWS, MLA_KV_LORA), lambda i: (i, 0))),
        compiler_params=_params("parallel"),
        name="mla",
    )(z_main, z_main, z_main, cache_ckv, cache_kr_pad,
      jnp.asarray(c), jnp.asarray(su), jnp.asarray(sd), *weights)


def _route(logits_t, rb):
    scores = jax.nn.sigmoid(logits_t)
    biased = scores + rb
    sc = [scores[e:e + 1, :] for e in range(N_EXPERTS)]
    bi = [biased[e:e + 1, :] for e in range(N_EXPERTS)]
    epg = EXPERTS_PER_GROUP
    gsum = []
    for g in range(N_GROUPS):
        v = bi[g * epg:(g + 1) * epg]
        best = None
        for a in range(epg):
            for b in range(a + 1, epg):
                pair = v[a] + v[b]
                best = pair if best is None else jnp.maximum(best, pair)
        gsum.append(best)
    combine = []
    sel = []
    for g in range(N_GROUPS):
        is_best = None
        for g2 in range(N_GROUPS):
            if g2 == g:
                continue
            c = gsum[g] > gsum[g2] if g2 < g else gsum[g] >= gsum[g2]
            is_best = c if is_best is None else jnp.logical_and(is_best, c)
        for a in range(epg):
            e = g * epg + a
            rank = jnp.zeros_like(bi[e])
            for b in range(epg):
                if b == a:
                    continue
                e2 = g * epg + b
                ahead = bi[e2] >= bi[e] if b < a else bi[e2] > bi[e]
                rank = rank + jnp.where(ahead, 1.0, 0.0)
            sel.append(jnp.logical_and(is_best, rank < 2.0))
    wsum = None
    for e in range(N_EXPERTS):
        w = jnp.where(sel[e], sc[e], 0.0)
        wsum = w if wsum is None else wsum + w
    for e in range(N_EXPERTS):
        combine.append(jnp.where(sel[e], ROUTE_SCALE * sc[e] / wsum, 0.0))
    return jnp.concatenate(combine, 0)


MERGE_TILE = 512

MERGE_BRANCH_ROWS = (HY_W, WIN_HEADS * WIN_HD, RET_HEADS * RET_DV, MLA_HEADS * MLA_V)
MERGE_ROWS = sum(MERGE_BRANCH_ROWS) + D_MODEL


def _merge_kernel(ya_ref, yb_ref, yc_ref, yd_ref, gt_ref, xc_ref, xl_ref, m_ref,
                  w_ref, g_ref, b_ref, rw_ref, rb_ref,
                  x1_ref, h2_ref, cmb_ref, *, ctx_tiles, sub_rows):
    D = D_MODEL
    rw = rw_ref[...]
    rw_hi = rw.astype(BF16)
    rw_lo = (rw - rw_hi.astype(F32)).astype(BF16)
    g1 = m_ref[:, 2 * D:3 * D]
    s2 = m_ref[:, 3 * D:4 * D]
    sc2 = m_ref[:, 4 * D:5 * D]
    is_ctx = pl.program_id(0) < ctx_tiles
    offs = np.cumsum((0,) + MERGE_BRANCH_ROWS)
    branches = tuple((y_ref, slice(int(offs[i]), int(offs[i + 1])))
                     for i, y_ref in enumerate((ya_ref, yb_ref, yc_ref, yd_ref)))
    w_out_rows = slice(int(offs[-1]), MERGE_ROWS)
    for r0 in range(0, x1_ref.shape[0], sub_rows):
        rows = slice(r0, r0 + sub_rows)
        merged = None
        for i, (y_ref, w_rows) in enumerate(branches):
            t = gt_ref[rows, i * D:(i + 1) * D] * jnp.dot(
                y_ref[rows, :], w_ref[w_rows, :], preferred_element_type=F32).astype(BF16)
            merged = t if merged is None else merged + t
        out1 = jnp.dot(merged, w_ref[w_out_rows, :], preferred_element_type=F32)
        x = jnp.where(is_ctx, xc_ref[rows, :], xl_ref[rows, :])
        x1 = _layer_norm(ALPHA * x + g1 * out1) * g_ref[...] + b_ref[...]
        x1_ref[rows, :] = x1
        h2 = _layer_norm(x1) * (1.0 + sc2) + s2
        h2_hi = h2.astype(BF16)
        h2_ref[rows, :] = h2_hi
        h2_lo = (h2 - h2_hi.astype(F32)).astype(BF16)
        logits = (jnp.dot(h2_hi, rw_hi, preferred_element_type=F32)
                  + (jnp.dot(h2_lo, rw_hi, preferred_element_type=F32)
                     + jnp.dot(h2_hi, rw_lo, preferred_element_type=F32)))
        combine_t = _route(logits.T[0:N_EXPERTS], rb_ref[...])
        cmb_ref[rows, :] = jnp.concatenate(
            [combine_t, jnp.zeros((LANE - N_EXPERTS, sub_rows), F32)], 0).T


def _merge(ya, yb, yc, yd, gates, x_ctx, x_lat, lat_block0, mods, w_merge, ln1_g, ln1_b,
           router_w, router_b, layer):
    tm = MERGE_TILE
    row = _mod_row(tm)
    D = D_MODEL
    ctx_tiles = T_CTX // tm

    def tile(w):
        return pl.BlockSpec((tm, w), lambda i: (i, 0))

    def weight(k, n):
        return pl.BlockSpec((None, k, n), lambda i: (layer, 0, 0))

    return pl.pallas_call(
        functools.partial(_merge_kernel, ctx_tiles=ctx_tiles, sub_rows=256),
        out_shape=(jax.ShapeDtypeStruct((T_ALL, D), F32),
                   jax.ShapeDtypeStruct((T_ALL, D), BF16),
                   jax.ShapeDtypeStruct((T_ALL, LANE), F32)),
        grid=(T_ALL // tm,),
        in_specs=[
            tile(256), tile(256), tile(512), tile(256), tile(4 * D),
            *_x_specs(tm, ctx_tiles, lat_block0),
            _mod_spec(layer, row),
            weight(MERGE_ROWS, D), weight(1, D), weight(1, D),
            pl.BlockSpec((D, LANE), lambda i: (0, 0)),
            pl.BlockSpec((N_EXPERTS, 1), lambda i: (0, 0)),
        ],
        out_specs=(tile(D), tile(D), tile(LANE)),
        compiler_params=_params("parallel"),
        name="merge",
    )(ya, yb, yc, yd, gates, x_ctx, x_lat, mods, w_merge,
      ln1_g.reshape(DEPTH, 1, D), ln1_b.reshape(DEPTH, 1, D), router_w,
      router_b.reshape(N_EXPERTS, 1))


MOE_EXPERTS_PER_STEP = 2


def _moe_kernel(*refs, next_h):
    if next_h:
        (h_ref, c_ref, x1_ref, m_ref, wg_ref, wu_ref, wd_ref, g_ref, b_ref, mn_ref,
         o_ref, hn_ref, acc_ref) = refs
    else:
        h_ref, c_ref, x1_ref, m_ref, wg_ref, wu_ref, wd_ref, g_ref, b_ref, o_ref, acc_ref = refs
    eg = pl.program_id(1)

    @pl.when(eg == 0)
    def _():
        acc_ref[...] = jnp.zeros_like(acc_ref)

    h = h_ref[...]
    cmb = c_ref[...]
    lane = lax.broadcasted_iota(jnp.int32, cmb.shape, 1)
    hid = []
    for k in range(MOE_EXPERTS_PER_STEP):
        gate = jnp.dot(h, wg_ref[k].astype(BF16), preferred_element_type=F32)
        up = jnp.dot(h, wu_ref[k].astype(BF16), preferred_element_type=F32)
        e = eg * MOE_EXPERTS_PER_STEP + k
        ce = jnp.sum(jnp.where(lane == e, cmb, 0.0), -1, keepdims=True)
        sig = 0.5 * jnp.tanh(0.5 * gate) + 0.5
        hid.append((gate * sig * (up * ce)).astype(BF16))
    wd = wd_ref[...].reshape(MOE_EXPERTS_PER_STEP * D_EXPERT, D_MODEL).astype(BF16)
    acc_ref[...] += jnp.dot(jnp.concatenate(hid, 1), wd, preferred_element_type=F32)

    @pl.when(eg == N_EXPERTS // MOE_EXPERTS_PER_STEP - 1)
    def _():
        g2 = m_ref[:, 5 * D_MODEL:6 * D_MODEL]
        y = _layer_norm(ALPHA * x1_ref[...] + g2 * acc_ref[...])
        y = y * g_ref[...] + b_ref[...]
        o_ref[...] = y
        if next_h:
            s1 = mn_ref[:, 0:D_MODEL]
            sc1 = mn_ref[:, D_MODEL:2 * D_MODEL]
            hn_ref[...] = (_layer_norm(y) * (1.0 + sc1) + s1).astype(hn_ref.dtype)


def _moe(h2, combine, x1, mods, w_gate, w_up, w_down, ln2_g, ln2_b, layer, row0, n_rows,
         next_h=False):
    tm = 1024
    row = _mod_row(tm)
    D = D_MODEL
    t0 = row0 // tm
    eps = MOE_EXPERTS_PER_STEP
    mod_spec = _mod_spec(layer, lambda i, e: row(t0 + i))
    in_specs = [
        pl.BlockSpec((tm, D), lambda i, e: (t0 + i, 0)),
        pl.BlockSpec((tm, LANE), lambda i, e: (t0 + i, 0)),
        pl.BlockSpec((tm, D), lambda i, e: (t0 + i, 0)),
        mod_spec,
        pl.BlockSpec((None, eps, D, D_EXPERT), lambda i, e: (layer, e, 0, 0)),
        pl.BlockSpec((None, eps, D, D_EXPERT), lambda i, e: (layer, e, 0, 0)),
        pl.BlockSpec((None, eps, D_EXPERT, D), lambda i, e: (layer, e, 0, 0)),
        pl.BlockSpec((None, 1, D), lambda i, e: (layer, 0, 0)),
        pl.BlockSpec((None, 1, D), lambda i, e: (layer, 0, 0)),
    ]
    args = [h2, combine, x1, mods, w_gate, w_up, w_down,
            ln2_g.reshape(DEPTH, 1, D), ln2_b.reshape(DEPTH, 1, D)]
    out_shape = jax.ShapeDtypeStruct((n_rows, D), F32)
    out_specs = pl.BlockSpec((tm, D), lambda i, e: (i, 0))
    if next_h:
        in_specs.append(_mod_spec(layer + 1, lambda i, e: row(t0 + i)))
        args.append(mods)
        out_shape = (out_shape, jax.ShapeDtypeStruct((T_ALL, D), BF16))
        assert n_rows == T_ALL
        out_specs = (out_specs, pl.BlockSpec((tm, D), lambda i, e: (i, 0)))
    return pl.pallas_call(
        functools.partial(_moe_kernel, next_h=next_h),
        out_shape=out_shape,
        grid=(n_rows // tm, N_EXPERTS // eps),
        in_specs=in_specs,
        out_specs=out_specs,
        scratch_shapes=[pltpu.VMEM((tm, D), F32)],
        compiler_params=_params("parallel", "arbitrary"),
        name="moe",
    )(*args)


def kernel(x_prompt, x_sample, cache_win_k, cache_win_v, cache_mla_ckv, cache_mla_krope,
           state_ret_fwd, state_ret_bwd, c, c_ctx, w_ada, b_ada, w_in,
           hy_conv_w, hy_conv_b, hy_w1, hy_b1, hy_w2, hy_b2, hy_w3, hy_bias,
           win_sink, ret_decay_fwd, ret_decay_bwd, mla_q_norm, mla_kv_norm, mla_w_uq, mla_w_ukv,
           w_br_a, w_br_b, w_br_c, w_br_d, w_out, ln1_g, ln1_b, ln2_g, ln2_b,
           router_w, router_b, moe_w_gate, moe_w_up, moe_w_down):
    D = D_MODEL
    x_ctx = x_prompt.reshape(T_CTX, D)
    x_lat = x_sample.reshape(T_LAT, D)
    lat_block0 = 0

    mods = _ada_mods(c_ctx, c, w_ada, b_ada)

    w_in_t = jnp.swapaxes(w_in, 1, 2).reshape(DEPTH * IN_COLS, D)
    cache_k = cache_win_k.reshape(DEC_BATCH, DEPTH, PAST_LEN, WIN_KV_HEADS * WIN_HD)
    cache_v = cache_win_v.reshape(DEC_BATCH, DEPTH, PAST_LEN, WIN_KV_HEADS * WIN_HD)
    cache_kr = jnp.pad(cache_mla_krope, ((0, 0), (0, 0), (0, 0), (0, LANE - MLA_ROPE)))

    uq = mla_w_uq.reshape(DEPTH, MLA_Q_LORA, MLA_HEADS, MLA_NOPE + MLA_ROPE)
    ukv = mla_w_ukv.reshape(DEPTH, MLA_KV_LORA, MLA_HEADS, MLA_NOPE + MLA_V)
    mla_weights = (
        mla_q_norm.reshape(DEPTH, 1, MLA_Q_LORA),
        mla_kv_norm.reshape(DEPTH, 1, MLA_KV_LORA),
        jnp.concatenate([uq[..., :MLA_NOPE].reshape(DEPTH, MLA_Q_LORA, MLA_HEADS * MLA_NOPE),
                         uq[..., MLA_NOPE:].reshape(DEPTH, MLA_Q_LORA, MLA_HEADS * MLA_ROPE)], -1),
        jnp.concatenate([ukv[..., :MLA_NOPE].reshape(DEPTH, MLA_KV_LORA, MLA_HEADS * MLA_NOPE),
                         ukv[..., MLA_NOPE:].reshape(DEPTH, MLA_KV_LORA, MLA_HEADS * MLA_V)], -1),
    )

    hy_w1p = jnp.pad(hy_w1, ((0, 0), (0, LANE - HY_EMB), (0, 0)))
    dft = {}
    for L in (SEQ, DEC_SEQ):
        fwd, inv = _dft_tables(L)
        dft[L] = (jnp.asarray(fwd).astype(BF16), jnp.asarray(inv).astype(BF16))
    router_w_pad = jnp.pad(router_w, ((0, 0), (0, LANE - N_EXPERTS)))
    w_merge = jnp.concatenate([w_br_a, w_br_b, w_br_c, w_br_d, w_out], 1).astype(BF16)

    new_k, new_v, new_ckv, new_kr, new_sf, new_sb = [], [], [], [], [], []
    for l in range(DEPTH):
        if l == 0:
            h = _ln_mod(x_ctx, x_lat, mods, l)
        z = _in_proj(h, w_in_t, l, 0, Z_MAIN, Z_MAIN // 2, BF16, gate=False)
        gates = _in_proj(h, w_in_t, l, COL_GATE, 4 * D, D, BF16, gate=True)

        filters = {L: _hy_filters(L, hy_w1p[l], hy_b1[l][None], hy_w2[l], hy_b2[l][None], hy_w3[l],
                                  dft[L][0]) for L in (SEQ, DEC_SEQ)}
        ya = _hyena(z, l, hy_conv_w, hy_conv_b, hy_bias, filters, dft)
        yb = _win(z, win_sink, cache_k, cache_v, l)
        yc, sf, sb = _retention(z, ret_decay_fwd, ret_decay_bwd, state_ret_fwd, state_ret_bwd, l)
        yd, ckvn = _mla(z, cache_mla_ckv, cache_kr, mla_weights, l)

        x1, h2, combine = _merge(ya, yb, yc, yd, gates, x_ctx, x_lat, lat_block0, mods, w_merge,
                                   ln1_g, ln1_b, router_w_pad, router_b, l)
        moe_args = (h2, combine, x1, mods, moe_w_gate, moe_w_up, moe_w_down, ln2_g, ln2_b, l)
        if l + 1 < DEPTH:
            x_ctx, h = _moe(*moe_args, 0, T_ALL, next_h=True)
            x_lat, lat_block0 = x_ctx, T_CTX // MERGE_TILE
        else:
            x_ctx = _moe(*moe_args, 0, T_CTX)
            x_lat = _moe(*moe_args, T_CTX, T_LAT)

        def ctx_cols(col, width):
            return z[:T_CTX, col:col + width].astype(F32)

        new_k.append(ctx_cols(COL_WK, 128).reshape(BATCH, SEQ, WIN_KV_HEADS, WIN_HD))
        new_v.append(ctx_cols(COL_WV, 128).reshape(BATCH, SEQ, WIN_KV_HEADS, WIN_HD))
        new_ckv.append(ckvn[:T_CTX].reshape(BATCH, SEQ, MLA_KV_LORA))
        new_kr.append(ctx_cols(COL_KROPE, MLA_ROPE).reshape(BATCH, SEQ, MLA_ROPE))
        new_sf.append(sf[:BATCH])
        new_sb.append(sb[:BATCH])

    y_prompt = x_ctx.reshape(BATCH, SEQ, D)
    y_sample = x_lat.reshape(DEC_BATCH, DEC_SEQ, D)
    return (y_prompt, y_sample, jnp.stack(new_k, 1), jnp.stack(new_v, 1), jnp.stack(new_ckv, 1),
            jnp.stack(new_kr, 1), jnp.stack(new_sf, 1), jnp.stack(new_sb, 1))
```
